```python
import jax, jax.numpy as jnp
from jax import lax
import numpy as np

D_MODEL = 2048
BATCH = 8
SEQ = 8192
DEPTH = 1

MEM_LEN = 256
HGRN_WIDTH = D_MODEL // 2
HGRN_HEADS = 8
HGRN_KDIM = HGRN_WIDTH // HGRN_HEADS
HGRN_VDIM = HGRN_WIDTH // HGRN_HEADS
CONV_CH = D_MODEL - HGRN_WIDTH
CONV_GROUPS = 8
SHORT_CONV_K = 3
IN_COLS = 4 * HGRN_WIDTH + 3 * CONV_CH
CHUNK = 64
MEM_HEADS = 4
MEM_HEAD_DIM = D_MODEL // MEM_HEADS
D_FF = 5632
FFN_CONV_K = 3
EPS = 1e-6

kernel_name = "hgrn2_shortconv_hybrid_block"


def rmsnorm(x, w):
    xf = x.astype(jnp.float32)
    y = xf * lax.rsqrt(jnp.mean(xf * xf, axis=-1, keepdims=True) + EPS)
    return (y * w.astype(jnp.float32)).astype(x.dtype)


def causal_dwconv(x, w):
    k = w.shape[0]
    s = x.shape[1]
    xp = jnp.pad(x, ((0, 0), (k - 1, 0), (0, 0)))
    y = xp[:, 0:s] * w[0]
    for j in range(1, k):
        y = y + xp[:, j:j + s] * w[j]
    return y


def hgrn2_chunked(q, k, v, logf):
    bb, s, h, kd = q.shape
    vd = v.shape[-1]
    n = s // CHUNK

    def to_chunks(t):
        return t.reshape(bb, n, CHUNK, h, t.shape[-1]).transpose(1, 0, 3, 2, 4)

    qc, kc, vc, gc = to_chunks(q), to_chunks(k), to_chunks(v), to_chunks(logf)
    causal = jnp.tril(jnp.ones((CHUNK, CHUNK), dtype=bool))

    def step(state, inp):
        q_, k_, v_, g_ = inp
        b = jnp.cumsum(g_, axis=2)
        o_inter = jnp.einsum('bhtk,bhkv->bhtv', q_ * jnp.exp(b), state)
        diff = b[:, :, :, None, :] - b[:, :, None, :, :]
        decay = jnp.exp(jnp.where(causal[:, :, None], diff, -jnp.inf))
        scores = jnp.einsum('bhtk,bhtsk,bhsk->bhts', q_, decay, k_)
        o = o_inter + jnp.einsum('bhts,bhsv->bhtv', scores, v_)
        b_last = b[:, :, -1:, :]
        new_state = (jnp.exp(b_last[:, :, 0, :])[..., None] * state
                     + jnp.einsum('bhsk,bhsv->bhkv', k_ * jnp.exp(b_last - b), v_))
        return new_state, o

    s0 = jnp.zeros((bb, h, kd, vd), jnp.float32)
    _, o = lax.scan(step, s0, (qc, kc, vc, gc))
    return o.transpose(1, 0, 3, 2, 4).reshape(bb, s, h, vd)


def hybrid_mixer(h, w_in, lb, hgrn_norm_w, sconv_w, w_out):
    bb, s, _ = h.shape
    proj = h @ w_in
    W, C = HGRN_WIDTH, CONV_CH
    splits = [W, 2 * W, 3 * W, 4 * W, 4 * W + C, 4 * W + 2 * C]
    q, f_pre, i_in, g, cb, cc, ch = jnp.split(proj, splits, axis=-1)

    f = lb + (1.0 - lb) * jax.nn.sigmoid(f_pre.astype(jnp.float32))
    logf = jnp.log(f)
    k = 1.0 - f
    qf = jax.nn.silu(q.astype(jnp.float32))
    heads = lambda t, d: t.reshape(bb, s, HGRN_HEADS, d)
    o = hgrn2_chunked(heads(qf, HGRN_KDIM), heads(k, HGRN_KDIM),
                      heads(i_in.astype(jnp.float32), HGRN_VDIM), heads(logf, HGRN_KDIM))
    o = rmsnorm(o, hgrn_norm_w).reshape(bb, s, W).astype(h.dtype)
    o = o * jax.nn.silu(g)

    y = cb * causal_dwconv(cc * ch, sconv_w)

    return jnp.concatenate([o, y], axis=-1) @ w_out


def memory_cross_attention(h, mem_n, wq, wk, wv, wo):
    bb, s, _ = h.shape
    m = mem_n.shape[1]
    q = (h @ wq).reshape(bb, s, MEM_HEADS, MEM_HEAD_DIM)
    k = (mem_n @ wk).reshape(bb, m, MEM_HEADS, MEM_HEAD_DIM)
    v = (mem_n @ wv).reshape(bb, m, MEM_HEADS, MEM_HEAD_DIM)
    sc = jnp.einsum('bqhd,bmhd->bhqm', q, k).astype(jnp.float32) * (MEM_HEAD_DIM ** -0.5)
    p = jax.nn.softmax(sc, axis=-1).astype(v.dtype)
    o = jnp.einsum('bhqm,bmhd->bqhd', p, v).reshape(bb, s, D_MODEL)
    return o @ wo


def conv_ffn(h, w_gate, w_up, conv_w, conv_b, w_down):
    a = causal_dwconv(h @ w_gate, conv_w) + conv_b
    return (jax.nn.silu(a) * (h @ w_up)) @ w_down


def _fwd_setup_inputs(seed: int = 0) -> dict:
    key = jax.random.key(seed)
    ks = jax.random.split(key, 24)
    f32 = jnp.float32
    nrm = lambda k, shape, scale: jax.random.normal(k, shape, f32) * scale
    gain = lambda k, shape: 1.0 + 0.02 * jax.random.normal(k, shape, f32)
    L = DEPTH
    return {
        "x": nrm(ks[0], (BATCH, SEQ, D_MODEL), 1.0),
        "mem": nrm(ks[1], (BATCH, MEM_LEN, D_MODEL), 1.0),
        "hgrn_lb": nrm(ks[2], (DEPTH + 1, HGRN_WIDTH), 0.1),
        "norm1_w": gain(ks[3], (L, D_MODEL)),
        "w_in": nrm(ks[4], (L, D_MODEL, IN_COLS), D_MODEL ** -0.5),
        "hgrn_norm_w": gain(ks[5], (L, HGRN_VDIM)),
        "sconv_w": nrm(ks[6], (L, SHORT_CONV_K, CONV_CH), SHORT_CONV_K ** -0.5),
        "w_out": nrm(ks[7], (L, D_MODEL, D_MODEL), D_MODEL ** -0.5),
        "norm2_w": gain(ks[8], (L, D_MODEL)),
        "mem_norm_w": gain(ks[9], (L, D_MODEL)),
        "wq": nrm(ks[10], (L, D_MODEL, D_MODEL), D_MODEL ** -0.5),
        "wk": nrm(ks[11], (L, D_MODEL, D_MODEL), D_MODEL ** -0.5),
        "wv": nrm(ks[12], (L, D_MODEL, D_MODEL), D_MODEL ** -0.5),
        "wo": nrm(ks[13], (L, D_MODEL, D_MODEL), D_MODEL ** -0.5),
        "norm3_w": gain(ks[14], (L, D_MODEL)),
        "w_gate": nrm(ks[15], (L, D_MODEL, D_FF), D_MODEL ** -0.5),
        "w_up": nrm(ks[16], (L, D_MODEL, D_FF), D_MODEL ** -0.5),
        "ffn_conv_w": nrm(ks[17], (L, FFN_CONV_K, D_FF), FFN_CONV_K ** -0.5),
        "ffn_conv_b": nrm(ks[18], (L, D_FF), 0.02),
        "w_down": nrm(ks[19], (L, D_FF, D_MODEL), D_FF ** -0.5),
        "final_norm_w": gain(ks[20], (D_MODEL,)),
    }


def _fwd_reference(x, mem, hgrn_lb, norm1_w, w_in, hgrn_norm_w, sconv_w, w_out,
              norm2_w, mem_norm_w, wq, wk, wv, wo, norm3_w, w_gate, w_up,
              ffn_conv_w, ffn_conv_b, w_down, final_norm_w):
    lb_table = jnp.cumsum(jax.nn.softmax(hgrn_lb.astype(jnp.float32), axis=0), axis=0)
    for l in range(DEPTH):
        h = rmsnorm(x, norm1_w[l])
        x = x + hybrid_mixer(h, w_in[l], lb_table[l], hgrn_norm_w[l], sconv_w[l], w_out[l])
        h = rmsnorm(x, norm2_w[l])
        mem_n = rmsnorm(mem, mem_norm_w[l])
        x = x + memory_cross_attention(h, mem_n, wq[l], wk[l], wv[l], wo[l])
        h = rmsnorm(x, norm3_w[l])
        x = x + conv_ffn(h, w_gate[l], w_up[l], ffn_conv_w[l], ffn_conv_b[l], w_down[l])
    return rmsnorm(x, final_norm_w)


import jax as _jax
import jax.numpy as _jnp

TWIN_FORMAT = 'train_step'
FWD_PARAMS = ['x', 'mem', 'hgrn_lb', 'norm1_w', 'w_in', 'hgrn_norm_w', 'sconv_w', 'w_out', 'norm2_w', 'mem_norm_w', 'wq', 'wk', 'wv', 'wo', 'norm3_w', 'w_gate', 'w_up', 'ffn_conv_w', 'ffn_conv_b', 'w_down', 'final_norm_w']
TWIN_WEIGHTS = ['hgrn_lb', 'norm1_w', 'w_in', 'hgrn_norm_w', 'sconv_w', 'w_out', 'norm2_w', 'mem_norm_w', 'wq', 'wk', 'wv', 'wo', 'norm3_w', 'w_gate', 'w_up', 'ffn_conv_w', 'ffn_conv_b', 'w_down', 'final_norm_w']
TWIN_DIFF_INPUT = 'x'
TWIN_INPUTS = ['x', 'mem', 'hgrn_lb', 'norm1_w', 'w_in', 'hgrn_norm_w', 'sconv_w', 'w_out', 'norm2_w', 'mem_norm_w', 'wq', 'wk', 'wv', 'wo', 'norm3_w', 'w_gate', 'w_up', 'ffn_conv_w', 'ffn_conv_b', 'w_down', 'final_norm_w', 'loss_target', 'm_hgrn_lb', 'm_norm1_w', 'm_w_in', 'm_hgrn_norm_w', 'm_sconv_w', 'm_w_out', 'm_norm2_w', 'm_mem_norm_w', 'm_wq', 'm_wk', 'm_wv', 'm_wo', 'm_norm3_w', 'm_w_gate', 'm_w_up', 'm_ffn_conv_w', 'm_ffn_conv_b', 'm_w_down', 'm_final_norm_w', 'v_hgrn_lb', 'v_norm1_w', 'v_w_in', 'v_hgrn_norm_w', 'v_sconv_w', 'v_w_out', 'v_norm2_w', 'v_mem_norm_w', 'v_wq', 'v_wk', 'v_wv', 'v_wo', 'v_norm3_w', 'v_w_gate', 'v_w_up', 'v_ffn_conv_w', 'v_ffn_conv_b', 'v_w_down', 'v_final_norm_w']
TWIN_OUTPUTS = ['loss', 'grad_x', 'grad_hgrn_lb', 'grad_norm1_w', 'grad_w_in', 'grad_hgrn_norm_w', 'grad_sconv_w', 'grad_w_out', 'grad_norm2_w', 'grad_mem_norm_w', 'grad_wq', 'grad_wk', 'grad_wv', 'grad_wo', 'grad_norm3_w', 'grad_w_gate', 'grad_w_up', 'grad_ffn_conv_w', 'grad_ffn_conv_b', 'grad_w_down', 'grad_final_norm_w', 'delta_hgrn_lb', 'delta_norm1_w', 'delta_w_in', 'delta_hgrn_norm_w', 'delta_sconv_w', 'delta_w_out', 'delta_norm2_w', 'delta_mem_norm_w', 'delta_wq', 'delta_wk', 'delta_wv', 'delta_wo', 'delta_norm3_w', 'delta_w_gate', 'delta_w_up', 'delta_ffn_conv_w', 'delta_ffn_conv_b', 'delta_w_down', 'delta_final_norm_w', 'new_m_hgrn_lb', 'new_m_norm1_w', 'new_m_w_in', 'new_m_hgrn_norm_w', 'new_m_sconv_w', 'new_m_w_out', 'new_m_norm2_w', 'new_m_mem_norm_w', 'new_m_wq', 'new_m_wk', 'new_m_wv', 'new_m_wo', 'new_m_norm3_w', 'new_m_w_gate', 'new_m_w_up', 'new_m_ffn_conv_w', 'new_m_ffn_conv_b', 'new_m_w_down', 'new_m_final_norm_w', 'new_v_hgrn_lb', 'new_v_norm1_w', 'new_v_w_in', 'new_v_hgrn_norm_w', 'new_v_sconv_w', 'new_v_w_out', 'new_v_norm2_w', 'new_v_mem_norm_w', 'new_v_wq', 'new_v_wk', 'new_v_wv', 'new_v_wo', 'new_v_norm3_w', 'new_v_w_gate', 'new_v_w_up', 'new_v_ffn_conv_w', 'new_v_ffn_conv_b', 'new_v_w_down', 'new_v_final_norm_w']
TWIN_LEAF_KINDS = {'loss': 'loss', 'grad_x': 'grad_x', 'grad_hgrn_lb': 'grad_w', 'grad_norm1_w': 'grad_w', 'grad_w_in': 'grad_w', 'grad_hgrn_norm_w': 'grad_w', 'grad_sconv_w': 'grad_w', 'grad_w_out': 'grad_w', 'grad_norm2_w': 'grad_w', 'grad_mem_norm_w': 'grad_w', 'grad_wq': 'grad_w', 'grad_wk': 'grad_w', 'grad_wv': 'grad_w', 'grad_wo': 'grad_w', 'grad_norm3_w': 'grad_w', 'grad_w_gate': 'grad_w', 'grad_w_up': 'grad_w', 'grad_ffn_conv_w': 'grad_w', 'grad_ffn_conv_b': 'grad_w', 'grad_w_down': 'grad_w', 'grad_final_norm_w': 'grad_w', 'delta_hgrn_lb': 'delta_w', 'delta_norm1_w': 'delta_w', 'delta_w_in': 'delta_w', 'delta_hgrn_norm_w': 'delta_w', 'delta_sconv_w': 'delta_w', 'delta_w_out': 'delta_w', 'delta_norm2_w': 'delta_w', 'delta_mem_norm_w': 'delta_w', 'delta_wq': 'delta_w', 'delta_wk': 'delta_w', 'delta_wv': 'delta_w', 'delta_wo': 'delta_w', 'delta_norm3_w': 'delta_w', 'delta_w_gate': 'delta_w', 'delta_w_up': 'delta_w', 'delta_ffn_conv_w': 'delta_w', 'delta_ffn_conv_b': 'delta_w', 'delta_w_down': 'delta_w', 'delta_final_norm_w': 'delta_w', 'new_m_hgrn_lb': 'new_m', 'new_m_norm1_w': 'new_m', 'new_m_w_in': 'new_m', 'new_m_hgrn_norm_w': 'new_m', 'new_m_sconv_w': 'new_m', 'new_m_w_out': 'new_m', 'new_m_norm2_w': 'new_m', 'new_m_mem_norm_w': 'new_m', 'new_m_wq': 'new_m', 'new_m_wk': 'new_m', 'new_m_wv': 'new_m', 'new_m_wo': 'new_m', 'new_m_norm3_w': 'new_m', 'new_m_w_gate': 'new_m', 'new_m_w_up': 'new_m', 'new_m_ffn_conv_w': 'new_m', 'new_m_ffn_conv_b': 'new_m', 'new_m_w_down': 'new_m', 'new_m_final_norm_w': 'new_m', 'new_v_hgrn_lb': 'new_v', 'new_v_norm1_w': 'new_v', 'new_v_w_in': 'new_v', 'new_v_hgrn_norm_w': 'new_v', 'new_v_sconv_w': 'new_v', 'new_v_w_out': 'new_v', 'new_v_norm2_w': 'new_v', 'new_v_mem_norm_w': 'new_v', 'new_v_wq': 'new_v', 'new_v_wk': 'new_v', 'new_v_wv': 'new_v', 'new_v_wo': 'new_v', 'new_v_norm3_w': 'new_v', 'new_v_w_gate': 'new_v', 'new_v_w_up': 'new_v', 'new_v_ffn_conv_w': 'new_v', 'new_v_ffn_conv_b': 'new_v', 'new_v_w_down': 'new_v', 'new_v_final_norm_w': 'new_v'}


def _forward(args):
    return _fwd_reference(*[args[k] for k in FWD_PARAMS])


def _output_shape():
    def fwd():
        inp = _fwd_setup_inputs(0)
        return _fwd_reference(*[inp[k] for k in FWD_PARAMS])
    out = _jax.eval_shape(fwd)
    return out.shape, out.dtype

N_MICROBATCH = 1
ADAM_LR = 0.001
ADAM_B1 = 0.9
ADAM_B2 = 0.999
ADAM_EPS = 1e-08
ADAM_WD = 0.01
ADAM_STEP = 10
PER_EXAMPLE_BATCH_AXIS = {'x': 0, 'mem': 0, 'loss_target': 0}
SHARED_INPUTS = []
_WEIGHT_DTYPES = {'hgrn_lb': _jnp.float32, 'norm1_w': _jnp.float32, 'w_in': _jnp.float32, 'hgrn_norm_w': _jnp.float32, 'sconv_w': _jnp.float32, 'w_out': _jnp.float32, 'norm2_w': _jnp.float32, 'mem_norm_w': _jnp.float32, 'wq': _jnp.float32, 'wk': _jnp.float32, 'wv': _jnp.float32, 'wo': _jnp.float32, 'norm3_w': _jnp.float32, 'w_gate': _jnp.float32, 'w_up': _jnp.float32, 'ffn_conv_w': _jnp.float32, 'ffn_conv_b': _jnp.float32, 'w_down': _jnp.float32, 'final_norm_w': _jnp.float32}
MOMENT_SCALE = {'hgrn_lb': 5.969280e-03, 'norm1_w': 1.468258e-01, 'w_in': 7.708765e-02, 'hgrn_norm_w': 2.021445e-01, 'sconv_w': 1.093773e-01, 'w_out': 8.694901e-02, 'norm2_w': 1.085461e-02, 'mem_norm_w': 1.637216e-02, 'wq': 1.078485e-02, 'wk': 1.078590e-02, 'wv': 1.101811e-02, 'wo': 1.105272e-02, 'norm3_w': 7.736487e-02, 'w_gate': 3.313459e-02, 'w_up': 3.217536e-02, 'ffn_conv_w': 3.316979e-02, 'ffn_conv_b': 3.147014e-02, 'w_down': 5.338289e-02, 'final_norm_w': 3.198119e+01}


def _to_microbatches(a, axis):
    t = _jnp.moveaxis(a, axis, 0)
    t = t.reshape((N_MICROBATCH, t.shape[0] // N_MICROBATCH) + t.shape[1:])
    return _jnp.moveaxis(t, 1, axis + 1)


def setup_inputs(seed: int = 0) -> dict:
    inp = _fwd_setup_inputs(seed)
    key = _jax.random.fold_in(_jax.random.key(seed), 7919)
    shape, _ = _output_shape()
    out = dict(inp)
    out["loss_target"] = _jax.random.normal(_jax.random.fold_in(key, 0), shape, _jnp.float32)
    for i, name in enumerate(TWIN_WEIGHTS):
        w = inp[name].astype(_jnp.float32)
        if MOMENT_SCALE is None:
            s = _jnp.sqrt(_jnp.mean(_jnp.square(w)) + 1e-30)
        else:
            s = MOMENT_SCALE[name]
        km, kv = _jax.random.split(_jax.random.fold_in(key, i + 1))
        out[name] = w
        out["m_" + name] = s * _jax.random.normal(km, w.shape, _jnp.float32)
        out["v_" + name] = (s * s) * _jax.random.uniform(kv, w.shape, _jnp.float32, 0.5, 1.5)
    if N_MICROBATCH > 1:
        for name, axis in PER_EXAMPLE_BATCH_AXIS.items():
            out[name] = _to_microbatches(out[name], axis)
    return {'x': out['x'], 'mem': out['mem'], 'hgrn_lb': out['hgrn_lb'], 'norm1_w': out['norm1_w'], 'w_in': out['w_in'], 'hgrn_norm_w': out['hgrn_norm_w'], 'sconv_w': out['sconv_w'], 'w_out': out['w_out'], 'norm2_w': out['norm2_w'], 'mem_norm_w': out['mem_norm_w'], 'wq': out['wq'], 'wk': out['wk'], 'wv': out['wv'], 'wo': out['wo'], 'norm3_w': out['norm3_w'], 'w_gate': out['w_gate'], 'w_up': out['w_up'], 'ffn_conv_w': out['ffn_conv_w'], 'ffn_conv_b': out['ffn_conv_b'], 'w_down': out['w_down'], 'final_norm_w': out['final_norm_w'], 'loss_target': out['loss_target'], 'm_hgrn_lb': out['m_hgrn_lb'], 'm_norm1_w': out['m_norm1_w'], 'm_w_in': out['m_w_in'], 'm_hgrn_norm_w': out['m_hgrn_norm_w'], 'm_sconv_w': out['m_sconv_w'], 'm_w_out': out['m_w_out'], 'm_norm2_w': out['m_norm2_w'], 'm_mem_norm_w': out['m_mem_norm_w'], 'm_wq': out['m_wq'], 'm_wk': out['m_wk'], 'm_wv': out['m_wv'], 'm_wo': out['m_wo'], 'm_norm3_w': out['m_norm3_w'], 'm_w_gate': out['m_w_gate'], 'm_w_up': out['m_w_up'], 'm_ffn_conv_w': out['m_ffn_conv_w'], 'm_ffn_conv_b': out['m_ffn_conv_b'], 'm_w_down': out['m_w_down'], 'm_final_norm_w': out['m_final_norm_w'], 'v_hgrn_lb': out['v_hgrn_lb'], 'v_norm1_w': out['v_norm1_w'], 'v_w_in': out['v_w_in'], 'v_hgrn_norm_w': out['v_hgrn_norm_w'], 'v_sconv_w': out['v_sconv_w'], 'v_w_out': out['v_w_out'], 'v_norm2_w': out['v_norm2_w'], 'v_mem_norm_w': out['v_mem_norm_w'], 'v_wq': out['v_wq'], 'v_wk': out['v_wk'], 'v_wv': out['v_wv'], 'v_wo': out['v_wo'], 'v_norm3_w': out['v_norm3_w'], 'v_w_gate': out['v_w_gate'], 'v_w_up': out['v_w_up'], 'v_ffn_conv_w': out['v_ffn_conv_w'], 'v_ffn_conv_b': out['v_ffn_conv_b'], 'v_w_down': out['v_w_down'], 'v_final_norm_w': out['v_final_norm_w']}


def _loss(weights, diff, rest, loss_target):
    with _jax.named_scope("forward"):
        args = {**rest, TWIN_DIFF_INPUT: diff, **{k: w.astype(_WEIGHT_DTYPES[k]) for k, w in weights.items()}}
        y = _forward(args)
    with _jax.named_scope("loss_head"):
        err = _jnp.square(y.astype(_jnp.float32) - loss_target)
        return 0.5 * _jnp.sum(_jnp.mean(err, axis=-1)) if err.ndim else 0.5 * err


def _adamw(w, g, m, v):
    m = ADAM_B1 * m + (1.0 - ADAM_B1) * g
    v = ADAM_B2 * v + (1.0 - ADAM_B2) * _jnp.square(g)
    m_hat = m / (1.0 - ADAM_B1 ** ADAM_STEP)
    v_hat = v / (1.0 - ADAM_B2 ** ADAM_STEP)
    delta = -ADAM_LR * (m_hat / (_jnp.sqrt(v_hat) + ADAM_EPS) + ADAM_WD * w)
    return delta, m, v


def reference(x, mem, hgrn_lb, norm1_w, w_in, hgrn_norm_w, sconv_w, w_out, norm2_w, mem_norm_w, wq, wk, wv, wo, norm3_w, w_gate, w_up, ffn_conv_w, ffn_conv_b, w_down, final_norm_w, loss_target, m_hgrn_lb, m_norm1_w, m_w_in, m_hgrn_norm_w, m_sconv_w, m_w_out, m_norm2_w, m_mem_norm_w, m_wq, m_wk, m_wv, m_wo, m_norm3_w, m_w_gate, m_w_up, m_ffn_conv_w, m_ffn_conv_b, m_w_down, m_final_norm_w, v_hgrn_lb, v_norm1_w, v_w_in, v_hgrn_norm_w, v_sconv_w, v_w_out, v_norm2_w, v_mem_norm_w, v_wq, v_wk, v_wv, v_wo, v_norm3_w, v_w_gate, v_w_up, v_ffn_conv_w, v_ffn_conv_b, v_w_down, v_final_norm_w):
    given = dict(x=x, mem=mem, hgrn_lb=hgrn_lb, norm1_w=norm1_w, w_in=w_in, hgrn_norm_w=hgrn_norm_w, sconv_w=sconv_w, w_out=w_out, norm2_w=norm2_w, mem_norm_w=mem_norm_w, wq=wq, wk=wk, wv=wv, wo=wo, norm3_w=norm3_w, w_gate=w_gate, w_up=w_up, ffn_conv_w=ffn_conv_w, ffn_conv_b=ffn_conv_b, w_down=w_down, final_norm_w=final_norm_w, loss_target=loss_target, m_hgrn_lb=m_hgrn_lb, m_norm1_w=m_norm1_w, m_w_in=m_w_in, m_hgrn_norm_w=m_hgrn_norm_w, m_sconv_w=m_sconv_w, m_w_out=m_w_out, m_norm2_w=m_norm2_w, m_mem_norm_w=m_mem_norm_w, m_wq=m_wq, m_wk=m_wk, m_wv=m_wv, m_wo=m_wo, m_norm3_w=m_norm3_w, m_w_gate=m_w_gate, m_w_up=m_w_up, m_ffn_conv_w=m_ffn_conv_w, m_ffn_conv_b=m_ffn_conv_b, m_w_down=m_w_down, m_final_norm_w=m_final_norm_w, v_hgrn_lb=v_hgrn_lb, v_norm1_w=v_norm1_w, v_w_in=v_w_in, v_hgrn_norm_w=v_hgrn_norm_w, v_sconv_w=v_sconv_w, v_w_out=v_w_out, v_norm2_w=v_norm2_w, v_mem_norm_w=v_mem_norm_w, v_wq=v_wq, v_wk=v_wk, v_wv=v_wv, v_wo=v_wo, v_norm3_w=v_norm3_w, v_w_gate=v_w_gate, v_w_up=v_w_up, v_ffn_conv_w=v_ffn_conv_w, v_ffn_conv_b=v_ffn_conv_b, v_w_down=v_w_down, v_final_norm_w=v_final_norm_w)
    weights = {n: given[n] for n in TWIN_WEIGHTS}
    shared = {n: given[n] for n in SHARED_INPUTS}
    per_example = {n: given[n] for n in ['x', 'mem']}
    grad_fn = _jax.value_and_grad(_loss, argnums=(0, 1))

    def one_microbatch(ex, loss_target):
        ex = dict(ex)
        diff = ex.pop(TWIN_DIFF_INPUT)
        return grad_fn(weights, diff, {**shared, **ex}, loss_target)

    if N_MICROBATCH == 1:
        loss, (grad_w, grad_x) = one_microbatch(per_example, given["loss_target"])
    else:
        def body(carry, xs):
            loss_sum, grad_sum = carry
            l_k, (gw_k, gx_k) = one_microbatch(xs[0], xs[1])
            with _jax.named_scope("update"):
                return (loss_sum + l_k, _jax.tree.map(_jnp.add, grad_sum, gw_k)), gx_k

        init = (_jnp.zeros((), _jnp.float32), _jax.tree.map(_jnp.zeros_like, weights))
        (loss, grad_w), grad_x = _jax.lax.scan(body, init, (per_example, given["loss_target"]))
    with _jax.named_scope("update"):
        delta_w, new_m, new_v = {}, {}, {}
        for n in TWIN_WEIGHTS:
            delta_w[n], new_m[n], new_v[n] = _adamw(weights[n], grad_w[n], given["m_" + n], given["v_" + n])
    return (loss, grad_x, *[grad_w[n] for n in TWIN_WEIGHTS], *[delta_w[n] for n in TWIN_WEIGHTS],
            *[new_m[n] for n in TWIN_WEIGHTS], *[new_v[n] for n in TWIN_WEIGHTS])
```

```python
import functools

import jax
import jax.numpy as jnp
from jax import lax
from jax.experimental import pallas as pl
from jax.experimental.pallas import tpu as pltpu

F32 = jnp.float32
BF16 = jnp.bfloat16
MESH = pl.DeviceIdType.MESH

EPS = 1e-6
HGRN_W = 1024
HEAD = 128
N_HEADS = 8
CHUNK = 64
MEM_HEADS = 4
MEM_HEAD_DIM = 512
N_CHIPS = 4
HALO = 8

ADAM_LR = 0.001
ADAM_B1 = 0.9
ADAM_B2 = 0.999
ADAM_EPS = 1e-08
ADAM_WD = 0.01
ADAM_STEP = 10


def _sigmoid(x):
    return 1.0 / (1.0 + jnp.exp(-x))


def _dot(a, b, dims):
    return lax.dot_general(a.astype(BF16), b.astype(BF16), (dims, ((), ())),
                           preferred_element_type=F32)


def _dot_nn(a, b):
    return _dot(a, b, ((1,), (0,)))


def _dot_nt(a, b):
    return _dot(a, b, ((1,), (1,)))


def _dot_tn(a, b):
    return _dot(a, b, ((0,), (0,)))


def _hdot(a, b, dims):
    return lax.dot_general(a, b, (dims, ((), ())), precision=lax.Precision.HIGHEST, preferred_element_type=F32)


def _hdot_nn(a, b):
    return _hdot(a, b, ((1,), (0,)))


def _hdot_nt(a, b):
    return _hdot(a, b, ((1,), (1,)))


def _hdot_tn(a, b):
    return _hdot(a, b, ((0,), (0,)))


def _exact_ones_dot(ones_bf16, x):
    hi = x.astype(BF16)
    r1 = x - hi.astype(F32)
    mid = r1.astype(BF16)
    lo = (r1 - mid.astype(F32)).astype(BF16)
    dims = (((1,), (0,)), ((), ()))
    return (lax.dot_general(ones_bf16, hi, dims, preferred_element_type=F32)
            + lax.dot_general(ones_bf16, mid, dims, preferred_element_type=F32)
            + lax.dot_general(ones_bf16, lo, dims, preferred_element_type=F32))


def _rows8(v):
    t, c = v.shape
    return v.reshape(t // 8, 8, c).sum(axis=0)


def _shift_down(x, halo, s):
    rolled = pltpu.roll(x, s, 0)
    hrolled = pltpu.roll(halo, s, 0)
    row = lax.broadcasted_iota(jnp.int32, hrolled.shape, 0)
    head = jnp.where(row < s, hrolled, rolled[:HALO])
    return jnp.concatenate([head, rolled[HALO:]], axis=0)


def _shift_up(x, halo, s):
    t = x.shape[0]
    rolled = pltpu.roll(x, t - s, 0)
    hrolled = pltpu.roll(halo, HALO - s, 0)
    row = lax.broadcasted_iota(jnp.int32, hrolled.shape, 0)
    tail = jnp.where(row >= HALO - s, hrolled, rolled[t - HALO:])
    return jnp.concatenate([rolled[:t - HALO], tail], axis=0)


def _params(*sem):
    return pltpu.CompilerParams(dimension_semantics=sem)


def _row_tile(r, pref):
    while r % pref:
        pref //= 2
    return pref


def _cast_bf16(x, name):
    r, c = x.shape
    tr = _row_tile(r, 256)

    def body(x_ref, o_ref):
        o_ref[...] = x_ref[...].astype(BF16)

    return pl.pallas_call(
        body, name=name, grid=(r // tr,),
        in_specs=[pl.BlockSpec((tr, c), lambda i: (i, 0))],
        out_specs=pl.BlockSpec((tr, c), lambda i: (i, 0)),
        out_shape=jax.ShapeDtypeStruct((r, c), BF16),
        compiler_params=_params("parallel"),
    )(x)


def _rmsnorm_fwd(x, w, name, tm=256):
    s, d = x.shape
    tm = min(tm, s)

    def body(x_ref, w_ref, o_ref):
        xv = x_ref[...]
        r = lax.rsqrt(jnp.mean(xv * xv, axis=-1, keepdims=True) + EPS)
        o_ref[...] = ((xv * r) * w_ref[...]).astype(BF16)

    return pl.pallas_call(
        body, name=name, grid=(s // tm,),
        in_specs=[pl.BlockSpec((tm, d), lambda i: (i, 0)), pl.BlockSpec((1, d), lambda i: (0, 0))],
        out_specs=pl.BlockSpec((tm, d), lambda i: (i, 0)),
        out_shape=jax.ShapeDtypeStruct((s, d), BF16),
        compiler_params=_params("parallel"),
    )(x, w)


def _rmsnorm_bwd(dh, x, w, dres, name, tm=256):
    s, d = x.shape
    tm = min(tm, s)
    has_res = dres is not None

    def body(*refs):
        if has_res:
            dh_ref, x_ref, w_ref, dres_ref, dx_ref, gw_ref = refs
        else:
            dh_ref, x_ref, w_ref, dx_ref, gw_ref = refs

        @pl.when(pl.program_id(0) == 0)
        def _():
            gw_ref[...] = jnp.zeros_like(gw_ref)

        xv = x_ref[...]
        dhv = dh_ref[...].astype(F32)
        r = lax.rsqrt(jnp.mean(xv * xv, axis=-1, keepdims=True) + EPS)
        xhat = xv * r
        gw_ref[...] += _rows8(dhv * xhat)
        dxh = dhv * w_ref[...]
        dx = r * (dxh - xhat * jnp.mean(dxh * xhat, axis=-1, keepdims=True))
        if has_res:
            dx = dres_ref[...] + dx
        dx_ref[...] = dx

    row = pl.BlockSpec((tm, d), lambda i: (i, 0))
    in_specs = [row, row, pl.BlockSpec((1, d), lambda i: (0, 0))] + ([row] if has_res else [])
    args = (dh, x, w) + ((dres,) if has_res else ())
    return pl.pallas_call(
        body, name=name, grid=(s // tm,),
        in_specs=in_specs,
        out_specs=[row, pl.BlockSpec((8, d), lambda i: (0, 0))],
        out_shape=[jax.ShapeDtypeStruct((s, d), F32), jax.ShapeDtypeStruct((8, d), F32)],
        compiler_params=_params("arbitrary"),
    )(*args)


def _final_loss_bwd(x3, target, w, name, tm=256):
    s, d = x3.shape
    tm = min(tm, s)

    def body(x_ref, t_ref, w_ref, dx_ref, gw_ref, loss_ref):
        @pl.when(pl.program_id(0) == 0)
        def _():
            gw_ref[...] = jnp.zeros_like(gw_ref)
            loss_ref[...] = jnp.zeros_like(loss_ref)

        xv = x_ref[...]
        r = lax.rsqrt(jnp.mean(xv * xv, axis=-1, keepdims=True) + EPS)
        xhat = xv * r
        y = xhat * w_ref[...]
        err = y - t_ref[...]
        part = 0.5 * jnp.mean(err * err, axis=-1, keepdims=True)
        tot = jnp.sum(part, axis=0, keepdims=True)
        rr = lax.broadcasted_iota(jnp.int32, loss_ref.shape, 0)
        cc = lax.broadcasted_iota(jnp.int32, loss_ref.shape, 1)
        loss_ref[...] += jnp.where((rr == 0) & (cc == 0), tot, 0.0)
        dy = err * (1.0 / d)
        gw_ref[...] += _rows8(dy * xhat)
        dxh = dy * w_ref[...]
        dx_ref[...] = r * (dxh - xhat * jnp.mean(dxh * xhat, axis=-1, keepdims=True))

    row = pl.BlockSpec((tm, d), lambda i: (i, 0))
    return pl.pallas_call(
        body, name=name, grid=(s // tm,),
        in_specs=[row, row, pl.BlockSpec((1, d), lambda i: (0, 0))],
        out_specs=[row, pl.BlockSpec((8, d), lambda i: (0, 0)), pl.BlockSpec((8, 128), lambda i: (0, 0))],
        out_shape=[jax.ShapeDtypeStruct((s, d), F32), jax.ShapeDtypeStruct((8, d), F32),
                   jax.ShapeDtypeStruct((8, 128), F32)],
        compiler_params=_params("arbitrary"),
    )(x3, target, w)


def _matmul(a, b, mode, name, *, out_dtype=F32, residual=None, extra_bf16=False, tm=512, tn=512, tk=2048):
    if mode == "nn":
        (m, k), (k2, n) = a.shape, b.shape
    elif mode == "nt":
        (m, k), (n, k2) = a.shape, b.shape
    else:
        (k, m), (k2, n) = a.shape, b.shape
    assert k == k2, (a.shape, b.shape, mode)
    tm, tn, tk = min(tm, m), min(tn, n), min(tk, k)
    assert m % tm == 0 and n % tn == 0 and k % tk == 0, (m, n, k, tm, tn, tk)
    nk = k // tk
    dims = {"nn": ((1,), (0,)), "nt": ((1,), (1,)), "tn": ((0,), (0,))}[mode]
    has_res = residual is not None

    def body(*refs):
        refs = list(refs)
        a_ref, b_ref = refs[0], refs[1]
        r_ref = refs[2] if has_res else None
        outs = refs[2 + has_res:]
        o_ref = outs[0]
        o2_ref = outs[1] if extra_bf16 else None
        acc = outs[-1]
        kk = pl.program_id(2)

        @pl.when(kk == 0)
        def _():
            acc[...] = jnp.zeros_like(acc)

        acc[...] += _dot(a_ref[...], b_ref[...], dims)

        @pl.when(kk == nk - 1)
        def _():
            r = acc[...]
            if has_res:
                r = r_ref[...] + r
            o_ref[...] = r.astype(out_dtype)
            if extra_bf16:
                o2_ref[...] = r.astype(BF16)

    if mode == "tn":
        a_spec = pl.BlockSpec((tk, tm), lambda i, j, kk: (kk, i))
    else:
        a_spec = pl.BlockSpec((tm, tk), lambda i, j, kk: (i, kk))
    if mode == "nt":
        b_spec = pl.BlockSpec((tn, tk), lambda i, j, kk: (j, kk))
    else:
        b_spec = pl.BlockSpec((tk, tn), lambda i, j, kk: (kk, j))
    o_spec = pl.BlockSpec((tm, tn), lambda i, j, kk: (i, j))
    in_specs = [a_spec, b_spec] + ([o_spec] if has_res else [])
    out_specs = [o_spec] + ([o_spec] if extra_bf16 else [])
    out_shape = [jax.ShapeDtypeStruct((m, n), out_dtype)] + ([jax.ShapeDtypeStruct((m, n), BF16)] if extra_bf16 else [])
    args = (a, b) + ((residual,) if has_res else ())
    res = pl.pallas_call(
        body, name=name, grid=(m // tm, n // tn, nk),
        in_specs=in_specs, out_specs=out_specs, out_shape=out_shape,
        scratch_shapes=[pltpu.VMEM((tm, tn), F32)],
        compiler_params=_params("parallel", "parallel", "arbitrary"),
    )(*args)
    return res if extra_bf16 else res[0]


def _hgrn_gates(qp, fp, lb):
    sig = _sigmoid(fp)
    f = lb + (1.0 - lb) * sig
    logf = jnp.log(f)
    k = 1.0 - f
    sq = _sigmoid(qp)
    q = qp * sq
    return sig, f, logf, k, sq, q


def _hgrn_fwd(proj, lb0, lb1, norm_w, name, tb=512):
    s = proj.shape[0]
    tb = min(tb, s)
    nb, ncb = s // tb, tb // CHUNK

    def body(q_ref, f_ref, i_ref, g_ref, a0_ref, a1_ref, nw_ref, o_ref, og_ref, st_ref, state):
        @pl.when(pl.program_id(1) == 0)
        def _():
            state[...] = jnp.zeros_like(state)

        lb = _sigmoid(a0_ref[...] - a1_ref[...])
        row = lax.broadcasted_iota(jnp.int32, (CHUNK, CHUNK), 0)
        col = lax.broadcasted_iota(jnp.int32, (CHUNK, CHUNK), 1)
        tril = row >= col
        ones_l = tril.astype(BF16)
        nw = nw_ref[...]

        def chunk(c, carry):
            rows = pl.ds(pl.multiple_of(c * CHUNK, CHUNK), CHUNK)
            v = i_ref[rows, :]
            _, _, logf, k, _, q = _hgrn_gates(q_ref[rows, :], f_ref[rows, :], lb)
            b = _exact_ones_dot(ones_l, logf)
            bl = jnp.sum(logf, axis=0, keepdims=True)
            bm = 0.5 * bl
            st = state[...]
            st_ref[0, c] = st
            qt = q * jnp.exp(b - bm)
            kt = k * jnp.exp(bm - b)
            a = jnp.where(tril, _hdot_nt(qt, kt), 0.0)
            o = _hdot_nt(q * jnp.exp(b), st) + _hdot_nn(a, v)
            state[...] = st * jnp.exp(bl) + _hdot_tn(v, k * jnp.exp(bl - b))
            o_ref[rows, :] = o
            on = (o * lax.rsqrt(jnp.mean(o * o, axis=-1, keepdims=True) + EPS)) * nw
            gv = g_ref[rows, :]
            og_ref[rows, :] = (on * (gv * _sigmoid(gv))).astype(BF16)
            return carry

        lax.fori_loop(0, ncb, chunk, 0)

    def colblk(group):
        return pl.BlockSpec((tb, HEAD), lambda h, j: (j, group * N_HEADS + h))

    vec = pl.BlockSpec((1, HEAD), lambda h, j: (0, h))
    out_blk = pl.BlockSpec((tb, HEAD), lambda h, j: (j, h))
    return pl.pallas_call(
        body, name=name, grid=(N_HEADS, nb),
        in_specs=[colblk(0), colblk(1), colblk(2), colblk(3), vec, vec, pl.BlockSpec((1, HEAD), lambda h, j: (0, 0))],
        out_specs=[out_blk, out_blk, pl.BlockSpec((1, ncb, HEAD, HEAD), lambda h, j: (h, j, 0, 0))],
        out_shape=[jax.ShapeDtypeStruct((s, HGRN_W), F32), jax.ShapeDtypeStruct((s, HGRN_W), BF16),
                   jax.ShapeDtypeStruct((N_HEADS, s // CHUNK, HEAD, HEAD), F32)],
        scratch_shapes=[pltpu.VMEM((HEAD, HEAD), F32)],
        compiler_params=_params("parallel", "arbitrary"),
    )(proj, proj, proj, proj, lb0, lb1, norm_w)


def _hgrn_bwd(proj, lb0, lb1, norm_w, o, states, dmix, name, tb=512):
    s = proj.shape[0]
    tb = min(tb, s)
    nb, ncb = s // tb, tb // CHUNK

    def body(q_ref, f_ref, i_ref, g_ref, a0_ref, a1_ref, nw_ref, o_ref, st_ref, dm_ref,
             dq_ref, df_ref, di_ref, dg_ref, glb_ref, gnw_ref, dstate):
        h = pl.program_id(0)

        @pl.when(pl.program_id(1) == 0)
        def _():
            dstate[...] = jnp.zeros_like(dstate)
            glb_ref[...] = jnp.zeros_like(glb_ref)

        @pl.when((pl.program_id(1) == 0) & (h == 0))
        def _():
            gnw_ref[...] = jnp.zeros_like(gnw_ref)

        lb = _sigmoid(a0_ref[...] - a1_ref[...])
        row = lax.broadcasted_iota(jnp.int32, (CHUNK, CHUNK), 0)
        col = lax.broadcasted_iota(jnp.int32, (CHUNK, CHUNK), 1)
        tril = row >= col
        ones_l = tril.astype(BF16)
        ones_u = (row <= col).astype(BF16)
        nw = nw_ref[...]

        def chunk(cc, carry):
            c = ncb - 1 - cc
            rows = pl.ds(pl.multiple_of(c * CHUNK, CHUNK), CHUNK)
            qp = q_ref[rows, :]
            v = i_ref[rows, :]
            sig, f, logf, k, sq, q = _hgrn_gates(qp, f_ref[rows, :], lb)
            gv = g_ref[rows, :]
            sg = _sigmoid(gv)
            silu_g = gv * sg
            dog = dm_ref[rows, :]
            ov = o_ref[rows, :]
            r = lax.rsqrt(jnp.mean(ov * ov, axis=-1, keepdims=True) + EPS)
            ohat = ov * r
            on = ohat * nw
            dg_ref[rows, :] = (dog * on * (sg * (1.0 + gv * (1.0 - sg)))).astype(BF16)
            don = dog * silu_g
            gnw_ref[...] += _rows8(don * ohat)
            doh = don * nw
            do = r * (doh - ohat * jnp.mean(doh * ohat, axis=-1, keepdims=True))
            b = _exact_ones_dot(ones_l, logf)
            bl = jnp.sum(logf, axis=0, keepdims=True)
            bm = 0.5 * bl
            e_q = jnp.exp(b - bm)
            e_k = jnp.exp(bm - b)
            e_b = jnp.exp(b)
            e_l = jnp.exp(bl - b)
            qt, kt, qb, kb = q * e_q, k * e_k, q * e_b, k * e_l
            st0 = st_ref[0, c]
            dst = dstate[...]
            a = jnp.where(tril, _hdot_nt(qt, kt), 0.0)
            da = jnp.where(tril, _hdot_nt(do, v), 0.0)
            dq = _hdot_nn(da, kt) * e_q + _hdot_nn(do, st0) * e_b
            dkb = _hdot_nn(v, dst) * e_l
            dk = _hdot_tn(da, qt) * e_k + dkb
            dv = _hdot_tn(a, do) + _hdot_nt(kb, dst)
            e_bl = jnp.exp(bl)
            dstate[...] = dst * e_bl + _hdot_tn(do, qb)
            db = q * dq - k * dk
            db_last = jnp.sum(k * dkb, axis=0, keepdims=True) + e_bl * jnp.sum(st0 * dst, axis=0, keepdims=True)
            dlogf = _exact_ones_dot(ones_u, db) + db_last
            dfg = dlogf / f - dk
            df_ref[rows, :] = (dfg * (1.0 - lb) * (sig * (1.0 - sig))).astype(BF16)
            glb_ref[...] += _rows8(dfg * (1.0 - sig)) * (lb * (1.0 - lb))
            dq_ref[rows, :] = (dq * (sq * (1.0 + qp * (1.0 - sq)))).astype(BF16)
            di_ref[rows, :] = dv.astype(BF16)
            return carry

        lax.fori_loop(0, ncb, chunk, 0)

    def colblk(group):
        return pl.BlockSpec((tb, HEAD), lambda h, j: (nb - 1 - j, group * N_HEADS + h))

    vec = pl.BlockSpec((1, HEAD), lambda h, j: (0, h))
    blk = pl.BlockSpec((tb, HEAD), lambda h, j: (nb - 1 - j, h))
    grad = jax.ShapeDtypeStruct((s, HGRN_W), BF16)
    return pl.pallas_call(
        body, name=name, grid=(N_HEADS, nb),
        in_specs=[colblk(0), colblk(1), colblk(2), colblk(3), vec, vec, pl.BlockSpec((1, HEAD), lambda h, j: (0, 0)),
                  blk, pl.BlockSpec((1, ncb, HEAD, HEAD), lambda h, j: (h, nb - 1 - j, 0, 0)), blk],
        out_specs=[blk, blk, blk, blk, pl.BlockSpec((8, HEAD), lambda h, j: (0, h)),
                   pl.BlockSpec((8, HEAD), lambda h, j: (0, 0))],
        out_shape=[grad, grad, grad, grad, jax.ShapeDtypeStruct((8, HGRN_W), F32), jax.ShapeDtypeStruct((8, HEAD), F32)],
        scratch_shapes=[pltpu.VMEM((HEAD, HEAD), F32)],
        compiler_params=_params("arbitrary", "arbitrary"),
    )(proj, proj, proj, proj, lb0, lb1, norm_w, o, states, dmix)


def _conv3(x0, x1, x2, w_ref):
    y = x0 * w_ref[0:1, :]
    y = y + x1 * w_ref[1:2, :]
    return y + x2 * w_ref[2:3, :]


def _sconv_fwd(proj, w8, name, tb=256):
    s = proj.shape[0]
    tb = min(tb, s)
    hb = tb // HALO

    def body(cb_ref, cc_ref, ch_ref, cch_ref, chh_ref, w_ref, y_ref):
        first = pl.program_id(0) == 0
        u = cc_ref[...] * ch_ref[...]
        uh = jnp.where(first, 0.0, cch_ref[...] * chh_ref[...])
        conv = _conv3(_shift_down(u, uh, 2), _shift_down(u, uh, 1), u, w_ref)
        y_ref[...] = (cb_ref[...] * conv).astype(BF16)

    def blk(g):
        return pl.BlockSpec((tb, HGRN_W), lambda j: (j, g))

    def halo(g):
        return pl.BlockSpec((HALO, HGRN_W), lambda j: (jnp.maximum(j * hb - 1, 0), g))

    return pl.pallas_call(
        body, name=name, grid=(s // tb,),
        in_specs=[blk(4), blk(5), blk(6), halo(5), halo(6), pl.BlockSpec((HALO, HGRN_W), lambda j: (0, 0))],
        out_specs=pl.BlockSpec((tb, HGRN_W), lambda j: (j, 0)),
        out_shape=jax.ShapeDtypeStruct((s, HGRN_W), BF16),
        compiler_params=_params("parallel"),
    )(proj, proj, proj, proj, proj, w8)


def _sconv_bwd(proj, w8, dmix, name, tb=256):
    s = proj.shape[0]
    tb = min(tb, s)
    hb = tb // HALO
    nb = s // tb
    last_h = s // HALO - 1

    def body(cb_ref, cc_ref, ch_ref, cch_ref, chh_ref, cbn_ref, dy_ref, dyn_ref, w_ref,
             dcb_ref, dcc_ref, dch_ref, gw_ref):
        j = pl.program_id(0)

        @pl.when(j == 0)
        def _():
            gw_ref[...] = jnp.zeros_like(gw_ref)

        cc, ch, cb = cc_ref[...], ch_ref[...], cb_ref[...]
        u = cc * ch
        uh = jnp.where(j == 0, 0.0, cch_ref[...] * chh_ref[...])
        u2, u1 = _shift_down(u, uh, 2), _shift_down(u, uh, 1)
        conv = _conv3(u2, u1, u, w_ref)
        dy = dy_ref[...]
        dcb_ref[...] = (dy * conv).astype(BF16)
        dc = dy * cb
        dcn = jnp.where(j == nb - 1, 0.0, dyn_ref[...] * cbn_ref[...])
        gw_ref[0:8, :] += _rows8(dc * u2)
        gw_ref[8:16, :] += _rows8(dc * u1)
        gw_ref[16:24, :] += _rows8(dc * u)
        du = dc * w_ref[2:3, :] + _shift_up(dc, dcn, 1) * w_ref[1:2, :] + _shift_up(dc, dcn, 2) * w_ref[0:1, :]
        dcc_ref[...] = (du * ch).astype(BF16)
        dch_ref[...] = (du * cc).astype(BF16)

    def blk(g):
        return pl.BlockSpec((tb, HGRN_W), lambda j: (j, g))

    def halo_prev(g):
        return pl.BlockSpec((HALO, HGRN_W), lambda j: (jnp.maximum(j * hb - 1, 0), g))

    def halo_next(g):
        return pl.BlockSpec((HALO, HGRN_W), lambda j: (jnp.minimum((j + 1) * hb, last_h), g))

    out = pl.BlockSpec((tb, HGRN_W), lambda j: (j, 0))
    grad = jax.ShapeDtypeStruct((s, HGRN_W), BF16)
    return pl.pallas_call(
        body, name=name, grid=(nb,),
        in_specs=[blk(4), blk(5), blk(6), halo_prev(5), halo_prev(6), halo_next(4), blk(1), halo_next(1),
                  pl.BlockSpec((HALO, HGRN_W), lambda j: (0, 0))],
        out_specs=[out, out, out, pl.BlockSpec((24, HGRN_W), lambda j: (0, 0))],
        out_shape=[grad, grad, grad, jax.ShapeDtypeStruct((24, HGRN_W), F32)],
        compiler_params=_params("arbitrary"),
    )(proj, proj, proj, proj, proj, proj, dmix, dmix, w8)


def _attn_fwd(q, kk, vv, name, tb=256):
    s, d = q.shape
    m = kk.shape[0]
    tb = min(tb, s)
    scale = MEM_HEAD_DIM ** -0.5

    def body(q_ref, k_ref, v_ref, o_ref):
        for hh in range(MEM_HEADS):
            cols = slice(hh * MEM_HEAD_DIM, (hh + 1) * MEM_HEAD_DIM)
            sc = _dot_nt(q_ref[:, cols], k_ref[:, cols]) * scale
            sc = sc - jnp.max(sc, axis=-1, keepdims=True)
            e = jnp.exp(sc)
            p = e / jnp.sum(e, axis=-1, keepdims=True)
            o_ref[:, cols] = _dot_nn(p, v_ref[:, cols]).astype(BF16)

    full = pl.BlockSpec((m, d), lambda i: (0, 0))
    return pl.pallas_call(
        body, name=name, grid=(s // tb,),
        in_specs=[pl.BlockSpec((tb, d), lambda i: (i, 0)), full, full],
        out_specs=pl.BlockSpec((tb, d), lambda i: (i, 0)),
        out_shape=jax.ShapeDtypeStruct((s, d), BF16),
        compiler_params=_params("parallel"),
    )(q, kk, vv)


def _attn_bwd(q, kk, vv, datt, name, tb=256):
    s, d = q.shape
    m = kk.shape[0]
    tb = min(tb, s)
    scale = MEM_HEAD_DIM ** -0.5

    def body(q_ref, k_ref, v_ref, do_ref, dq_ref, dk_ref, dv_ref):
        @pl.when(pl.program_id(0) == 0)
        def _():
            dk_ref[...] = jnp.zeros_like(dk_ref)
            dv_ref[...] = jnp.zeros_like(dv_ref)

        for hh in range(MEM_HEADS):
            cols = slice(hh * MEM_HEAD_DIM, (hh + 1) * MEM_HEAD_DIM)
            qh, kh, vh, doh = q_ref[:, cols], k_ref[:, cols], v_ref[:, cols], do_ref[:, cols]
            sc = _dot_nt(qh, kh) * scale
            sc = sc - jnp.max(sc, axis=-1, keepdims=True)
            e = jnp.exp(sc)
            p = e / jnp.sum(e, axis=-1, keepdims=True)
            dp = _dot_nt(doh, vh)
            ds = p * (dp - jnp.sum(dp * p, axis=-1, keepdims=True)) * scale
            dq_ref[:, cols] = _dot_nn(ds, kh).astype(BF16)
            dk_ref[:, cols] += _dot_tn(ds, qh)
            dv_ref[:, cols] += _dot_tn(p, doh)

    full = pl.BlockSpec((m, d), lambda i: (0, 0))
    row = pl.BlockSpec((tb, d), lambda i: (i, 0))
    return pl.pallas_call(
        body, name=name, grid=(s // tb,),
        in_specs=[row, full, full, row],
        out_specs=[row, full, full],
        out_shape=[jax.ShapeDtypeStruct((s, d), BF16), jax.ShapeDtypeStruct((m, d), F32),
                   jax.ShapeDtypeStruct((m, d), F32)],
        compiler_params=_params("arbitrary"),
    )(q, kk, vv, datt)


def _ffn_fwd(g, u, w8, bias, name, tb=256, tc=512):
    s, f = g.shape
    tb = min(tb, s)
    hb = tb // HALO

    def body(g_ref, gh_ref, u_ref, w_ref, b_ref, z_ref):
        gv = g_ref[...]
        gh = jnp.where(pl.program_id(1) == 0, 0.0, gh_ref[...])
        a = _conv3(_shift_down(gv, gh, 2), _shift_down(gv, gh, 1), gv, w_ref) + b_ref[...]
        z_ref[...] = ((a * _sigmoid(a)) * u_ref[...]).astype(BF16)

    blk = pl.BlockSpec((tb, tc), lambda c, j: (j, c))
    return pl.pallas_call(
        body, name=name, grid=(f // tc, s // tb),
        in_specs=[blk, pl.BlockSpec((HALO, tc), lambda c, j: (jnp.maximum(j * hb - 1, 0), c)), blk,
                  pl.BlockSpec((HALO, tc), lambda c, j: (0, c)), pl.BlockSpec((1, tc), lambda c, j: (0, c))],
        out_specs=blk,
        out_shape=jax.ShapeDtypeStruct((s, f), BF16),
        compiler_params=_params("parallel", "parallel"),
    )(g, g, u, w8, bias)


def _ffn_bwd_a(g, u, dz, w8, bias, name, tb=256, tc=512):
    s, f = g.shape
    tb = min(tb, s)
    hb = tb // HALO

    def body(g_ref, gh_ref, u_ref, dz_ref, w_ref, b_ref, da_ref, du_ref, gb_ref, gw_ref):
        j = pl.program_id(1)

        @pl.when(j == 0)
        def _():
            gb_ref[...] = jnp.zeros_like(gb_ref)
            gw_ref[...] = jnp.zeros_like(gw_ref)

        gv = g_ref[...]
        gh = jnp.where(j == 0, 0.0, gh_ref[...])
        g2, g1 = _shift_down(gv, gh, 2), _shift_down(gv, gh, 1)
        a = _conv3(g2, g1, gv, w_ref) + b_ref[...]
        sa = _sigmoid(a)
        dz = dz_ref[...]
        du_ref[...] = (dz * (a * sa)).astype(BF16)
        da = dz * u_ref[...] * (sa * (1.0 + a * (1.0 - sa)))
        da_ref[...] = da
        gb_ref[...] += _rows8(da)
        gw_ref[0:8, :] += _rows8(da * g2)
        gw_ref[8:16, :] += _rows8(da * g1)
        gw_ref[16:24, :] += _rows8(da * gv)

    blk = pl.BlockSpec((tb, tc), lambda c, j: (j, c))
    return pl.pallas_call(
        body, name=name, grid=(f // tc, s // tb),
        in_specs=[blk, pl.BlockSpec((HALO, tc), lambda c, j: (jnp.maximum(j * hb - 1, 0), c)), blk, blk,
                  pl.BlockSpec((HALO, tc), lambda c, j: (0, c)), pl.BlockSpec((1, tc), lambda c, j: (0, c))],
        out_specs=[blk, blk, pl.BlockSpec((8, tc), lambda c, j: (0, c)), pl.BlockSpec((24, tc), lambda c, j: (0, c))],
        out_shape=[jax.ShapeDtypeStruct((s, f), F32), jax.ShapeDtypeStruct((s, f), BF16),
                   jax.ShapeDtypeStruct((8, f), F32), jax.ShapeDtypeStruct((24, f), F32)],
        compiler_params=_params("parallel", "arbitrary"),
    )(g, g, u, dz, w8, bias)


def _ffn_bwd_b(da, w8, name, tb=256, tc=512):
    s, f = da.shape
    tb = min(tb, s)
    hb = tb // HALO
    nb = s // tb
    last_h = s // HALO - 1

    def body(da_ref, dan_ref, w_ref, dg_ref):
        da = da_ref[...]
        dan = jnp.where(pl.program_id(1) == nb - 1, 0.0, dan_ref[...])
        dg = da * w_ref[2:3, :] + _shift_up(da, dan, 1) * w_ref[1:2, :] + _shift_up(da, dan, 2) * w_ref[0:1, :]
        dg_ref[...] = dg.astype(BF16)

    blk = pl.BlockSpec((tb, tc), lambda c, j: (j, c))
    return pl.pallas_call(
        body, name=name, grid=(f // tc, nb),
        in_specs=[blk, pl.BlockSpec((HALO, tc), lambda c, j: (jnp.minimum((j + 1) * hb, last_h), c)),
                  pl.BlockSpec((HALO, tc), lambda c, j: (0, c))],
        out_specs=blk,
        out_shape=jax.ShapeDtypeStruct((s, f), BF16),
        compiler_params=_params("parallel", "parallel"),
    )(da, da, w8)


def _window(ref, axis, slot, size):
    start = pl.multiple_of(slot * size, size)
    if axis == 0:
        return ref.at[pl.ds(start, size), :]
    return ref.at[:, pl.ds(start, size)]


def _chip_peers():
    x, y, c = lax.axis_index("x"), lax.axis_index("y"), lax.axis_index("c")
    peers = [(1 - x, y, c), (x, 1 - y, c), (1 - x, 1 - y, c)]
    slots = [2 * (1 - x) + y, 2 * x + (1 - y), 2 * (1 - x) + (1 - y)]
    return 2 * x + y, peers, slots


def _all_gather_chips(shards, axes, name):
    n = len(shards)
    any_spec = pl.BlockSpec(memory_space=pl.ANY)

    def full_shape(sh, ax):
        return (sh.shape[0] * N_CHIPS, sh.shape[1]) if ax == 0 else (sh.shape[0], sh.shape[1] * N_CHIPS)

    def body(*refs):
        ins, outs = refs[:n], refs[n:2 * n]
        send_sems, recv_sems, local_sems = refs[2 * n:]
        slot, peers, slots = _chip_peers()
        local, remote = [], []
        for t in range(n):
            size = ins[t].shape[axes[t]]
            mine = _window(outs[t], axes[t], slot, size)
            cp = pltpu.make_async_copy(ins[t], mine, local_sems.at[t])
            cp.start()
            local.append(cp)
            for k in range(3):
                rc = pltpu.make_async_remote_copy(
                    src_ref=ins[t], dst_ref=mine, send_sem=send_sems.at[3 * t + k], recv_sem=recv_sems.at[3 * t + k],
                    device_id=peers[k], device_id_type=MESH)
                rc.start()
                remote.append(rc)
        for t in range(n):
            size = ins[t].shape[axes[t]]
            for k in range(3):
                pltpu.make_async_remote_copy(
                    src_ref=ins[t], dst_ref=_window(outs[t], axes[t], slots[k], size),
                    send_sem=send_sems.at[3 * t + k], recv_sem=recv_sems.at[3 * t + k],
                    device_id=peers[k], device_id_type=MESH).wait_recv()
        for rc in remote:
            rc.wait_send()
        for cp in local:
            cp.wait()

    return pl.pallas_call(
        body, name=name,
        in_specs=[any_spec] * n, out_specs=[any_spec] * n,
        out_shape=[jax.ShapeDtypeStruct(full_shape(sh, ax), sh.dtype) for sh, ax in zip(shards, axes)],
        scratch_shapes=[pltpu.SemaphoreType.DMA((3 * n,)), pltpu.SemaphoreType.DMA((3 * n,)),
                        pltpu.SemaphoreType.DMA((n,))],
    )(*shards)


def _reduce_scatter_send(grads_bf16, grads_f32, axes, name):
    n = len(grads_bf16)
    any_spec = pl.BlockSpec(memory_space=pl.ANY)

    def shard_shape(g, ax):
        return (g.shape[0] // N_CHIPS, g.shape[1]) if ax == 0 else (g.shape[0], g.shape[1] // N_CHIPS)

    shapes = [shard_shape(g, ax) for g, ax in zip(grads_f32, axes)]

    def body(*refs):
        gb, gf = refs[:n], refs[n:2 * n]
        recv, own = refs[2 * n:3 * n], refs[3 * n:4 * n]
        send_sems, recv_sems, local_sems = refs[4 * n:]
        slot, peers, slots = _chip_peers()
        local, remote = [], []
        for t in range(n):
            size = shapes[t][axes[t]]
            cp = pltpu.make_async_copy(_window(gf[t], axes[t], slot, size), own[t], local_sems.at[t])
            cp.start()
            local.append(cp)
            for k in range(3):
                rc = pltpu.make_async_remote_copy(
                    src_ref=_window(gb[t], axes[t], slots[k], size), dst_ref=recv[t].at[k],
                    send_sem=send_sems.at[3 * t + k], recv_sem=recv_sems.at[3 * t + k],
                    device_id=peers[k], device_id_type=MESH)
                rc.start()
                remote.append(rc)
        for rc in remote:
            rc.wait_recv()
        for rc in remote:
            rc.wait_send()
        for cp in local:
            cp.wait()

    return pl.pallas_call(
        body, name=name,
        in_specs=[any_spec] * (2 * n), out_specs=[any_spec] * (2 * n),
        out_shape=[jax.ShapeDtypeStruct((3,) + sh, BF16) for sh in shapes]
        + [jax.ShapeDtypeStruct(sh, F32) for sh in shapes],
        scratch_shapes=[pltpu.SemaphoreType.DMA((3 * n,)), pltpu.SemaphoreType.DMA((3 * n,)),
                        pltpu.SemaphoreType.DMA((n,))],
    )(*grads_bf16, *grads_f32)


def _sibling_exchange(arrs, name):
    n = len(arrs)
    any_spec = pl.BlockSpec(memory_space=pl.ANY)

    def body(*refs):
        ins, outs = refs[:n], refs[n:2 * n]
        send_sems, recv_sems = refs[2 * n:]
        sibling = (lax.axis_index("x"), lax.axis_index("y"), 1 - lax.axis_index("c"))
        copies = []
        for t in range(n):
            rc = pltpu.make_async_remote_copy(
                src_ref=ins[t], dst_ref=outs[t], send_sem=send_sems.at[t], recv_sem=recv_sems.at[t],
                device_id=sibling, device_id_type=MESH)
            rc.start()
            copies.append(rc)
        for rc in copies:
            rc.wait_recv()
        for rc in copies:
            rc.wait_send()

    return pl.pallas_call(
        body, name=name,
        in_specs=[any_spec] * n, out_specs=[any_spec] * n,
        out_shape=[jax.ShapeDtypeStruct(a.shape, a.dtype) for a in arrs],
        scratch_shapes=[pltpu.SemaphoreType.DMA((n,)), pltpu.SemaphoreType.DMA((n,))],
    )(*arrs)


def _all_reduce_small(packed, name):
    nc = packed.shape[1]
    vmem = pl.BlockSpec(memory_space=pltpu.VMEM)

    def body(in_ref, out_ref, gbuf, send_sems, recv_sems):
        x, y, c = lax.axis_index("x"), lax.axis_index("y"), lax.axis_index("c")
        me = 4 * x + 2 * y + c
        gbuf[me] = jnp.sum(in_ref[...], axis=0, keepdims=True)
        copies = []
        for k in range(1, 8):
            peer = (x ^ ((k >> 2) & 1), y ^ ((k >> 1) & 1), c ^ (k & 1))
            rc = pltpu.make_async_remote_copy(
                src_ref=gbuf.at[me], dst_ref=gbuf.at[me], send_sem=send_sems.at[k - 1], recv_sem=recv_sems.at[k - 1],
                device_id=peer, device_id_type=MESH)
            rc.start()
            copies.append(rc)
        for k in range(1, 8):
            peer = (x ^ ((k >> 2) & 1), y ^ ((k >> 1) & 1), c ^ (k & 1))
            pltpu.make_async_remote_copy(
                src_ref=gbuf.at[me], dst_ref=gbuf.at[me ^ k], send_sem=send_sems.at[k - 1],
                recv_sem=recv_sems.at[k - 1], device_id=peer, device_id_type=MESH).wait_recv()
        for rc in copies:
            rc.wait_send()
        tot = gbuf[0]
        for d in range(1, 8):
            tot = tot + gbuf[d]
        out_ref[...] = tot

    return pl.pallas_call(
        body, name=name,
        in_specs=[vmem], out_specs=vmem,
        out_shape=jax.ShapeDtypeStruct((1, nc), F32),
        scratch_shapes=[pltpu.VMEM((8, 1, nc), F32), pltpu.SemaphoreType.DMA((7,)), pltpu.SemaphoreType.DMA((7,))],
    )(packed)


def _sum4(own, recv, name):
    r, c = own.shape
    tr = min(r, 128)

    def body(own_ref, recv_ref, o_ref):
        acc = own_ref[...]
        for k in range(3):
            acc = acc + recv_ref[k].astype(F32)
        o_ref[...] = acc

    return pl.pallas_call(
        body, name=name, grid=(r // tr,),
        in_specs=[pl.BlockSpec((tr, c), lambda i: (i, 0)), pl.BlockSpec((3, tr, c), lambda i: (0, i, 0))],
        out_specs=pl.BlockSpec((tr, c), lambda i: (i, 0)),
        out_shape=jax.ShapeDtypeStruct((r, c), F32),
        compiler_params=_params("parallel"),
    )(own, recv)


def _adamw(w, g_parts, m, v, name):
    r, c = w.shape
    tr = r if r % 128 else 128
    npart = len(g_parts)

    def body(*refs):
        w_ref = refs[0]
        g_refs = refs[1:1 + npart]
        m_ref, v_ref, g_out, d_out, m_out, v_out = refs[1 + npart:]
        g = g_refs[0][...]
        for gr in g_refs[1:]:
            g = g + gr[...]
        mm = ADAM_B1 * m_ref[...] + (1.0 - ADAM_B1) * g
        vv = ADAM_B2 * v_ref[...] + (1.0 - ADAM_B2) * (g * g)
        m_hat = mm / (1.0 - ADAM_B1 ** ADAM_STEP)
        v_hat = vv / (1.0 - ADAM_B2 ** ADAM_STEP)
        g_out[...] = g
        d_out[...] = -ADAM_LR * (m_hat / (jnp.sqrt(v_hat) + ADAM_EPS) + ADAM_WD * w_ref[...])
        m_out[...] = mm
        v_out[...] = vv

    blk = pl.BlockSpec((tr, c), lambda i: (i, 0))
    shp = jax.ShapeDtypeStruct((r, c), F32)
    return pl.pallas_call(
        body, name=name, grid=(r // tr,),
        in_specs=[blk] * (3 + npart), out_specs=[blk] * 4, out_shape=[shp] * 4,
        compiler_params=_params("parallel"),
    )(w, *g_parts, m, v)


def _pad_rows8(w):
    return jnp.pad(w, ((0, HALO - w.shape[0]), (0, 0)))


def kernel(x, mem, hgrn_lb, norm1_w, w_in, hgrn_norm_w, sconv_w, w_out, norm2_w, mem_norm_w, wq, wk, wv, wo, norm3_w, w_gate, w_up, ffn_conv_w, ffn_conv_b, w_down, final_norm_w, loss_target, m_hgrn_lb, m_norm1_w, m_w_in, m_hgrn_norm_w, m_sconv_w, m_w_out, m_norm2_w, m_mem_norm_w, m_wq, m_wk, m_wv, m_wo, m_norm3_w, m_w_gate, m_w_up, m_ffn_conv_w, m_ffn_conv_b, m_w_down, m_final_norm_w, v_hgrn_lb, v_norm1_w, v_w_in, v_hgrn_norm_w, v_sconv_w, v_w_out, v_norm2_w, v_mem_norm_w, v_wq, v_wk, v_wv, v_wo, v_norm3_w, v_w_gate, v_w_up, v_ffn_conv_w, v_ffn_conv_b, v_w_down, v_final_norm_w):
    xs, mems, tgt = x[0], mem[0], loss_target[0]
    d = xs.shape[1]
    fnw = final_norm_w.reshape(1, d)

    big = {"w_in": (w_in[0], 1), "w_out": (w_out[0], 0), "wq": (wq[0], 0), "wk": (wk[0], 0), "wv": (wv[0], 0),
           "wo": (wo[0], 0), "w_gate": (w_gate[0], 1), "w_up": (w_up[0], 1), "w_down": (w_down[0], 0)}
    names = list(big)
    shards = [_cast_bf16(big[n][0], "cast_" + n) for n in names] + [_pad_rows8(sconv_w[0]), _pad_rows8(ffn_conv_w[0])]
    axes = [big[n][1] for n in names] + [1, 1]
    gathered = _all_gather_chips(shards, axes, "gather_weights")
    wf = dict(zip(names, gathered[:len(names)]))
    sconv8, fconv8 = gathered[len(names)], gathered[len(names) + 1]
    lb0, lb1 = hgrn_lb[0:1], hgrn_lb[1:2]

    h1 = _rmsnorm_fwd(xs, norm1_w, "norm1")
    proj = _matmul(h1, wf["w_in"], "nn", "proj_in")
    o_h, og, states = _hgrn_fwd(proj, lb0, lb1, hgrn_norm_w, "hgrn_fwd")
    yc = _sconv_fwd(proj, sconv8, "sconv_fwd")
    mix = jnp.concatenate([og, yc], axis=1)
    x1 = _matmul(mix, wf["w_out"], "nn", "proj_out", residual=xs)
    h2 = _rmsnorm_fwd(x1, norm2_w, "norm2")
    mem_n = _rmsnorm_fwd(mems, mem_norm_w, "norm_mem")
    qa = _matmul(h2, wf["wq"], "nn", "attn_q", out_dtype=BF16)
    ka = _matmul(mem_n, wf["wk"], "nn", "attn_k", out_dtype=BF16)
    va = _matmul(mem_n, wf["wv"], "nn", "attn_v", out_dtype=BF16)
    att = _attn_fwd(qa, ka, va, "attn_fwd")
    x2 = _matmul(att, wf["wo"], "nn", "attn_o", residual=x1)
    h3 = _rmsnorm_fwd(x2, norm3_w, "norm3")
    gate = _matmul(h3, wf["w_gate"], "nn", "ffn_gate")
    up = _matmul(h3, wf["w_up"], "nn", "ffn_up")
    z = _ffn_fwd(gate, up, fconv8, ffn_conv_b, "ffn_act")
    x3 = _matmul(z, wf["w_down"], "nn", "ffn_down", residual=x2, tk=1408)

    dx3, g_final, loss8 = _final_loss_bwd(x3, tgt, fnw, "loss_bwd")
    gw = {}
    dz = _matmul(dx3, wf["w_down"], "nt", "d_z")
    gw["w_down"] = _matmul(z, dx3, "tn", "g_w_down", extra_bf16=True, tk=1024)
    da, du, g_fb, g_fw = _ffn_bwd_a(gate, up, dz, fconv8, ffn_conv_b, "ffn_act_bwd")
    dgate = _ffn_bwd_b(da, fconv8, "ffn_conv_bwd")
    dh3 = _matmul(dgate, wf["w_gate"], "nt", "d_h3_gate", tk=1408)
    dh3 = _matmul(du, wf["w_up"], "nt", "d_h3_up", residual=dh3, tk=1408)
    gw["w_gate"] = _matmul(h3, dgate, "tn", "g_w_gate", extra_bf16=True, tk=1024)
    gw["w_up"] = _matmul(h3, du, "tn", "g_w_up", extra_bf16=True, tk=1024)
    dx2, g_n3 = _rmsnorm_bwd(dh3, x2, norm3_w, dx3, "norm3_bwd")
    datt = _matmul(dx2, wf["wo"], "nt", "d_att", out_dtype=BF16)
    gw["wo"] = _matmul(att, dx2, "tn", "g_wo", extra_bf16=True, tk=1024)
    dqa, dka, dva = _attn_bwd(qa, ka, va, datt, "attn_bwd")
    dh2 = _matmul(dqa, wf["wq"], "nt", "d_h2")
    gw["wq"] = _matmul(h2, dqa, "tn", "g_wq", extra_bf16=True, tk=1024)
    gw["wk"] = _matmul(mem_n, dka, "tn", "g_wk", extra_bf16=True)
    gw["wv"] = _matmul(mem_n, dva, "tn", "g_wv", extra_bf16=True)
    dmem_n = _matmul(dka, wf["wk"], "nt", "d_memn_k")
    dmem_n = _matmul(dva, wf["wv"], "nt", "d_memn_v", residual=dmem_n)
    _, g_nm = _rmsnorm_bwd(dmem_n, mems, mem_norm_w, None, "norm_mem_bwd")
    dx1, g_n2 = _rmsnorm_bwd(dh2, x1, norm2_w, dx2, "norm2_bwd")
    dmix = _matmul(dx1, wf["w_out"], "nt", "d_mix")
    gw["w_out"] = _matmul(mix, dx1, "tn", "g_w_out", extra_bf16=True, tk=1024)
    dcb, dcc, dch, g_sw = _sconv_bwd(proj, sconv8, dmix, "sconv_bwd")
    dq, df, di, dg, g_lb, g_hn = _hgrn_bwd(proj, lb0, lb1, hgrn_norm_w, o_h, states, dmix, "hgrn_bwd")
    dproj = jnp.concatenate([dq, df, di, dg, dcb, dcc, dch], axis=1)
    dh1 = _matmul(dproj, wf["w_in"], "nt", "d_h1", tk=1024)
    gw["w_in"] = _matmul(h1, dproj, "tn", "g_w_in", extra_bf16=True, tk=1024)
    dx, g_n1 = _rmsnorm_bwd(dh1, xs, norm1_w, dx1, "norm1_bwd")

    small = [g_n1, g_n2, g_n3, g_final, g_nm, g_lb, g_hn, g_fb,
             g_sw[0:8], g_sw[8:16], g_sw[16:24], g_fw[0:8], g_fw[8:16], g_fw[16:24], loss8]
    widths = [a.shape[1] for a in small]
    tot = _all_reduce_small(jnp.concatenate(small, axis=1), "all_reduce_small")
    offs = [0]
    for wd_ in widths:
        offs.append(offs[-1] + wd_)
    sm = [tot[:, offs[i]:offs[i + 1]] for i in range(len(small))]
    s_n1, s_n2, s_n3, s_final, s_nm, s_lb, s_hn, s_fb = sm[:8]
    s_sw = jnp.concatenate(sm[8:11], axis=0)
    s_fw = jnp.concatenate(sm[11:14], axis=0)
    loss = sm[14][0, 0]
    slot = 2 * lax.axis_index("x") + lax.axis_index("y")
    s_sw = lax.dynamic_slice_in_dim(s_sw, slot * (HGRN_W // N_CHIPS), HGRN_W // N_CHIPS, axis=1)
    fsh = ffn_conv_w.shape[2]
    s_fw = lax.dynamic_slice_in_dim(s_fw, slot * fsh, fsh, axis=1)
    s_lb2 = jnp.concatenate([s_lb, -s_lb], axis=0)

    recv_own = _reduce_scatter_send([gw[n][1] for n in names], [gw[n][0] for n in names],
                                    [big[n][1] for n in names], "reduce_scatter")
    recvs, owns = recv_own[:len(names)], recv_own[len(names):]
    core_sums = [_sum4(owns[i], recvs[i], "core_sum_" + n) for i, n in enumerate(names)]
    sib_sums = _sibling_exchange(core_sums, "sibling_exchange")

    moments = {"hgrn_lb": (m_hgrn_lb, v_hgrn_lb), "norm1_w": (m_norm1_w, v_norm1_w), "w_in": (m_w_in, v_w_in),
               "hgrn_norm_w": (m_hgrn_norm_w, v_hgrn_norm_w), "sconv_w": (m_sconv_w, v_sconv_w),
               "w_out": (m_w_out, v_w_out), "norm2_w": (m_norm2_w, v_norm2_w),
               "mem_norm_w": (m_mem_norm_w, v_mem_norm_w), "wq": (m_wq, v_wq), "wk": (m_wk, v_wk), "wv": (m_wv, v_wv),
               "wo": (m_wo, v_wo), "norm3_w": (m_norm3_w, v_norm3_w), "w_gate": (m_w_gate, v_w_gate),
               "w_up": (m_w_up, v_w_up), "ffn_conv_w": (m_ffn_conv_w, v_ffn_conv_w),
               "ffn_conv_b": (m_ffn_conv_b, v_ffn_conv_b), "w_down": (m_w_down, v_w_down),
               "final_norm_w": (m_final_norm_w, v_final_norm_w)}
    weights = {"hgrn_lb": hgrn_lb, "norm1_w": norm1_w, "w_in": w_in, "hgrn_norm_w": hgrn_norm_w, "sconv_w": sconv_w,
               "w_out": w_out, "norm2_w": norm2_w, "mem_norm_w": mem_norm_w, "wq": wq, "wk": wk, "wv": wv, "wo": wo,
               "norm3_w": norm3_w, "w_gate": w_gate, "w_up": w_up, "ffn_conv_w": ffn_conv_w, "ffn_conv_b": ffn_conv_b,
               "w_down": w_down, "final_norm_w": final_norm_w}
    small_g = {"hgrn_lb": s_lb2, "norm1_w": s_n1, "hgrn_norm_w": s_hn, "sconv_w": s_sw, "norm2_w": s_n2,
               "mem_norm_w": s_nm, "norm3_w": s_n3, "ffn_conv_w": s_fw, "ffn_conv_b": s_fb, "final_norm_w": s_final}
    order = list(weights)
    res = {}
    for n in order:
        w_full = weights[n]
        shape = w_full.shape
        w2 = w_full.reshape((-1, shape[-1]))
        m2, v2 = (t.reshape(w2.shape) for t in moments[n])
        if n in big:
            i = names.index(n)
            parts = [core_sums[i], sib_sums[i]]
        else:
            parts = [small_g[n].reshape(w2.shape)]
        res[n] = [t.reshape(shape) for t in _adamw(w2, parts, m2, v2, "adamw_" + n)]

    return (loss, dx[None], *[res[n][0] for n in order], *[res[n][1] for n in order],
            *[res[n][2] for n in order], *[res[n][3] for n in order])
```

```python
import functools

import jax
import jax.numpy as jnp
from jax import lax
from jax.experimental import pallas as pl
from jax.experimental.pallas import tpu as pltpu

F32 = jnp.float32
BF16 = jnp.bfloat16
MESH = pl.DeviceIdType.MESH

EPS = 1e-6
HGRN_W = 1024
HEAD = 128
N_HEADS = 8
CHUNK = 64
HGRN_UNROLL = 8
MEM_HEADS = 4
MEM_HEAD_DIM = 512
N_CHIPS = 4
HALO = 8

ADAM_LR = 0.001
ADAM_B1 = 0.9
ADAM_B2 = 0.999
ADAM_EPS = 1e-08
ADAM_WD = 0.01
ADAM_STEP = 10


def _sigmoid(x):
    return 1.0 / (1.0 + jnp.exp(-x))


def _dot(a, b, dims):
    return lax.dot_general(a.astype(BF16), b.astype(BF16), (dims, ((), ())),
                           preferred_element_type=F32)


def _dot_nn(a, b):
    return _dot(a, b, ((1,), (0,)))


def _dot_nt(a, b):
    return _dot(a, b, ((1,), (1,)))


def _dot_tn(a, b):
    return _dot(a, b, ((0,), (0,)))


def _hdot(a, b, dims):
    return lax.dot_general(a, b, (dims, ((), ())), precision=lax.Precision.HIGH, preferred_element_type=F32)


def _hdot_nn(a, b):
    return _hdot(a, b, ((1,), (0,)))


def _hdot_nt(a, b):
    return _hdot(a, b, ((1,), (1,)))


def _hdot_tn(a, b):
    return _hdot(a, b, ((0,), (0,)))


def _exact_ones_dot(ones_bf16, x):
    hi = x.astype(BF16)
    r1 = x - hi.astype(F32)
    mid = r1.astype(BF16)
    lo = (r1 - mid.astype(F32)).astype(BF16)
    dims = (((1,), (0,)), ((), ()))
    return (lax.dot_general(ones_bf16, hi, dims, preferred_element_type=F32)
            + lax.dot_general(ones_bf16, mid, dims, preferred_element_type=F32)
            + lax.dot_general(ones_bf16, lo, dims, preferred_element_type=F32))


def _rows8(v):
    t, c = v.shape
    return v.reshape(t // 8, 8, c).sum(axis=0)


def _shift_down(x, halo, s):
    rolled = pltpu.roll(x, s, 0)
    hrolled = pltpu.roll(halo, s, 0)
    row = lax.broadcasted_iota(jnp.int32, hrolled.shape, 0)
    head = jnp.where(row < s, hrolled, rolled[:HALO])
    return jnp.concatenate([head, rolled[HALO:]], axis=0)


def _shift_up(x, halo, s):
    t = x.shape[0]
    rolled = pltpu.roll(x, t - s, 0)
    hrolled = pltpu.roll(halo, HALO - s, 0)
    row = lax.broadcasted_iota(jnp.int32, hrolled.shape, 0)
    tail = jnp.where(row >= HALO - s, hrolled, rolled[t - HALO:])
    return jnp.concatenate([rolled[:t - HALO], tail], axis=0)


def _params(*sem):
    return pltpu.CompilerParams(dimension_semantics=sem)


def _row_tile(r, pref):
    while r % pref:
        pref //= 2
    return pref


def _cast_bf16(x, name):
    r, c = x.shape
    tr = _row_tile(r, 256)

    def body(x_ref, o_ref):
        o_ref[...] = x_ref[...].astype(BF16)

    return pl.pallas_call(
        body, name=name, grid=(r // tr,),
        in_specs=[pl.BlockSpec((tr, c), lambda i: (i, 0))],
        out_specs=pl.BlockSpec((tr, c), lambda i: (i, 0)),
        out_shape=jax.ShapeDtypeStruct((r, c), BF16),
        compiler_params=_params("parallel"),
    )(x)


def _rmsnorm_fwd(x, w, name, tm=256):
    s, d = x.shape
    tm = min(tm, s)

    def body(x_ref, w_ref, o_ref):
        xv = x_ref[...]
        r = lax.rsqrt(jnp.mean(xv * xv, axis=-1, keepdims=True) + EPS)
        o_ref[...] = ((xv * r) * w_ref[...]).astype(BF16)

    return pl.pallas_call(
        body, name=name, grid=(s // tm,),
        in_specs=[pl.BlockSpec((tm, d), lambda i: (i, 0)), pl.BlockSpec((1, d), lambda i: (0, 0))],
        out_specs=pl.BlockSpec((tm, d), lambda i: (i, 0)),
        out_shape=jax.ShapeDtypeStruct((s, d), BF16),
        compiler_params=_params("parallel"),
    )(x, w)


def _rmsnorm_bwd(dh, x, w, dres, name, tm=256):
    s, d = x.shape
    tm = min(tm, s)
    has_res = dres is not None

    def body(*refs):
        if has_res:
            dh_ref, x_ref, w_ref, dres_ref, dx_ref, gw_ref = refs
        else:
            dh_ref, x_ref, w_ref, dx_ref, gw_ref = refs

        @pl.when(pl.program_id(0) == 0)
        def _():
            gw_ref[...] = jnp.zeros_like(gw_ref)

        xv = x_ref[...]
        dhv = dh_ref[...].astype(F32)
        r = lax.rsqrt(jnp.mean(xv * xv, axis=-1, keepdims=True) + EPS)
        xhat = xv * r
        gw_ref[...] += _rows8(dhv * xhat)
        dxh = dhv * w_ref[...]
        dx = r * (dxh - xhat * jnp.mean(dxh * xhat, axis=-1, keepdims=True))
        if has_res:
            dx = dres_ref[...] + dx
        dx_ref[...] = dx

    row = pl.BlockSpec((tm, d), lambda i: (i, 0))
    in_specs = [row, row, pl.BlockSpec((1, d), lambda i: (0, 0))] + ([row] if has_res else [])
    args = (dh, x, w) + ((dres,) if has_res else ())
    return pl.pallas_call(
        body, name=name, grid=(s // tm,),
        in_specs=in_specs,
        out_specs=[row, pl.BlockSpec((8, d), lambda i: (0, 0))],
        out_shape=[jax.ShapeDtypeStruct((s, d), F32), jax.ShapeDtypeStruct((8, d), F32)],
        compiler_params=_params("arbitrary"),
    )(*args)


def _final_loss_bwd(x3, target, w, name, tm=256):
    s, d = x3.shape
    tm = min(tm, s)

    def body(x_ref, t_ref, w_ref, dx_ref, gw_ref, loss_ref):
        @pl.when(pl.program_id(0) == 0)
        def _():
            gw_ref[...] = jnp.zeros_like(gw_ref)
            loss_ref[...] = jnp.zeros_like(loss_ref)

        xv = x_ref[...]
        r = lax.rsqrt(jnp.mean(xv * xv, axis=-1, keepdims=True) + EPS)
        xhat = xv * r
        y = xhat * w_ref[...]
        err = y - t_ref[...]
        part = 0.5 * jnp.mean(err * err, axis=-1, keepdims=True)
        tot = jnp.sum(part, axis=0, keepdims=True)
        rr = lax.broadcasted_iota(jnp.int32, loss_ref.shape, 0)
        cc = lax.broadcasted_iota(jnp.int32, loss_ref.shape, 1)
        loss_ref[...] += jnp.where((rr == 0) & (cc == 0), tot, 0.0)
        dy = err * (1.0 / d)
        gw_ref[...] += _rows8(dy * xhat)
        dxh = dy * w_ref[...]
        dx_ref[...] = r * (dxh - xhat * jnp.mean(dxh * xhat, axis=-1, keepdims=True))

    row = pl.BlockSpec((tm, d), lambda i: (i, 0))
    return pl.pallas_call(
        body, name=name, grid=(s // tm,),
        in_specs=[row, row, pl.BlockSpec((1, d), lambda i: (0, 0))],
        out_specs=[row, pl.BlockSpec((8, d), lambda i: (0, 0)), pl.BlockSpec((8, 128), lambda i: (0, 0))],
        out_shape=[jax.ShapeDtypeStruct((s, d), F32), jax.ShapeDtypeStruct((8, d), F32),
                   jax.ShapeDtypeStruct((8, 128), F32)],
        compiler_params=_params("arbitrary"),
    )(x3, target, w)


MM_TILES = (1024, 1408, 512, 256, 128)
MM_VMEM_LIMIT = 48 * 1024 * 1024


def _pick_tile(dim):
    for t in MM_TILES:
        if dim % t == 0:
            return t
    return dim


def _matmul(a, b, mode, name, *, out_dtype=F32, residual=None, extra_bf16=False, tm=None, tn=None, tk=None):
    if mode == "nn":
        (m, k), (k2, n) = a.shape, b.shape
    elif mode == "nt":
        (m, k), (n, k2) = a.shape, b.shape
    else:
        (k, m), (k2, n) = a.shape, b.shape
    assert k == k2, (a.shape, b.shape, mode)
    tm = _pick_tile(m) if tm is None else min(tm, m)
    tn = _pick_tile(n) if tn is None else min(tn, n)
    if tk is None:
        tk = _pick_tile(k)
        if k == 2 * tk and a.dtype == BF16 and b.dtype == BF16:
            tk = k
    assert m % tm == 0 and n % tn == 0 and k % tk == 0, (m, n, k, tm, tn, tk)
    nk = k // tk
    dims = {"nn": ((1,), (0,)), "nt": ((1,), (1,)), "tn": ((0,), (0,))}[mode]
    has_res = residual is not None

    def body(*refs):
        refs = list(refs)
        a_ref, b_ref = refs[0], refs[1]
        r_ref = refs[2] if has_res else None
        outs = refs[2 + has_res:]
        o_ref = outs[0]
        o2_ref = outs[1] if extra_bf16 else None
        def finish(r):
            if has_res:
                r = r_ref[...] + r
            o_ref[...] = r.astype(out_dtype)
            if extra_bf16:
                o2_ref[...] = r.astype(BF16)

        if nk == 1:
            finish(_dot(a_ref[...], b_ref[...], dims))
            return
        acc = outs[-1]
        kk = pl.program_id(2)

        @pl.when(kk == 0)
        def _():
            acc[...] = jnp.zeros_like(acc)

        acc[...] += _dot(a_ref[...], b_ref[...], dims)

        @pl.when(kk == nk - 1)
        def _():
            finish(acc[...])

    if mode == "tn":
        a_spec = pl.BlockSpec((tk, tm), lambda i, j, kk: (kk, i))
    else:
        a_spec = pl.BlockSpec((tm, tk), lambda i, j, kk: (i, kk))
    if mode == "nt":
        b_spec = pl.BlockSpec((tn, tk), lambda i, j, kk: (j, kk))
    else:
        b_spec = pl.BlockSpec((tk, tn), lambda i, j, kk: (kk, j))
    o_spec = pl.BlockSpec((tm, tn), lambda i, j, kk: (i, j))
    in_specs = [a_spec, b_spec] + ([o_spec] if has_res else [])
    out_specs = [o_spec] + ([o_spec] if extra_bf16 else [])
    out_shape = [jax.ShapeDtypeStruct((m, n), out_dtype)] + ([jax.ShapeDtypeStruct((m, n), BF16)] if extra_bf16 else [])
    args = (a, b) + ((residual,) if has_res else ())
    res = pl.pallas_call(
        body, name=name, grid=(m // tm, n // tn, nk),
        in_specs=in_specs, out_specs=out_specs, out_shape=out_shape,
        scratch_shapes=[pltpu.VMEM((tm, tn) if nk > 1 else (8, 128), F32)],
        compiler_params=pltpu.CompilerParams(dimension_semantics=("parallel", "parallel", "arbitrary"),
                                             vmem_limit_bytes=MM_VMEM_LIMIT),
    )(*args)
    return res if extra_bf16 else res[0]


def _hgrn_gates(qp, fp, lb):
    sig = _sigmoid(fp)
    f = lb + (1.0 - lb) * sig
    logf = jnp.log(f)
    k = 1.0 - f
    sq = _sigmoid(qp)
    q = qp * sq
    return sig, f, logf, k, sq, q


def _hgrn_fwd(proj, lb0, lb1, norm_w, name, tb=512):
    s = proj.shape[0]
    tb = min(tb, s)
    nb, ncb = s // tb, tb // CHUNK

    def body(q_ref, f_ref, i_ref, g_ref, a0_ref, a1_ref, nw_ref, o_ref, og_ref, st_ref, state):
        @pl.when(pl.program_id(1) == 0)
        def _():
            state[...] = jnp.zeros_like(state)

        lb = _sigmoid(a0_ref[...] - a1_ref[...])
        row = lax.broadcasted_iota(jnp.int32, (CHUNK, CHUNK), 0)
        col = lax.broadcasted_iota(jnp.int32, (CHUNK, CHUNK), 1)
        tril = row >= col
        ones_l = tril.astype(BF16)
        nw = nw_ref[...]

        def chunk(c, carry):
            rows = pl.ds(pl.multiple_of(c * CHUNK, CHUNK), CHUNK)
            v = i_ref[rows, :]
            _, _, logf, k, _, q = _hgrn_gates(q_ref[rows, :], f_ref[rows, :], lb)
            b = _exact_ones_dot(ones_l, logf)
            bl = jnp.sum(logf, axis=0, keepdims=True)
            bm = 0.5 * bl
            st = state[...]
            st_ref[0, c] = st
            qt = q * jnp.exp(b - bm)
            kt = k * jnp.exp(bm - b)
            a = jnp.where(tril, _hdot_nt(qt, kt), 0.0)
            o = _hdot_nt(q * jnp.exp(b), st) + _hdot_nn(a, v)
            state[...] = st * jnp.exp(bl) + _hdot_tn(v, k * jnp.exp(bl - b))
            o_ref[rows, :] = o
            on = (o * lax.rsqrt(jnp.mean(o * o, axis=-1, keepdims=True) + EPS)) * nw
            gv = g_ref[rows, :]
            og_ref[rows, :] = (on * (gv * _sigmoid(gv))).astype(BF16)
            return carry

        lax.fori_loop(0, ncb, chunk, 0, unroll=HGRN_UNROLL)

    def colblk(group):
        return pl.BlockSpec((tb, HEAD), lambda h, j: (j, group * N_HEADS + h))

    vec = pl.BlockSpec((1, HEAD), lambda h, j: (0, h))
    out_blk = pl.BlockSpec((tb, HEAD), lambda h, j: (j, h))
    return pl.pallas_call(
        body, name=name, grid=(N_HEADS, nb),
        in_specs=[colblk(0), colblk(1), colblk(2), colblk(3), vec, vec, pl.BlockSpec((1, HEAD), lambda h, j: (0, 0))],
        out_specs=[out_blk, out_blk, pl.BlockSpec((1, ncb, HEAD, HEAD), lambda h, j: (h, j, 0, 0))],
        out_shape=[jax.ShapeDtypeStruct((s, HGRN_W), F32), jax.ShapeDtypeStruct((s, HGRN_W), BF16),
                   jax.ShapeDtypeStruct((N_HEADS, s // CHUNK, HEAD, HEAD), F32)],
        scratch_shapes=[pltpu.VMEM((HEAD, HEAD), F32)],
        compiler_params=_params("parallel", "arbitrary"),
    )(proj, proj, proj, proj, lb0, lb1, norm_w)


def _hgrn_bwd(proj, lb0, lb1, norm_w, o, states, dmix, name, tb=512):
    s = proj.shape[0]
    tb = min(tb, s)
    nb, ncb = s // tb, tb // CHUNK

    def body(q_ref, f_ref, i_ref, g_ref, a0_ref, a1_ref, nw_ref, o_ref, st_ref, dm_ref,
             dq_ref, df_ref, di_ref, dg_ref, glb_ref, gnw_ref, dstate):
        h = pl.program_id(0)

        @pl.when(pl.program_id(1) == 0)
        def _():
            dstate[...] = jnp.zeros_like(dstate)
            glb_ref[...] = jnp.zeros_like(glb_ref)

        @pl.when((pl.program_id(1) == 0) & (h == 0))
        def _():
            gnw_ref[...] = jnp.zeros_like(gnw_ref)

        lb = _sigmoid(a0_ref[...] - a1_ref[...])
        row = lax.broadcasted_iota(jnp.int32, (CHUNK, CHUNK), 0)
        col = lax.broadcasted_iota(jnp.int32, (CHUNK, CHUNK), 1)
        tril = row >= col
        ones_l = tril.astype(BF16)
        ones_u = (row <= col).astype(BF16)
        nw = nw_ref[...]

        def chunk(cc, carry):
            c = ncb - 1 - cc
            rows = pl.ds(pl.multiple_of(c * CHUNK, CHUNK), CHUNK)
            qp = q_ref[rows, :]
            v = i_ref[rows, :]
            sig, f, logf, k, sq, q = _hgrn_gates(qp, f_ref[rows, :], lb)
            gv = g_ref[rows, :]
            sg = _sigmoid(gv)
            silu_g = gv * sg
            dog = dm_ref[rows, :]
            ov = o_ref[rows, :]
            r = lax.rsqrt(jnp.mean(ov * ov, axis=-1, keepdims=True) + EPS)
            ohat = ov * r
            on = ohat * nw
            dg_ref[rows, :] = (dog * on * (sg * (1.0 + gv * (1.0 - sg)))).astype(BF16)
            don = dog * silu_g
            gnw_ref[...] += _rows8(don * ohat)
            doh = don * nw
            do = r * (doh - ohat * jnp.mean(doh * ohat, axis=-1, keepdims=True))
            b = _exact_ones_dot(ones_l, logf)
            bl = jnp.sum(logf, axis=0, keepdims=True)
            bm = 0.5 * bl
            e_q = jnp.exp(b - bm)
            e_k = jnp.exp(bm - b)
            e_b = jnp.exp(b)
            e_l = jnp.exp(bl - b)
            qt, kt, qb, kb = q * e_q, k * e_k, q * e_b, k * e_l
            st0 = st_ref[0, c]
            dst = dstate[...]
            a = jnp.where(tril, _hdot_nt(qt, kt), 0.0)
            da = jnp.where(tril, _hdot_nt(do, v), 0.0)
            dq = _hdot_nn(da, kt) * e_q + _hdot_nn(do, st0) * e_b
            dkb = _hdot_nn(v, dst) * e_l
            dk = _hdot_tn(da, qt) * e_k + dkb
            dv = _hdot_tn(a, do) + _hdot_nt(kb, dst)
            e_bl = jnp.exp(bl)
            dstate[...] = dst * e_bl + _hdot_tn(do, qb)
            db = q * dq - k * dk
            db_last = jnp.sum(k * dkb, axis=0, keepdims=True) + e_bl * jnp.sum(st0 * dst, axis=0, keepdims=True)
            dlogf = _exact_ones_dot(ones_u, db) + db_last
            dfg = dlogf / f - dk
            df_ref[rows, :] = (dfg * (1.0 - lb) * (sig * (1.0 - sig))).astype(BF16)
            glb_ref[...] += _rows8(dfg * (1.0 - sig)) * (lb * (1.0 - lb))
            dq_ref[rows, :] = (dq * (sq * (1.0 + qp * (1.0 - sq)))).astype(BF16)
            di_ref[rows, :] = dv.astype(BF16)
            return carry

        lax.fori_loop(0, ncb, chunk, 0, unroll=HGRN_UNROLL)

    def colblk(group):
        return pl.BlockSpec((tb, HEAD), lambda h, j: (nb - 1 - j, group * N_HEADS + h))

    vec = pl.BlockSpec((1, HEAD), lambda h, j: (0, h))
    blk = pl.BlockSpec((tb, HEAD), lambda h, j: (nb - 1 - j, h))
    grad = jax.ShapeDtypeStruct((s, HGRN_W), BF16)
    return pl.pallas_call(
        body, name=name, grid=(N_HEADS, nb),
        in_specs=[colblk(0), colblk(1), colblk(2), colblk(3), vec, vec, pl.BlockSpec((1, HEAD), lambda h, j: (0, 0)),
                  blk, pl.BlockSpec((1, ncb, HEAD, HEAD), lambda h, j: (h, nb - 1 - j, 0, 0)), blk],
        out_specs=[blk, blk, blk, blk, pl.BlockSpec((8, HEAD), lambda h, j: (0, h)),
                   pl.BlockSpec((8, HEAD), lambda h, j: (0, 0))],
        out_shape=[grad, grad, grad, grad, jax.ShapeDtypeStruct((8, HGRN_W), F32), jax.ShapeDtypeStruct((8, HEAD), F32)],
        scratch_shapes=[pltpu.VMEM((HEAD, HEAD), F32)],
        compiler_params=_params("arbitrary", "arbitrary"),
    )(proj, proj, proj, proj, lb0, lb1, norm_w, o, states, dmix)


def _conv3(x0, x1, x2, w_ref):
    y = x0 * w_ref[0:1, :]
    y = y + x1 * w_ref[1:2, :]
    return y + x2 * w_ref[2:3, :]


def _sconv_fwd(proj, w8, name, tb=256):
    s = proj.shape[0]
    tb = min(tb, s)
    hb = tb // HALO

    def body(cb_ref, cc_ref, ch_ref, cch_ref, chh_ref, w_ref, y_ref):
        first = pl.program_id(0) == 0
        u = cc_ref[...] * ch_ref[...]
        uh = jnp.where(first, 0.0, cch_ref[...] * chh_ref[...])
        conv = _conv3(_shift_down(u, uh, 2), _shift_down(u, uh, 1), u, w_ref)
        y_ref[...] = (cb_ref[...] * conv).astype(BF16)

    def blk(g):
        return pl.BlockSpec((tb, HGRN_W), lambda j: (j, g))

    def halo(g):
        return pl.BlockSpec((HALO, HGRN_W), lambda j: (jnp.maximum(j * hb - 1, 0), g))

    return pl.pallas_call(
        body, name=name, grid=(s // tb,),
        in_specs=[blk(4), blk(5), blk(6), halo(5), halo(6), pl.BlockSpec((HALO, HGRN_W), lambda j: (0, 0))],
        out_specs=pl.BlockSpec((tb, HGRN_W), lambda j: (j, 0)),
        out_shape=jax.ShapeDtypeStruct((s, HGRN_W), BF16),
        compiler_params=_params("parallel"),
    )(proj, proj, proj, proj, proj, w8)


def _sconv_bwd(proj, w8, dmix, name, tb=256):
    s = proj.shape[0]
    tb = min(tb, s)
    hb = tb // HALO
    nb = s // tb
    last_h = s // HALO - 1

    def body(cb_ref, cc_ref, ch_ref, cch_ref, chh_ref, cbn_ref, dy_ref, dyn_ref, w_ref,
             dcb_ref, dcc_ref, dch_ref, gw_ref):
        j = pl.program_id(0)

        @pl.when(j == 0)
        def _():
            gw_ref[...] = jnp.zeros_like(gw_ref)

        cc, ch, cb = cc_ref[...], ch_ref[...], cb_ref[...]
        u = cc * ch
        uh = jnp.where(j == 0, 0.0, cch_ref[...] * chh_ref[...])
        u2, u1 = _shift_down(u, uh, 2), _shift_down(u, uh, 1)
        conv = _conv3(u2, u1, u, w_ref)
        dy = dy_ref[...]
        dcb_ref[...] = (dy * conv).astype(BF16)
        dc = dy * cb
        dcn = jnp.where(j == nb - 1, 0.0, dyn_ref[...] * cbn_ref[...])
        gw_ref[0:8, :] += _rows8(dc * u2)
        gw_ref[8:16, :] += _rows8(dc * u1)
        gw_ref[16:24, :] += _rows8(dc * u)
        du = dc * w_ref[2:3, :] + _shift_up(dc, dcn, 1) * w_ref[1:2, :] + _shift_up(dc, dcn, 2) * w_ref[0:1, :]
        dcc_ref[...] = (du * ch).astype(BF16)
        dch_ref[...] = (du * cc).astype(BF16)

    def blk(g):
        return pl.BlockSpec((tb, HGRN_W), lambda j: (j, g))

    def halo_prev(g):
        return pl.BlockSpec((HALO, HGRN_W), lambda j: (jnp.maximum(j * hb - 1, 0), g))

    def halo_next(g):
        return pl.BlockSpec((HALO, HGRN_W), lambda j: (jnp.minimum((j + 1) * hb, last_h), g))

    out = pl.BlockSpec((tb, HGRN_W), lambda j: (j, 0))
    grad = jax.ShapeDtypeStruct((s, HGRN_W), BF16)
    return pl.pallas_call(
        body, name=name, grid=(nb,),
        in_specs=[blk(4), blk(5), blk(6), halo_prev(5), halo_prev(6), halo_next(4), blk(1), halo_next(1),
                  pl.BlockSpec((HALO, HGRN_W), lambda j: (0, 0))],
        out_specs=[out, out, out, pl.BlockSpec((24, HGRN_W), lambda j: (0, 0))],
        out_shape=[grad, grad, grad, jax.ShapeDtypeStruct((24, HGRN_W), F32)],
        compiler_params=_params("arbitrary"),
    )(proj, proj, proj, proj, proj, proj, dmix, dmix, w8)


def _attn_fwd(q, kk, vv, name, tb=256):
    s, d = q.shape
    m = kk.shape[0]
    tb = min(tb, s)
    scale = MEM_HEAD_DIM ** -0.5

    def body(q_ref, k_ref, v_ref, o_ref):
        for hh in range(MEM_HEADS):
            cols = slice(hh * MEM_HEAD_DIM, (hh + 1) * MEM_HEAD_DIM)
            sc = _dot_nt(q_ref[:, cols], k_ref[:, cols]) * scale
            sc = sc - jnp.max(sc, axis=-1, keepdims=True)
            e = jnp.exp(sc)
            p = e / jnp.sum(e, axis=-1, keepdims=True)
            o_ref[:, cols] = _dot_nn(p, v_ref[:, cols]).astype(BF16)

    full = pl.BlockSpec((m, d), lambda i: (0, 0))
    return pl.pallas_call(
        body, name=name, grid=(s // tb,),
        in_specs=[pl.BlockSpec((tb, d), lambda i: (i, 0)), full, full],
        out_specs=pl.BlockSpec((tb, d), lambda i: (i, 0)),
        out_shape=jax.ShapeDtypeStruct((s, d), BF16),
        compiler_params=_params("parallel"),
    )(q, kk, vv)


def _attn_bwd(q, kk, vv, datt, name, tb=256):
    s, d = q.shape
    m = kk.shape[0]
    tb = min(tb, s)
    scale = MEM_HEAD_DIM ** -0.5

    def body(q_ref, k_ref, v_ref, do_ref, dq_ref, dk_ref, dv_ref):
        @pl.when(pl.program_id(0) == 0)
        def _():
            dk_ref[...] = jnp.zeros_like(dk_ref)
            dv_ref[...] = jnp.zeros_like(dv_ref)

        for hh in range(MEM_HEADS):
            cols = slice(hh * MEM_HEAD_DIM, (hh + 1) * MEM_HEAD_DIM)
            qh, kh, vh, doh = q_ref[:, cols], k_ref[:, cols], v_ref[:, cols], do_ref[:, cols]
            sc = _dot_nt(qh, kh) * scale
            sc = sc - jnp.max(sc, axis=-1, keepdims=True)
            e = jnp.exp(sc)
            p = e / jnp.sum(e, axis=-1, keepdims=True)
            dp = _dot_nt(doh, vh)
            ds = p * (dp - jnp.sum(dp * p, axis=-1, keepdims=True)) * scale
            dq_ref[:, cols] = _dot_nn(ds, kh).astype(BF16)
            dk_ref[:, cols] += _dot_tn(ds, qh)
            dv_ref[:, cols] += _dot_tn(p, doh)

    full = pl.BlockSpec((m, d), lambda i: (0, 0))
    row = pl.BlockSpec((tb, d), lambda i: (i, 0))
    return pl.pallas_call(
        body, name=name, grid=(s // tb,),
        in_specs=[row, full, full, row],
        out_specs=[row, full, full],
        out_shape=[jax.ShapeDtypeStruct((s, d), BF16), jax.ShapeDtypeStruct((m, d), F32),
                   jax.ShapeDtypeStruct((m, d), F32)],
        compiler_params=_params("arbitrary"),
    )(q, kk, vv, datt)


def _ffn_fwd(g, u, w8, bias, name, tb=256, tc=512):
    s, f = g.shape
    tb = min(tb, s)
    hb = tb // HALO

    def body(g_ref, gh_ref, u_ref, w_ref, b_ref, z_ref):
        gv = g_ref[...]
        gh = jnp.where(pl.program_id(1) == 0, 0.0, gh_ref[...])
        a = _conv3(_shift_down(gv, gh, 2), _shift_down(gv, gh, 1), gv, w_ref) + b_ref[...]
        z_ref[...] = ((a * _sigmoid(a)) * u_ref[...]).astype(BF16)

    blk = pl.BlockSpec((tb, tc), lambda c, j: (j, c))
    return pl.pallas_call(
        body, name=name, grid=(f // tc, s // tb),
        in_specs=[blk, pl.BlockSpec((HALO, tc), lambda c, j: (jnp.maximum(j * hb - 1, 0), c)), blk,
                  pl.BlockSpec((HALO, tc), lambda c, j: (0, c)), pl.BlockSpec((1, tc), lambda c, j: (0, c))],
        out_specs=blk,
        out_shape=jax.ShapeDtypeStruct((s, f), BF16),
        compiler_params=_params("parallel", "parallel"),
    )(g, g, u, w8, bias)


def _ffn_bwd_a(g, u, dz, w8, bias, name, tb=256, tc=512):
    s, f = g.shape
    tb = min(tb, s)
    hb = tb // HALO

    def body(g_ref, gh_ref, u_ref, dz_ref, w_ref, b_ref, da_ref, du_ref, gb_ref, gw_ref):
        j = pl.program_id(1)

        @pl.when(j == 0)
        def _():
            gb_ref[...] = jnp.zeros_like(gb_ref)
            gw_ref[...] = jnp.zeros_like(gw_ref)

        gv = g_ref[...]
        gh = jnp.where(j == 0, 0.0, gh_ref[...])
        g2, g1 = _shift_down(gv, gh, 2), _shift_down(gv, gh, 1)
        a = _conv3(g2, g1, gv, w_ref) + b_ref[...]
        sa = _sigmoid(a)
        dz = dz_ref[...]
        du_ref[...] = (dz * (a * sa)).astype(BF16)
        da = dz * u_ref[...] * (sa * (1.0 + a * (1.0 - sa)))
        da_ref[...] = da
        gb_ref[...] += _rows8(da)
        gw_ref[0:8, :] += _rows8(da * g2)
        gw_ref[8:16, :] += _rows8(da * g1)
        gw_ref[16:24, :] += _rows8(da * gv)

    blk = pl.BlockSpec((tb, tc), lambda c, j: (j, c))
    return pl.pallas_call(
        body, name=name, grid=(f // tc, s // tb),
        in_specs=[blk, pl.BlockSpec((HALO, tc), lambda c, j: (jnp.maximum(j * hb - 1, 0), c)), blk, blk,
                  pl.BlockSpec((HALO, tc), lambda c, j: (0, c)), pl.BlockSpec((1, tc), lambda c, j: (0, c))],
        out_specs=[blk, blk, pl.BlockSpec((8, tc), lambda c, j: (0, c)), pl.BlockSpec((24, tc), lambda c, j: (0, c))],
        out_shape=[jax.ShapeDtypeStruct((s, f), F32), jax.ShapeDtypeStruct((s, f), BF16),
                   jax.ShapeDtypeStruct((8, f), F32), jax.ShapeDtypeStruct((24, f), F32)],
        compiler_params=_params("parallel", "arbitrary"),
    )(g, g, u, dz, w8, bias)


def _ffn_bwd_b(da, w8, name, tb=256, tc=512):
    s, f = da.shape
    tb = min(tb, s)
    hb = tb // HALO
    nb = s // tb
    last_h = s // HALO - 1

    def body(da_ref, dan_ref, w_ref, dg_ref):
        da = da_ref[...]
        dan = jnp.where(pl.program_id(1) == nb - 1, 0.0, dan_ref[...])
        dg = da * w_ref[2:3, :] + _shift_up(da, dan, 1) * w_ref[1:2, :] + _shift_up(da, dan, 2) * w_ref[0:1, :]
        dg_ref[...] = dg.astype(BF16)

    blk = pl.BlockSpec((tb, tc), lambda c, j: (j, c))
    return pl.pallas_call(
        body, name=name, grid=(f // tc, nb),
        in_specs=[blk, pl.BlockSpec((HALO, tc), lambda c, j: (jnp.minimum((j + 1) * hb, last_h), c)),
                  pl.BlockSpec((HALO, tc), lambda c, j: (0, c))],
        out_specs=blk,
        out_shape=jax.ShapeDtypeStruct((s, f), BF16),
        compiler_params=_params("parallel", "parallel"),
    )(da, da, w8)


def _window(ref, axis, slot, size):
    start = pl.multiple_of(slot * size, size)
    if axis == 0:
        return ref.at[pl.ds(start, size), :]
    return ref.at[:, pl.ds(start, size)]


def _chip_peers():
    x, y, c = lax.axis_index("x"), lax.axis_index("y"), lax.axis_index("c")
    peers = [(1 - x, y, c), (x, 1 - y, c), (1 - x, 1 - y, c)]
    slots = [2 * (1 - x) + y, 2 * x + (1 - y), 2 * (1 - x) + (1 - y)]
    return 2 * x + y, peers, slots


HBM_SPEC = pl.BlockSpec(memory_space=pltpu.HBM)
SEM_SPEC = pl.BlockSpec(memory_space=pltpu.SEMAPHORE)
EFFECT = pltpu.SideEffectType.DATAFLOW_SIDE_EFFECTING


def _hbm(a):
    return pltpu.with_memory_space_constraint(a, pltpu.HBM)


def _cast_into_full(x, axis, slot_arr, dtype, name):
    r, c = x.shape
    tr = _row_tile(r, 256)
    nb = r // tr
    full = (r * N_CHIPS, c) if axis == 0 else (r, c * N_CHIPS)

    def body(slot_ref, x_ref, o_ref):
        o_ref[...] = x_ref[...].astype(dtype)

    if axis == 0:
        out_map = lambda i, s: (s[0] * nb + i, 0)
    else:
        out_map = lambda i, s: (i, s[0])
    return pl.pallas_call(
        body, name=name,
        grid_spec=pltpu.PrefetchScalarGridSpec(
            num_scalar_prefetch=1, grid=(nb,),
            in_specs=[pl.BlockSpec((tr, c), lambda i, s: (i, 0))],
            out_specs=pl.BlockSpec((tr, c), out_map)),
        out_shape=jax.ShapeDtypeStruct(full, dtype),
        compiler_params=_params("parallel"),
    )(slot_arr, x)


def _gather_start(fulls, axes, groups, name):
    n, ng = len(fulls), len(groups)

    def body(*refs):
        outs = refs[n:]
        sems = outs[:2 * ng]
        thru = outs[2 * ng:2 * ng + n]
        token = outs[-1]
        slot, peers, _ = _chip_peers()
        for g, members in enumerate(groups):
            for i, t in enumerate(members):
                size = thru[t].shape[axes[t]] // N_CHIPS
                mine = _window(thru[t], axes[t], slot, size)
                for k in range(3):
                    pltpu.make_async_remote_copy(
                        src_ref=mine, dst_ref=mine, send_sem=sems[2 * g].at[3 * i + k],
                        recv_sem=sems[2 * g + 1].at[3 * i + k], device_id=peers[k], device_id_type=MESH).start()
        token[...] = jnp.zeros_like(token)

    sem_shapes = []
    for members in groups:
        sem_shapes += [pltpu.SemaphoreType.DMA((3 * len(members),))] * 2
    res = pl.pallas_call(
        body, name=name,
        in_specs=[HBM_SPEC] * n,
        out_specs=[SEM_SPEC] * (2 * ng) + [HBM_SPEC] * n + [pl.BlockSpec(memory_space=pltpu.VMEM)],
        out_shape=sem_shapes + [pltpu.HBM(f.shape, f.dtype) for f in fulls] + [jax.ShapeDtypeStruct((8, 128), F32)],
        input_output_aliases={t: 2 * ng + t for t in range(n)},
        compiler_params=pltpu.CompilerParams(has_side_effects=EFFECT),
    )(*[_hbm(f) for f in fulls])
    sems = [(res[2 * g], res[2 * g + 1]) for g in range(ng)]
    return sems, list(res[2 * ng:2 * ng + n]), res[-1]


def _gather_wait(fulls, axes, sems, after, name):
    n = len(fulls)

    def body(*refs):
        send_sems, recv_sems = refs[n], refs[n + 1]
        thru = refs[n + 3:]
        slot, peers, slots = _chip_peers()
        for t in range(n):
            size = thru[t].shape[axes[t]] // N_CHIPS
            mine = _window(thru[t], axes[t], slot, size)
            for k in range(3):
                cp = pltpu.make_async_remote_copy(
                    src_ref=mine, dst_ref=_window(thru[t], axes[t], slots[k], size),
                    send_sem=send_sems.at[3 * t + k], recv_sem=recv_sems.at[3 * t + k],
                    device_id=peers[k], device_id_type=MESH)
                cp.wait_send()
                cp.wait_recv()

    return pl.pallas_call(
        body, name=name,
        in_specs=[HBM_SPEC] * n + [SEM_SPEC, SEM_SPEC, pl.BlockSpec(memory_space=pl.ANY)],
        out_specs=[HBM_SPEC] * n,
        out_shape=[pltpu.HBM(f.shape, f.dtype) for f in fulls],
        input_output_aliases={t: t for t in range(n)},
        compiler_params=pltpu.CompilerParams(has_side_effects=EFFECT),
    )(*fulls, sems[0], sems[1], after)


def _scatter_start(grads_bf16, axes, name):
    n = len(grads_bf16)

    def shard_shape(g, ax):
        return (g.shape[0] // N_CHIPS, g.shape[1]) if ax == 0 else (g.shape[0], g.shape[1] // N_CHIPS)

    shapes = [shard_shape(g, ax) for g, ax in zip(grads_bf16, axes)]

    def body(*refs):
        outs = refs[2 * n:]
        send_sems, recv_sems = outs[0], outs[1]
        gb, land = outs[2:2 + n], outs[2 + n:2 + 2 * n]
        token = outs[-1]
        _, peers, slots = _chip_peers()
        for t in range(n):
            size = shapes[t][axes[t]]
            for k in range(3):
                pltpu.make_async_remote_copy(
                    src_ref=_window(gb[t], axes[t], slots[k], size), dst_ref=land[t].at[k],
                    send_sem=send_sems.at[3 * t + k], recv_sem=recv_sems.at[3 * t + k],
                    device_id=peers[k], device_id_type=MESH).start()
        token[...] = jnp.zeros_like(token)

    lands = [_hbm(lax.empty((3,) + sh, BF16)) for sh in shapes]
    res = pl.pallas_call(
        body, name=name,
        in_specs=[HBM_SPEC] * (2 * n),
        out_specs=[SEM_SPEC, SEM_SPEC] + [HBM_SPEC] * (2 * n) + [pl.BlockSpec(memory_space=pltpu.VMEM)],
        out_shape=[pltpu.SemaphoreType.DMA((3 * n,)), pltpu.SemaphoreType.DMA((3 * n,))]
        + [pltpu.HBM(g.shape, g.dtype) for g in grads_bf16] + [pltpu.HBM((3,) + sh, BF16) for sh in shapes]
        + [jax.ShapeDtypeStruct((8, 128), F32)],
        input_output_aliases={t: 2 + t for t in range(2 * n)},
        compiler_params=pltpu.CompilerParams(has_side_effects=EFFECT),
    )(*[_hbm(g) for g in grads_bf16], *lands)
    return (res[0], res[1]), list(res[2:2 + n]), list(res[2 + n:2 + 2 * n]), res[-1]


def _scatter_wait(grads_thru, lands_thru, axes, sems, after, name):
    n = len(grads_thru)

    def body(*refs):
        send_sems, recv_sems = refs[2 * n], refs[2 * n + 1]
        outs = refs[2 * n + 3:]
        gb, land = outs[:n], outs[n:]
        _, peers, slots = _chip_peers()
        for t in range(n):
            size = land[t].shape[1 + axes[t]]
            for k in range(3):
                cp = pltpu.make_async_remote_copy(
                    src_ref=_window(gb[t], axes[t], slots[k], size), dst_ref=land[t].at[k],
                    send_sem=send_sems.at[3 * t + k], recv_sem=recv_sems.at[3 * t + k],
                    device_id=peers[k], device_id_type=MESH)
                cp.wait_send()
                cp.wait_recv()

    res = pl.pallas_call(
        body, name=name,
        in_specs=[HBM_SPEC] * (2 * n) + [SEM_SPEC, SEM_SPEC, pl.BlockSpec(memory_space=pl.ANY)],
        out_specs=[HBM_SPEC] * (2 * n),
        out_shape=[pltpu.HBM(g.shape, g.dtype) for g in grads_thru] + [pltpu.HBM(l.shape, l.dtype) for l in lands_thru],
        input_output_aliases={t: t for t in range(2 * n)},
        compiler_params=pltpu.CompilerParams(has_side_effects=EFFECT),
    )(*grads_thru, *lands_thru, sems[0], sems[1], after)
    return list(res[n:])


def _sibling_exchange(arrs, name):
    n = len(arrs)
    any_spec = pl.BlockSpec(memory_space=pl.ANY)

    def body(*refs):
        ins, outs = refs[:n], refs[n:2 * n]
        send_sems, recv_sems = refs[2 * n:]
        sibling = (lax.axis_index("x"), lax.axis_index("y"), 1 - lax.axis_index("c"))
        copies = []
        for t in range(n):
            rc = pltpu.make_async_remote_copy(
                src_ref=ins[t], dst_ref=outs[t], send_sem=send_sems.at[t], recv_sem=recv_sems.at[t],
                device_id=sibling, device_id_type=MESH)
            rc.start()
            copies.append(rc)
        for rc in copies:
            rc.wait_recv()
        for rc in copies:
            rc.wait_send()

    return pl.pallas_call(
        body, name=name,
        in_specs=[any_spec] * n, out_specs=[any_spec] * n,
        out_shape=[jax.ShapeDtypeStruct(a.shape, a.dtype) for a in arrs],
        scratch_shapes=[pltpu.SemaphoreType.DMA((n,)), pltpu.SemaphoreType.DMA((n,))],
    )(*arrs)


def _all_reduce_small(packed, name):
    nc = packed.shape[1]
    vmem = pl.BlockSpec(memory_space=pltpu.VMEM)

    def body(in_ref, out_ref, gbuf, send_sems, recv_sems):
        x, y, c = lax.axis_index("x"), lax.axis_index("y"), lax.axis_index("c")
        me = 4 * x + 2 * y + c
        gbuf[me] = jnp.sum(in_ref[...], axis=0, keepdims=True)
        copies = []
        for k in range(1, 8):
            peer = (x ^ ((k >> 2) & 1), y ^ ((k >> 1) & 1), c ^ (k & 1))
            rc = pltpu.make_async_remote_copy(
                src_ref=gbuf.at[me], dst_ref=gbuf.at[me], send_sem=send_sems.at[k - 1], recv_sem=recv_sems.at[k - 1],
                device_id=peer, device_id_type=MESH)
            rc.start()
            copies.append(rc)
        for k in range(1, 8):
            peer = (x ^ ((k >> 2) & 1), y ^ ((k >> 1) & 1), c ^ (k & 1))
            pltpu.make_async_remote_copy(
                src_ref=gbuf.at[me], dst_ref=gbuf.at[me ^ k], send_sem=send_sems.at[k - 1],
                recv_sem=recv_sems.at[k - 1], device_id=peer, device_id_type=MESH).wait_recv()
        for rc in copies:
            rc.wait_send()
        tot = gbuf[0]
        for d in range(1, 8):
            tot = tot + gbuf[d]
        out_ref[...] = tot

    return pl.pallas_call(
        body, name=name,
        in_specs=[vmem], out_specs=vmem,
        out_shape=jax.ShapeDtypeStruct((1, nc), F32),
        scratch_shapes=[pltpu.VMEM((8, 1, nc), F32), pltpu.SemaphoreType.DMA((7,)), pltpu.SemaphoreType.DMA((7,))],
    )(packed)


def _sum4(g_full, axis, slot_arr, recv, name):
    _, r, c = recv.shape
    tr = min(r, 128)
    nb = r // tr

    def body(slot_ref, own_ref, recv_ref, o_ref):
        acc = own_ref[...]
        for k in range(3):
            acc = acc + recv_ref[k].astype(F32)
        o_ref[...] = acc

    if axis == 0:
        own_map = lambda i, s: (s[0] * nb + i, 0)
    else:
        own_map = lambda i, s: (i, s[0])
    return pl.pallas_call(
        body, name=name,
        grid_spec=pltpu.PrefetchScalarGridSpec(
            num_scalar_prefetch=1, grid=(nb,),
            in_specs=[pl.BlockSpec((tr, c), own_map), pl.BlockSpec((3, tr, c), lambda i, s: (0, i, 0))],
            out_specs=pl.BlockSpec((tr, c), lambda i, s: (i, 0))),
        out_shape=jax.ShapeDtypeStruct((r, c), F32),
        compiler_params=_params("parallel"),
    )(slot_arr, g_full, recv)


def _adamw(w, g_parts, m, v, name):
    r, c = w.shape
    tr = r if r % 128 else 128
    npart = len(g_parts)

    def body(*refs):
        w_ref = refs[0]
        g_refs = refs[1:1 + npart]
        m_ref, v_ref, g_out, d_out, m_out, v_out = refs[1 + npart:]
        g = g_refs[0][...]
        for gr in g_refs[1:]:
            g = g + gr[...]
        mm = ADAM_B1 * m_ref[...] + (1.0 - ADAM_B1) * g
        vv = ADAM_B2 * v_ref[...] + (1.0 - ADAM_B2) * (g * g)
        m_hat = mm / (1.0 - ADAM_B1 ** ADAM_STEP)
        v_hat = vv / (1.0 - ADAM_B2 ** ADAM_STEP)
        g_out[...] = g
        d_out[...] = -ADAM_LR * (m_hat / (jnp.sqrt(v_hat) + ADAM_EPS) + ADAM_WD * w_ref[...])
        m_out[...] = mm
        v_out[...] = vv

    blk = pl.BlockSpec((tr, c), lambda i: (i, 0))
    shp = jax.ShapeDtypeStruct((r, c), F32)
    return pl.pallas_call(
        body, name=name, grid=(r // tr,),
        in_specs=[blk] * (3 + npart), out_specs=[blk] * 4, out_shape=[shp] * 4,
        compiler_params=_params("parallel"),
    )(w, *g_parts, m, v)


def _pad_rows8(w):
    return jnp.pad(w, ((0, HALO - w.shape[0]), (0, 0)))


def kernel(x, mem, hgrn_lb, norm1_w, w_in, hgrn_norm_w, sconv_w, w_out, norm2_w, mem_norm_w, wq, wk, wv, wo, norm3_w, w_gate, w_up, ffn_conv_w, ffn_conv_b, w_down, final_norm_w, loss_target, m_hgrn_lb, m_norm1_w, m_w_in, m_hgrn_norm_w, m_sconv_w, m_w_out, m_norm2_w, m_mem_norm_w, m_wq, m_wk, m_wv, m_wo, m_norm3_w, m_w_gate, m_w_up, m_ffn_conv_w, m_ffn_conv_b, m_w_down, m_final_norm_w, v_hgrn_lb, v_norm1_w, v_w_in, v_hgrn_norm_w, v_sconv_w, v_w_out, v_norm2_w, v_mem_norm_w, v_wq, v_wk, v_wv, v_wo, v_norm3_w, v_w_gate, v_w_up, v_ffn_conv_w, v_ffn_conv_b, v_w_down, v_final_norm_w):
    xs, mems, tgt = x[0], mem[0], loss_target[0]
    d = xs.shape[1]
    fnw = final_norm_w.reshape(1, d)

    big = {"w_in": (w_in[0], 1), "w_out": (w_out[0], 0), "wq": (wq[0], 0), "wk": (wk[0], 0), "wv": (wv[0], 0),
           "wo": (wo[0], 0), "w_gate": (w_gate[0], 1), "w_up": (w_up[0], 1), "w_down": (w_down[0], 0)}
    names = list(big)
    slot_arr = (2 * lax.axis_index("x") + lax.axis_index("y")).astype(jnp.int32).reshape(1)
    gnames = names + ["sconv8", "fconv8"]
    fulls = [_cast_into_full(big[n][0], big[n][1], slot_arr, BF16, "cast_" + n) for n in names]
    fulls += [_cast_into_full(_pad_rows8(sconv_w[0]), 1, slot_arr, F32, "cast_sconv_w"),
              _cast_into_full(_pad_rows8(ffn_conv_w[0]), 1, slot_arr, F32, "cast_ffn_conv_w")]
    axes = [big[n][1] for n in names] + [1, 1]
    groups = [["w_in"], ["w_out", "sconv8"], ["wq", "wk", "wv", "wo"], ["w_gate", "w_up", "fconv8", "w_down"]]
    gidx = [[gnames.index(n) for n in grp] for grp in groups]
    gsems, fulls, tok = _gather_start(fulls, axes, gidx, "gather_start")
    wf = {}

    def gather_wait(g, after):
        got = _gather_wait([fulls[t] for t in gidx[g]], [axes[t] for t in gidx[g]], gsems[g], after,
                           "gather_wait_%d" % g)
        wf.update(zip(groups[g], got))

    lb0, lb1 = hgrn_lb[0:1], hgrn_lb[1:2]

    h1 = _rmsnorm_fwd(xs, norm1_w + tok[0:1, 0:1], "norm1")
    gather_wait(0, h1)
    proj = _matmul(h1, wf["w_in"], "nn", "proj_in")
    gather_wait(1, proj)
    sconv8 = wf["sconv8"]
    o_h, og, states = _hgrn_fwd(proj, lb0, lb1, hgrn_norm_w, "hgrn_fwd")
    yc = _sconv_fwd(proj, sconv8, "sconv_fwd")
    mix = jnp.concatenate([og, yc], axis=1)
    x1 = _matmul(mix, wf["w_out"], "nn", "proj_out", residual=xs)
    gather_wait(2, x1)
    h2 = _rmsnorm_fwd(x1, norm2_w, "norm2")
    mem_n = _rmsnorm_fwd(mems, mem_norm_w, "norm_mem")
    qa = _matmul(h2, wf["wq"], "nn", "attn_q", out_dtype=BF16)
    ka = _matmul(mem_n, wf["wk"], "nn", "attn_k", out_dtype=BF16)
    va = _matmul(mem_n, wf["wv"], "nn", "attn_v", out_dtype=BF16)
    att = _attn_fwd(qa, ka, va, "attn_fwd")
    x2 = _matmul(att, wf["wo"], "nn", "attn_o", residual=x1)
    gather_wait(3, x2)
    fconv8 = wf["fconv8"]
    h3 = _rmsnorm_fwd(x2, norm3_w, "norm3")
    gate = _matmul(h3, wf["w_gate"], "nn", "ffn_gate")
    up = _matmul(h3, wf["w_up"], "nn", "ffn_up")
    z = _ffn_fwd(gate, up, fconv8, ffn_conv_b, "ffn_act")
    x3 = _matmul(z, wf["w_down"], "nn", "ffn_down", residual=x2, tk=1408)

    dx3, g_final, loss8 = _final_loss_bwd(x3, tgt, fnw, "loss_bwd")
    gw = {}
    dz = _matmul(dx3, wf["w_down"], "nt", "d_z")
    gw["w_down"] = _matmul(z, dx3, "tn", "g_w_down", extra_bf16=True, tk=1024)
    da, du, g_fb, g_fw = _ffn_bwd_a(gate, up, dz, fconv8, ffn_conv_b, "ffn_act_bwd")
    dgate = _ffn_bwd_b(da, fconv8, "ffn_conv_bwd")
    dh3 = _matmul(dgate, wf["w_gate"], "nt", "d_h3_gate", tk=1408)
    dh3 = _matmul(du, wf["w_up"], "nt", "d_h3_up", residual=dh3, tk=1408)
    gw["w_gate"] = _matmul(h3, dgate, "tn", "g_w_gate", extra_bf16=True, tk=1024)
    gw["w_up"] = _matmul(h3, du, "tn", "g_w_up", extra_bf16=True, tk=1024)
    pending = []

    def scatter_start(grp):
        sems, g_thru, lands, token = _scatter_start([gw[n][1] for n in grp], [big[n][1] for n in grp],
                                                    "scatter_start_" + grp[0])
        pending.append((grp, sems, g_thru, lands))
        return token[0:1, 0:1]

    tok1 = scatter_start(["w_down", "w_gate", "w_up"])
    dx2, g_n3 = _rmsnorm_bwd(dh3, x2, norm3_w + tok1, dx3, "norm3_bwd")
    datt = _matmul(dx2, wf["wo"], "nt", "d_att", out_dtype=BF16)
    gw["wo"] = _matmul(att, dx2, "tn", "g_wo", extra_bf16=True, tk=1024)
    dqa, dka, dva = _attn_bwd(qa, ka, va, datt, "attn_bwd")
    dh2 = _matmul(dqa, wf["wq"], "nt", "d_h2")
    gw["wq"] = _matmul(h2, dqa, "tn", "g_wq", extra_bf16=True, tk=1024)
    gw["wk"] = _matmul(mem_n, dka, "tn", "g_wk", extra_bf16=True)
    gw["wv"] = _matmul(mem_n, dva, "tn", "g_wv", extra_bf16=True)
    tok2 = scatter_start(["wo", "wq", "wk", "wv"])
    dmem_n = _matmul(dka, wf["wk"], "nt", "d_memn_k")
    dmem_n = _matmul(dva, wf["wv"], "nt", "d_memn_v", residual=dmem_n)
    _, g_nm = _rmsnorm_bwd(dmem_n, mems, mem_norm_w, None, "norm_mem_bwd")
    dx1, g_n2 = _rmsnorm_bwd(dh2, x1, norm2_w + tok2, dx2, "norm2_bwd")
    dmix = _matmul(dx1, wf["w_out"], "nt", "d_mix")
    gw["w_out"] = _matmul(mix, dx1, "tn", "g_w_out", extra_bf16=True, tk=1024)
    tok3 = scatter_start(["w_out"])
    dcb, dcc, dch, g_sw = _sconv_bwd(proj, sconv8, dmix, "sconv_bwd")
    dq, df, di, dg, g_lb, g_hn = _hgrn_bwd(proj, lb0, lb1, hgrn_norm_w + tok3, o_h, states, dmix, "hgrn_bwd")
    dproj = jnp.concatenate([dq, df, di, dg, dcb, dcc, dch], axis=1)
    gw["w_in"] = _matmul(h1, dproj, "tn", "g_w_in", extra_bf16=True, tk=1024)
    tok4 = scatter_start(["w_in"])
    dh1 = _matmul(dproj, wf["w_in"], "nt", "d_h1", tk=1024)
    dx, g_n1 = _rmsnorm_bwd(dh1, xs, norm1_w + tok4, dx1, "norm1_bwd")

    small = [g_n1, g_n2, g_n3, g_final, g_nm, g_lb, g_hn, g_fb,
             g_sw[0:8], g_sw[8:16], g_sw[16:24], g_fw[0:8], g_fw[8:16], g_fw[16:24], loss8]
    widths = [a.shape[1] for a in small]
    tot = _all_reduce_small(jnp.concatenate(small, axis=1), "all_reduce_small")
    offs = [0]
    for wd_ in widths:
        offs.append(offs[-1] + wd_)
    sm = [tot[:, offs[i]:offs[i + 1]] for i in range(len(small))]
    s_n1, s_n2, s_n3, s_final, s_nm, s_lb, s_hn, s_fb = sm[:8]
    s_sw = jnp.concatenate(sm[8:11], axis=0)
    s_fw = jnp.concatenate(sm[11:14], axis=0)
    loss = sm[14][0, 0]
    slot = 2 * lax.axis_index("x") + lax.axis_index("y")
    s_sw = lax.dynamic_slice_in_dim(s_sw, slot * (HGRN_W // N_CHIPS), HGRN_W // N_CHIPS, axis=1)
    fsh = ffn_conv_w.shape[2]
    s_fw = lax.dynamic_slice_in_dim(s_fw, slot * fsh, fsh, axis=1)
    s_lb2 = jnp.concatenate([s_lb, -s_lb], axis=0)

    recv = {}
    for grp, sems, g_thru, lands in pending:
        got = _scatter_wait(g_thru, lands, [big[n][1] for n in grp], sems, tot, "scatter_wait_" + grp[0])
        recv.update(zip(grp, got))
    core_sums = [_sum4(gw[n][0], big[n][1], slot_arr, recv[n], "core_sum_" + n) for n in names]
    sib_sums = _sibling_exchange(core_sums, "sibling_exchange")

    moments = {"hgrn_lb": (m_hgrn_lb, v_hgrn_lb), "norm1_w": (m_norm1_w, v_norm1_w), "w_in": (m_w_in, v_w_in),
               "hgrn_norm_w": (m_hgrn_norm_w, v_hgrn_norm_w), "sconv_w": (m_sconv_w, v_sconv_w),
               "w_out": (m_w_out, v_w_out), "norm2_w": (m_norm2_w, v_norm2_w),
               "mem_norm_w": (m_mem_norm_w, v_mem_norm_w), "wq": (m_wq, v_wq), "wk": (m_wk, v_wk), "wv": (m_wv, v_wv),
               "wo": (m_wo, v_wo), "norm3_w": (m_norm3_w, v_norm3_w), "w_gate": (m_w_gate, v_w_gate),
               "w_up": (m_w_up, v_w_up), "ffn_conv_w": (m_ffn_conv_w, v_ffn_conv_w),
               "ffn_conv_b": (m_ffn_conv_b, v_ffn_conv_b), "w_down": (m_w_down, v_w_down),
               "final_norm_w": (m_final_norm_w, v_final_norm_w)}
    weights = {"hgrn_lb": hgrn_lb, "norm1_w": norm1_w, "w_in": w_in, "hgrn_norm_w": hgrn_norm_w, "sconv_w": sconv_w,
               "w_out": w_out, "norm2_w": norm2_w, "mem_norm_w": mem_norm_w, "wq": wq, "wk": wk, "wv": wv, "wo": wo,
               "norm3_w": norm3_w, "w_gate": w_gate, "w_up": w_up, "ffn_conv_w": ffn_conv_w, "ffn_conv_b": ffn_conv_b,
               "w_down": w_down, "final_norm_w": final_norm_w}
    small_g = {"hgrn_lb": s_lb2, "norm1_w": s_n1, "hgrn_norm_w": s_hn, "sconv_w": s_sw, "norm2_w": s_n2,
               "mem_norm_w": s_nm, "norm3_w": s_n3, "ffn_conv_w": s_fw, "ffn_conv_b": s_fb, "final_norm_w": s_final}
    order = list(weights)
    res = {}
    for n in order:
        w_full = weights[n]
        shape = w_full.shape
        w2 = w_full.reshape((-1, shape[-1]))
        m2, v2 = (t.reshape(w2.shape) for t in moments[n])
        if n in big:
            i = names.index(n)
            parts = [core_sums[i], sib_sums[i]]
        else:
            parts = [small_g[n].reshape(w2.shape)]
        res[n] = [t.reshape(shape) for t in _adamw(w2, parts, m2, v2, "adamw_" + n)]

    return (loss, dx[None], *[res[n][0] for n in order], *[res[n][1] for n in order],
            *[res[n][2] for n in order], *[res[n][3] for n in order])
```

```python
import functools

import jax
import jax.numpy as jnp
from jax import lax
from jax.experimental import pallas as pl
from jax.experimental.pallas import tpu as pltpu

F32 = jnp.float32
BF16 = jnp.bfloat16
MESH = pl.DeviceIdType.MESH

EPS = 1e-6
HGRN_W = 1024
HEAD = 128
N_HEADS = 8
CHUNK = 64
HGRN_UNROLL = 8
MEM_HEADS = 4
MEM_HEAD_DIM = 512
N_CHIPS = 4
HALO = 8

ADAM_LR = 0.001
ADAM_B1 = 0.9
ADAM_B2 = 0.999
ADAM_EPS = 1e-08
ADAM_WD = 0.01
ADAM_STEP = 10


def _sigmoid(x):
    return 1.0 / (1.0 + jnp.exp(-x))


def _dot(a, b, dims):
    return lax.dot_general(a.astype(BF16), b.astype(BF16), (dims, ((), ())),
                           preferred_element_type=F32)


def _dot_nn(a, b):
    return _dot(a, b, ((1,), (0,)))


def _dot_nt(a, b):
    return _dot(a, b, ((1,), (1,)))


def _dot_tn(a, b):
    return _dot(a, b, ((0,), (0,)))


def _hdot(a, b, dims):
    return lax.dot_general(a, b, (dims, ((), ())), precision=lax.Precision.HIGH, preferred_element_type=F32)


def _hdot_nn(a, b):
    return _hdot(a, b, ((1,), (0,)))


def _hdot_nt(a, b):
    return _hdot(a, b, ((1,), (1,)))


def _hdot_tn(a, b):
    return _hdot(a, b, ((0,), (0,)))


def _exact_ones_dot(ones_bf16, x):
    hi = x.astype(BF16)
    r1 = x - hi.astype(F32)
    mid = r1.astype(BF16)
    lo = (r1 - mid.astype(F32)).astype(BF16)
    dims = (((1,), (0,)), ((), ()))
    return (lax.dot_general(ones_bf16, hi, dims, preferred_element_type=F32)
            + lax.dot_general(ones_bf16, mid, dims, preferred_element_type=F32)
            + lax.dot_general(ones_bf16, lo, dims, preferred_element_type=F32))


def _rows8(v):
    t, c = v.shape
    return v.reshape(t // 8, 8, c).sum(axis=0)


def _shift_down(x, halo, s):
    rolled = pltpu.roll(x, s, 0)
    hrolled = pltpu.roll(halo, s, 0)
    row = lax.broadcasted_iota(jnp.int32, hrolled.shape, 0)
    head = jnp.where(row < s, hrolled, rolled[:HALO])
    return jnp.concatenate([head, rolled[HALO:]], axis=0)


def _shift_up(x, halo, s):
    t = x.shape[0]
    rolled = pltpu.roll(x, t - s, 0)
    hrolled = pltpu.roll(halo, HALO - s, 0)
    row = lax.broadcasted_iota(jnp.int32, hrolled.shape, 0)
    tail = jnp.where(row >= HALO - s, hrolled, rolled[t - HALO:])
    return jnp.concatenate([rolled[:t - HALO], tail], axis=0)


def _params(*sem):
    return pltpu.CompilerParams(dimension_semantics=sem)


def _row_tile(r, pref):
    while r % pref:
        pref //= 2
    return pref


def _rmsnorm_fwd(x, w, name, tm=256):
    s, d = x.shape
    tm = min(tm, s)

    def body(x_ref, w_ref, o_ref):
        xv = x_ref[...]
        r = lax.rsqrt(jnp.mean(xv * xv, axis=-1, keepdims=True) + EPS)
        o_ref[...] = ((xv * r) * w_ref[...]).astype(BF16)

    return pl.pallas_call(
        body, name=name, grid=(s // tm,),
        in_specs=[pl.BlockSpec((tm, d), lambda i: (i, 0)), pl.BlockSpec((1, d), lambda i: (0, 0))],
        out_specs=pl.BlockSpec((tm, d), lambda i: (i, 0)),
        out_shape=jax.ShapeDtypeStruct((s, d), BF16),
        compiler_params=_params("parallel"),
    )(x, w)


def _rmsnorm_bwd(dh, x, w, dres, name, tm=256):
    s, d = x.shape
    tm = min(tm, s)
    has_res = dres is not None

    def body(*refs):
        if has_res:
            dh_ref, x_ref, w_ref, dres_ref, dx_ref, dxb_ref, gw_ref = refs
        else:
            dh_ref, x_ref, w_ref, dx_ref, dxb_ref, gw_ref = refs

        @pl.when(pl.program_id(0) == 0)
        def _():
            gw_ref[...] = jnp.zeros_like(gw_ref)

        xv = x_ref[...]
        dhv = dh_ref[...].astype(F32)
        r = lax.rsqrt(jnp.mean(xv * xv, axis=-1, keepdims=True) + EPS)
        xhat = xv * r
        gw_ref[...] += _rows8(dhv * xhat)
        dxh = dhv * w_ref[...]
        dx = r * (dxh - xhat * jnp.mean(dxh * xhat, axis=-1, keepdims=True))
        if has_res:
            dx = dres_ref[...] + dx
        dx_ref[...] = dx
        dxb_ref[...] = dx.astype(BF16)

    row = pl.BlockSpec((tm, d), lambda i: (i, 0))
    in_specs = [row, row, pl.BlockSpec((1, d), lambda i: (0, 0))] + ([row] if has_res else [])
    args = (dh, x, w) + ((dres,) if has_res else ())
    return pl.pallas_call(
        body, name=name, grid=(s // tm,),
        in_specs=in_specs,
        out_specs=[row, row, pl.BlockSpec((8, d), lambda i: (0, 0))],
        out_shape=[jax.ShapeDtypeStruct((s, d), F32), jax.ShapeDtypeStruct((s, d), BF16),
                   jax.ShapeDtypeStruct((8, d), F32)],
        compiler_params=_params("arbitrary"),
    )(*args)


def _final_loss_bwd(x3, target, w, name, tm=256):
    s, d = x3.shape
    tm = min(tm, s)

    def body(x_ref, t_ref, w_ref, dx_ref, dxb_ref, gw_ref, loss_ref):
        @pl.when(pl.program_id(0) == 0)
        def _():
            gw_ref[...] = jnp.zeros_like(gw_ref)
            loss_ref[...] = jnp.zeros_like(loss_ref)

        xv = x_ref[...]
        r = lax.rsqrt(jnp.mean(xv * xv, axis=-1, keepdims=True) + EPS)
        xhat = xv * r
        y = xhat * w_ref[...]
        err = y - t_ref[...]
        part = 0.5 * jnp.mean(err * err, axis=-1, keepdims=True)
        tot = jnp.sum(part, axis=0, keepdims=True)
        rr = lax.broadcasted_iota(jnp.int32, loss_ref.shape, 0)
        cc = lax.broadcasted_iota(jnp.int32, loss_ref.shape, 1)
        loss_ref[...] += jnp.where((rr == 0) & (cc == 0), tot, 0.0)
        dy = err * (1.0 / d)
        gw_ref[...] += _rows8(dy * xhat)
        dxh = dy * w_ref[...]
        dx = r * (dxh - xhat * jnp.mean(dxh * xhat, axis=-1, keepdims=True))
        dx_ref[...] = dx
        dxb_ref[...] = dx.astype(BF16)

    row = pl.BlockSpec((tm, d), lambda i: (i, 0))
    return pl.pallas_call(
        body, name=name, grid=(s // tm,),
        in_specs=[row, row, pl.BlockSpec((1, d), lambda i: (0, 0))],
        out_specs=[row, row, pl.BlockSpec((8, d), lambda i: (0, 0)), pl.BlockSpec((8, 128), lambda i: (0, 0))],
        out_shape=[jax.ShapeDtypeStruct((s, d), F32), jax.ShapeDtypeStruct((s, d), BF16),
                   jax.ShapeDtypeStruct((8, d), F32), jax.ShapeDtypeStruct((8, 128), F32)],
        compiler_params=_params("arbitrary"),
    )(x3, target, w)


MM_TILES = (1024, 1408, 512, 256, 128)
MM_VMEM_LIMIT = 48 * 1024 * 1024


def _pick_tile(dim):
    for t in MM_TILES:
        if dim % t == 0:
            return t
    return dim


def _matmul(a, b, mode, name, *, out_dtype=F32, residual=None, extra_bf16=False, tm=None, tn=None, tk=None):
    if mode == "nn":
        (m, k), (k2, n) = a.shape, b.shape
    elif mode == "nt":
        (m, k), (n, k2) = a.shape, b.shape
    else:
        (k, m), (k2, n) = a.shape, b.shape
    assert k == k2, (a.shape, b.shape, mode)
    tm = _pick_tile(m) if tm is None else min(tm, m)
    tn = _pick_tile(n) if tn is None else min(tn, n)
    if tk is None:
        tk = _pick_tile(k)
        if k == 2 * tk and a.dtype == BF16 and b.dtype == BF16:
            tk = k
    assert m % tm == 0 and n % tn == 0 and k % tk == 0, (m, n, k, tm, tn, tk)
    nk = k // tk
    dims = {"nn": ((1,), (0,)), "nt": ((1,), (1,)), "tn": ((0,), (0,))}[mode]
    has_res = residual is not None

    def body(*refs):
        refs = list(refs)
        a_ref, b_ref = refs[0], refs[1]
        r_ref = refs[2] if has_res else None
        outs = refs[2 + has_res:]
        o_ref = outs[0]
        o2_ref = outs[1] if extra_bf16 else None
        def finish(r):
            if has_res:
                r = r_ref[...] + r
            o_ref[...] = r.astype(out_dtype)
            if extra_bf16:
                o2_ref[...] = r.astype(BF16)

        if nk == 1:
            finish(_dot(a_ref[...], b_ref[...], dims))
            return
        acc = outs[-1]
        kk = pl.program_id(2)

        @pl.when(kk == 0)
        def _():
            acc[...] = jnp.zeros_like(acc)

        acc[...] += _dot(a_ref[...], b_ref[...], dims)

        @pl.when(kk == nk - 1)
        def _():
            finish(acc[...])

    if mode == "tn":
        a_spec = pl.BlockSpec((tk, tm), lambda i, j, kk: (kk, i))
    else:
        a_spec = pl.BlockSpec((tm, tk), lambda i, j, kk: (i, kk))
    if mode == "nt":
        b_spec = pl.BlockSpec((tn, tk), lambda i, j, kk: (j, kk))
    else:
        b_spec = pl.BlockSpec((tk, tn), lambda i, j, kk: (kk, j))
    o_spec = pl.BlockSpec((tm, tn), lambda i, j, kk: (i, j))
    in_specs = [a_spec, b_spec] + ([o_spec] if has_res else [])
    out_specs = [o_spec] + ([o_spec] if extra_bf16 else [])
    out_shape = [jax.ShapeDtypeStruct((m, n), out_dtype)] + ([jax.ShapeDtypeStruct((m, n), BF16)] if extra_bf16 else [])
    args = (a, b) + ((residual,) if has_res else ())
    res = pl.pallas_call(
        body, name=name, grid=(m // tm, n // tn, nk),
        in_specs=in_specs, out_specs=out_specs, out_shape=out_shape,
        scratch_shapes=[pltpu.VMEM((tm, tn) if nk > 1 else (8, 128), F32)],
        compiler_params=pltpu.CompilerParams(dimension_semantics=("parallel", "parallel", "arbitrary"),
                                             vmem_limit_bytes=MM_VMEM_LIMIT),
    )(*args)
    return res if extra_bf16 else res[0]


def _hgrn_gates(qp, fp, lb):
    sig = _sigmoid(fp)
    f = lb + (1.0 - lb) * sig
    logf = jnp.log(f)
    k = 1.0 - f
    sq = _sigmoid(qp)
    q = qp * sq
    return sig, f, logf, k, sq, q


def _hgrn_fwd(proj, lb0, lb1, norm_w, name, tb=512):
    s = proj.shape[0]
    tb = min(tb, s)
    nb, ncb = s // tb, tb // CHUNK

    def body(q_ref, f_ref, i_ref, g_ref, a0_ref, a1_ref, nw_ref, o_ref, og_ref, st_ref, state):
        @pl.when(pl.program_id(1) == 0)
        def _():
            state[...] = jnp.zeros_like(state)

        lb = _sigmoid(a0_ref[...] - a1_ref[...])
        row = lax.broadcasted_iota(jnp.int32, (CHUNK, CHUNK), 0)
        col = lax.broadcasted_iota(jnp.int32, (CHUNK, CHUNK), 1)
        tril = row >= col
        ones_l = tril.astype(BF16)
        nw = nw_ref[...]

        def chunk(c, carry):
            rows = pl.ds(pl.multiple_of(c * CHUNK, CHUNK), CHUNK)
            v = i_ref[rows, :]
            _, _, logf, k, _, q = _hgrn_gates(q_ref[rows, :], f_ref[rows, :], lb)
            b = _exact_ones_dot(ones_l, logf)
            bl = jnp.sum(logf, axis=0, keepdims=True)
            bm = 0.5 * bl
            st = state[...]
            st_ref[0, c] = st
            qt = q * jnp.exp(b - bm)
            kt = k * jnp.exp(bm - b)
            a = jnp.where(tril, _dot_nt(qt, kt), 0.0)
            o = _dot_nt(q * jnp.exp(b), st) + _dot_nn(a, v)
            state[...] = st * jnp.exp(bl) + _dot_tn(v, k * jnp.exp(bl - b))
            o_ref[rows, :] = o
            on = (o * lax.rsqrt(jnp.mean(o * o, axis=-1, keepdims=True) + EPS)) * nw
            gv = g_ref[rows, :]
            og_ref[rows, :] = (on * (gv * _sigmoid(gv))).astype(BF16)
            return carry

        lax.fori_loop(0, ncb, chunk, 0, unroll=HGRN_UNROLL)

    def colblk(group):
        return pl.BlockSpec((tb, HEAD), lambda h, j: (j, group * N_HEADS + h))

    vec = pl.BlockSpec((1, HEAD), lambda h, j: (0, h))
    out_blk = pl.BlockSpec((tb, HEAD), lambda h, j: (j, h))
    return pl.pallas_call(
        body, name=name, grid=(N_HEADS, nb),
        in_specs=[colblk(0), colblk(1), colblk(2), colblk(3), vec, vec, pl.BlockSpec((1, HEAD), lambda h, j: (0, 0))],
        out_specs=[out_blk, out_blk, pl.BlockSpec((1, ncb, HEAD, HEAD), lambda h, j: (h, j, 0, 0))],
        out_shape=[jax.ShapeDtypeStruct((s, HGRN_W), F32), jax.ShapeDtypeStruct((s, HGRN_W), BF16),
                   jax.ShapeDtypeStruct((N_HEADS, s // CHUNK, HEAD, HEAD), F32)],
        scratch_shapes=[pltpu.VMEM((HEAD, HEAD), F32)],
        compiler_params=_params("parallel", "arbitrary"),
    )(proj, proj, proj, proj, lb0, lb1, norm_w)


def _hgrn_bwd(proj, lb0, lb1, norm_w, o, states, dmix, name, tb=512):
    s = proj.shape[0]
    tb = min(tb, s)
    nb, ncb = s // tb, tb // CHUNK

    def body(q_ref, f_ref, i_ref, g_ref, a0_ref, a1_ref, nw_ref, o_ref, st_ref, dm_ref,
             dq_ref, df_ref, di_ref, dg_ref, glb_ref, gnw_ref, dstate):
        h = pl.program_id(0)

        @pl.when(pl.program_id(1) == 0)
        def _():
            dstate[...] = jnp.zeros_like(dstate)
            glb_ref[...] = jnp.zeros_like(glb_ref)

        @pl.when((pl.program_id(1) == 0) & (h == 0))
        def _():
            gnw_ref[...] = jnp.zeros_like(gnw_ref)

        lb = _sigmoid(a0_ref[...] - a1_ref[...])
        row = lax.broadcasted_iota(jnp.int32, (CHUNK, CHUNK), 0)
        col = lax.broadcasted_iota(jnp.int32, (CHUNK, CHUNK), 1)
        tril = row >= col
        ones_l = tril.astype(BF16)
        ones_u = (row <= col).astype(BF16)
        nw = nw_ref[...]

        def chunk(cc, carry):
            c = ncb - 1 - cc
            rows = pl.ds(pl.multiple_of(c * CHUNK, CHUNK), CHUNK)
            qp = q_ref[rows, :]
            v = i_ref[rows, :]
            sig, f, logf, k, sq, q = _hgrn_gates(qp, f_ref[rows, :], lb)
            gv = g_ref[rows, :]
            sg = _sigmoid(gv)
            silu_g = gv * sg
            dog = dm_ref[rows, :]
            ov = o_ref[rows, :]
            r = lax.rsqrt(jnp.mean(ov * ov, axis=-1, keepdims=True) + EPS)
            ohat = ov * r
            on = ohat * nw
            dg_ref[rows, :] = (dog * on * (sg * (1.0 + gv * (1.0 - sg)))).astype(BF16)
            don = dog * silu_g
            gnw_ref[...] += _rows8(don * ohat)
            doh = don * nw
            do = r * (doh - ohat * jnp.mean(doh * ohat, axis=-1, keepdims=True))
            b = _exact_ones_dot(ones_l, logf)
            bl = jnp.sum(logf, axis=0, keepdims=True)
            bm = 0.5 * bl
            e_q = jnp.exp(b - bm)
            e_k = jnp.exp(bm - b)
            e_b = jnp.exp(b)
            e_l = jnp.exp(bl - b)
            qt, kt, qb, kb = q * e_q, k * e_k, q * e_b, k * e_l
            st0 = st_ref[0, c]
            dst = dstate[...]
            a = jnp.where(tril, _dot_nt(qt, kt), 0.0)
            da = jnp.where(tril, _dot_nt(do, v), 0.0)
            dq = _hdot_nn(da, kt) * e_q + _hdot_nn(do, st0) * e_b
            dkb = _hdot_nn(v, dst) * e_l
            dk = _hdot_tn(da, qt) * e_k + dkb
            dv = _dot_tn(a, do) + _dot_nt(kb, dst)
            e_bl = jnp.exp(bl)
            dstate[...] = dst * e_bl + _dot_tn(do, qb)
            db = q * dq - k * dk
            db_last = jnp.sum(k * dkb, axis=0, keepdims=True) + e_bl * jnp.sum(st0 * dst, axis=0, keepdims=True)
            dlogf = _exact_ones_dot(ones_u, db) + db_last
            dfg = dlogf / f - dk
            df_ref[rows, :] = (dfg * (1.0 - lb) * (sig * (1.0 - sig))).astype(BF16)
            glb_ref[...] += _rows8(dfg * (1.0 - sig)) * (lb * (1.0 - lb))
            dq_ref[rows, :] = (dq * (sq * (1.0 + qp * (1.0 - sq)))).astype(BF16)
            di_ref[rows, :] = dv.astype(BF16)
            return carry

        lax.fori_loop(0, ncb, chunk, 0, unroll=HGRN_UNROLL)

    def colblk(group):
        return pl.BlockSpec((tb, HEAD), lambda h, j: (nb - 1 - j, group * N_HEADS + h))

    vec = pl.BlockSpec((1, HEAD), lambda h, j: (0, h))
    blk = pl.BlockSpec((tb, HEAD), lambda h, j: (nb - 1 - j, h))
    grad = jax.ShapeDtypeStruct((s, HGRN_W), BF16)
    return pl.pallas_call(
        body, name=name, grid=(N_HEADS, nb),
        in_specs=[colblk(0), colblk(1), colblk(2), colblk(3), vec, vec, pl.BlockSpec((1, HEAD), lambda h, j: (0, 0)),
                  blk, pl.BlockSpec((1, ncb, HEAD, HEAD), lambda h, j: (h, nb - 1 - j, 0, 0)), blk],
        out_specs=[blk, blk, blk, blk, pl.BlockSpec((8, HEAD), lambda h, j: (0, h)),
                   pl.BlockSpec((8, HEAD), lambda h, j: (0, 0))],
        out_shape=[grad, grad, grad, grad, jax.ShapeDtypeStruct((8, HGRN_W), F32), jax.ShapeDtypeStruct((8, HEAD), F32)],
        scratch_shapes=[pltpu.VMEM((HEAD, HEAD), F32)],
        compiler_params=_params("arbitrary", "arbitrary"),
    )(proj, proj, proj, proj, lb0, lb1, norm_w, o, states, dmix)


def _conv3(x0, x1, x2, w_ref):
    y = x0 * w_ref[0:1, :]
    y = y + x1 * w_ref[1:2, :]
    return y + x2 * w_ref[2:3, :]


def _sconv_fwd(proj, w8, name, tb=256):
    s = proj.shape[0]
    tb = min(tb, s)
    hb = tb // HALO

    def body(cb_ref, cc_ref, ch_ref, cch_ref, chh_ref, w_ref, y_ref):
        first = pl.program_id(0) == 0
        u = cc_ref[...] * ch_ref[...]
        uh = jnp.where(first, 0.0, cch_ref[...] * chh_ref[...])
        conv = _conv3(_shift_down(u, uh, 2), _shift_down(u, uh, 1), u, w_ref)
        y_ref[...] = (cb_ref[...] * conv).astype(BF16)

    def blk(g):
        return pl.BlockSpec((tb, HGRN_W), lambda j: (j, g))

    def halo(g):
        return pl.BlockSpec((HALO, HGRN_W), lambda j: (jnp.maximum(j * hb - 1, 0), g))

    return pl.pallas_call(
        body, name=name, grid=(s // tb,),
        in_specs=[blk(4), blk(5), blk(6), halo(5), halo(6), pl.BlockSpec((HALO, HGRN_W), lambda j: (0, 0))],
        out_specs=pl.BlockSpec((tb, HGRN_W), lambda j: (j, 0)),
        out_shape=jax.ShapeDtypeStruct((s, HGRN_W), BF16),
        compiler_params=_params("parallel"),
    )(proj, proj, proj, proj, proj, w8)


def _sconv_bwd(proj, w8, dmix, name, tb=256):
    s = proj.shape[0]
    tb = min(tb, s)
    hb = tb // HALO
    nb = s // tb
    last_h = s // HALO - 1

    def body(cb_ref, cc_ref, ch_ref, cch_ref, chh_ref, cbn_ref, dy_ref, dyn_ref, w_ref,
             dcb_ref, dcc_ref, dch_ref, gw_ref):
        j = pl.program_id(0)

        @pl.when(j == 0)
        def _():
            gw_ref[...] = jnp.zeros_like(gw_ref)

        cc, ch, cb = cc_ref[...], ch_ref[...], cb_ref[...]
        u = cc * ch
        uh = jnp.where(j == 0, 0.0, cch_ref[...] * chh_ref[...])
        u2, u1 = _shift_down(u, uh, 2), _shift_down(u, uh, 1)
        conv = _conv3(u2, u1, u, w_ref)
        dy = dy_ref[...]
        dcb_ref[...] = (dy * conv).astype(BF16)
        dc = dy * cb
        dcn = jnp.where(j == nb - 1, 0.0, dyn_ref[...] * cbn_ref[...])
        gw_ref[0:8, :] += _rows8(dc * u2)
        gw_ref[8:16, :] += _rows8(dc * u1)
        gw_ref[16:24, :] += _rows8(dc * u)
        du = dc * w_ref[2:3, :] + _shift_up(dc, dcn, 1) * w_ref[1:2, :] + _shift_up(dc, dcn, 2) * w_ref[0:1, :]
        dcc_ref[...] = (du * ch).astype(BF16)
        dch_ref[...] = (du * cc).astype(BF16)

    def blk(g):
        return pl.BlockSpec((tb, HGRN_W), lambda j: (j, g))

    def halo_prev(g):
        return pl.BlockSpec((HALO, HGRN_W), lambda j: (jnp.maximum(j * hb - 1, 0), g))

    def halo_next(g):
        return pl.BlockSpec((HALO, HGRN_W), lambda j: (jnp.minimum((j + 1) * hb, last_h), g))

    out = pl.BlockSpec((tb, HGRN_W), lambda j: (j, 0))
    grad = jax.ShapeDtypeStruct((s, HGRN_W), BF16)
    return pl.pallas_call(
        body, name=name, grid=(nb,),
        in_specs=[blk(4), blk(5), blk(6), halo_prev(5), halo_prev(6), halo_next(4), blk(1), halo_next(1),
                  pl.BlockSpec((HALO, HGRN_W), lambda j: (0, 0))],
        out_specs=[out, out, out, pl.BlockSpec((24, HGRN_W), lambda j: (0, 0))],
        out_shape=[grad, grad, grad, jax.ShapeDtypeStruct((24, HGRN_W), F32)],
        compiler_params=_params("arbitrary"),
    )(proj, proj, proj, proj, proj, proj, dmix, dmix, w8)


def _attn_fwd(q, kk, vv, name, tb=256):
    s, d = q.shape
    m = kk.shape[0]
    tb = min(tb, s)
    scale = MEM_HEAD_DIM ** -0.5

    def body(q_ref, k_ref, v_ref, o_ref):
        for hh in range(MEM_HEADS):
            cols = slice(hh * MEM_HEAD_DIM, (hh + 1) * MEM_HEAD_DIM)
            sc = _dot_nt(q_ref[:, cols], k_ref[:, cols]) * scale
            sc = sc - jnp.max(sc, axis=-1, keepdims=True)
            e = jnp.exp(sc)
            p = e / jnp.sum(e, axis=-1, keepdims=True)
            o_ref[:, cols] = _dot_nn(p, v_ref[:, cols]).astype(BF16)

    full = pl.BlockSpec((m, d), lambda i: (0, 0))
    return pl.pallas_call(
        body, name=name, grid=(s // tb,),
        in_specs=[pl.BlockSpec((tb, d), lambda i: (i, 0)), full, full],
        out_specs=pl.BlockSpec((tb, d), lambda i: (i, 0)),
        out_shape=jax.ShapeDtypeStruct((s, d), BF16),
        compiler_params=_params("parallel"),
    )(q, kk, vv)


def _attn_bwd(q, kk, vv, datt, name, tb=256):
    s, d = q.shape
    m = kk.shape[0]
    tb = min(tb, s)
    scale = MEM_HEAD_DIM ** -0.5

    def body(q_ref, k_ref, v_ref, do_ref, dq_ref, dk_ref, dv_ref):
        @pl.when(pl.program_id(0) == 0)
        def _():
            dk_ref[...] = jnp.zeros_like(dk_ref)
            dv_ref[...] = jnp.zeros_like(dv_ref)

        for hh in range(MEM_HEADS):
            cols = slice(hh * MEM_HEAD_DIM, (hh + 1) * MEM_HEAD_DIM)
            qh, kh, vh, doh = q_ref[:, cols], k_ref[:, cols], v_ref[:, cols], do_ref[:, cols]
            sc = _dot_nt(qh, kh) * scale
            sc = sc - jnp.max(sc, axis=-1, keepdims=True)
            e = jnp.exp(sc)
            p = e / jnp.sum(e, axis=-1, keepdims=True)
            dp = _dot_nt(doh, vh)
            ds = p * (dp - jnp.sum(dp * p, axis=-1, keepdims=True)) * scale
            dq_ref[:, cols] = _dot_nn(ds, kh).astype(BF16)
            dk_ref[:, cols] += _dot_tn(ds, qh)
            dv_ref[:, cols] += _dot_tn(p, doh)

    full = pl.BlockSpec((m, d), lambda i: (0, 0))
    row = pl.BlockSpec((tb, d), lambda i: (i, 0))
    return pl.pallas_call(
        body, name=name, grid=(s // tb,),
        in_specs=[row, full, full, row],
        out_specs=[row, full, full],
        out_shape=[jax.ShapeDtypeStruct((s, d), BF16), jax.ShapeDtypeStruct((m, d), F32),
                   jax.ShapeDtypeStruct((m, d), F32)],
        compiler_params=_params("arbitrary"),
    )(q, kk, vv, datt)


def _ffn_fwd(g, u, w8, bias, name, tb=512, tc=1408):
    s, f = g.shape
    tb = min(tb, s)
    tc = tc if f % tc == 0 else 512
    hb = tb // HALO

    def body(g_ref, gh_ref, u_ref, w_ref, b_ref, z_ref):
        gv = g_ref[...]
        gh = jnp.where(pl.program_id(1) == 0, 0.0, gh_ref[...])
        a = _conv3(_shift_down(gv, gh, 2), _shift_down(gv, gh, 1), gv, w_ref) + b_ref[...]
        z_ref[...] = ((a * _sigmoid(a)) * u_ref[...]).astype(BF16)

    blk = pl.BlockSpec((tb, tc), lambda c, j: (j, c))
    return pl.pallas_call(
        body, name=name, grid=(f // tc, s // tb),
        in_specs=[blk, pl.BlockSpec((HALO, tc), lambda c, j: (jnp.maximum(j * hb - 1, 0), c)), blk,
                  pl.BlockSpec((HALO, tc), lambda c, j: (0, c)), pl.BlockSpec((1, tc), lambda c, j: (0, c))],
        out_specs=blk,
        out_shape=jax.ShapeDtypeStruct((s, f), BF16),
        compiler_params=pltpu.CompilerParams(dimension_semantics=("parallel", "parallel"),
                                             vmem_limit_bytes=MM_VMEM_LIMIT),
    )(g, g, u, w8, bias)


def _ffn_bwd(g, u, dz, w8, bias, name, tb=256, tc=1408):
    s, f = g.shape
    tb = min(tb, s)
    tc = tc if f % tc == 0 else 512
    hb = tb // HALO
    nb = s // tb

    def body(g_ref, gh_ref, u_ref, dz_ref, w_ref, b_ref, dg_ref, du_ref, gb_ref, gw_ref, da_next):
        jj = pl.program_id(1)

        @pl.when(jj == 0)
        def _():
            gb_ref[...] = jnp.zeros_like(gb_ref)
            gw_ref[...] = jnp.zeros_like(gw_ref)
            da_next[...] = jnp.zeros_like(da_next)

        gv = g_ref[...]
        gh = jnp.where(jj == nb - 1, 0.0, gh_ref[...])
        g2, g1 = _shift_down(gv, gh, 2), _shift_down(gv, gh, 1)
        a = _conv3(g2, g1, gv, w_ref) + b_ref[...]
        sa = _sigmoid(a)
        dz = dz_ref[...]
        du_ref[...] = (dz * (a * sa)).astype(BF16)
        da = dz * u_ref[...] * (sa * (1.0 + a * (1.0 - sa)))
        gb_ref[...] += _rows8(da)
        gw_ref[0:8, :] += _rows8(da * g2)
        gw_ref[8:16, :] += _rows8(da * g1)
        gw_ref[16:24, :] += _rows8(da * gv)
        dan = da_next[...]
        dg = da * w_ref[2:3, :] + _shift_up(da, dan, 1) * w_ref[1:2, :] + _shift_up(da, dan, 2) * w_ref[0:1, :]
        dg_ref[...] = dg.astype(BF16)
        da_next[...] = da[:HALO]

    blk = pl.BlockSpec((tb, tc), lambda c, jj: (nb - 1 - jj, c))
    return pl.pallas_call(
        body, name=name, grid=(f // tc, nb),
        in_specs=[blk, pl.BlockSpec((HALO, tc), lambda c, jj: (jnp.maximum((nb - 1 - jj) * hb - 1, 0), c)), blk, blk,
                  pl.BlockSpec((HALO, tc), lambda c, jj: (0, c)), pl.BlockSpec((1, tc), lambda c, jj: (0, c))],
        out_specs=[blk, blk, pl.BlockSpec((8, tc), lambda c, jj: (0, c)), pl.BlockSpec((24, tc), lambda c, jj: (0, c))],
        out_shape=[jax.ShapeDtypeStruct((s, f), BF16), jax.ShapeDtypeStruct((s, f), BF16),
                   jax.ShapeDtypeStruct((8, f), F32), jax.ShapeDtypeStruct((24, f), F32)],
        scratch_shapes=[pltpu.VMEM((HALO, tc), F32)],
        compiler_params=pltpu.CompilerParams(dimension_semantics=("parallel", "arbitrary"),
                                             vmem_limit_bytes=MM_VMEM_LIMIT),
    )(g, g, u, dz, w8, bias)


def _window(ref, axis, slot, size):
    start = pl.multiple_of(slot * size, size)
    if axis == 0:
        return ref.at[pl.ds(start, size), :]
    return ref.at[:, pl.ds(start, size)]


def _chip_peers():
    x, y, c = lax.axis_index("x"), lax.axis_index("y"), lax.axis_index("c")
    peers = [(1 - x, y, c), (x, 1 - y, c), (1 - x, 1 - y, c)]
    slots = [2 * (1 - x) + y, 2 * x + (1 - y), 2 * (1 - x) + (1 - y)]
    return 2 * x + y, peers, slots


HBM_SPEC = pl.BlockSpec(memory_space=pltpu.HBM)
SEM_SPEC = pl.BlockSpec(memory_space=pltpu.SEMAPHORE)
EFFECT = pltpu.SideEffectType.DATAFLOW_SIDE_EFFECTING


def _hbm(a):
    return pltpu.with_memory_space_constraint(a, pltpu.HBM)


def _cast_into_full(x, axis, slot_arr, dtype, name):
    r, c = x.shape
    tr = _row_tile(r, 256)
    nb = r // tr
    full = (r * N_CHIPS, c) if axis == 0 else (r, c * N_CHIPS)

    def body(slot_ref, x_ref, o_ref):
        o_ref[...] = x_ref[...].astype(dtype)

    if axis == 0:
        out_map = lambda i, s: (s[0] * nb + i, 0)
    else:
        out_map = lambda i, s: (i, s[0])
    return pl.pallas_call(
        body, name=name,
        grid_spec=pltpu.PrefetchScalarGridSpec(
            num_scalar_prefetch=1, grid=(nb,),
            in_specs=[pl.BlockSpec((tr, c), lambda i, s: (i, 0))],
            out_specs=pl.BlockSpec((tr, c), out_map)),
        out_shape=jax.ShapeDtypeStruct(full, dtype),
        compiler_params=_params("parallel"),
    )(slot_arr, x)


def _gather_start(fulls, axes, groups, name):
    n, ng = len(fulls), len(groups)

    def body(*refs):
        outs = refs[n:]
        sems = outs[:2 * ng]
        thru = outs[2 * ng:2 * ng + n]
        token = outs[-1]
        slot, peers, _ = _chip_peers()
        for g, members in enumerate(groups):
            for i, t in enumerate(members):
                size = thru[t].shape[axes[t]] // N_CHIPS
                mine = _window(thru[t], axes[t], slot, size)
                for k in range(3):
                    pltpu.make_async_remote_copy(
                        src_ref=mine, dst_ref=mine, send_sem=sems[2 * g].at[3 * i + k],
                        recv_sem=sems[2 * g + 1].at[3 * i + k], device_id=peers[k], device_id_type=MESH).start()
        token[...] = jnp.zeros_like(token)

    sem_shapes = []
    for members in groups:
        sem_shapes += [pltpu.SemaphoreType.DMA((3 * len(members),))] * 2
    res = pl.pallas_call(
        body, name=name,
        in_specs=[HBM_SPEC] * n,
        out_specs=[SEM_SPEC] * (2 * ng) + [HBM_SPEC] * n + [pl.BlockSpec(memory_space=pltpu.VMEM)],
        out_shape=sem_shapes + [pltpu.HBM(f.shape, f.dtype) for f in fulls] + [jax.ShapeDtypeStruct((8, 128), F32)],
        input_output_aliases={t: 2 * ng + t for t in range(n)},
        compiler_params=pltpu.CompilerParams(has_side_effects=EFFECT),
    )(*[_hbm(f) for f in fulls])
    sems = [(res[2 * g], res[2 * g + 1]) for g in range(ng)]
    return sems, list(res[2 * ng:2 * ng + n]), res[-1]


def _gather_wait(fulls, axes, sems, after, name):
    n = len(fulls)

    def body(*refs):
        send_sems, recv_sems = refs[n], refs[n + 1]
        thru = refs[n + 3:]
        slot, peers, slots = _chip_peers()
        for t in range(n):
            size = thru[t].shape[axes[t]] // N_CHIPS
            mine = _window(thru[t], axes[t], slot, size)
            for k in range(3):
                cp = pltpu.make_async_remote_copy(
                    src_ref=mine, dst_ref=_window(thru[t], axes[t], slots[k], size),
                    send_sem=send_sems.at[3 * t + k], recv_sem=recv_sems.at[3 * t + k],
                    device_id=peers[k], device_id_type=MESH)
                cp.wait_send()
                cp.wait_recv()

    return pl.pallas_call(
        body, name=name,
        in_specs=[HBM_SPEC] * n + [SEM_SPEC, SEM_SPEC, pl.BlockSpec(memory_space=pl.ANY)],
        out_specs=[HBM_SPEC] * n,
        out_shape=[pltpu.HBM(f.shape, f.dtype) for f in fulls],
        input_output_aliases={t: t for t in range(n)},
        compiler_params=pltpu.CompilerParams(has_side_effects=EFFECT),
    )(*fulls, sems[0], sems[1], after)


def _scatter_start(grads_bf16, axes, name):
    n = len(grads_bf16)

    def shard_shape(g, ax):
        return (g.shape[0] // N_CHIPS, g.shape[1]) if ax == 0 else (g.shape[0], g.shape[1] // N_CHIPS)

    shapes = [shard_shape(g, ax) for g, ax in zip(grads_bf16, axes)]

    def body(*refs):
        outs = refs[2 * n:]
        send_sems, recv_sems = outs[0], outs[1]
        gb, land = outs[2:2 + n], outs[2 + n:2 + 2 * n]
        token = outs[-1]
        _, peers, slots = _chip_peers()
        for t in range(n):
            size = shapes[t][axes[t]]
            for k in range(3):
                pltpu.make_async_remote_copy(
                    src_ref=_window(gb[t], axes[t], slots[k], size), dst_ref=land[t].at[k],
                    send_sem=send_sems.at[3 * t + k], recv_sem=recv_sems.at[3 * t + k],
                    device_id=peers[k], device_id_type=MESH).start()
        token[...] = jnp.zeros_like(token)

    lands = [_hbm(lax.empty((3,) + sh, BF16)) for sh in shapes]
    res = pl.pallas_call(
        body, name=name,
        in_specs=[HBM_SPEC] * (2 * n),
        out_specs=[SEM_SPEC, SEM_SPEC] + [HBM_SPEC] * (2 * n) + [pl.BlockSpec(memory_space=pltpu.VMEM)],
        out_shape=[pltpu.SemaphoreType.DMA((3 * n,)), pltpu.SemaphoreType.DMA((3 * n,))]
        + [pltpu.HBM(g.shape, g.dtype) for g in grads_bf16] + [pltpu.HBM((3,) + sh, BF16) for sh in shapes]
        + [jax.ShapeDtypeStruct((8, 128), F32)],
        input_output_aliases={t: 2 + t for t in range(2 * n)},
        compiler_params=pltpu.CompilerParams(has_side_effects=EFFECT),
    )(*[_hbm(g) for g in grads_bf16], *lands)
    return (res[0], res[1]), list(res[2:2 + n]), list(res[2 + n:2 + 2 * n]), res[-1]


def _scatter_wait(grads_thru, lands_thru, axes, sems, after, name):
    n = len(grads_thru)

    def body(*refs):
        send_sems, recv_sems = refs[2 * n], refs[2 * n + 1]
        outs = refs[2 * n + 3:]
        gb, land = outs[:n], outs[n:]
        _, peers, slots = _chip_peers()
        for t in range(n):
            size = land[t].shape[1 + axes[t]]
            for k in range(3):
                cp = pltpu.make_async_remote_copy(
                    src_ref=_window(gb[t], axes[t], slots[k], size), dst_ref=land[t].at[k],
                    send_sem=send_sems.at[3 * t + k], recv_sem=recv_sems.at[3 * t + k],
                    device_id=peers[k], device_id_type=MESH)
                cp.wait_send()
                cp.wait_recv()

    res = pl.pallas_call(
        body, name=name,
        in_specs=[HBM_SPEC] * (2 * n) + [SEM_SPEC, SEM_SPEC, pl.BlockSpec(memory_space=pl.ANY)],
        out_specs=[HBM_SPEC] * (2 * n),
        out_shape=[pltpu.HBM(g.shape, g.dtype) for g in grads_thru] + [pltpu.HBM(l.shape, l.dtype) for l in lands_thru],
        input_output_aliases={t: t for t in range(2 * n)},
        compiler_params=pltpu.CompilerParams(has_side_effects=EFFECT),
    )(*grads_thru, *lands_thru, sems[0], sems[1], after)
    return list(res[n:])


def _sibling_exchange(arrs, name):
    n = len(arrs)
    any_spec = pl.BlockSpec(memory_space=pl.ANY)

    def body(*refs):
        ins, outs = refs[:n], refs[n:2 * n]
        send_sems, recv_sems = refs[2 * n:]
        sibling = (lax.axis_index("x"), lax.axis_index("y"), 1 - lax.axis_index("c"))
        copies = []
        for t in range(n):
            rc = pltpu.make_async_remote_copy(
                src_ref=ins[t], dst_ref=outs[t], send_sem=send_sems.at[t], recv_sem=recv_sems.at[t],
                device_id=sibling, device_id_type=MESH)
            rc.start()
            copies.append(rc)
        for rc in copies:
            rc.wait_recv()
        for rc in copies:
            rc.wait_send()

    return pl.pallas_call(
        body, name=name,
        in_specs=[any_spec] * n, out_specs=[any_spec] * n,
        out_shape=[jax.ShapeDtypeStruct(a.shape, a.dtype) for a in arrs],
        scratch_shapes=[pltpu.SemaphoreType.DMA((n,)), pltpu.SemaphoreType.DMA((n,))],
    )(*arrs)


def _all_reduce_small(packed, name):
    nc = packed.shape[1]
    vmem = pl.BlockSpec(memory_space=pltpu.VMEM)

    def body(in_ref, out_ref, gbuf, send_sems, recv_sems):
        x, y, c = lax.axis_index("x"), lax.axis_index("y"), lax.axis_index("c")
        me = 4 * x + 2 * y + c
        gbuf[me] = jnp.sum(in_ref[...], axis=0, keepdims=True)
        copies = []
        for k in range(1, 8):
            peer = (x ^ ((k >> 2) & 1), y ^ ((k >> 1) & 1), c ^ (k & 1))
            rc = pltpu.make_async_remote_copy(
                src_ref=gbuf.at[me], dst_ref=gbuf.at[me], send_sem=send_sems.at[k - 1], recv_sem=recv_sems.at[k - 1],
                device_id=peer, device_id_type=MESH)
            rc.start()
            copies.append(rc)
        for k in range(1, 8):
            peer = (x ^ ((k >> 2) & 1), y ^ ((k >> 1) & 1), c ^ (k & 1))
            pltpu.make_async_remote_copy(
                src_ref=gbuf.at[me], dst_ref=gbuf.at[me ^ k], send_sem=send_sems.at[k - 1],
                recv_sem=recv_sems.at[k - 1], device_id=peer, device_id_type=MESH).wait_recv()
        for rc in copies:
            rc.wait_send()
        tot = gbuf[0]
        for d in range(1, 8):
            tot = tot + gbuf[d]
        out_ref[...] = tot

    return pl.pallas_call(
        body, name=name,
        in_specs=[vmem], out_specs=vmem,
        out_shape=jax.ShapeDtypeStruct((1, nc), F32),
        scratch_shapes=[pltpu.VMEM((8, 1, nc), F32), pltpu.SemaphoreType.DMA((7,)), pltpu.SemaphoreType.DMA((7,))],
    )(packed)


def _sum4(g_full, axis, slot_arr, recv, name):
    _, r, c = recv.shape
    tr = min(r, 128)
    nb = r // tr

    def body(slot_ref, own_ref, recv_ref, o_ref):
        acc = own_ref[...]
        for k in range(3):
            acc = acc + recv_ref[k].astype(F32)
        o_ref[...] = acc

    if axis == 0:
        own_map = lambda i, s: (s[0] * nb + i, 0)
    else:
        own_map = lambda i, s: (i, s[0])
    return pl.pallas_call(
        body, name=name,
        grid_spec=pltpu.PrefetchScalarGridSpec(
            num_scalar_prefetch=1, grid=(nb,),
            in_specs=[pl.BlockSpec((tr, c), own_map), pl.BlockSpec((3, tr, c), lambda i, s: (0, i, 0))],
            out_specs=pl.BlockSpec((tr, c), lambda i, s: (i, 0))),
        out_shape=jax.ShapeDtypeStruct((r, c), F32),
        compiler_params=_params("parallel"),
    )(slot_arr, g_full, recv)


def _adamw(w, g_parts, m, v, name):
    r, c = w.shape
    tr = r if r % 128 else 128
    npart = len(g_parts)

    def body(*refs):
        w_ref = refs[0]
        g_refs = refs[1:1 + npart]
        m_ref, v_ref, g_out, d_out, m_out, v_out = refs[1 + npart:]
        g = g_refs[0][...]
        for gr in g_refs[1:]:
            g = g + gr[...]
        mm = ADAM_B1 * m_ref[...] + (1.0 - ADAM_B1) * g
        vv = ADAM_B2 * v_ref[...] + (1.0 - ADAM_B2) * (g * g)
        m_hat = mm / (1.0 - ADAM_B1 ** ADAM_STEP)
        v_hat = vv / (1.0 - ADAM_B2 ** ADAM_STEP)
        g_out[...] = g
        d_out[...] = -ADAM_LR * (m_hat / (jnp.sqrt(v_hat) + ADAM_EPS) + ADAM_WD * w_ref[...])
        m_out[...] = mm
        v_out[...] = vv

    blk = pl.BlockSpec((tr, c), lambda i: (i, 0))
    shp = jax.ShapeDtypeStruct((r, c), F32)
    return pl.pallas_call(
        body, name=name, grid=(r // tr,),
        in_specs=[blk] * (3 + npart), out_specs=[blk] * 4, out_shape=[shp] * 4,
        compiler_params=_params("parallel"),
    )(w, *g_parts, m, v)


def _pad_rows8(w):
    return jnp.pad(w, ((0, HALO - w.shape[0]), (0, 0)))


def kernel(x, mem, hgrn_lb, norm1_w, w_in, hgrn_norm_w, sconv_w, w_out, norm2_w, mem_norm_w, wq, wk, wv, wo, norm3_w, w_gate, w_up, ffn_conv_w, ffn_conv_b, w_down, final_norm_w, loss_target, m_hgrn_lb, m_norm1_w, m_w_in, m_hgrn_norm_w, m_sconv_w, m_w_out, m_norm2_w, m_mem_norm_w, m_wq, m_wk, m_wv, m_wo, m_norm3_w, m_w_gate, m_w_up, m_ffn_conv_w, m_ffn_conv_b, m_w_down, m_final_norm_w, v_hgrn_lb, v_norm1_w, v_w_in, v_hgrn_norm_w, v_sconv_w, v_w_out, v_norm2_w, v_mem_norm_w, v_wq, v_wk, v_wv, v_wo, v_norm3_w, v_w_gate, v_w_up, v_ffn_conv_w, v_ffn_conv_b, v_w_down, v_final_norm_w):
    xs, mems, tgt = x[0], mem[0], loss_target[0]
    d = xs.shape[1]
    fnw = final_norm_w.reshape(1, d)

    big = {"w_in": (w_in[0], 1), "w_out": (w_out[0], 0), "wq": (wq[0], 0), "wk": (wk[0], 0), "wv": (wv[0], 0),
           "wo": (wo[0], 0), "w_gate": (w_gate[0], 1), "w_up": (w_up[0], 1), "w_down": (w_down[0], 0)}
    names = list(big)
    slot_arr = (2 * lax.axis_index("x") + lax.axis_index("y")).astype(jnp.int32).reshape(1)
    gnames = names + ["sconv8", "fconv8"]
    fulls = [_cast_into_full(big[n][0], big[n][1], slot_arr, BF16, "cast_" + n) for n in names]
    fulls += [_cast_into_full(_pad_rows8(sconv_w[0]), 1, slot_arr, F32, "cast_sconv_w"),
              _cast_into_full(_pad_rows8(ffn_conv_w[0]), 1, slot_arr, F32, "cast_ffn_conv_w")]
    axes = [big[n][1] for n in names] + [1, 1]
    groups = [["w_in"], ["w_out", "sconv8"], ["wq", "wk", "wv", "wo"], ["w_gate", "w_up", "fconv8", "w_down"]]
    gidx = [[gnames.index(n) for n in grp] for grp in groups]
    gsems, fulls, tok = _gather_start(fulls, axes, gidx, "gather_start")
    wf = {}

    def gather_wait(g, after):
        got = _gather_wait([fulls[t] for t in gidx[g]], [axes[t] for t in gidx[g]], gsems[g], after,
                           "gather_wait_%d" % g)
        wf.update(zip(groups[g], got))

    lb0, lb1 = hgrn_lb[0:1], hgrn_lb[1:2]

    h1 = _rmsnorm_fwd(xs, norm1_w + tok[0:1, 0:1], "norm1")
    gather_wait(0, h1)
    proj = _matmul(h1, wf["w_in"], "nn", "proj_in")
    gather_wait(1, proj)
    sconv8 = wf["sconv8"]
    o_h, og, states = _hgrn_fwd(proj, lb0, lb1, hgrn_norm_w, "hgrn_fwd")
    yc = _sconv_fwd(proj, sconv8, "sconv_fwd")
    mix = jnp.concatenate([og, yc], axis=1)
    x1 = _matmul(mix, wf["w_out"], "nn", "proj_out", residual=xs)
    gather_wait(2, x1)
    h2 = _rmsnorm_fwd(x1, norm2_w, "norm2")
    mem_n = _rmsnorm_fwd(mems, mem_norm_w, "norm_mem")
    qa = _matmul(h2, wf["wq"], "nn", "attn_q", out_dtype=BF16)
    ka = _matmul(mem_n, wf["wk"], "nn", "attn_k", out_dtype=BF16)
    va = _matmul(mem_n, wf["wv"], "nn", "attn_v", out_dtype=BF16)
    att = _attn_fwd(qa, ka, va, "attn_fwd")
    x2 = _matmul(att, wf["wo"], "nn", "attn_o", residual=x1)
    gather_wait(3, x2)
    fconv8 = wf["fconv8"]
    h3 = _rmsnorm_fwd(x2, norm3_w, "norm3")
    gate = _matmul(h3, wf["w_gate"], "nn", "ffn_gate")
    up = _matmul(h3, wf["w_up"], "nn", "ffn_up")
    z = _ffn_fwd(gate, up, fconv8, ffn_conv_b, "ffn_act")
    x3 = _matmul(z, wf["w_down"], "nn", "ffn_down", residual=x2, tk=1408)

    dx3, dx3b, g_final, loss8 = _final_loss_bwd(x3, tgt, fnw, "loss_bwd")
    gw = {}
    dz = _matmul(dx3b, wf["w_down"], "nt", "d_z")
    gw["w_down"] = _matmul(z, dx3b, "tn", "g_w_down", extra_bf16=True, tk=1024)
    dgate, du, g_fb, g_fw = _ffn_bwd(gate, up, dz, fconv8, ffn_conv_b, "ffn_act_bwd")
    dh3 = _matmul(dgate, wf["w_gate"], "nt", "d_h3_gate", tk=1408)
    dh3 = _matmul(du, wf["w_up"], "nt", "d_h3_up", residual=dh3, tk=1408)
    gw["w_gate"] = _matmul(h3, dgate, "tn", "g_w_gate", extra_bf16=True, tk=1024)
    gw["w_up"] = _matmul(h3, du, "tn", "g_w_up", extra_bf16=True, tk=1024)
    pending = []

    def scatter_start(grp):
        sems, g_thru, lands, token = _scatter_start([gw[n][1] for n in grp], [big[n][1] for n in grp],
                                                    "scatter_start_" + grp[0])
        pending.append((grp, sems, g_thru, lands))
        return token[0:1, 0:1]

    tok1 = scatter_start(["w_down", "w_gate", "w_up"])
    dx2, dx2b, g_n3 = _rmsnorm_bwd(dh3, x2, norm3_w + tok1, dx3, "norm3_bwd")
    datt = _matmul(dx2b, wf["wo"], "nt", "d_att", out_dtype=BF16)
    gw["wo"] = _matmul(att, dx2b, "tn", "g_wo", extra_bf16=True, tk=1024)
    dqa, dka, dva = _attn_bwd(qa, ka, va, datt, "attn_bwd")
    dh2 = _matmul(dqa, wf["wq"], "nt", "d_h2")
    gw["wq"] = _matmul(h2, dqa, "tn", "g_wq", extra_bf16=True, tk=1024)
    gw["wk"] = _matmul(mem_n, dka, "tn", "g_wk", extra_bf16=True)
    gw["wv"] = _matmul(mem_n, dva, "tn", "g_wv", extra_bf16=True)
    tok2 = scatter_start(["wo", "wq", "wk", "wv"])
    dmem_n = _matmul(dka, wf["wk"], "nt", "d_memn_k")
    dmem_n = _matmul(dva, wf["wv"], "nt", "d_memn_v", residual=dmem_n)
    _, _, g_nm = _rmsnorm_bwd(dmem_n, mems, mem_norm_w, None, "norm_mem_bwd")
    dx1, dx1b, g_n2 = _rmsnorm_bwd(dh2, x1, norm2_w + tok2, dx2, "norm2_bwd")
    dmix = _matmul(dx1b, wf["w_out"], "nt", "d_mix")
    gw["w_out"] = _matmul(mix, dx1b, "tn", "g_w_out", extra_bf16=True, tk=1024)
    tok3 = scatter_start(["w_out"])
    dcb, dcc, dch, g_sw = _sconv_bwd(proj, sconv8, dmix, "sconv_bwd")
    dq, df, di, dg, g_lb, g_hn = _hgrn_bwd(proj, lb0, lb1, hgrn_norm_w + tok3, o_h, states, dmix, "hgrn_bwd")
    dproj = jnp.concatenate([dq, df, di, dg, dcb, dcc, dch], axis=1)
    gw["w_in"] = _matmul(h1, dproj, "tn", "g_w_in", extra_bf16=True, tk=1024)
    tok4 = scatter_start(["w_in"])
    dh1 = _matmul(dproj, wf["w_in"], "nt", "d_h1", tk=1024)
    dx, _, g_n1 = _rmsnorm_bwd(dh1, xs, norm1_w + tok4, dx1, "norm1_bwd")

    small = [g_n1, g_n2, g_n3, g_final, g_nm, g_lb, g_hn, g_fb,
             g_sw[0:8], g_sw[8:16], g_sw[16:24], g_fw[0:8], g_fw[8:16], g_fw[16:24], loss8]
    widths = [a.shape[1] for a in small]
    tot = _all_reduce_small(jnp.concatenate(small, axis=1), "all_reduce_small")
    offs = [0]
    for wd_ in widths:
        offs.append(offs[-1] + wd_)
    sm = [tot[:, offs[i]:offs[i + 1]] for i in range(len(small))]
    s_n1, s_n2, s_n3, s_final, s_nm, s_lb, s_hn, s_fb = sm[:8]
    s_sw = jnp.concatenate(sm[8:11], axis=0)
    s_fw = jnp.concatenate(sm[11:14], axis=0)
    loss = sm[14][0, 0]
    slot = 2 * lax.axis_index("x") + lax.axis_index("y")
    s_sw = lax.dynamic_slice_in_dim(s_sw, slot * (HGRN_W // N_CHIPS), HGRN_W // N_CHIPS, axis=1)
    fsh = ffn_conv_w.shape[2]
    s_fw = lax.dynamic_slice_in_dim(s_fw, slot * fsh, fsh, axis=1)
    s_lb2 = jnp.concatenate([s_lb, -s_lb], axis=0)

    recv = {}
    for grp, sems, g_thru, lands in pending:
        got = _scatter_wait(g_thru, lands, [big[n][1] for n in grp], sems, tot, "scatter_wait_" + grp[0])
        recv.update(zip(grp, got))
    core_sums = [_sum4(gw[n][0], big[n][1], slot_arr, recv[n], "core_sum_" + n) for n in names]
    sib_sums = _sibling_exchange(core_sums, "sibling_exchange")

    moments = {"hgrn_lb": (m_hgrn_lb, v_hgrn_lb), "norm1_w": (m_norm1_w, v_norm1_w), "w_in": (m_w_in, v_w_in),
               "hgrn_norm_w": (m_hgrn_norm_w, v_hgrn_norm_w), "sconv_w": (m_sconv_w, v_sconv_w),
               "w_out": (m_w_out, v_w_out), "norm2_w": (m_norm2_w, v_norm2_w),
               "mem_norm_w": (m_mem_norm_w, v_mem_norm_w), "wq": (m_wq, v_wq), "wk": (m_wk, v_wk), "wv": (m_wv, v_wv),
               "wo": (m_wo, v_wo), "norm3_w": (m_norm3_w, v_norm3_w), "w_gate": (m_w_gate, v_w_gate),
               "w_up": (m_w_up, v_w_up), "ffn_conv_w": (m_ffn_conv_w, v_ffn_conv_w),
               "ffn_conv_b": (m_ffn_conv_b, v_ffn_conv_b), "w_down": (m_w_down, v_w_down),
               "final_norm_w": (m_final_norm_w, v_final_norm_w)}
    weights = {"hgrn_lb": hgrn_lb, "norm1_w": norm1_w, "w_in": w_in, "hgrn_norm_w": hgrn_norm_w, "sconv_w": sconv_w,
               "w_out": w_out, "norm2_w": norm2_w, "mem_norm_w": mem_norm_w, "wq": wq, "wk": wk, "wv": wv, "wo": wo,
               "norm3_w": norm3_w, "w_gate": w_gate, "w_up": w_up, "ffn_conv_w": ffn_conv_w, "ffn_conv_b": ffn_conv_b,
               "w_down": w_down, "final_norm_w": final_norm_w}
    small_g = {"hgrn_lb": s_lb2, "norm1_w": s_n1, "hgrn_norm_w": s_hn, "sconv_w": s_sw, "norm2_w": s_n2,
               "mem_norm_w": s_nm, "norm3_w": s_n3, "ffn_conv_w": s_fw, "ffn_conv_b": s_fb, "final_norm_w": s_final}
    order = list(weights)
    res = {}
    for n in order:
        w_full = weights[n]
        shape = w_full.shape
        w2 = w_full.reshape((-1, shape[-1]))
        m2, v2 = (t.reshape(w2.shape) for t in moments[n])
        if n in big:
            i = names.index(n)
            parts = [core_sums[i], sib_sums[i]]
        else:
            parts = [small_g[n].reshape(w2.shape)]
        res[n] = [t.reshape(shape) for t in _adamw(w2, parts, m2, v2, "adamw_" + n)]

    return (loss, dx[None], *[res[n][0] for n in order], *[res[n][1] for n in order],
            *[res[n][2] for n in order], *[res[n][3] for n in order])
```

```python
import functools

import jax
import jax.numpy as jnp
from jax import lax
from jax.experimental import pallas as pl
from jax.experimental.pallas import tpu as pltpu

F32 = jnp.float32
BF16 = jnp.bfloat16
MESH = pl.DeviceIdType.MESH

EPS = 1e-6
HGRN_W = 1024
HEAD = 128
N_HEADS = 8
CHUNK = 64
HGRN_UNROLL = 8
HGRN_HEADS_PER_STEP = 2
MEM_HEADS = 4
MEM_HEAD_DIM = 512
N_CHIPS = 4
HALO = 8

ADAM_LR = 0.001
ADAM_B1 = 0.9
ADAM_B2 = 0.999
ADAM_EPS = 1e-08
ADAM_WD = 0.01
ADAM_STEP = 10


def _sigmoid(x):
    return 1.0 / (1.0 + jnp.exp(-x))


def _dot(a, b, dims):
    return lax.dot_general(a.astype(BF16), b.astype(BF16), (dims, ((), ())),
                           preferred_element_type=F32)


def _dot_nn(a, b):
    return _dot(a, b, ((1,), (0,)))


def _dot_nt(a, b):
    return _dot(a, b, ((1,), (1,)))


def _dot_tn(a, b):
    return _dot(a, b, ((0,), (0,)))


def _hdot(a, b, dims):
    return lax.dot_general(a, b, (dims, ((), ())), precision=lax.Precision.HIGH, preferred_element_type=F32)


def _hdot_nn(a, b):
    return _hdot(a, b, ((1,), (0,)))


def _hdot_nt(a, b):
    return _hdot(a, b, ((1,), (1,)))


def _hdot_tn(a, b):
    return _hdot(a, b, ((0,), (0,)))


def _exact_ones_dot(ones_bf16, x):
    hi = x.astype(BF16)
    r1 = x - hi.astype(F32)
    mid = r1.astype(BF16)
    lo = (r1 - mid.astype(F32)).astype(BF16)
    dims = (((1,), (0,)), ((), ()))
    return (lax.dot_general(ones_bf16, hi, dims, preferred_element_type=F32)
            + lax.dot_general(ones_bf16, mid, dims, preferred_element_type=F32)
            + lax.dot_general(ones_bf16, lo, dims, preferred_element_type=F32))


def _rows8(v):
    t, c = v.shape
    return v.reshape(t // 8, 8, c).sum(axis=0)


def _shift_down(x, halo, s):
    rolled = pltpu.roll(x, s, 0)
    hrolled = pltpu.roll(halo, s, 0)
    row = lax.broadcasted_iota(jnp.int32, hrolled.shape, 0)
    head = jnp.where(row < s, hrolled, rolled[:HALO])
    return jnp.concatenate([head, rolled[HALO:]], axis=0)


def _shift_up(x, halo, s):
    t = x.shape[0]
    rolled = pltpu.roll(x, t - s, 0)
    hrolled = pltpu.roll(halo, HALO - s, 0)
    row = lax.broadcasted_iota(jnp.int32, hrolled.shape, 0)
    tail = jnp.where(row >= HALO - s, hrolled, rolled[t - HALO:])
    return jnp.concatenate([rolled[:t - HALO], tail], axis=0)


def _params(*sem):
    return pltpu.CompilerParams(dimension_semantics=sem)


def _row_tile(r, pref):
    while r % pref:
        pref //= 2
    return pref


def _rmsnorm_fwd(x, w, name, tm=256):
    s, d = x.shape
    tm = min(tm, s)

    def body(x_ref, w_ref, o_ref):
        xv = x_ref[...]
        r = lax.rsqrt(jnp.mean(xv * xv, axis=-1, keepdims=True) + EPS)
        o_ref[...] = ((xv * r) * w_ref[...]).astype(BF16)

    return pl.pallas_call(
        body, name=name, grid=(s // tm,),
        in_specs=[pl.BlockSpec((tm, d), lambda i: (i, 0)), pl.BlockSpec((1, d), lambda i: (0, 0))],
        out_specs=pl.BlockSpec((tm, d), lambda i: (i, 0)),
        out_shape=jax.ShapeDtypeStruct((s, d), BF16),
        compiler_params=_params("parallel"),
    )(x, w)


def _rmsnorm_bwd(dh, x, w, dres, name, tm=256):
    s, d = x.shape
    tm = min(tm, s)
    has_res = dres is not None

    def body(*refs):
        if has_res:
            dh_ref, x_ref, w_ref, dres_ref, dx_ref, dxb_ref, gw_ref = refs
        else:
            dh_ref, x_ref, w_ref, dx_ref, dxb_ref, gw_ref = refs

        @pl.when(pl.program_id(0) == 0)
        def _():
            gw_ref[...] = jnp.zeros_like(gw_ref)

        xv = x_ref[...]
        dhv = dh_ref[...].astype(F32)
        r = lax.rsqrt(jnp.mean(xv * xv, axis=-1, keepdims=True) + EPS)
        xhat = xv * r
        gw_ref[...] += _rows8(dhv * xhat)
        dxh = dhv * w_ref[...]
        dx = r * (dxh - xhat * jnp.mean(dxh * xhat, axis=-1, keepdims=True))
        if has_res:
            dx = dres_ref[...] + dx
        dx_ref[...] = dx
        dxb_ref[...] = dx.astype(BF16)

    row = pl.BlockSpec((tm, d), lambda i: (i, 0))
    in_specs = [row, row, pl.BlockSpec((1, d), lambda i: (0, 0))] + ([row] if has_res else [])
    args = (dh, x, w) + ((dres,) if has_res else ())
    return pl.pallas_call(
        body, name=name, grid=(s // tm,),
        in_specs=in_specs,
        out_specs=[row, row, pl.BlockSpec((8, d), lambda i: (0, 0))],
        out_shape=[jax.ShapeDtypeStruct((s, d), F32), jax.ShapeDtypeStruct((s, d), BF16),
                   jax.ShapeDtypeStruct((8, d), F32)],
        compiler_params=_params("arbitrary"),
    )(*args)


def _final_loss_bwd(x3, target, w, name, tm=256):
    s, d = x3.shape
    tm = min(tm, s)

    def body(x_ref, t_ref, w_ref, dx_ref, dxb_ref, gw_ref, loss_ref):
        @pl.when(pl.program_id(0) == 0)
        def _():
            gw_ref[...] = jnp.zeros_like(gw_ref)
            loss_ref[...] = jnp.zeros_like(loss_ref)

        xv = x_ref[...]
        r = lax.rsqrt(jnp.mean(xv * xv, axis=-1, keepdims=True) + EPS)
        xhat = xv * r
        y = xhat * w_ref[...]
        err = y - t_ref[...]
        part = 0.5 * jnp.mean(err * err, axis=-1, keepdims=True)
        tot = jnp.sum(part, axis=0, keepdims=True)
        rr = lax.broadcasted_iota(jnp.int32, loss_ref.shape, 0)
        cc = lax.broadcasted_iota(jnp.int32, loss_ref.shape, 1)
        loss_ref[...] += jnp.where((rr == 0) & (cc == 0), tot, 0.0)
        dy = err * (1.0 / d)
        gw_ref[...] += _rows8(dy * xhat)
        dxh = dy * w_ref[...]
        dx = r * (dxh - xhat * jnp.mean(dxh * xhat, axis=-1, keepdims=True))
        dx_ref[...] = dx
        dxb_ref[...] = dx.astype(BF16)

    row = pl.BlockSpec((tm, d), lambda i: (i, 0))
    return pl.pallas_call(
        body, name=name, grid=(s // tm,),
        in_specs=[row, row, pl.BlockSpec((1, d), lambda i: (0, 0))],
        out_specs=[row, row, pl.BlockSpec((8, d), lambda i: (0, 0)), pl.BlockSpec((8, 128), lambda i: (0, 0))],
        out_shape=[jax.ShapeDtypeStruct((s, d), F32), jax.ShapeDtypeStruct((s, d), BF16),
                   jax.ShapeDtypeStruct((8, d), F32), jax.ShapeDtypeStruct((8, 128), F32)],
        compiler_params=_params("arbitrary"),
    )(x3, target, w)


MM_TILES = (1024, 1408, 512, 256, 128)
MM_K_TILES = (2816, 2048, 1792, 1408, 1024, 512, 256, 128)
MM_VMEM_LIMIT = 48 * 1024 * 1024
MM_VMEM_BUDGET = 40 * 1024 * 1024


def _pick_tile(dim):
    for t in MM_TILES:
        if dim % t == 0:
            return t
    return dim


def _matmul(a, b, mode, name, *, out_dtype=F32, residual=None, extra_bf16=False, tm=None, tn=None, tk=None):
    if mode == "nn":
        (m, k), (k2, n) = a.shape, b.shape
    elif mode == "nt":
        (m, k), (n, k2) = a.shape, b.shape
    else:
        (k, m), (k2, n) = a.shape, b.shape
    assert k == k2, (a.shape, b.shape, mode)
    tm = _pick_tile(m) if tm is None else min(tm, m)
    tn = _pick_tile(n) if tn is None else min(tn, n)
    out_bytes = tm * tn * (jnp.dtype(out_dtype).itemsize + (2 if extra_bf16 else 0) + (4 if residual is not None else 0))

    def vmem_bytes(t):
        return (2 * (tm * t * a.dtype.itemsize + t * tn * b.dtype.itemsize) + 2 * out_bytes
                + (tm * tn * 4 if t < k else 0))

    if tk is None:
        tk = next(t for t in MM_K_TILES if k % t == 0 and t <= k and vmem_bytes(t) <= MM_VMEM_BUDGET)
    assert m % tm == 0 and n % tn == 0 and k % tk == 0, (m, n, k, tm, tn, tk)
    nk = k // tk
    dims = {"nn": ((1,), (0,)), "nt": ((1,), (1,)), "tn": ((0,), (0,))}[mode]
    has_res = residual is not None

    def body(*refs):
        refs = list(refs)
        a_ref, b_ref = refs[0], refs[1]
        r_ref = refs[2] if has_res else None
        outs = refs[2 + has_res:]
        o_ref = outs[0]
        o2_ref = outs[1] if extra_bf16 else None
        def finish(r):
            if has_res:
                r = r_ref[...] + r
            o_ref[...] = r.astype(out_dtype)
            if extra_bf16:
                o2_ref[...] = r.astype(BF16)

        if nk == 1:
            finish(_dot(a_ref[...], b_ref[...], dims))
            return
        acc = outs[-1]
        kk = pl.program_id(2)

        @pl.when(kk == 0)
        def _():
            acc[...] = jnp.zeros_like(acc)

        acc[...] += _dot(a_ref[...], b_ref[...], dims)

        @pl.when(kk == nk - 1)
        def _():
            finish(acc[...])

    if mode == "tn":
        a_spec = pl.BlockSpec((tk, tm), lambda i, j, kk: (kk, i))
    else:
        a_spec = pl.BlockSpec((tm, tk), lambda i, j, kk: (i, kk))
    if mode == "nt":
        b_spec = pl.BlockSpec((tn, tk), lambda i, j, kk: (j, kk))
    else:
        b_spec = pl.BlockSpec((tk, tn), lambda i, j, kk: (kk, j))
    o_spec = pl.BlockSpec((tm, tn), lambda i, j, kk: (i, j))
    in_specs = [a_spec, b_spec] + ([o_spec] if has_res else [])
    out_specs = [o_spec] + ([o_spec] if extra_bf16 else [])
    out_shape = [jax.ShapeDtypeStruct((m, n), out_dtype)] + ([jax.ShapeDtypeStruct((m, n), BF16)] if extra_bf16 else [])
    args = (a, b) + ((residual,) if has_res else ())
    res = pl.pallas_call(
        body, name=name, grid=(m // tm, n // tn, nk),
        in_specs=in_specs, out_specs=out_specs, out_shape=out_shape,
        scratch_shapes=[pltpu.VMEM((tm, tn) if nk > 1 else (8, 128), F32)],
        compiler_params=pltpu.CompilerParams(dimension_semantics=("parallel", "parallel", "arbitrary"),
                                             vmem_limit_bytes=MM_VMEM_LIMIT),
    )(*args)
    return res if extra_bf16 else res[0]


def _hgrn_gates(qp, fp, lb):
    sig = _sigmoid(fp)
    f = lb + (1.0 - lb) * sig
    logf = jnp.log(f)
    k = 1.0 - f
    sq = _sigmoid(qp)
    q = qp * sq
    return sig, f, logf, k, sq, q


def _hgrn_fwd(proj, lb0, lb1, norm_w, name, tb=512):
    s = proj.shape[0]
    tb = min(tb, s)
    nb, ncb = s // tb, tb // CHUNK

    def body(q_ref, f_ref, i_ref, g_ref, a0_ref, a1_ref, nw_ref, o_ref, og_ref, st_ref, state):
        @pl.when(pl.program_id(1) == 0)
        def _():
            state[...] = jnp.zeros_like(state)

        lb2 = _sigmoid(a0_ref[...] - a1_ref[...])
        row = lax.broadcasted_iota(jnp.int32, (CHUNK, CHUNK), 0)
        col = lax.broadcasted_iota(jnp.int32, (CHUNK, CHUNK), 1)
        tril = row >= col
        ones_l = tril.astype(BF16)
        nw = nw_ref[...]

        def chunk(c, carry):
            rows = pl.ds(pl.multiple_of(c * CHUNK, CHUNK), CHUNK)
            for hh in range(HGRN_HEADS_PER_STEP):
                cols = slice(hh * HEAD, (hh + 1) * HEAD)
                v = i_ref[rows, cols].astype(F32)
                _, _, logf, k, _, q = _hgrn_gates(q_ref[rows, cols].astype(F32), f_ref[rows, cols].astype(F32),
                                                  lb2[:, cols])
                b = _exact_ones_dot(ones_l, logf)
                bl = jnp.sum(logf, axis=0, keepdims=True)
                bm = 0.5 * bl
                st = state[hh]
                st_ref[hh, c] = st
                qt = q * jnp.exp(b - bm)
                kt = k * jnp.exp(bm - b)
                a = jnp.where(tril, _dot_nt(qt, kt), 0.0)
                o = _dot_nt(q * jnp.exp(b), st) + _dot_nn(a, v)
                state[hh] = st * jnp.exp(bl) + _dot_tn(v, k * jnp.exp(bl - b))
                o_ref[rows, cols] = o
                on = (o * lax.rsqrt(jnp.mean(o * o, axis=-1, keepdims=True) + EPS)) * nw
                gv = g_ref[rows, cols].astype(F32)
                og_ref[rows, cols] = (on * (gv * _sigmoid(gv))).astype(BF16)
            return carry

        lax.fori_loop(0, ncb, chunk, 0, unroll=HGRN_UNROLL)

    hp, wd = HGRN_HEADS_PER_STEP, HGRN_HEADS_PER_STEP * HEAD
    ngrp = N_HEADS // hp

    def colblk(group):
        return pl.BlockSpec((tb, wd), lambda h, j: (j, group * ngrp + h))

    vec = pl.BlockSpec((1, wd), lambda h, j: (0, h))
    out_blk = pl.BlockSpec((tb, wd), lambda h, j: (j, h))
    return pl.pallas_call(
        body, name=name, grid=(ngrp, nb),
        in_specs=[colblk(0), colblk(1), colblk(2), colblk(3), vec, vec, pl.BlockSpec((1, HEAD), lambda h, j: (0, 0))],
        out_specs=[out_blk, out_blk, pl.BlockSpec((hp, ncb, HEAD, HEAD), lambda h, j: (h, j, 0, 0))],
        out_shape=[jax.ShapeDtypeStruct((s, HGRN_W), F32), jax.ShapeDtypeStruct((s, HGRN_W), BF16),
                   jax.ShapeDtypeStruct((N_HEADS, s // CHUNK, HEAD, HEAD), F32)],
        scratch_shapes=[pltpu.VMEM((hp, HEAD, HEAD), F32)],
        compiler_params=_params("parallel", "arbitrary"),
    )(proj, proj, proj, proj, lb0, lb1, norm_w)


def _hgrn_bwd(proj, lb0, lb1, norm_w, o, states, dmix, name, tb=512):
    s = proj.shape[0]
    tb = min(tb, s)
    nb, ncb = s // tb, tb // CHUNK

    def body(q_ref, f_ref, i_ref, g_ref, a0_ref, a1_ref, nw_ref, o_ref, st_ref, dm_ref,
             dq_ref, df_ref, di_ref, dg_ref, glb_ref, gnw_ref, dstate):
        h = pl.program_id(0)

        @pl.when(pl.program_id(1) == 0)
        def _():
            dstate[...] = jnp.zeros_like(dstate)
            glb_ref[...] = jnp.zeros_like(glb_ref)

        @pl.when((pl.program_id(1) == 0) & (h == 0))
        def _():
            gnw_ref[...] = jnp.zeros_like(gnw_ref)

        lb2 = _sigmoid(a0_ref[...] - a1_ref[...])
        row = lax.broadcasted_iota(jnp.int32, (CHUNK, CHUNK), 0)
        col = lax.broadcasted_iota(jnp.int32, (CHUNK, CHUNK), 1)
        tril = row >= col
        ones_l = tril.astype(BF16)
        ones_u = (row <= col).astype(BF16)
        nw = nw_ref[...]

        def chunk(cc, carry):
            c = ncb - 1 - cc
            rows = pl.ds(pl.multiple_of(c * CHUNK, CHUNK), CHUNK)
            for hh in range(HGRN_HEADS_PER_STEP):
                cols = slice(hh * HEAD, (hh + 1) * HEAD)
                lb = lb2[:, cols]
                qp = q_ref[rows, cols].astype(F32)
                v = i_ref[rows, cols].astype(F32)
                sig, f, logf, k, sq, q = _hgrn_gates(qp, f_ref[rows, cols].astype(F32), lb)
                gv = g_ref[rows, cols].astype(F32)
                sg = _sigmoid(gv)
                silu_g = gv * sg
                dog = dm_ref[rows, cols].astype(F32)
                ov = o_ref[rows, cols]
                r = lax.rsqrt(jnp.mean(ov * ov, axis=-1, keepdims=True) + EPS)
                ohat = ov * r
                on = ohat * nw
                dg_ref[rows, cols] = (dog * on * (sg * (1.0 + gv * (1.0 - sg)))).astype(BF16)
                don = dog * silu_g
                gnw_ref[...] += _rows8(don * ohat)
                doh = don * nw
                do = r * (doh - ohat * jnp.mean(doh * ohat, axis=-1, keepdims=True))
                b = _exact_ones_dot(ones_l, logf)
                bl = jnp.sum(logf, axis=0, keepdims=True)
                bm = 0.5 * bl
                e_q = jnp.exp(b - bm)
                e_k = jnp.exp(bm - b)
                e_b = jnp.exp(b)
                e_l = jnp.exp(bl - b)
                qt, kt, qb, kb = q * e_q, k * e_k, q * e_b, k * e_l
                st0 = st_ref[hh, c]
                dst = dstate[hh]
                a = jnp.where(tril, _dot_nt(qt, kt), 0.0)
                da = jnp.where(tril, _dot_nt(do, v), 0.0)
                dq = _hdot_nn(da, kt) * e_q + _hdot_nn(do, st0) * e_b
                dkb = _hdot_nn(v, dst) * e_l
                dk = _hdot_tn(da, qt) * e_k + dkb
                dv = _dot_tn(a, do) + _dot_nt(kb, dst)
                e_bl = jnp.exp(bl)
                dstate[hh] = dst * e_bl + _dot_tn(do, qb)
                db = q * dq - k * dk
                db_last = jnp.sum(k * dkb, axis=0, keepdims=True) + e_bl * jnp.sum(st0 * dst, axis=0, keepdims=True)
                dlogf = _exact_ones_dot(ones_u, db) + db_last
                dfg = dlogf / f - dk
                df_ref[rows, cols] = (dfg * (1.0 - lb) * (sig * (1.0 - sig))).astype(BF16)
                glb_ref[:, cols] += _rows8(dfg * (1.0 - sig)) * (lb * (1.0 - lb))
                dq_ref[rows, cols] = (dq * (sq * (1.0 + qp * (1.0 - sq)))).astype(BF16)
                di_ref[rows, cols] = dv.astype(BF16)
            return carry

        lax.fori_loop(0, ncb, chunk, 0, unroll=HGRN_UNROLL)

    hp, wd = HGRN_HEADS_PER_STEP, HGRN_HEADS_PER_STEP * HEAD
    ngrp = N_HEADS // hp

    def colblk(group):
        return pl.BlockSpec((tb, wd), lambda h, j: (nb - 1 - j, group * ngrp + h))

    vec = pl.BlockSpec((1, wd), lambda h, j: (0, h))
    blk = pl.BlockSpec((tb, wd), lambda h, j: (nb - 1 - j, h))
    grad = jax.ShapeDtypeStruct((s, HGRN_W), BF16)
    return pl.pallas_call(
        body, name=name, grid=(ngrp, nb),
        in_specs=[colblk(0), colblk(1), colblk(2), colblk(3), vec, vec, pl.BlockSpec((1, HEAD), lambda h, j: (0, 0)),
                  blk, pl.BlockSpec((hp, ncb, HEAD, HEAD), lambda h, j: (h, nb - 1 - j, 0, 0)), blk],
        out_specs=[blk, blk, blk, blk, pl.BlockSpec((8, wd), lambda h, j: (0, h)),
                   pl.BlockSpec((8, HEAD), lambda h, j: (0, 0))],
        out_shape=[grad, grad, grad, grad, jax.ShapeDtypeStruct((8, HGRN_W), F32), jax.ShapeDtypeStruct((8, HEAD), F32)],
        scratch_shapes=[pltpu.VMEM((hp, HEAD, HEAD), F32)],
        compiler_params=_params("arbitrary", "arbitrary"),
    )(proj, proj, proj, proj, lb0, lb1, norm_w, o, states, dmix)


HALO_BLK = 16


def _f32(ref):
    return ref[...].astype(F32)


def _halo_prev(ref):
    return ref[...].astype(F32)[HALO_BLK - HALO:]


def _halo_next(ref):
    return ref[...].astype(F32)[:HALO]


def _conv3(x0, x1, x2, w_ref):
    y = x0 * w_ref[0:1, :]
    y = y + x1 * w_ref[1:2, :]
    return y + x2 * w_ref[2:3, :]


def _sconv_fwd(proj, w8, name, tb=256):
    s = proj.shape[0]
    tb = min(tb, s)
    hb = tb // HALO_BLK

    def body(cb_ref, cc_ref, ch_ref, cch_ref, chh_ref, w_ref, y_ref):
        first = pl.program_id(0) == 0
        u = _f32(cc_ref) * _f32(ch_ref)
        uh = jnp.where(first, 0.0, _halo_prev(cch_ref) * _halo_prev(chh_ref))
        conv = _conv3(_shift_down(u, uh, 2), _shift_down(u, uh, 1), u, w_ref)
        y_ref[...] = (_f32(cb_ref) * conv).astype(BF16)

    def blk(g):
        return pl.BlockSpec((tb, HGRN_W), lambda j: (j, g))

    def halo(g):
        return pl.BlockSpec((HALO_BLK, HGRN_W), lambda j: (jnp.maximum(j * hb - 1, 0), g))

    return pl.pallas_call(
        body, name=name, grid=(s // tb,),
        in_specs=[blk(4), blk(5), blk(6), halo(5), halo(6), pl.BlockSpec((HALO, HGRN_W), lambda j: (0, 0))],
        out_specs=pl.BlockSpec((tb, HGRN_W), lambda j: (j, 0)),
        out_shape=jax.ShapeDtypeStruct((s, HGRN_W), BF16),
        compiler_params=_params("parallel"),
    )(proj, proj, proj, proj, proj, w8)


def _sconv_bwd(proj, w8, dmix, name, tb=256):
    s = proj.shape[0]
    tb = min(tb, s)
    hb = tb // HALO_BLK
    nb = s // tb
    last_h = s // HALO_BLK - 1

    def body(cb_ref, cc_ref, ch_ref, cch_ref, chh_ref, cbn_ref, dy_ref, dyn_ref, w_ref,
             dcb_ref, dcc_ref, dch_ref, gw_ref):
        j = pl.program_id(0)

        @pl.when(j == 0)
        def _():
            gw_ref[...] = jnp.zeros_like(gw_ref)

        cc, ch, cb = _f32(cc_ref), _f32(ch_ref), _f32(cb_ref)
        u = cc * ch
        uh = jnp.where(j == 0, 0.0, _halo_prev(cch_ref) * _halo_prev(chh_ref))
        u2, u1 = _shift_down(u, uh, 2), _shift_down(u, uh, 1)
        conv = _conv3(u2, u1, u, w_ref)
        dy = _f32(dy_ref)
        dcb_ref[...] = (dy * conv).astype(BF16)
        dc = dy * cb
        dcn = jnp.where(j == nb - 1, 0.0, _halo_next(dyn_ref) * _halo_next(cbn_ref))
        gw_ref[0:8, :] += _rows8(dc * u2)
        gw_ref[8:16, :] += _rows8(dc * u1)
        gw_ref[16:24, :] += _rows8(dc * u)
        du = dc * w_ref[2:3, :] + _shift_up(dc, dcn, 1) * w_ref[1:2, :] + _shift_up(dc, dcn, 2) * w_ref[0:1, :]
        dcc_ref[...] = (du * ch).astype(BF16)
        dch_ref[...] = (du * cc).astype(BF16)

    def blk(g):
        return pl.BlockSpec((tb, HGRN_W), lambda j: (j, g))

    def halo_prev(g):
        return pl.BlockSpec((HALO_BLK, HGRN_W), lambda j: (jnp.maximum(j * hb - 1, 0), g))

    def halo_next(g):
        return pl.BlockSpec((HALO_BLK, HGRN_W), lambda j: (jnp.minimum((j + 1) * hb, last_h), g))

    out = pl.BlockSpec((tb, HGRN_W), lambda j: (j, 0))
    grad = jax.ShapeDtypeStruct((s, HGRN_W), BF16)
    return pl.pallas_call(
        body, name=name, grid=(nb,),
        in_specs=[blk(4), blk(5), blk(6), halo_prev(5), halo_prev(6), halo_next(4), blk(1), halo_next(1),
                  pl.BlockSpec((HALO, HGRN_W), lambda j: (0, 0))],
        out_specs=[out, out, out, pl.BlockSpec((24, HGRN_W), lambda j: (0, 0))],
        out_shape=[grad, grad, grad, jax.ShapeDtypeStruct((24, HGRN_W), F32)],
        compiler_params=_params("arbitrary"),
    )(proj, proj, proj, proj, proj, proj, dmix, dmix, w8)


def _attn_fwd(q, kk, vv, name, tb=256):
    s, d = q.shape
    m = kk.shape[0]
    tb = min(tb, s)
    scale = MEM_HEAD_DIM ** -0.5

    def body(q_ref, k_ref, v_ref, o_ref):
        for hh in range(MEM_HEADS):
            cols = slice(hh * MEM_HEAD_DIM, (hh + 1) * MEM_HEAD_DIM)
            sc = _dot_nt(q_ref[:, cols], k_ref[:, cols]) * scale
            sc = sc - jnp.max(sc, axis=-1, keepdims=True)
            e = jnp.exp(sc)
            p = e / jnp.sum(e, axis=-1, keepdims=True)
            o_ref[:, cols] = _dot_nn(p, v_ref[:, cols]).astype(BF16)

    full = pl.BlockSpec((m, d), lambda i: (0, 0))
    return pl.pallas_call(
        body, name=name, grid=(s // tb,),
        in_specs=[pl.BlockSpec((tb, d), lambda i: (i, 0)), full, full],
        out_specs=pl.BlockSpec((tb, d), lambda i: (i, 0)),
        out_shape=jax.ShapeDtypeStruct((s, d), BF16),
        compiler_params=_params("parallel"),
    )(q, kk, vv)


def _attn_bwd(q, kk, vv, datt, name, tb=256):
    s, d = q.shape
    m = kk.shape[0]
    tb = min(tb, s)
    scale = MEM_HEAD_DIM ** -0.5

    def body(q_ref, k_ref, v_ref, do_ref, dq_ref, dk_ref, dv_ref):
        @pl.when(pl.program_id(0) == 0)
        def _():
            dk_ref[...] = jnp.zeros_like(dk_ref)
            dv_ref[...] = jnp.zeros_like(dv_ref)

        for hh in range(MEM_HEADS):
            cols = slice(hh * MEM_HEAD_DIM, (hh + 1) * MEM_HEAD_DIM)
            qh, kh, vh, doh = q_ref[:, cols], k_ref[:, cols], v_ref[:, cols], do_ref[:, cols]
            sc = _dot_nt(qh, kh) * scale
            sc = sc - jnp.max(sc, axis=-1, keepdims=True)
            e = jnp.exp(sc)
            p = e / jnp.sum(e, axis=-1, keepdims=True)
            dp = _dot_nt(doh, vh)
            ds = p * (dp - jnp.sum(dp * p, axis=-1, keepdims=True)) * scale
            dq_ref[:, cols] = _dot_nn(ds, kh).astype(BF16)
            dk_ref[:, cols] += _dot_tn(ds, qh)
            dv_ref[:, cols] += _dot_tn(p, doh)

    full = pl.BlockSpec((m, d), lambda i: (0, 0))
    row = pl.BlockSpec((tb, d), lambda i: (i, 0))
    return pl.pallas_call(
        body, name=name, grid=(s // tb,),
        in_specs=[row, full, full, row],
        out_specs=[row, full, full],
        out_shape=[jax.ShapeDtypeStruct((s, d), BF16), jax.ShapeDtypeStruct((m, d), F32),
                   jax.ShapeDtypeStruct((m, d), F32)],
        compiler_params=_params("arbitrary"),
    )(q, kk, vv, datt)


def _ffn_fwd(g, u, w8, bias, name, tb=512, tc=1408):
    s, f = g.shape
    tb = min(tb, s)
    tc = tc if f % tc == 0 else 512
    hb = tb // HALO_BLK

    def body(g_ref, gh_ref, u_ref, w_ref, b_ref, z_ref):
        gv = _f32(g_ref)
        gh = jnp.where(pl.program_id(1) == 0, 0.0, _halo_prev(gh_ref))
        a = _conv3(_shift_down(gv, gh, 2), _shift_down(gv, gh, 1), gv, w_ref) + b_ref[...]
        z_ref[...] = ((a * _sigmoid(a)) * _f32(u_ref)).astype(BF16)

    blk = pl.BlockSpec((tb, tc), lambda c, j: (j, c))
    return pl.pallas_call(
        body, name=name, grid=(f // tc, s // tb),
        in_specs=[blk, pl.BlockSpec((HALO_BLK, tc), lambda c, j: (jnp.maximum(j * hb - 1, 0), c)), blk,
                  pl.BlockSpec((HALO, tc), lambda c, j: (0, c)), pl.BlockSpec((1, tc), lambda c, j: (0, c))],
        out_specs=blk,
        out_shape=jax.ShapeDtypeStruct((s, f), BF16),
        compiler_params=pltpu.CompilerParams(dimension_semantics=("parallel", "parallel"),
                                             vmem_limit_bytes=MM_VMEM_LIMIT),
    )(g, g, u, w8, bias)


def _ffn_bwd(g, u, dz, w8, bias, name, tb=256, tc=1408):
    s, f = g.shape
    tb = min(tb, s)
    tc = tc if f % tc == 0 else 512
    hb = tb // HALO_BLK
    nb = s // tb

    def body(g_ref, gh_ref, u_ref, dz_ref, w_ref, b_ref, dg_ref, du_ref, gb_ref, gw_ref, da_next):
        jj = pl.program_id(1)

        @pl.when(jj == 0)
        def _():
            gb_ref[...] = jnp.zeros_like(gb_ref)
            gw_ref[...] = jnp.zeros_like(gw_ref)
            da_next[...] = jnp.zeros_like(da_next)

        gv = _f32(g_ref)
        gh = jnp.where(jj == nb - 1, 0.0, _halo_prev(gh_ref))
        g2, g1 = _shift_down(gv, gh, 2), _shift_down(gv, gh, 1)
        a = _conv3(g2, g1, gv, w_ref) + b_ref[...]
        sa = _sigmoid(a)
        dz = _f32(dz_ref)
        du_ref[...] = (dz * (a * sa)).astype(BF16)
        da = dz * _f32(u_ref) * (sa * (1.0 + a * (1.0 - sa)))
        gb_ref[...] += _rows8(da)
        gw_ref[0:8, :] += _rows8(da * g2)
        gw_ref[8:16, :] += _rows8(da * g1)
        gw_ref[16:24, :] += _rows8(da * gv)
        dan = da_next[...]
        dg = da * w_ref[2:3, :] + _shift_up(da, dan, 1) * w_ref[1:2, :] + _shift_up(da, dan, 2) * w_ref[0:1, :]
        dg_ref[...] = dg.astype(BF16)
        da_next[...] = da[:HALO]

    blk = pl.BlockSpec((tb, tc), lambda c, jj: (nb - 1 - jj, c))
    return pl.pallas_call(
        body, name=name, grid=(f // tc, nb),
        in_specs=[blk, pl.BlockSpec((HALO_BLK, tc), lambda c, jj: (jnp.maximum((nb - 1 - jj) * hb - 1, 0), c)), blk, blk,
                  pl.BlockSpec((HALO, tc), lambda c, jj: (0, c)), pl.BlockSpec((1, tc), lambda c, jj: (0, c))],
        out_specs=[blk, blk, pl.BlockSpec((8, tc), lambda c, jj: (0, c)), pl.BlockSpec((24, tc), lambda c, jj: (0, c))],
        out_shape=[jax.ShapeDtypeStruct((s, f), BF16), jax.ShapeDtypeStruct((s, f), BF16),
                   jax.ShapeDtypeStruct((8, f), F32), jax.ShapeDtypeStruct((24, f), F32)],
        scratch_shapes=[pltpu.VMEM((HALO, tc), F32)],
        compiler_params=pltpu.CompilerParams(dimension_semantics=("parallel", "arbitrary"),
                                             vmem_limit_bytes=MM_VMEM_LIMIT),
    )(g, g, u, dz, w8, bias)


def _window(ref, axis, slot, size):
    start = pl.multiple_of(slot * size, size)
    if axis == 0:
        return ref.at[pl.ds(start, size), :]
    return ref.at[:, pl.ds(start, size)]


def _chip_peers():
    x, y, c = lax.axis_index("x"), lax.axis_index("y"), lax.axis_index("c")
    peers = [(1 - x, y, c), (x, 1 - y, c), (1 - x, 1 - y, c)]
    slots = [2 * (1 - x) + y, 2 * x + (1 - y), 2 * (1 - x) + (1 - y)]
    return 2 * x + y, peers, slots


HBM_SPEC = pl.BlockSpec(memory_space=pltpu.HBM)
SEM_SPEC = pl.BlockSpec(memory_space=pltpu.SEMAPHORE)
EFFECT = pltpu.SideEffectType.DATAFLOW_SIDE_EFFECTING


def _hbm(a):
    return pltpu.with_memory_space_constraint(a, pltpu.HBM)


def _cast_into_full(x, axis, slot_arr, dtype, name):
    r, c = x.shape
    tr = _row_tile(r, 256)
    nb = r // tr
    full = (r * N_CHIPS, c) if axis == 0 else (r, c * N_CHIPS)

    def body(slot_ref, x_ref, o_ref):
        o_ref[...] = x_ref[...].astype(dtype)

    if axis == 0:
        out_map = lambda i, s: (s[0] * nb + i, 0)
    else:
        out_map = lambda i, s: (i, s[0])
    return pl.pallas_call(
        body, name=name,
        grid_spec=pltpu.PrefetchScalarGridSpec(
            num_scalar_prefetch=1, grid=(nb,),
            in_specs=[pl.BlockSpec((tr, c), lambda i, s: (i, 0))],
            out_specs=pl.BlockSpec((tr, c), out_map)),
        out_shape=jax.ShapeDtypeStruct(full, dtype),
        compiler_params=_params("parallel"),
    )(slot_arr, x)


def _gather_start(fulls, axes, groups, name):
    n, ng = len(fulls), len(groups)

    def body(*refs):
        outs = refs[n:]
        sems = outs[:2 * ng]
        thru = outs[2 * ng:2 * ng + n]
        token = outs[-1]
        slot, peers, _ = _chip_peers()
        for g, members in enumerate(groups):
            for i, t in enumerate(members):
                size = thru[t].shape[axes[t]] // N_CHIPS
                mine = _window(thru[t], axes[t], slot, size)
                for k in range(3):
                    pltpu.make_async_remote_copy(
                        src_ref=mine, dst_ref=mine, send_sem=sems[2 * g].at[3 * i + k],
                        recv_sem=sems[2 * g + 1].at[3 * i + k], device_id=peers[k], device_id_type=MESH).start()
        token[...] = jnp.zeros_like(token)

    sem_shapes = []
    for members in groups:
        sem_shapes += [pltpu.SemaphoreType.DMA((3 * len(members),))] * 2
    res = pl.pallas_call(
        body, name=name,
        in_specs=[HBM_SPEC] * n,
        out_specs=[SEM_SPEC] * (2 * ng) + [HBM_SPEC] * n + [pl.BlockSpec(memory_space=pltpu.VMEM)],
        out_shape=sem_shapes + [pltpu.HBM(f.shape, f.dtype) for f in fulls] + [jax.ShapeDtypeStruct((8, 128), F32)],
        input_output_aliases={t: 2 * ng + t for t in range(n)},
        compiler_params=pltpu.CompilerParams(has_side_effects=EFFECT),
    )(*[_hbm(f) for f in fulls])
    sems = [(res[2 * g], res[2 * g + 1]) for g in range(ng)]
    return sems, list(res[2 * ng:2 * ng + n]), res[-1]


def _gather_wait(fulls, axes, sems, after, name):
    n = len(fulls)

    def body(*refs):
        send_sems, recv_sems = refs[n], refs[n + 1]
        thru = refs[n + 3:]
        slot, peers, slots = _chip_peers()
        for t in range(n):
            size = thru[t].shape[axes[t]] // N_CHIPS
            mine = _window(thru[t], axes[t], slot, size)
            for k in range(3):
                cp = pltpu.make_async_remote_copy(
                    src_ref=mine, dst_ref=_window(thru[t], axes[t], slots[k], size),
                    send_sem=send_sems.at[3 * t + k], recv_sem=recv_sems.at[3 * t + k],
                    device_id=peers[k], device_id_type=MESH)
                cp.wait_send()
                cp.wait_recv()

    return pl.pallas_call(
        body, name=name,
        in_specs=[HBM_SPEC] * n + [SEM_SPEC, SEM_SPEC, pl.BlockSpec(memory_space=pl.ANY)],
        out_specs=[HBM_SPEC] * n,
        out_shape=[pltpu.HBM(f.shape, f.dtype) for f in fulls],
        input_output_aliases={t: t for t in range(n)},
        compiler_params=pltpu.CompilerParams(has_side_effects=EFFECT),
    )(*fulls, sems[0], sems[1], after)


def _scatter_start(grads_bf16, axes, name):
    n = len(grads_bf16)

    def shard_shape(g, ax):
        return (g.shape[0] // N_CHIPS, g.shape[1]) if ax == 0 else (g.shape[0], g.shape[1] // N_CHIPS)

    shapes = [shard_shape(g, ax) for g, ax in zip(grads_bf16, axes)]

    def body(*refs):
        outs = refs[2 * n:]
        send_sems, recv_sems = outs[0], outs[1]
        gb, land = outs[2:2 + n], outs[2 + n:2 + 2 * n]
        token = outs[-1]
        _, peers, slots = _chip_peers()
        for t in range(n):
            size = shapes[t][axes[t]]
            for k in range(3):
                pltpu.make_async_remote_copy(
                    src_ref=_window(gb[t], axes[t], slots[k], size), dst_ref=land[t].at[k],
                    send_sem=send_sems.at[3 * t + k], recv_sem=recv_sems.at[3 * t + k],
                    device_id=peers[k], device_id_type=MESH).start()
        token[...] = jnp.zeros_like(token)

    lands = [_hbm(lax.empty((3,) + sh, BF16)) for sh in shapes]
    res = pl.pallas_call(
        body, name=name,
        in_specs=[HBM_SPEC] * (2 * n),
        out_specs=[SEM_SPEC, SEM_SPEC] + [HBM_SPEC] * (2 * n) + [pl.BlockSpec(memory_space=pltpu.VMEM)],
        out_shape=[pltpu.SemaphoreType.DMA((3 * n,)), pltpu.SemaphoreType.DMA((3 * n,))]
        + [pltpu.HBM(g.shape, g.dtype) for g in grads_bf16] + [pltpu.HBM((3,) + sh, BF16) for sh in shapes]
        + [jax.ShapeDtypeStruct((8, 128), F32)],
        input_output_aliases={t: 2 + t for t in range(2 * n)},
        compiler_params=pltpu.CompilerParams(has_side_effects=EFFECT),
    )(*[_hbm(g) for g in grads_bf16], *lands)
    return (res[0], res[1]), list(res[2:2 + n]), list(res[2 + n:2 + 2 * n]), res[-1]


def _scatter_wait(grads_thru, lands_thru, axes, sems, after, name):
    n = len(grads_thru)

    def body(*refs):
        send_sems, recv_sems = refs[2 * n], refs[2 * n + 1]
        outs = refs[2 * n + 3:]
        gb, land = outs[:n], outs[n:]
        _, peers, slots = _chip_peers()
        for t in range(n):
            size = land[t].shape[1 + axes[t]]
            for k in range(3):
                cp = pltpu.make_async_remote_copy(
                    src_ref=_window(gb[t], axes[t], slots[k], size), dst_ref=land[t].at[k],
                    send_sem=send_sems.at[3 * t + k], recv_sem=recv_sems.at[3 * t + k],
                    device_id=peers[k], device_id_type=MESH)
                cp.wait_send()
                cp.wait_recv()

    res = pl.pallas_call(
        body, name=name,
        in_specs=[HBM_SPEC] * (2 * n) + [SEM_SPEC, SEM_SPEC, pl.BlockSpec(memory_space=pl.ANY)],
        out_specs=[HBM_SPEC] * (2 * n),
        out_shape=[pltpu.HBM(g.shape, g.dtype) for g in grads_thru] + [pltpu.HBM(l.shape, l.dtype) for l in lands_thru],
        input_output_aliases={t: t for t in range(2 * n)},
        compiler_params=pltpu.CompilerParams(has_side_effects=EFFECT),
    )(*grads_thru, *lands_thru, sems[0], sems[1], after)
    return list(res[n:])


def _sibling_exchange(arrs, name):
    n = len(arrs)
    any_spec = pl.BlockSpec(memory_space=pl.ANY)

    def body(*refs):
        ins, outs = refs[:n], refs[n:2 * n]
        send_sems, recv_sems = refs[2 * n:]
        sibling = (lax.axis_index("x"), lax.axis_index("y"), 1 - lax.axis_index("c"))
        copies = []
        for t in range(n):
            rc = pltpu.make_async_remote_copy(
                src_ref=ins[t], dst_ref=outs[t], send_sem=send_sems.at[t], recv_sem=recv_sems.at[t],
                device_id=sibling, device_id_type=MESH)
            rc.start()
            copies.append(rc)
        for rc in copies:
            rc.wait_recv()
        for rc in copies:
            rc.wait_send()

    return pl.pallas_call(
        body, name=name,
        in_specs=[any_spec] * n, out_specs=[any_spec] * n,
        out_shape=[jax.ShapeDtypeStruct(a.shape, a.dtype) for a in arrs],
        scratch_shapes=[pltpu.SemaphoreType.DMA((n,)), pltpu.SemaphoreType.DMA((n,))],
    )(*arrs)


def _all_reduce_small(packed, name):
    nc = packed.shape[1]
    vmem = pl.BlockSpec(memory_space=pltpu.VMEM)

    def body(in_ref, out_ref, gbuf, send_sems, recv_sems):
        x, y, c = lax.axis_index("x"), lax.axis_index("y"), lax.axis_index("c")
        me = 4 * x + 2 * y + c
        gbuf[me] = jnp.sum(in_ref[...], axis=0, keepdims=True)
        copies = []
        for k in range(1, 8):
            peer = (x ^ ((k >> 2) & 1), y ^ ((k >> 1) & 1), c ^ (k & 1))
            rc = pltpu.make_async_remote_copy(
                src_ref=gbuf.at[me], dst_ref=gbuf.at[me], send_sem=send_sems.at[k - 1], recv_sem=recv_sems.at[k - 1],
                device_id=peer, device_id_type=MESH)
            rc.start()
            copies.append(rc)
        for k in range(1, 8):
            peer = (x ^ ((k >> 2) & 1), y ^ ((k >> 1) & 1), c ^ (k & 1))
            pltpu.make_async_remote_copy(
                src_ref=gbuf.at[me], dst_ref=gbuf.at[me ^ k], send_sem=send_sems.at[k - 1],
                recv_sem=recv_sems.at[k - 1], device_id=peer, device_id_type=MESH).wait_recv()
        for rc in copies:
            rc.wait_send()
        tot = gbuf[0]
        for d in range(1, 8):
            tot = tot + gbuf[d]
        out_ref[...] = tot

    return pl.pallas_call(
        body, name=name,
        in_specs=[vmem], out_specs=vmem,
        out_shape=jax.ShapeDtypeStruct((1, nc), F32),
        scratch_shapes=[pltpu.VMEM((8, 1, nc), F32), pltpu.SemaphoreType.DMA((7,)), pltpu.SemaphoreType.DMA((7,))],
    )(packed)


def _sum4(g_full, axis, slot_arr, recv, name):
    _, r, c = recv.shape
    tr = min(r, 128)
    nb = r // tr

    def body(slot_ref, own_ref, recv_ref, o_ref):
        acc = own_ref[...]
        for k in range(3):
            acc = acc + recv_ref[k].astype(F32)
        o_ref[...] = acc

    if axis == 0:
        own_map = lambda i, s: (s[0] * nb + i, 0)
    else:
        own_map = lambda i, s: (i, s[0])
    return pl.pallas_call(
        body, name=name,
        grid_spec=pltpu.PrefetchScalarGridSpec(
            num_scalar_prefetch=1, grid=(nb,),
            in_specs=[pl.BlockSpec((tr, c), own_map), pl.BlockSpec((3, tr, c), lambda i, s: (0, i, 0))],
            out_specs=pl.BlockSpec((tr, c), lambda i, s: (i, 0))),
        out_shape=jax.ShapeDtypeStruct((r, c), F32),
        compiler_params=_params("parallel"),
    )(slot_arr, g_full, recv)


def _adamw(w, g_parts, m, v, name):
    r, c = w.shape
    tr = r if r % 128 else 128
    npart = len(g_parts)

    def body(*refs):
        w_ref = refs[0]
        g_refs = refs[1:1 + npart]
        m_ref, v_ref, g_out, d_out, m_out, v_out = refs[1 + npart:]
        g = g_refs[0][...]
        for gr in g_refs[1:]:
            g = g + gr[...]
        mm = ADAM_B1 * m_ref[...] + (1.0 - ADAM_B1) * g
        vv = ADAM_B2 * v_ref[...] + (1.0 - ADAM_B2) * (g * g)
        m_hat = mm / (1.0 - ADAM_B1 ** ADAM_STEP)
        v_hat = vv / (1.0 - ADAM_B2 ** ADAM_STEP)
        g_out[...] = g
        d_out[...] = -ADAM_LR * (m_hat / (jnp.sqrt(v_hat) + ADAM_EPS) + ADAM_WD * w_ref[...])
        m_out[...] = mm
        v_out[...] = vv

    blk = pl.BlockSpec((tr, c), lambda i: (i, 0))
    shp = jax.ShapeDtypeStruct((r, c), F32)
    return pl.pallas_call(
        body, name=name, grid=(r // tr,),
        in_specs=[blk] * (3 + npart), out_specs=[blk] * 4, out_shape=[shp] * 4,
        compiler_params=_params("parallel"),
    )(w, *g_parts, m, v)


def _pad_rows8(w):
    return jnp.pad(w, ((0, HALO - w.shape[0]), (0, 0)))


def kernel(x, mem, hgrn_lb, norm1_w, w_in, hgrn_norm_w, sconv_w, w_out, norm2_w, mem_norm_w, wq, wk, wv, wo, norm3_w, w_gate, w_up, ffn_conv_w, ffn_conv_b, w_down, final_norm_w, loss_target, m_hgrn_lb, m_norm1_w, m_w_in, m_hgrn_norm_w, m_sconv_w, m_w_out, m_norm2_w, m_mem_norm_w, m_wq, m_wk, m_wv, m_wo, m_norm3_w, m_w_gate, m_w_up, m_ffn_conv_w, m_ffn_conv_b, m_w_down, m_final_norm_w, v_hgrn_lb, v_norm1_w, v_w_in, v_hgrn_norm_w, v_sconv_w, v_w_out, v_norm2_w, v_mem_norm_w, v_wq, v_wk, v_wv, v_wo, v_norm3_w, v_w_gate, v_w_up, v_ffn_conv_w, v_ffn_conv_b, v_w_down, v_final_norm_w):
    xs, mems, tgt = x[0], mem[0], loss_target[0]
    d = xs.shape[1]
    fnw = final_norm_w.reshape(1, d)

    big = {"w_in": (w_in[0], 1), "w_out": (w_out[0], 0), "wq": (wq[0], 0), "wk": (wk[0], 0), "wv": (wv[0], 0),
           "wo": (wo[0], 0), "w_gate": (w_gate[0], 1), "w_up": (w_up[0], 1), "w_down": (w_down[0], 0)}
    names = list(big)
    slot_arr = (2 * lax.axis_index("x") + lax.axis_index("y")).astype(jnp.int32).reshape(1)
    gnames = names + ["sconv8", "fconv8"]
    fulls = [_cast_into_full(big[n][0], big[n][1], slot_arr, BF16, "cast_" + n) for n in names]
    fulls += [_cast_into_full(_pad_rows8(sconv_w[0]), 1, slot_arr, F32, "cast_sconv_w"),
              _cast_into_full(_pad_rows8(ffn_conv_w[0]), 1, slot_arr, F32, "cast_ffn_conv_w")]
    axes = [big[n][1] for n in names] + [1, 1]
    groups = [["w_in"], ["w_out", "sconv8"], ["wq", "wk", "wv", "wo"], ["w_gate", "w_up", "fconv8", "w_down"]]
    gidx = [[gnames.index(n) for n in grp] for grp in groups]
    gsems, fulls, tok = _gather_start(fulls, axes, gidx, "gather_start")
    wf = {}

    def gather_wait(g, after):
        got = _gather_wait([fulls[t] for t in gidx[g]], [axes[t] for t in gidx[g]], gsems[g], after,
                           "gather_wait_%d" % g)
        wf.update(zip(groups[g], got))

    lb0, lb1 = hgrn_lb[0:1], hgrn_lb[1:2]

    h1 = _rmsnorm_fwd(xs, norm1_w + tok[0:1, 0:1], "norm1")
    gather_wait(0, h1)
    proj = _matmul(h1, wf["w_in"], "nn", "proj_in", out_dtype=BF16)
    gather_wait(1, proj)
    sconv8 = wf["sconv8"]
    o_h, og, states = _hgrn_fwd(proj, lb0, lb1, hgrn_norm_w, "hgrn_fwd")
    yc = _sconv_fwd(proj, sconv8, "sconv_fwd")
    mix = jnp.concatenate([og, yc], axis=1)
    x1 = _matmul(mix, wf["w_out"], "nn", "proj_out", residual=xs)
    gather_wait(2, x1)
    h2 = _rmsnorm_fwd(x1, norm2_w, "norm2")
    mem_n = _rmsnorm_fwd(mems, mem_norm_w, "norm_mem")
    qa = _matmul(h2, wf["wq"], "nn", "attn_q", out_dtype=BF16)
    ka = _matmul(mem_n, wf["wk"], "nn", "attn_k", out_dtype=BF16)
    va = _matmul(mem_n, wf["wv"], "nn", "attn_v", out_dtype=BF16)
    att = _attn_fwd(qa, ka, va, "attn_fwd")
    x2 = _matmul(att, wf["wo"], "nn", "attn_o", residual=x1)
    gather_wait(3, x2)
    fconv8 = wf["fconv8"]
    h3 = _rmsnorm_fwd(x2, norm3_w, "norm3")
    gate = _matmul(h3, wf["w_gate"], "nn", "ffn_gate", out_dtype=BF16)
    up = _matmul(h3, wf["w_up"], "nn", "ffn_up", out_dtype=BF16)
    z = _ffn_fwd(gate, up, fconv8, ffn_conv_b, "ffn_act")
    x3 = _matmul(z, wf["w_down"], "nn", "ffn_down", residual=x2)

    dx3, dx3b, g_final, loss8 = _final_loss_bwd(x3, tgt, fnw, "loss_bwd")
    gw = {}
    dz = _matmul(dx3b, wf["w_down"], "nt", "d_z", out_dtype=BF16)
    gw["w_down"] = _matmul(z, dx3b, "tn", "g_w_down", extra_bf16=True)
    dgate, du, g_fb, g_fw = _ffn_bwd(gate, up, dz, fconv8, ffn_conv_b, "ffn_act_bwd")
    dh3 = _matmul(dgate, wf["w_gate"], "nt", "d_h3_gate")
    dh3 = _matmul(du, wf["w_up"], "nt", "d_h3_up", residual=dh3)
    gw["w_gate"] = _matmul(h3, dgate, "tn", "g_w_gate", extra_bf16=True)
    gw["w_up"] = _matmul(h3, du, "tn", "g_w_up", extra_bf16=True)
    pending = []

    def scatter_start(grp):
        sems, g_thru, lands, token = _scatter_start([gw[n][1] for n in grp], [big[n][1] for n in grp],
                                                    "scatter_start_" + grp[0])
        pending.append((grp, sems, g_thru, lands))
        return token[0:1, 0:1]

    tok1 = scatter_start(["w_down", "w_gate", "w_up"])
    dx2, dx2b, g_n3 = _rmsnorm_bwd(dh3, x2, norm3_w + tok1, dx3, "norm3_bwd")
    datt = _matmul(dx2b, wf["wo"], "nt", "d_att", out_dtype=BF16)
    gw["wo"] = _matmul(att, dx2b, "tn", "g_wo", extra_bf16=True)
    dqa, dka, dva = _attn_bwd(qa, ka, va, datt, "attn_bwd")
    dh2 = _matmul(dqa, wf["wq"], "nt", "d_h2")
    gw["wq"] = _matmul(h2, dqa, "tn", "g_wq", extra_bf16=True)
    gw["wk"] = _matmul(mem_n, dka, "tn", "g_wk", extra_bf16=True)
    gw["wv"] = _matmul(mem_n, dva, "tn", "g_wv", extra_bf16=True)
    tok2 = scatter_start(["wo", "wq", "wk", "wv"])
    dmem_n = _matmul(dka, wf["wk"], "nt", "d_memn_k")
    dmem_n = _matmul(dva, wf["wv"], "nt", "d_memn_v", residual=dmem_n)
    _, _, g_nm = _rmsnorm_bwd(dmem_n, mems, mem_norm_w, None, "norm_mem_bwd")
    dx1, dx1b, g_n2 = _rmsnorm_bwd(dh2, x1, norm2_w + tok2, dx2, "norm2_bwd")
    dmix = _matmul(dx1b, wf["w_out"], "nt", "d_mix", out_dtype=BF16)
    gw["w_out"] = _matmul(mix, dx1b, "tn", "g_w_out", extra_bf16=True)
    tok3 = scatter_start(["w_out"])
    dcb, dcc, dch, g_sw = _sconv_bwd(proj, sconv8, dmix, "sconv_bwd")
    dq, df, di, dg, g_lb, g_hn = _hgrn_bwd(proj, lb0, lb1, hgrn_norm_w + tok3, o_h, states, dmix, "hgrn_bwd")
    dproj = jnp.concatenate([dq, df, di, dg, dcb, dcc, dch], axis=1)
    gw["w_in"] = _matmul(h1, dproj, "tn", "g_w_in", extra_bf16=True)
    tok4 = scatter_start(["w_in"])
    dh1 = _matmul(dproj, wf["w_in"], "nt", "d_h1")
    dx, _, g_n1 = _rmsnorm_bwd(dh1, xs, norm1_w + tok4, dx1, "norm1_bwd")

    small = [g_n1, g_n2, g_n3, g_final, g_nm, g_lb, g_hn, g_fb,
             g_sw[0:8], g_sw[8:16], g_sw[16:24], g_fw[0:8], g_fw[8:16], g_fw[16:24], loss8]
    widths = [a.shape[1] for a in small]
    tot = _all_reduce_small(jnp.concatenate(small, axis=1), "all_reduce_small")
    offs = [0]
    for wd_ in widths:
        offs.append(offs[-1] + wd_)
    sm = [tot[:, offs[i]:offs[i + 1]] for i in range(len(small))]
    s_n1, s_n2, s_n3, s_final, s_nm, s_lb, s_hn, s_fb = sm[:8]
    s_sw = jnp.concatenate(sm[8:11], axis=0)
    s_fw = jnp.concatenate(sm[11:14], axis=0)
    loss = sm[14][0, 0]
    slot = 2 * lax.axis_index("x") + lax.axis_index("y")
    s_sw = lax.dynamic_slice_in_dim(s_sw, slot * (HGRN_W // N_CHIPS), HGRN_W // N_CHIPS, axis=1)
    fsh = ffn_conv_w.shape[2]
    s_fw = lax.dynamic_slice_in_dim(s_fw, slot * fsh, fsh, axis=1)
    s_lb2 = jnp.concatenate([s_lb, -s_lb], axis=0)

    recv = {}
    for grp, sems, g_thru, lands in pending:
        got = _scatter_wait(g_thru, lands, [big[n][1] for n in grp], sems, tot, "scatter_wait_" + grp[0])
        recv.update(zip(grp, got))
    core_sums = [_sum4(gw[n][0], big[n][1], slot_arr, recv[n], "core_sum_" + n) for n in names]
    sib_sums = _sibling_exchange(core_sums, "sibling_exchange")

    moments = {"hgrn_lb": (m_hgrn_lb, v_hgrn_lb), "norm1_w": (m_norm1_w, v_norm1_w), "w_in": (m_w_in, v_w_in),
               "hgrn_norm_w": (m_hgrn_norm_w, v_hgrn_norm_w), "sconv_w": (m_sconv_w, v_sconv_w),
               "w_out": (m_w_out, v_w_out), "norm2_w": (m_norm2_w, v_norm2_w),
               "mem_norm_w": (m_mem_norm_w, v_mem_norm_w), "wq": (m_wq, v_wq), "wk": (m_wk, v_wk), "wv": (m_wv, v_wv),
               "wo": (m_wo, v_wo), "norm3_w": (m_norm3_w, v_norm3_w), "w_gate": (m_w_gate, v_w_gate),
               "w_up": (m_w_up, v_w_up), "ffn_conv_w": (m_ffn_conv_w, v_ffn_conv_w),
               "ffn_conv_b": (m_ffn_conv_b, v_ffn_conv_b), "w_down": (m_w_down, v_w_down),
               "final_norm_w": (m_final_norm_w, v_final_norm_w)}
    weights = {"hgrn_lb": hgrn_lb, "norm1_w": norm1_w, "w_in": w_in, "hgrn_norm_w": hgrn_norm_w, "sconv_w": sconv_w,
               "w_out": w_out, "norm2_w": norm2_w, "mem_norm_w": mem_norm_w, "wq": wq, "wk": wk, "wv": wv, "wo": wo,
               "norm3_w": norm3_w, "w_gate": w_gate, "w_up": w_up, "ffn_conv_w": ffn_conv_w, "ffn_conv_b": ffn_conv_b,
               "w_down": w_down, "final_norm_w": final_norm_w}
    small_g = {"hgrn_lb": s_lb2, "norm1_w": s_n1, "hgrn_norm_w": s_hn, "sconv_w": s_sw, "norm2_w": s_n2,
               "mem_norm_w": s_nm, "norm3_w": s_n3, "ffn_conv_w": s_fw, "ffn_conv_b": s_fb, "final_norm_w": s_final}
    order = list(weights)
    res = {}
    for n in order:
        w_full = weights[n]
        shape = w_full.shape
        w2 = w_full.reshape((-1, shape[-1]))
        m2, v2 = (t.reshape(w2.shape) for t in moments[n])
        if n in big:
            i = names.index(n)
            parts = [core_sums[i], sib_sums[i]]
        else:
            parts = [small_g[n].reshape(w2.shape)]
        res[n] = [t.reshape(shape) for t in _adamw(w2, parts, m2, v2, "adamw_" + n)]

    return (loss, dx[None], *[res[n][0] for n in order], *[res[n][1] for n in order],
            *[res[n][2] for n in order], *[res[n][3] for n in order])
```

```python
import functools

import jax
import jax.numpy as jnp
from jax import lax
from jax.experimental import pallas as pl
from jax.experimental.pallas import tpu as pltpu

F32 = jnp.float32
BF16 = jnp.bfloat16
MESH = pl.DeviceIdType.MESH

EPS = 1e-6
HGRN_W = 1024
HEAD = 128
N_HEADS = 8
CHUNK = 64
HGRN_UNROLL = 8
HGRN_HEADS_PER_STEP = 2
MEM_HEADS = 4
MEM_HEAD_DIM = 512
N_CHIPS = 4
HALO = 8

ADAM_LR = 0.001
ADAM_B1 = 0.9
ADAM_B2 = 0.999
ADAM_EPS = 1e-08
ADAM_WD = 0.01
ADAM_STEP = 10


def _sigmoid(x):
    return 1.0 / (1.0 + jnp.exp(-x))


def _dot(a, b, dims):
    return lax.dot_general(a.astype(BF16), b.astype(BF16), (dims, ((), ())),
                           preferred_element_type=F32)


def _dot_nn(a, b):
    return _dot(a, b, ((1,), (0,)))


def _dot_nt(a, b):
    return _dot(a, b, ((1,), (1,)))


def _dot_tn(a, b):
    return _dot(a, b, ((0,), (0,)))


def _hdot(a, b, dims):
    return lax.dot_general(a, b, (dims, ((), ())), precision=lax.Precision.HIGH, preferred_element_type=F32)


def _hdot_nn(a, b):
    return _hdot(a, b, ((1,), (0,)))


def _hdot_nt(a, b):
    return _hdot(a, b, ((1,), (1,)))


def _hdot_tn(a, b):
    return _hdot(a, b, ((0,), (0,)))


def _exact_ones_dot(ones_bf16, x):
    hi = x.astype(BF16)
    r1 = x - hi.astype(F32)
    mid = r1.astype(BF16)
    lo = (r1 - mid.astype(F32)).astype(BF16)
    dims = (((1,), (0,)), ((), ()))
    return (lax.dot_general(ones_bf16, hi, dims, preferred_element_type=F32)
            + lax.dot_general(ones_bf16, mid, dims, preferred_element_type=F32)
            + lax.dot_general(ones_bf16, lo, dims, preferred_element_type=F32))


def _rows8(v):
    t, c = v.shape
    return v.reshape(t // 8, 8, c).sum(axis=0)


def _shift_down(x, halo, s):
    rolled = pltpu.roll(x, s, 0)
    hrolled = pltpu.roll(halo, s, 0)
    row = lax.broadcasted_iota(jnp.int32, hrolled.shape, 0)
    head = jnp.where(row < s, hrolled, rolled[:HALO])
    return jnp.concatenate([head, rolled[HALO:]], axis=0)


def _shift_up(x, halo, s):
    t = x.shape[0]
    rolled = pltpu.roll(x, t - s, 0)
    hrolled = pltpu.roll(halo, HALO - s, 0)
    row = lax.broadcasted_iota(jnp.int32, hrolled.shape, 0)
    tail = jnp.where(row >= HALO - s, hrolled, rolled[t - HALO:])
    return jnp.concatenate([rolled[:t - HALO], tail], axis=0)


def _params(*sem):
    return pltpu.CompilerParams(dimension_semantics=sem)


def _row_tile(r, pref):
    while r % pref:
        pref //= 2
    return pref


def _rmsnorm_fwd(x, w, name, tm=256):
    s, d = x.shape
    tm = min(tm, s)

    def body(x_ref, w_ref, o_ref):
        xv = x_ref[...]
        r = lax.rsqrt(jnp.mean(xv * xv, axis=-1, keepdims=True) + EPS)
        o_ref[...] = ((xv * r) * w_ref[...]).astype(BF16)

    return pl.pallas_call(
        body, name=name, grid=(s // tm,),
        in_specs=[pl.BlockSpec((tm, d), lambda i: (i, 0)), pl.BlockSpec((1, d), lambda i: (0, 0))],
        out_specs=pl.BlockSpec((tm, d), lambda i: (i, 0)),
        out_shape=jax.ShapeDtypeStruct((s, d), BF16),
        compiler_params=_params("parallel"),
    )(x, w)


def _rmsnorm_bwd(dh, x, w, dres, name, tm=256):
    s, d = x.shape
    tm = min(tm, s)
    has_res = dres is not None

    def body(*refs):
        if has_res:
            dh_ref, x_ref, w_ref, dres_ref, dx_ref, dxb_ref, gw_ref = refs
        else:
            dh_ref, x_ref, w_ref, dx_ref, dxb_ref, gw_ref = refs

        @pl.when(pl.program_id(0) == 0)
        def _():
            gw_ref[...] = jnp.zeros_like(gw_ref)

        xv = x_ref[...]
        dhv = dh_ref[...].astype(F32)
        r = lax.rsqrt(jnp.mean(xv * xv, axis=-1, keepdims=True) + EPS)
        xhat = xv * r
        gw_ref[...] += _rows8(dhv * xhat)
        dxh = dhv * w_ref[...]
        dx = r * (dxh - xhat * jnp.mean(dxh * xhat, axis=-1, keepdims=True))
        if has_res:
            dx = dres_ref[...] + dx
        dx_ref[...] = dx
        dxb_ref[...] = dx.astype(BF16)

    row = pl.BlockSpec((tm, d), lambda i: (i, 0))
    in_specs = [row, row, pl.BlockSpec((1, d), lambda i: (0, 0))] + ([row] if has_res else [])
    args = (dh, x, w) + ((dres,) if has_res else ())
    return pl.pallas_call(
        body, name=name, grid=(s // tm,),
        in_specs=in_specs,
        out_specs=[row, row, pl.BlockSpec((8, d), lambda i: (0, 0))],
        out_shape=[jax.ShapeDtypeStruct((s, d), F32), jax.ShapeDtypeStruct((s, d), BF16),
                   jax.ShapeDtypeStruct((8, d), F32)],
        compiler_params=_params("arbitrary"),
    )(*args)


def _final_loss_bwd(x3, target, w, name, tm=256):
    s, d = x3.shape
    tm = min(tm, s)

    def body(x_ref, t_ref, w_ref, dx_ref, dxb_ref, gw_ref, loss_ref):
        @pl.when(pl.program_id(0) == 0)
        def _():
            gw_ref[...] = jnp.zeros_like(gw_ref)
            loss_ref[...] = jnp.zeros_like(loss_ref)

        xv = x_ref[...]
        r = lax.rsqrt(jnp.mean(xv * xv, axis=-1, keepdims=True) + EPS)
        xhat = xv * r
        y = xhat * w_ref[...]
        err = y - t_ref[...]
        part = 0.5 * jnp.mean(err * err, axis=-1, keepdims=True)
        tot = jnp.sum(part, axis=0, keepdims=True)
        rr = lax.broadcasted_iota(jnp.int32, loss_ref.shape, 0)
        cc = lax.broadcasted_iota(jnp.int32, loss_ref.shape, 1)
        loss_ref[...] += jnp.where((rr == 0) & (cc == 0), tot, 0.0)
        dy = err * (1.0 / d)
        gw_ref[...] += _rows8(dy * xhat)
        dxh = dy * w_ref[...]
        dx = r * (dxh - xhat * jnp.mean(dxh * xhat, axis=-1, keepdims=True))
        dx_ref[...] = dx
        dxb_ref[...] = dx.astype(BF16)

    row = pl.BlockSpec((tm, d), lambda i: (i, 0))
    return pl.pallas_call(
        body, name=name, grid=(s // tm,),
        in_specs=[row, row, pl.BlockSpec((1, d), lambda i: (0, 0))],
        out_specs=[row, row, pl.BlockSpec((8, d), lambda i: (0, 0)), pl.BlockSpec((8, 128), lambda i: (0, 0))],
        out_shape=[jax.ShapeDtypeStruct((s, d), F32), jax.ShapeDtypeStruct((s, d), BF16),
                   jax.ShapeDtypeStruct((8, d), F32), jax.ShapeDtypeStruct((8, 128), F32)],
        compiler_params=_params("arbitrary"),
    )(x3, target, w)


MM_TILES = (1024, 1408, 512, 256, 128)
MM_K_TILES = (2816, 2048, 1792, 1408, 1024, 512, 256, 128)
MM_VMEM_LIMIT = 48 * 1024 * 1024
MM_VMEM_BUDGET = 40 * 1024 * 1024


def _pick_tile(dim):
    for t in MM_TILES:
        if dim % t == 0:
            return t
    return dim


def _matmul(a, b, mode, name, *, out_dtype=F32, residual=None, extra_bf16=False, tm=None, tn=None, tk=None):
    if mode == "nn":
        (m, k), (k2, n) = a.shape, b.shape
    elif mode == "nt":
        (m, k), (n, k2) = a.shape, b.shape
    else:
        (k, m), (k2, n) = a.shape, b.shape
    assert k == k2, (a.shape, b.shape, mode)
    tm = _pick_tile(m) if tm is None else min(tm, m)
    tn = _pick_tile(n) if tn is None else min(tn, n)
    out_bytes = tm * tn * (jnp.dtype(out_dtype).itemsize + (2 if extra_bf16 else 0) + (4 if residual is not None else 0))

    def vmem_bytes(t):
        return (2 * (tm * t * a.dtype.itemsize + t * tn * b.dtype.itemsize) + 2 * out_bytes
                + (tm * tn * 4 if t < k else 0))

    if tk is None:
        tk = next(t for t in MM_K_TILES if k % t == 0 and t <= k and vmem_bytes(t) <= MM_VMEM_BUDGET)
    assert m % tm == 0 and n % tn == 0 and k % tk == 0, (m, n, k, tm, tn, tk)
    nk = k // tk
    dims = {"nn": ((1,), (0,)), "nt": ((1,), (1,)), "tn": ((0,), (0,))}[mode]
    has_res = residual is not None

    def body(*refs):
        refs = list(refs)
        a_ref, b_ref = refs[0], refs[1]
        r_ref = refs[2] if has_res else None
        outs = refs[2 + has_res:]
        o_ref = outs[0]
        o2_ref = outs[1] if extra_bf16 else None
        def finish(r):
            if has_res:
                r = r_ref[...] + r
            o_ref[...] = r.astype(out_dtype)
            if extra_bf16:
                o2_ref[...] = r.astype(BF16)

        if nk == 1:
            finish(_dot(a_ref[...], b_ref[...], dims))
            return
        acc = outs[-1]
        kk = pl.program_id(2)

        @pl.when(kk == 0)
        def _():
            acc[...] = jnp.zeros_like(acc)

        acc[...] += _dot(a_ref[...], b_ref[...], dims)

        @pl.when(kk == nk - 1)
        def _():
            finish(acc[...])

    if mode == "tn":
        a_spec = pl.BlockSpec((tk, tm), lambda i, j, kk: (kk, i))
    else:
        a_spec = pl.BlockSpec((tm, tk), lambda i, j, kk: (i, kk))
    if mode == "nt":
        b_spec = pl.BlockSpec((tn, tk), lambda i, j, kk: (j, kk))
    else:
        b_spec = pl.BlockSpec((tk, tn), lambda i, j, kk: (kk, j))
    o_spec = pl.BlockSpec((tm, tn), lambda i, j, kk: (i, j))
    in_specs = [a_spec, b_spec] + ([o_spec] if has_res else [])
    out_specs = [o_spec] + ([o_spec] if extra_bf16 else [])
    out_shape = [jax.ShapeDtypeStruct((m, n), out_dtype)] + ([jax.ShapeDtypeStruct((m, n), BF16)] if extra_bf16 else [])
    args = (a, b) + ((residual,) if has_res else ())
    res = pl.pallas_call(
        body, name=name, grid=(m // tm, n // tn, nk),
        in_specs=in_specs, out_specs=out_specs, out_shape=out_shape,
        scratch_shapes=[pltpu.VMEM((tm, tn) if nk > 1 else (8, 128), F32)],
        compiler_params=pltpu.CompilerParams(dimension_semantics=("parallel", "parallel", "arbitrary"),
                                             vmem_limit_bytes=MM_VMEM_LIMIT),
    )(*args)
    return res if extra_bf16 else res[0]


def _hgrn_gates(qp, fp, lb):
    sig = _sigmoid(fp)
    f = lb + (1.0 - lb) * sig
    logf = jnp.log(f)
    k = 1.0 - f
    sq = _sigmoid(qp)
    q = qp * sq
    return sig, f, logf, k, sq, q


def _hgrn_fwd(proj, lb0, lb1, norm_w, name, tb=512):
    s = proj.shape[0]
    tb = min(tb, s)
    nb, ncb = s // tb, tb // CHUNK

    def body(q_ref, f_ref, i_ref, g_ref, a0_ref, a1_ref, nw_ref, o_ref, og_ref, st_ref, state):
        @pl.when(pl.program_id(1) == 0)
        def _():
            state[...] = jnp.zeros_like(state)

        lb2 = _sigmoid(a0_ref[...] - a1_ref[...])
        row = lax.broadcasted_iota(jnp.int32, (CHUNK, CHUNK), 0)
        col = lax.broadcasted_iota(jnp.int32, (CHUNK, CHUNK), 1)
        tril = row >= col
        ones_l = tril.astype(BF16)
        nw = nw_ref[...]

        def chunk(c, carry):
            rows = pl.ds(pl.multiple_of(c * CHUNK, CHUNK), CHUNK)
            for hh in range(HGRN_HEADS_PER_STEP):
                cols = slice(hh * HEAD, (hh + 1) * HEAD)
                v = i_ref[rows, cols].astype(F32)
                _, _, logf, k, _, q = _hgrn_gates(q_ref[rows, cols].astype(F32), f_ref[rows, cols].astype(F32),
                                                  lb2[:, cols])
                b = _exact_ones_dot(ones_l, logf)
                bl = jnp.sum(logf, axis=0, keepdims=True)
                bm = 0.5 * bl
                st = state[hh]
                st_ref[hh, c] = st
                qt = q * jnp.exp(b - bm)
                kt = k * jnp.exp(bm - b)
                a = jnp.where(tril, _dot_nt(qt, kt), 0.0)
                o = _dot_nt(q * jnp.exp(b), st) + _dot_nn(a, v)
                state[hh] = st * jnp.exp(bl) + _dot_tn(v, k * jnp.exp(bl - b))
                o_ref[rows, cols] = o
                on = (o * lax.rsqrt(jnp.mean(o * o, axis=-1, keepdims=True) + EPS)) * nw
                gv = g_ref[rows, cols].astype(F32)
                og_ref[rows, cols] = (on * (gv * _sigmoid(gv))).astype(BF16)
            return carry

        lax.fori_loop(0, ncb, chunk, 0, unroll=HGRN_UNROLL)

    hp, wd = HGRN_HEADS_PER_STEP, HGRN_HEADS_PER_STEP * HEAD
    ngrp = N_HEADS // hp

    def colblk(group):
        return pl.BlockSpec((tb, wd), lambda h, j: (j, group * ngrp + h))

    vec = pl.BlockSpec((1, wd), lambda h, j: (0, h))
    out_blk = pl.BlockSpec((tb, wd), lambda h, j: (j, h))
    return pl.pallas_call(
        body, name=name, grid=(ngrp, nb),
        in_specs=[colblk(0), colblk(1), colblk(2), colblk(3), vec, vec, pl.BlockSpec((1, HEAD), lambda h, j: (0, 0))],
        out_specs=[out_blk, out_blk, pl.BlockSpec((hp, ncb, HEAD, HEAD), lambda h, j: (h, j, 0, 0))],
        out_shape=[jax.ShapeDtypeStruct((s, HGRN_W), F32), jax.ShapeDtypeStruct((s, HGRN_W), BF16),
                   jax.ShapeDtypeStruct((N_HEADS, s // CHUNK, HEAD, HEAD), F32)],
        scratch_shapes=[pltpu.VMEM((hp, HEAD, HEAD), F32)],
        compiler_params=_params("parallel", "arbitrary"),
    )(proj, proj, proj, proj, lb0, lb1, norm_w)


def _hgrn_bwd(proj, lb0, lb1, norm_w, o, states, dmix, name, tb=512):
    s = proj.shape[0]
    tb = min(tb, s)
    nb, ncb = s // tb, tb // CHUNK

    def body(q_ref, f_ref, i_ref, g_ref, a0_ref, a1_ref, nw_ref, o_ref, st_ref, dm_ref,
             dq_ref, df_ref, di_ref, dg_ref, glb_ref, gnw_ref, dstate):
        h = pl.program_id(0)

        @pl.when(pl.program_id(1) == 0)
        def _():
            dstate[...] = jnp.zeros_like(dstate)
            glb_ref[...] = jnp.zeros_like(glb_ref)

        @pl.when((pl.program_id(1) == 0) & (h == 0))
        def _():
            gnw_ref[...] = jnp.zeros_like(gnw_ref)

        lb2 = _sigmoid(a0_ref[...] - a1_ref[...])
        row = lax.broadcasted_iota(jnp.int32, (CHUNK, CHUNK), 0)
        col = lax.broadcasted_iota(jnp.int32, (CHUNK, CHUNK), 1)
        tril = row >= col
        ones_l = tril.astype(BF16)
        ones_u = (row <= col).astype(BF16)
        nw = nw_ref[...]

        def chunk(cc, carry):
            c = ncb - 1 - cc
            rows = pl.ds(pl.multiple_of(c * CHUNK, CHUNK), CHUNK)
            for hh in range(HGRN_HEADS_PER_STEP):
                cols = slice(hh * HEAD, (hh + 1) * HEAD)
                lb = lb2[:, cols]
                qp = q_ref[rows, cols].astype(F32)
                v = i_ref[rows, cols].astype(F32)
                sig, f, logf, k, sq, q = _hgrn_gates(qp, f_ref[rows, cols].astype(F32), lb)
                gv = g_ref[rows, cols].astype(F32)
                sg = _sigmoid(gv)
                silu_g = gv * sg
                dog = dm_ref[rows, cols].astype(F32)
                ov = o_ref[rows, cols]
                r = lax.rsqrt(jnp.mean(ov * ov, axis=-1, keepdims=True) + EPS)
                ohat = ov * r
                on = ohat * nw
                dg_ref[rows, cols] = (dog * on * (sg * (1.0 + gv * (1.0 - sg)))).astype(BF16)
                don = dog * silu_g
                gnw_ref[...] += _rows8(don * ohat)
                doh = don * nw
                do = r * (doh - ohat * jnp.mean(doh * ohat, axis=-1, keepdims=True))
                b = _exact_ones_dot(ones_l, logf)
                bl = jnp.sum(logf, axis=0, keepdims=True)
                bm = 0.5 * bl
                e_q = jnp.exp(b - bm)
                e_k = jnp.exp(bm - b)
                e_b = jnp.exp(b)
                e_l = jnp.exp(bl - b)
                qt, kt, qb, kb = q * e_q, k * e_k, q * e_b, k * e_l
                st0 = st_ref[hh, c]
                dst = dstate[hh]
                a = jnp.where(tril, _dot_nt(qt, kt), 0.0)
                da = jnp.where(tril, _dot_nt(do, v), 0.0)
                dq = _hdot_nn(da, kt) * e_q + _hdot_nn(do, st0) * e_b
                dkb = _hdot_nn(v, dst) * e_l
                dk = _hdot_tn(da, qt) * e_k + dkb
                dv = _dot_tn(a, do) + _dot_nt(kb, dst)
                e_bl = jnp.exp(bl)
                dstate[hh] = dst * e_bl + _dot_tn(do, qb)
                db = q * dq - k * dk
                db_last = jnp.sum(k * dkb, axis=0, keepdims=True) + e_bl * jnp.sum(st0 * dst, axis=0, keepdims=True)
                dlogf = _exact_ones_dot(ones_u, db) + db_last
                dfg = dlogf / f - dk
                df_ref[rows, cols] = (dfg * (1.0 - lb) * (sig * (1.0 - sig))).astype(BF16)
                glb_ref[:, cols] += _rows8(dfg * (1.0 - sig)) * (lb * (1.0 - lb))
                dq_ref[rows, cols] = (dq * (sq * (1.0 + qp * (1.0 - sq)))).astype(BF16)
                di_ref[rows, cols] = dv.astype(BF16)
            return carry

        lax.fori_loop(0, ncb, chunk, 0, unroll=HGRN_UNROLL)

    hp, wd = HGRN_HEADS_PER_STEP, HGRN_HEADS_PER_STEP * HEAD
    ngrp = N_HEADS // hp

    def colblk(group):
        return pl.BlockSpec((tb, wd), lambda h, j: (nb - 1 - j, group * ngrp + h))

    vec = pl.BlockSpec((1, wd), lambda h, j: (0, h))
    blk = pl.BlockSpec((tb, wd), lambda h, j: (nb - 1 - j, h))
    grad = jax.ShapeDtypeStruct((s, HGRN_W), BF16)
    return pl.pallas_call(
        body, name=name, grid=(ngrp, nb),
        in_specs=[colblk(0), colblk(1), colblk(2), colblk(3), vec, vec, pl.BlockSpec((1, HEAD), lambda h, j: (0, 0)),
                  blk, pl.BlockSpec((hp, ncb, HEAD, HEAD), lambda h, j: (h, nb - 1 - j, 0, 0)), blk],
        out_specs=[blk, blk, blk, blk, pl.BlockSpec((8, wd), lambda h, j: (0, h)),
                   pl.BlockSpec((8, HEAD), lambda h, j: (0, 0))],
        out_shape=[grad, grad, grad, grad, jax.ShapeDtypeStruct((8, HGRN_W), F32), jax.ShapeDtypeStruct((8, HEAD), F32)],
        scratch_shapes=[pltpu.VMEM((hp, HEAD, HEAD), F32)],
        compiler_params=_params("arbitrary", "arbitrary"),
    )(proj, proj, proj, proj, lb0, lb1, norm_w, o, states, dmix)


HALO_BLK = 16


def _f32(ref):
    return ref[...].astype(F32)


def _halo_prev(ref):
    return ref[...].astype(F32)[HALO_BLK - HALO:]


def _halo_next(ref):
    return ref[...].astype(F32)[:HALO]


def _conv3(x0, x1, x2, w_ref):
    y = x0 * w_ref[0:1, :]
    y = y + x1 * w_ref[1:2, :]
    return y + x2 * w_ref[2:3, :]


def _sconv_fwd(proj, w8, name, tb=256):
    s = proj.shape[0]
    tb = min(tb, s)
    hb = tb // HALO_BLK

    def body(cb_ref, cc_ref, ch_ref, cch_ref, chh_ref, w_ref, y_ref):
        first = pl.program_id(0) == 0
        u = _f32(cc_ref) * _f32(ch_ref)
        uh = jnp.where(first, 0.0, _halo_prev(cch_ref) * _halo_prev(chh_ref))
        conv = _conv3(_shift_down(u, uh, 2), _shift_down(u, uh, 1), u, w_ref)
        y_ref[...] = (_f32(cb_ref) * conv).astype(BF16)

    def blk(g):
        return pl.BlockSpec((tb, HGRN_W), lambda j: (j, g))

    def halo(g):
        return pl.BlockSpec((HALO_BLK, HGRN_W), lambda j: (jnp.maximum(j * hb - 1, 0), g))

    return pl.pallas_call(
        body, name=name, grid=(s // tb,),
        in_specs=[blk(4), blk(5), blk(6), halo(5), halo(6), pl.BlockSpec((HALO, HGRN_W), lambda j: (0, 0))],
        out_specs=pl.BlockSpec((tb, HGRN_W), lambda j: (j, 0)),
        out_shape=jax.ShapeDtypeStruct((s, HGRN_W), BF16),
        compiler_params=_params("parallel"),
    )(proj, proj, proj, proj, proj, w8)


def _sconv_bwd(proj, w8, dmix, name, tb=256):
    s = proj.shape[0]
    tb = min(tb, s)
    hb = tb // HALO_BLK
    nb = s // tb
    last_h = s // HALO_BLK - 1

    def body(cb_ref, cc_ref, ch_ref, cch_ref, chh_ref, cbn_ref, dy_ref, dyn_ref, w_ref,
             dcb_ref, dcc_ref, dch_ref, gw_ref):
        j = pl.program_id(0)

        @pl.when(j == 0)
        def _():
            gw_ref[...] = jnp.zeros_like(gw_ref)

        cc, ch, cb = _f32(cc_ref), _f32(ch_ref), _f32(cb_ref)
        u = cc * ch
        uh = jnp.where(j == 0, 0.0, _halo_prev(cch_ref) * _halo_prev(chh_ref))
        u2, u1 = _shift_down(u, uh, 2), _shift_down(u, uh, 1)
        conv = _conv3(u2, u1, u, w_ref)
        dy = _f32(dy_ref)
        dcb_ref[...] = (dy * conv).astype(BF16)
        dc = dy * cb
        dcn = jnp.where(j == nb - 1, 0.0, _halo_next(dyn_ref) * _halo_next(cbn_ref))
        gw_ref[0:8, :] += _rows8(dc * u2)
        gw_ref[8:16, :] += _rows8(dc * u1)
        gw_ref[16:24, :] += _rows8(dc * u)
        du = dc * w_ref[2:3, :] + _shift_up(dc, dcn, 1) * w_ref[1:2, :] + _shift_up(dc, dcn, 2) * w_ref[0:1, :]
        dcc_ref[...] = (du * ch).astype(BF16)
        dch_ref[...] = (du * cc).astype(BF16)

    def blk(g):
        return pl.BlockSpec((tb, HGRN_W), lambda j: (j, g))

    def halo_prev(g):
        return pl.BlockSpec((HALO_BLK, HGRN_W), lambda j: (jnp.maximum(j * hb - 1, 0), g))

    def halo_next(g):
        return pl.BlockSpec((HALO_BLK, HGRN_W), lambda j: (jnp.minimum((j + 1) * hb, last_h), g))

    out = pl.BlockSpec((tb, HGRN_W), lambda j: (j, 0))
    grad = jax.ShapeDtypeStruct((s, HGRN_W), BF16)
    return pl.pallas_call(
        body, name=name, grid=(nb,),
        in_specs=[blk(4), blk(5), blk(6), halo_prev(5), halo_prev(6), halo_next(4), blk(1), halo_next(1),
                  pl.BlockSpec((HALO, HGRN_W), lambda j: (0, 0))],
        out_specs=[out, out, out, pl.BlockSpec((24, HGRN_W), lambda j: (0, 0))],
        out_shape=[grad, grad, grad, jax.ShapeDtypeStruct((24, HGRN_W), F32)],
        compiler_params=_params("arbitrary"),
    )(proj, proj, proj, proj, proj, proj, dmix, dmix, w8)


def _attn_fwd(q, kk, vv, name, tb=256):
    s, d = q.shape
    m = kk.shape[0]
    tb = min(tb, s)
    scale = MEM_HEAD_DIM ** -0.5

    def body(q_ref, k_ref, v_ref, o_ref):
        for hh in range(MEM_HEADS):
            cols = slice(hh * MEM_HEAD_DIM, (hh + 1) * MEM_HEAD_DIM)
            sc = _dot_nt(q_ref[:, cols], k_ref[:, cols]) * scale
            sc = sc - jnp.max(sc, axis=-1, keepdims=True)
            e = jnp.exp(sc)
            p = e / jnp.sum(e, axis=-1, keepdims=True)
            o_ref[:, cols] = _dot_nn(p, v_ref[:, cols]).astype(BF16)

    full = pl.BlockSpec((m, d), lambda i: (0, 0))
    return pl.pallas_call(
        body, name=name, grid=(s // tb,),
        in_specs=[pl.BlockSpec((tb, d), lambda i: (i, 0)), full, full],
        out_specs=pl.BlockSpec((tb, d), lambda i: (i, 0)),
        out_shape=jax.ShapeDtypeStruct((s, d), BF16),
        compiler_params=_params("parallel"),
    )(q, kk, vv)


def _attn_bwd(q, kk, vv, datt, name, tb=256):
    s, d = q.shape
    m = kk.shape[0]
    tb = min(tb, s)
    scale = MEM_HEAD_DIM ** -0.5

    def body(q_ref, k_ref, v_ref, do_ref, dq_ref, dk_ref, dv_ref):
        @pl.when(pl.program_id(0) == 0)
        def _():
            dk_ref[...] = jnp.zeros_like(dk_ref)
            dv_ref[...] = jnp.zeros_like(dv_ref)

        for hh in range(MEM_HEADS):
            cols = slice(hh * MEM_HEAD_DIM, (hh + 1) * MEM_HEAD_DIM)
            qh, kh, vh, doh = q_ref[:, cols], k_ref[:, cols], v_ref[:, cols], do_ref[:, cols]
            sc = _dot_nt(qh, kh) * scale
            sc = sc - jnp.max(sc, axis=-1, keepdims=True)
            e = jnp.exp(sc)
            p = e / jnp.sum(e, axis=-1, keepdims=True)
            dp = _dot_nt(doh, vh)
            ds = p * (dp - jnp.sum(dp * p, axis=-1, keepdims=True)) * scale
            dq_ref[:, cols] = _dot_nn(ds, kh).astype(BF16)
            dk_ref[:, cols] += _dot_tn(ds, qh)
            dv_ref[:, cols] += _dot_tn(p, doh)

    full = pl.BlockSpec((m, d), lambda i: (0, 0))
    row = pl.BlockSpec((tb, d), lambda i: (i, 0))
    return pl.pallas_call(
        body, name=name, grid=(s // tb,),
        in_specs=[row, full, full, row],
        out_specs=[row, full, full],
        out_shape=[jax.ShapeDtypeStruct((s, d), BF16), jax.ShapeDtypeStruct((m, d), F32),
                   jax.ShapeDtypeStruct((m, d), F32)],
        compiler_params=_params("arbitrary"),
    )(q, kk, vv, datt)


def _ffn_fwd(g, u, w8, bias, name, tb=512, tc=1408):
    s, f = g.shape
    tb = min(tb, s)
    tc = tc if f % tc == 0 else 512
    hb = tb // HALO_BLK

    def body(g_ref, gh_ref, u_ref, w_ref, b_ref, z_ref):
        gv = _f32(g_ref)
        gh = jnp.where(pl.program_id(1) == 0, 0.0, _halo_prev(gh_ref))
        a = _conv3(_shift_down(gv, gh, 2), _shift_down(gv, gh, 1), gv, w_ref) + b_ref[...]
        z_ref[...] = ((a * _sigmoid(a)) * _f32(u_ref)).astype(BF16)

    blk = pl.BlockSpec((tb, tc), lambda c, j: (j, c))
    return pl.pallas_call(
        body, name=name, grid=(f // tc, s // tb),
        in_specs=[blk, pl.BlockSpec((HALO_BLK, tc), lambda c, j: (jnp.maximum(j * hb - 1, 0), c)), blk,
                  pl.BlockSpec((HALO, tc), lambda c, j: (0, c)), pl.BlockSpec((1, tc), lambda c, j: (0, c))],
        out_specs=blk,
        out_shape=jax.ShapeDtypeStruct((s, f), BF16),
        compiler_params=pltpu.CompilerParams(dimension_semantics=("parallel", "parallel"),
                                             vmem_limit_bytes=MM_VMEM_LIMIT),
    )(g, g, u, w8, bias)


def _ffn_bwd(g, u, dz, w8, bias, name, tb=256, tc=1408):
    s, f = g.shape
    tb = min(tb, s)
    tc = tc if f % tc == 0 else 512
    hb = tb // HALO_BLK
    nb = s // tb

    def body(g_ref, gh_ref, u_ref, dz_ref, w_ref, b_ref, dg_ref, du_ref, gb_ref, gw_ref, da_next):
        jj = pl.program_id(1)

        @pl.when(jj == 0)
        def _():
            gb_ref[...] = jnp.zeros_like(gb_ref)
            gw_ref[...] = jnp.zeros_like(gw_ref)
            da_next[...] = jnp.zeros_like(da_next)

        gv = _f32(g_ref)
        gh = jnp.where(jj == nb - 1, 0.0, _halo_prev(gh_ref))
        g2, g1 = _shift_down(gv, gh, 2), _shift_down(gv, gh, 1)
        a = _conv3(g2, g1, gv, w_ref) + b_ref[...]
        sa = _sigmoid(a)
        dz = _f32(dz_ref)
        du_ref[...] = (dz * (a * sa)).astype(BF16)
        da = dz * _f32(u_ref) * (sa * (1.0 + a * (1.0 - sa)))
        gb_ref[...] += _rows8(da)
        gw_ref[0:8, :] += _rows8(da * g2)
        gw_ref[8:16, :] += _rows8(da * g1)
        gw_ref[16:24, :] += _rows8(da * gv)
        dan = da_next[...]
        dg = da * w_ref[2:3, :] + _shift_up(da, dan, 1) * w_ref[1:2, :] + _shift_up(da, dan, 2) * w_ref[0:1, :]
        dg_ref[...] = dg.astype(BF16)
        da_next[...] = da[:HALO]

    blk = pl.BlockSpec((tb, tc), lambda c, jj: (nb - 1 - jj, c))
    return pl.pallas_call(
        body, name=name, grid=(f // tc, nb),
        in_specs=[blk, pl.BlockSpec((HALO_BLK, tc), lambda c, jj: (jnp.maximum((nb - 1 - jj) * hb - 1, 0), c)), blk, blk,
                  pl.BlockSpec((HALO, tc), lambda c, jj: (0, c)), pl.BlockSpec((1, tc), lambda c, jj: (0, c))],
        out_specs=[blk, blk, pl.BlockSpec((8, tc), lambda c, jj: (0, c)), pl.BlockSpec((24, tc), lambda c, jj: (0, c))],
        out_shape=[jax.ShapeDtypeStruct((s, f), BF16), jax.ShapeDtypeStruct((s, f), BF16),
                   jax.ShapeDtypeStruct((8, f), F32), jax.ShapeDtypeStruct((24, f), F32)],
        scratch_shapes=[pltpu.VMEM((HALO, tc), F32)],
        compiler_params=pltpu.CompilerParams(dimension_semantics=("parallel", "arbitrary"),
                                             vmem_limit_bytes=MM_VMEM_LIMIT),
    )(g, g, u, dz, w8, bias)


def _window(ref, axis, slot, size):
    start = pl.multiple_of(slot * size, size)
    if axis == 0:
        return ref.at[pl.ds(start, size), :]
    return ref.at[:, pl.ds(start, size)]


def _chip_peers():
    x, y, c = lax.axis_index("x"), lax.axis_index("y"), lax.axis_index("c")
    peers = [(1 - x, y, c), (x, 1 - y, c), (1 - x, 1 - y, c)]
    slots = [2 * (1 - x) + y, 2 * x + (1 - y), 2 * (1 - x) + (1 - y)]
    return 2 * x + y, peers, slots


HBM_SPEC = pl.BlockSpec(memory_space=pltpu.HBM)
SEM_SPEC = pl.BlockSpec(memory_space=pltpu.SEMAPHORE)
EFFECT = pltpu.SideEffectType.DATAFLOW_SIDE_EFFECTING


def _hbm(a):
    return pltpu.with_memory_space_constraint(a, pltpu.HBM)


def _cast_into_full(x, axis, slot_arr, dtype, name):
    r, c = x.shape
    tr = _row_tile(r, 256)
    nb = r // tr
    full = (r * N_CHIPS, c) if axis == 0 else (r, c * N_CHIPS)

    def body(slot_ref, x_ref, o_ref):
        o_ref[...] = x_ref[...].astype(dtype)

    if axis == 0:
        out_map = lambda i, s: (s[0] * nb + i, 0)
    else:
        out_map = lambda i, s: (i, s[0])
    return pl.pallas_call(
        body, name=name,
        grid_spec=pltpu.PrefetchScalarGridSpec(
            num_scalar_prefetch=1, grid=(nb,),
            in_specs=[pl.BlockSpec((tr, c), lambda i, s: (i, 0))],
            out_specs=pl.BlockSpec((tr, c), out_map)),
        out_shape=jax.ShapeDtypeStruct(full, dtype),
        compiler_params=_params("parallel"),
    )(slot_arr, x)


def _piece(ref, axis, slot, half):
    size = ref.shape[axis] // N_CHIPS
    if half is None:
        return _window(ref, axis, slot, size)
    if axis == 0:
        h = size // 2
        return ref.at[pl.ds(pl.multiple_of(slot * size + half * h, h), h), :]
    h = ref.shape[0] // 2
    return ref.at[pl.ds(pl.multiple_of(half * h, h), h), pl.ds(pl.multiple_of(slot * size, size), size)]


def _gather_start(fulls, axes, split, groups, name):
    n, ng = len(fulls), len(groups)

    def body(*refs):
        outs = refs[n:]
        sems = outs[:2 * ng]
        thru = outs[2 * ng:2 * ng + n]
        token = outs[-1]
        slot, peers, _ = _chip_peers()
        c = lax.axis_index("c")
        for g, members in enumerate(groups):
            for i, t in enumerate(members):
                mine = _piece(thru[t], axes[t], slot, c if split[t] else None)
                for k in range(3):
                    pltpu.make_async_remote_copy(
                        src_ref=mine, dst_ref=mine, send_sem=sems[2 * g].at[3 * i + k],
                        recv_sem=sems[2 * g + 1].at[3 * i + k], device_id=peers[k], device_id_type=MESH).start()
        token[...] = jnp.zeros_like(token)

    sem_shapes = []
    for members in groups:
        sem_shapes += [pltpu.SemaphoreType.DMA((3 * len(members),))] * 2
    res = pl.pallas_call(
        body, name=name,
        in_specs=[HBM_SPEC] * n,
        out_specs=[SEM_SPEC] * (2 * ng) + [HBM_SPEC] * n + [pl.BlockSpec(memory_space=pltpu.VMEM)],
        out_shape=sem_shapes + [pltpu.HBM(f.shape, f.dtype) for f in fulls] + [jax.ShapeDtypeStruct((8, 128), F32)],
        input_output_aliases={t: 2 * ng + t for t in range(n)},
        compiler_params=pltpu.CompilerParams(has_side_effects=EFFECT),
    )(*[_hbm(f) for f in fulls])
    sems = [(res[2 * g], res[2 * g + 1]) for g in range(ng)]
    return sems, list(res[2 * ng:2 * ng + n]), res[-1]


def _gather_relay(fulls, axes, split, sems, after, name):
    n = len(fulls)
    nsplit = sum(split)

    def body(*refs):
        send_sems, recv_sems = refs[n], refs[n + 1]
        outs = refs[n + 3:]
        d_send, d_recv = outs[0], outs[1]
        thru = outs[2:2 + n]
        token = outs[-1]
        slot, peers, slots = _chip_peers()
        c = lax.axis_index("c")
        sibling = (lax.axis_index("x"), lax.axis_index("y"), 1 - c)
        for t in range(n):
            half = c if split[t] else None
            for k in range(3):
                cp = pltpu.make_async_remote_copy(
                    src_ref=_piece(thru[t], axes[t], slot, half), dst_ref=_piece(thru[t], axes[t], slots[k], half),
                    send_sem=send_sems.at[3 * t + k], recv_sem=recv_sems.at[3 * t + k],
                    device_id=peers[k], device_id_type=MESH)
                cp.wait_send()
                cp.wait_recv()
        i = 0
        for t in range(n):
            if not split[t]:
                continue
            for k in range(3):
                got = _piece(thru[t], axes[t], slots[k], c)
                pltpu.make_async_remote_copy(
                    src_ref=got, dst_ref=got, send_sem=d_send.at[3 * i + k], recv_sem=d_recv.at[3 * i + k],
                    device_id=sibling, device_id_type=MESH).start()
            i += 1
        token[...] = jnp.zeros_like(token)

    res = pl.pallas_call(
        body, name=name,
        in_specs=[HBM_SPEC] * n + [SEM_SPEC, SEM_SPEC, pl.BlockSpec(memory_space=pl.ANY)],
        out_specs=[SEM_SPEC, SEM_SPEC] + [HBM_SPEC] * n + [pl.BlockSpec(memory_space=pltpu.VMEM)],
        out_shape=[pltpu.SemaphoreType.DMA((3 * nsplit,)), pltpu.SemaphoreType.DMA((3 * nsplit,))]
        + [pltpu.HBM(f.shape, f.dtype) for f in fulls] + [jax.ShapeDtypeStruct((8, 128), F32)],
        input_output_aliases={t: 2 + t for t in range(n)},
        compiler_params=pltpu.CompilerParams(has_side_effects=EFFECT),
    )(*fulls, sems[0], sems[1], after)
    return (res[0], res[1]), list(res[2:2 + n]), res[-1]


def _gather_finish(fulls, axes, split, sems, after, name):
    n = len(fulls)

    def body(*refs):
        d_send, d_recv = refs[n], refs[n + 1]
        thru = refs[n + 3:]
        _, _, slots = _chip_peers()
        c = lax.axis_index("c")
        sibling = (lax.axis_index("x"), lax.axis_index("y"), 1 - c)
        i = 0
        for t in range(n):
            if not split[t]:
                continue
            for k in range(3):
                cp = pltpu.make_async_remote_copy(
                    src_ref=_piece(thru[t], axes[t], slots[k], c), dst_ref=_piece(thru[t], axes[t], slots[k], 1 - c),
                    send_sem=d_send.at[3 * i + k], recv_sem=d_recv.at[3 * i + k],
                    device_id=sibling, device_id_type=MESH)
                cp.wait_send()
                cp.wait_recv()
            i += 1

    return pl.pallas_call(
        body, name=name,
        in_specs=[HBM_SPEC] * n + [SEM_SPEC, SEM_SPEC, pl.BlockSpec(memory_space=pl.ANY)],
        out_specs=[HBM_SPEC] * n,
        out_shape=[pltpu.HBM(f.shape, f.dtype) for f in fulls],
        input_output_aliases={t: t for t in range(n)},
        compiler_params=pltpu.CompilerParams(has_side_effects=EFFECT),
    )(*fulls, sems[0], sems[1], after)


def _scatter_start(grads_bf16, axes, name):
    n = len(grads_bf16)

    def shard_shape(g, ax):
        return (g.shape[0] // N_CHIPS, g.shape[1]) if ax == 0 else (g.shape[0], g.shape[1] // N_CHIPS)

    shapes = [shard_shape(g, ax) for g, ax in zip(grads_bf16, axes)]

    def body(*refs):
        outs = refs[2 * n:]
        send_sems, recv_sems = outs[0], outs[1]
        gb, land = outs[2:2 + n], outs[2 + n:2 + 2 * n]
        token = outs[-1]
        _, peers, slots = _chip_peers()
        for t in range(n):
            size = shapes[t][axes[t]]
            for k in range(3):
                pltpu.make_async_remote_copy(
                    src_ref=_window(gb[t], axes[t], slots[k], size), dst_ref=land[t].at[k],
                    send_sem=send_sems.at[3 * t + k], recv_sem=recv_sems.at[3 * t + k],
                    device_id=peers[k], device_id_type=MESH).start()
        token[...] = jnp.zeros_like(token)

    lands = [_hbm(lax.empty((3,) + sh, BF16)) for sh in shapes]
    res = pl.pallas_call(
        body, name=name,
        in_specs=[HBM_SPEC] * (2 * n),
        out_specs=[SEM_SPEC, SEM_SPEC] + [HBM_SPEC] * (2 * n) + [pl.BlockSpec(memory_space=pltpu.VMEM)],
        out_shape=[pltpu.SemaphoreType.DMA((3 * n,)), pltpu.SemaphoreType.DMA((3 * n,))]
        + [pltpu.HBM(g.shape, g.dtype) for g in grads_bf16] + [pltpu.HBM((3,) + sh, BF16) for sh in shapes]
        + [jax.ShapeDtypeStruct((8, 128), F32)],
        input_output_aliases={t: 2 + t for t in range(2 * n)},
        compiler_params=pltpu.CompilerParams(has_side_effects=EFFECT),
    )(*[_hbm(g) for g in grads_bf16], *lands)
    return (res[0], res[1]), list(res[2:2 + n]), list(res[2 + n:2 + 2 * n]), res[-1]


def _scatter_wait(grads_thru, lands_thru, axes, sems, after, name):
    n = len(grads_thru)

    def body(*refs):
        send_sems, recv_sems = refs[2 * n], refs[2 * n + 1]
        outs = refs[2 * n + 3:]
        gb, land = outs[:n], outs[n:]
        _, peers, slots = _chip_peers()
        for t in range(n):
            size = land[t].shape[1 + axes[t]]
            for k in range(3):
                cp = pltpu.make_async_remote_copy(
                    src_ref=_window(gb[t], axes[t], slots[k], size), dst_ref=land[t].at[k],
                    send_sem=send_sems.at[3 * t + k], recv_sem=recv_sems.at[3 * t + k],
                    device_id=peers[k], device_id_type=MESH)
                cp.wait_send()
                cp.wait_recv()

    res = pl.pallas_call(
        body, name=name,
        in_specs=[HBM_SPEC] * (2 * n) + [SEM_SPEC, SEM_SPEC, pl.BlockSpec(memory_space=pl.ANY)],
        out_specs=[HBM_SPEC] * (2 * n),
        out_shape=[pltpu.HBM(g.shape, g.dtype) for g in grads_thru] + [pltpu.HBM(l.shape, l.dtype) for l in lands_thru],
        input_output_aliases={t: t for t in range(2 * n)},
        compiler_params=pltpu.CompilerParams(has_side_effects=EFFECT),
    )(*grads_thru, *lands_thru, sems[0], sems[1], after)
    return list(res[n:])


def _sibling_exchange(arrs, name):
    n = len(arrs)
    any_spec = pl.BlockSpec(memory_space=pl.ANY)

    def body(*refs):
        ins, outs = refs[:n], refs[n:2 * n]
        send_sems, recv_sems = refs[2 * n:]
        sibling = (lax.axis_index("x"), lax.axis_index("y"), 1 - lax.axis_index("c"))
        copies = []
        for t in range(n):
            rc = pltpu.make_async_remote_copy(
                src_ref=ins[t], dst_ref=outs[t], send_sem=send_sems.at[t], recv_sem=recv_sems.at[t],
                device_id=sibling, device_id_type=MESH)
            rc.start()
            copies.append(rc)
        for rc in copies:
            rc.wait_recv()
        for rc in copies:
            rc.wait_send()

    return pl.pallas_call(
        body, name=name,
        in_specs=[any_spec] * n, out_specs=[any_spec] * n,
        out_shape=[jax.ShapeDtypeStruct(a.shape, a.dtype) for a in arrs],
        scratch_shapes=[pltpu.SemaphoreType.DMA((n,)), pltpu.SemaphoreType.DMA((n,))],
    )(*arrs)


def _all_reduce_small(packed, name):
    nc = packed.shape[1]
    vmem = pl.BlockSpec(memory_space=pltpu.VMEM)

    def body(in_ref, out_ref, gbuf, send_sems, recv_sems):
        x, y, c = lax.axis_index("x"), lax.axis_index("y"), lax.axis_index("c")
        me = 4 * x + 2 * y + c
        gbuf[me] = jnp.sum(in_ref[...], axis=0, keepdims=True)
        copies = []
        for k in range(1, 8):
            peer = (x ^ ((k >> 2) & 1), y ^ ((k >> 1) & 1), c ^ (k & 1))
            rc = pltpu.make_async_remote_copy(
                src_ref=gbuf.at[me], dst_ref=gbuf.at[me], send_sem=send_sems.at[k - 1], recv_sem=recv_sems.at[k - 1],
                device_id=peer, device_id_type=MESH)
            rc.start()
            copies.append(rc)
        for k in range(1, 8):
            peer = (x ^ ((k >> 2) & 1), y ^ ((k >> 1) & 1), c ^ (k & 1))
            pltpu.make_async_remote_copy(
                src_ref=gbuf.at[me], dst_ref=gbuf.at[me ^ k], send_sem=send_sems.at[k - 1],
                recv_sem=recv_sems.at[k - 1], device_id=peer, device_id_type=MESH).wait_recv()
        for rc in copies:
            rc.wait_send()
        tot = gbuf[0]
        for d in range(1, 8):
            tot = tot + gbuf[d]
        out_ref[...] = tot

    return pl.pallas_call(
        body, name=name,
        in_specs=[vmem], out_specs=vmem,
        out_shape=jax.ShapeDtypeStruct((1, nc), F32),
        scratch_shapes=[pltpu.VMEM((8, 1, nc), F32), pltpu.SemaphoreType.DMA((7,)), pltpu.SemaphoreType.DMA((7,))],
    )(packed)


def _sum4(g_full, axis, slot_arr, recv, name):
    _, r, c = recv.shape
    tr = min(r, 128)
    nb = r // tr

    def body(slot_ref, own_ref, recv_ref, o_ref):
        acc = own_ref[...]
        for k in range(3):
            acc = acc + recv_ref[k].astype(F32)
        o_ref[...] = acc

    if axis == 0:
        own_map = lambda i, s: (s[0] * nb + i, 0)
    else:
        own_map = lambda i, s: (i, s[0])
    return pl.pallas_call(
        body, name=name,
        grid_spec=pltpu.PrefetchScalarGridSpec(
            num_scalar_prefetch=1, grid=(nb,),
            in_specs=[pl.BlockSpec((tr, c), own_map), pl.BlockSpec((3, tr, c), lambda i, s: (0, i, 0))],
            out_specs=pl.BlockSpec((tr, c), lambda i, s: (i, 0))),
        out_shape=jax.ShapeDtypeStruct((r, c), F32),
        compiler_params=_params("parallel"),
    )(slot_arr, g_full, recv)


def _adamw(w, g_parts, m, v, name):
    r, c = w.shape
    tr = r if r % 128 else 128
    npart = len(g_parts)

    def body(*refs):
        w_ref = refs[0]
        g_refs = refs[1:1 + npart]
        m_ref, v_ref, g_out, d_out, m_out, v_out = refs[1 + npart:]
        g = g_refs[0][...]
        for gr in g_refs[1:]:
            g = g + gr[...]
        mm = ADAM_B1 * m_ref[...] + (1.0 - ADAM_B1) * g
        vv = ADAM_B2 * v_ref[...] + (1.0 - ADAM_B2) * (g * g)
        m_hat = mm / (1.0 - ADAM_B1 ** ADAM_STEP)
        v_hat = vv / (1.0 - ADAM_B2 ** ADAM_STEP)
        g_out[...] = g
        d_out[...] = -ADAM_LR * (m_hat / (jnp.sqrt(v_hat) + ADAM_EPS) + ADAM_WD * w_ref[...])
        m_out[...] = mm
        v_out[...] = vv

    blk = pl.BlockSpec((tr, c), lambda i: (i, 0))
    shp = jax.ShapeDtypeStruct((r, c), F32)
    return pl.pallas_call(
        body, name=name, grid=(r // tr,),
        in_specs=[blk] * (3 + npart), out_specs=[blk] * 4, out_shape=[shp] * 4,
        compiler_params=_params("parallel"),
    )(w, *g_parts, m, v)


def _pad_rows8(w):
    return jnp.pad(w, ((0, HALO - w.shape[0]), (0, 0)))


def kernel(x, mem, hgrn_lb, norm1_w, w_in, hgrn_norm_w, sconv_w, w_out, norm2_w, mem_norm_w, wq, wk, wv, wo, norm3_w, w_gate, w_up, ffn_conv_w, ffn_conv_b, w_down, final_norm_w, loss_target, m_hgrn_lb, m_norm1_w, m_w_in, m_hgrn_norm_w, m_sconv_w, m_w_out, m_norm2_w, m_mem_norm_w, m_wq, m_wk, m_wv, m_wo, m_norm3_w, m_w_gate, m_w_up, m_ffn_conv_w, m_ffn_conv_b, m_w_down, m_final_norm_w, v_hgrn_lb, v_norm1_w, v_w_in, v_hgrn_norm_w, v_sconv_w, v_w_out, v_norm2_w, v_mem_norm_w, v_wq, v_wk, v_wv, v_wo, v_norm3_w, v_w_gate, v_w_up, v_ffn_conv_w, v_ffn_conv_b, v_w_down, v_final_norm_w):
    xs, mems, tgt = x[0], mem[0], loss_target[0]
    d = xs.shape[1]
    fnw = final_norm_w.reshape(1, d)

    big = {"w_in": (w_in[0], 1), "w_out": (w_out[0], 0), "wq": (wq[0], 0), "wk": (wk[0], 0), "wv": (wv[0], 0),
           "wo": (wo[0], 0), "w_gate": (w_gate[0], 1), "w_up": (w_up[0], 1), "w_down": (w_down[0], 0)}
    names = list(big)
    slot_arr = (2 * lax.axis_index("x") + lax.axis_index("y")).astype(jnp.int32).reshape(1)
    gnames = names + ["sconv8", "fconv8"]
    fulls = [_cast_into_full(big[n][0], big[n][1], slot_arr, BF16, "cast_" + n) for n in names]
    fulls += [_cast_into_full(_pad_rows8(sconv_w[0]), 1, slot_arr, F32, "cast_sconv_w"),
              _cast_into_full(_pad_rows8(ffn_conv_w[0]), 1, slot_arr, F32, "cast_ffn_conv_w")]
    axes = [big[n][1] for n in names] + [1, 1]
    groups = [["w_in"], ["w_out", "sconv8"], ["wq", "wk", "wv", "wo"], ["w_gate", "w_up", "fconv8", "w_down"]]
    gidx = [[gnames.index(n) for n in grp] for grp in groups]
    split = [True] * len(names) + [False, False]
    gsems, fulls, tok = _gather_start(fulls, axes, split, gidx, "gather_start")
    wf, relayed = {}, {}

    def gather_relay(g, after):
        idx = gidx[g]
        dsems, arrs, token = _gather_relay([fulls[t] for t in idx], [axes[t] for t in idx], [split[t] for t in idx],
                                           gsems[g], after, "gather_relay_%d" % g)
        relayed[g] = (dsems, arrs)
        return token[0:1, 0:1]

    def gather_finish(g, after):
        idx = gidx[g]
        dsems, arrs = relayed[g]
        got = _gather_finish(arrs, [axes[t] for t in idx], [split[t] for t in idx], dsems, after,
                             "gather_finish_%d" % g)
        wf.update(zip(groups[g], got))

    lb0, lb1 = hgrn_lb[0:1], hgrn_lb[1:2]

    h1 = _rmsnorm_fwd(xs, norm1_w + tok[0:1, 0:1], "norm1")
    gather_relay(0, h1)
    gather_finish(0, h1)
    proj = _matmul(h1, wf["w_in"], "nn", "proj_in", out_dtype=BF16)
    t1 = gather_relay(1, proj)
    o_h, og, states = _hgrn_fwd(proj, lb0, lb1, hgrn_norm_w + t1, "hgrn_fwd")
    gather_finish(1, o_h)
    t2 = gather_relay(2, o_h)
    sconv8 = wf["sconv8"]
    yc = _sconv_fwd(proj, sconv8, "sconv_fwd")
    mix = jnp.concatenate([og, yc], axis=1)
    x1 = _matmul(mix, wf["w_out"], "nn", "proj_out", residual=xs)
    t3 = gather_relay(3, x1)
    h2 = _rmsnorm_fwd(x1, norm2_w + (t2 + t3), "norm2")
    mem_n = _rmsnorm_fwd(mems, mem_norm_w, "norm_mem")
    gather_finish(2, h2)
    qa = _matmul(h2, wf["wq"], "nn", "attn_q", out_dtype=BF16)
    ka = _matmul(mem_n, wf["wk"], "nn", "attn_k", out_dtype=BF16)
    va = _matmul(mem_n, wf["wv"], "nn", "attn_v", out_dtype=BF16)
    att = _attn_fwd(qa, ka, va, "attn_fwd")
    x2 = _matmul(att, wf["wo"], "nn", "attn_o", residual=x1)
    h3 = _rmsnorm_fwd(x2, norm3_w, "norm3")
    gather_finish(3, h3)
    fconv8 = wf["fconv8"]
    gate = _matmul(h3, wf["w_gate"], "nn", "ffn_gate", out_dtype=BF16)
    up = _matmul(h3, wf["w_up"], "nn", "ffn_up", out_dtype=BF16)
    z = _ffn_fwd(gate, up, fconv8, ffn_conv_b, "ffn_act")
    x3 = _matmul(z, wf["w_down"], "nn", "ffn_down", residual=x2)

    dx3, dx3b, g_final, loss8 = _final_loss_bwd(x3, tgt, fnw, "loss_bwd")
    gw = {}
    dz = _matmul(dx3b, wf["w_down"], "nt", "d_z", out_dtype=BF16)
    gw["w_down"] = _matmul(z, dx3b, "tn", "g_w_down", extra_bf16=True)
    dgate, du, g_fb, g_fw = _ffn_bwd(gate, up, dz, fconv8, ffn_conv_b, "ffn_act_bwd")
    dh3 = _matmul(dgate, wf["w_gate"], "nt", "d_h3_gate")
    dh3 = _matmul(du, wf["w_up"], "nt", "d_h3_up", residual=dh3)
    gw["w_gate"] = _matmul(h3, dgate, "tn", "g_w_gate", extra_bf16=True)
    gw["w_up"] = _matmul(h3, du, "tn", "g_w_up", extra_bf16=True)
    pending = []

    def scatter_start(grp):
        sems, g_thru, lands, token = _scatter_start([gw[n][1] for n in grp], [big[n][1] for n in grp],
                                                    "scatter_start_" + grp[0])
        pending.append((grp, sems, g_thru, lands))
        return token[0:1, 0:1]

    tok1 = scatter_start(["w_down", "w_gate", "w_up"])
    dx2, dx2b, g_n3 = _rmsnorm_bwd(dh3, x2, norm3_w + tok1, dx3, "norm3_bwd")
    datt = _matmul(dx2b, wf["wo"], "nt", "d_att", out_dtype=BF16)
    gw["wo"] = _matmul(att, dx2b, "tn", "g_wo", extra_bf16=True)
    dqa, dka, dva = _attn_bwd(qa, ka, va, datt, "attn_bwd")
    dh2 = _matmul(dqa, wf["wq"], "nt", "d_h2")
    gw["wq"] = _matmul(h2, dqa, "tn", "g_wq", extra_bf16=True)
    gw["wk"] = _matmul(mem_n, dka, "tn", "g_wk", extra_bf16=True)
    gw["wv"] = _matmul(mem_n, dva, "tn", "g_wv", extra_bf16=True)
    tok2 = scatter_start(["wo", "wq", "wk", "wv"])
    dmem_n = _matmul(dka, wf["wk"], "nt", "d_memn_k")
    dmem_n = _matmul(dva, wf["wv"], "nt", "d_memn_v", residual=dmem_n)
    _, _, g_nm = _rmsnorm_bwd(dmem_n, mems, mem_norm_w, None, "norm_mem_bwd")
    dx1, dx1b, g_n2 = _rmsnorm_bwd(dh2, x1, norm2_w + tok2, dx2, "norm2_bwd")
    dmix = _matmul(dx1b, wf["w_out"], "nt", "d_mix", out_dtype=BF16)
    gw["w_out"] = _matmul(mix, dx1b, "tn", "g_w_out", extra_bf16=True)
    tok3 = scatter_start(["w_out"])
    dcb, dcc, dch, g_sw = _sconv_bwd(proj, sconv8, dmix, "sconv_bwd")
    dq, df, di, dg, g_lb, g_hn = _hgrn_bwd(proj, lb0, lb1, hgrn_norm_w + tok3, o_h, states, dmix, "hgrn_bwd")
    dproj = jnp.concatenate([dq, df, di, dg, dcb, dcc, dch], axis=1)
    gw["w_in"] = _matmul(h1, dproj, "tn", "g_w_in", extra_bf16=True)
    tok4 = scatter_start(["w_in"])
    dh1 = _matmul(dproj, wf["w_in"], "nt", "d_h1")
    dx, _, g_n1 = _rmsnorm_bwd(dh1, xs, norm1_w + tok4, dx1, "norm1_bwd")

    small = [g_n1, g_n2, g_n3, g_final, g_nm, g_lb, g_hn, g_fb,
             g_sw[0:8], g_sw[8:16], g_sw[16:24], g_fw[0:8], g_fw[8:16], g_fw[16:24], loss8]
    widths = [a.shape[1] for a in small]
    tot = _all_reduce_small(jnp.concatenate(small, axis=1), "all_reduce_small")
    offs = [0]
    for wd_ in widths:
        offs.append(offs[-1] + wd_)
    sm = [tot[:, offs[i]:offs[i + 1]] for i in range(len(small))]
    s_n1, s_n2, s_n3, s_final, s_nm, s_lb, s_hn, s_fb = sm[:8]
    s_sw = jnp.concatenate(sm[8:11], axis=0)
    s_fw = jnp.concatenate(sm[11:14], axis=0)
    loss = sm[14][0, 0]
    slot = 2 * lax.axis_index("x") + lax.axis_index("y")
    s_sw = lax.dynamic_slice_in_dim(s_sw, slot * (HGRN_W // N_CHIPS), HGRN_W // N_CHIPS, axis=1)
    fsh = ffn_conv_w.shape[2]
    s_fw = lax.dynamic_slice_in_dim(s_fw, slot * fsh, fsh, axis=1)
    s_lb2 = jnp.concatenate([s_lb, -s_lb], axis=0)

    recv = {}
    for grp, sems, g_thru, lands in pending:
        got = _scatter_wait(g_thru, lands, [big[n][1] for n in grp], sems, tot, "scatter_wait_" + grp[0])
        recv.update(zip(grp, got))
    core_sums = [_sum4(gw[n][0], big[n][1], slot_arr, recv[n], "core_sum_" + n) for n in names]
    sib_sums = _sibling_exchange(core_sums, "sibling_exchange")

    moments = {"hgrn_lb": (m_hgrn_lb, v_hgrn_lb), "norm1_w": (m_norm1_w, v_norm1_w), "w_in": (m_w_in, v_w_in),
               "hgrn_norm_w": (m_hgrn_norm_w, v_hgrn_norm_w), "sconv_w": (m_sconv_w, v_sconv_w),
               "w_out": (m_w_out, v_w_out), "norm2_w": (m_norm2_w, v_norm2_w),
               "mem_norm_w": (m_mem_norm_w, v_mem_norm_w), "wq": (m_wq, v_wq), "wk": (m_wk, v_wk), "wv": (m_wv, v_wv),
               "wo": (m_wo, v_wo), "norm3_w": (m_norm3_w, v_norm3_w), "w_gate": (m_w_gate, v_w_gate),
               "w_up": (m_w_up, v_w_up), "ffn_conv_w": (m_ffn_conv_w, v_ffn_conv_w),
               "ffn_conv_b": (m_ffn_conv_b, v_ffn_conv_b), "w_down": (m_w_down, v_w_down),
               "final_norm_w": (m_final_norm_w, v_final_norm_w)}
    weights = {"hgrn_lb": hgrn_lb, "norm1_w": norm1_w, "w_in": w_in, "hgrn_norm_w": hgrn_norm_w, "sconv_w": sconv_w,
               "w_out": w_out, "norm2_w": norm2_w, "mem_norm_w": mem_norm_w, "wq": wq, "wk": wk, "wv": wv, "wo": wo,
               "norm3_w": norm3_w, "w_gate": w_gate, "w_up": w_up, "ffn_conv_w": ffn_conv_w, "ffn_conv_b": ffn_conv_b,
               "w_down": w_down, "final_norm_w": final_norm_w}
    small_g = {"hgrn_lb": s_lb2, "norm1_w": s_n1, "hgrn_norm_w": s_hn, "sconv_w": s_sw, "norm2_w": s_n2,
               "mem_norm_w": s_nm, "norm3_w": s_n3, "ffn_conv_w": s_fw, "ffn_conv_b": s_fb, "final_norm_w": s_final}
    order = list(weights)
    res = {}
    for n in order:
        w_full = weights[n]
        shape = w_full.shape
        w2 = w_full.reshape((-1, shape[-1]))
        m2, v2 = (t.reshape(w2.shape) for t in moments[n])
        if n in big:
            i = names.index(n)
            parts = [core_sums[i], sib_sums[i]]
        else:
            parts = [small_g[n].reshape(w2.shape)]
        res[n] = [t.reshape(shape) for t in _adamw(w2, parts, m2, v2, "adamw_" + n)]

    return (loss, dx[None], *[res[n][0] for n in order], *[res[n][1] for n in order],
            *[res[n][2] for n in order], *[res[n][3] for n in order])
```

```python
import functools

import jax
import jax.numpy as jnp
from jax import lax
from jax.experimental import pallas as pl
from jax.experimental.pallas import tpu as pltpu

F32 = jnp.float32
BF16 = jnp.bfloat16
MESH = pl.DeviceIdType.MESH

EPS = 1e-6
HGRN_W = 1024
HEAD = 128
N_HEADS = 8
CHUNK = 64
HGRN_UNROLL = 8
HGRN_HEADS_PER_STEP = 2
MEM_HEADS = 4
MEM_HEAD_DIM = 512
N_CHIPS = 4
HALO = 8

ADAM_LR = 0.001
ADAM_B1 = 0.9
ADAM_B2 = 0.999
ADAM_EPS = 1e-08
ADAM_WD = 0.01
ADAM_STEP = 10


def _sigmoid(x):
    return 1.0 / (1.0 + jnp.exp(-x))


def _dot(a, b, dims):
    return lax.dot_general(a.astype(BF16), b.astype(BF16), (dims, ((), ())),
                           preferred_element_type=F32)


def _dot_nn(a, b):
    return _dot(a, b, ((1,), (0,)))


def _dot_nt(a, b):
    return _dot(a, b, ((1,), (1,)))


def _dot_tn(a, b):
    return _dot(a, b, ((0,), (0,)))


def _hdot(a, b, dims):
    return lax.dot_general(a, b, (dims, ((), ())), precision=lax.Precision.HIGH, preferred_element_type=F32)


def _hdot_nn(a, b):
    return _hdot(a, b, ((1,), (0,)))


def _hdot_nt(a, b):
    return _hdot(a, b, ((1,), (1,)))


def _hdot_tn(a, b):
    return _hdot(a, b, ((0,), (0,)))


def _exact_ones_dot(ones_bf16, x):
    hi = x.astype(BF16)
    r1 = x - hi.astype(F32)
    mid = r1.astype(BF16)
    lo = (r1 - mid.astype(F32)).astype(BF16)
    dims = (((1,), (0,)), ((), ()))
    return (lax.dot_general(ones_bf16, hi, dims, preferred_element_type=F32)
            + lax.dot_general(ones_bf16, mid, dims, preferred_element_type=F32)
            + lax.dot_general(ones_bf16, lo, dims, preferred_element_type=F32))


def _rows8(v):
    t, c = v.shape
    return v.reshape(t // 8, 8, c).sum(axis=0)


def _shift_down(x, halo, s):
    rolled = pltpu.roll(x, s, 0)
    hrolled = pltpu.roll(halo, s, 0)
    row = lax.broadcasted_iota(jnp.int32, hrolled.shape, 0)
    head = jnp.where(row < s, hrolled, rolled[:HALO])
    return jnp.concatenate([head, rolled[HALO:]], axis=0)


def _shift_up(x, halo, s):
    t = x.shape[0]
    rolled = pltpu.roll(x, t - s, 0)
    hrolled = pltpu.roll(halo, HALO - s, 0)
    row = lax.broadcasted_iota(jnp.int32, hrolled.shape, 0)
    tail = jnp.where(row >= HALO - s, hrolled, rolled[t - HALO:])
    return jnp.concatenate([rolled[:t - HALO], tail], axis=0)


def _params(*sem):
    return pltpu.CompilerParams(dimension_semantics=sem)


def _row_tile(r, pref):
    while r % pref:
        pref //= 2
    return pref


def _rmsnorm_fwd(x, w, name, tm=256):
    s, d = x.shape
    tm = min(tm, s)

    def body(x_ref, w_ref, o_ref):
        xv = x_ref[...]
        r = lax.rsqrt(jnp.mean(xv * xv, axis=-1, keepdims=True) + EPS)
        o_ref[...] = ((xv * r) * w_ref[...]).astype(BF16)

    return pl.pallas_call(
        body, name=name, grid=(s // tm,),
        in_specs=[pl.BlockSpec((tm, d), lambda i: (i, 0)), pl.BlockSpec((1, d), lambda i: (0, 0))],
        out_specs=pl.BlockSpec((tm, d), lambda i: (i, 0)),
        out_shape=jax.ShapeDtypeStruct((s, d), BF16),
        compiler_params=_params("parallel"),
    )(x, w)


def _rmsnorm_bwd(dh, x, w, dres, name, tm=256):
    s, d = x.shape
    tm = min(tm, s)
    has_res = dres is not None

    def body(*refs):
        if has_res:
            dh_ref, x_ref, w_ref, dres_ref, dx_ref, dxb_ref, gw_ref = refs
        else:
            dh_ref, x_ref, w_ref, dx_ref, dxb_ref, gw_ref = refs

        @pl.when(pl.program_id(0) == 0)
        def _():
            gw_ref[...] = jnp.zeros_like(gw_ref)

        xv = x_ref[...]
        dhv = dh_ref[...].astype(F32)
        r = lax.rsqrt(jnp.mean(xv * xv, axis=-1, keepdims=True) + EPS)
        xhat = xv * r
        gw_ref[...] += _rows8(dhv * xhat)
        dxh = dhv * w_ref[...]
        dx = r * (dxh - xhat * jnp.mean(dxh * xhat, axis=-1, keepdims=True))
        if has_res:
            dx = dres_ref[...] + dx
        dx_ref[...] = dx
        dxb_ref[...] = dx.astype(BF16)

    row = pl.BlockSpec((tm, d), lambda i: (i, 0))
    in_specs = [row, row, pl.BlockSpec((1, d), lambda i: (0, 0))] + ([row] if has_res else [])
    args = (dh, x, w) + ((dres,) if has_res else ())
    return pl.pallas_call(
        body, name=name, grid=(s // tm,),
        in_specs=in_specs,
        out_specs=[row, row, pl.BlockSpec((8, d), lambda i: (0, 0))],
        out_shape=[jax.ShapeDtypeStruct((s, d), F32), jax.ShapeDtypeStruct((s, d), BF16),
                   jax.ShapeDtypeStruct((8, d), F32)],
        compiler_params=_params("arbitrary"),
    )(*args)


def _final_loss_bwd(x3, target, w, name, tm=256):
    s, d = x3.shape
    tm = min(tm, s)

    def body(x_ref, t_ref, w_ref, dx_ref, dxb_ref, gw_ref, loss_ref):
        @pl.when(pl.program_id(0) == 0)
        def _():
            gw_ref[...] = jnp.zeros_like(gw_ref)
            loss_ref[...] = jnp.zeros_like(loss_ref)

        xv = x_ref[...]
        r = lax.rsqrt(jnp.mean(xv * xv, axis=-1, keepdims=True) + EPS)
        xhat = xv * r
        y = xhat * w_ref[...]
        err = y - t_ref[...]
        part = 0.5 * jnp.mean(err * err, axis=-1, keepdims=True)
        tot = jnp.sum(part, axis=0, keepdims=True)
        rr = lax.broadcasted_iota(jnp.int32, loss_ref.shape, 0)
        cc = lax.broadcasted_iota(jnp.int32, loss_ref.shape, 1)
        loss_ref[...] += jnp.where((rr == 0) & (cc == 0), tot, 0.0)
        dy = err * (1.0 / d)
        gw_ref[...] += _rows8(dy * xhat)
        dxh = dy * w_ref[...]
        dx = r * (dxh - xhat * jnp.mean(dxh * xhat, axis=-1, keepdims=True))
        dx_ref[...] = dx
        dxb_ref[...] = dx.astype(BF16)

    row = pl.BlockSpec((tm, d), lambda i: (i, 0))
    return pl.pallas_call(
        body, name=name, grid=(s // tm,),
        in_specs=[row, row, pl.BlockSpec((1, d), lambda i: (0, 0))],
        out_specs=[row, row, pl.BlockSpec((8, d), lambda i: (0, 0)), pl.BlockSpec((8, 128), lambda i: (0, 0))],
        out_shape=[jax.ShapeDtypeStruct((s, d), F32), jax.ShapeDtypeStruct((s, d), BF16),
                   jax.ShapeDtypeStruct((8, d), F32), jax.ShapeDtypeStruct((8, 128), F32)],
        compiler_params=_params("arbitrary"),
    )(x3, target, w)


MM_TILES = (1024, 1408, 512, 256, 128)
MM_K_TILES = (2816, 2048, 1792, 1408, 1024, 512, 256, 128)
MM_VMEM_LIMIT = 56 * 1024 * 1024
MM_VMEM_BUDGET = 46 * 1024 * 1024


def _pick_tile(dim):
    for t in MM_TILES:
        if dim % t == 0:
            return t
    return dim


def _matmul(a, b, mode, name, *, out_dtype=F32, residual=None, extra_bf16=False, tm=None, tn=None, tk=None):
    if mode == "nn":
        (m, k), (k2, n) = a.shape, b.shape
    elif mode == "nt":
        (m, k), (n, k2) = a.shape, b.shape
    else:
        (k, m), (k2, n) = a.shape, b.shape
    assert k == k2, (a.shape, b.shape, mode)
    tm = _pick_tile(m) if tm is None else min(tm, m)
    tn = _pick_tile(n) if tn is None else min(tn, n)
    out_bytes = tm * tn * (jnp.dtype(out_dtype).itemsize + (2 if extra_bf16 else 0) + (4 if residual is not None else 0))

    def vmem_bytes(t):
        return (2 * (tm * t * a.dtype.itemsize + t * tn * b.dtype.itemsize) + 2 * out_bytes
                + (tm * tn * 4 if t < k else 0))

    if tk is None:
        tk = next(t for t in MM_K_TILES if k % t == 0 and t <= k and vmem_bytes(t) <= MM_VMEM_BUDGET)
    assert m % tm == 0 and n % tn == 0 and k % tk == 0, (m, n, k, tm, tn, tk)
    nk = k // tk
    dims = {"nn": ((1,), (0,)), "nt": ((1,), (1,)), "tn": ((0,), (0,))}[mode]
    has_res = residual is not None

    def body(*refs):
        refs = list(refs)
        a_ref, b_ref = refs[0], refs[1]
        r_ref = refs[2] if has_res else None
        outs = refs[2 + has_res:]
        o_ref = outs[0]
        o2_ref = outs[1] if extra_bf16 else None
        def finish(r):
            if has_res:
                r = r_ref[...] + r
            o_ref[...] = r.astype(out_dtype)
            if extra_bf16:
                o2_ref[...] = r.astype(BF16)

        if nk == 1:
            finish(_dot(a_ref[...], b_ref[...], dims))
            return
        acc = outs[-1]
        kk = pl.program_id(2)

        @pl.when(kk == 0)
        def _():
            acc[...] = jnp.zeros_like(acc)

        acc[...] += _dot(a_ref[...], b_ref[...], dims)

        @pl.when(kk == nk - 1)
        def _():
            finish(acc[...])

    if mode == "tn":
        a_spec = pl.BlockSpec((tk, tm), lambda i, j, kk: (kk, i))
    else:
        a_spec = pl.BlockSpec((tm, tk), lambda i, j, kk: (i, kk))
    if mode == "nt":
        b_spec = pl.BlockSpec((tn, tk), lambda i, j, kk: (j, kk))
    else:
        b_spec = pl.BlockSpec((tk, tn), lambda i, j, kk: (kk, j))
    o_spec = pl.BlockSpec((tm, tn), lambda i, j, kk: (i, j))
    in_specs = [a_spec, b_spec] + ([o_spec] if has_res else [])
    out_specs = [o_spec] + ([o_spec] if extra_bf16 else [])
    out_shape = [jax.ShapeDtypeStruct((m, n), out_dtype)] + ([jax.ShapeDtypeStruct((m, n), BF16)] if extra_bf16 else [])
    args = (a, b) + ((residual,) if has_res else ())
    res = pl.pallas_call(
        body, name=name, grid=(m // tm, n // tn, nk),
        in_specs=in_specs, out_specs=out_specs, out_shape=out_shape,
        scratch_shapes=[pltpu.VMEM((tm, tn) if nk > 1 else (8, 128), F32)],
        compiler_params=pltpu.CompilerParams(dimension_semantics=("parallel", "parallel", "arbitrary"),
                                             vmem_limit_bytes=MM_VMEM_LIMIT),
    )(*args)
    return res if extra_bf16 else res[0]


def _hgrn_gates(qp, fp, lb):
    sig = _sigmoid(fp)
    f = lb + (1.0 - lb) * sig
    logf = jnp.log(f)
    k = 1.0 - f
    sq = _sigmoid(qp)
    q = qp * sq
    return sig, f, logf, k, sq, q


def _hgrn_fwd(proj, lb0, lb1, norm_w, name, tb=512):
    s = proj.shape[0]
    tb = min(tb, s)
    nb, ncb = s // tb, tb // CHUNK

    def body(q_ref, f_ref, i_ref, g_ref, a0_ref, a1_ref, nw_ref, o_ref, og_ref, st_ref, state):
        @pl.when(pl.program_id(1) == 0)
        def _():
            state[...] = jnp.zeros_like(state)

        lb2 = _sigmoid(a0_ref[...] - a1_ref[...])
        row = lax.broadcasted_iota(jnp.int32, (CHUNK, CHUNK), 0)
        col = lax.broadcasted_iota(jnp.int32, (CHUNK, CHUNK), 1)
        tril = row >= col
        ones_l = tril.astype(BF16)
        nw = nw_ref[...]

        def chunk(c, carry):
            rows = pl.ds(pl.multiple_of(c * CHUNK, CHUNK), CHUNK)
            for hh in range(HGRN_HEADS_PER_STEP):
                cols = slice(hh * HEAD, (hh + 1) * HEAD)
                v = i_ref[rows, cols].astype(F32)
                _, _, logf, k, _, q = _hgrn_gates(q_ref[rows, cols].astype(F32), f_ref[rows, cols].astype(F32),
                                                  lb2[:, cols])
                b = _exact_ones_dot(ones_l, logf)
                bl = jnp.sum(logf, axis=0, keepdims=True)
                bm = 0.5 * bl
                st = state[hh]
                st_ref[hh, c] = st
                qt = q * jnp.exp(b - bm)
                kt = k * jnp.exp(bm - b)
                a = jnp.where(tril, _dot_nt(qt, kt), 0.0)
                o = _dot_nt(q * jnp.exp(b), st) + _dot_nn(a, v)
                state[hh] = st * jnp.exp(bl) + _dot_tn(v, k * jnp.exp(bl - b))
                o_ref[rows, cols] = o
                on = (o * lax.rsqrt(jnp.mean(o * o, axis=-1, keepdims=True) + EPS)) * nw
                gv = g_ref[rows, cols].astype(F32)
                og_ref[rows, cols] = (on * (gv * _sigmoid(gv))).astype(BF16)
            return carry

        lax.fori_loop(0, ncb, chunk, 0, unroll=HGRN_UNROLL)

    hp, wd = HGRN_HEADS_PER_STEP, HGRN_HEADS_PER_STEP * HEAD
    ngrp = N_HEADS // hp

    def colblk(group):
        return pl.BlockSpec((tb, wd), lambda h, j: (j, group * ngrp + h))

    vec = pl.BlockSpec((1, wd), lambda h, j: (0, h))
    out_blk = pl.BlockSpec((tb, wd), lambda h, j: (j, h))
    return pl.pallas_call(
        body, name=name, grid=(ngrp, nb),
        in_specs=[colblk(0), colblk(1), colblk(2), colblk(3), vec, vec, pl.BlockSpec((1, HEAD), lambda h, j: (0, 0))],
        out_specs=[out_blk, out_blk, pl.BlockSpec((hp, ncb, HEAD, HEAD), lambda h, j: (h, j, 0, 0))],
        out_shape=[jax.ShapeDtypeStruct((s, HGRN_W), F32), jax.ShapeDtypeStruct((s, 2 * HGRN_W), BF16),
                   jax.ShapeDtypeStruct((N_HEADS, s // CHUNK, HEAD, HEAD), F32)],
        scratch_shapes=[pltpu.VMEM((hp, HEAD, HEAD), F32)],
        compiler_params=_params("parallel", "arbitrary"),
    )(proj, proj, proj, proj, lb0, lb1, norm_w)


def _hgrn_bwd(proj, lb0, lb1, norm_w, o, states, dmix, name, tb=512):
    s = proj.shape[0]
    tb = min(tb, s)
    nb, ncb = s // tb, tb // CHUNK

    def body(q_ref, f_ref, i_ref, g_ref, a0_ref, a1_ref, nw_ref, o_ref, st_ref, dm_ref,
             dq_ref, df_ref, di_ref, dg_ref, glb_ref, gnw_ref, dstate):
        h = pl.program_id(0)

        @pl.when(pl.program_id(1) == 0)
        def _():
            dstate[...] = jnp.zeros_like(dstate)
            glb_ref[...] = jnp.zeros_like(glb_ref)

        @pl.when((pl.program_id(1) == 0) & (h == 0))
        def _():
            gnw_ref[...] = jnp.zeros_like(gnw_ref)

        lb2 = _sigmoid(a0_ref[...] - a1_ref[...])
        row = lax.broadcasted_iota(jnp.int32, (CHUNK, CHUNK), 0)
        col = lax.broadcasted_iota(jnp.int32, (CHUNK, CHUNK), 1)
        tril = row >= col
        ones_l = tril.astype(BF16)
        ones_u = (row <= col).astype(BF16)
        nw = nw_ref[...]

        def chunk(cc, carry):
            c = ncb - 1 - cc
            rows = pl.ds(pl.multiple_of(c * CHUNK, CHUNK), CHUNK)
            for hh in range(HGRN_HEADS_PER_STEP):
                cols = slice(hh * HEAD, (hh + 1) * HEAD)
                lb = lb2[:, cols]
                qp = q_ref[rows, cols].astype(F32)
                v = i_ref[rows, cols].astype(F32)
                sig, f, logf, k, sq, q = _hgrn_gates(qp, f_ref[rows, cols].astype(F32), lb)
                gv = g_ref[rows, cols].astype(F32)
                sg = _sigmoid(gv)
                silu_g = gv * sg
                dog = dm_ref[rows, cols].astype(F32)
                ov = o_ref[rows, cols]
                r = lax.rsqrt(jnp.mean(ov * ov, axis=-1, keepdims=True) + EPS)
                ohat = ov * r
                on = ohat * nw
                dg_ref[rows, cols] = (dog * on * (sg * (1.0 + gv * (1.0 - sg)))).astype(BF16)
                don = dog * silu_g
                gnw_ref[...] += _rows8(don * ohat)
                doh = don * nw
                do = r * (doh - ohat * jnp.mean(doh * ohat, axis=-1, keepdims=True))
                b = _exact_ones_dot(ones_l, logf)
                bl = jnp.sum(logf, axis=0, keepdims=True)
                bm = 0.5 * bl
                e_q = jnp.exp(b - bm)
                e_k = jnp.exp(bm - b)
                e_b = jnp.exp(b)
                e_l = jnp.exp(bl - b)
                qt, kt, qb, kb = q * e_q, k * e_k, q * e_b, k * e_l
                st0 = st_ref[hh, c]
                dst = dstate[hh]
                a = jnp.where(tril, _dot_nt(qt, kt), 0.0)
                da = jnp.where(tril, _dot_nt(do, v), 0.0)
                dq = _hdot_nn(da, kt) * e_q + _hdot_nn(do, st0) * e_b
                dkb = _hdot_nn(v, dst) * e_l
                dk = _hdot_tn(da, qt) * e_k + dkb
                dv = _dot_tn(a, do) + _dot_nt(kb, dst)
                e_bl = jnp.exp(bl)
                dstate[hh] = dst * e_bl + _dot_tn(do, qb)
                db = q * dq - k * dk
                db_last = jnp.sum(k * dkb, axis=0, keepdims=True) + e_bl * jnp.sum(st0 * dst, axis=0, keepdims=True)
                dlogf = _exact_ones_dot(ones_u, db) + db_last
                dfg = dlogf / f - dk
                df_ref[rows, cols] = (dfg * (1.0 - lb) * (sig * (1.0 - sig))).astype(BF16)
                glb_ref[:, cols] += _rows8(dfg * (1.0 - sig)) * (lb * (1.0 - lb))
                dq_ref[rows, cols] = (dq * (sq * (1.0 + qp * (1.0 - sq)))).astype(BF16)
                di_ref[rows, cols] = dv.astype(BF16)
            return carry

        lax.fori_loop(0, ncb, chunk, 0, unroll=HGRN_UNROLL)

    hp, wd = HGRN_HEADS_PER_STEP, HGRN_HEADS_PER_STEP * HEAD
    ngrp = N_HEADS // hp

    def colblk(group):
        return pl.BlockSpec((tb, wd), lambda h, j: (nb - 1 - j, group * ngrp + h))

    vec = pl.BlockSpec((1, wd), lambda h, j: (0, h))
    blk = pl.BlockSpec((tb, wd), lambda h, j: (nb - 1 - j, h))
    grad = jax.ShapeDtypeStruct((s, HGRN_W), BF16)
    return pl.pallas_call(
        body, name=name, grid=(ngrp, nb),
        in_specs=[colblk(0), colblk(1), colblk(2), colblk(3), vec, vec, pl.BlockSpec((1, HEAD), lambda h, j: (0, 0)),
                  blk, pl.BlockSpec((hp, ncb, HEAD, HEAD), lambda h, j: (h, nb - 1 - j, 0, 0)), blk],
        out_specs=[blk, blk, blk, blk, pl.BlockSpec((8, wd), lambda h, j: (0, h)),
                   pl.BlockSpec((8, HEAD), lambda h, j: (0, 0))],
        out_shape=[grad, grad, grad, grad, jax.ShapeDtypeStruct((8, HGRN_W), F32), jax.ShapeDtypeStruct((8, HEAD), F32)],
        scratch_shapes=[pltpu.VMEM((hp, HEAD, HEAD), F32)],
        compiler_params=_params("arbitrary", "arbitrary"),
    )(proj, proj, proj, proj, lb0, lb1, norm_w, o, states, dmix)


HALO_BLK = 16


def _f32(ref):
    return ref[...].astype(F32)


def _halo_prev(ref):
    return ref[...].astype(F32)[HALO_BLK - HALO:]


def _halo_next(ref):
    return ref[...].astype(F32)[:HALO]


def _conv3(x0, x1, x2, w_ref):
    y = x0 * w_ref[0:1, :]
    y = y + x1 * w_ref[1:2, :]
    return y + x2 * w_ref[2:3, :]


def _sconv_fwd(proj, w8, mix, name, tb=256):
    s = proj.shape[0]
    tb = min(tb, s)
    hb = tb // HALO_BLK

    def body(cb_ref, cc_ref, ch_ref, cch_ref, chh_ref, w_ref, mix_ref, y_ref):
        first = pl.program_id(0) == 0
        u = _f32(cc_ref) * _f32(ch_ref)
        uh = jnp.where(first, 0.0, _halo_prev(cch_ref) * _halo_prev(chh_ref))
        conv = _conv3(_shift_down(u, uh, 2), _shift_down(u, uh, 1), u, w_ref)
        y_ref[...] = (_f32(cb_ref) * conv).astype(BF16)

    def blk(g):
        return pl.BlockSpec((tb, HGRN_W), lambda j: (j, g))

    def halo(g):
        return pl.BlockSpec((HALO_BLK, HGRN_W), lambda j: (jnp.maximum(j * hb - 1, 0), g))

    return pl.pallas_call(
        body, name=name, grid=(s // tb,),
        in_specs=[blk(4), blk(5), blk(6), halo(5), halo(6), pl.BlockSpec((HALO, HGRN_W), lambda j: (0, 0)),
                  pl.BlockSpec(memory_space=pl.ANY)],
        out_specs=pl.BlockSpec((tb, HGRN_W), lambda j: (j, 1)),
        out_shape=jax.ShapeDtypeStruct(mix.shape, BF16),
        input_output_aliases={6: 0},
        compiler_params=_params("parallel"),
    )(proj, proj, proj, proj, proj, w8, mix)


def _sconv_bwd(proj, w8, dmix, name, tb=256):
    s = proj.shape[0]
    tb = min(tb, s)
    hb = tb // HALO_BLK
    nb = s // tb
    last_h = s // HALO_BLK - 1

    def body(cb_ref, cc_ref, ch_ref, cch_ref, chh_ref, cbn_ref, dy_ref, dyn_ref, w_ref,
             dcb_ref, dcc_ref, dch_ref, gw_ref):
        j = pl.program_id(0)

        @pl.when(j == 0)
        def _():
            gw_ref[...] = jnp.zeros_like(gw_ref)

        cc, ch, cb = _f32(cc_ref), _f32(ch_ref), _f32(cb_ref)
        u = cc * ch
        uh = jnp.where(j == 0, 0.0, _halo_prev(cch_ref) * _halo_prev(chh_ref))
        u2, u1 = _shift_down(u, uh, 2), _shift_down(u, uh, 1)
        conv = _conv3(u2, u1, u, w_ref)
        dy = _f32(dy_ref)
        dcb_ref[...] = (dy * conv).astype(BF16)
        dc = dy * cb
        dcn = jnp.where(j == nb - 1, 0.0, _halo_next(dyn_ref) * _halo_next(cbn_ref))
        gw_ref[0:8, :] += _rows8(dc * u2)
        gw_ref[8:16, :] += _rows8(dc * u1)
        gw_ref[16:24, :] += _rows8(dc * u)
        du = dc * w_ref[2:3, :] + _shift_up(dc, dcn, 1) * w_ref[1:2, :] + _shift_up(dc, dcn, 2) * w_ref[0:1, :]
        dcc_ref[...] = (du * ch).astype(BF16)
        dch_ref[...] = (du * cc).astype(BF16)

    def blk(g):
        return pl.BlockSpec((tb, HGRN_W), lambda j: (j, g))

    def halo_prev(g):
        return pl.BlockSpec((HALO_BLK, HGRN_W), lambda j: (jnp.maximum(j * hb - 1, 0), g))

    def halo_next(g):
        return pl.BlockSpec((HALO_BLK, HGRN_W), lambda j: (jnp.minimum((j + 1) * hb, last_h), g))

    out = pl.BlockSpec((tb, HGRN_W), lambda j: (j, 0))
    grad = jax.ShapeDtypeStruct((s, HGRN_W), BF16)
    return pl.pallas_call(
        body, name=name, grid=(nb,),
        in_specs=[blk(4), blk(5), blk(6), halo_prev(5), halo_prev(6), halo_next(4), blk(1), halo_next(1),
                  pl.BlockSpec((HALO, HGRN_W), lambda j: (0, 0))],
        out_specs=[out, out, out, pl.BlockSpec((24, HGRN_W), lambda j: (0, 0))],
        out_shape=[grad, grad, grad, jax.ShapeDtypeStruct((24, HGRN_W), F32)],
        compiler_params=_params("arbitrary"),
    )(proj, proj, proj, proj, proj, proj, dmix, dmix, w8)


def _attn_fwd(q, kk, vv, name, tb=256):
    s, d = q.shape
    m = kk.shape[0]
    tb = min(tb, s)
    scale = MEM_HEAD_DIM ** -0.5

    def body(q_ref, k_ref, v_ref, o_ref):
        for hh in range(MEM_HEADS):
            cols = slice(hh * MEM_HEAD_DIM, (hh + 1) * MEM_HEAD_DIM)
            sc = _dot_nt(q_ref[:, cols], k_ref[:, cols]) * scale
            sc = sc - jnp.max(sc, axis=-1, keepdims=True)
            e = jnp.exp(sc)
            p = e / jnp.sum(e, axis=-1, keepdims=True)
            o_ref[:, cols] = _dot_nn(p, v_ref[:, cols]).astype(BF16)

    full = pl.BlockSpec((m, d), lambda i: (0, 0))
    return pl.pallas_call(
        body, name=name, grid=(s // tb,),
        in_specs=[pl.BlockSpec((tb, d), lambda i: (i, 0)), full, full],
        out_specs=pl.BlockSpec((tb, d), lambda i: (i, 0)),
        out_shape=jax.ShapeDtypeStruct((s, d), BF16),
        compiler_params=_params("parallel"),
    )(q, kk, vv)


def _attn_bwd(q, kk, vv, datt, name, tb=256):
    s, d = q.shape
    m = kk.shape[0]
    tb = min(tb, s)
    scale = MEM_HEAD_DIM ** -0.5

    def body(q_ref, k_ref, v_ref, do_ref, dq_ref, dk_ref, dv_ref):
        @pl.when(pl.program_id(0) == 0)
        def _():
            dk_ref[...] = jnp.zeros_like(dk_ref)
            dv_ref[...] = jnp.zeros_like(dv_ref)

        for hh in range(MEM_HEADS):
            cols = slice(hh * MEM_HEAD_DIM, (hh + 1) * MEM_HEAD_DIM)
            qh, kh, vh, doh = q_ref[:, cols], k_ref[:, cols], v_ref[:, cols], do_ref[:, cols]
            sc = _dot_nt(qh, kh) * scale
            sc = sc - jnp.max(sc, axis=-1, keepdims=True)
            e = jnp.exp(sc)
            p = e / jnp.sum(e, axis=-1, keepdims=True)
            dp = _dot_nt(doh, vh)
            ds = p * (dp - jnp.sum(dp * p, axis=-1, keepdims=True)) * scale
            dq_ref[:, cols] = _dot_nn(ds, kh).astype(BF16)
            dk_ref[:, cols] += _dot_tn(ds, qh)
            dv_ref[:, cols] += _dot_tn(p, doh)

    full = pl.BlockSpec((m, d), lambda i: (0, 0))
    row = pl.BlockSpec((tb, d), lambda i: (i, 0))
    return pl.pallas_call(
        body, name=name, grid=(s // tb,),
        in_specs=[row, full, full, row],
        out_specs=[row, full, full],
        out_shape=[jax.ShapeDtypeStruct((s, d), BF16), jax.ShapeDtypeStruct((m, d), F32),
                   jax.ShapeDtypeStruct((m, d), F32)],
        compiler_params=_params("arbitrary"),
    )(q, kk, vv, datt)


def _ffn_fwd(g, u, w8, bias, name, tb=512, tc=1408):
    s, f = g.shape
    tb = min(tb, s)
    tc = tc if f % tc == 0 else 512
    hb = tb // HALO_BLK

    def body(g_ref, gh_ref, u_ref, w_ref, b_ref, z_ref):
        gv = _f32(g_ref)
        gh = jnp.where(pl.program_id(1) == 0, 0.0, _halo_prev(gh_ref))
        a = _conv3(_shift_down(gv, gh, 2), _shift_down(gv, gh, 1), gv, w_ref) + b_ref[...]
        z_ref[...] = ((a * _sigmoid(a)) * _f32(u_ref)).astype(BF16)

    blk = pl.BlockSpec((tb, tc), lambda c, j: (j, c))
    return pl.pallas_call(
        body, name=name, grid=(f // tc, s // tb),
        in_specs=[blk, pl.BlockSpec((HALO_BLK, tc), lambda c, j: (jnp.maximum(j * hb - 1, 0), c)), blk,
                  pl.BlockSpec((HALO, tc), lambda c, j: (0, c)), pl.BlockSpec((1, tc), lambda c, j: (0, c))],
        out_specs=blk,
        out_shape=jax.ShapeDtypeStruct((s, f), BF16),
        compiler_params=pltpu.CompilerParams(dimension_semantics=("parallel", "parallel"),
                                             vmem_limit_bytes=MM_VMEM_LIMIT),
    )(g, g, u, w8, bias)


def _ffn_bwd(g, u, dz, w8, bias, name, tb=256, tc=1408):
    s, f = g.shape
    tb = min(tb, s)
    tc = tc if f % tc == 0 else 512
    hb = tb // HALO_BLK
    nb = s // tb

    def body(g_ref, gh_ref, u_ref, dz_ref, w_ref, b_ref, dg_ref, du_ref, gb_ref, gw_ref, da_next):
        jj = pl.program_id(1)

        @pl.when(jj == 0)
        def _():
            gb_ref[...] = jnp.zeros_like(gb_ref)
            gw_ref[...] = jnp.zeros_like(gw_ref)
            da_next[...] = jnp.zeros_like(da_next)

        gv = _f32(g_ref)
        gh = jnp.where(jj == nb - 1, 0.0, _halo_prev(gh_ref))
        g2, g1 = _shift_down(gv, gh, 2), _shift_down(gv, gh, 1)
        a = _conv3(g2, g1, gv, w_ref) + b_ref[...]
        sa = _sigmoid(a)
        dz = _f32(dz_ref)
        du_ref[...] = (dz * (a * sa)).astype(BF16)
        da = dz * _f32(u_ref) * (sa * (1.0 + a * (1.0 - sa)))
        gb_ref[...] += _rows8(da)
        gw_ref[0:8, :] += _rows8(da * g2)
        gw_ref[8:16, :] += _rows8(da * g1)
        gw_ref[16:24, :] += _rows8(da * gv)
        dan = da_next[...]
        dg = da * w_ref[2:3, :] + _shift_up(da, dan, 1) * w_ref[1:2, :] + _shift_up(da, dan, 2) * w_ref[0:1, :]
        dg_ref[...] = dg.astype(BF16)
        da_next[...] = da[:HALO]

    blk = pl.BlockSpec((tb, tc), lambda c, jj: (nb - 1 - jj, c))
    return pl.pallas_call(
        body, name=name, grid=(f // tc, nb),
        in_specs=[blk, pl.BlockSpec((HALO_BLK, tc), lambda c, jj: (jnp.maximum((nb - 1 - jj) * hb - 1, 0), c)), blk, blk,
                  pl.BlockSpec((HALO, tc), lambda c, jj: (0, c)), pl.BlockSpec((1, tc), lambda c, jj: (0, c))],
        out_specs=[blk, blk, pl.BlockSpec((8, tc), lambda c, jj: (0, c)), pl.BlockSpec((24, tc), lambda c, jj: (0, c))],
        out_shape=[jax.ShapeDtypeStruct((s, f), BF16), jax.ShapeDtypeStruct((s, f), BF16),
                   jax.ShapeDtypeStruct((8, f), F32), jax.ShapeDtypeStruct((24, f), F32)],
        scratch_shapes=[pltpu.VMEM((HALO, tc), F32)],
        compiler_params=pltpu.CompilerParams(dimension_semantics=("parallel", "arbitrary"),
                                             vmem_limit_bytes=MM_VMEM_LIMIT),
    )(g, g, u, dz, w8, bias)


def _window(ref, axis, slot, size):
    start = pl.multiple_of(slot * size, size)
    if axis == 0:
        return ref.at[pl.ds(start, size), :]
    return ref.at[:, pl.ds(start, size)]


def _chip_peers():
    x, y, c = lax.axis_index("x"), lax.axis_index("y"), lax.axis_index("c")
    peers = [(1 - x, y, c), (x, 1 - y, c), (1 - x, 1 - y, c)]
    slots = [2 * (1 - x) + y, 2 * x + (1 - y), 2 * (1 - x) + (1 - y)]
    return 2 * x + y, peers, slots


HBM_SPEC = pl.BlockSpec(memory_space=pltpu.HBM)
SEM_SPEC = pl.BlockSpec(memory_space=pltpu.SEMAPHORE)
EFFECT = pltpu.SideEffectType.DATAFLOW_SIDE_EFFECTING


def _hbm(a):
    return pltpu.with_memory_space_constraint(a, pltpu.HBM)


def _cast_into_full(x, axis, slot_arr, dtype, name):
    r, c = x.shape
    tr = _row_tile(r, 256)
    nb = r // tr
    full = (r * N_CHIPS, c) if axis == 0 else (r, c * N_CHIPS)

    def body(slot_ref, x_ref, o_ref):
        o_ref[...] = x_ref[...].astype(dtype)

    if axis == 0:
        out_map = lambda i, s: (s[0] * nb + i, 0)
    else:
        out_map = lambda i, s: (i, s[0])
    return pl.pallas_call(
        body, name=name,
        grid_spec=pltpu.PrefetchScalarGridSpec(
            num_scalar_prefetch=1, grid=(nb,),
            in_specs=[pl.BlockSpec((tr, c), lambda i, s: (i, 0))],
            out_specs=pl.BlockSpec((tr, c), out_map)),
        out_shape=jax.ShapeDtypeStruct(full, dtype),
        compiler_params=_params("parallel"),
    )(slot_arr, x)


def _piece(ref, axis, slot, half):
    size = ref.shape[axis] // N_CHIPS
    if half is None:
        return _window(ref, axis, slot, size)
    if axis == 0:
        h = size // 2
        return ref.at[pl.ds(pl.multiple_of(slot * size + half * h, h), h), :]
    h = ref.shape[0] // 2
    return ref.at[pl.ds(pl.multiple_of(half * h, h), h), pl.ds(pl.multiple_of(slot * size, size), size)]


def _gather_start(fulls, axes, split, groups, name):
    n, ng = len(fulls), len(groups)

    def body(*refs):
        outs = refs[n:]
        sems = outs[:2 * ng]
        thru = outs[2 * ng:2 * ng + n]
        token = outs[-1]
        slot, peers, _ = _chip_peers()
        c = lax.axis_index("c")
        for g, members in enumerate(groups):
            for i, t in enumerate(members):
                mine = _piece(thru[t], axes[t], slot, c if split[t] else None)
                for k in range(3):
                    pltpu.make_async_remote_copy(
                        src_ref=mine, dst_ref=mine, send_sem=sems[2 * g].at[3 * i + k],
                        recv_sem=sems[2 * g + 1].at[3 * i + k], device_id=peers[k], device_id_type=MESH).start()
        token[...] = jnp.zeros_like(token)

    sem_shapes = []
    for members in groups:
        sem_shapes += [pltpu.SemaphoreType.DMA((3 * len(members),))] * 2
    res = pl.pallas_call(
        body, name=name,
        in_specs=[HBM_SPEC] * n,
        out_specs=[SEM_SPEC] * (2 * ng) + [HBM_SPEC] * n + [pl.BlockSpec(memory_space=pltpu.VMEM)],
        out_shape=sem_shapes + [pltpu.HBM(f.shape, f.dtype) for f in fulls] + [jax.ShapeDtypeStruct((8, 128), F32)],
        input_output_aliases={t: 2 * ng + t for t in range(n)},
        compiler_params=pltpu.CompilerParams(has_side_effects=EFFECT),
    )(*[_hbm(f) for f in fulls])
    sems = [(res[2 * g], res[2 * g + 1]) for g in range(ng)]
    return sems, list(res[2 * ng:2 * ng + n]), res[-1]


def _gather_relay(fulls, axes, split, sems, after, name):
    n = len(fulls)
    nsplit = sum(split)

    def body(*refs):
        send_sems, recv_sems = refs[n], refs[n + 1]
        outs = refs[n + 3:]
        d_send, d_recv = outs[0], outs[1]
        thru = outs[2:2 + n]
        token = outs[-1]
        slot, peers, slots = _chip_peers()
        c = lax.axis_index("c")
        sibling = (lax.axis_index("x"), lax.axis_index("y"), 1 - c)
        for t in range(n):
            half = c if split[t] else None
            for k in range(3):
                cp = pltpu.make_async_remote_copy(
                    src_ref=_piece(thru[t], axes[t], slot, half), dst_ref=_piece(thru[t], axes[t], slots[k], half),
                    send_sem=send_sems.at[3 * t + k], recv_sem=recv_sems.at[3 * t + k],
                    device_id=peers[k], device_id_type=MESH)
                cp.wait_send()
                cp.wait_recv()
        i = 0
        for t in range(n):
            if not split[t]:
                continue
            for k in range(3):
                got = _piece(thru[t], axes[t], slots[k], c)
                pltpu.make_async_remote_copy(
                    src_ref=got, dst_ref=got, send_sem=d_send.at[3 * i + k], recv_sem=d_recv.at[3 * i + k],
                    device_id=sibling, device_id_type=MESH).start()
            i += 1
        token[...] = jnp.zeros_like(token)

    res = pl.pallas_call(
        body, name=name,
        in_specs=[HBM_SPEC] * n + [SEM_SPEC, SEM_SPEC, pl.BlockSpec(memory_space=pl.ANY)],
        out_specs=[SEM_SPEC, SEM_SPEC] + [HBM_SPEC] * n + [pl.BlockSpec(memory_space=pltpu.VMEM)],
        out_shape=[pltpu.SemaphoreType.DMA((3 * nsplit,)), pltpu.SemaphoreType.DMA((3 * nsplit,))]
        + [pltpu.HBM(f.shape, f.dtype) for f in fulls] + [jax.ShapeDtypeStruct((8, 128), F32)],
        input_output_aliases={t: 2 + t for t in range(n)},
        compiler_params=pltpu.CompilerParams(has_side_effects=EFFECT),
    )(*fulls, sems[0], sems[1], after)
    return (res[0], res[1]), list(res[2:2 + n]), res[-1]


def _gather_finish(fulls, axes, split, sems, after, name):
    n = len(fulls)

    def body(*refs):
        d_send, d_recv = refs[n], refs[n + 1]
        thru = refs[n + 3:]
        _, _, slots = _chip_peers()
        c = lax.axis_index("c")
        sibling = (lax.axis_index("x"), lax.axis_index("y"), 1 - c)
        i = 0
        for t in range(n):
            if not split[t]:
                continue
            for k in range(3):
                cp = pltpu.make_async_remote_copy(
                    src_ref=_piece(thru[t], axes[t], slots[k], c), dst_ref=_piece(thru[t], axes[t], slots[k], 1 - c),
                    send_sem=d_send.at[3 * i + k], recv_sem=d_recv.at[3 * i + k],
                    device_id=sibling, device_id_type=MESH)
                cp.wait_send()
                cp.wait_recv()
            i += 1

    return pl.pallas_call(
        body, name=name,
        in_specs=[HBM_SPEC] * n + [SEM_SPEC, SEM_SPEC, pl.BlockSpec(memory_space=pl.ANY)],
        out_specs=[HBM_SPEC] * n,
        out_shape=[pltpu.HBM(f.shape, f.dtype) for f in fulls],
        input_output_aliases={t: t for t in range(n)},
        compiler_params=pltpu.CompilerParams(has_side_effects=EFFECT),
    )(*fulls, sems[0], sems[1], after)


def _scatter_start(grads_bf16, axes, name):
    n = len(grads_bf16)

    def shard_shape(g, ax):
        return (g.shape[0] // N_CHIPS, g.shape[1]) if ax == 0 else (g.shape[0], g.shape[1] // N_CHIPS)

    shapes = [shard_shape(g, ax) for g, ax in zip(grads_bf16, axes)]

    def body(*refs):
        outs = refs[2 * n:]
        send_sems, recv_sems = outs[0], outs[1]
        gb, land = outs[2:2 + n], outs[2 + n:2 + 2 * n]
        token = outs[-1]
        _, peers, slots = _chip_peers()
        for t in range(n):
            size = shapes[t][axes[t]]
            for k in range(3):
                pltpu.make_async_remote_copy(
                    src_ref=_window(gb[t], axes[t], slots[k], size), dst_ref=land[t].at[k],
                    send_sem=send_sems.at[3 * t + k], recv_sem=recv_sems.at[3 * t + k],
                    device_id=peers[k], device_id_type=MESH).start()
        token[...] = jnp.zeros_like(token)

    lands = [_hbm(lax.empty((3,) + sh, BF16)) for sh in shapes]
    res = pl.pallas_call(
        body, name=name,
        in_specs=[HBM_SPEC] * (2 * n),
        out_specs=[SEM_SPEC, SEM_SPEC] + [HBM_SPEC] * (2 * n) + [pl.BlockSpec(memory_space=pltpu.VMEM)],
        out_shape=[pltpu.SemaphoreType.DMA((3 * n,)), pltpu.SemaphoreType.DMA((3 * n,))]
        + [pltpu.HBM(g.shape, g.dtype) for g in grads_bf16] + [pltpu.HBM((3,) + sh, BF16) for sh in shapes]
        + [jax.ShapeDtypeStruct((8, 128), F32)],
        input_output_aliases={t: 2 + t for t in range(2 * n)},
        compiler_params=pltpu.CompilerParams(has_side_effects=EFFECT),
    )(*[_hbm(g) for g in grads_bf16], *lands)
    return (res[0], res[1]), list(res[2:2 + n]), list(res[2 + n:2 + 2 * n]), res[-1]


def _scatter_wait(grads_thru, lands_thru, axes, sems, after, name):
    n = len(grads_thru)

    def body(*refs):
        send_sems, recv_sems = refs[2 * n], refs[2 * n + 1]
        outs = refs[2 * n + 3:]
        gb, land = outs[:n], outs[n:]
        _, peers, slots = _chip_peers()
        for t in range(n):
            size = land[t].shape[1 + axes[t]]
            for k in range(3):
                cp = pltpu.make_async_remote_copy(
                    src_ref=_window(gb[t], axes[t], slots[k], size), dst_ref=land[t].at[k],
                    send_sem=send_sems.at[3 * t + k], recv_sem=recv_sems.at[3 * t + k],
                    device_id=peers[k], device_id_type=MESH)
                cp.wait_send()
                cp.wait_recv()

    res = pl.pallas_call(
        body, name=name,
        in_specs=[HBM_SPEC] * (2 * n) + [SEM_SPEC, SEM_SPEC, pl.BlockSpec(memory_space=pl.ANY)],
        out_specs=[HBM_SPEC] * (2 * n),
        out_shape=[pltpu.HBM(g.shape, g.dtype) for g in grads_thru] + [pltpu.HBM(l.shape, l.dtype) for l in lands_thru],
        input_output_aliases={t: t for t in range(2 * n)},
        compiler_params=pltpu.CompilerParams(has_side_effects=EFFECT),
    )(*grads_thru, *lands_thru, sems[0], sems[1], after)
    return list(res[n:])


def _sibling_start(arrs, name):
    n = len(arrs)

    def body(*refs):
        outs = refs[2 * n:]
        send_sems, recv_sems = outs[0], outs[1]
        src, land = outs[2:2 + n], outs[2 + n:2 + 2 * n]
        token = outs[-1]
        sibling = (lax.axis_index("x"), lax.axis_index("y"), 1 - lax.axis_index("c"))
        for t in range(n):
            pltpu.make_async_remote_copy(
                src_ref=src[t], dst_ref=land[t], send_sem=send_sems.at[t], recv_sem=recv_sems.at[t],
                device_id=sibling, device_id_type=MESH).start()
        token[...] = jnp.zeros_like(token)

    lands = [_hbm(lax.empty(a.shape, a.dtype)) for a in arrs]
    res = pl.pallas_call(
        body, name=name,
        in_specs=[HBM_SPEC] * (2 * n),
        out_specs=[SEM_SPEC, SEM_SPEC] + [HBM_SPEC] * (2 * n) + [pl.BlockSpec(memory_space=pltpu.VMEM)],
        out_shape=[pltpu.SemaphoreType.DMA((n,)), pltpu.SemaphoreType.DMA((n,))]
        + [pltpu.HBM(a.shape, a.dtype) for a in arrs] * 2 + [jax.ShapeDtypeStruct((8, 128), F32)],
        input_output_aliases={t: 2 + t for t in range(2 * n)},
        compiler_params=pltpu.CompilerParams(has_side_effects=EFFECT),
    )(*[_hbm(a) for a in arrs], *lands)
    return (res[0], res[1]), list(res[2:2 + n]), list(res[2 + n:2 + 2 * n]), res[-1]


def _sibling_wait(src_thru, lands_thru, sems, after, name):
    n = len(src_thru)

    def body(*refs):
        send_sems, recv_sems = refs[2 * n], refs[2 * n + 1]
        outs = refs[2 * n + 3:]
        src, land = outs[:n], outs[n:]
        sibling = (lax.axis_index("x"), lax.axis_index("y"), 1 - lax.axis_index("c"))
        for t in range(n):
            cp = pltpu.make_async_remote_copy(
                src_ref=src[t], dst_ref=land[t], send_sem=send_sems.at[t], recv_sem=recv_sems.at[t],
                device_id=sibling, device_id_type=MESH)
            cp.wait_send()
            cp.wait_recv()

    res = pl.pallas_call(
        body, name=name,
        in_specs=[HBM_SPEC] * (2 * n) + [SEM_SPEC, SEM_SPEC, pl.BlockSpec(memory_space=pl.ANY)],
        out_specs=[HBM_SPEC] * (2 * n),
        out_shape=[pltpu.HBM(a.shape, a.dtype) for a in src_thru] * 2,
        input_output_aliases={t: t for t in range(2 * n)},
        compiler_params=pltpu.CompilerParams(has_side_effects=EFFECT),
    )(*src_thru, *lands_thru, sems[0], sems[1], after)
    return list(res[:n]), list(res[n:])


def _all_reduce_small(packed, name):
    nc = packed.shape[1]
    vmem = pl.BlockSpec(memory_space=pltpu.VMEM)

    def body(in_ref, out_ref, gbuf, send_sems, recv_sems):
        x, y, c = lax.axis_index("x"), lax.axis_index("y"), lax.axis_index("c")
        me = 4 * x + 2 * y + c
        gbuf[me] = jnp.sum(in_ref[...], axis=0, keepdims=True)
        copies = []
        for k in range(1, 8):
            peer = (x ^ ((k >> 2) & 1), y ^ ((k >> 1) & 1), c ^ (k & 1))
            rc = pltpu.make_async_remote_copy(
                src_ref=gbuf.at[me], dst_ref=gbuf.at[me], send_sem=send_sems.at[k - 1], recv_sem=recv_sems.at[k - 1],
                device_id=peer, device_id_type=MESH)
            rc.start()
            copies.append(rc)
        for k in range(1, 8):
            peer = (x ^ ((k >> 2) & 1), y ^ ((k >> 1) & 1), c ^ (k & 1))
            pltpu.make_async_remote_copy(
                src_ref=gbuf.at[me], dst_ref=gbuf.at[me ^ k], send_sem=send_sems.at[k - 1],
                recv_sem=recv_sems.at[k - 1], device_id=peer, device_id_type=MESH).wait_recv()
        for rc in copies:
            rc.wait_send()
        tot = gbuf[0]
        for d in range(1, 8):
            tot = tot + gbuf[d]
        out_ref[...] = tot

    return pl.pallas_call(
        body, name=name,
        in_specs=[vmem], out_specs=vmem,
        out_shape=jax.ShapeDtypeStruct((1, nc), F32),
        scratch_shapes=[pltpu.VMEM((8, 1, nc), F32), pltpu.SemaphoreType.DMA((7,)), pltpu.SemaphoreType.DMA((7,))],
    )(packed)


def _sum4(g_full, axis, slot_arr, recv, name):
    _, r, c = recv.shape
    tr = min(r, 128)
    nb = r // tr

    def body(slot_ref, own_ref, recv_ref, o_ref):
        acc = own_ref[...]
        for k in range(3):
            acc = acc + recv_ref[k].astype(F32)
        o_ref[...] = acc

    if axis == 0:
        own_map = lambda i, s: (s[0] * nb + i, 0)
    else:
        own_map = lambda i, s: (i, s[0])
    return pl.pallas_call(
        body, name=name,
        grid_spec=pltpu.PrefetchScalarGridSpec(
            num_scalar_prefetch=1, grid=(nb,),
            in_specs=[pl.BlockSpec((tr, c), own_map), pl.BlockSpec((3, tr, c), lambda i, s: (0, i, 0))],
            out_specs=pl.BlockSpec((tr, c), lambda i, s: (i, 0))),
        out_shape=jax.ShapeDtypeStruct((r, c), F32),
        compiler_params=_params("parallel"),
    )(slot_arr, g_full, recv)


def _adamw(w, g_parts, m, v, name):
    r, c = w.shape
    tr = r if r % 128 else _row_tile(r, 256)
    npart = len(g_parts)

    def body(*refs):
        w_ref = refs[0]
        g_refs = refs[1:1 + npart]
        m_ref, v_ref, g_out, d_out, m_out, v_out = refs[1 + npart:]
        g = g_refs[0][...]
        for gr in g_refs[1:]:
            g = g + gr[...]
        mm = ADAM_B1 * m_ref[...] + (1.0 - ADAM_B1) * g
        vv = ADAM_B2 * v_ref[...] + (1.0 - ADAM_B2) * (g * g)
        m_hat = mm / (1.0 - ADAM_B1 ** ADAM_STEP)
        v_hat = vv / (1.0 - ADAM_B2 ** ADAM_STEP)
        g_out[...] = g
        d_out[...] = -ADAM_LR * (m_hat / (jnp.sqrt(v_hat) + ADAM_EPS) + ADAM_WD * w_ref[...])
        m_out[...] = mm
        v_out[...] = vv

    blk = pl.BlockSpec((tr, c), lambda i: (i, 0))
    shp = jax.ShapeDtypeStruct((r, c), F32)
    return pl.pallas_call(
        body, name=name, grid=(r // tr,),
        in_specs=[blk] * (3 + npart), out_specs=[blk] * 4, out_shape=[shp] * 4,
        compiler_params=pltpu.CompilerParams(dimension_semantics=("parallel",), vmem_limit_bytes=MM_VMEM_LIMIT),
    )(w, *g_parts, m, v)


def _pad_rows8(w):
    return jnp.pad(w, ((0, HALO - w.shape[0]), (0, 0)))


def kernel(x, mem, hgrn_lb, norm1_w, w_in, hgrn_norm_w, sconv_w, w_out, norm2_w, mem_norm_w, wq, wk, wv, wo, norm3_w, w_gate, w_up, ffn_conv_w, ffn_conv_b, w_down, final_norm_w, loss_target, m_hgrn_lb, m_norm1_w, m_w_in, m_hgrn_norm_w, m_sconv_w, m_w_out, m_norm2_w, m_mem_norm_w, m_wq, m_wk, m_wv, m_wo, m_norm3_w, m_w_gate, m_w_up, m_ffn_conv_w, m_ffn_conv_b, m_w_down, m_final_norm_w, v_hgrn_lb, v_norm1_w, v_w_in, v_hgrn_norm_w, v_sconv_w, v_w_out, v_norm2_w, v_mem_norm_w, v_wq, v_wk, v_wv, v_wo, v_norm3_w, v_w_gate, v_w_up, v_ffn_conv_w, v_ffn_conv_b, v_w_down, v_final_norm_w):
    xs, mems, tgt = x[0], mem[0], loss_target[0]
    d = xs.shape[1]
    fnw = final_norm_w.reshape(1, d)

    big = {"w_in": (w_in[0], 1), "w_out": (w_out[0], 0), "wq": (wq[0], 0), "wk": (wk[0], 0), "wv": (wv[0], 0),
           "wo": (wo[0], 0), "w_gate": (w_gate[0], 1), "w_up": (w_up[0], 1), "w_down": (w_down[0], 0)}
    names = list(big)
    slot_arr = (2 * lax.axis_index("x") + lax.axis_index("y")).astype(jnp.int32).reshape(1)
    gnames = names + ["sconv8", "fconv8"]
    fulls = [_cast_into_full(big[n][0], big[n][1], slot_arr, BF16, "cast_" + n) for n in names]
    fulls += [_cast_into_full(_pad_rows8(sconv_w[0]), 1, slot_arr, F32, "cast_sconv_w"),
              _cast_into_full(_pad_rows8(ffn_conv_w[0]), 1, slot_arr, F32, "cast_ffn_conv_w")]
    axes = [big[n][1] for n in names] + [1, 1]
    groups = [["w_in"], ["w_out", "sconv8"], ["wq", "wk", "wv", "wo"], ["w_gate", "w_up", "fconv8", "w_down"]]
    gidx = [[gnames.index(n) for n in grp] for grp in groups]
    split = [True] * len(names) + [False, False]
    gsems, fulls, tok = _gather_start(fulls, axes, split, gidx, "gather_start")
    wf, relayed = {}, {}

    def gather_relay(g, after):
        idx = gidx[g]
        dsems, arrs, token = _gather_relay([fulls[t] for t in idx], [axes[t] for t in idx], [split[t] for t in idx],
                                           gsems[g], after, "gather_relay_%d" % g)
        relayed[g] = (dsems, arrs)
        return token[0:1, 0:1]

    def gather_finish(g, after):
        idx = gidx[g]
        dsems, arrs = relayed[g]
        got = _gather_finish(arrs, [axes[t] for t in idx], [split[t] for t in idx], dsems, after,
                             "gather_finish_%d" % g)
        wf.update(zip(groups[g], got))

    lb0, lb1 = hgrn_lb[0:1], hgrn_lb[1:2]

    h1 = _rmsnorm_fwd(xs, norm1_w + tok[0:1, 0:1], "norm1")
    gather_relay(0, h1)
    gather_finish(0, h1)
    proj = _matmul(h1, wf["w_in"], "nn", "proj_in", out_dtype=BF16)
    t1 = gather_relay(1, proj)
    o_h, og, states = _hgrn_fwd(proj, lb0, lb1, hgrn_norm_w + t1, "hgrn_fwd")
    gather_finish(1, o_h)
    t2 = gather_relay(2, o_h)
    sconv8 = wf["sconv8"]
    mix = _sconv_fwd(proj, sconv8, og, "sconv_fwd")
    x1 = _matmul(mix, wf["w_out"], "nn", "proj_out", residual=xs)
    t3 = gather_relay(3, x1)
    h2 = _rmsnorm_fwd(x1, norm2_w + (t2 + t3), "norm2")
    mem_n = _rmsnorm_fwd(mems, mem_norm_w, "norm_mem")
    gather_finish(2, h2)
    qa = _matmul(h2, wf["wq"], "nn", "attn_q", out_dtype=BF16)
    ka = _matmul(mem_n, wf["wk"], "nn", "attn_k", out_dtype=BF16)
    va = _matmul(mem_n, wf["wv"], "nn", "attn_v", out_dtype=BF16)
    att = _attn_fwd(qa, ka, va, "attn_fwd")
    x2 = _matmul(att, wf["wo"], "nn", "attn_o", residual=x1)
    h3 = _rmsnorm_fwd(x2, norm3_w, "norm3")
    gather_finish(3, h3)
    fconv8 = wf["fconv8"]
    gate = _matmul(h3, wf["w_gate"], "nn", "ffn_gate", out_dtype=BF16)
    up = _matmul(h3, wf["w_up"], "nn", "ffn_up", out_dtype=BF16)
    z = _ffn_fwd(gate, up, fconv8, ffn_conv_b, "ffn_act")
    x3 = _matmul(z, wf["w_down"], "nn", "ffn_down", residual=x2)

    dx3, dx3b, g_final, loss8 = _final_loss_bwd(x3, tgt, fnw, "loss_bwd")
    gw = {}
    dz = _matmul(dx3b, wf["w_down"], "nt", "d_z", out_dtype=BF16)
    gw["w_down"] = _matmul(z, dx3b, "tn", "g_w_down", extra_bf16=True)
    dgate, du, g_fb, g_fw = _ffn_bwd(gate, up, dz, fconv8, ffn_conv_b, "ffn_act_bwd")
    dh3 = _matmul(dgate, wf["w_gate"], "nt", "d_h3_gate")
    dh3 = _matmul(du, wf["w_up"], "nt", "d_h3_up", residual=dh3, out_dtype=BF16)
    gw["w_gate"] = _matmul(h3, dgate, "tn", "g_w_gate", extra_bf16=True)
    gw["w_up"] = _matmul(h3, du, "tn", "g_w_up", extra_bf16=True)
    pending = []

    def scatter_start(grp):
        sems, g_thru, lands, token = _scatter_start([gw[n][1] for n in grp], [big[n][1] for n in grp],
                                                    "scatter_start_" + grp[0])
        pending.append((grp, sems, g_thru, lands))
        return token[0:1, 0:1]

    tok1 = scatter_start(["w_down", "w_gate", "w_up"])
    dx2, dx2b, g_n3 = _rmsnorm_bwd(dh3, x2, norm3_w + tok1, dx3, "norm3_bwd")
    datt = _matmul(dx2b, wf["wo"], "nt", "d_att", out_dtype=BF16)
    gw["wo"] = _matmul(att, dx2b, "tn", "g_wo", extra_bf16=True)
    dqa, dka, dva = _attn_bwd(qa, ka, va, datt, "attn_bwd")
    dh2 = _matmul(dqa, wf["wq"], "nt", "d_h2", out_dtype=BF16)
    gw["wq"] = _matmul(h2, dqa, "tn", "g_wq", extra_bf16=True)
    gw["wk"] = _matmul(mem_n, dka, "tn", "g_wk", extra_bf16=True)
    gw["wv"] = _matmul(mem_n, dva, "tn", "g_wv", extra_bf16=True)
    tok2 = scatter_start(["wo", "wq", "wk", "wv"])
    dmem_n = _matmul(dka, wf["wk"], "nt", "d_memn_k")
    dmem_n = _matmul(dva, wf["wv"], "nt", "d_memn_v", residual=dmem_n)
    _, _, g_nm = _rmsnorm_bwd(dmem_n, mems, mem_norm_w, None, "norm_mem_bwd")
    dx1, dx1b, g_n2 = _rmsnorm_bwd(dh2, x1, norm2_w + tok2, dx2, "norm2_bwd")
    dmix = _matmul(dx1b, wf["w_out"], "nt", "d_mix", out_dtype=BF16)
    gw["w_out"] = _matmul(mix, dx1b, "tn", "g_w_out", extra_bf16=True)
    tok3 = scatter_start(["w_out"])
    dcb, dcc, dch, g_sw = _sconv_bwd(proj, sconv8, dmix, "sconv_bwd")
    dq, df, di, dg, g_lb, g_hn = _hgrn_bwd(proj, lb0, lb1, hgrn_norm_w + tok3, o_h, states, dmix, "hgrn_bwd")
    dproj = jnp.concatenate([dq, df, di, dg, dcb, dcc, dch], axis=1)
    gw["w_in"] = _matmul(h1, dproj, "tn", "g_w_in", extra_bf16=True)
    tok4 = scatter_start(["w_in"])
    dh1 = _matmul(dproj, wf["w_in"], "nt", "d_h1", out_dtype=BF16)
    dx, _, g_n1 = _rmsnorm_bwd(dh1, xs, norm1_w + tok4, dx1, "norm1_bwd")

    small = [g_n1, g_n2, g_n3, g_final, g_nm, g_lb, g_hn, g_fb,
             g_sw[0:8], g_sw[8:16], g_sw[16:24], g_fw[0:8], g_fw[8:16], g_fw[16:24], loss8]
    widths = [a.shape[1] for a in small]
    tot = _all_reduce_small(jnp.concatenate(small, axis=1), "all_reduce_small")
    offs = [0]
    for wd_ in widths:
        offs.append(offs[-1] + wd_)
    sm = [tot[:, offs[i]:offs[i + 1]] for i in range(len(small))]
    s_n1, s_n2, s_n3, s_final, s_nm, s_lb, s_hn, s_fb = sm[:8]
    s_sw = jnp.concatenate(sm[8:11], axis=0)
    s_fw = jnp.concatenate(sm[11:14], axis=0)
    loss = sm[14][0, 0]
    slot = 2 * lax.axis_index("x") + lax.axis_index("y")
    s_sw = lax.dynamic_slice_in_dim(s_sw, slot * (HGRN_W // N_CHIPS), HGRN_W // N_CHIPS, axis=1)
    fsh = ffn_conv_w.shape[2]
    s_fw = lax.dynamic_slice_in_dim(s_fw, slot * fsh, fsh, axis=1)
    s_lb2 = jnp.concatenate([s_lb, -s_lb], axis=0)

    swaps = []
    after = tot
    for grp, sems, g_thru, lands in pending:
        got = _scatter_wait(g_thru, lands, [big[n][1] for n in grp], sems, after, "scatter_wait_" + grp[0])
        sums = [_sum4(gw[n][0], big[n][1], slot_arr, r, "core_sum_" + n) for n, r in zip(grp, got)]
        ssems, s_thru, s_lands, after = _sibling_start(sums, "sibling_start_" + grp[0])
        swaps.append((grp, ssems, s_thru, s_lands))

    moments = {"hgrn_lb": (m_hgrn_lb, v_hgrn_lb), "norm1_w": (m_norm1_w, v_norm1_w), "w_in": (m_w_in, v_w_in),
               "hgrn_norm_w": (m_hgrn_norm_w, v_hgrn_norm_w), "sconv_w": (m_sconv_w, v_sconv_w),
               "w_out": (m_w_out, v_w_out), "norm2_w": (m_norm2_w, v_norm2_w),
               "mem_norm_w": (m_mem_norm_w, v_mem_norm_w), "wq": (m_wq, v_wq), "wk": (m_wk, v_wk), "wv": (m_wv, v_wv),
               "wo": (m_wo, v_wo), "norm3_w": (m_norm3_w, v_norm3_w), "w_gate": (m_w_gate, v_w_gate),
               "w_up": (m_w_up, v_w_up), "ffn_conv_w": (m_ffn_conv_w, v_ffn_conv_w),
               "ffn_conv_b": (m_ffn_conv_b, v_ffn_conv_b), "w_down": (m_w_down, v_w_down),
               "final_norm_w": (m_final_norm_w, v_final_norm_w)}
    weights = {"hgrn_lb": hgrn_lb, "norm1_w": norm1_w, "w_in": w_in, "hgrn_norm_w": hgrn_norm_w, "sconv_w": sconv_w,
               "w_out": w_out, "norm2_w": norm2_w, "mem_norm_w": mem_norm_w, "wq": wq, "wk": wk, "wv": wv, "wo": wo,
               "norm3_w": norm3_w, "w_gate": w_gate, "w_up": w_up, "ffn_conv_w": ffn_conv_w, "ffn_conv_b": ffn_conv_b,
               "w_down": w_down, "final_norm_w": final_norm_w}
    small_g = {"hgrn_lb": s_lb2, "norm1_w": s_n1, "hgrn_norm_w": s_hn, "sconv_w": s_sw, "norm2_w": s_n2,
               "mem_norm_w": s_nm, "norm3_w": s_n3, "ffn_conv_w": s_fw, "ffn_conv_b": s_fb, "final_norm_w": s_final}
    order = list(weights)
    res = {}

    def adamw(n, parts):
        shape = weights[n].shape
        w2 = weights[n].reshape((-1, shape[-1]))
        m2, v2 = (t.reshape(w2.shape) for t in moments[n])
        res[n] = [t.reshape(shape) for t in _adamw(w2, [p.reshape(w2.shape) for p in parts], m2, v2, "adamw_" + n)]

    for n in order:
        if n not in big:
            adamw(n, [small_g[n]])
    after = after + res["final_norm_w"][1][0]
    for grp, ssems, s_thru, s_lands in swaps:
        own, other = _sibling_wait(s_thru, s_lands, ssems, after, "sibling_wait_" + grp[0])
        for n, a, b in zip(grp, own, other):
            adamw(n, [a, b])
        after = res[grp[-1]][1]

    return (loss, dx[None], *[res[n][0] for n in order], *[res[n][1] for n in order],
            *[res[n][2] for n in order], *[res[n][3] for n in order])
```

```python
import functools

import jax
import jax.numpy as jnp
from jax import lax
from jax.experimental import pallas as pl
from jax.experimental.pallas import tpu as pltpu

F32 = jnp.float32
BF16 = jnp.bfloat16
MESH = pl.DeviceIdType.MESH

EPS = 1e-6
HGRN_W = 1024
HEAD = 128
N_HEADS = 8
CHUNK = 64
HGRN_UNROLL = 8
HGRN_HEADS_PER_STEP = 2
MEM_HEADS = 4
MEM_HEAD_DIM = 512
N_CHIPS = 4
HALO = 8

ADAM_LR = 0.001
ADAM_B1 = 0.9
ADAM_B2 = 0.999
ADAM_EPS = 1e-08
ADAM_WD = 0.01
ADAM_STEP = 10


def _sigmoid(x):
    return 1.0 / (1.0 + jnp.exp(-x))


def _dot(a, b, dims):
    return lax.dot_general(a.astype(BF16), b.astype(BF16), (dims, ((), ())),
                           preferred_element_type=F32)


def _dot_nn(a, b):
    return _dot(a, b, ((1,), (0,)))


def _dot_nt(a, b):
    return _dot(a, b, ((1,), (1,)))


def _dot_tn(a, b):
    return _dot(a, b, ((0,), (0,)))


def _hdot(a, b, dims):
    return lax.dot_general(a, b, (dims, ((), ())), precision=lax.Precision.HIGH, preferred_element_type=F32)


def _hdot_nn(a, b):
    return _hdot(a, b, ((1,), (0,)))


def _hdot_nt(a, b):
    return _hdot(a, b, ((1,), (1,)))


def _hdot_tn(a, b):
    return _hdot(a, b, ((0,), (0,)))


def _exact_ones_dot(ones_bf16, x):
    hi = x.astype(BF16)
    r1 = x - hi.astype(F32)
    mid = r1.astype(BF16)
    lo = (r1 - mid.astype(F32)).astype(BF16)
    dims = (((1,), (0,)), ((), ()))
    return (lax.dot_general(ones_bf16, hi, dims, preferred_element_type=F32)
            + lax.dot_general(ones_bf16, mid, dims, preferred_element_type=F32)
            + lax.dot_general(ones_bf16, lo, dims, preferred_element_type=F32))


def _rows8(v):
    t, c = v.shape
    return v.reshape(t // 8, 8, c).sum(axis=0)


def _shift_down(x, halo, s):
    rolled = pltpu.roll(x, s, 0)
    hrolled = pltpu.roll(halo, s, 0)
    row = lax.broadcasted_iota(jnp.int32, hrolled.shape, 0)
    head = jnp.where(row < s, hrolled, rolled[:HALO])
    return jnp.concatenate([head, rolled[HALO:]], axis=0)


def _shift_up(x, halo, s):
    t = x.shape[0]
    rolled = pltpu.roll(x, t - s, 0)
    hrolled = pltpu.roll(halo, HALO - s, 0)
    row = lax.broadcasted_iota(jnp.int32, hrolled.shape, 0)
    tail = jnp.where(row >= HALO - s, hrolled, rolled[t - HALO:])
    return jnp.concatenate([rolled[:t - HALO], tail], axis=0)


def _params(*sem):
    return pltpu.CompilerParams(dimension_semantics=sem)


def _row_tile(r, pref):
    while r % pref:
        pref //= 2
    return pref


def _rmsnorm_fwd(x, w, name, tm=256):
    s, d = x.shape
    tm = min(tm, s)

    def body(x_ref, w_ref, o_ref):
        xv = x_ref[...]
        r = lax.rsqrt(jnp.mean(xv * xv, axis=-1, keepdims=True) + EPS)
        o_ref[...] = ((xv * r) * w_ref[...]).astype(BF16)

    return pl.pallas_call(
        body, name=name, grid=(s // tm,),
        in_specs=[pl.BlockSpec((tm, d), lambda i: (i, 0)), pl.BlockSpec((1, d), lambda i: (0, 0))],
        out_specs=pl.BlockSpec((tm, d), lambda i: (i, 0)),
        out_shape=jax.ShapeDtypeStruct((s, d), BF16),
        compiler_params=_params("parallel"),
    )(x, w)


def _rmsnorm_bwd(dh, x, w, dres, name, tm=256):
    s, d = x.shape
    tm = min(tm, s)
    has_res = dres is not None

    def body(*refs):
        if has_res:
            dh_ref, x_ref, w_ref, dres_ref, dx_ref, dxb_ref, gw_ref = refs
        else:
            dh_ref, x_ref, w_ref, dx_ref, dxb_ref, gw_ref = refs

        @pl.when(pl.program_id(0) == 0)
        def _():
            gw_ref[...] = jnp.zeros_like(gw_ref)

        xv = x_ref[...]
        dhv = dh_ref[...].astype(F32)
        r = lax.rsqrt(jnp.mean(xv * xv, axis=-1, keepdims=True) + EPS)
        xhat = xv * r
        gw_ref[...] += _rows8(dhv * xhat)
        dxh = dhv * w_ref[...]
        dx = r * (dxh - xhat * jnp.mean(dxh * xhat, axis=-1, keepdims=True))
        if has_res:
            dx = dres_ref[...] + dx
        dx_ref[...] = dx
        dxb_ref[...] = dx.astype(BF16)

    row = pl.BlockSpec((tm, d), lambda i: (i, 0))
    in_specs = [row, row, pl.BlockSpec((1, d), lambda i: (0, 0))] + ([row] if has_res else [])
    args = (dh, x, w) + ((dres,) if has_res else ())
    return pl.pallas_call(
        body, name=name, grid=(s // tm,),
        in_specs=in_specs,
        out_specs=[row, row, pl.BlockSpec((8, d), lambda i: (0, 0))],
        out_shape=[jax.ShapeDtypeStruct((s, d), F32), jax.ShapeDtypeStruct((s, d), BF16),
                   jax.ShapeDtypeStruct((8, d), F32)],
        compiler_params=_params("arbitrary"),
    )(*args)


def _final_loss_bwd(x3, target, w, name, tm=256):
    s, d = x3.shape
    tm = min(tm, s)

    def body(x_ref, t_ref, w_ref, dx_ref, dxb_ref, gw_ref, loss_ref):
        @pl.when(pl.program_id(0) == 0)
        def _():
            gw_ref[...] = jnp.zeros_like(gw_ref)
            loss_ref[...] = jnp.zeros_like(loss_ref)

        xv = x_ref[...]
        r = lax.rsqrt(jnp.mean(xv * xv, axis=-1, keepdims=True) + EPS)
        xhat = xv * r
        y = xhat * w_ref[...]
        err = y - t_ref[...]
        part = 0.5 * jnp.mean(err * err, axis=-1, keepdims=True)
        tot = jnp.sum(part, axis=0, keepdims=True)
        rr = lax.broadcasted_iota(jnp.int32, loss_ref.shape, 0)
        cc = lax.broadcasted_iota(jnp.int32, loss_ref.shape, 1)
        loss_ref[...] += jnp.where((rr == 0) & (cc == 0), tot, 0.0)
        dy = err * (1.0 / d)
        gw_ref[...] += _rows8(dy * xhat)
        dxh = dy * w_ref[...]
        dx = r * (dxh - xhat * jnp.mean(dxh * xhat, axis=-1, keepdims=True))
        dx_ref[...] = dx
        dxb_ref[...] = dx.astype(BF16)

    row = pl.BlockSpec((tm, d), lambda i: (i, 0))
    return pl.pallas_call(
        body, name=name, grid=(s // tm,),
        in_specs=[row, row, pl.BlockSpec((1, d), lambda i: (0, 0))],
        out_specs=[row, row, pl.BlockSpec((8, d), lambda i: (0, 0)), pl.BlockSpec((8, 128), lambda i: (0, 0))],
        out_shape=[jax.ShapeDtypeStruct((s, d), F32), jax.ShapeDtypeStruct((s, d), BF16),
                   jax.ShapeDtypeStruct((8, d), F32), jax.ShapeDtypeStruct((8, 128), F32)],
        compiler_params=_params("arbitrary"),
    )(x3, target, w)


MM_TILES = (1024, 1408, 512, 256, 128)
MM_K_TILES = (2816, 2048, 1792, 1408, 1024, 512, 256, 128)
MM_VMEM_LIMIT = 56 * 1024 * 1024
MM_VMEM_BUDGET = 46 * 1024 * 1024


def _pick_tile(dim):
    for t in MM_TILES:
        if dim % t == 0:
            return t
    return dim


def _matmul(a, b, mode, name, *, out_dtype=F32, residual=None, extra_bf16=False, tm=None, tn=None, tk=None):
    if mode == "nn":
        (m, k), (k2, n) = a.shape, b.shape
    elif mode == "nt":
        (m, k), (n, k2) = a.shape, b.shape
    else:
        (k, m), (k2, n) = a.shape, b.shape
    assert k == k2, (a.shape, b.shape, mode)
    tm = _pick_tile(m) if tm is None else min(tm, m)
    tn = _pick_tile(n) if tn is None else min(tn, n)
    out_bytes = tm * tn * (jnp.dtype(out_dtype).itemsize + (2 if extra_bf16 else 0) + (4 if residual is not None else 0))

    def vmem_bytes(t):
        return (2 * (tm * t * a.dtype.itemsize + t * tn * b.dtype.itemsize) + 2 * out_bytes
                + (tm * tn * 4 if t < k else 0))

    if tk is None:
        tk = next(t for t in MM_K_TILES if k % t == 0 and t <= k and vmem_bytes(t) <= MM_VMEM_BUDGET)
    assert m % tm == 0 and n % tn == 0 and k % tk == 0, (m, n, k, tm, tn, tk)
    nk = k // tk
    dims = {"nn": ((1,), (0,)), "nt": ((1,), (1,)), "tn": ((0,), (0,))}[mode]
    has_res = residual is not None

    def body(*refs):
        refs = list(refs)
        a_ref, b_ref = refs[0], refs[1]
        r_ref = refs[2] if has_res else None
        outs = refs[2 + has_res:]
        o_ref = outs[0]
        o2_ref = outs[1] if extra_bf16 else None
        def finish(r):
            if has_res:
                r = r_ref[...] + r
            o_ref[...] = r.astype(out_dtype)
            if extra_bf16:
                o2_ref[...] = r.astype(BF16)

        if nk == 1:
            finish(_dot(a_ref[...], b_ref[...], dims))
            return
        acc = outs[-1]
        kk = pl.program_id(2)

        @pl.when(kk == 0)
        def _():
            acc[...] = _dot(a_ref[...], b_ref[...], dims)

        if nk > 2:
            @pl.when((kk > 0) & (kk < nk - 1))
            def _():
                acc[...] += _dot(a_ref[...], b_ref[...], dims)

        @pl.when(kk == nk - 1)
        def _():
            finish(acc[...] + _dot(a_ref[...], b_ref[...], dims))

    if mode == "tn":
        a_spec = pl.BlockSpec((tk, tm), lambda i, j, kk: (kk, i))
    else:
        a_spec = pl.BlockSpec((tm, tk), lambda i, j, kk: (i, kk))
    if mode == "nt":
        b_spec = pl.BlockSpec((tn, tk), lambda i, j, kk: (j, kk))
    else:
        b_spec = pl.BlockSpec((tk, tn), lambda i, j, kk: (kk, j))
    o_spec = pl.BlockSpec((tm, tn), lambda i, j, kk: (i, j))
    in_specs = [a_spec, b_spec] + ([o_spec] if has_res else [])
    out_specs = [o_spec] + ([o_spec] if extra_bf16 else [])
    out_shape = [jax.ShapeDtypeStruct((m, n), out_dtype)] + ([jax.ShapeDtypeStruct((m, n), BF16)] if extra_bf16 else [])
    args = (a, b) + ((residual,) if has_res else ())
    res = pl.pallas_call(
        body, name=name, grid=(m // tm, n // tn, nk),
        in_specs=in_specs, out_specs=out_specs, out_shape=out_shape,
        scratch_shapes=[pltpu.VMEM((tm, tn) if nk > 1 else (8, 128), F32)],
        compiler_params=pltpu.CompilerParams(dimension_semantics=("parallel", "parallel", "arbitrary"),
                                             vmem_limit_bytes=MM_VMEM_LIMIT),
    )(*args)
    return res if extra_bf16 else res[0]


def _hgrn_gates(qp, fp, lb):
    sig = _sigmoid(fp)
    f = lb + (1.0 - lb) * sig
    logf = jnp.log(f)
    k = 1.0 - f
    sq = _sigmoid(qp)
    q = qp * sq
    return sig, f, logf, k, sq, q


def _hgrn_fwd(proj, lb0, lb1, norm_w, name, tb=512):
    s = proj.shape[0]
    tb = min(tb, s)
    nb, ncb = s // tb, tb // CHUNK

    def body(q_ref, f_ref, i_ref, g_ref, a0_ref, a1_ref, nw_ref, o_ref, og_ref, st_ref, state):
        @pl.when(pl.program_id(1) == 0)
        def _():
            state[...] = jnp.zeros_like(state)

        lb2 = _sigmoid(a0_ref[...] - a1_ref[...])
        row = lax.broadcasted_iota(jnp.int32, (CHUNK, CHUNK), 0)
        col = lax.broadcasted_iota(jnp.int32, (CHUNK, CHUNK), 1)
        tril = row >= col
        ones_l = tril.astype(BF16)
        nw = nw_ref[...]

        def chunk(c, carry):
            rows = pl.ds(pl.multiple_of(c * CHUNK, CHUNK), CHUNK)
            for hh in range(HGRN_HEADS_PER_STEP):
                cols = slice(hh * HEAD, (hh + 1) * HEAD)
                v = i_ref[rows, cols].astype(F32)
                _, _, logf, k, _, q = _hgrn_gates(q_ref[rows, cols].astype(F32), f_ref[rows, cols].astype(F32),
                                                  lb2[:, cols])
                b = _exact_ones_dot(ones_l, logf)
                bl = jnp.sum(logf, axis=0, keepdims=True)
                bm = 0.5 * bl
                st = state[hh]
                st_ref[hh, c] = st
                qt = q * jnp.exp(b - bm)
                kt = k * jnp.exp(bm - b)
                a = jnp.where(tril, _dot_nt(qt, kt), 0.0)
                o = _dot_nt(q * jnp.exp(b), st) + _dot_nn(a, v)
                state[hh] = st * jnp.exp(bl) + _dot_tn(v, k * jnp.exp(bl - b))
                o_ref[rows, cols] = o
                on = (o * lax.rsqrt(jnp.mean(o * o, axis=-1, keepdims=True) + EPS)) * nw
                gv = g_ref[rows, cols].astype(F32)
                og_ref[rows, cols] = (on * (gv * _sigmoid(gv))).astype(BF16)
            return carry

        lax.fori_loop(0, ncb, chunk, 0, unroll=HGRN_UNROLL)

    hp, wd = HGRN_HEADS_PER_STEP, HGRN_HEADS_PER_STEP * HEAD
    ngrp = N_HEADS // hp

    def colblk(group):
        return pl.BlockSpec((tb, wd), lambda h, j: (j, group * ngrp + h))

    vec = pl.BlockSpec((1, wd), lambda h, j: (0, h))
    out_blk = pl.BlockSpec((tb, wd), lambda h, j: (j, h))
    return pl.pallas_call(
        body, name=name, grid=(ngrp, nb),
        in_specs=[colblk(0), colblk(1), colblk(2), colblk(3), vec, vec, pl.BlockSpec((1, HEAD), lambda h, j: (0, 0))],
        out_specs=[out_blk, out_blk, pl.BlockSpec((hp, ncb, HEAD, HEAD), lambda h, j: (h, j, 0, 0))],
        out_shape=[jax.ShapeDtypeStruct((s, HGRN_W), F32), jax.ShapeDtypeStruct((s, 2 * HGRN_W), BF16),
                   jax.ShapeDtypeStruct((N_HEADS, s // CHUNK, HEAD, HEAD), F32)],
        scratch_shapes=[pltpu.VMEM((hp, HEAD, HEAD), F32)],
        compiler_params=_params("parallel", "arbitrary"),
    )(proj, proj, proj, proj, lb0, lb1, norm_w)


def _hgrn_bwd(proj, lb0, lb1, norm_w, o, states, dmix, name, tb=512):
    s = proj.shape[0]
    tb = min(tb, s)
    nb, ncb = s // tb, tb // CHUNK

    def body(q_ref, f_ref, i_ref, g_ref, a0_ref, a1_ref, nw_ref, o_ref, st_ref, dm_ref,
             dq_ref, df_ref, di_ref, dg_ref, glb_ref, gnw_ref, dstate):
        h = pl.program_id(0)

        @pl.when(pl.program_id(1) == 0)
        def _():
            dstate[...] = jnp.zeros_like(dstate)
            glb_ref[...] = jnp.zeros_like(glb_ref)

        @pl.when((pl.program_id(1) == 0) & (h == 0))
        def _():
            gnw_ref[...] = jnp.zeros_like(gnw_ref)

        lb2 = _sigmoid(a0_ref[...] - a1_ref[...])
        row = lax.broadcasted_iota(jnp.int32, (CHUNK, CHUNK), 0)
        col = lax.broadcasted_iota(jnp.int32, (CHUNK, CHUNK), 1)
        tril = row >= col
        ones_l = tril.astype(BF16)
        ones_u = (row <= col).astype(BF16)
        nw = nw_ref[...]

        def chunk(cc, carry):
            c = ncb - 1 - cc
            rows = pl.ds(pl.multiple_of(c * CHUNK, CHUNK), CHUNK)
            for hh in range(HGRN_HEADS_PER_STEP):
                cols = slice(hh * HEAD, (hh + 1) * HEAD)
                lb = lb2[:, cols]
                qp = q_ref[rows, cols].astype(F32)
                v = i_ref[rows, cols].astype(F32)
                sig, f, logf, k, sq, q = _hgrn_gates(qp, f_ref[rows, cols].astype(F32), lb)
                gv = g_ref[rows, cols].astype(F32)
                sg = _sigmoid(gv)
                silu_g = gv * sg
                dog = dm_ref[rows, cols].astype(F32)
                ov = o_ref[rows, cols]
                r = lax.rsqrt(jnp.mean(ov * ov, axis=-1, keepdims=True) + EPS)
                ohat = ov * r
                on = ohat * nw
                dg_ref[rows, cols] = (dog * on * (sg * (1.0 + gv * (1.0 - sg)))).astype(BF16)
                don = dog * silu_g
                gnw_ref[...] += _rows8(don * ohat)
                doh = don * nw
                do = r * (doh - ohat * jnp.mean(doh * ohat, axis=-1, keepdims=True))
                b = _exact_ones_dot(ones_l, logf)
                bl = jnp.sum(logf, axis=0, keepdims=True)
                bm = 0.5 * bl
                e_q = jnp.exp(b - bm)
                e_k = jnp.exp(bm - b)
                e_b = jnp.exp(b)
                e_l = jnp.exp(bl - b)
                qt, kt, qb, kb = q * e_q, k * e_k, q * e_b, k * e_l
                st0 = st_ref[hh, c]
                dst = dstate[hh]
                a = jnp.where(tril, _dot_nt(qt, kt), 0.0)
                da = jnp.where(tril, _dot_nt(do, v), 0.0)
                dq = _hdot_nn(da, kt) * e_q + _hdot_nn(do, st0) * e_b
                dkb = _hdot_nn(v, dst) * e_l
                dk = _hdot_tn(da, qt) * e_k + dkb
                dv = _dot_tn(a, do) + _dot_nt(kb, dst)
                e_bl = jnp.exp(bl)
                dstate[hh] = dst * e_bl + _dot_tn(do, qb)
                db = q * dq - k * dk
                db_last = jnp.sum(k * dkb, axis=0, keepdims=True) + e_bl * jnp.sum(st0 * dst, axis=0, keepdims=True)
                dlogf = _exact_ones_dot(ones_u, db) + db_last
                dfg = dlogf / f - dk
                df_ref[rows, cols] = (dfg * (1.0 - lb) * (sig * (1.0 - sig))).astype(BF16)
                glb_ref[:, cols] += _rows8(dfg * (1.0 - sig)) * (lb * (1.0 - lb))
                dq_ref[rows, cols] = (dq * (sq * (1.0 + qp * (1.0 - sq)))).astype(BF16)
                di_ref[rows, cols] = dv.astype(BF16)
            return carry

        lax.fori_loop(0, ncb, chunk, 0, unroll=HGRN_UNROLL)

    hp, wd = HGRN_HEADS_PER_STEP, HGRN_HEADS_PER_STEP * HEAD
    ngrp = N_HEADS // hp

    def colblk(group):
        return pl.BlockSpec((tb, wd), lambda h, j: (nb - 1 - j, group * ngrp + h))

    vec = pl.BlockSpec((1, wd), lambda h, j: (0, h))
    blk = pl.BlockSpec((tb, wd), lambda h, j: (nb - 1 - j, h))
    grad = jax.ShapeDtypeStruct((s, HGRN_W), BF16)
    return pl.pallas_call(
        body, name=name, grid=(ngrp, nb),
        in_specs=[colblk(0), colblk(1), colblk(2), colblk(3), vec, vec, pl.BlockSpec((1, HEAD), lambda h, j: (0, 0)),
                  blk, pl.BlockSpec((hp, ncb, HEAD, HEAD), lambda h, j: (h, nb - 1 - j, 0, 0)), blk],
        out_specs=[blk, blk, blk, blk, pl.BlockSpec((8, wd), lambda h, j: (0, h)),
                   pl.BlockSpec((8, HEAD), lambda h, j: (0, 0))],
        out_shape=[grad, grad, grad, grad, jax.ShapeDtypeStruct((8, HGRN_W), F32), jax.ShapeDtypeStruct((8, HEAD), F32)],
        scratch_shapes=[pltpu.VMEM((hp, HEAD, HEAD), F32)],
        compiler_params=_params("arbitrary", "arbitrary"),
    )(proj, proj, proj, proj, lb0, lb1, norm_w, o, states, dmix)


HALO_BLK = 16


def _f32(ref):
    return ref[...].astype(F32)


def _halo_prev(ref):
    return ref[...].astype(F32)[HALO_BLK - HALO:]


def _halo_next(ref):
    return ref[...].astype(F32)[:HALO]


def _conv3(x0, x1, x2, w_ref):
    y = x0 * w_ref[0:1, :]
    y = y + x1 * w_ref[1:2, :]
    return y + x2 * w_ref[2:3, :]


def _sconv_fwd(proj, w8, mix, name, tb=256):
    s = proj.shape[0]
    tb = min(tb, s)
    hb = tb // HALO_BLK

    def body(cb_ref, cc_ref, ch_ref, cch_ref, chh_ref, w_ref, mix_ref, y_ref):
        first = pl.program_id(0) == 0
        u = _f32(cc_ref) * _f32(ch_ref)
        uh = jnp.where(first, 0.0, _halo_prev(cch_ref) * _halo_prev(chh_ref))
        conv = _conv3(_shift_down(u, uh, 2), _shift_down(u, uh, 1), u, w_ref)
        y_ref[...] = (_f32(cb_ref) * conv).astype(BF16)

    def blk(g):
        return pl.BlockSpec((tb, HGRN_W), lambda j: (j, g))

    def halo(g):
        return pl.BlockSpec((HALO_BLK, HGRN_W), lambda j: (jnp.maximum(j * hb - 1, 0), g))

    return pl.pallas_call(
        body, name=name, grid=(s // tb,),
        in_specs=[blk(4), blk(5), blk(6), halo(5), halo(6), pl.BlockSpec((HALO, HGRN_W), lambda j: (0, 0)),
                  pl.BlockSpec(memory_space=pl.ANY)],
        out_specs=pl.BlockSpec((tb, HGRN_W), lambda j: (j, 1)),
        out_shape=jax.ShapeDtypeStruct(mix.shape, BF16),
        input_output_aliases={6: 0},
        compiler_params=_params("parallel"),
    )(proj, proj, proj, proj, proj, w8, mix)


def _sconv_bwd(proj, w8, dmix, name, tb=256):
    s = proj.shape[0]
    tb = min(tb, s)
    hb = tb // HALO_BLK
    nb = s // tb
    last_h = s // HALO_BLK - 1

    def body(cb_ref, cc_ref, ch_ref, cch_ref, chh_ref, cbn_ref, dy_ref, dyn_ref, w_ref,
             dcb_ref, dcc_ref, dch_ref, gw_ref):
        j = pl.program_id(0)

        @pl.when(j == 0)
        def _():
            gw_ref[...] = jnp.zeros_like(gw_ref)

        cc, ch, cb = _f32(cc_ref), _f32(ch_ref), _f32(cb_ref)
        u = cc * ch
        uh = jnp.where(j == 0, 0.0, _halo_prev(cch_ref) * _halo_prev(chh_ref))
        u2, u1 = _shift_down(u, uh, 2), _shift_down(u, uh, 1)
        conv = _conv3(u2, u1, u, w_ref)
        dy = _f32(dy_ref)
        dcb_ref[...] = (dy * conv).astype(BF16)
        dc = dy * cb
        dcn = jnp.where(j == nb - 1, 0.0, _halo_next(dyn_ref) * _halo_next(cbn_ref))
        gw_ref[0:8, :] += _rows8(dc * u2)
        gw_ref[8:16, :] += _rows8(dc * u1)
        gw_ref[16:24, :] += _rows8(dc * u)
        du = dc * w_ref[2:3, :] + _shift_up(dc, dcn, 1) * w_ref[1:2, :] + _shift_up(dc, dcn, 2) * w_ref[0:1, :]
        dcc_ref[...] = (du * ch).astype(BF16)
        dch_ref[...] = (du * cc).astype(BF16)

    def blk(g):
        return pl.BlockSpec((tb, HGRN_W), lambda j: (j, g))

    def halo_prev(g):
        return pl.BlockSpec((HALO_BLK, HGRN_W), lambda j: (jnp.maximum(j * hb - 1, 0), g))

    def halo_next(g):
        return pl.BlockSpec((HALO_BLK, HGRN_W), lambda j: (jnp.minimum((j + 1) * hb, last_h), g))

    out = pl.BlockSpec((tb, HGRN_W), lambda j: (j, 0))
    grad = jax.ShapeDtypeStruct((s, HGRN_W), BF16)
    return pl.pallas_call(
        body, name=name, grid=(nb,),
        in_specs=[blk(4), blk(5), blk(6), halo_prev(5), halo_prev(6), halo_next(4), blk(1), halo_next(1),
                  pl.BlockSpec((HALO, HGRN_W), lambda j: (0, 0))],
        out_specs=[out, out, out, pl.BlockSpec((24, HGRN_W), lambda j: (0, 0))],
        out_shape=[grad, grad, grad, jax.ShapeDtypeStruct((24, HGRN_W), F32)],
        compiler_params=_params("arbitrary"),
    )(proj, proj, proj, proj, proj, proj, dmix, dmix, w8)


def _attn_fwd(q, kk, vv, name, tb=256):
    s, d = q.shape
    m = kk.shape[0]
    tb = min(tb, s)
    scale = MEM_HEAD_DIM ** -0.5

    def body(q_ref, k_ref, v_ref, o_ref):
        for hh in range(MEM_HEADS):
            cols = slice(hh * MEM_HEAD_DIM, (hh + 1) * MEM_HEAD_DIM)
            sc = _dot_nt(q_ref[:, cols], k_ref[:, cols]) * scale
            sc = sc - jnp.max(sc, axis=-1, keepdims=True)
            e = jnp.exp(sc)
            p = e / jnp.sum(e, axis=-1, keepdims=True)
            o_ref[:, cols] = _dot_nn(p, v_ref[:, cols]).astype(BF16)

    full = pl.BlockSpec((m, d), lambda i: (0, 0))
    return pl.pallas_call(
        body, name=name, grid=(s // tb,),
        in_specs=[pl.BlockSpec((tb, d), lambda i: (i, 0)), full, full],
        out_specs=pl.BlockSpec((tb, d), lambda i: (i, 0)),
        out_shape=jax.ShapeDtypeStruct((s, d), BF16),
        compiler_params=_params("parallel"),
    )(q, kk, vv)


def _attn_bwd(q, kk, vv, datt, name, tb=256):
    s, d = q.shape
    m = kk.shape[0]
    tb = min(tb, s)
    scale = MEM_HEAD_DIM ** -0.5

    def body(q_ref, k_ref, v_ref, do_ref, dq_ref, dk_ref, dv_ref):
        @pl.when(pl.program_id(0) == 0)
        def _():
            dk_ref[...] = jnp.zeros_like(dk_ref)
            dv_ref[...] = jnp.zeros_like(dv_ref)

        for hh in range(MEM_HEADS):
            cols = slice(hh * MEM_HEAD_DIM, (hh + 1) * MEM_HEAD_DIM)
            qh, kh, vh, doh = q_ref[:, cols], k_ref[:, cols], v_ref[:, cols], do_ref[:, cols]
            sc = _dot_nt(qh, kh) * scale
            sc = sc - jnp.max(sc, axis=-1, keepdims=True)
            e = jnp.exp(sc)
            p = e / jnp.sum(e, axis=-1, keepdims=True)
            dp = _dot_nt(doh, vh)
            ds = p * (dp - jnp.sum(dp * p, axis=-1, keepdims=True)) * scale
            dq_ref[:, cols] = _dot_nn(ds, kh).astype(BF16)
            dk_ref[:, cols] += _dot_tn(ds, qh)
            dv_ref[:, cols] += _dot_tn(p, doh)

    full = pl.BlockSpec((m, d), lambda i: (0, 0))
    row = pl.BlockSpec((tb, d), lambda i: (i, 0))
    return pl.pallas_call(
        body, name=name, grid=(s // tb,),
        in_specs=[row, full, full, row],
        out_specs=[row, full, full],
        out_shape=[jax.ShapeDtypeStruct((s, d), BF16), jax.ShapeDtypeStruct((m, d), F32),
                   jax.ShapeDtypeStruct((m, d), F32)],
        compiler_params=_params("arbitrary"),
    )(q, kk, vv, datt)


def _ffn_fwd(g, u, w8, bias, name, tb=512, tc=1408):
    s, f = g.shape
    tb = min(tb, s)
    tc = tc if f % tc == 0 else 512
    hb = tb // HALO_BLK

    def body(g_ref, gh_ref, u_ref, w_ref, b_ref, z_ref):
        gv = _f32(g_ref)
        gh = jnp.where(pl.program_id(1) == 0, 0.0, _halo_prev(gh_ref))
        a = _conv3(_shift_down(gv, gh, 2), _shift_down(gv, gh, 1), gv, w_ref) + b_ref[...]
        z_ref[...] = ((a * _sigmoid(a)) * _f32(u_ref)).astype(BF16)

    blk = pl.BlockSpec((tb, tc), lambda c, j: (j, c))
    return pl.pallas_call(
        body, name=name, grid=(f // tc, s // tb),
        in_specs=[blk, pl.BlockSpec((HALO_BLK, tc), lambda c, j: (jnp.maximum(j * hb - 1, 0), c)), blk,
                  pl.BlockSpec((HALO, tc), lambda c, j: (0, c)), pl.BlockSpec((1, tc), lambda c, j: (0, c))],
        out_specs=blk,
        out_shape=jax.ShapeDtypeStruct((s, f), BF16),
        compiler_params=pltpu.CompilerParams(dimension_semantics=("parallel", "parallel"),
                                             vmem_limit_bytes=MM_VMEM_LIMIT),
    )(g, g, u, w8, bias)


def _ffn_bwd(g, u, dz, w8, bias, name, tb=256, tc=1408):
    s, f = g.shape
    tb = min(tb, s)
    tc = tc if f % tc == 0 else 512
    hb = tb // HALO_BLK
    nb = s // tb

    def body(g_ref, gh_ref, u_ref, dz_ref, w_ref, b_ref, dg_ref, du_ref, gb_ref, gw_ref, da_next):
        jj = pl.program_id(1)

        @pl.when(jj == 0)
        def _():
            gb_ref[...] = jnp.zeros_like(gb_ref)
            gw_ref[...] = jnp.zeros_like(gw_ref)
            da_next[...] = jnp.zeros_like(da_next)

        gv = _f32(g_ref)
        gh = jnp.where(jj == nb - 1, 0.0, _halo_prev(gh_ref))
        g2, g1 = _shift_down(gv, gh, 2), _shift_down(gv, gh, 1)
        a = _conv3(g2, g1, gv, w_ref) + b_ref[...]
        sa = _sigmoid(a)
        dz = _f32(dz_ref)
        du_ref[...] = (dz * (a * sa)).astype(BF16)
        da = dz * _f32(u_ref) * (sa * (1.0 + a * (1.0 - sa)))
        gb_ref[...] += _rows8(da)
        gw_ref[0:8, :] += _rows8(da * g2)
        gw_ref[8:16, :] += _rows8(da * g1)
        gw_ref[16:24, :] += _rows8(da * gv)
        dan = da_next[...]
        dg = da * w_ref[2:3, :] + _shift_up(da, dan, 1) * w_ref[1:2, :] + _shift_up(da, dan, 2) * w_ref[0:1, :]
        dg_ref[...] = dg.astype(BF16)
        da_next[...] = da[:HALO]

    blk = pl.BlockSpec((tb, tc), lambda c, jj: (nb - 1 - jj, c))
    return pl.pallas_call(
        body, name=name, grid=(f // tc, nb),
        in_specs=[blk, pl.BlockSpec((HALO_BLK, tc), lambda c, jj: (jnp.maximum((nb - 1 - jj) * hb - 1, 0), c)), blk, blk,
                  pl.BlockSpec((HALO, tc), lambda c, jj: (0, c)), pl.BlockSpec((1, tc), lambda c, jj: (0, c))],
        out_specs=[blk, blk, pl.BlockSpec((8, tc), lambda c, jj: (0, c)), pl.BlockSpec((24, tc), lambda c, jj: (0, c))],
        out_shape=[jax.ShapeDtypeStruct((s, f), BF16), jax.ShapeDtypeStruct((s, f), BF16),
                   jax.ShapeDtypeStruct((8, f), F32), jax.ShapeDtypeStruct((24, f), F32)],
        scratch_shapes=[pltpu.VMEM((HALO, tc), F32)],
        compiler_params=pltpu.CompilerParams(dimension_semantics=("parallel", "arbitrary"),
                                             vmem_limit_bytes=MM_VMEM_LIMIT),
    )(g, g, u, dz, w8, bias)


def _window(ref, axis, slot, size):
    start = pl.multiple_of(slot * size, size)
    if axis == 0:
        return ref.at[pl.ds(start, size), :]
    return ref.at[:, pl.ds(start, size)]


def _chip_peers():
    x, y, c = lax.axis_index("x"), lax.axis_index("y"), lax.axis_index("c")
    peers = [(1 - x, y, c), (x, 1 - y, c), (1 - x, 1 - y, c)]
    slots = [2 * (1 - x) + y, 2 * x + (1 - y), 2 * (1 - x) + (1 - y)]
    return 2 * x + y, peers, slots


HBM_SPEC = pl.BlockSpec(memory_space=pltpu.HBM)
SEM_SPEC = pl.BlockSpec(memory_space=pltpu.SEMAPHORE)
EFFECT = pltpu.SideEffectType.DATAFLOW_SIDE_EFFECTING


def _hbm(a):
    return pltpu.with_memory_space_constraint(a, pltpu.HBM)


def _cast_into_full(x, axis, slot_arr, dtype, name):
    r, c = x.shape
    tr = _row_tile(r, 256)
    nb = r // tr
    full = (r * N_CHIPS, c) if axis == 0 else (r, c * N_CHIPS)

    def body(slot_ref, x_ref, o_ref):
        o_ref[...] = x_ref[...].astype(dtype)

    if axis == 0:
        out_map = lambda i, s: (s[0] * nb + i, 0)
    else:
        out_map = lambda i, s: (i, s[0])
    return pl.pallas_call(
        body, name=name,
        grid_spec=pltpu.PrefetchScalarGridSpec(
            num_scalar_prefetch=1, grid=(nb,),
            in_specs=[pl.BlockSpec((tr, c), lambda i, s: (i, 0))],
            out_specs=pl.BlockSpec((tr, c), out_map)),
        out_shape=jax.ShapeDtypeStruct(full, dtype),
        compiler_params=_params("parallel"),
    )(slot_arr, x)


def _piece(ref, axis, slot, half):
    size = ref.shape[axis] // N_CHIPS
    if half is None:
        return _window(ref, axis, slot, size)
    if axis == 0:
        h = size // 2
        return ref.at[pl.ds(pl.multiple_of(slot * size + half * h, h), h), :]
    h = ref.shape[0] // 2
    return ref.at[pl.ds(pl.multiple_of(half * h, h), h), pl.ds(pl.multiple_of(slot * size, size), size)]


def _gather_start(fulls, axes, split, groups, name):
    n, ng = len(fulls), len(groups)

    def body(*refs):
        outs = refs[n:]
        sems = outs[:2 * ng]
        thru = outs[2 * ng:2 * ng + n]
        token = outs[-1]
        slot, peers, _ = _chip_peers()
        c = lax.axis_index("c")
        for g, members in enumerate(groups):
            for i, t in enumerate(members):
                mine = _piece(thru[t], axes[t], slot, c if split[t] else None)
                for k in range(3):
                    pltpu.make_async_remote_copy(
                        src_ref=mine, dst_ref=mine, send_sem=sems[2 * g].at[3 * i + k],
                        recv_sem=sems[2 * g + 1].at[3 * i + k], device_id=peers[k], device_id_type=MESH).start()
        token[...] = jnp.zeros_like(token)

    sem_shapes = []
    for members in groups:
        sem_shapes += [pltpu.SemaphoreType.DMA((3 * len(members),))] * 2
    res = pl.pallas_call(
        body, name=name,
        in_specs=[HBM_SPEC] * n,
        out_specs=[SEM_SPEC] * (2 * ng) + [HBM_SPEC] * n + [pl.BlockSpec(memory_space=pltpu.VMEM)],
        out_shape=sem_shapes + [pltpu.HBM(f.shape, f.dtype) for f in fulls] + [jax.ShapeDtypeStruct((8, 128), F32)],
        input_output_aliases={t: 2 * ng + t for t in range(n)},
        compiler_params=pltpu.CompilerParams(has_side_effects=EFFECT),
    )(*[_hbm(f) for f in fulls])
    sems = [(res[2 * g], res[2 * g + 1]) for g in range(ng)]
    return sems, list(res[2 * ng:2 * ng + n]), res[-1]


def _gather_relay(fulls, axes, split, sems, after, name):
    n = len(fulls)
    nsplit = sum(split)

    def body(*refs):
        send_sems, recv_sems = refs[n], refs[n + 1]
        outs = refs[n + 3:]
        d_send, d_recv = outs[0], outs[1]
        thru = outs[2:2 + n]
        token = outs[-1]
        slot, peers, slots = _chip_peers()
        c = lax.axis_index("c")
        sibling = (lax.axis_index("x"), lax.axis_index("y"), 1 - c)
        for t in range(n):
            half = c if split[t] else None
            for k in range(3):
                cp = pltpu.make_async_remote_copy(
                    src_ref=_piece(thru[t], axes[t], slot, half), dst_ref=_piece(thru[t], axes[t], slots[k], half),
                    send_sem=send_sems.at[3 * t + k], recv_sem=recv_sems.at[3 * t + k],
                    device_id=peers[k], device_id_type=MESH)
                cp.wait_send()
                cp.wait_recv()
        i = 0
        for t in range(n):
            if not split[t]:
                continue
            for k in range(3):
                got = _piece(thru[t], axes[t], slots[k], c)
                pltpu.make_async_remote_copy(
                    src_ref=got, dst_ref=got, send_sem=d_send.at[3 * i + k], recv_sem=d_recv.at[3 * i + k],
                    device_id=sibling, device_id_type=MESH).start()
            i += 1
        token[...] = jnp.zeros_like(token)

    res = pl.pallas_call(
        body, name=name,
        in_specs=[HBM_SPEC] * n + [SEM_SPEC, SEM_SPEC, pl.BlockSpec(memory_space=pl.ANY)],
        out_specs=[SEM_SPEC, SEM_SPEC] + [HBM_SPEC] * n + [pl.BlockSpec(memory_space=pltpu.VMEM)],
        out_shape=[pltpu.SemaphoreType.DMA((3 * nsplit,)), pltpu.SemaphoreType.DMA((3 * nsplit,))]
        + [pltpu.HBM(f.shape, f.dtype) for f in fulls] + [jax.ShapeDtypeStruct((8, 128), F32)],
        input_output_aliases={t: 2 + t for t in range(n)},
        compiler_params=pltpu.CompilerParams(has_side_effects=EFFECT),
    )(*fulls, sems[0], sems[1], after)
    return (res[0], res[1]), list(res[2:2 + n]), res[-1]


def _gather_finish(fulls, axes, split, sems, after, name):
    n = len(fulls)

    def body(*refs):
        d_send, d_recv = refs[n], refs[n + 1]
        thru = refs[n + 3:]
        _, _, slots = _chip_peers()
        c = lax.axis_index("c")
        sibling = (lax.axis_index("x"), lax.axis_index("y"), 1 - c)
        i = 0
        for t in range(n):
            if not split[t]:
                continue
            for k in range(3):
                cp = pltpu.make_async_remote_copy(
                    src_ref=_piece(thru[t], axes[t], slots[k], c), dst_ref=_piece(thru[t], axes[t], slots[k], 1 - c),
                    send_sem=d_send.at[3 * i + k], recv_sem=d_recv.at[3 * i + k],
                    device_id=sibling, device_id_type=MESH)
                cp.wait_send()
                cp.wait_recv()
            i += 1

    return pl.pallas_call(
        body, name=name,
        in_specs=[HBM_SPEC] * n + [SEM_SPEC, SEM_SPEC, pl.BlockSpec(memory_space=pl.ANY)],
        out_specs=[HBM_SPEC] * n,
        out_shape=[pltpu.HBM(f.shape, f.dtype) for f in fulls],
        input_output_aliases={t: t for t in range(n)},
        compiler_params=pltpu.CompilerParams(has_side_effects=EFFECT),
    )(*fulls, sems[0], sems[1], after)


def _scatter_start(grads_bf16, axes, name):
    n = len(grads_bf16)

    def shard_shape(g, ax):
        return (g.shape[0] // N_CHIPS, g.shape[1]) if ax == 0 else (g.shape[0], g.shape[1] // N_CHIPS)

    shapes = [shard_shape(g, ax) for g, ax in zip(grads_bf16, axes)]

    def body(*refs):
        outs = refs[2 * n:]
        send_sems, recv_sems = outs[0], outs[1]
        gb, land = outs[2:2 + n], outs[2 + n:2 + 2 * n]
        token = outs[-1]
        _, peers, slots = _chip_peers()
        for t in range(n):
            size = shapes[t][axes[t]]
            for k in range(3):
                pltpu.make_async_remote_copy(
                    src_ref=_window(gb[t], axes[t], slots[k], size), dst_ref=land[t].at[k],
                    send_sem=send_sems.at[3 * t + k], recv_sem=recv_sems.at[3 * t + k],
                    device_id=peers[k], device_id_type=MESH).start()
        token[...] = jnp.zeros_like(token)

    lands = [_hbm(lax.empty((3,) + sh, BF16)) for sh in shapes]
    res = pl.pallas_call(
        body, name=name,
        in_specs=[HBM_SPEC] * (2 * n),
        out_specs=[SEM_SPEC, SEM_SPEC] + [HBM_SPEC] * (2 * n) + [pl.BlockSpec(memory_space=pltpu.VMEM)],
        out_shape=[pltpu.SemaphoreType.DMA((3 * n,)), pltpu.SemaphoreType.DMA((3 * n,))]
        + [pltpu.HBM(g.shape, g.dtype) for g in grads_bf16] + [pltpu.HBM((3,) + sh, BF16) for sh in shapes]
        + [jax.ShapeDtypeStruct((8, 128), F32)],
        input_output_aliases={t: 2 + t for t in range(2 * n)},
        compiler_params=pltpu.CompilerParams(has_side_effects=EFFECT),
    )(*[_hbm(g) for g in grads_bf16], *lands)
    return (res[0], res[1]), list(res[2:2 + n]), list(res[2 + n:2 + 2 * n]), res[-1]


def _scatter_wait(grads_thru, lands_thru, axes, sems, after, name):
    n = len(grads_thru)

    def body(*refs):
        send_sems, recv_sems = refs[2 * n], refs[2 * n + 1]
        outs = refs[2 * n + 3:]
        gb, land = outs[:n], outs[n:]
        _, peers, slots = _chip_peers()
        for t in range(n):
            size = land[t].shape[1 + axes[t]]
            for k in range(3):
                cp = pltpu.make_async_remote_copy(
                    src_ref=_window(gb[t], axes[t], slots[k], size), dst_ref=land[t].at[k],
                    send_sem=send_sems.at[3 * t + k], recv_sem=recv_sems.at[3 * t + k],
                    device_id=peers[k], device_id_type=MESH)
                cp.wait_send()
                cp.wait_recv()

    res = pl.pallas_call(
        body, name=name,
        in_specs=[HBM_SPEC] * (2 * n) + [SEM_SPEC, SEM_SPEC, pl.BlockSpec(memory_space=pl.ANY)],
        out_specs=[HBM_SPEC] * (2 * n),
        out_shape=[pltpu.HBM(g.shape, g.dtype) for g in grads_thru] + [pltpu.HBM(l.shape, l.dtype) for l in lands_thru],
        input_output_aliases={t: t for t in range(2 * n)},
        compiler_params=pltpu.CompilerParams(has_side_effects=EFFECT),
    )(*grads_thru, *lands_thru, sems[0], sems[1], after)
    return list(res[n:])


def _sibling_start(arrs, name):
    n = len(arrs)

    def body(*refs):
        outs = refs[2 * n:]
        send_sems, recv_sems = outs[0], outs[1]
        src, land = outs[2:2 + n], outs[2 + n:2 + 2 * n]
        token = outs[-1]
        sibling = (lax.axis_index("x"), lax.axis_index("y"), 1 - lax.axis_index("c"))
        for t in range(n):
            pltpu.make_async_remote_copy(
                src_ref=src[t], dst_ref=land[t], send_sem=send_sems.at[t], recv_sem=recv_sems.at[t],
                device_id=sibling, device_id_type=MESH).start()
        token[...] = jnp.zeros_like(token)

    lands = [_hbm(lax.empty(a.shape, a.dtype)) for a in arrs]
    res = pl.pallas_call(
        body, name=name,
        in_specs=[HBM_SPEC] * (2 * n),
        out_specs=[SEM_SPEC, SEM_SPEC] + [HBM_SPEC] * (2 * n) + [pl.BlockSpec(memory_space=pltpu.VMEM)],
        out_shape=[pltpu.SemaphoreType.DMA((n,)), pltpu.SemaphoreType.DMA((n,))]
        + [pltpu.HBM(a.shape, a.dtype) for a in arrs] * 2 + [jax.ShapeDtypeStruct((8, 128), F32)],
        input_output_aliases={t: 2 + t for t in range(2 * n)},
        compiler_params=pltpu.CompilerParams(has_side_effects=EFFECT),
    )(*[_hbm(a) for a in arrs], *lands)
    return (res[0], res[1]), list(res[2:2 + n]), list(res[2 + n:2 + 2 * n]), res[-1]


def _sibling_wait(src_thru, lands_thru, sems, after, name):
    n = len(src_thru)

    def body(*refs):
        send_sems, recv_sems = refs[2 * n], refs[2 * n + 1]
        outs = refs[2 * n + 3:]
        src, land = outs[:n], outs[n:]
        sibling = (lax.axis_index("x"), lax.axis_index("y"), 1 - lax.axis_index("c"))
        for t in range(n):
            cp = pltpu.make_async_remote_copy(
                src_ref=src[t], dst_ref=land[t], send_sem=send_sems.at[t], recv_sem=recv_sems.at[t],
                device_id=sibling, device_id_type=MESH)
            cp.wait_send()
            cp.wait_recv()

    res = pl.pallas_call(
        body, name=name,
        in_specs=[HBM_SPEC] * (2 * n) + [SEM_SPEC, SEM_SPEC, pl.BlockSpec(memory_space=pl.ANY)],
        out_specs=[HBM_SPEC] * (2 * n),
        out_shape=[pltpu.HBM(a.shape, a.dtype) for a in src_thru] * 2,
        input_output_aliases={t: t for t in range(2 * n)},
        compiler_params=pltpu.CompilerParams(has_side_effects=EFFECT),
    )(*src_thru, *lands_thru, sems[0], sems[1], after)
    return list(res[:n]), list(res[n:])


def _all_reduce_small(packed, name):
    nc = packed.shape[1]
    vmem = pl.BlockSpec(memory_space=pltpu.VMEM)

    def body(in_ref, out_ref, gbuf, send_sems, recv_sems):
        x, y, c = lax.axis_index("x"), lax.axis_index("y"), lax.axis_index("c")
        me = 4 * x + 2 * y + c
        gbuf[me] = jnp.sum(in_ref[...], axis=0, keepdims=True)
        copies = []
        for k in range(1, 8):
            peer = (x ^ ((k >> 2) & 1), y ^ ((k >> 1) & 1), c ^ (k & 1))
            rc = pltpu.make_async_remote_copy(
                src_ref=gbuf.at[me], dst_ref=gbuf.at[me], send_sem=send_sems.at[k - 1], recv_sem=recv_sems.at[k - 1],
                device_id=peer, device_id_type=MESH)
            rc.start()
            copies.append(rc)
        for k in range(1, 8):
            peer = (x ^ ((k >> 2) & 1), y ^ ((k >> 1) & 1), c ^ (k & 1))
            pltpu.make_async_remote_copy(
                src_ref=gbuf.at[me], dst_ref=gbuf.at[me ^ k], send_sem=send_sems.at[k - 1],
                recv_sem=recv_sems.at[k - 1], device_id=peer, device_id_type=MESH).wait_recv()
        for rc in copies:
            rc.wait_send()
        tot = gbuf[0]
        for d in range(1, 8):
            tot = tot + gbuf[d]
        out_ref[...] = tot

    return pl.pallas_call(
        body, name=name,
        in_specs=[vmem], out_specs=vmem,
        out_shape=jax.ShapeDtypeStruct((1, nc), F32),
        scratch_shapes=[pltpu.VMEM((8, 1, nc), F32), pltpu.SemaphoreType.DMA((7,)), pltpu.SemaphoreType.DMA((7,))],
    )(packed)


def _sum4(g_full, axis, slot_arr, recv, name):
    _, r, c = recv.shape
    tr = min(r, 128)
    nb = r // tr

    def body(slot_ref, own_ref, recv_ref, o_ref):
        acc = own_ref[...]
        for k in range(3):
            acc = acc + recv_ref[k].astype(F32)
        o_ref[...] = acc

    if axis == 0:
        own_map = lambda i, s: (s[0] * nb + i, 0)
    else:
        own_map = lambda i, s: (i, s[0])
    return pl.pallas_call(
        body, name=name,
        grid_spec=pltpu.PrefetchScalarGridSpec(
            num_scalar_prefetch=1, grid=(nb,),
            in_specs=[pl.BlockSpec((tr, c), own_map), pl.BlockSpec((3, tr, c), lambda i, s: (0, i, 0))],
            out_specs=pl.BlockSpec((tr, c), lambda i, s: (i, 0))),
        out_shape=jax.ShapeDtypeStruct((r, c), F32),
        compiler_params=_params("parallel"),
    )(slot_arr, g_full, recv)


def _adamw(w, g_parts, m, v, name):
    r, c = w.shape
    tr = r if r % 128 else _row_tile(r, 256)
    npart = len(g_parts)

    def body(*refs):
        w_ref = refs[0]
        g_refs = refs[1:1 + npart]
        m_ref, v_ref, g_out, d_out, m_out, v_out = refs[1 + npart:]
        g = g_refs[0][...]
        for gr in g_refs[1:]:
            g = g + gr[...]
        mm = ADAM_B1 * m_ref[...] + (1.0 - ADAM_B1) * g
        vv = ADAM_B2 * v_ref[...] + (1.0 - ADAM_B2) * (g * g)
        m_hat = mm / (1.0 - ADAM_B1 ** ADAM_STEP)
        v_hat = vv / (1.0 - ADAM_B2 ** ADAM_STEP)
        g_out[...] = g
        d_out[...] = -ADAM_LR * (m_hat / (jnp.sqrt(v_hat) + ADAM_EPS) + ADAM_WD * w_ref[...])
        m_out[...] = mm
        v_out[...] = vv

    blk = pl.BlockSpec((tr, c), lambda i: (i, 0))
    shp = jax.ShapeDtypeStruct((r, c), F32)
    return pl.pallas_call(
        body, name=name, grid=(r // tr,),
        in_specs=[blk] * (3 + npart), out_specs=[blk] * 4, out_shape=[shp] * 4,
        compiler_params=pltpu.CompilerParams(dimension_semantics=("parallel",), vmem_limit_bytes=MM_VMEM_LIMIT),
    )(w, *g_parts, m, v)


def _pad_rows8(w):
    return jnp.pad(w, ((0, HALO - w.shape[0]), (0, 0)))


def kernel(x, mem, hgrn_lb, norm1_w, w_in, hgrn_norm_w, sconv_w, w_out, norm2_w, mem_norm_w, wq, wk, wv, wo, norm3_w, w_gate, w_up, ffn_conv_w, ffn_conv_b, w_down, final_norm_w, loss_target, m_hgrn_lb, m_norm1_w, m_w_in, m_hgrn_norm_w, m_sconv_w, m_w_out, m_norm2_w, m_mem_norm_w, m_wq, m_wk, m_wv, m_wo, m_norm3_w, m_w_gate, m_w_up, m_ffn_conv_w, m_ffn_conv_b, m_w_down, m_final_norm_w, v_hgrn_lb, v_norm1_w, v_w_in, v_hgrn_norm_w, v_sconv_w, v_w_out, v_norm2_w, v_mem_norm_w, v_wq, v_wk, v_wv, v_wo, v_norm3_w, v_w_gate, v_w_up, v_ffn_conv_w, v_ffn_conv_b, v_w_down, v_final_norm_w):
    xs, mems, tgt = x[0], mem[0], loss_target[0]
    d = xs.shape[1]
    fnw = final_norm_w.reshape(1, d)

    big = {"w_in": (w_in[0], 1), "w_out": (w_out[0], 0), "wq": (wq[0], 0), "wk": (wk[0], 0), "wv": (wv[0], 0),
           "wo": (wo[0], 0), "w_gate": (w_gate[0], 1), "w_up": (w_up[0], 1), "w_down": (w_down[0], 0)}
    names = list(big)
    slot_arr = (2 * lax.axis_index("x") + lax.axis_index("y")).astype(jnp.int32).reshape(1)
    gnames = names + ["sconv8", "fconv8"]
    fulls = [_cast_into_full(big[n][0], big[n][1], slot_arr, BF16, "cast_" + n) for n in names]
    fulls += [_cast_into_full(_pad_rows8(sconv_w[0]), 1, slot_arr, F32, "cast_sconv_w"),
              _cast_into_full(_pad_rows8(ffn_conv_w[0]), 1, slot_arr, F32, "cast_ffn_conv_w")]
    axes = [big[n][1] for n in names] + [1, 1]
    groups = [["w_in"], ["w_out", "sconv8"], ["wq", "wk", "wv", "wo"], ["w_gate", "w_up", "fconv8", "w_down"]]
    gidx = [[gnames.index(n) for n in grp] for grp in groups]
    split = [True] * len(names) + [False, False]
    gsems, fulls, tok = _gather_start(fulls, axes, split, gidx, "gather_start")
    wf, relayed = {}, {}

    def gather_relay(g, after):
        idx = gidx[g]
        dsems, arrs, token = _gather_relay([fulls[t] for t in idx], [axes[t] for t in idx], [split[t] for t in idx],
                                           gsems[g], after, "gather_relay_%d" % g)
        relayed[g] = (dsems, arrs)
        return token[0:1, 0:1]

    def gather_finish(g, after):
        idx = gidx[g]
        dsems, arrs = relayed[g]
        got = _gather_finish(arrs, [axes[t] for t in idx], [split[t] for t in idx], dsems, after,
                             "gather_finish_%d" % g)
        wf.update(zip(groups[g], got))

    lb0, lb1 = hgrn_lb[0:1], hgrn_lb[1:2]

    h1 = _rmsnorm_fwd(xs, norm1_w + tok[0:1, 0:1], "norm1")
    gather_relay(0, h1)
    gather_finish(0, h1)
    proj = _matmul(h1, wf["w_in"], "nn", "proj_in", out_dtype=BF16)
    t1 = gather_relay(1, proj)
    o_h, og, states = _hgrn_fwd(proj, lb0, lb1, hgrn_norm_w + t1, "hgrn_fwd")
    gather_finish(1, o_h)
    t2 = gather_relay(2, o_h)
    sconv8 = wf["sconv8"]
    mix = _sconv_fwd(proj, sconv8, og, "sconv_fwd")
    x1 = _matmul(mix, wf["w_out"], "nn", "proj_out", residual=xs)
    t3 = gather_relay(3, x1)
    h2 = _rmsnorm_fwd(x1, norm2_w + (t2 + t3), "norm2")
    mem_n = _rmsnorm_fwd(mems, mem_norm_w, "norm_mem")
    gather_finish(2, h2)
    qa = _matmul(h2, wf["wq"], "nn", "attn_q", out_dtype=BF16)
    ka = _matmul(mem_n, wf["wk"], "nn", "attn_k", out_dtype=BF16)
    va = _matmul(mem_n, wf["wv"], "nn", "attn_v", out_dtype=BF16)
    att = _attn_fwd(qa, ka, va, "attn_fwd")
    x2 = _matmul(att, wf["wo"], "nn", "attn_o", residual=x1)
    h3 = _rmsnorm_fwd(x2, norm3_w, "norm3")
    gather_finish(3, h3)
    fconv8 = wf["fconv8"]
    gate = _matmul(h3, wf["w_gate"], "nn", "ffn_gate", out_dtype=BF16)
    up = _matmul(h3, wf["w_up"], "nn", "ffn_up", out_dtype=BF16)
    z = _ffn_fwd(gate, up, fconv8, ffn_conv_b, "ffn_act")
    x3 = _matmul(z, wf["w_down"], "nn", "ffn_down", residual=x2)

    dx3, dx3b, g_final, loss8 = _final_loss_bwd(x3, tgt, fnw, "loss_bwd")
    gw = {}
    dz = _matmul(dx3b, wf["w_down"], "nt", "d_z", out_dtype=BF16)
    gw["w_down"] = _matmul(z, dx3b, "tn", "g_w_down", extra_bf16=True)
    dgate, du, g_fb, g_fw = _ffn_bwd(gate, up, dz, fconv8, ffn_conv_b, "ffn_act_bwd")
    dh3 = _matmul(dgate, wf["w_gate"], "nt", "d_h3_gate")
    dh3 = _matmul(du, wf["w_up"], "nt", "d_h3_up", residual=dh3, out_dtype=BF16)
    gw["w_gate"] = _matmul(h3, dgate, "tn", "g_w_gate", extra_bf16=True)
    gw["w_up"] = _matmul(h3, du, "tn", "g_w_up", extra_bf16=True)
    pending = []

    def scatter_start(grp):
        sems, g_thru, lands, token = _scatter_start([gw[n][1] for n in grp], [big[n][1] for n in grp],
                                                    "scatter_start_" + grp[0])
        pending.append((grp, sems, g_thru, lands))
        return token[0:1, 0:1]

    tok1 = scatter_start(["w_down", "w_gate", "w_up"])
    dx2, dx2b, g_n3 = _rmsnorm_bwd(dh3, x2, norm3_w + tok1, dx3, "norm3_bwd")
    datt = _matmul(dx2b, wf["wo"], "nt", "d_att", out_dtype=BF16)
    gw["wo"] = _matmul(att, dx2b, "tn", "g_wo", extra_bf16=True)
    dqa, dka, dva = _attn_bwd(qa, ka, va, datt, "attn_bwd")
    dh2 = _matmul(dqa, wf["wq"], "nt", "d_h2", out_dtype=BF16)
    gw["wq"] = _matmul(h2, dqa, "tn", "g_wq", extra_bf16=True)
    gw["wk"] = _matmul(mem_n, dka, "tn", "g_wk", extra_bf16=True)
    gw["wv"] = _matmul(mem_n, dva, "tn", "g_wv", extra_bf16=True)
    tok2 = scatter_start(["wo", "wq", "wk", "wv"])
    dmem_n = _matmul(dka, wf["wk"], "nt", "d_memn_k")
    dmem_n = _matmul(dva, wf["wv"], "nt", "d_memn_v", residual=dmem_n)
    _, _, g_nm = _rmsnorm_bwd(dmem_n, mems, mem_norm_w, None, "norm_mem_bwd")
    dx1, dx1b, g_n2 = _rmsnorm_bwd(dh2, x1, norm2_w + tok2, dx2, "norm2_bwd")
    dmix = _matmul(dx1b, wf["w_out"], "nt", "d_mix", out_dtype=BF16)
    gw["w_out"] = _matmul(mix, dx1b, "tn", "g_w_out", extra_bf16=True)
    tok3 = scatter_start(["w_out"])
    dcb, dcc, dch, g_sw = _sconv_bwd(proj, sconv8, dmix, "sconv_bwd")
    dq, df, di, dg, g_lb, g_hn = _hgrn_bwd(proj, lb0, lb1, hgrn_norm_w + tok3, o_h, states, dmix, "hgrn_bwd")
    dproj = jnp.concatenate([dq, df, di, dg, dcb, dcc, dch], axis=1)
    gw["w_in"] = _matmul(h1, dproj, "tn", "g_w_in", extra_bf16=True)
    tok4 = scatter_start(["w_in"])
    dh1 = _matmul(dproj, wf["w_in"], "nt", "d_h1", out_dtype=BF16)
    dx, _, g_n1 = _rmsnorm_bwd(dh1, xs, norm1_w + tok4, dx1, "norm1_bwd")

    small = [g_n1, g_n2, g_n3, g_final, g_nm, g_lb, g_hn, g_fb,
             g_sw[0:8], g_sw[8:16], g_sw[16:24], g_fw[0:8], g_fw[8:16], g_fw[16:24], loss8]
    widths = [a.shape[1] for a in small]
    tot = _all_reduce_small(jnp.concatenate(small, axis=1), "all_reduce_small")
    offs = [0]
    for wd_ in widths:
        offs.append(offs[-1] + wd_)
    sm = [tot[:, offs[i]:offs[i + 1]] for i in range(len(small))]
    s_n1, s_n2, s_n3, s_final, s_nm, s_lb, s_hn, s_fb = sm[:8]
    s_sw = jnp.concatenate(sm[8:11], axis=0)
    s_fw = jnp.concatenate(sm[11:14], axis=0)
    loss = sm[14][0, 0]
    slot = 2 * lax.axis_index("x") + lax.axis_index("y")
    s_sw = lax.dynamic_slice_in_dim(s_sw, slot * (HGRN_W // N_CHIPS), HGRN_W // N_CHIPS, axis=1)
    fsh = ffn_conv_w.shape[2]
    s_fw = lax.dynamic_slice_in_dim(s_fw, slot * fsh, fsh, axis=1)
    s_lb2 = jnp.concatenate([s_lb, -s_lb], axis=0)

    swaps = []
    after = tot
    for grp, sems, g_thru, lands in pending:
        got = _scatter_wait(g_thru, lands, [big[n][1] for n in grp], sems, after, "scatter_wait_" + grp[0])
        sums = [_sum4(gw[n][0], big[n][1], slot_arr, r, "core_sum_" + n) for n, r in zip(grp, got)]
        ssems, s_thru, s_lands, after = _sibling_start(sums, "sibling_start_" + grp[0])
        swaps.append((grp, ssems, s_thru, s_lands))

    moments = {"hgrn_lb": (m_hgrn_lb, v_hgrn_lb), "norm1_w": (m_norm1_w, v_norm1_w), "w_in": (m_w_in, v_w_in),
               "hgrn_norm_w": (m_hgrn_norm_w, v_hgrn_norm_w), "sconv_w": (m_sconv_w, v_sconv_w),
               "w_out": (m_w_out, v_w_out), "norm2_w": (m_norm2_w, v_norm2_w),
               "mem_norm_w": (m_mem_norm_w, v_mem_norm_w), "wq": (m_wq, v_wq), "wk": (m_wk, v_wk), "wv": (m_wv, v_wv),
               "wo": (m_wo, v_wo), "norm3_w": (m_norm3_w, v_norm3_w), "w_gate": (m_w_gate, v_w_gate),
               "w_up": (m_w_up, v_w_up), "ffn_conv_w": (m_ffn_conv_w, v_ffn_conv_w),
               "ffn_conv_b": (m_ffn_conv_b, v_ffn_conv_b), "w_down": (m_w_down, v_w_down),
               "final_norm_w": (m_final_norm_w, v_final_norm_w)}
    weights = {"hgrn_lb": hgrn_lb, "norm1_w": norm1_w, "w_in": w_in, "hgrn_norm_w": hgrn_norm_w, "sconv_w": sconv_w,
               "w_out": w_out, "norm2_w": norm2_w, "mem_norm_w": mem_norm_w, "wq": wq, "wk": wk, "wv": wv, "wo": wo,
               "norm3_w": norm3_w, "w_gate": w_gate, "w_up": w_up, "ffn_conv_w": ffn_conv_w, "ffn_conv_b": ffn_conv_b,
               "w_down": w_down, "final_norm_w": final_norm_w}
    small_g = {"hgrn_lb": s_lb2, "norm1_w": s_n1, "hgrn_norm_w": s_hn, "sconv_w": s_sw, "norm2_w": s_n2,
               "mem_norm_w": s_nm, "norm3_w": s_n3, "ffn_conv_w": s_fw, "ffn_conv_b": s_fb, "final_norm_w": s_final}
    order = list(weights)
    res = {}

    def adamw(n, parts):
        shape = weights[n].shape
        w2 = weights[n].reshape((-1, shape[-1]))
        m2, v2 = (t.reshape(w2.shape) for t in moments[n])
        res[n] = [t.reshape(shape) for t in _adamw(w2, [p.reshape(w2.shape) for p in parts], m2, v2, "adamw_" + n)]

    for n in order:
        if n not in big:
            adamw(n, [small_g[n]])
    after = after + res["final_norm_w"][1][0]
    for grp, ssems, s_thru, s_lands in swaps:
        own, other = _sibling_wait(s_thru, s_lands, ssems, after, "sibling_wait_" + grp[0])
        for n, a, b in zip(grp, own, other):
            adamw(n, [a, b])
        after = res[grp[-1]][1]

    return (loss, dx[None], *[res[n][0] for n in order], *[res[n][1] for n in order],
            *[res[n][2] for n in order], *[res[n][3] for n in order])
```

```python
import functools

import jax
import jax.numpy as jnp
from jax import lax
from jax.experimental import pallas as pl
from jax.experimental.pallas import tpu as pltpu

F32 = jnp.float32
BF16 = jnp.bfloat16
MESH = pl.DeviceIdType.MESH

EPS = 1e-6
HGRN_W = 1024
HEAD = 128
N_HEADS = 8
CHUNK = 64
HGRN_UNROLL = 8
HGRN_HEADS_PER_STEP = 2
MEM_HEADS = 4
MEM_HEAD_DIM = 512
N_CHIPS = 4
HALO = 8

ADAM_LR = 0.001
ADAM_B1 = 0.9
ADAM_B2 = 0.999
ADAM_EPS = 1e-08
ADAM_WD = 0.01
ADAM_STEP = 10


def _sigmoid(x):
    return 1.0 / (1.0 + jnp.exp(-x))


def _dot(a, b, dims):
    return lax.dot_general(a.astype(BF16), b.astype(BF16), (dims, ((), ())),
                           preferred_element_type=F32)


def _dot_nn(a, b):
    return _dot(a, b, ((1,), (0,)))


def _dot_nt(a, b):
    return _dot(a, b, ((1,), (1,)))


def _dot_tn(a, b):
    return _dot(a, b, ((0,), (0,)))


def _hdot(a, b, dims):
    return lax.dot_general(a, b, (dims, ((), ())), precision=lax.Precision.HIGH, preferred_element_type=F32)


def _hdot_nn(a, b):
    return _hdot(a, b, ((1,), (0,)))


def _hdot_nt(a, b):
    return _hdot(a, b, ((1,), (1,)))


def _hdot_tn(a, b):
    return _hdot(a, b, ((0,), (0,)))


def _exact_ones_dot(ones_bf16, x):
    hi = x.astype(BF16)
    r1 = x - hi.astype(F32)
    mid = r1.astype(BF16)
    lo = (r1 - mid.astype(F32)).astype(BF16)
    dims = (((1,), (0,)), ((), ()))
    return (lax.dot_general(ones_bf16, hi, dims, preferred_element_type=F32)
            + lax.dot_general(ones_bf16, mid, dims, preferred_element_type=F32)
            + lax.dot_general(ones_bf16, lo, dims, preferred_element_type=F32))


def _rows8(v):
    t, c = v.shape
    return v.reshape(t // 8, 8, c).sum(axis=0)


def _shift_down(x, halo, s):
    rolled = pltpu.roll(x, s, 0)
    hrolled = pltpu.roll(halo, s, 0)
    row = lax.broadcasted_iota(jnp.int32, hrolled.shape, 0)
    head = jnp.where(row < s, hrolled, rolled[:HALO])
    return jnp.concatenate([head, rolled[HALO:]], axis=0)


def _shift_up(x, halo, s):
    t = x.shape[0]
    rolled = pltpu.roll(x, t - s, 0)
    hrolled = pltpu.roll(halo, HALO - s, 0)
    row = lax.broadcasted_iota(jnp.int32, hrolled.shape, 0)
    tail = jnp.where(row >= HALO - s, hrolled, rolled[t - HALO:])
    return jnp.concatenate([rolled[:t - HALO], tail], axis=0)


def _params(*sem):
    return pltpu.CompilerParams(dimension_semantics=sem)


def _row_tile(r, pref):
    while r % pref:
        pref //= 2
    return pref


def _rmsnorm_fwd(x, w, name, tm=256):
    s, d = x.shape
    tm = min(tm, s)

    def body(x_ref, w_ref, o_ref):
        xv = x_ref[...]
        r = lax.rsqrt(jnp.mean(xv * xv, axis=-1, keepdims=True) + EPS)
        o_ref[...] = ((xv * r) * w_ref[...]).astype(BF16)

    return pl.pallas_call(
        body, name=name, grid=(s // tm,),
        in_specs=[pl.BlockSpec((tm, d), lambda i: (i, 0)), pl.BlockSpec((1, d), lambda i: (0, 0))],
        out_specs=pl.BlockSpec((tm, d), lambda i: (i, 0)),
        out_shape=jax.ShapeDtypeStruct((s, d), BF16),
        compiler_params=_params("parallel"),
    )(x, w)


def _rmsnorm_bwd(dh, x, w, dres, name, tm=256):
    s, d = x.shape
    tm = min(tm, s)
    has_res = dres is not None

    def body(*refs):
        if has_res:
            dh_ref, x_ref, w_ref, dres_ref, dx_ref, dxb_ref, gw_ref = refs
        else:
            dh_ref, x_ref, w_ref, dx_ref, dxb_ref, gw_ref = refs

        @pl.when(pl.program_id(0) == 0)
        def _():
            gw_ref[...] = jnp.zeros_like(gw_ref)

        xv = x_ref[...]
        dhv = dh_ref[...].astype(F32)
        r = lax.rsqrt(jnp.mean(xv * xv, axis=-1, keepdims=True) + EPS)
        xhat = xv * r
        gw_ref[...] += _rows8(dhv * xhat)
        dxh = dhv * w_ref[...]
        dx = r * (dxh - xhat * jnp.mean(dxh * xhat, axis=-1, keepdims=True))
        if has_res:
            dx = dres_ref[...] + dx
        dx_ref[...] = dx
        dxb_ref[...] = dx.astype(BF16)

    row = pl.BlockSpec((tm, d), lambda i: (i, 0))
    in_specs = [row, row, pl.BlockSpec((1, d), lambda i: (0, 0))] + ([row] if has_res else [])
    args = (dh, x, w) + ((dres,) if has_res else ())
    return pl.pallas_call(
        body, name=name, grid=(s // tm,),
        in_specs=in_specs,
        out_specs=[row, row, pl.BlockSpec((8, d), lambda i: (0, 0))],
        out_shape=[jax.ShapeDtypeStruct((s, d), F32), jax.ShapeDtypeStruct((s, d), BF16),
                   jax.ShapeDtypeStruct((8, d), F32)],
        compiler_params=_params("arbitrary"),
    )(*args)


def _final_loss_bwd(x3, target, w, name, tm=256):
    s, d = x3.shape
    tm = min(tm, s)

    def body(x_ref, t_ref, w_ref, dx_ref, dxb_ref, gw_ref, loss_ref):
        @pl.when(pl.program_id(0) == 0)
        def _():
            gw_ref[...] = jnp.zeros_like(gw_ref)
            loss_ref[...] = jnp.zeros_like(loss_ref)

        xv = x_ref[...]
        r = lax.rsqrt(jnp.mean(xv * xv, axis=-1, keepdims=True) + EPS)
        xhat = xv * r
        y = xhat * w_ref[...]
        err = y - t_ref[...]
        part = 0.5 * jnp.mean(err * err, axis=-1, keepdims=True)
        tot = jnp.sum(part, axis=0, keepdims=True)
        rr = lax.broadcasted_iota(jnp.int32, loss_ref.shape, 0)
        cc = lax.broadcasted_iota(jnp.int32, loss_ref.shape, 1)
        loss_ref[...] += jnp.where((rr == 0) & (cc == 0), tot, 0.0)
        dy = err * (1.0 / d)
        gw_ref[...] += _rows8(dy * xhat)
        dxh = dy * w_ref[...]
        dx = r * (dxh - xhat * jnp.mean(dxh * xhat, axis=-1, keepdims=True))
        dx_ref[...] = dx
        dxb_ref[...] = dx.astype(BF16)

    row = pl.BlockSpec((tm, d), lambda i: (i, 0))
    return pl.pallas_call(
        body, name=name, grid=(s // tm,),
        in_specs=[row, row, pl.BlockSpec((1, d), lambda i: (0, 0))],
        out_specs=[row, row, pl.BlockSpec((8, d), lambda i: (0, 0)), pl.BlockSpec((8, 128), lambda i: (0, 0))],
        out_shape=[jax.ShapeDtypeStruct((s, d), F32), jax.ShapeDtypeStruct((s, d), BF16),
                   jax.ShapeDtypeStruct((8, d), F32), jax.ShapeDtypeStruct((8, 128), F32)],
        compiler_params=_params("arbitrary"),
    )(x3, target, w)


MM_TILES = (1024, 1408, 512, 256, 128)
MM_K_TILES = (2816, 2048, 1792, 1408, 1024, 512, 256, 128)
MM_VMEM_LIMIT = 56 * 1024 * 1024
MM_VMEM_BUDGET = 46 * 1024 * 1024


def _pick_tile(dim):
    for t in MM_TILES:
        if dim % t == 0:
            return t
    return dim


def _matmul(a, b, mode, name, *, out_dtype=F32, residual=None, extra_bf16=False, tm=None, tn=None, tk=None):
    if mode == "nn":
        (m, k), (k2, n) = a.shape, b.shape
    elif mode == "nt":
        (m, k), (n, k2) = a.shape, b.shape
    else:
        (k, m), (k2, n) = a.shape, b.shape
    assert k == k2, (a.shape, b.shape, mode)
    tm = _pick_tile(m) if tm is None else min(tm, m)
    tn = _pick_tile(n) if tn is None else min(tn, n)
    out_bytes = tm * tn * (jnp.dtype(out_dtype).itemsize + (2 if extra_bf16 else 0) + (4 if residual is not None else 0))

    def vmem_bytes(t):
        return (2 * (tm * t * a.dtype.itemsize + t * tn * b.dtype.itemsize) + 2 * out_bytes
                + (tm * tn * 4 if t < k else 0))

    if tk is None:
        tk = next(t for t in MM_K_TILES if k % t == 0 and t <= k and vmem_bytes(t) <= MM_VMEM_BUDGET)
    assert m % tm == 0 and n % tn == 0 and k % tk == 0, (m, n, k, tm, tn, tk)
    nk = k // tk
    dims = {"nn": ((1,), (0,)), "nt": ((1,), (1,)), "tn": ((0,), (0,))}[mode]
    has_res = residual is not None

    def body(*refs):
        refs = list(refs)
        a_ref, b_ref = refs[0], refs[1]
        r_ref = refs[2] if has_res else None
        outs = refs[2 + has_res:]
        o_ref = outs[0]
        o2_ref = outs[1] if extra_bf16 else None
        def finish(r):
            if has_res:
                r = r_ref[...] + r
            o_ref[...] = r.astype(out_dtype)
            if extra_bf16:
                o2_ref[...] = r.astype(BF16)

        if nk == 1:
            finish(_dot(a_ref[...], b_ref[...], dims))
            return
        acc = outs[-1]
        kk = pl.program_id(2)

        @pl.when(kk == 0)
        def _():
            acc[...] = _dot(a_ref[...], b_ref[...], dims)

        if nk > 2:
            @pl.when((kk > 0) & (kk < nk - 1))
            def _():
                acc[...] += _dot(a_ref[...], b_ref[...], dims)

        @pl.when(kk == nk - 1)
        def _():
            finish(acc[...] + _dot(a_ref[...], b_ref[...], dims))

    if mode == "tn":
        a_spec = pl.BlockSpec((tk, tm), lambda i, j, kk: (kk, i))
    else:
        a_spec = pl.BlockSpec((tm, tk), lambda i, j, kk: (i, kk))
    if mode == "nt":
        b_spec = pl.BlockSpec((tn, tk), lambda i, j, kk: (j, kk))
    else:
        b_spec = pl.BlockSpec((tk, tn), lambda i, j, kk: (kk, j))
    o_spec = pl.BlockSpec((tm, tn), lambda i, j, kk: (i, j))
    in_specs = [a_spec, b_spec] + ([o_spec] if has_res else [])
    out_specs = [o_spec] + ([o_spec] if extra_bf16 else [])
    out_shape = [jax.ShapeDtypeStruct((m, n), out_dtype)] + ([jax.ShapeDtypeStruct((m, n), BF16)] if extra_bf16 else [])
    args = (a, b) + ((residual,) if has_res else ())
    res = pl.pallas_call(
        body, name=name, grid=(m // tm, n // tn, nk),
        in_specs=in_specs, out_specs=out_specs, out_shape=out_shape,
        scratch_shapes=[pltpu.VMEM((tm, tn) if nk > 1 else (8, 128), F32)],
        compiler_params=pltpu.CompilerParams(dimension_semantics=("parallel", "parallel", "arbitrary"),
                                             vmem_limit_bytes=MM_VMEM_LIMIT),
    )(*args)
    return res if extra_bf16 else res[0]


def _hgrn_gates(qp, fp, lb):
    sig = _sigmoid(fp)
    f = lb + (1.0 - lb) * sig
    logf = jnp.log(f)
    k = 1.0 - f
    sq = _sigmoid(qp)
    q = qp * sq
    return sig, f, logf, k, sq, q


def _hgrn_fwd(proj, lb0, lb1, norm_w, name, tb=512):
    s = proj.shape[0]
    tb = min(tb, s)
    nb, ncb = s // tb, tb // CHUNK

    def body(q_ref, f_ref, i_ref, g_ref, a0_ref, a1_ref, nw_ref, o_ref, og_ref, st_ref, state):
        @pl.when(pl.program_id(1) == 0)
        def _():
            state[...] = jnp.zeros_like(state)

        lb2 = _sigmoid(a0_ref[...] - a1_ref[...])
        row = lax.broadcasted_iota(jnp.int32, (CHUNK, CHUNK), 0)
        col = lax.broadcasted_iota(jnp.int32, (CHUNK, CHUNK), 1)
        tril = row >= col
        ones_l = tril.astype(BF16)
        nw = nw_ref[...]

        def chunk(c, carry):
            rows = pl.ds(pl.multiple_of(c * CHUNK, CHUNK), CHUNK)
            for hh in range(HGRN_HEADS_PER_STEP):
                cols = slice(hh * HEAD, (hh + 1) * HEAD)
                v = i_ref[rows, cols].astype(F32)
                _, _, logf, k, _, q = _hgrn_gates(q_ref[rows, cols].astype(F32), f_ref[rows, cols].astype(F32),
                                                  lb2[:, cols])
                b = _exact_ones_dot(ones_l, logf)
                bl = jnp.sum(logf, axis=0, keepdims=True)
                bm = 0.5 * bl
                st = state[hh]
                st_ref[hh, c] = st
                qt = q * jnp.exp(b - bm)
                kt = k * jnp.exp(bm - b)
                a = jnp.where(tril, _dot_nt(qt, kt), 0.0)
                o = _dot_nt(q * jnp.exp(b), st) + _dot_nn(a, v)
                state[hh] = st * jnp.exp(bl) + _dot_tn(v, k * jnp.exp(bl - b))
                o_ref[rows, cols] = o
                on = (o * lax.rsqrt(jnp.mean(o * o, axis=-1, keepdims=True) + EPS)) * nw
                gv = g_ref[rows, cols].astype(F32)
                og_ref[rows, cols] = (on * (gv * _sigmoid(gv))).astype(BF16)
            return carry

        lax.fori_loop(0, ncb, chunk, 0, unroll=HGRN_UNROLL)

    hp, wd = HGRN_HEADS_PER_STEP, HGRN_HEADS_PER_STEP * HEAD
    ngrp = N_HEADS // hp

    def colblk(group):
        return pl.BlockSpec((tb, wd), lambda h, j: (j, group * ngrp + h))

    vec = pl.BlockSpec((1, wd), lambda h, j: (0, h))
    out_blk = pl.BlockSpec((tb, wd), lambda h, j: (j, h))
    return pl.pallas_call(
        body, name=name, grid=(ngrp, nb),
        in_specs=[colblk(0), colblk(1), colblk(2), colblk(3), vec, vec, pl.BlockSpec((1, HEAD), lambda h, j: (0, 0))],
        out_specs=[out_blk, out_blk, pl.BlockSpec((hp, ncb, HEAD, HEAD), lambda h, j: (h, j, 0, 0))],
        out_shape=[jax.ShapeDtypeStruct((s, HGRN_W), F32), jax.ShapeDtypeStruct((s, 2 * HGRN_W), BF16),
                   jax.ShapeDtypeStruct((N_HEADS, s // CHUNK, HEAD, HEAD), F32)],
        scratch_shapes=[pltpu.VMEM((hp, HEAD, HEAD), F32)],
        compiler_params=_params("parallel", "arbitrary"),
    )(proj, proj, proj, proj, lb0, lb1, norm_w)


def _hgrn_bwd(proj, lb0, lb1, norm_w, o, states, dmix, name, tb=512):
    s = proj.shape[0]
    tb = min(tb, s)
    nb, ncb = s // tb, tb // CHUNK

    def body(q_ref, f_ref, i_ref, g_ref, a0_ref, a1_ref, nw_ref, o_ref, st_ref, dm_ref,
             dq_ref, df_ref, di_ref, dg_ref, glb_ref, gnw_ref, dstate):
        h = pl.program_id(0)

        @pl.when(pl.program_id(1) == 0)
        def _():
            dstate[...] = jnp.zeros_like(dstate)
            glb_ref[...] = jnp.zeros_like(glb_ref)

        @pl.when((pl.program_id(1) == 0) & (h == 0))
        def _():
            gnw_ref[...] = jnp.zeros_like(gnw_ref)

        lb2 = _sigmoid(a0_ref[...] - a1_ref[...])
        row = lax.broadcasted_iota(jnp.int32, (CHUNK, CHUNK), 0)
        col = lax.broadcasted_iota(jnp.int32, (CHUNK, CHUNK), 1)
        tril = row >= col
        ones_l = tril.astype(BF16)
        ones_u = (row <= col).astype(BF16)
        nw = nw_ref[...]

        def chunk(cc, carry):
            c = ncb - 1 - cc
            rows = pl.ds(pl.multiple_of(c * CHUNK, CHUNK), CHUNK)
            for hh in range(HGRN_HEADS_PER_STEP):
                cols = slice(hh * HEAD, (hh + 1) * HEAD)
                lb = lb2[:, cols]
                qp = q_ref[rows, cols].astype(F32)
                v = i_ref[rows, cols].astype(F32)
                sig, f, logf, k, sq, q = _hgrn_gates(qp, f_ref[rows, cols].astype(F32), lb)
                gv = g_ref[rows, cols].astype(F32)
                sg = _sigmoid(gv)
                silu_g = gv * sg
                dog = dm_ref[rows, cols].astype(F32)
                ov = o_ref[rows, cols]
                r = lax.rsqrt(jnp.mean(ov * ov, axis=-1, keepdims=True) + EPS)
                ohat = ov * r
                on = ohat * nw
                dg_ref[rows, cols] = (dog * on * (sg * (1.0 + gv * (1.0 - sg)))).astype(BF16)
                don = dog * silu_g
                gnw_ref[...] += _rows8(don * ohat)
                doh = don * nw
                do = r * (doh - ohat * jnp.mean(doh * ohat, axis=-1, keepdims=True))
                b = _exact_ones_dot(ones_l, logf)
                bl = jnp.sum(logf, axis=0, keepdims=True)
                bm = 0.5 * bl
                e_q = jnp.exp(b - bm)
                e_k = jnp.exp(bm - b)
                e_b = jnp.exp(b)
                e_l = jnp.exp(bl - b)
                qt, kt, qb, kb = q * e_q, k * e_k, q * e_b, k * e_l
                st0 = st_ref[hh, c]
                dst = dstate[hh]
                a = jnp.where(tril, _dot_nt(qt, kt), 0.0)
                da = jnp.where(tril, _dot_nt(do, v), 0.0)
                dq = _hdot_nn(da, kt) * e_q + _hdot_nn(do, st0) * e_b
                dkb = _hdot_nn(v, dst) * e_l
                dk = _hdot_tn(da, qt) * e_k + dkb
                dv = _dot_tn(a, do) + _dot_nt(kb, dst)
                e_bl = jnp.exp(bl)
                dstate[hh] = dst * e_bl + _dot_tn(do, qb)
                db = q * dq - k * dk
                db_last = jnp.sum(k * dkb, axis=0, keepdims=True) + e_bl * jnp.sum(st0 * dst, axis=0, keepdims=True)
                dlogf = _exact_ones_dot(ones_u, db) + db_last
                dfg = dlogf / f - dk
                df_ref[rows, cols] = (dfg * (1.0 - lb) * (sig * (1.0 - sig))).astype(BF16)
                glb_ref[:, cols] += _rows8(dfg * (1.0 - sig)) * (lb * (1.0 - lb))
                dq_ref[rows, cols] = (dq * (sq * (1.0 + qp * (1.0 - sq)))).astype(BF16)
                di_ref[rows, cols] = dv.astype(BF16)
            return carry

        lax.fori_loop(0, ncb, chunk, 0, unroll=HGRN_UNROLL)

    hp, wd = HGRN_HEADS_PER_STEP, HGRN_HEADS_PER_STEP * HEAD
    ngrp = N_HEADS // hp

    def colblk(group):
        return pl.BlockSpec((tb, wd), lambda h, j: (nb - 1 - j, group * ngrp + h))

    vec = pl.BlockSpec((1, wd), lambda h, j: (0, h))
    blk = pl.BlockSpec((tb, wd), lambda h, j: (nb - 1 - j, h))
    grad = jax.ShapeDtypeStruct((s, HGRN_W), BF16)
    return pl.pallas_call(
        body, name=name, grid=(ngrp, nb),
        in_specs=[colblk(0), colblk(1), colblk(2), colblk(3), vec, vec, pl.BlockSpec((1, HEAD), lambda h, j: (0, 0)),
                  blk, pl.BlockSpec((hp, ncb, HEAD, HEAD), lambda h, j: (h, nb - 1 - j, 0, 0)), blk],
        out_specs=[blk, blk, blk, blk, pl.BlockSpec((8, wd), lambda h, j: (0, h)),
                   pl.BlockSpec((8, HEAD), lambda h, j: (0, 0))],
        out_shape=[grad, grad, grad, grad, jax.ShapeDtypeStruct((8, HGRN_W), F32), jax.ShapeDtypeStruct((8, HEAD), F32)],
        scratch_shapes=[pltpu.VMEM((hp, HEAD, HEAD), F32)],
        compiler_params=_params("arbitrary", "arbitrary"),
    )(proj, proj, proj, proj, lb0, lb1, norm_w, o, states, dmix)


HALO_BLK = 16


def _f32(ref):
    return ref[...].astype(F32)


def _halo_prev(ref):
    return ref[...].astype(F32)[HALO_BLK - HALO:]


def _halo_next(ref):
    return ref[...].astype(F32)[:HALO]


def _conv3(x0, x1, x2, w_ref):
    y = x0 * w_ref[0:1, :]
    y = y + x1 * w_ref[1:2, :]
    return y + x2 * w_ref[2:3, :]


def _sconv_fwd(proj, w8, mix, name, tb=256):
    s = proj.shape[0]
    tb = min(tb, s)
    hb = tb // HALO_BLK

    def body(cb_ref, cc_ref, ch_ref, cch_ref, chh_ref, w_ref, mix_ref, y_ref):
        first = pl.program_id(0) == 0
        u = _f32(cc_ref) * _f32(ch_ref)
        uh = jnp.where(first, 0.0, _halo_prev(cch_ref) * _halo_prev(chh_ref))
        conv = _conv3(_shift_down(u, uh, 2), _shift_down(u, uh, 1), u, w_ref)
        y_ref[...] = (_f32(cb_ref) * conv).astype(BF16)

    def blk(g):
        return pl.BlockSpec((tb, HGRN_W), lambda j: (j, g))

    def halo(g):
        return pl.BlockSpec((HALO_BLK, HGRN_W), lambda j: (jnp.maximum(j * hb - 1, 0), g))

    return pl.pallas_call(
        body, name=name, grid=(s // tb,),
        in_specs=[blk(4), blk(5), blk(6), halo(5), halo(6), pl.BlockSpec((HALO, HGRN_W), lambda j: (0, 0)),
                  pl.BlockSpec(memory_space=pl.ANY)],
        out_specs=pl.BlockSpec((tb, HGRN_W), lambda j: (j, 1)),
        out_shape=jax.ShapeDtypeStruct(mix.shape, BF16),
        input_output_aliases={6: 0},
        compiler_params=_params("parallel"),
    )(proj, proj, proj, proj, proj, w8, mix)


def _sconv_bwd(proj, w8, dmix, name, tb=256):
    s = proj.shape[0]
    tb = min(tb, s)
    hb = tb // HALO_BLK
    nb = s // tb
    last_h = s // HALO_BLK - 1

    def body(cb_ref, cc_ref, ch_ref, cch_ref, chh_ref, cbn_ref, dy_ref, dyn_ref, w_ref,
             dcb_ref, dcc_ref, dch_ref, gw_ref):
        j = pl.program_id(0)

        @pl.when(j == 0)
        def _():
            gw_ref[...] = jnp.zeros_like(gw_ref)

        cc, ch, cb = _f32(cc_ref), _f32(ch_ref), _f32(cb_ref)
        u = cc * ch
        uh = jnp.where(j == 0, 0.0, _halo_prev(cch_ref) * _halo_prev(chh_ref))
        u2, u1 = _shift_down(u, uh, 2), _shift_down(u, uh, 1)
        conv = _conv3(u2, u1, u, w_ref)
        dy = _f32(dy_ref)
        dcb_ref[...] = (dy * conv).astype(BF16)
        dc = dy * cb
        dcn = jnp.where(j == nb - 1, 0.0, _halo_next(dyn_ref) * _halo_next(cbn_ref))
        gw_ref[0:8, :] += _rows8(dc * u2)
        gw_ref[8:16, :] += _rows8(dc * u1)
        gw_ref[16:24, :] += _rows8(dc * u)
        du = dc * w_ref[2:3, :] + _shift_up(dc, dcn, 1) * w_ref[1:2, :] + _shift_up(dc, dcn, 2) * w_ref[0:1, :]
        dcc_ref[...] = (du * ch).astype(BF16)
        dch_ref[...] = (du * cc).astype(BF16)

    def blk(g):
        return pl.BlockSpec((tb, HGRN_W), lambda j: (j, g))

    def halo_prev(g):
        return pl.BlockSpec((HALO_BLK, HGRN_W), lambda j: (jnp.maximum(j * hb - 1, 0), g))

    def halo_next(g):
        return pl.BlockSpec((HALO_BLK, HGRN_W), lambda j: (jnp.minimum((j + 1) * hb, last_h), g))

    out = pl.BlockSpec((tb, HGRN_W), lambda j: (j, 0))
    grad = jax.ShapeDtypeStruct((s, HGRN_W), BF16)
    return pl.pallas_call(
        body, name=name, grid=(nb,),
        in_specs=[blk(4), blk(5), blk(6), halo_prev(5), halo_prev(6), halo_next(4), blk(1), halo_next(1),
                  pl.BlockSpec((HALO, HGRN_W), lambda j: (0, 0))],
        out_specs=[out, out, out, pl.BlockSpec((24, HGRN_W), lambda j: (0, 0))],
        out_shape=[grad, grad, grad, jax.ShapeDtypeStruct((24, HGRN_W), F32)],
        compiler_params=_params("arbitrary"),
    )(proj, proj, proj, proj, proj, proj, dmix, dmix, w8)


def _attn_fwd(q, kk, vv, name, tb=256):
    s, d = q.shape
    m = kk.shape[0]
    tb = min(tb, s)
    scale = MEM_HEAD_DIM ** -0.5

    def body(q_ref, k_ref, v_ref, o_ref):
        for hh in range(MEM_HEADS):
            cols = slice(hh * MEM_HEAD_DIM, (hh + 1) * MEM_HEAD_DIM)
            sc = _dot_nt(q_ref[:, cols], k_ref[:, cols]) * scale
            sc = sc - jnp.max(sc, axis=-1, keepdims=True)
            e = jnp.exp(sc)
            p = e / jnp.sum(e, axis=-1, keepdims=True)
            o_ref[:, cols] = _dot_nn(p, v_ref[:, cols]).astype(BF16)

    full = pl.BlockSpec((m, d), lambda i: (0, 0))
    return pl.pallas_call(
        body, name=name, grid=(s // tb,),
        in_specs=[pl.BlockSpec((tb, d), lambda i: (i, 0)), full, full],
        out_specs=pl.BlockSpec((tb, d), lambda i: (i, 0)),
        out_shape=jax.ShapeDtypeStruct((s, d), BF16),
        compiler_params=_params("parallel"),
    )(q, kk, vv)


def _attn_bwd(q, kk, vv, datt, name, tb=256):
    s, d = q.shape
    m = kk.shape[0]
    tb = min(tb, s)
    scale = MEM_HEAD_DIM ** -0.5

    def body(q_ref, k_ref, v_ref, do_ref, dq_ref, dk_ref, dv_ref):
        @pl.when(pl.program_id(0) == 0)
        def _():
            dk_ref[...] = jnp.zeros_like(dk_ref)
            dv_ref[...] = jnp.zeros_like(dv_ref)

        for hh in range(MEM_HEADS):
            cols = slice(hh * MEM_HEAD_DIM, (hh + 1) * MEM_HEAD_DIM)
            qh, kh, vh, doh = q_ref[:, cols], k_ref[:, cols], v_ref[:, cols], do_ref[:, cols]
            sc = _dot_nt(qh, kh) * scale
            sc = sc - jnp.max(sc, axis=-1, keepdims=True)
            e = jnp.exp(sc)
            p = e / jnp.sum(e, axis=-1, keepdims=True)
            dp = _dot_nt(doh, vh)
            ds = p * (dp - jnp.sum(dp * p, axis=-1, keepdims=True)) * scale
            dq_ref[:, cols] = _dot_nn(ds, kh).astype(BF16)
            dk_ref[:, cols] += _dot_tn(ds, qh)
            dv_ref[:, cols] += _dot_tn(p, doh)

    full = pl.BlockSpec((m, d), lambda i: (0, 0))
    row = pl.BlockSpec((tb, d), lambda i: (i, 0))
    return pl.pallas_call(
        body, name=name, grid=(s // tb,),
        in_specs=[row, full, full, row],
        out_specs=[row, full, full],
        out_shape=[jax.ShapeDtypeStruct((s, d), BF16), jax.ShapeDtypeStruct((m, d), F32),
                   jax.ShapeDtypeStruct((m, d), F32)],
        compiler_params=_params("arbitrary"),
    )(q, kk, vv, datt)


def _ffn_fwd(g, u, w8, bias, name, tb=512, tc=1408):
    s, f = g.shape
    tb = min(tb, s)
    tc = tc if f % tc == 0 else 512
    hb = tb // HALO_BLK

    def body(g_ref, gh_ref, u_ref, w_ref, b_ref, z_ref, a_ref):
        gv = _f32(g_ref)
        gh = jnp.where(pl.program_id(1) == 0, 0.0, _halo_prev(gh_ref))
        a = _conv3(_shift_down(gv, gh, 2), _shift_down(gv, gh, 1), gv, w_ref) + b_ref[...]
        a_ref[...] = a.astype(BF16)
        z_ref[...] = ((a * _sigmoid(a)) * _f32(u_ref)).astype(BF16)

    blk = pl.BlockSpec((tb, tc), lambda c, j: (j, c))
    return pl.pallas_call(
        body, name=name, grid=(f // tc, s // tb),
        in_specs=[blk, pl.BlockSpec((HALO_BLK, tc), lambda c, j: (jnp.maximum(j * hb - 1, 0), c)), blk,
                  pl.BlockSpec((HALO, tc), lambda c, j: (0, c)), pl.BlockSpec((1, tc), lambda c, j: (0, c))],
        out_specs=[blk, blk],
        out_shape=[jax.ShapeDtypeStruct((s, f), BF16), jax.ShapeDtypeStruct((s, f), BF16)],
        compiler_params=pltpu.CompilerParams(dimension_semantics=("parallel", "parallel"),
                                             vmem_limit_bytes=MM_VMEM_LIMIT),
    )(g, g, u, w8, bias)


def _ffn_bwd(a, g, u, dz, w8, name, tb=256, tc=1408):
    s, f = g.shape
    tb = min(tb, s)
    tc = tc if f % tc == 0 else 512
    nb = s // tb

    def body(a_ref, g_ref, u_ref, dz_ref, w_ref, dg_ref, du_ref, gb_ref, gw_ref, da_next):
        jj = pl.program_id(1)

        @pl.when(jj == 0)
        def _():
            gb_ref[...] = jnp.zeros_like(gb_ref)
            gw_ref[...] = jnp.zeros_like(gw_ref)
            da_next[...] = jnp.zeros_like(da_next)

        a = _f32(a_ref)
        sa = _sigmoid(a)
        dz = _f32(dz_ref)
        du_ref[...] = (dz * (a * sa)).astype(BF16)
        da = dz * _f32(u_ref) * (sa * (1.0 + a * (1.0 - sa)))
        gb_ref[...] += _rows8(da)
        dan = da_next[...]
        da1, da2 = _shift_up(da, dan, 1), _shift_up(da, dan, 2)
        gv = _f32(g_ref)
        gw_ref[0:8, :] += _rows8(da2 * gv)
        gw_ref[8:16, :] += _rows8(da1 * gv)
        gw_ref[16:24, :] += _rows8(da * gv)
        dg_ref[...] = (da * w_ref[2:3, :] + da1 * w_ref[1:2, :] + da2 * w_ref[0:1, :]).astype(BF16)
        da_next[...] = da[:HALO]

    blk = pl.BlockSpec((tb, tc), lambda c, jj: (nb - 1 - jj, c))
    return pl.pallas_call(
        body, name=name, grid=(f // tc, nb),
        in_specs=[blk, blk, blk, blk, pl.BlockSpec((HALO, tc), lambda c, jj: (0, c))],
        out_specs=[blk, blk, pl.BlockSpec((8, tc), lambda c, jj: (0, c)), pl.BlockSpec((24, tc), lambda c, jj: (0, c))],
        out_shape=[jax.ShapeDtypeStruct((s, f), BF16), jax.ShapeDtypeStruct((s, f), BF16),
                   jax.ShapeDtypeStruct((8, f), F32), jax.ShapeDtypeStruct((24, f), F32)],
        scratch_shapes=[pltpu.VMEM((HALO, tc), F32)],
        compiler_params=pltpu.CompilerParams(dimension_semantics=("parallel", "arbitrary"),
                                             vmem_limit_bytes=MM_VMEM_LIMIT),
    )(a, g, u, dz, w8)


def _window(ref, axis, slot, size):
    start = pl.multiple_of(slot * size, size)
    if axis == 0:
        return ref.at[pl.ds(start, size), :]
    return ref.at[:, pl.ds(start, size)]


def _chip_peers():
    x, y, c = lax.axis_index("x"), lax.axis_index("y"), lax.axis_index("c")
    peers = [(1 - x, y, c), (x, 1 - y, c), (1 - x, 1 - y, c)]
    slots = [2 * (1 - x) + y, 2 * x + (1 - y), 2 * (1 - x) + (1 - y)]
    return 2 * x + y, peers, slots


HBM_SPEC = pl.BlockSpec(memory_space=pltpu.HBM)
SEM_SPEC = pl.BlockSpec(memory_space=pltpu.SEMAPHORE)
EFFECT = pltpu.SideEffectType.DATAFLOW_SIDE_EFFECTING


def _hbm(a):
    return pltpu.with_memory_space_constraint(a, pltpu.HBM)


def _cast_into_full(x, axis, slot_arr, dtype, name, after=None):
    r, c = x.shape
    tr = _row_tile(r, 256)
    nb = r // tr
    full = (r * N_CHIPS, c) if axis == 0 else (r, c * N_CHIPS)

    def body(slot_ref, x_ref, *rest):
        rest[-1][...] = x_ref[...].astype(dtype)

    if axis == 0:
        out_map = lambda i, s: (s[0] * nb + i, 0)
    else:
        out_map = lambda i, s: (i, s[0])
    extra = [] if after is None else [after]
    return pl.pallas_call(
        body, name=name,
        grid_spec=pltpu.PrefetchScalarGridSpec(
            num_scalar_prefetch=1, grid=(nb,),
            in_specs=[pl.BlockSpec((tr, c), lambda i, s: (i, 0))] + [pl.BlockSpec(memory_space=pl.ANY)] * len(extra),
            out_specs=pl.BlockSpec((tr, c), out_map)),
        out_shape=jax.ShapeDtypeStruct(full, dtype),
        compiler_params=_params("parallel"),
    )(slot_arr, x, *extra)


def _piece(ref, axis, slot, half):
    size = ref.shape[axis] // N_CHIPS
    if half is None:
        return _window(ref, axis, slot, size)
    if axis == 0:
        h = size // 2
        return ref.at[pl.ds(pl.multiple_of(slot * size + half * h, h), h), :]
    h = ref.shape[0] // 2
    return ref.at[pl.ds(pl.multiple_of(half * h, h), h), pl.ds(pl.multiple_of(slot * size, size), size)]


def _gather_start(fulls, axes, split, groups, name):
    n, ng = len(fulls), len(groups)

    def body(*refs):
        outs = refs[n:]
        sems = outs[:2 * ng]
        thru = outs[2 * ng:2 * ng + n]
        token = outs[-1]
        slot, peers, _ = _chip_peers()
        c = lax.axis_index("c")
        for g, members in enumerate(groups):
            for i, t in enumerate(members):
                mine = _piece(thru[t], axes[t], slot, c if split[t] else None)
                for k in range(3):
                    pltpu.make_async_remote_copy(
                        src_ref=mine, dst_ref=mine, send_sem=sems[2 * g].at[3 * i + k],
                        recv_sem=sems[2 * g + 1].at[3 * i + k], device_id=peers[k], device_id_type=MESH).start()
        token[...] = jnp.zeros_like(token)

    sem_shapes = []
    for members in groups:
        sem_shapes += [pltpu.SemaphoreType.DMA((3 * len(members),))] * 2
    res = pl.pallas_call(
        body, name=name,
        in_specs=[HBM_SPEC] * n,
        out_specs=[SEM_SPEC] * (2 * ng) + [HBM_SPEC] * n + [pl.BlockSpec(memory_space=pltpu.VMEM)],
        out_shape=sem_shapes + [pltpu.HBM(f.shape, f.dtype) for f in fulls] + [jax.ShapeDtypeStruct((8, 128), F32)],
        input_output_aliases={t: 2 * ng + t for t in range(n)},
        compiler_params=pltpu.CompilerParams(has_side_effects=EFFECT),
    )(*[_hbm(f) for f in fulls])
    sems = [(res[2 * g], res[2 * g + 1]) for g in range(ng)]
    return sems, list(res[2 * ng:2 * ng + n]), res[-1]


def _gather_relay(fulls, axes, split, sems, after, name):
    n = len(fulls)
    nsplit = sum(split)

    def body(*refs):
        send_sems, recv_sems = refs[n], refs[n + 1]
        outs = refs[n + 3:]
        d_send, d_recv = outs[0], outs[1]
        thru = outs[2:2 + n]
        token = outs[-1]
        slot, peers, slots = _chip_peers()
        c = lax.axis_index("c")
        sibling = (lax.axis_index("x"), lax.axis_index("y"), 1 - c)
        for t in range(n):
            half = c if split[t] else None
            for k in range(3):
                cp = pltpu.make_async_remote_copy(
                    src_ref=_piece(thru[t], axes[t], slot, half), dst_ref=_piece(thru[t], axes[t], slots[k], half),
                    send_sem=send_sems.at[3 * t + k], recv_sem=recv_sems.at[3 * t + k],
                    device_id=peers[k], device_id_type=MESH)
                cp.wait_send()
                cp.wait_recv()
        i = 0
        for t in range(n):
            if not split[t]:
                continue
            for k in range(3):
                got = _piece(thru[t], axes[t], slots[k], c)
                pltpu.make_async_remote_copy(
                    src_ref=got, dst_ref=got, send_sem=d_send.at[3 * i + k], recv_sem=d_recv.at[3 * i + k],
                    device_id=sibling, device_id_type=MESH).start()
            i += 1
        token[...] = jnp.zeros_like(token)

    res = pl.pallas_call(
        body, name=name,
        in_specs=[HBM_SPEC] * n + [SEM_SPEC, SEM_SPEC, pl.BlockSpec(memory_space=pl.ANY)],
        out_specs=[SEM_SPEC, SEM_SPEC] + [HBM_SPEC] * n + [pl.BlockSpec(memory_space=pltpu.VMEM)],
        out_shape=[pltpu.SemaphoreType.DMA((3 * nsplit,)), pltpu.SemaphoreType.DMA((3 * nsplit,))]
        + [pltpu.HBM(f.shape, f.dtype) for f in fulls] + [jax.ShapeDtypeStruct((8, 128), F32)],
        input_output_aliases={t: 2 + t for t in range(n)},
        compiler_params=pltpu.CompilerParams(has_side_effects=EFFECT),
    )(*fulls, sems[0], sems[1], after)
    return (res[0], res[1]), list(res[2:2 + n]), res[-1]


def _gather_finish(fulls, axes, split, sems, after, name):
    n = len(fulls)

    def body(*refs):
        d_send, d_recv = refs[n], refs[n + 1]
        thru = refs[n + 3:]
        _, _, slots = _chip_peers()
        c = lax.axis_index("c")
        sibling = (lax.axis_index("x"), lax.axis_index("y"), 1 - c)
        i = 0
        for t in range(n):
            if not split[t]:
                continue
            for k in range(3):
                cp = pltpu.make_async_remote_copy(
                    src_ref=_piece(thru[t], axes[t], slots[k], c), dst_ref=_piece(thru[t], axes[t], slots[k], 1 - c),
                    send_sem=d_send.at[3 * i + k], recv_sem=d_recv.at[3 * i + k],
                    device_id=sibling, device_id_type=MESH)
                cp.wait_send()
                cp.wait_recv()
            i += 1

    return pl.pallas_call(
        body, name=name,
        in_specs=[HBM_SPEC] * n + [SEM_SPEC, SEM_SPEC, pl.BlockSpec(memory_space=pl.ANY)],
        out_specs=[HBM_SPEC] * n,
        out_shape=[pltpu.HBM(f.shape, f.dtype) for f in fulls],
        input_output_aliases={t: t for t in range(n)},
        compiler_params=pltpu.CompilerParams(has_side_effects=EFFECT),
    )(*fulls, sems[0], sems[1], after)


def _scatter_start(grads_bf16, axes, name):
    n = len(grads_bf16)

    def shard_shape(g, ax):
        return (g.shape[0] // N_CHIPS, g.shape[1]) if ax == 0 else (g.shape[0], g.shape[1] // N_CHIPS)

    shapes = [shard_shape(g, ax) for g, ax in zip(grads_bf16, axes)]

    def body(*refs):
        outs = refs[2 * n:]
        send_sems, recv_sems = outs[0], outs[1]
        gb, land = outs[2:2 + n], outs[2 + n:2 + 2 * n]
        token = outs[-1]
        _, peers, slots = _chip_peers()
        for t in range(n):
            size = shapes[t][axes[t]]
            for k in range(3):
                pltpu.make_async_remote_copy(
                    src_ref=_window(gb[t], axes[t], slots[k], size), dst_ref=land[t].at[k],
                    send_sem=send_sems.at[3 * t + k], recv_sem=recv_sems.at[3 * t + k],
                    device_id=peers[k], device_id_type=MESH).start()
        token[...] = jnp.zeros_like(token)

    lands = [_hbm(lax.empty((3,) + sh, BF16)) for sh in shapes]
    res = pl.pallas_call(
        body, name=name,
        in_specs=[HBM_SPEC] * (2 * n),
        out_specs=[SEM_SPEC, SEM_SPEC] + [HBM_SPEC] * (2 * n) + [pl.BlockSpec(memory_space=pltpu.VMEM)],
        out_shape=[pltpu.SemaphoreType.DMA((3 * n,)), pltpu.SemaphoreType.DMA((3 * n,))]
        + [pltpu.HBM(g.shape, g.dtype) for g in grads_bf16] + [pltpu.HBM((3,) + sh, BF16) for sh in shapes]
        + [jax.ShapeDtypeStruct((8, 128), F32)],
        input_output_aliases={t: 2 + t for t in range(2 * n)},
        compiler_params=pltpu.CompilerParams(has_side_effects=EFFECT),
    )(*[_hbm(g) for g in grads_bf16], *lands)
    return (res[0], res[1]), list(res[2:2 + n]), list(res[2 + n:2 + 2 * n]), res[-1]


def _scatter_wait(grads_thru, lands_thru, axes, sems, after, name):
    n = len(grads_thru)

    def body(*refs):
        send_sems, recv_sems = refs[2 * n], refs[2 * n + 1]
        outs = refs[2 * n + 3:]
        gb, land = outs[:n], outs[n:]
        _, peers, slots = _chip_peers()
        for t in range(n):
            size = land[t].shape[1 + axes[t]]
            for k in range(3):
                cp = pltpu.make_async_remote_copy(
                    src_ref=_window(gb[t], axes[t], slots[k], size), dst_ref=land[t].at[k],
                    send_sem=send_sems.at[3 * t + k], recv_sem=recv_sems.at[3 * t + k],
                    device_id=peers[k], device_id_type=MESH)
                cp.wait_send()
                cp.wait_recv()

    res = pl.pallas_call(
        body, name=name,
        in_specs=[HBM_SPEC] * (2 * n) + [SEM_SPEC, SEM_SPEC, pl.BlockSpec(memory_space=pl.ANY)],
        out_specs=[HBM_SPEC] * (2 * n),
        out_shape=[pltpu.HBM(g.shape, g.dtype) for g in grads_thru] + [pltpu.HBM(l.shape, l.dtype) for l in lands_thru],
        input_output_aliases={t: t for t in range(2 * n)},
        compiler_params=pltpu.CompilerParams(has_side_effects=EFFECT),
    )(*grads_thru, *lands_thru, sems[0], sems[1], after)
    return list(res[n:])


def _sibling_start(arrs, name):
    n = len(arrs)

    def body(*refs):
        outs = refs[2 * n:]
        send_sems, recv_sems = outs[0], outs[1]
        src, land = outs[2:2 + n], outs[2 + n:2 + 2 * n]
        token = outs[-1]
        sibling = (lax.axis_index("x"), lax.axis_index("y"), 1 - lax.axis_index("c"))
        for t in range(n):
            pltpu.make_async_remote_copy(
                src_ref=src[t], dst_ref=land[t], send_sem=send_sems.at[t], recv_sem=recv_sems.at[t],
                device_id=sibling, device_id_type=MESH).start()
        token[...] = jnp.zeros_like(token)

    lands = [_hbm(lax.empty(a.shape, a.dtype)) for a in arrs]
    res = pl.pallas_call(
        body, name=name,
        in_specs=[HBM_SPEC] * (2 * n),
        out_specs=[SEM_SPEC, SEM_SPEC] + [HBM_SPEC] * (2 * n) + [pl.BlockSpec(memory_space=pltpu.VMEM)],
        out_shape=[pltpu.SemaphoreType.DMA((n,)), pltpu.SemaphoreType.DMA((n,))]
        + [pltpu.HBM(a.shape, a.dtype) for a in arrs] * 2 + [jax.ShapeDtypeStruct((8, 128), F32)],
        input_output_aliases={t: 2 + t for t in range(2 * n)},
        compiler_params=pltpu.CompilerParams(has_side_effects=EFFECT),
    )(*[_hbm(a) for a in arrs], *lands)
    return (res[0], res[1]), list(res[2:2 + n]), list(res[2 + n:2 + 2 * n]), res[-1]


def _sibling_wait(src_thru, lands_thru, sems, after, name):
    n = len(src_thru)

    def body(*refs):
        send_sems, recv_sems = refs[2 * n], refs[2 * n + 1]
        outs = refs[2 * n + 3:]
        src, land = outs[:n], outs[n:]
        sibling = (lax.axis_index("x"), lax.axis_index("y"), 1 - lax.axis_index("c"))
        for t in range(n):
            cp = pltpu.make_async_remote_copy(
                src_ref=src[t], dst_ref=land[t], send_sem=send_sems.at[t], recv_sem=recv_sems.at[t],
                device_id=sibling, device_id_type=MESH)
            cp.wait_send()
            cp.wait_recv()

    res = pl.pallas_call(
        body, name=name,
        in_specs=[HBM_SPEC] * (2 * n) + [SEM_SPEC, SEM_SPEC, pl.BlockSpec(memory_space=pl.ANY)],
        out_specs=[HBM_SPEC] * (2 * n),
        out_shape=[pltpu.HBM(a.shape, a.dtype) for a in src_thru] * 2,
        input_output_aliases={t: t for t in range(2 * n)},
        compiler_params=pltpu.CompilerParams(has_side_effects=EFFECT),
    )(*src_thru, *lands_thru, sems[0], sems[1], after)
    return list(res[:n]), list(res[n:])


def _all_reduce_small(packed, name):
    nc = packed.shape[1]
    vmem = pl.BlockSpec(memory_space=pltpu.VMEM)

    def body(in_ref, out_ref, gbuf, send_sems, recv_sems):
        x, y, c = lax.axis_index("x"), lax.axis_index("y"), lax.axis_index("c")
        me = 4 * x + 2 * y + c
        gbuf[me] = jnp.sum(in_ref[...], axis=0, keepdims=True)
        copies = []
        for k in range(1, 8):
            peer = (x ^ ((k >> 2) & 1), y ^ ((k >> 1) & 1), c ^ (k & 1))
            rc = pltpu.make_async_remote_copy(
                src_ref=gbuf.at[me], dst_ref=gbuf.at[me], send_sem=send_sems.at[k - 1], recv_sem=recv_sems.at[k - 1],
                device_id=peer, device_id_type=MESH)
            rc.start()
            copies.append(rc)
        for k in range(1, 8):
            peer = (x ^ ((k >> 2) & 1), y ^ ((k >> 1) & 1), c ^ (k & 1))
            pltpu.make_async_remote_copy(
                src_ref=gbuf.at[me], dst_ref=gbuf.at[me ^ k], send_sem=send_sems.at[k - 1],
                recv_sem=recv_sems.at[k - 1], device_id=peer, device_id_type=MESH).wait_recv()
        for rc in copies:
            rc.wait_send()
        tot = gbuf[0]
        for d in range(1, 8):
            tot = tot + gbuf[d]
        out_ref[...] = tot

    return pl.pallas_call(
        body, name=name,
        in_specs=[vmem], out_specs=vmem,
        out_shape=jax.ShapeDtypeStruct((1, nc), F32),
        scratch_shapes=[pltpu.VMEM((8, 1, nc), F32), pltpu.SemaphoreType.DMA((7,)), pltpu.SemaphoreType.DMA((7,))],
    )(packed)


def _sum4(g_full, axis, slot_arr, recv, name):
    _, r, c = recv.shape
    tr = min(r, 128)
    nb = r // tr

    def body(slot_ref, own_ref, recv_ref, o_ref):
        acc = own_ref[...]
        for k in range(3):
            acc = acc + recv_ref[k].astype(F32)
        o_ref[...] = acc

    if axis == 0:
        own_map = lambda i, s: (s[0] * nb + i, 0)
    else:
        own_map = lambda i, s: (i, s[0])
    return pl.pallas_call(
        body, name=name,
        grid_spec=pltpu.PrefetchScalarGridSpec(
            num_scalar_prefetch=1, grid=(nb,),
            in_specs=[pl.BlockSpec((tr, c), own_map), pl.BlockSpec((3, tr, c), lambda i, s: (0, i, 0))],
            out_specs=pl.BlockSpec((tr, c), lambda i, s: (i, 0))),
        out_shape=jax.ShapeDtypeStruct((r, c), F32),
        compiler_params=_params("parallel"),
    )(slot_arr, g_full, recv)


def _adamw(w, g_parts, m, v, name):
    r, c = w.shape
    tr = r if r % 128 else _row_tile(r, 256)
    npart = len(g_parts)

    def body(*refs):
        w_ref = refs[0]
        g_refs = refs[1:1 + npart]
        m_ref, v_ref, g_out, d_out, m_out, v_out = refs[1 + npart:]
        g = g_refs[0][...]
        for gr in g_refs[1:]:
            g = g + gr[...]
        mm = ADAM_B1 * m_ref[...] + (1.0 - ADAM_B1) * g
        vv = ADAM_B2 * v_ref[...] + (1.0 - ADAM_B2) * (g * g)
        m_hat = mm / (1.0 - ADAM_B1 ** ADAM_STEP)
        v_hat = vv / (1.0 - ADAM_B2 ** ADAM_STEP)
        g_out[...] = g
        d_out[...] = -ADAM_LR * (m_hat / (jnp.sqrt(v_hat) + ADAM_EPS) + ADAM_WD * w_ref[...])
        m_out[...] = mm
        v_out[...] = vv

    blk = pl.BlockSpec((tr, c), lambda i: (i, 0))
    shp = jax.ShapeDtypeStruct((r, c), F32)
    return pl.pallas_call(
        body, name=name, grid=(r // tr,),
        in_specs=[blk] * (3 + npart), out_specs=[blk] * 4, out_shape=[shp] * 4,
        compiler_params=pltpu.CompilerParams(dimension_semantics=("parallel",), vmem_limit_bytes=MM_VMEM_LIMIT),
    )(w, *g_parts, m, v)


def _pad_rows8(w):
    return jnp.pad(w, ((0, HALO - w.shape[0]), (0, 0)))


def kernel(x, mem, hgrn_lb, norm1_w, w_in, hgrn_norm_w, sconv_w, w_out, norm2_w, mem_norm_w, wq, wk, wv, wo, norm3_w, w_gate, w_up, ffn_conv_w, ffn_conv_b, w_down, final_norm_w, loss_target, m_hgrn_lb, m_norm1_w, m_w_in, m_hgrn_norm_w, m_sconv_w, m_w_out, m_norm2_w, m_mem_norm_w, m_wq, m_wk, m_wv, m_wo, m_norm3_w, m_w_gate, m_w_up, m_ffn_conv_w, m_ffn_conv_b, m_w_down, m_final_norm_w, v_hgrn_lb, v_norm1_w, v_w_in, v_hgrn_norm_w, v_sconv_w, v_w_out, v_norm2_w, v_mem_norm_w, v_wq, v_wk, v_wv, v_wo, v_norm3_w, v_w_gate, v_w_up, v_ffn_conv_w, v_ffn_conv_b, v_w_down, v_final_norm_w):
    xs, mems, tgt = x[0], mem[0], loss_target[0]
    d = xs.shape[1]
    fnw = final_norm_w.reshape(1, d)

    big = {"w_in": (w_in[0], 1), "w_out": (w_out[0], 0), "wq": (wq[0], 0), "wk": (wk[0], 0), "wv": (wv[0], 0),
           "wo": (wo[0], 0), "w_gate": (w_gate[0], 1), "w_up": (w_up[0], 1), "w_down": (w_down[0], 0)}
    names = list(big)
    slot_arr = (2 * lax.axis_index("x") + lax.axis_index("y")).astype(jnp.int32).reshape(1)
    gnames = names + ["sconv8", "fconv8"]
    axes = [big[n][1] for n in names] + [1, 1]
    groups = [["w_in"], ["w_out", "sconv8"], ["wq", "wk", "wv", "wo"], ["w_gate", "w_up", "fconv8", "w_down"]]
    gidx = [[gnames.index(n) for n in grp] for grp in groups]
    split = [True] * len(names) + [False, False]
    first = _cast_into_full(big["w_in"][0], 1, slot_arr, BF16, "cast_w_in")
    sems0, first, tok0 = _gather_start([first], [1], [True], [[0]], "gather_start_w_in")
    rest = [_cast_into_full(big[n][0], big[n][1], slot_arr, BF16, "cast_" + n, after=tok0) for n in names[1:]]
    rest += [_cast_into_full(_pad_rows8(sconv_w[0]), 1, slot_arr, F32, "cast_sconv_w", after=tok0),
             _cast_into_full(_pad_rows8(ffn_conv_w[0]), 1, slot_arr, F32, "cast_ffn_conv_w", after=tok0)]
    sems1, rest, tok = _gather_start(rest, axes[1:], split[1:], [[t - 1 for t in idx] for idx in gidx[1:]],
                                     "gather_start")
    gsems, fulls = sems0 + sems1, first + rest
    wf, relayed = {}, {}

    def gather_relay(g, after):
        idx = gidx[g]
        dsems, arrs, token = _gather_relay([fulls[t] for t in idx], [axes[t] for t in idx], [split[t] for t in idx],
                                           gsems[g], after, "gather_relay_%d" % g)
        relayed[g] = (dsems, arrs)
        return token[0:1, 0:1]

    def gather_finish(g, after):
        idx = gidx[g]
        dsems, arrs = relayed[g]
        got = _gather_finish(arrs, [axes[t] for t in idx], [split[t] for t in idx], dsems, after,
                             "gather_finish_%d" % g)
        wf.update(zip(groups[g], got))

    lb0, lb1 = hgrn_lb[0:1], hgrn_lb[1:2]

    h1 = _rmsnorm_fwd(xs, norm1_w + tok[0:1, 0:1], "norm1")
    gather_relay(0, h1)
    gather_finish(0, h1)
    proj = _matmul(h1, wf["w_in"], "nn", "proj_in", out_dtype=BF16)
    t1 = gather_relay(1, proj)
    o_h, og, states = _hgrn_fwd(proj, lb0, lb1, hgrn_norm_w + t1, "hgrn_fwd")
    gather_finish(1, o_h)
    t2 = gather_relay(2, o_h)
    sconv8 = wf["sconv8"]
    mix = _sconv_fwd(proj, sconv8, og, "sconv_fwd")
    x1 = _matmul(mix, wf["w_out"], "nn", "proj_out", residual=xs)
    t3 = gather_relay(3, x1)
    h2 = _rmsnorm_fwd(x1, norm2_w + (t2 + t3), "norm2")
    mem_n = _rmsnorm_fwd(mems, mem_norm_w, "norm_mem")
    gather_finish(2, h2)
    qa = _matmul(h2, wf["wq"], "nn", "attn_q", out_dtype=BF16)
    ka = _matmul(mem_n, wf["wk"], "nn", "attn_k", out_dtype=BF16)
    va = _matmul(mem_n, wf["wv"], "nn", "attn_v", out_dtype=BF16)
    att = _attn_fwd(qa, ka, va, "attn_fwd")
    x2 = _matmul(att, wf["wo"], "nn", "attn_o", residual=x1)
    h3 = _rmsnorm_fwd(x2, norm3_w, "norm3")
    gather_finish(3, h3)
    fconv8 = wf["fconv8"]
    gate = _matmul(h3, wf["w_gate"], "nn", "ffn_gate", out_dtype=BF16)
    up = _matmul(h3, wf["w_up"], "nn", "ffn_up", out_dtype=BF16)
    z, act = _ffn_fwd(gate, up, fconv8, ffn_conv_b, "ffn_act")
    x3 = _matmul(z, wf["w_down"], "nn", "ffn_down", residual=x2)

    dx3, dx3b, g_final, loss8 = _final_loss_bwd(x3, tgt, fnw, "loss_bwd")
    gw = {}
    dz = _matmul(dx3b, wf["w_down"], "nt", "d_z", out_dtype=BF16)
    gw["w_down"] = _matmul(z, dx3b, "tn", "g_w_down", extra_bf16=True)
    dgate, du, g_fb, g_fw = _ffn_bwd(act, gate, up, dz, fconv8, "ffn_act_bwd")
    dh3 = _matmul(dgate, wf["w_gate"], "nt", "d_h3_gate")
    dh3 = _matmul(du, wf["w_up"], "nt", "d_h3_up", residual=dh3, out_dtype=BF16)
    gw["w_gate"] = _matmul(h3, dgate, "tn", "g_w_gate", extra_bf16=True)
    gw["w_up"] = _matmul(h3, du, "tn", "g_w_up", extra_bf16=True)
    pending = []

    def scatter_start(grp):
        sems, g_thru, lands, token = _scatter_start([gw[n][1] for n in grp], [big[n][1] for n in grp],
                                                    "scatter_start_" + grp[0])
        pending.append((grp, sems, g_thru, lands))
        return token[0:1, 0:1]

    tok1 = scatter_start(["w_down", "w_gate", "w_up"])
    dx2, dx2b, g_n3 = _rmsnorm_bwd(dh3, x2, norm3_w + tok1, dx3, "norm3_bwd")
    datt = _matmul(dx2b, wf["wo"], "nt", "d_att", out_dtype=BF16)
    gw["wo"] = _matmul(att, dx2b, "tn", "g_wo", extra_bf16=True)
    dqa, dka, dva = _attn_bwd(qa, ka, va, datt, "attn_bwd")
    dh2 = _matmul(dqa, wf["wq"], "nt", "d_h2", out_dtype=BF16)
    gw["wq"] = _matmul(h2, dqa, "tn", "g_wq", extra_bf16=True)
    gw["wk"] = _matmul(mem_n, dka, "tn", "g_wk", extra_bf16=True)
    gw["wv"] = _matmul(mem_n, dva, "tn", "g_wv", extra_bf16=True)
    tok2 = scatter_start(["wo", "wq", "wk", "wv"])
    dmem_n = _matmul(dka, wf["wk"], "nt", "d_memn_k")
    dmem_n = _matmul(dva, wf["wv"], "nt", "d_memn_v", residual=dmem_n)
    _, _, g_nm = _rmsnorm_bwd(dmem_n, mems, mem_norm_w, None, "norm_mem_bwd")
    dx1, dx1b, g_n2 = _rmsnorm_bwd(dh2, x1, norm2_w + tok2, dx2, "norm2_bwd")
    dmix = _matmul(dx1b, wf["w_out"], "nt", "d_mix", out_dtype=BF16)
    gw["w_out"] = _matmul(mix, dx1b, "tn", "g_w_out", extra_bf16=True)
    tok3 = scatter_start(["w_out"])
    dcb, dcc, dch, g_sw = _sconv_bwd(proj, sconv8, dmix, "sconv_bwd")
    dq, df, di, dg, g_lb, g_hn = _hgrn_bwd(proj, lb0, lb1, hgrn_norm_w + tok3, o_h, states, dmix, "hgrn_bwd")
    dproj = jnp.concatenate([dq, df, di, dg, dcb, dcc, dch], axis=1)
    gw["w_in"] = _matmul(h1, dproj, "tn", "g_w_in", extra_bf16=True)
    tok4 = scatter_start(["w_in"])
    dh1 = _matmul(dproj, wf["w_in"], "nt", "d_h1", out_dtype=BF16)
    dx, _, g_n1 = _rmsnorm_bwd(dh1, xs, norm1_w + tok4, dx1, "norm1_bwd")

    small = [g_n1, g_n2, g_n3, g_final, g_nm, g_lb, g_hn, g_fb,
             g_sw[0:8], g_sw[8:16], g_sw[16:24], g_fw[0:8], g_fw[8:16], g_fw[16:24], loss8]
    widths = [a.shape[1] for a in small]
    tot = _all_reduce_small(jnp.concatenate(small, axis=1), "all_reduce_small")
    offs = [0]
    for wd_ in widths:
        offs.append(offs[-1] + wd_)
    sm = [tot[:, offs[i]:offs[i + 1]] for i in range(len(small))]
    s_n1, s_n2, s_n3, s_final, s_nm, s_lb, s_hn, s_fb = sm[:8]
    s_sw = jnp.concatenate(sm[8:11], axis=0)
    s_fw = jnp.concatenate(sm[11:14], axis=0)
    loss = sm[14][0, 0]
    slot = 2 * lax.axis_index("x") + lax.axis_index("y")
    s_sw = lax.dynamic_slice_in_dim(s_sw, slot * (HGRN_W // N_CHIPS), HGRN_W // N_CHIPS, axis=1)
    fsh = ffn_conv_w.shape[2]
    s_fw = lax.dynamic_slice_in_dim(s_fw, slot * fsh, fsh, axis=1)
    s_lb2 = jnp.concatenate([s_lb, -s_lb], axis=0)

    swaps = []
    after = tot
    for grp, sems, g_thru, lands in pending:
        got = _scatter_wait(g_thru, lands, [big[n][1] for n in grp], sems, after, "scatter_wait_" + grp[0])
        sums = [_sum4(gw[n][0], big[n][1], slot_arr, r, "core_sum_" + n) for n, r in zip(grp, got)]
        ssems, s_thru, s_lands, after = _sibling_start(sums, "sibling_start_" + grp[0])
        swaps.append((grp, ssems, s_thru, s_lands))

    moments = {"hgrn_lb": (m_hgrn_lb, v_hgrn_lb), "norm1_w": (m_norm1_w, v_norm1_w), "w_in": (m_w_in, v_w_in),
               "hgrn_norm_w": (m_hgrn_norm_w, v_hgrn_norm_w), "sconv_w": (m_sconv_w, v_sconv_w),
               "w_out": (m_w_out, v_w_out), "norm2_w": (m_norm2_w, v_norm2_w),
               "mem_norm_w": (m_mem_norm_w, v_mem_norm_w), "wq": (m_wq, v_wq), "wk": (m_wk, v_wk), "wv": (m_wv, v_wv),
               "wo": (m_wo, v_wo), "norm3_w": (m_norm3_w, v_norm3_w), "w_gate": (m_w_gate, v_w_gate),
               "w_up": (m_w_up, v_w_up), "ffn_conv_w": (m_ffn_conv_w, v_ffn_conv_w),
               "ffn_conv_b": (m_ffn_conv_b, v_ffn_conv_b), "w_down": (m_w_down, v_w_down),
               "final_norm_w": (m_final_norm_w, v_final_norm_w)}
    weights = {"hgrn_lb": hgrn_lb, "norm1_w": norm1_w, "w_in": w_in, "hgrn_norm_w": hgrn_norm_w, "sconv_w": sconv_w,
               "w_out": w_out, "norm2_w": norm2_w, "mem_norm_w": mem_norm_w, "wq": wq, "wk": wk, "wv": wv, "wo": wo,
               "norm3_w": norm3_w, "w_gate": w_gate, "w_up": w_up, "ffn_conv_w": ffn_conv_w, "ffn_conv_b": ffn_conv_b,
               "w_down": w_down, "final_norm_w": final_norm_w}
    small_g = {"hgrn_lb": s_lb2, "norm1_w": s_n1, "hgrn_norm_w": s_hn, "sconv_w": s_sw, "norm2_w": s_n2,
               "mem_norm_w": s_nm, "norm3_w": s_n3, "ffn_conv_w": s_fw, "ffn_conv_b": s_fb, "final_norm_w": s_final}
    order = list(weights)
    res = {}

    def adamw(n, parts):
        shape = weights[n].shape
        w2 = weights[n].reshape((-1, shape[-1]))
        m2, v2 = (t.reshape(w2.shape) for t in moments[n])
        res[n] = [t.reshape(shape) for t in _adamw(w2, [p.reshape(w2.shape) for p in parts], m2, v2, "adamw_" + n)]

    for n in order:
        if n not in big:
            adamw(n, [small_g[n]])
    after = after + res["final_norm_w"][1][0]
    for grp, ssems, s_thru, s_lands in swaps:
        own, other = _sibling_wait(s_thru, s_lands, ssems, after, "sibling_wait_" + grp[0])
        for n, a, b in zip(grp, own, other):
            adamw(n, [a, b])
        after = res[grp[-1]][1]

    return (loss, dx[None], *[res[n][0] for n in order], *[res[n][1] for n in order],
            *[res[n][2] for n in order], *[res[n][3] for n in order])
```

```python
import functools

import jax
import jax.numpy as jnp
from jax import lax
from jax.experimental import pallas as pl
from jax.experimental.pallas import tpu as pltpu

F32 = jnp.float32
BF16 = jnp.bfloat16
MESH = pl.DeviceIdType.MESH

EPS = 1e-6
HGRN_W = 1024
HEAD = 128
N_HEADS = 8
CHUNK = 64
HGRN_UNROLL = 8
HGRN_HEADS_PER_STEP = 2
DPROJ_GROUPS = 8
MEM_HEADS = 4
MEM_HEAD_DIM = 512
N_CHIPS = 4
HALO = 8

ADAM_LR = 0.001
ADAM_B1 = 0.9
ADAM_B2 = 0.999
ADAM_EPS = 1e-08
ADAM_WD = 0.01
ADAM_STEP = 10


def _sigmoid(x):
    return 1.0 / (1.0 + jnp.exp(-x))


def _dot(a, b, dims):
    return lax.dot_general(a.astype(BF16), b.astype(BF16), (dims, ((), ())),
                           preferred_element_type=F32)


def _dot_nn(a, b):
    return _dot(a, b, ((1,), (0,)))


def _dot_nt(a, b):
    return _dot(a, b, ((1,), (1,)))


def _dot_tn(a, b):
    return _dot(a, b, ((0,), (0,)))


def _hdot(a, b, dims):
    return lax.dot_general(a, b, (dims, ((), ())), precision=lax.Precision.HIGH, preferred_element_type=F32)


def _hdot_nn(a, b):
    return _hdot(a, b, ((1,), (0,)))


def _hdot_nt(a, b):
    return _hdot(a, b, ((1,), (1,)))


def _hdot_tn(a, b):
    return _hdot(a, b, ((0,), (0,)))


def _exact_ones_dot(ones_bf16, x):
    hi = x.astype(BF16)
    r1 = x - hi.astype(F32)
    mid = r1.astype(BF16)
    lo = (r1 - mid.astype(F32)).astype(BF16)
    dims = (((1,), (0,)), ((), ()))
    return (lax.dot_general(ones_bf16, hi, dims, preferred_element_type=F32)
            + lax.dot_general(ones_bf16, mid, dims, preferred_element_type=F32)
            + lax.dot_general(ones_bf16, lo, dims, preferred_element_type=F32))


def _rows8(v):
    t, c = v.shape
    return v.reshape(t // 8, 8, c).sum(axis=0)


def _shift_down(x, halo, s):
    rolled = pltpu.roll(x, s, 0)
    hrolled = pltpu.roll(halo, s, 0)
    row = lax.broadcasted_iota(jnp.int32, hrolled.shape, 0)
    head = jnp.where(row < s, hrolled, rolled[:HALO])
    return jnp.concatenate([head, rolled[HALO:]], axis=0)


def _shift_up(x, halo, s):
    t = x.shape[0]
    rolled = pltpu.roll(x, t - s, 0)
    hrolled = pltpu.roll(halo, HALO - s, 0)
    row = lax.broadcasted_iota(jnp.int32, hrolled.shape, 0)
    tail = jnp.where(row >= HALO - s, hrolled, rolled[t - HALO:])
    return jnp.concatenate([rolled[:t - HALO], tail], axis=0)


def _params(*sem):
    return pltpu.CompilerParams(dimension_semantics=sem)


def _row_tile(r, pref):
    while r % pref:
        pref //= 2
    return pref


def _rmsnorm_fwd(x, w, name, tm=256):
    s, d = x.shape
    tm = min(tm, s)

    def body(x_ref, w_ref, o_ref):
        xv = x_ref[...]
        r = lax.rsqrt(jnp.mean(xv * xv, axis=-1, keepdims=True) + EPS)
        o_ref[...] = ((xv * r) * w_ref[...]).astype(BF16)

    return pl.pallas_call(
        body, name=name, grid=(s // tm,),
        in_specs=[pl.BlockSpec((tm, d), lambda i: (i, 0)), pl.BlockSpec((1, d), lambda i: (0, 0))],
        out_specs=pl.BlockSpec((tm, d), lambda i: (i, 0)),
        out_shape=jax.ShapeDtypeStruct((s, d), BF16),
        compiler_params=_params("parallel"),
    )(x, w)


def _rmsnorm_bwd(dh, x, w, dres, name, tm=256):
    s, d = x.shape
    tm = min(tm, s)
    has_res = dres is not None

    def body(*refs):
        if has_res:
            dh_ref, x_ref, w_ref, dres_ref, dx_ref, dxb_ref, gw_ref = refs
        else:
            dh_ref, x_ref, w_ref, dx_ref, dxb_ref, gw_ref = refs

        @pl.when(pl.program_id(0) == 0)
        def _():
            gw_ref[...] = jnp.zeros_like(gw_ref)

        xv = x_ref[...]
        dhv = dh_ref[...].astype(F32)
        r = lax.rsqrt(jnp.mean(xv * xv, axis=-1, keepdims=True) + EPS)
        xhat = xv * r
        gw_ref[...] += _rows8(dhv * xhat)
        dxh = dhv * w_ref[...]
        dx = r * (dxh - xhat * jnp.mean(dxh * xhat, axis=-1, keepdims=True))
        if has_res:
            dx = dres_ref[...] + dx
        dx_ref[...] = dx
        dxb_ref[...] = dx.astype(BF16)

    row = pl.BlockSpec((tm, d), lambda i: (i, 0))
    in_specs = [row, row, pl.BlockSpec((1, d), lambda i: (0, 0))] + ([row] if has_res else [])
    args = (dh, x, w) + ((dres,) if has_res else ())
    return pl.pallas_call(
        body, name=name, grid=(s // tm,),
        in_specs=in_specs,
        out_specs=[row, row, pl.BlockSpec((8, d), lambda i: (0, 0))],
        out_shape=[jax.ShapeDtypeStruct((s, d), F32), jax.ShapeDtypeStruct((s, d), BF16),
                   jax.ShapeDtypeStruct((8, d), F32)],
        compiler_params=_params("arbitrary"),
    )(*args)


def _final_loss_bwd(x3, target, w, name, tm=256):
    s, d = x3.shape
    tm = min(tm, s)

    def body(x_ref, t_ref, w_ref, dx_ref, dxb_ref, gw_ref, loss_ref):
        @pl.when(pl.program_id(0) == 0)
        def _():
            gw_ref[...] = jnp.zeros_like(gw_ref)
            loss_ref[...] = jnp.zeros_like(loss_ref)

        xv = x_ref[...]
        r = lax.rsqrt(jnp.mean(xv * xv, axis=-1, keepdims=True) + EPS)
        xhat = xv * r
        y = xhat * w_ref[...]
        err = y - t_ref[...]
        part = 0.5 * jnp.mean(err * err, axis=-1, keepdims=True)
        tot = jnp.sum(part, axis=0, keepdims=True)
        rr = lax.broadcasted_iota(jnp.int32, loss_ref.shape, 0)
        cc = lax.broadcasted_iota(jnp.int32, loss_ref.shape, 1)
        loss_ref[...] += jnp.where((rr == 0) & (cc == 0), tot, 0.0)
        dy = err * (1.0 / d)
        gw_ref[...] += _rows8(dy * xhat)
        dxh = dy * w_ref[...]
        dx = r * (dxh - xhat * jnp.mean(dxh * xhat, axis=-1, keepdims=True))
        dx_ref[...] = dx
        dxb_ref[...] = dx.astype(BF16)

    row = pl.BlockSpec((tm, d), lambda i: (i, 0))
    return pl.pallas_call(
        body, name=name, grid=(s // tm,),
        in_specs=[row, row, pl.BlockSpec((1, d), lambda i: (0, 0))],
        out_specs=[row, row, pl.BlockSpec((8, d), lambda i: (0, 0)), pl.BlockSpec((8, 128), lambda i: (0, 0))],
        out_shape=[jax.ShapeDtypeStruct((s, d), F32), jax.ShapeDtypeStruct((s, d), BF16),
                   jax.ShapeDtypeStruct((8, d), F32), jax.ShapeDtypeStruct((8, 128), F32)],
        compiler_params=_params("arbitrary"),
    )(x3, target, w)


MM_TILES = (1024, 1408, 512, 256, 128)
MM_K_TILES = (2816, 2048, 1792, 1408, 1024, 512, 256, 128)
MM_VMEM_LIMIT = 56 * 1024 * 1024
MM_VMEM_BUDGET = 46 * 1024 * 1024


def _pick_tile(dim):
    for t in MM_TILES:
        if dim % t == 0:
            return t
    return dim


def _matmul(a, b, mode, name, *, out_dtype=F32, residual=None, extra_bf16=False, tm=None, tn=None, tk=None,
            groups=None):
    if groups is not None and mode == "nt":
        _, m, gw = a.shape
        n, k2 = b.shape
        k, tk = groups * gw, gw
    elif groups is not None and mode == "tn":
        k, m = a.shape
        _, k2, gw = b.shape
        n, tn = groups * gw, gw
    elif mode == "nn":
        (m, k), (k2, n) = a.shape, b.shape
    elif mode == "nt":
        (m, k), (n, k2) = a.shape, b.shape
    else:
        (k, m), (k2, n) = a.shape, b.shape
    assert k == k2, (a.shape, b.shape, mode)
    tm = _pick_tile(m) if tm is None else min(tm, m)
    tn = _pick_tile(n) if tn is None else min(tn, n)
    out_bytes = tm * tn * (jnp.dtype(out_dtype).itemsize + (2 if extra_bf16 else 0) + (4 if residual is not None else 0))

    def vmem_bytes(t):
        return (2 * (tm * t * a.dtype.itemsize + t * tn * b.dtype.itemsize) + 2 * out_bytes
                + (tm * tn * 4 if t < k else 0))

    if tk is None:
        tk = next(t for t in MM_K_TILES if k % t == 0 and t <= k and vmem_bytes(t) <= MM_VMEM_BUDGET)
    assert m % tm == 0 and n % tn == 0 and k % tk == 0, (m, n, k, tm, tn, tk)
    nk = k // tk
    dims = {"nn": ((1,), (0,)), "nt": ((1,), (1,)), "tn": ((0,), (0,))}[mode]
    has_res = residual is not None

    def body(*refs):
        refs = list(refs)
        a_ref, b_ref = refs[0], refs[1]
        r_ref = refs[2] if has_res else None
        outs = refs[2 + has_res:]
        o_ref = outs[0]
        o2_ref = outs[1] if extra_bf16 else None
        def finish(r):
            if has_res:
                r = r_ref[...] + r
            o_ref[...] = r.astype(out_dtype)
            if extra_bf16:
                o2_ref[...] = r.astype(BF16)

        if nk == 1:
            finish(_dot(a_ref[...], b_ref[...], dims))
            return
        acc = outs[-1]
        kk = pl.program_id(2)

        @pl.when(kk == 0)
        def _():
            acc[...] = _dot(a_ref[...], b_ref[...], dims)

        if nk > 2:
            @pl.when((kk > 0) & (kk < nk - 1))
            def _():
                acc[...] += _dot(a_ref[...], b_ref[...], dims)

        @pl.when(kk == nk - 1)
        def _():
            finish(acc[...] + _dot(a_ref[...], b_ref[...], dims))

    if mode == "tn":
        a_spec = pl.BlockSpec((tk, tm), lambda i, j, kk: (kk, i))
    elif groups is not None:
        a_spec = pl.BlockSpec((None, tm, tk), lambda i, j, kk: (kk, i, 0))
    else:
        a_spec = pl.BlockSpec((tm, tk), lambda i, j, kk: (i, kk))
    if mode == "nt":
        b_spec = pl.BlockSpec((tn, tk), lambda i, j, kk: (j, kk))
    elif groups is not None:
        b_spec = pl.BlockSpec((None, tk, tn), lambda i, j, kk: (j, kk, 0))
    else:
        b_spec = pl.BlockSpec((tk, tn), lambda i, j, kk: (kk, j))
    o_spec = pl.BlockSpec((tm, tn), lambda i, j, kk: (i, j))
    in_specs = [a_spec, b_spec] + ([o_spec] if has_res else [])
    out_specs = [o_spec] + ([o_spec] if extra_bf16 else [])
    out_shape = [jax.ShapeDtypeStruct((m, n), out_dtype)] + ([jax.ShapeDtypeStruct((m, n), BF16)] if extra_bf16 else [])
    args = (a, b) + ((residual,) if has_res else ())
    res = pl.pallas_call(
        body, name=name, grid=(m // tm, n // tn, nk),
        in_specs=in_specs, out_specs=out_specs, out_shape=out_shape,
        scratch_shapes=[pltpu.VMEM((tm, tn) if nk > 1 else (8, 128), F32)],
        compiler_params=pltpu.CompilerParams(dimension_semantics=("parallel", "parallel", "arbitrary"),
                                             vmem_limit_bytes=MM_VMEM_LIMIT),
    )(*args)
    return res if extra_bf16 else res[0]


def _hgrn_gates(qp, fp, lb):
    sig = _sigmoid(fp)
    f = lb + (1.0 - lb) * sig
    logf = jnp.log(f)
    k = 1.0 - f
    sq = _sigmoid(qp)
    q = qp * sq
    return sig, f, logf, k, sq, q


def _hgrn_fwd(proj, lb0, lb1, norm_w, name, tb=512):
    s = proj.shape[0]
    tb = min(tb, s)
    nb, ncb = s // tb, tb // CHUNK

    def body(q_ref, f_ref, i_ref, g_ref, a0_ref, a1_ref, nw_ref, o_ref, og_ref, st_ref, state):
        @pl.when(pl.program_id(1) == 0)
        def _():
            state[...] = jnp.zeros_like(state)

        lb2 = _sigmoid(a0_ref[...] - a1_ref[...])
        row = lax.broadcasted_iota(jnp.int32, (CHUNK, CHUNK), 0)
        col = lax.broadcasted_iota(jnp.int32, (CHUNK, CHUNK), 1)
        tril = row >= col
        ones_l = tril.astype(BF16)
        nw = nw_ref[...]

        def chunk(c, carry):
            rows = pl.ds(pl.multiple_of(c * CHUNK, CHUNK), CHUNK)
            for hh in range(HGRN_HEADS_PER_STEP):
                cols = slice(hh * HEAD, (hh + 1) * HEAD)
                v = i_ref[rows, cols].astype(F32)
                _, _, logf, k, _, q = _hgrn_gates(q_ref[rows, cols].astype(F32), f_ref[rows, cols].astype(F32),
                                                  lb2[:, cols])
                b = _exact_ones_dot(ones_l, logf)
                bl = jnp.sum(logf, axis=0, keepdims=True)
                bm = 0.5 * bl
                st = state[hh]
                st_ref[hh, c] = st
                qt = q * jnp.exp(b - bm)
                kt = k * jnp.exp(bm - b)
                a = jnp.where(tril, _dot_nt(qt, kt), 0.0)
                o = _dot_nt(q * jnp.exp(b), st) + _dot_nn(a, v)
                state[hh] = st * jnp.exp(bl) + _dot_tn(v, k * jnp.exp(bl - b))
                o_ref[rows, cols] = o
                on = (o * lax.rsqrt(jnp.mean(o * o, axis=-1, keepdims=True) + EPS)) * nw
                gv = g_ref[rows, cols].astype(F32)
                og_ref[rows, cols] = (on * (gv * _sigmoid(gv))).astype(BF16)
            return carry

        lax.fori_loop(0, ncb, chunk, 0, unroll=HGRN_UNROLL)

    hp, wd = HGRN_HEADS_PER_STEP, HGRN_HEADS_PER_STEP * HEAD
    ngrp = N_HEADS // hp

    def colblk(group):
        return pl.BlockSpec((tb, wd), lambda h, j: (j, group * ngrp + h))

    vec = pl.BlockSpec((1, wd), lambda h, j: (0, h))
    out_blk = pl.BlockSpec((tb, wd), lambda h, j: (j, h))
    return pl.pallas_call(
        body, name=name, grid=(ngrp, nb),
        in_specs=[colblk(0), colblk(1), colblk(2), colblk(3), vec, vec, pl.BlockSpec((1, HEAD), lambda h, j: (0, 0))],
        out_specs=[out_blk, out_blk, pl.BlockSpec((hp, ncb, HEAD, HEAD), lambda h, j: (h, j, 0, 0))],
        out_shape=[jax.ShapeDtypeStruct((s, HGRN_W), F32), jax.ShapeDtypeStruct((s, 2 * HGRN_W), BF16),
                   jax.ShapeDtypeStruct((N_HEADS, s // CHUNK, HEAD, HEAD), F32)],
        scratch_shapes=[pltpu.VMEM((hp, HEAD, HEAD), F32)],
        compiler_params=_params("parallel", "arbitrary"),
    )(proj, proj, proj, proj, lb0, lb1, norm_w)


def _hgrn_bwd(proj, lb0, lb1, norm_w, o, states, dmix, name, tb=512):
    s = proj.shape[0]
    tb = min(tb, s)
    nb, ncb = s // tb, tb // CHUNK

    def body(q_ref, f_ref, i_ref, g_ref, a0_ref, a1_ref, nw_ref, o_ref, st_ref, dm_ref,
             dp_ref, glb_ref, gnw_ref, dstate):
        h = pl.program_id(0)

        @pl.when(pl.program_id(1) == 0)
        def _():
            dstate[...] = jnp.zeros_like(dstate)
            glb_ref[...] = jnp.zeros_like(glb_ref)

        @pl.when((pl.program_id(1) == 0) & (h == 0))
        def _():
            gnw_ref[...] = jnp.zeros_like(gnw_ref)

        lb2 = _sigmoid(a0_ref[...] - a1_ref[...])
        row = lax.broadcasted_iota(jnp.int32, (CHUNK, CHUNK), 0)
        col = lax.broadcasted_iota(jnp.int32, (CHUNK, CHUNK), 1)
        tril = row >= col
        ones_l = tril.astype(BF16)
        ones_u = (row <= col).astype(BF16)
        nw = nw_ref[...]

        def chunk(cc, carry):
            c = ncb - 1 - cc
            rows = pl.ds(pl.multiple_of(c * CHUNK, CHUNK), CHUNK)
            for hh in range(HGRN_HEADS_PER_STEP):
                cols = slice(hh * HEAD, (hh + 1) * HEAD)
                lb = lb2[:, cols]
                qp = q_ref[rows, cols].astype(F32)
                v = i_ref[rows, cols].astype(F32)
                sig, f, logf, k, sq, q = _hgrn_gates(qp, f_ref[rows, cols].astype(F32), lb)
                gv = g_ref[rows, cols].astype(F32)
                sg = _sigmoid(gv)
                silu_g = gv * sg
                dog = dm_ref[rows, cols].astype(F32)
                ov = o_ref[rows, cols]
                r = lax.rsqrt(jnp.mean(ov * ov, axis=-1, keepdims=True) + EPS)
                ohat = ov * r
                on = ohat * nw
                dp_ref[3, rows, cols] = (dog * on * (sg * (1.0 + gv * (1.0 - sg)))).astype(BF16)
                don = dog * silu_g
                gnw_ref[...] += _rows8(don * ohat)
                doh = don * nw
                do = r * (doh - ohat * jnp.mean(doh * ohat, axis=-1, keepdims=True))
                b = _exact_ones_dot(ones_l, logf)
                bl = jnp.sum(logf, axis=0, keepdims=True)
                bm = 0.5 * bl
                e_q = jnp.exp(b - bm)
                e_k = jnp.exp(bm - b)
                e_b = jnp.exp(b)
                e_l = jnp.exp(bl - b)
                qt, kt, qb, kb = q * e_q, k * e_k, q * e_b, k * e_l
                st0 = st_ref[hh, c]
                dst = dstate[hh]
                a = jnp.where(tril, _dot_nt(qt, kt), 0.0)
                da = jnp.where(tril, _dot_nt(do, v), 0.0)
                dq = _hdot_nn(da, kt) * e_q + _hdot_nn(do, st0) * e_b
                dkb = _hdot_nn(v, dst) * e_l
                dk = _hdot_tn(da, qt) * e_k + dkb
                dv = _dot_tn(a, do) + _dot_nt(kb, dst)
                e_bl = jnp.exp(bl)
                dstate[hh] = dst * e_bl + _dot_tn(do, qb)
                db = q * dq - k * dk
                db_last = jnp.sum(k * dkb, axis=0, keepdims=True) + e_bl * jnp.sum(st0 * dst, axis=0, keepdims=True)
                dlogf = _exact_ones_dot(ones_u, db) + db_last
                dfg = dlogf / f - dk
                dp_ref[1, rows, cols] = (dfg * (1.0 - lb) * (sig * (1.0 - sig))).astype(BF16)
                glb_ref[:, cols] += _rows8(dfg * (1.0 - sig)) * (lb * (1.0 - lb))
                dp_ref[0, rows, cols] = (dq * (sq * (1.0 + qp * (1.0 - sq)))).astype(BF16)
                dp_ref[2, rows, cols] = dv.astype(BF16)
            return carry

        lax.fori_loop(0, ncb, chunk, 0, unroll=HGRN_UNROLL)

    hp, wd = HGRN_HEADS_PER_STEP, HGRN_HEADS_PER_STEP * HEAD
    ngrp = N_HEADS // hp

    def colblk(group):
        return pl.BlockSpec((tb, wd), lambda h, j: (nb - 1 - j, group * ngrp + h))

    vec = pl.BlockSpec((1, wd), lambda h, j: (0, h))
    blk = pl.BlockSpec((tb, wd), lambda h, j: (nb - 1 - j, h))
    return pl.pallas_call(
        body, name=name, grid=(ngrp, nb),
        in_specs=[colblk(0), colblk(1), colblk(2), colblk(3), vec, vec, pl.BlockSpec((1, HEAD), lambda h, j: (0, 0)),
                  blk, pl.BlockSpec((hp, ncb, HEAD, HEAD), lambda h, j: (h, nb - 1 - j, 0, 0)), blk],
        out_specs=[pl.BlockSpec((4, tb, wd), lambda h, j: (0, nb - 1 - j, h)),
                   pl.BlockSpec((8, wd), lambda h, j: (0, h)), pl.BlockSpec((8, HEAD), lambda h, j: (0, 0))],
        out_shape=[jax.ShapeDtypeStruct((DPROJ_GROUPS, s, HGRN_W), BF16),
                   jax.ShapeDtypeStruct((8, HGRN_W), F32), jax.ShapeDtypeStruct((8, HEAD), F32)],
        scratch_shapes=[pltpu.VMEM((hp, HEAD, HEAD), F32)],
        compiler_params=_params("arbitrary", "arbitrary"),
    )(proj, proj, proj, proj, lb0, lb1, norm_w, o, states, dmix)


HALO_BLK = 16


def _f32(ref):
    return ref[...].astype(F32)


def _halo_prev(ref):
    return ref[...].astype(F32)[HALO_BLK - HALO:]


def _halo_next(ref):
    return ref[...].astype(F32)[:HALO]


def _conv3(x0, x1, x2, w_ref):
    y = x0 * w_ref[0:1, :]
    y = y + x1 * w_ref[1:2, :]
    return y + x2 * w_ref[2:3, :]


def _sconv_fwd(proj, w8, mix, name, tb=256):
    s = proj.shape[0]
    tb = min(tb, s)
    hb = tb // HALO_BLK

    def body(cb_ref, cc_ref, ch_ref, cch_ref, chh_ref, w_ref, mix_ref, y_ref):
        first = pl.program_id(0) == 0
        u = _f32(cc_ref) * _f32(ch_ref)
        uh = jnp.where(first, 0.0, _halo_prev(cch_ref) * _halo_prev(chh_ref))
        conv = _conv3(_shift_down(u, uh, 2), _shift_down(u, uh, 1), u, w_ref)
        y_ref[...] = (_f32(cb_ref) * conv).astype(BF16)

    def blk(g):
        return pl.BlockSpec((tb, HGRN_W), lambda j: (j, g))

    def halo(g):
        return pl.BlockSpec((HALO_BLK, HGRN_W), lambda j: (jnp.maximum(j * hb - 1, 0), g))

    return pl.pallas_call(
        body, name=name, grid=(s // tb,),
        in_specs=[blk(4), blk(5), blk(6), halo(5), halo(6), pl.BlockSpec((HALO, HGRN_W), lambda j: (0, 0)),
                  pl.BlockSpec(memory_space=pl.ANY)],
        out_specs=pl.BlockSpec((tb, HGRN_W), lambda j: (j, 1)),
        out_shape=jax.ShapeDtypeStruct(mix.shape, BF16),
        input_output_aliases={6: 0},
        compiler_params=_params("parallel"),
    )(proj, proj, proj, proj, proj, w8, mix)


def _sconv_bwd(proj, w8, dmix, dproj, name, tb=256):
    s = proj.shape[0]
    tb = min(tb, s)
    hb = tb // HALO_BLK
    nb = s // tb
    last_h = s // HALO_BLK - 1

    def body(cb_ref, cc_ref, ch_ref, cch_ref, chh_ref, cbn_ref, dy_ref, dyn_ref, w_ref, dproj_ref,
             dp_ref, gw_ref):
        j = pl.program_id(0)

        @pl.when(j == 0)
        def _():
            gw_ref[...] = jnp.zeros_like(gw_ref)

        cc, ch, cb = _f32(cc_ref), _f32(ch_ref), _f32(cb_ref)
        u = cc * ch
        uh = jnp.where(j == 0, 0.0, _halo_prev(cch_ref) * _halo_prev(chh_ref))
        u2, u1 = _shift_down(u, uh, 2), _shift_down(u, uh, 1)
        conv = _conv3(u2, u1, u, w_ref)
        dy = _f32(dy_ref)
        dp_ref[0] = (dy * conv).astype(BF16)
        dc = dy * cb
        dcn = jnp.where(j == nb - 1, 0.0, _halo_next(dyn_ref) * _halo_next(cbn_ref))
        gw_ref[0:8, :] += _rows8(dc * u2)
        gw_ref[8:16, :] += _rows8(dc * u1)
        gw_ref[16:24, :] += _rows8(dc * u)
        du = dc * w_ref[2:3, :] + _shift_up(dc, dcn, 1) * w_ref[1:2, :] + _shift_up(dc, dcn, 2) * w_ref[0:1, :]
        dp_ref[1] = (du * ch).astype(BF16)
        dp_ref[2] = (du * cc).astype(BF16)
        dp_ref[3] = jnp.zeros(dp_ref.shape[1:], BF16)

    def blk(g):
        return pl.BlockSpec((tb, HGRN_W), lambda j: (j, g))

    def halo_prev(g):
        return pl.BlockSpec((HALO_BLK, HGRN_W), lambda j: (jnp.maximum(j * hb - 1, 0), g))

    def halo_next(g):
        return pl.BlockSpec((HALO_BLK, HGRN_W), lambda j: (jnp.minimum((j + 1) * hb, last_h), g))

    return pl.pallas_call(
        body, name=name, grid=(nb,),
        in_specs=[blk(4), blk(5), blk(6), halo_prev(5), halo_prev(6), halo_next(4), blk(1), halo_next(1),
                  pl.BlockSpec((HALO, HGRN_W), lambda j: (0, 0)), pl.BlockSpec(memory_space=pl.ANY)],
        out_specs=[pl.BlockSpec((4, tb, HGRN_W), lambda j: (1, j, 0)), pl.BlockSpec((24, HGRN_W), lambda j: (0, 0))],
        out_shape=[jax.ShapeDtypeStruct(dproj.shape, BF16), jax.ShapeDtypeStruct((24, HGRN_W), F32)],
        input_output_aliases={9: 0},
        compiler_params=_params("arbitrary"),
    )(proj, proj, proj, proj, proj, proj, dmix, dmix, w8, dproj)


def _attn_fwd(q, kk, vv, name, tb=256):
    s, d = q.shape
    m = kk.shape[0]
    tb = min(tb, s)
    scale = MEM_HEAD_DIM ** -0.5

    def body(q_ref, k_ref, v_ref, o_ref):
        for hh in range(MEM_HEADS):
            cols = slice(hh * MEM_HEAD_DIM, (hh + 1) * MEM_HEAD_DIM)
            sc = _dot_nt(q_ref[:, cols], k_ref[:, cols]) * scale
            sc = sc - jnp.max(sc, axis=-1, keepdims=True)
            e = jnp.exp(sc)
            p = e / jnp.sum(e, axis=-1, keepdims=True)
            o_ref[:, cols] = _dot_nn(p, v_ref[:, cols]).astype(BF16)

    full = pl.BlockSpec((m, d), lambda i: (0, 0))
    return pl.pallas_call(
        body, name=name, grid=(s // tb,),
        in_specs=[pl.BlockSpec((tb, d), lambda i: (i, 0)), full, full],
        out_specs=pl.BlockSpec((tb, d), lambda i: (i, 0)),
        out_shape=jax.ShapeDtypeStruct((s, d), BF16),
        compiler_params=_params("parallel"),
    )(q, kk, vv)


def _attn_bwd(q, kk, vv, datt, name, tb=256):
    s, d = q.shape
    m = kk.shape[0]
    tb = min(tb, s)
    scale = MEM_HEAD_DIM ** -0.5

    def body(q_ref, k_ref, v_ref, do_ref, dq_ref, dk_ref, dv_ref):
        @pl.when(pl.program_id(0) == 0)
        def _():
            dk_ref[...] = jnp.zeros_like(dk_ref)
            dv_ref[...] = jnp.zeros_like(dv_ref)

        for hh in range(MEM_HEADS):
            cols = slice(hh * MEM_HEAD_DIM, (hh + 1) * MEM_HEAD_DIM)
            qh, kh, vh, doh = q_ref[:, cols], k_ref[:, cols], v_ref[:, cols], do_ref[:, cols]
            sc = _dot_nt(qh, kh) * scale
            sc = sc - jnp.max(sc, axis=-1, keepdims=True)
            e = jnp.exp(sc)
            p = e / jnp.sum(e, axis=-1, keepdims=True)
            dp = _dot_nt(doh, vh)
            ds = p * (dp - jnp.sum(dp * p, axis=-1, keepdims=True)) * scale
            dq_ref[:, cols] = _dot_nn(ds, kh).astype(BF16)
            dk_ref[:, cols] += _dot_tn(ds, qh)
            dv_ref[:, cols] += _dot_tn(p, doh)

    full = pl.BlockSpec((m, d), lambda i: (0, 0))
    row = pl.BlockSpec((tb, d), lambda i: (i, 0))
    return pl.pallas_call(
        body, name=name, grid=(s // tb,),
        in_specs=[row, full, full, row],
        out_specs=[row, full, full],
        out_shape=[jax.ShapeDtypeStruct((s, d), BF16), jax.ShapeDtypeStruct((m, d), F32),
                   jax.ShapeDtypeStruct((m, d), F32)],
        compiler_params=_params("arbitrary"),
    )(q, kk, vv, datt)


def _ffn_fwd(g, u, w8, bias, name, tb=512, tc=1408):
    s, f = g.shape
    tb = min(tb, s)
    tc = tc if f % tc == 0 else 512
    hb = tb // HALO_BLK

    def body(g_ref, gh_ref, u_ref, w_ref, b_ref, z_ref, a_ref):
        gv = _f32(g_ref)
        gh = jnp.where(pl.program_id(1) == 0, 0.0, _halo_prev(gh_ref))
        a = _conv3(_shift_down(gv, gh, 2), _shift_down(gv, gh, 1), gv, w_ref) + b_ref[...]
        a_ref[...] = a.astype(BF16)
        z_ref[...] = ((a * _sigmoid(a)) * _f32(u_ref)).astype(BF16)

    blk = pl.BlockSpec((tb, tc), lambda c, j: (j, c))
    return pl.pallas_call(
        body, name=name, grid=(f // tc, s // tb),
        in_specs=[blk, pl.BlockSpec((HALO_BLK, tc), lambda c, j: (jnp.maximum(j * hb - 1, 0), c)), blk,
                  pl.BlockSpec((HALO, tc), lambda c, j: (0, c)), pl.BlockSpec((1, tc), lambda c, j: (0, c))],
        out_specs=[blk, blk],
        out_shape=[jax.ShapeDtypeStruct((s, f), BF16), jax.ShapeDtypeStruct((s, f), BF16)],
        compiler_params=pltpu.CompilerParams(dimension_semantics=("parallel", "parallel"),
                                             vmem_limit_bytes=MM_VMEM_LIMIT),
    )(g, g, u, w8, bias)


def _ffn_bwd(a, g, u, dz, w8, name, tb=256, tc=1408):
    s, f = g.shape
    tb = min(tb, s)
    tc = tc if f % tc == 0 else 512
    nb = s // tb

    def body(a_ref, g_ref, u_ref, dz_ref, w_ref, dg_ref, du_ref, gb_ref, gw_ref, da_next):
        jj = pl.program_id(1)

        @pl.when(jj == 0)
        def _():
            gb_ref[...] = jnp.zeros_like(gb_ref)
            gw_ref[...] = jnp.zeros_like(gw_ref)
            da_next[...] = jnp.zeros_like(da_next)

        a = _f32(a_ref)
        sa = _sigmoid(a)
        dz = _f32(dz_ref)
        du_ref[...] = (dz * (a * sa)).astype(BF16)
        da = dz * _f32(u_ref) * (sa * (1.0 + a * (1.0 - sa)))
        gb_ref[...] += _rows8(da)
        dan = da_next[...]
        da1, da2 = _shift_up(da, dan, 1), _shift_up(da, dan, 2)
        gv = _f32(g_ref)
        gw_ref[0:8, :] += _rows8(da2 * gv)
        gw_ref[8:16, :] += _rows8(da1 * gv)
        gw_ref[16:24, :] += _rows8(da * gv)
        dg_ref[...] = (da * w_ref[2:3, :] + da1 * w_ref[1:2, :] + da2 * w_ref[0:1, :]).astype(BF16)
        da_next[...] = da[:HALO]

    blk = pl.BlockSpec((tb, tc), lambda c, jj: (nb - 1 - jj, c))
    return pl.pallas_call(
        body, name=name, grid=(f // tc, nb),
        in_specs=[blk, blk, blk, blk, pl.BlockSpec((HALO, tc), lambda c, jj: (0, c))],
        out_specs=[blk, blk, pl.BlockSpec((8, tc), lambda c, jj: (0, c)), pl.BlockSpec((24, tc), lambda c, jj: (0, c))],
        out_shape=[jax.ShapeDtypeStruct((s, f), BF16), jax.ShapeDtypeStruct((s, f), BF16),
                   jax.ShapeDtypeStruct((8, f), F32), jax.ShapeDtypeStruct((24, f), F32)],
        scratch_shapes=[pltpu.VMEM((HALO, tc), F32)],
        compiler_params=pltpu.CompilerParams(dimension_semantics=("parallel", "arbitrary"),
                                             vmem_limit_bytes=MM_VMEM_LIMIT),
    )(a, g, u, dz, w8)


def _window(ref, axis, slot, size):
    start = pl.multiple_of(slot * size, size)
    if axis == 0:
        return ref.at[pl.ds(start, size), :]
    return ref.at[:, pl.ds(start, size)]


def _chip_peers():
    x, y, c = lax.axis_index("x"), lax.axis_index("y"), lax.axis_index("c")
    peers = [(1 - x, y, c), (x, 1 - y, c), (1 - x, 1 - y, c)]
    slots = [2 * (1 - x) + y, 2 * x + (1 - y), 2 * (1 - x) + (1 - y)]
    return 2 * x + y, peers, slots


HBM_SPEC = pl.BlockSpec(memory_space=pltpu.HBM)
SEM_SPEC = pl.BlockSpec(memory_space=pltpu.SEMAPHORE)
EFFECT = pltpu.SideEffectType.DATAFLOW_SIDE_EFFECTING


def _hbm(a):
    return pltpu.with_memory_space_constraint(a, pltpu.HBM)


def _cast_into_full(x, axis, slot_arr, dtype, name, after=None):
    r, c = x.shape
    tr = _row_tile(r, 256)
    nb = r // tr
    full = (r * N_CHIPS, c) if axis == 0 else (r, c * N_CHIPS)

    def body(slot_ref, x_ref, *rest):
        rest[-1][...] = x_ref[...].astype(dtype)

    if axis == 0:
        out_map = lambda i, s: (s[0] * nb + i, 0)
    else:
        out_map = lambda i, s: (i, s[0])
    extra = [] if after is None else [after]
    return pl.pallas_call(
        body, name=name,
        grid_spec=pltpu.PrefetchScalarGridSpec(
            num_scalar_prefetch=1, grid=(nb,),
            in_specs=[pl.BlockSpec((tr, c), lambda i, s: (i, 0))] + [pl.BlockSpec(memory_space=pl.ANY)] * len(extra),
            out_specs=pl.BlockSpec((tr, c), out_map)),
        out_shape=jax.ShapeDtypeStruct(full, dtype),
        compiler_params=_params("parallel"),
    )(slot_arr, x, *extra)


def _piece(ref, axis, slot, half):
    size = ref.shape[axis] // N_CHIPS
    if half is None:
        return _window(ref, axis, slot, size)
    if axis == 0:
        h = size // 2
        return ref.at[pl.ds(pl.multiple_of(slot * size + half * h, h), h), :]
    h = ref.shape[0] // 2
    return ref.at[pl.ds(pl.multiple_of(half * h, h), h), pl.ds(pl.multiple_of(slot * size, size), size)]


def _gather_start(fulls, axes, split, groups, name):
    n, ng = len(fulls), len(groups)

    def body(*refs):
        outs = refs[n:]
        sems = outs[:2 * ng]
        thru = outs[2 * ng:2 * ng + n]
        token = outs[-1]
        slot, peers, _ = _chip_peers()
        c = lax.axis_index("c")
        for g, members in enumerate(groups):
            for i, t in enumerate(members):
                mine = _piece(thru[t], axes[t], slot, c if split[t] else None)
                for k in range(3):
                    pltpu.make_async_remote_copy(
                        src_ref=mine, dst_ref=mine, send_sem=sems[2 * g].at[3 * i + k],
                        recv_sem=sems[2 * g + 1].at[3 * i + k], device_id=peers[k], device_id_type=MESH).start()
        token[...] = jnp.zeros_like(token)

    sem_shapes = []
    for members in groups:
        sem_shapes += [pltpu.SemaphoreType.DMA((3 * len(members),))] * 2
    res = pl.pallas_call(
        body, name=name,
        in_specs=[HBM_SPEC] * n,
        out_specs=[SEM_SPEC] * (2 * ng) + [HBM_SPEC] * n + [pl.BlockSpec(memory_space=pltpu.VMEM)],
        out_shape=sem_shapes + [pltpu.HBM(f.shape, f.dtype) for f in fulls] + [jax.ShapeDtypeStruct((8, 128), F32)],
        input_output_aliases={t: 2 * ng + t for t in range(n)},
        compiler_params=pltpu.CompilerParams(has_side_effects=EFFECT),
    )(*[_hbm(f) for f in fulls])
    sems = [(res[2 * g], res[2 * g + 1]) for g in range(ng)]
    return sems, list(res[2 * ng:2 * ng + n]), res[-1]


def _gather_relay(fulls, axes, split, sems, after, name):
    n = len(fulls)
    nsplit = sum(split)

    def body(*refs):
        send_sems, recv_sems = refs[n], refs[n + 1]
        outs = refs[n + 3:]
        d_send, d_recv = outs[0], outs[1]
        thru = outs[2:2 + n]
        token = outs[-1]
        slot, peers, slots = _chip_peers()
        c = lax.axis_index("c")
        sibling = (lax.axis_index("x"), lax.axis_index("y"), 1 - c)
        for t in range(n):
            half = c if split[t] else None
            for k in range(3):
                cp = pltpu.make_async_remote_copy(
                    src_ref=_piece(thru[t], axes[t], slot, half), dst_ref=_piece(thru[t], axes[t], slots[k], half),
                    send_sem=send_sems.at[3 * t + k], recv_sem=recv_sems.at[3 * t + k],
                    device_id=peers[k], device_id_type=MESH)
                cp.wait_send()
                cp.wait_recv()
        i = 0
        for t in range(n):
            if not split[t]:
                continue
            for k in range(3):
                got = _piece(thru[t], axes[t], slots[k], c)
                pltpu.make_async_remote_copy(
                    src_ref=got, dst_ref=got, send_sem=d_send.at[3 * i + k], recv_sem=d_recv.at[3 * i + k],
                    device_id=sibling, device_id_type=MESH).start()
            i += 1
        token[...] = jnp.zeros_like(token)

    res = pl.pallas_call(
        body, name=name,
        in_specs=[HBM_SPEC] * n + [SEM_SPEC, SEM_SPEC, pl.BlockSpec(memory_space=pl.ANY)],
        out_specs=[SEM_SPEC, SEM_SPEC] + [HBM_SPEC] * n + [pl.BlockSpec(memory_space=pltpu.VMEM)],
        out_shape=[pltpu.SemaphoreType.DMA((3 * nsplit,)), pltpu.SemaphoreType.DMA((3 * nsplit,))]
        + [pltpu.HBM(f.shape, f.dtype) for f in fulls] + [jax.ShapeDtypeStruct((8, 128), F32)],
        input_output_aliases={t: 2 + t for t in range(n)},
        compiler_params=pltpu.CompilerParams(has_side_effects=EFFECT),
    )(*fulls, sems[0], sems[1], after)
    return (res[0], res[1]), list(res[2:2 + n]), res[-1]


def _gather_finish(fulls, axes, split, sems, after, name):
    n = len(fulls)

    def body(*refs):
        d_send, d_recv = refs[n], refs[n + 1]
        thru = refs[n + 3:]
        _, _, slots = _chip_peers()
        c = lax.axis_index("c")
        sibling = (lax.axis_index("x"), lax.axis_index("y"), 1 - c)
        i = 0
        for t in range(n):
            if not split[t]:
                continue
            for k in range(3):
                cp = pltpu.make_async_remote_copy(
                    src_ref=_piece(thru[t], axes[t], slots[k], c), dst_ref=_piece(thru[t], axes[t], slots[k], 1 - c),
                    send_sem=d_send.at[3 * i + k], recv_sem=d_recv.at[3 * i + k],
                    device_id=sibling, device_id_type=MESH)
                cp.wait_send()
                cp.wait_recv()
            i += 1

    return pl.pallas_call(
        body, name=name,
        in_specs=[HBM_SPEC] * n + [SEM_SPEC, SEM_SPEC, pl.BlockSpec(memory_space=pl.ANY)],
        out_specs=[HBM_SPEC] * n,
        out_shape=[pltpu.HBM(f.shape, f.dtype) for f in fulls],
        input_output_aliases={t: t for t in range(n)},
        compiler_params=pltpu.CompilerParams(has_side_effects=EFFECT),
    )(*fulls, sems[0], sems[1], after)


def _scatter_start(grads_bf16, axes, name):
    n = len(grads_bf16)

    def shard_shape(g, ax):
        return (g.shape[0] // N_CHIPS, g.shape[1]) if ax == 0 else (g.shape[0], g.shape[1] // N_CHIPS)

    shapes = [shard_shape(g, ax) for g, ax in zip(grads_bf16, axes)]

    def body(*refs):
        outs = refs[2 * n:]
        send_sems, recv_sems = outs[0], outs[1]
        gb, land = outs[2:2 + n], outs[2 + n:2 + 2 * n]
        token = outs[-1]
        _, peers, slots = _chip_peers()
        for t in range(n):
            size = shapes[t][axes[t]]
            for k in range(3):
                pltpu.make_async_remote_copy(
                    src_ref=_window(gb[t], axes[t], slots[k], size), dst_ref=land[t].at[k],
                    send_sem=send_sems.at[3 * t + k], recv_sem=recv_sems.at[3 * t + k],
                    device_id=peers[k], device_id_type=MESH).start()
        token[...] = jnp.zeros_like(token)

    lands = [_hbm(lax.empty((3,) + sh, BF16)) for sh in shapes]
    res = pl.pallas_call(
        body, name=name,
        in_specs=[HBM_SPEC] * (2 * n),
        out_specs=[SEM_SPEC, SEM_SPEC] + [HBM_SPEC] * (2 * n) + [pl.BlockSpec(memory_space=pltpu.VMEM)],
        out_shape=[pltpu.SemaphoreType.DMA((3 * n,)), pltpu.SemaphoreType.DMA((3 * n,))]
        + [pltpu.HBM(g.shape, g.dtype) for g in grads_bf16] + [pltpu.HBM((3,) + sh, BF16) for sh in shapes]
        + [jax.ShapeDtypeStruct((8, 128), F32)],
        input_output_aliases={t: 2 + t for t in range(2 * n)},
        compiler_params=pltpu.CompilerParams(has_side_effects=EFFECT),
    )(*[_hbm(g) for g in grads_bf16], *lands)
    return (res[0], res[1]), list(res[2:2 + n]), list(res[2 + n:2 + 2 * n]), res[-1]


def _scatter_wait(grads_thru, lands_thru, axes, sems, after, name):
    n = len(grads_thru)

    def body(*refs):
        send_sems, recv_sems = refs[2 * n], refs[2 * n + 1]
        outs = refs[2 * n + 3:]
        gb, land = outs[:n], outs[n:]
        _, peers, slots = _chip_peers()
        for t in range(n):
            size = land[t].shape[1 + axes[t]]
            for k in range(3):
                cp = pltpu.make_async_remote_copy(
                    src_ref=_window(gb[t], axes[t], slots[k], size), dst_ref=land[t].at[k],
                    send_sem=send_sems.at[3 * t + k], recv_sem=recv_sems.at[3 * t + k],
                    device_id=peers[k], device_id_type=MESH)
                cp.wait_send()
                cp.wait_recv()

    res = pl.pallas_call(
        body, name=name,
        in_specs=[HBM_SPEC] * (2 * n) + [SEM_SPEC, SEM_SPEC, pl.BlockSpec(memory_space=pl.ANY)],
        out_specs=[HBM_SPEC] * (2 * n),
        out_shape=[pltpu.HBM(g.shape, g.dtype) for g in grads_thru] + [pltpu.HBM(l.shape, l.dtype) for l in lands_thru],
        input_output_aliases={t: t for t in range(2 * n)},
        compiler_params=pltpu.CompilerParams(has_side_effects=EFFECT),
    )(*grads_thru, *lands_thru, sems[0], sems[1], after)
    return list(res[n:])


def _sibling_start(arrs, name):
    n = len(arrs)

    def body(*refs):
        outs = refs[2 * n:]
        send_sems, recv_sems = outs[0], outs[1]
        src, land = outs[2:2 + n], outs[2 + n:2 + 2 * n]
        token = outs[-1]
        sibling = (lax.axis_index("x"), lax.axis_index("y"), 1 - lax.axis_index("c"))
        for t in range(n):
            pltpu.make_async_remote_copy(
                src_ref=src[t], dst_ref=land[t], send_sem=send_sems.at[t], recv_sem=recv_sems.at[t],
                device_id=sibling, device_id_type=MESH).start()
        token[...] = jnp.zeros_like(token)

    lands = [_hbm(lax.empty(a.shape, a.dtype)) for a in arrs]
    res = pl.pallas_call(
        body, name=name,
        in_specs=[HBM_SPEC] * (2 * n),
        out_specs=[SEM_SPEC, SEM_SPEC] + [HBM_SPEC] * (2 * n) + [pl.BlockSpec(memory_space=pltpu.VMEM)],
        out_shape=[pltpu.SemaphoreType.DMA((n,)), pltpu.SemaphoreType.DMA((n,))]
        + [pltpu.HBM(a.shape, a.dtype) for a in arrs] * 2 + [jax.ShapeDtypeStruct((8, 128), F32)],
        input_output_aliases={t: 2 + t for t in range(2 * n)},
        compiler_params=pltpu.CompilerParams(has_side_effects=EFFECT),
    )(*[_hbm(a) for a in arrs], *lands)
    return (res[0], res[1]), list(res[2:2 + n]), list(res[2 + n:2 + 2 * n]), res[-1]


def _sibling_wait(src_thru, lands_thru, sems, after, name):
    n = len(src_thru)

    def body(*refs):
        send_sems, recv_sems = refs[2 * n], refs[2 * n + 1]
        outs = refs[2 * n + 3:]
        src, land = outs[:n], outs[n:]
        sibling = (lax.axis_index("x"), lax.axis_index("y"), 1 - lax.axis_index("c"))
        for t in range(n):
            cp = pltpu.make_async_remote_copy(
                src_ref=src[t], dst_ref=land[t], send_sem=send_sems.at[t], recv_sem=recv_sems.at[t],
                device_id=sibling, device_id_type=MESH)
            cp.wait_send()
            cp.wait_recv()

    res = pl.pallas_call(
        body, name=name,
        in_specs=[HBM_SPEC] * (2 * n) + [SEM_SPEC, SEM_SPEC, pl.BlockSpec(memory_space=pl.ANY)],
        out_specs=[HBM_SPEC] * (2 * n),
        out_shape=[pltpu.HBM(a.shape, a.dtype) for a in src_thru] * 2,
        input_output_aliases={t: t for t in range(2 * n)},
        compiler_params=pltpu.CompilerParams(has_side_effects=EFFECT),
    )(*src_thru, *lands_thru, sems[0], sems[1], after)
    return list(res[:n]), list(res[n:])


def _all_reduce_small(packed, name):
    nc = packed.shape[1]
    vmem = pl.BlockSpec(memory_space=pltpu.VMEM)

    def body(in_ref, out_ref, gbuf, send_sems, recv_sems):
        x, y, c = lax.axis_index("x"), lax.axis_index("y"), lax.axis_index("c")
        me = 4 * x + 2 * y + c
        gbuf[me] = jnp.sum(in_ref[...], axis=0, keepdims=True)
        copies = []
        for k in range(1, 8):
            peer = (x ^ ((k >> 2) & 1), y ^ ((k >> 1) & 1), c ^ (k & 1))
            rc = pltpu.make_async_remote_copy(
                src_ref=gbuf.at[me], dst_ref=gbuf.at[me], send_sem=send_sems.at[k - 1], recv_sem=recv_sems.at[k - 1],
                device_id=peer, device_id_type=MESH)
            rc.start()
            copies.append(rc)
        for k in range(1, 8):
            peer = (x ^ ((k >> 2) & 1), y ^ ((k >> 1) & 1), c ^ (k & 1))
            pltpu.make_async_remote_copy(
                src_ref=gbuf.at[me], dst_ref=gbuf.at[me ^ k], send_sem=send_sems.at[k - 1],
                recv_sem=recv_sems.at[k - 1], device_id=peer, device_id_type=MESH).wait_recv()
        for rc in copies:
            rc.wait_send()
        tot = gbuf[0]
        for d in range(1, 8):
            tot = tot + gbuf[d]
        out_ref[...] = tot

    return pl.pallas_call(
        body, name=name,
        in_specs=[vmem], out_specs=vmem,
        out_shape=jax.ShapeDtypeStruct((1, nc), F32),
        scratch_shapes=[pltpu.VMEM((8, 1, nc), F32), pltpu.SemaphoreType.DMA((7,)), pltpu.SemaphoreType.DMA((7,))],
    )(packed)


def _sum4(g_full, axis, slot_arr, recv, name):
    _, r, c = recv.shape
    tr = min(r, 128)
    nb = r // tr

    def body(slot_ref, own_ref, recv_ref, o_ref):
        acc = own_ref[...]
        for k in range(3):
            acc = acc + recv_ref[k].astype(F32)
        o_ref[...] = acc

    if axis == 0:
        own_map = lambda i, s: (s[0] * nb + i, 0)
    else:
        own_map = lambda i, s: (i, s[0])
    return pl.pallas_call(
        body, name=name,
        grid_spec=pltpu.PrefetchScalarGridSpec(
            num_scalar_prefetch=1, grid=(nb,),
            in_specs=[pl.BlockSpec((tr, c), own_map), pl.BlockSpec((3, tr, c), lambda i, s: (0, i, 0))],
            out_specs=pl.BlockSpec((tr, c), lambda i, s: (i, 0))),
        out_shape=jax.ShapeDtypeStruct((r, c), F32),
        compiler_params=_params("parallel"),
    )(slot_arr, g_full, recv)


def _adamw(w, g_parts, m, v, name):
    r, c = w.shape
    tr = r if r % 128 else _row_tile(r, 256)
    npart = len(g_parts)

    def body(*refs):
        w_ref = refs[0]
        g_refs = refs[1:1 + npart]
        m_ref, v_ref, g_out, d_out, m_out, v_out = refs[1 + npart:]
        g = g_refs[0][...]
        for gr in g_refs[1:]:
            g = g + gr[...]
        mm = ADAM_B1 * m_ref[...] + (1.0 - ADAM_B1) * g
        vv = ADAM_B2 * v_ref[...] + (1.0 - ADAM_B2) * (g * g)
        m_hat = mm / (1.0 - ADAM_B1 ** ADAM_STEP)
        v_hat = vv / (1.0 - ADAM_B2 ** ADAM_STEP)
        g_out[...] = g
        d_out[...] = -ADAM_LR * (m_hat / (jnp.sqrt(v_hat) + ADAM_EPS) + ADAM_WD * w_ref[...])
        m_out[...] = mm
        v_out[...] = vv

    blk = pl.BlockSpec((tr, c), lambda i: (i, 0))
    shp = jax.ShapeDtypeStruct((r, c), F32)
    return pl.pallas_call(
        body, name=name, grid=(r // tr,),
        in_specs=[blk] * (3 + npart), out_specs=[blk] * 4, out_shape=[shp] * 4,
        compiler_params=pltpu.CompilerParams(dimension_semantics=("parallel",), vmem_limit_bytes=MM_VMEM_LIMIT),
    )(w, *g_parts, m, v)


def _pad_rows8(w):
    return jnp.pad(w, ((0, HALO - w.shape[0]), (0, 0)))


def kernel(x, mem, hgrn_lb, norm1_w, w_in, hgrn_norm_w, sconv_w, w_out, norm2_w, mem_norm_w, wq, wk, wv, wo, norm3_w, w_gate, w_up, ffn_conv_w, ffn_conv_b, w_down, final_norm_w, loss_target, m_hgrn_lb, m_norm1_w, m_w_in, m_hgrn_norm_w, m_sconv_w, m_w_out, m_norm2_w, m_mem_norm_w, m_wq, m_wk, m_wv, m_wo, m_norm3_w, m_w_gate, m_w_up, m_ffn_conv_w, m_ffn_conv_b, m_w_down, m_final_norm_w, v_hgrn_lb, v_norm1_w, v_w_in, v_hgrn_norm_w, v_sconv_w, v_w_out, v_norm2_w, v_mem_norm_w, v_wq, v_wk, v_wv, v_wo, v_norm3_w, v_w_gate, v_w_up, v_ffn_conv_w, v_ffn_conv_b, v_w_down, v_final_norm_w):
    xs, mems, tgt = x[0], mem[0], loss_target[0]
    d = xs.shape[1]
    fnw = final_norm_w.reshape(1, d)

    big = {"w_in": (w_in[0], 1), "w_out": (w_out[0], 0), "wq": (wq[0], 0), "wk": (wk[0], 0), "wv": (wv[0], 0),
           "wo": (wo[0], 0), "w_gate": (w_gate[0], 1), "w_up": (w_up[0], 1), "w_down": (w_down[0], 0)}
    names = list(big)
    slot_arr = (2 * lax.axis_index("x") + lax.axis_index("y")).astype(jnp.int32).reshape(1)
    gnames = names + ["sconv8", "fconv8"]
    axes = [big[n][1] for n in names] + [1, 1]
    groups = [["w_in"], ["w_out", "sconv8"], ["wq", "wk", "wv", "wo"], ["w_gate", "w_up", "fconv8", "w_down"]]
    gidx = [[gnames.index(n) for n in grp] for grp in groups]
    split = [True] * len(names) + [False, False]
    first = _cast_into_full(big["w_in"][0], 1, slot_arr, BF16, "cast_w_in")
    sems0, first, tok0 = _gather_start([first], [1], [True], [[0]], "gather_start_w_in")
    rest = [_cast_into_full(big[n][0], big[n][1], slot_arr, BF16, "cast_" + n, after=tok0) for n in names[1:]]
    rest += [_cast_into_full(_pad_rows8(sconv_w[0]), 1, slot_arr, F32, "cast_sconv_w", after=tok0),
             _cast_into_full(_pad_rows8(ffn_conv_w[0]), 1, slot_arr, F32, "cast_ffn_conv_w", after=tok0)]
    sems1, rest, tok = _gather_start(rest, axes[1:], split[1:], [[t - 1 for t in idx] for idx in gidx[1:]],
                                     "gather_start")
    gsems, fulls = sems0 + sems1, first + rest
    wf, relayed = {}, {}

    def gather_relay(g, after):
        idx = gidx[g]
        dsems, arrs, token = _gather_relay([fulls[t] for t in idx], [axes[t] for t in idx], [split[t] for t in idx],
                                           gsems[g], after, "gather_relay_%d" % g)
        relayed[g] = (dsems, arrs)
        return token[0:1, 0:1]

    def gather_finish(g, after):
        idx = gidx[g]
        dsems, arrs = relayed[g]
        got = _gather_finish(arrs, [axes[t] for t in idx], [split[t] for t in idx], dsems, after,
                             "gather_finish_%d" % g)
        wf.update(zip(groups[g], got))

    lb0, lb1 = hgrn_lb[0:1], hgrn_lb[1:2]

    h1 = _rmsnorm_fwd(xs, norm1_w + tok[0:1, 0:1], "norm1")
    gather_relay(0, h1)
    gather_finish(0, h1)
    proj = _matmul(h1, wf["w_in"], "nn", "proj_in", out_dtype=BF16)
    t1 = gather_relay(1, proj)
    o_h, og, states = _hgrn_fwd(proj, lb0, lb1, hgrn_norm_w + t1, "hgrn_fwd")
    gather_finish(1, o_h)
    t2 = gather_relay(2, o_h)
    sconv8 = wf["sconv8"]
    mix = _sconv_fwd(proj, sconv8, og, "sconv_fwd")
    x1 = _matmul(mix, wf["w_out"], "nn", "proj_out", residual=xs)
    t3 = gather_relay(3, x1)
    h2 = _rmsnorm_fwd(x1, norm2_w + (t2 + t3), "norm2")
    mem_n = _rmsnorm_fwd(mems, mem_norm_w, "norm_mem")
    gather_finish(2, h2)
    qa = _matmul(h2, wf["wq"], "nn", "attn_q", out_dtype=BF16)
    ka = _matmul(mem_n, wf["wk"], "nn", "attn_k", out_dtype=BF16)
    va = _matmul(mem_n, wf["wv"], "nn", "attn_v", out_dtype=BF16)
    att = _attn_fwd(qa, ka, va, "attn_fwd")
    x2 = _matmul(att, wf["wo"], "nn", "attn_o", residual=x1)
    h3 = _rmsnorm_fwd(x2, norm3_w, "norm3")
    gather_finish(3, h3)
    fconv8 = wf["fconv8"]
    gate = _matmul(h3, wf["w_gate"], "nn", "ffn_gate", out_dtype=BF16)
    up = _matmul(h3, wf["w_up"], "nn", "ffn_up", out_dtype=BF16)
    z, act = _ffn_fwd(gate, up, fconv8, ffn_conv_b, "ffn_act")
    x3 = _matmul(z, wf["w_down"], "nn", "ffn_down", residual=x2)

    dx3, dx3b, g_final, loss8 = _final_loss_bwd(x3, tgt, fnw, "loss_bwd")
    gw = {}
    dz = _matmul(dx3b, wf["w_down"], "nt", "d_z", out_dtype=BF16)
    gw["w_down"] = _matmul(z, dx3b, "tn", "g_w_down", extra_bf16=True)
    dgate, du, g_fb, g_fw = _ffn_bwd(act, gate, up, dz, fconv8, "ffn_act_bwd")
    dh3 = _matmul(dgate, wf["w_gate"], "nt", "d_h3_gate")
    dh3 = _matmul(du, wf["w_up"], "nt", "d_h3_up", residual=dh3, out_dtype=BF16)
    gw["w_gate"] = _matmul(h3, dgate, "tn", "g_w_gate", extra_bf16=True)
    gw["w_up"] = _matmul(h3, du, "tn", "g_w_up", extra_bf16=True)
    pending = []

    def scatter_start(grp):
        sems, g_thru, lands, token = _scatter_start([gw[n][1] for n in grp], [big[n][1] for n in grp],
                                                    "scatter_start_" + grp[0])
        pending.append((grp, sems, g_thru, lands))
        return token[0:1, 0:1]

    tok1 = scatter_start(["w_down", "w_gate", "w_up"])
    dx2, dx2b, g_n3 = _rmsnorm_bwd(dh3, x2, norm3_w + tok1, dx3, "norm3_bwd")
    datt = _matmul(dx2b, wf["wo"], "nt", "d_att", out_dtype=BF16)
    gw["wo"] = _matmul(att, dx2b, "tn", "g_wo", extra_bf16=True)
    dqa, dka, dva = _attn_bwd(qa, ka, va, datt, "attn_bwd")
    dh2 = _matmul(dqa, wf["wq"], "nt", "d_h2", out_dtype=BF16)
    gw["wq"] = _matmul(h2, dqa, "tn", "g_wq", extra_bf16=True)
    gw["wk"] = _matmul(mem_n, dka, "tn", "g_wk", extra_bf16=True)
    gw["wv"] = _matmul(mem_n, dva, "tn", "g_wv", extra_bf16=True)
    tok2 = scatter_start(["wo", "wq", "wk", "wv"])
    dmem_n = _matmul(dka, wf["wk"], "nt", "d_memn_k")
    dmem_n = _matmul(dva, wf["wv"], "nt", "d_memn_v", residual=dmem_n)
    _, _, g_nm = _rmsnorm_bwd(dmem_n, mems, mem_norm_w, None, "norm_mem_bwd")
    dx1, dx1b, g_n2 = _rmsnorm_bwd(dh2, x1, norm2_w + tok2, dx2, "norm2_bwd")
    dmix = _matmul(dx1b, wf["w_out"], "nt", "d_mix", out_dtype=BF16)
    gw["w_out"] = _matmul(mix, dx1b, "tn", "g_w_out", extra_bf16=True)
    tok3 = scatter_start(["w_out"])
    dproj, g_lb, g_hn = _hgrn_bwd(proj, lb0, lb1, hgrn_norm_w + tok3, o_h, states, dmix, "hgrn_bwd")
    dproj, g_sw = _sconv_bwd(proj, sconv8, dmix, dproj, "sconv_bwd")
    gw["w_in"] = _matmul(h1, dproj, "tn", "g_w_in", extra_bf16=True, groups=7)
    tok4 = scatter_start(["w_in"])
    dh1 = _matmul(dproj, wf["w_in"], "nt", "d_h1", out_dtype=BF16, groups=7)
    dx, _, g_n1 = _rmsnorm_bwd(dh1, xs, norm1_w + tok4, dx1, "norm1_bwd")

    small = [g_n1, g_n2, g_n3, g_final, g_nm, g_lb, g_hn, g_fb,
             g_sw[0:8], g_sw[8:16], g_sw[16:24], g_fw[0:8], g_fw[8:16], g_fw[16:24], loss8]
    widths = [a.shape[1] for a in small]
    tot = _all_reduce_small(jnp.concatenate(small, axis=1), "all_reduce_small")
    offs = [0]
    for wd_ in widths:
        offs.append(offs[-1] + wd_)
    sm = [tot[:, offs[i]:offs[i + 1]] for i in range(len(small))]
    s_n1, s_n2, s_n3, s_final, s_nm, s_lb, s_hn, s_fb = sm[:8]
    s_sw = jnp.concatenate(sm[8:11], axis=0)
    s_fw = jnp.concatenate(sm[11:14], axis=0)
    loss = sm[14][0, 0]
    slot = 2 * lax.axis_index("x") + lax.axis_index("y")
    s_sw = lax.dynamic_slice_in_dim(s_sw, slot * (HGRN_W // N_CHIPS), HGRN_W // N_CHIPS, axis=1)
    fsh = ffn_conv_w.shape[2]
    s_fw = lax.dynamic_slice_in_dim(s_fw, slot * fsh, fsh, axis=1)
    s_lb2 = jnp.concatenate([s_lb, -s_lb], axis=0)

    swaps = []
    after = tot
    for grp, sems, g_thru, lands in pending:
        got = _scatter_wait(g_thru, lands, [big[n][1] for n in grp], sems, after, "scatter_wait_" + grp[0])
        sums = [_sum4(gw[n][0], big[n][1], slot_arr, r, "core_sum_" + n) for n, r in zip(grp, got)]
        ssems, s_thru, s_lands, after = _sibling_start(sums, "sibling_start_" + grp[0])
        swaps.append((grp, ssems, s_thru, s_lands))

    moments = {"hgrn_lb": (m_hgrn_lb, v_hgrn_lb), "norm1_w": (m_norm1_w, v_norm1_w), "w_in": (m_w_in, v_w_in),
               "hgrn_norm_w": (m_hgrn_norm_w, v_hgrn_norm_w), "sconv_w": (m_sconv_w, v_sconv_w),
               "w_out": (m_w_out, v_w_out), "norm2_w": (m_norm2_w, v_norm2_w),
               "mem_norm_w": (m_mem_norm_w, v_mem_norm_w), "wq": (m_wq, v_wq), "wk": (m_wk, v_wk), "wv": (m_wv, v_wv),
               "wo": (m_wo, v_wo), "norm3_w": (m_norm3_w, v_norm3_w), "w_gate": (m_w_gate, v_w_gate),
               "w_up": (m_w_up, v_w_up), "ffn_conv_w": (m_ffn_conv_w, v_ffn_conv_w),
               "ffn_conv_b": (m_ffn_conv_b, v_ffn_conv_b), "w_down": (m_w_down, v_w_down),
               "final_norm_w": (m_final_norm_w, v_final_norm_w)}
    weights = {"hgrn_lb": hgrn_lb, "norm1_w": norm1_w, "w_in": w_in, "hgrn_norm_w": hgrn_norm_w, "sconv_w": sconv_w,
               "w_out": w_out, "norm2_w": norm2_w, "mem_norm_w": mem_norm_w, "wq": wq, "wk": wk, "wv": wv, "wo": wo,
               "norm3_w": norm3_w, "w_gate": w_gate, "w_up": w_up, "ffn_conv_w": ffn_conv_w, "ffn_conv_b": ffn_conv_b,
               "w_down": w_down, "final_norm_w": final_norm_w}
    small_g = {"hgrn_lb": s_lb2, "norm1_w": s_n1, "hgrn_norm_w": s_hn, "sconv_w": s_sw, "norm2_w": s_n2,
               "mem_norm_w": s_nm, "norm3_w": s_n3, "ffn_conv_w": s_fw, "ffn_conv_b": s_fb, "final_norm_w": s_final}
    order = list(weights)
    res = {}

    def adamw(n, parts):
        shape = weights[n].shape
        w2 = weights[n].reshape((-1, shape[-1]))
        m2, v2 = (t.reshape(w2.shape) for t in moments[n])
        res[n] = [t.reshape(shape) for t in _adamw(w2, [p.reshape(w2.shape) for p in parts], m2, v2, "adamw_" + n)]

    for n in order:
        if n not in big:
            adamw(n, [small_g[n]])
    after = after + res["final_norm_w"][1][0]
    for grp, ssems, s_thru, s_lands in swaps:
        own, other = _sibling_wait(s_thru, s_lands, ssems, after, "sibling_wait_" + grp[0])
        for n, a, b in zip(grp, own, other):
            adamw(n, [a, b])
        after = res[grp[-1]][1]

    return (loss, dx[None], *[res[n][0] for n in order], *[res[n][1] for n in order],
            *[res[n][2] for n in order], *[res[n][3] for n in order])
```

```python
import functools

import jax
import jax.numpy as jnp
from jax import lax
from jax.experimental import pallas as pl
from jax.experimental.pallas import tpu as pltpu

F32 = jnp.float32
BF16 = jnp.bfloat16
MESH = pl.DeviceIdType.MESH

EPS = 1e-6
HGRN_W = 1024
HEAD = 128
N_HEADS = 8
CHUNK = 64
HGRN_UNROLL = 8
HGRN_HEADS_PER_STEP = 4
DPROJ_GROUPS = 8
MEM_HEADS = 4
MEM_HEAD_DIM = 512
N_CHIPS = 4
HALO = 8

ADAM_LR = 0.001
ADAM_B1 = 0.9
ADAM_B2 = 0.999
ADAM_EPS = 1e-08
ADAM_WD = 0.01
ADAM_STEP = 10


def _sigmoid(x):
    return 1.0 / (1.0 + jnp.exp(-x))


def _dot(a, b, dims):
    return lax.dot_general(a.astype(BF16), b.astype(BF16), (dims, ((), ())),
                           preferred_element_type=F32)


def _dot_nn(a, b):
    return _dot(a, b, ((1,), (0,)))


def _dot_nt(a, b):
    return _dot(a, b, ((1,), (1,)))


def _dot_tn(a, b):
    return _dot(a, b, ((0,), (0,)))


def _hdot(a, b, dims):
    return lax.dot_general(a, b, (dims, ((), ())), precision=lax.Precision.HIGH, preferred_element_type=F32)


def _hdot_nn(a, b):
    return _hdot(a, b, ((1,), (0,)))


def _hdot_nt(a, b):
    return _hdot(a, b, ((1,), (1,)))


def _hdot_tn(a, b):
    return _hdot(a, b, ((0,), (0,)))


def _exact_ones_dot(ones_bf16, x):
    hi = x.astype(BF16)
    r1 = x - hi.astype(F32)
    mid = r1.astype(BF16)
    lo = (r1 - mid.astype(F32)).astype(BF16)
    dims = (((1,), (0,)), ((), ()))
    return (lax.dot_general(ones_bf16, hi, dims, preferred_element_type=F32)
            + lax.dot_general(ones_bf16, mid, dims, preferred_element_type=F32)
            + lax.dot_general(ones_bf16, lo, dims, preferred_element_type=F32))


def _rows8(v):
    t, c = v.shape
    return v.reshape(t // 8, 8, c).sum(axis=0)


def _shift_down(x, halo, s):
    rolled = pltpu.roll(x, s, 0)
    hrolled = pltpu.roll(halo, s, 0)
    row = lax.broadcasted_iota(jnp.int32, hrolled.shape, 0)
    head = jnp.where(row < s, hrolled, rolled[:HALO])
    return jnp.concatenate([head, rolled[HALO:]], axis=0)


def _shift_up(x, halo, s):
    t = x.shape[0]
    rolled = pltpu.roll(x, t - s, 0)
    hrolled = pltpu.roll(halo, HALO - s, 0)
    row = lax.broadcasted_iota(jnp.int32, hrolled.shape, 0)
    tail = jnp.where(row >= HALO - s, hrolled, rolled[t - HALO:])
    return jnp.concatenate([rolled[:t - HALO], tail], axis=0)


def _params(*sem):
    return pltpu.CompilerParams(dimension_semantics=sem)


def _row_tile(r, pref):
    while r % pref:
        pref //= 2
    return pref


def _rmsnorm_fwd(x, w, name, tm=256):
    s, d = x.shape
    tm = min(tm, s)

    def body(x_ref, w_ref, o_ref):
        xv = x_ref[...]
        r = lax.rsqrt(jnp.mean(xv * xv, axis=-1, keepdims=True) + EPS)
        o_ref[...] = ((xv * r) * w_ref[...]).astype(BF16)

    return pl.pallas_call(
        body, name=name, grid=(s // tm,),
        in_specs=[pl.BlockSpec((tm, d), lambda i: (i, 0)), pl.BlockSpec((1, d), lambda i: (0, 0))],
        out_specs=pl.BlockSpec((tm, d), lambda i: (i, 0)),
        out_shape=jax.ShapeDtypeStruct((s, d), BF16),
        compiler_params=_params("parallel"),
    )(x, w)


def _rmsnorm_bwd(dh, x, w, dres, name, tm=256):
    s, d = x.shape
    tm = min(tm, s)
    has_res = dres is not None

    def body(*refs):
        if has_res:
            dh_ref, x_ref, w_ref, dres_ref, dx_ref, dxb_ref, gw_ref = refs
        else:
            dh_ref, x_ref, w_ref, dx_ref, dxb_ref, gw_ref = refs

        @pl.when(pl.program_id(0) == 0)
        def _():
            gw_ref[...] = jnp.zeros_like(gw_ref)

        xv = x_ref[...]
        dhv = dh_ref[...].astype(F32)
        r = lax.rsqrt(jnp.mean(xv * xv, axis=-1, keepdims=True) + EPS)
        xhat = xv * r
        gw_ref[...] += _rows8(dhv * xhat)
        dxh = dhv * w_ref[...]
        dx = r * (dxh - xhat * jnp.mean(dxh * xhat, axis=-1, keepdims=True))
        if has_res:
            dx = dres_ref[...] + dx
        dx_ref[...] = dx
        dxb_ref[...] = dx.astype(BF16)

    row = pl.BlockSpec((tm, d), lambda i: (i, 0))
    in_specs = [row, row, pl.BlockSpec((1, d), lambda i: (0, 0))] + ([row] if has_res else [])
    args = (dh, x, w) + ((dres,) if has_res else ())
    return pl.pallas_call(
        body, name=name, grid=(s // tm,),
        in_specs=in_specs,
        out_specs=[row, row, pl.BlockSpec((8, d), lambda i: (0, 0))],
        out_shape=[jax.ShapeDtypeStruct((s, d), F32), jax.ShapeDtypeStruct((s, d), BF16),
                   jax.ShapeDtypeStruct((8, d), F32)],
        compiler_params=_params("arbitrary"),
    )(*args)


def _final_loss_bwd(x3, target, w, name, tm=256):
    s, d = x3.shape
    tm = min(tm, s)

    def body(x_ref, t_ref, w_ref, dx_ref, dxb_ref, gw_ref, loss_ref):
        @pl.when(pl.program_id(0) == 0)
        def _():
            gw_ref[...] = jnp.zeros_like(gw_ref)
            loss_ref[...] = jnp.zeros_like(loss_ref)

        xv = x_ref[...]
        r = lax.rsqrt(jnp.mean(xv * xv, axis=-1, keepdims=True) + EPS)
        xhat = xv * r
        y = xhat * w_ref[...]
        err = y - t_ref[...]
        part = 0.5 * jnp.mean(err * err, axis=-1, keepdims=True)
        tot = jnp.sum(part, axis=0, keepdims=True)
        rr = lax.broadcasted_iota(jnp.int32, loss_ref.shape, 0)
        cc = lax.broadcasted_iota(jnp.int32, loss_ref.shape, 1)
        loss_ref[...] += jnp.where((rr == 0) & (cc == 0), tot, 0.0)
        dy = err * (1.0 / d)
        gw_ref[...] += _rows8(dy * xhat)
        dxh = dy * w_ref[...]
        dx = r * (dxh - xhat * jnp.mean(dxh * xhat, axis=-1, keepdims=True))
        dx_ref[...] = dx
        dxb_ref[...] = dx.astype(BF16)

    row = pl.BlockSpec((tm, d), lambda i: (i, 0))
    return pl.pallas_call(
        body, name=name, grid=(s // tm,),
        in_specs=[row, row, pl.BlockSpec((1, d), lambda i: (0, 0))],
        out_specs=[row, row, pl.BlockSpec((8, d), lambda i: (0, 0)), pl.BlockSpec((8, 128), lambda i: (0, 0))],
        out_shape=[jax.ShapeDtypeStruct((s, d), F32), jax.ShapeDtypeStruct((s, d), BF16),
                   jax.ShapeDtypeStruct((8, d), F32), jax.ShapeDtypeStruct((8, 128), F32)],
        compiler_params=_params("arbitrary"),
    )(x3, target, w)


MM_TILES = (1024, 1408, 512, 256, 128)
MM_K_TILES = (2816, 2048, 1792, 1408, 1024, 512, 256, 128)
MM_VMEM_LIMIT = 56 * 1024 * 1024
MM_VMEM_BUDGET = 46 * 1024 * 1024


def _pick_tile(dim):
    for t in MM_TILES:
        if dim % t == 0:
            return t
    return dim


def _matmul(a, b, mode, name, *, out_dtype=F32, residual=None, extra_bf16=False, tm=None, tn=None, tk=None,
            groups=None):
    if groups is not None and mode == "nt":
        _, m, gw = a.shape
        n, k2 = b.shape
        k, tk = groups * gw, gw
    elif groups is not None and mode == "tn":
        k, m = a.shape
        _, k2, gw = b.shape
        n, tn = groups * gw, gw
    elif mode == "nn":
        (m, k), (k2, n) = a.shape, b.shape
    elif mode == "nt":
        (m, k), (n, k2) = a.shape, b.shape
    else:
        (k, m), (k2, n) = a.shape, b.shape
    assert k == k2, (a.shape, b.shape, mode)
    tm = _pick_tile(m) if tm is None else min(tm, m)
    tn = _pick_tile(n) if tn is None else min(tn, n)
    out_bytes = tm * tn * (jnp.dtype(out_dtype).itemsize + (2 if extra_bf16 else 0) + (4 if residual is not None else 0))

    def vmem_bytes(t):
        return (2 * (tm * t * a.dtype.itemsize + t * tn * b.dtype.itemsize) + 2 * out_bytes
                + (tm * tn * 4 if t < k else 0))

    if tk is None:
        tk = next(t for t in MM_K_TILES if k % t == 0 and t <= k and vmem_bytes(t) <= MM_VMEM_BUDGET)
    assert m % tm == 0 and n % tn == 0 and k % tk == 0, (m, n, k, tm, tn, tk)
    nk = k // tk
    dims = {"nn": ((1,), (0,)), "nt": ((1,), (1,)), "tn": ((0,), (0,))}[mode]
    has_res = residual is not None

    def body(*refs):
        refs = list(refs)
        a_ref, b_ref = refs[0], refs[1]
        r_ref = refs[2] if has_res else None
        outs = refs[2 + has_res:]
        o_ref = outs[0]
        o2_ref = outs[1] if extra_bf16 else None
        def finish(r):
            if has_res:
                r = r_ref[...] + r
            o_ref[...] = r.astype(out_dtype)
            if extra_bf16:
                o2_ref[...] = r.astype(BF16)

        if nk == 1:
            finish(_dot(a_ref[...], b_ref[...], dims))
            return
        acc = outs[-1]
        kk = pl.program_id(2)

        @pl.when(kk == 0)
        def _():
            acc[...] = _dot(a_ref[...], b_ref[...], dims)

        if nk > 2:
            @pl.when((kk > 0) & (kk < nk - 1))
            def _():
                acc[...] += _dot(a_ref[...], b_ref[...], dims)

        @pl.when(kk == nk - 1)
        def _():
            finish(acc[...] + _dot(a_ref[...], b_ref[...], dims))

    if mode == "tn":
        a_spec = pl.BlockSpec((tk, tm), lambda i, j, kk: (kk, i))
    elif groups is not None:
        a_spec = pl.BlockSpec((None, tm, tk), lambda i, j, kk: (kk, i, 0))
    else:
        a_spec = pl.BlockSpec((tm, tk), lambda i, j, kk: (i, kk))
    if mode == "nt":
        b_spec = pl.BlockSpec((tn, tk), lambda i, j, kk: (j, kk))
    elif groups is not None:
        b_spec = pl.BlockSpec((None, tk, tn), lambda i, j, kk: (j, kk, 0))
    else:
        b_spec = pl.BlockSpec((tk, tn), lambda i, j, kk: (kk, j))
    o_spec = pl.BlockSpec((tm, tn), lambda i, j, kk: (i, j))
    in_specs = [a_spec, b_spec] + ([o_spec] if has_res else [])
    out_specs = [o_spec] + ([o_spec] if extra_bf16 else [])
    out_shape = [jax.ShapeDtypeStruct((m, n), out_dtype)] + ([jax.ShapeDtypeStruct((m, n), BF16)] if extra_bf16 else [])
    args = (a, b) + ((residual,) if has_res else ())
    res = pl.pallas_call(
        body, name=name, grid=(m // tm, n // tn, nk),
        in_specs=in_specs, out_specs=out_specs, out_shape=out_shape,
        scratch_shapes=[pltpu.VMEM((tm, tn) if nk > 1 else (8, 128), F32)],
        compiler_params=pltpu.CompilerParams(dimension_semantics=("parallel", "parallel", "arbitrary"),
                                             vmem_limit_bytes=MM_VMEM_LIMIT),
    )(*args)
    return res if extra_bf16 else res[0]


def _hgrn_gates(qp, fp, lb):
    sig = _sigmoid(fp)
    f = lb + (1.0 - lb) * sig
    logf = jnp.log(f)
    k = 1.0 - f
    sq = _sigmoid(qp)
    q = qp * sq
    return sig, f, logf, k, sq, q


def _hgrn_fwd(proj, lb0, lb1, norm_w, name, tb=512):
    s = proj.shape[0]
    tb = min(tb, s)
    nb, ncb = s // tb, tb // CHUNK

    def body(q_ref, f_ref, i_ref, g_ref, a0_ref, a1_ref, nw_ref, o_ref, og_ref, st_ref, state):
        @pl.when(pl.program_id(1) == 0)
        def _():
            state[...] = jnp.zeros_like(state)

        lb2 = _sigmoid(a0_ref[...] - a1_ref[...])
        row = lax.broadcasted_iota(jnp.int32, (CHUNK, CHUNK), 0)
        col = lax.broadcasted_iota(jnp.int32, (CHUNK, CHUNK), 1)
        tril = row >= col
        ones_l = tril.astype(BF16)
        nw = nw_ref[...]

        def chunk(c, carry):
            rows = pl.ds(pl.multiple_of(c * CHUNK, CHUNK), CHUNK)
            for hh in range(HGRN_HEADS_PER_STEP):
                cols = slice(hh * HEAD, (hh + 1) * HEAD)
                v = i_ref[rows, cols].astype(F32)
                _, _, logf, k, _, q = _hgrn_gates(q_ref[rows, cols].astype(F32), f_ref[rows, cols].astype(F32),
                                                  lb2[:, cols])
                b = _exact_ones_dot(ones_l, logf)
                bl = jnp.sum(logf, axis=0, keepdims=True)
                bm = 0.5 * bl
                st = state[hh]
                st_ref[hh, c] = st
                qt = q * jnp.exp(b - bm)
                kt = k * jnp.exp(bm - b)
                a = jnp.where(tril, _dot_nt(qt, kt), 0.0)
                o = _dot_nt(q * jnp.exp(b), st) + _dot_nn(a, v)
                state[hh] = st * jnp.exp(bl) + _dot_tn(v, k * jnp.exp(bl - b))
                o_ref[rows, cols] = o
                on = (o * lax.rsqrt(jnp.mean(o * o, axis=-1, keepdims=True) + EPS)) * nw
                gv = g_ref[rows, cols].astype(F32)
                og_ref[rows, cols] = (on * (gv * _sigmoid(gv))).astype(BF16)
            return carry

        lax.fori_loop(0, ncb, chunk, 0, unroll=HGRN_UNROLL)

    hp, wd = HGRN_HEADS_PER_STEP, HGRN_HEADS_PER_STEP * HEAD
    ngrp = N_HEADS // hp

    def colblk(group):
        return pl.BlockSpec((tb, wd), lambda h, j: (j, group * ngrp + h))

    vec = pl.BlockSpec((1, wd), lambda h, j: (0, h))
    out_blk = pl.BlockSpec((tb, wd), lambda h, j: (j, h))
    return pl.pallas_call(
        body, name=name, grid=(ngrp, nb),
        in_specs=[colblk(0), colblk(1), colblk(2), colblk(3), vec, vec, pl.BlockSpec((1, HEAD), lambda h, j: (0, 0))],
        out_specs=[out_blk, out_blk, pl.BlockSpec((hp, ncb, HEAD, HEAD), lambda h, j: (h, j, 0, 0))],
        out_shape=[jax.ShapeDtypeStruct((s, HGRN_W), F32), jax.ShapeDtypeStruct((s, 2 * HGRN_W), BF16),
                   jax.ShapeDtypeStruct((N_HEADS, s // CHUNK, HEAD, HEAD), F32)],
        scratch_shapes=[pltpu.VMEM((hp, HEAD, HEAD), F32)],
        compiler_params=_params("parallel", "arbitrary"),
    )(proj, proj, proj, proj, lb0, lb1, norm_w)


def _hgrn_bwd(proj, lb0, lb1, norm_w, o, states, dmix, name, tb=512):
    s = proj.shape[0]
    tb = min(tb, s)
    nb, ncb = s // tb, tb // CHUNK

    def body(q_ref, f_ref, i_ref, g_ref, a0_ref, a1_ref, nw_ref, o_ref, st_ref, dm_ref,
             dp_ref, glb_ref, gnw_ref, dstate):
        h = pl.program_id(0)

        @pl.when(pl.program_id(1) == 0)
        def _():
            dstate[...] = jnp.zeros_like(dstate)
            glb_ref[...] = jnp.zeros_like(glb_ref)

        @pl.when((pl.program_id(1) == 0) & (h == 0))
        def _():
            gnw_ref[...] = jnp.zeros_like(gnw_ref)

        lb2 = _sigmoid(a0_ref[...] - a1_ref[...])
        row = lax.broadcasted_iota(jnp.int32, (CHUNK, CHUNK), 0)
        col = lax.broadcasted_iota(jnp.int32, (CHUNK, CHUNK), 1)
        tril = row >= col
        ones_l = tril.astype(BF16)
        ones_u = (row <= col).astype(BF16)
        nw = nw_ref[...]

        def chunk(cc, carry):
            c = ncb - 1 - cc
            rows = pl.ds(pl.multiple_of(c * CHUNK, CHUNK), CHUNK)
            for hh in range(HGRN_HEADS_PER_STEP):
                cols = slice(hh * HEAD, (hh + 1) * HEAD)
                lb = lb2[:, cols]
                qp = q_ref[rows, cols].astype(F32)
                v = i_ref[rows, cols].astype(F32)
                sig, f, logf, k, sq, q = _hgrn_gates(qp, f_ref[rows, cols].astype(F32), lb)
                gv = g_ref[rows, cols].astype(F32)
                sg = _sigmoid(gv)
                silu_g = gv * sg
                dog = dm_ref[rows, cols].astype(F32)
                ov = o_ref[rows, cols]
                r = lax.rsqrt(jnp.mean(ov * ov, axis=-1, keepdims=True) + EPS)
                ohat = ov * r
                on = ohat * nw
                dp_ref[3, rows, cols] = (dog * on * (sg * (1.0 + gv * (1.0 - sg)))).astype(BF16)
                don = dog * silu_g
                gnw_ref[...] += _rows8(don * ohat)
                doh = don * nw
                do = r * (doh - ohat * jnp.mean(doh * ohat, axis=-1, keepdims=True))
                b = _exact_ones_dot(ones_l, logf)
                bl = jnp.sum(logf, axis=0, keepdims=True)
                bm = 0.5 * bl
                e_q = jnp.exp(b - bm)
                e_k = jnp.exp(bm - b)
                e_b = jnp.exp(b)
                e_l = jnp.exp(bl - b)
                qt, kt, qb, kb = q * e_q, k * e_k, q * e_b, k * e_l
                st0 = st_ref[hh, c]
                dst = dstate[hh]
                a = jnp.where(tril, _dot_nt(qt, kt), 0.0)
                da = jnp.where(tril, _dot_nt(do, v), 0.0)
                dq = _hdot_nn(da, kt) * e_q + _hdot_nn(do, st0) * e_b
                dkb = _hdot_nn(v, dst) * e_l
                dk = _hdot_tn(da, qt) * e_k + dkb
                dv = _dot_tn(a, do) + _dot_nt(kb, dst)
                e_bl = jnp.exp(bl)
                dstate[hh] = dst * e_bl + _dot_tn(do, qb)
                db = q * dq - k * dk
                db_last = jnp.sum(k * dkb, axis=0, keepdims=True) + e_bl * jnp.sum(st0 * dst, axis=0, keepdims=True)
                dlogf = _exact_ones_dot(ones_u, db) + db_last
                dfg = dlogf / f - dk
                dp_ref[1, rows, cols] = (dfg * (1.0 - lb) * (sig * (1.0 - sig))).astype(BF16)
                glb_ref[:, cols] += _rows8(dfg * (1.0 - sig)) * (lb * (1.0 - lb))
                dp_ref[0, rows, cols] = (dq * (sq * (1.0 + qp * (1.0 - sq)))).astype(BF16)
                dp_ref[2, rows, cols] = dv.astype(BF16)
            return carry

        lax.fori_loop(0, ncb, chunk, 0, unroll=HGRN_UNROLL)

    hp, wd = HGRN_HEADS_PER_STEP, HGRN_HEADS_PER_STEP * HEAD
    ngrp = N_HEADS // hp

    def colblk(group):
        return pl.BlockSpec((tb, wd), lambda h, j: (nb - 1 - j, group * ngrp + h))

    vec = pl.BlockSpec((1, wd), lambda h, j: (0, h))
    blk = pl.BlockSpec((tb, wd), lambda h, j: (nb - 1 - j, h))
    return pl.pallas_call(
        body, name=name, grid=(ngrp, nb),
        in_specs=[colblk(0), colblk(1), colblk(2), colblk(3), vec, vec, pl.BlockSpec((1, HEAD), lambda h, j: (0, 0)),
                  blk, pl.BlockSpec((hp, ncb, HEAD, HEAD), lambda h, j: (h, nb - 1 - j, 0, 0)), blk],
        out_specs=[pl.BlockSpec((4, tb, wd), lambda h, j: (0, nb - 1 - j, h)),
                   pl.BlockSpec((8, wd), lambda h, j: (0, h)), pl.BlockSpec((8, HEAD), lambda h, j: (0, 0))],
        out_shape=[jax.ShapeDtypeStruct((DPROJ_GROUPS, s, HGRN_W), BF16),
                   jax.ShapeDtypeStruct((8, HGRN_W), F32), jax.ShapeDtypeStruct((8, HEAD), F32)],
        scratch_shapes=[pltpu.VMEM((hp, HEAD, HEAD), F32)],
        compiler_params=_params("arbitrary", "arbitrary"),
    )(proj, proj, proj, proj, lb0, lb1, norm_w, o, states, dmix)


HALO_BLK = 16


def _f32(ref):
    return ref[...].astype(F32)


def _halo_prev(ref):
    return ref[...].astype(F32)[HALO_BLK - HALO:]


def _halo_next(ref):
    return ref[...].astype(F32)[:HALO]


def _conv3(x0, x1, x2, w_ref):
    y = x0 * w_ref[0:1, :]
    y = y + x1 * w_ref[1:2, :]
    return y + x2 * w_ref[2:3, :]


def _sconv_fwd(proj, w8, mix, name, tb=256):
    s = proj.shape[0]
    tb = min(tb, s)
    hb = tb // HALO_BLK

    def body(cb_ref, cc_ref, ch_ref, cch_ref, chh_ref, w_ref, mix_ref, y_ref):
        first = pl.program_id(0) == 0
        u = _f32(cc_ref) * _f32(ch_ref)
        uh = jnp.where(first, 0.0, _halo_prev(cch_ref) * _halo_prev(chh_ref))
        conv = _conv3(_shift_down(u, uh, 2), _shift_down(u, uh, 1), u, w_ref)
        y_ref[...] = (_f32(cb_ref) * conv).astype(BF16)

    def blk(g):
        return pl.BlockSpec((tb, HGRN_W), lambda j: (j, g))

    def halo(g):
        return pl.BlockSpec((HALO_BLK, HGRN_W), lambda j: (jnp.maximum(j * hb - 1, 0), g))

    return pl.pallas_call(
        body, name=name, grid=(s // tb,),
        in_specs=[blk(4), blk(5), blk(6), halo(5), halo(6), pl.BlockSpec((HALO, HGRN_W), lambda j: (0, 0)),
                  pl.BlockSpec(memory_space=pl.ANY)],
        out_specs=pl.BlockSpec((tb, HGRN_W), lambda j: (j, 1)),
        out_shape=jax.ShapeDtypeStruct(mix.shape, BF16),
        input_output_aliases={6: 0},
        compiler_params=_params("parallel"),
    )(proj, proj, proj, proj, proj, w8, mix)


def _sconv_bwd(proj, w8, dmix, dproj, name, tb=256):
    s = proj.shape[0]
    tb = min(tb, s)
    hb = tb // HALO_BLK
    nb = s // tb
    last_h = s // HALO_BLK - 1

    def body(cb_ref, cc_ref, ch_ref, cch_ref, chh_ref, cbn_ref, dy_ref, dyn_ref, w_ref, dproj_ref,
             dp_ref, gw_ref):
        j = pl.program_id(0)

        @pl.when(j == 0)
        def _():
            gw_ref[...] = jnp.zeros_like(gw_ref)

        cc, ch, cb = _f32(cc_ref), _f32(ch_ref), _f32(cb_ref)
        u = cc * ch
        uh = jnp.where(j == 0, 0.0, _halo_prev(cch_ref) * _halo_prev(chh_ref))
        u2, u1 = _shift_down(u, uh, 2), _shift_down(u, uh, 1)
        conv = _conv3(u2, u1, u, w_ref)
        dy = _f32(dy_ref)
        dp_ref[0] = (dy * conv).astype(BF16)
        dc = dy * cb
        dcn = jnp.where(j == nb - 1, 0.0, _halo_next(dyn_ref) * _halo_next(cbn_ref))
        gw_ref[0:8, :] += _rows8(dc * u2)
        gw_ref[8:16, :] += _rows8(dc * u1)
        gw_ref[16:24, :] += _rows8(dc * u)
        du = dc * w_ref[2:3, :] + _shift_up(dc, dcn, 1) * w_ref[1:2, :] + _shift_up(dc, dcn, 2) * w_ref[0:1, :]
        dp_ref[1] = (du * ch).astype(BF16)
        dp_ref[2] = (du * cc).astype(BF16)
        dp_ref[3] = jnp.zeros(dp_ref.shape[1:], BF16)

    def blk(g):
        return pl.BlockSpec((tb, HGRN_W), lambda j: (j, g))

    def halo_prev(g):
        return pl.BlockSpec((HALO_BLK, HGRN_W), lambda j: (jnp.maximum(j * hb - 1, 0), g))

    def halo_next(g):
        return pl.BlockSpec((HALO_BLK, HGRN_W), lambda j: (jnp.minimum((j + 1) * hb, last_h), g))

    return pl.pallas_call(
        body, name=name, grid=(nb,),
        in_specs=[blk(4), blk(5), blk(6), halo_prev(5), halo_prev(6), halo_next(4), blk(1), halo_next(1),
                  pl.BlockSpec((HALO, HGRN_W), lambda j: (0, 0)), pl.BlockSpec(memory_space=pl.ANY)],
        out_specs=[pl.BlockSpec((4, tb, HGRN_W), lambda j: (1, j, 0)), pl.BlockSpec((24, HGRN_W), lambda j: (0, 0))],
        out_shape=[jax.ShapeDtypeStruct(dproj.shape, BF16), jax.ShapeDtypeStruct((24, HGRN_W), F32)],
        input_output_aliases={9: 0},
        compiler_params=_params("arbitrary"),
    )(proj, proj, proj, proj, proj, proj, dmix, dmix, w8, dproj)


def _attn_fwd(q, kk, vv, name, tb=512):
    s, d = q.shape
    m = kk.shape[0]
    tb = min(tb, s)
    scale = MEM_HEAD_DIM ** -0.5

    def body(q_ref, k_ref, v_ref, o_ref):
        for hh in range(MEM_HEADS):
            cols = slice(hh * MEM_HEAD_DIM, (hh + 1) * MEM_HEAD_DIM)
            sc = _dot_nt(q_ref[:, cols], k_ref[:, cols]) * scale
            sc = sc - jnp.max(sc, axis=-1, keepdims=True)
            e = jnp.exp(sc)
            p = e / jnp.sum(e, axis=-1, keepdims=True)
            o_ref[:, cols] = _dot_nn(p, v_ref[:, cols]).astype(BF16)

    full = pl.BlockSpec((m, d), lambda i: (0, 0))
    return pl.pallas_call(
        body, name=name, grid=(s // tb,),
        in_specs=[pl.BlockSpec((tb, d), lambda i: (i, 0)), full, full],
        out_specs=pl.BlockSpec((tb, d), lambda i: (i, 0)),
        out_shape=jax.ShapeDtypeStruct((s, d), BF16),
        compiler_params=_params("parallel"),
    )(q, kk, vv)


def _attn_bwd(q, kk, vv, datt, name, tb=512):
    s, d = q.shape
    m = kk.shape[0]
    tb = min(tb, s)
    scale = MEM_HEAD_DIM ** -0.5

    def body(q_ref, k_ref, v_ref, do_ref, dq_ref, dk_ref, dv_ref):
        @pl.when(pl.program_id(0) == 0)
        def _():
            dk_ref[...] = jnp.zeros_like(dk_ref)
            dv_ref[...] = jnp.zeros_like(dv_ref)

        for hh in range(MEM_HEADS):
            cols = slice(hh * MEM_HEAD_DIM, (hh + 1) * MEM_HEAD_DIM)
            qh, kh, vh, doh = q_ref[:, cols], k_ref[:, cols], v_ref[:, cols], do_ref[:, cols]
            sc = _dot_nt(qh, kh) * scale
            sc = sc - jnp.max(sc, axis=-1, keepdims=True)
            e = jnp.exp(sc)
            p = e / jnp.sum(e, axis=-1, keepdims=True)
            dp = _dot_nt(doh, vh)
            ds = p * (dp - jnp.sum(dp * p, axis=-1, keepdims=True)) * scale
            dq_ref[:, cols] = _dot_nn(ds, kh).astype(BF16)
            dk_ref[:, cols] += _dot_tn(ds, qh)
            dv_ref[:, cols] += _dot_tn(p, doh)

    full = pl.BlockSpec((m, d), lambda i: (0, 0))
    row = pl.BlockSpec((tb, d), lambda i: (i, 0))
    return pl.pallas_call(
        body, name=name, grid=(s // tb,),
        in_specs=[row, full, full, row],
        out_specs=[row, full, full],
        out_shape=[jax.ShapeDtypeStruct((s, d), BF16), jax.ShapeDtypeStruct((m, d), F32),
                   jax.ShapeDtypeStruct((m, d), F32)],
        compiler_params=_params("arbitrary"),
    )(q, kk, vv, datt)


def _ffn_fwd(g, u, w8, bias, name, tb=512, tc=1408):
    s, f = g.shape
    tb = min(tb, s)
    tc = tc if f % tc == 0 else 512
    hb = tb // HALO_BLK

    def body(g_ref, gh_ref, u_ref, w_ref, b_ref, z_ref, a_ref):
        gv = _f32(g_ref)
        gh = jnp.where(pl.program_id(1) == 0, 0.0, _halo_prev(gh_ref))
        a = _conv3(_shift_down(gv, gh, 2), _shift_down(gv, gh, 1), gv, w_ref) + b_ref[...]
        a_ref[...] = a.astype(BF16)
        z_ref[...] = ((a * _sigmoid(a)) * _f32(u_ref)).astype(BF16)

    blk = pl.BlockSpec((tb, tc), lambda c, j: (j, c))
    return pl.pallas_call(
        body, name=name, grid=(f // tc, s // tb),
        in_specs=[blk, pl.BlockSpec((HALO_BLK, tc), lambda c, j: (jnp.maximum(j * hb - 1, 0), c)), blk,
                  pl.BlockSpec((HALO, tc), lambda c, j: (0, c)), pl.BlockSpec((1, tc), lambda c, j: (0, c))],
        out_specs=[blk, blk],
        out_shape=[jax.ShapeDtypeStruct((s, f), BF16), jax.ShapeDtypeStruct((s, f), BF16)],
        compiler_params=pltpu.CompilerParams(dimension_semantics=("parallel", "parallel"),
                                             vmem_limit_bytes=MM_VMEM_LIMIT),
    )(g, g, u, w8, bias)


def _ffn_bwd(a, g, u, dz, w8, name, tb=256, tc=1408):
    s, f = g.shape
    tb = min(tb, s)
    tc = tc if f % tc == 0 else 512
    nb = s // tb

    def body(a_ref, g_ref, u_ref, dz_ref, w_ref, dg_ref, du_ref, gb_ref, gw_ref, da_next):
        jj = pl.program_id(1)

        @pl.when(jj == 0)
        def _():
            gb_ref[...] = jnp.zeros_like(gb_ref)
            gw_ref[...] = jnp.zeros_like(gw_ref)
            da_next[...] = jnp.zeros_like(da_next)

        a = _f32(a_ref)
        sa = _sigmoid(a)
        dz = _f32(dz_ref)
        du_ref[...] = (dz * (a * sa)).astype(BF16)
        da = dz * _f32(u_ref) * (sa * (1.0 + a * (1.0 - sa)))
        gb_ref[...] += _rows8(da)
        dan = da_next[...]
        da1, da2 = _shift_up(da, dan, 1), _shift_up(da, dan, 2)
        gv = _f32(g_ref)
        gw_ref[0:8, :] += _rows8(da2 * gv)
        gw_ref[8:16, :] += _rows8(da1 * gv)
        gw_ref[16:24, :] += _rows8(da * gv)
        dg_ref[...] = (da * w_ref[2:3, :] + da1 * w_ref[1:2, :] + da2 * w_ref[0:1, :]).astype(BF16)
        da_next[...] = da[:HALO]

    blk = pl.BlockSpec((tb, tc), lambda c, jj: (nb - 1 - jj, c))
    return pl.pallas_call(
        body, name=name, grid=(f // tc, nb),
        in_specs=[blk, blk, blk, blk, pl.BlockSpec((HALO, tc), lambda c, jj: (0, c))],
        out_specs=[blk, blk, pl.BlockSpec((8, tc), lambda c, jj: (0, c)), pl.BlockSpec((24, tc), lambda c, jj: (0, c))],
        out_shape=[jax.ShapeDtypeStruct((s, f), BF16), jax.ShapeDtypeStruct((s, f), BF16),
                   jax.ShapeDtypeStruct((8, f), F32), jax.ShapeDtypeStruct((24, f), F32)],
        scratch_shapes=[pltpu.VMEM((HALO, tc), F32)],
        compiler_params=pltpu.CompilerParams(dimension_semantics=("parallel", "arbitrary"),
                                             vmem_limit_bytes=MM_VMEM_LIMIT),
    )(a, g, u, dz, w8)


def _window(ref, axis, slot, size):
    start = pl.multiple_of(slot * size, size)
    if axis == 0:
        return ref.at[pl.ds(start, size), :]
    return ref.at[:, pl.ds(start, size)]


def _chip_peers():
    x, y, c = lax.axis_index("x"), lax.axis_index("y"), lax.axis_index("c")
    peers = [(1 - x, y, c), (x, 1 - y, c), (1 - x, 1 - y, c)]
    slots = [2 * (1 - x) + y, 2 * x + (1 - y), 2 * (1 - x) + (1 - y)]
    return 2 * x + y, peers, slots


HBM_SPEC = pl.BlockSpec(memory_space=pltpu.HBM)
SEM_SPEC = pl.BlockSpec(memory_space=pltpu.SEMAPHORE)
EFFECT = pltpu.SideEffectType.DATAFLOW_SIDE_EFFECTING


def _hbm(a):
    return pltpu.with_memory_space_constraint(a, pltpu.HBM)


def _cast_into_full(x, axis, slot_arr, dtype, name, after=None):
    r, c = x.shape
    tr = _row_tile(r, 256)
    nb = r // tr
    full = (r * N_CHIPS, c) if axis == 0 else (r, c * N_CHIPS)

    def body(slot_ref, x_ref, *rest):
        rest[-1][...] = x_ref[...].astype(dtype)

    if axis == 0:
        out_map = lambda i, s: (s[0] * nb + i, 0)
    else:
        out_map = lambda i, s: (i, s[0])
    extra = [] if after is None else [after]
    return pl.pallas_call(
        body, name=name,
        grid_spec=pltpu.PrefetchScalarGridSpec(
            num_scalar_prefetch=1, grid=(nb,),
            in_specs=[pl.BlockSpec((tr, c), lambda i, s: (i, 0))] + [pl.BlockSpec(memory_space=pl.ANY)] * len(extra),
            out_specs=pl.BlockSpec((tr, c), out_map)),
        out_shape=jax.ShapeDtypeStruct(full, dtype),
        compiler_params=_params("parallel"),
    )(slot_arr, x, *extra)


def _piece(ref, axis, slot, half):
    size = ref.shape[axis] // N_CHIPS
    if half is None:
        return _window(ref, axis, slot, size)
    if axis == 0:
        h = size // 2
        return ref.at[pl.ds(pl.multiple_of(slot * size + half * h, h), h), :]
    h = ref.shape[0] // 2
    return ref.at[pl.ds(pl.multiple_of(half * h, h), h), pl.ds(pl.multiple_of(slot * size, size), size)]


def _gather_start(fulls, axes, split, groups, name):
    n, ng = len(fulls), len(groups)

    def body(*refs):
        outs = refs[n:]
        sems = outs[:2 * ng]
        thru = outs[2 * ng:2 * ng + n]
        token = outs[-1]
        slot, peers, _ = _chip_peers()
        c = lax.axis_index("c")
        for g, members in enumerate(groups):
            for i, t in enumerate(members):
                mine = _piece(thru[t], axes[t], slot, c if split[t] else None)
                for k in range(3):
                    pltpu.make_async_remote_copy(
                        src_ref=mine, dst_ref=mine, send_sem=sems[2 * g].at[3 * i + k],
                        recv_sem=sems[2 * g + 1].at[3 * i + k], device_id=peers[k], device_id_type=MESH).start()
        token[...] = jnp.zeros_like(token)

    sem_shapes = []
    for members in groups:
        sem_shapes += [pltpu.SemaphoreType.DMA((3 * len(members),))] * 2
    res = pl.pallas_call(
        body, name=name,
        in_specs=[HBM_SPEC] * n,
        out_specs=[SEM_SPEC] * (2 * ng) + [HBM_SPEC] * n + [pl.BlockSpec(memory_space=pltpu.VMEM)],
        out_shape=sem_shapes + [pltpu.HBM(f.shape, f.dtype) for f in fulls] + [jax.ShapeDtypeStruct((8, 128), F32)],
        input_output_aliases={t: 2 * ng + t for t in range(n)},
        compiler_params=pltpu.CompilerParams(has_side_effects=EFFECT),
    )(*[_hbm(f) for f in fulls])
    sems = [(res[2 * g], res[2 * g + 1]) for g in range(ng)]
    return sems, list(res[2 * ng:2 * ng + n]), res[-1]


def _gather_relay(fulls, axes, split, sems, after, name):
    n = len(fulls)
    nsplit = sum(split)

    def body(*refs):
        send_sems, recv_sems = refs[n], refs[n + 1]
        outs = refs[n + 3:]
        d_send, d_recv = outs[0], outs[1]
        thru = outs[2:2 + n]
        token = outs[-1]
        slot, peers, slots = _chip_peers()
        c = lax.axis_index("c")
        sibling = (lax.axis_index("x"), lax.axis_index("y"), 1 - c)
        for t in range(n):
            half = c if split[t] else None
            for k in range(3):
                cp = pltpu.make_async_remote_copy(
                    src_ref=_piece(thru[t], axes[t], slot, half), dst_ref=_piece(thru[t], axes[t], slots[k], half),
                    send_sem=send_sems.at[3 * t + k], recv_sem=recv_sems.at[3 * t + k],
                    device_id=peers[k], device_id_type=MESH)
                cp.wait_send()
                cp.wait_recv()
        i = 0
        for t in range(n):
            if not split[t]:
                continue
            for k in range(3):
                got = _piece(thru[t], axes[t], slots[k], c)
                pltpu.make_async_remote_copy(
                    src_ref=got, dst_ref=got, send_sem=d_send.at[3 * i + k], recv_sem=d_recv.at[3 * i + k],
                    device_id=sibling, device_id_type=MESH).start()
            i += 1
        token[...] = jnp.zeros_like(token)

    res = pl.pallas_call(
        body, name=name,
        in_specs=[HBM_SPEC] * n + [SEM_SPEC, SEM_SPEC, pl.BlockSpec(memory_space=pl.ANY)],
        out_specs=[SEM_SPEC, SEM_SPEC] + [HBM_SPEC] * n + [pl.BlockSpec(memory_space=pltpu.VMEM)],
        out_shape=[pltpu.SemaphoreType.DMA((3 * nsplit,)), pltpu.SemaphoreType.DMA((3 * nsplit,))]
        + [pltpu.HBM(f.shape, f.dtype) for f in fulls] + [jax.ShapeDtypeStruct((8, 128), F32)],
        input_output_aliases={t: 2 + t for t in range(n)},
        compiler_params=pltpu.CompilerParams(has_side_effects=EFFECT),
    )(*fulls, sems[0], sems[1], after)
    return (res[0], res[1]), list(res[2:2 + n]), res[-1]


def _gather_finish(fulls, axes, split, sems, after, name):
    n = len(fulls)

    def body(*refs):
        d_send, d_recv = refs[n], refs[n + 1]
        thru = refs[n + 3:]
        _, _, slots = _chip_peers()
        c = lax.axis_index("c")
        sibling = (lax.axis_index("x"), lax.axis_index("y"), 1 - c)
        i = 0
        for t in range(n):
            if not split[t]:
                continue
            for k in range(3):
                cp = pltpu.make_async_remote_copy(
                    src_ref=_piece(thru[t], axes[t], slots[k], c), dst_ref=_piece(thru[t], axes[t], slots[k], 1 - c),
                    send_sem=d_send.at[3 * i + k], recv_sem=d_recv.at[3 * i + k],
                    device_id=sibling, device_id_type=MESH)
                cp.wait_send()
                cp.wait_recv()
            i += 1

    return pl.pallas_call(
        body, name=name,
        in_specs=[HBM_SPEC] * n + [SEM_SPEC, SEM_SPEC, pl.BlockSpec(memory_space=pl.ANY)],
        out_specs=[HBM_SPEC] * n,
        out_shape=[pltpu.HBM(f.shape, f.dtype) for f in fulls],
        input_output_aliases={t: t for t in range(n)},
        compiler_params=pltpu.CompilerParams(has_side_effects=EFFECT),
    )(*fulls, sems[0], sems[1], after)


def _scatter_start(grads_bf16, axes, name):
    n = len(grads_bf16)

    def shard_shape(g, ax):
        return (g.shape[0] // N_CHIPS, g.shape[1]) if ax == 0 else (g.shape[0], g.shape[1] // N_CHIPS)

    shapes = [shard_shape(g, ax) for g, ax in zip(grads_bf16, axes)]

    def body(*refs):
        outs = refs[2 * n:]
        send_sems, recv_sems = outs[0], outs[1]
        gb, land = outs[2:2 + n], outs[2 + n:2 + 2 * n]
        token = outs[-1]
        _, peers, slots = _chip_peers()
        for t in range(n):
            size = shapes[t][axes[t]]
            for k in range(3):
                pltpu.make_async_remote_copy(
                    src_ref=_window(gb[t], axes[t], slots[k], size), dst_ref=land[t].at[k],
                    send_sem=send_sems.at[3 * t + k], recv_sem=recv_sems.at[3 * t + k],
                    device_id=peers[k], device_id_type=MESH).start()
        token[...] = jnp.zeros_like(token)

    lands = [_hbm(lax.empty((3,) + sh, BF16)) for sh in shapes]
    res = pl.pallas_call(
        body, name=name,
        in_specs=[HBM_SPEC] * (2 * n),
        out_specs=[SEM_SPEC, SEM_SPEC] + [HBM_SPEC] * (2 * n) + [pl.BlockSpec(memory_space=pltpu.VMEM)],
        out_shape=[pltpu.SemaphoreType.DMA((3 * n,)), pltpu.SemaphoreType.DMA((3 * n,))]
        + [pltpu.HBM(g.shape, g.dtype) for g in grads_bf16] + [pltpu.HBM((3,) + sh, BF16) for sh in shapes]
        + [jax.ShapeDtypeStruct((8, 128), F32)],
        input_output_aliases={t: 2 + t for t in range(2 * n)},
        compiler_params=pltpu.CompilerParams(has_side_effects=EFFECT),
    )(*[_hbm(g) for g in grads_bf16], *lands)
    return (res[0], res[1]), list(res[2:2 + n]), list(res[2 + n:2 + 2 * n]), res[-1]


def _scatter_wait(grads_thru, lands_thru, axes, sems, after, name):
    n = len(grads_thru)

    def body(*refs):
        send_sems, recv_sems = refs[2 * n], refs[2 * n + 1]
        outs = refs[2 * n + 3:]
        gb, land = outs[:n], outs[n:]
        _, peers, slots = _chip_peers()
        for t in range(n):
            size = land[t].shape[1 + axes[t]]
            for k in range(3):
                cp = pltpu.make_async_remote_copy(
                    src_ref=_window(gb[t], axes[t], slots[k], size), dst_ref=land[t].at[k],
                    send_sem=send_sems.at[3 * t + k], recv_sem=recv_sems.at[3 * t + k],
                    device_id=peers[k], device_id_type=MESH)
                cp.wait_send()
                cp.wait_recv()

    res = pl.pallas_call(
        body, name=name,
        in_specs=[HBM_SPEC] * (2 * n) + [SEM_SPEC, SEM_SPEC, pl.BlockSpec(memory_space=pl.ANY)],
        out_specs=[HBM_SPEC] * (2 * n),
        out_shape=[pltpu.HBM(g.shape, g.dtype) for g in grads_thru] + [pltpu.HBM(l.shape, l.dtype) for l in lands_thru],
        input_output_aliases={t: t for t in range(2 * n)},
        compiler_params=pltpu.CompilerParams(has_side_effects=EFFECT),
    )(*grads_thru, *lands_thru, sems[0], sems[1], after)
    return list(res[n:])


def _sibling_start(arrs, name):
    n = len(arrs)

    def body(*refs):
        outs = refs[2 * n:]
        send_sems, recv_sems = outs[0], outs[1]
        src, land = outs[2:2 + n], outs[2 + n:2 + 2 * n]
        token = outs[-1]
        sibling = (lax.axis_index("x"), lax.axis_index("y"), 1 - lax.axis_index("c"))
        for t in range(n):
            pltpu.make_async_remote_copy(
                src_ref=src[t], dst_ref=land[t], send_sem=send_sems.at[t], recv_sem=recv_sems.at[t],
                device_id=sibling, device_id_type=MESH).start()
        token[...] = jnp.zeros_like(token)

    lands = [_hbm(lax.empty(a.shape, a.dtype)) for a in arrs]
    res = pl.pallas_call(
        body, name=name,
        in_specs=[HBM_SPEC] * (2 * n),
        out_specs=[SEM_SPEC, SEM_SPEC] + [HBM_SPEC] * (2 * n) + [pl.BlockSpec(memory_space=pltpu.VMEM)],
        out_shape=[pltpu.SemaphoreType.DMA((n,)), pltpu.SemaphoreType.DMA((n,))]
        + [pltpu.HBM(a.shape, a.dtype) for a in arrs] * 2 + [jax.ShapeDtypeStruct((8, 128), F32)],
        input_output_aliases={t: 2 + t for t in range(2 * n)},
        compiler_params=pltpu.CompilerParams(has_side_effects=EFFECT),
    )(*[_hbm(a) for a in arrs], *lands)
    return (res[0], res[1]), list(res[2:2 + n]), list(res[2 + n:2 + 2 * n]), res[-1]


def _sibling_wait(src_thru, lands_thru, sems, after, name):
    n = len(src_thru)

    def body(*refs):
        send_sems, recv_sems = refs[2 * n], refs[2 * n + 1]
        outs = refs[2 * n + 3:]
        src, land = outs[:n], outs[n:]
        sibling = (lax.axis_index("x"), lax.axis_index("y"), 1 - lax.axis_index("c"))
        for t in range(n):
            cp = pltpu.make_async_remote_copy(
                src_ref=src[t], dst_ref=land[t], send_sem=send_sems.at[t], recv_sem=recv_sems.at[t],
                device_id=sibling, device_id_type=MESH)
            cp.wait_send()
            cp.wait_recv()

    res = pl.pallas_call(
        body, name=name,
        in_specs=[HBM_SPEC] * (2 * n) + [SEM_SPEC, SEM_SPEC, pl.BlockSpec(memory_space=pl.ANY)],
        out_specs=[HBM_SPEC] * (2 * n),
        out_shape=[pltpu.HBM(a.shape, a.dtype) for a in src_thru] * 2,
        input_output_aliases={t: t for t in range(2 * n)},
        compiler_params=pltpu.CompilerParams(has_side_effects=EFFECT),
    )(*src_thru, *lands_thru, sems[0], sems[1], after)
    return list(res[:n]), list(res[n:])


def _all_reduce_small(packed, name):
    nc = packed.shape[1]
    vmem = pl.BlockSpec(memory_space=pltpu.VMEM)

    def body(in_ref, out_ref, gbuf, send_sems, recv_sems):
        x, y, c = lax.axis_index("x"), lax.axis_index("y"), lax.axis_index("c")
        me = 4 * x + 2 * y + c
        gbuf[me] = jnp.sum(in_ref[...], axis=0, keepdims=True)
        copies = []
        for k in range(1, 8):
            peer = (x ^ ((k >> 2) & 1), y ^ ((k >> 1) & 1), c ^ (k & 1))
            rc = pltpu.make_async_remote_copy(
                src_ref=gbuf.at[me], dst_ref=gbuf.at[me], send_sem=send_sems.at[k - 1], recv_sem=recv_sems.at[k - 1],
                device_id=peer, device_id_type=MESH)
            rc.start()
            copies.append(rc)
        for k in range(1, 8):
            peer = (x ^ ((k >> 2) & 1), y ^ ((k >> 1) & 1), c ^ (k & 1))
            pltpu.make_async_remote_copy(
                src_ref=gbuf.at[me], dst_ref=gbuf.at[me ^ k], send_sem=send_sems.at[k - 1],
                recv_sem=recv_sems.at[k - 1], device_id=peer, device_id_type=MESH).wait_recv()
        for rc in copies:
            rc.wait_send()
        tot = gbuf[0]
        for d in range(1, 8):
            tot = tot + gbuf[d]
        out_ref[...] = tot

    return pl.pallas_call(
        body, name=name,
        in_specs=[vmem], out_specs=vmem,
        out_shape=jax.ShapeDtypeStruct((1, nc), F32),
        scratch_shapes=[pltpu.VMEM((8, 1, nc), F32), pltpu.SemaphoreType.DMA((7,)), pltpu.SemaphoreType.DMA((7,))],
    )(packed)


def _sum4(g_full, axis, slot_arr, recv, name):
    _, r, c = recv.shape
    tr = _row_tile(r, 512)
    nb = r // tr

    def body(slot_ref, own_ref, recv_ref, o_ref):
        acc = own_ref[...]
        for k in range(3):
            acc = acc + recv_ref[k].astype(F32)
        o_ref[...] = acc

    if axis == 0:
        own_map = lambda i, s: (s[0] * nb + i, 0)
    else:
        own_map = lambda i, s: (i, s[0])
    return pl.pallas_call(
        body, name=name,
        grid_spec=pltpu.PrefetchScalarGridSpec(
            num_scalar_prefetch=1, grid=(nb,),
            in_specs=[pl.BlockSpec((tr, c), own_map), pl.BlockSpec((3, tr, c), lambda i, s: (0, i, 0))],
            out_specs=pl.BlockSpec((tr, c), lambda i, s: (i, 0))),
        out_shape=jax.ShapeDtypeStruct((r, c), F32),
        compiler_params=pltpu.CompilerParams(dimension_semantics=("parallel",), vmem_limit_bytes=MM_VMEM_LIMIT),
    )(slot_arr, g_full, recv)


def _adamw(w, g_parts, m, v, name):
    r, c = w.shape
    tr = r if r % 128 else _row_tile(r, 256)
    npart = len(g_parts)

    def body(*refs):
        w_ref = refs[0]
        g_refs = refs[1:1 + npart]
        m_ref, v_ref, g_out, d_out, m_out, v_out = refs[1 + npart:]
        g = g_refs[0][...]
        for gr in g_refs[1:]:
            g = g + gr[...]
        mm = ADAM_B1 * m_ref[...] + (1.0 - ADAM_B1) * g
        vv = ADAM_B2 * v_ref[...] + (1.0 - ADAM_B2) * (g * g)
        m_hat = mm / (1.0 - ADAM_B1 ** ADAM_STEP)
        v_hat = vv / (1.0 - ADAM_B2 ** ADAM_STEP)
        g_out[...] = g
        d_out[...] = -ADAM_LR * (m_hat / (jnp.sqrt(v_hat) + ADAM_EPS) + ADAM_WD * w_ref[...])
        m_out[...] = mm
        v_out[...] = vv

    blk = pl.BlockSpec((tr, c), lambda i: (i, 0))
    shp = jax.ShapeDtypeStruct((r, c), F32)
    return pl.pallas_call(
        body, name=name, grid=(r // tr,),
        in_specs=[blk] * (3 + npart), out_specs=[blk] * 4, out_shape=[shp] * 4,
        compiler_params=pltpu.CompilerParams(dimension_semantics=("parallel",), vmem_limit_bytes=MM_VMEM_LIMIT),
    )(w, *g_parts, m, v)


def _pad_rows8(w):
    return jnp.pad(w, ((0, HALO - w.shape[0]), (0, 0)))


def kernel(x, mem, hgrn_lb, norm1_w, w_in, hgrn_norm_w, sconv_w, w_out, norm2_w, mem_norm_w, wq, wk, wv, wo, norm3_w, w_gate, w_up, ffn_conv_w, ffn_conv_b, w_down, final_norm_w, loss_target, m_hgrn_lb, m_norm1_w, m_w_in, m_hgrn_norm_w, m_sconv_w, m_w_out, m_norm2_w, m_mem_norm_w, m_wq, m_wk, m_wv, m_wo, m_norm3_w, m_w_gate, m_w_up, m_ffn_conv_w, m_ffn_conv_b, m_w_down, m_final_norm_w, v_hgrn_lb, v_norm1_w, v_w_in, v_hgrn_norm_w, v_sconv_w, v_w_out, v_norm2_w, v_mem_norm_w, v_wq, v_wk, v_wv, v_wo, v_norm3_w, v_w_gate, v_w_up, v_ffn_conv_w, v_ffn_conv_b, v_w_down, v_final_norm_w):
    xs, mems, tgt = x[0], mem[0], loss_target[0]
    d = xs.shape[1]
    fnw = final_norm_w.reshape(1, d)

    big = {"w_in": (w_in[0], 1), "w_out": (w_out[0], 0), "wq": (wq[0], 0), "wk": (wk[0], 0), "wv": (wv[0], 0),
           "wo": (wo[0], 0), "w_gate": (w_gate[0], 1), "w_up": (w_up[0], 1), "w_down": (w_down[0], 0)}
    names = list(big)
    slot_arr = (2 * lax.axis_index("x") + lax.axis_index("y")).astype(jnp.int32).reshape(1)
    gnames = names + ["sconv8", "fconv8"]
    axes = [big[n][1] for n in names] + [1, 1]
    groups = [["w_in"], ["w_out", "sconv8"], ["wq", "wk", "wv", "wo"], ["w_gate", "w_up", "fconv8", "w_down"]]
    gidx = [[gnames.index(n) for n in grp] for grp in groups]
    split = [True] * len(names) + [False, False]
    first = _cast_into_full(big["w_in"][0], 1, slot_arr, BF16, "cast_w_in")
    sems0, first, tok0 = _gather_start([first], [1], [True], [[0]], "gather_start_w_in")
    rest = [_cast_into_full(big[n][0], big[n][1], slot_arr, BF16, "cast_" + n, after=tok0) for n in names[1:]]
    rest += [_cast_into_full(_pad_rows8(sconv_w[0]), 1, slot_arr, F32, "cast_sconv_w", after=tok0),
             _cast_into_full(_pad_rows8(ffn_conv_w[0]), 1, slot_arr, F32, "cast_ffn_conv_w", after=tok0)]
    sems1, rest, tok = _gather_start(rest, axes[1:], split[1:], [[t - 1 for t in idx] for idx in gidx[1:]],
                                     "gather_start")
    gsems, fulls = sems0 + sems1, first + rest
    wf, relayed = {}, {}

    def gather_relay(g, after):
        idx = gidx[g]
        dsems, arrs, token = _gather_relay([fulls[t] for t in idx], [axes[t] for t in idx], [split[t] for t in idx],
                                           gsems[g], after, "gather_relay_%d" % g)
        relayed[g] = (dsems, arrs)
        return token[0:1, 0:1]

    def gather_finish(g, after):
        idx = gidx[g]
        dsems, arrs = relayed[g]
        got = _gather_finish(arrs, [axes[t] for t in idx], [split[t] for t in idx], dsems, after,
                             "gather_finish_%d" % g)
        wf.update(zip(groups[g], got))

    lb0, lb1 = hgrn_lb[0:1], hgrn_lb[1:2]

    h1 = _rmsnorm_fwd(xs, norm1_w + tok[0:1, 0:1], "norm1")
    gather_relay(0, h1)
    gather_finish(0, h1)
    proj = _matmul(h1, wf["w_in"], "nn", "proj_in", out_dtype=BF16, tn=1792)
    t1 = gather_relay(1, proj)
    o_h, og, states = _hgrn_fwd(proj, lb0, lb1, hgrn_norm_w + t1, "hgrn_fwd")
    gather_finish(1, o_h)
    t2 = gather_relay(2, o_h)
    sconv8 = wf["sconv8"]
    mix = _sconv_fwd(proj, sconv8, og, "sconv_fwd")
    x1 = _matmul(mix, wf["w_out"], "nn", "proj_out", residual=xs)
    t3 = gather_relay(3, x1)
    h2 = _rmsnorm_fwd(x1, norm2_w + (t2 + t3), "norm2")
    mem_n = _rmsnorm_fwd(mems, mem_norm_w, "norm_mem")
    gather_finish(2, h2)
    qa = _matmul(h2, wf["wq"], "nn", "attn_q", out_dtype=BF16)
    ka = _matmul(mem_n, wf["wk"], "nn", "attn_k", out_dtype=BF16)
    va = _matmul(mem_n, wf["wv"], "nn", "attn_v", out_dtype=BF16)
    att = _attn_fwd(qa, ka, va, "attn_fwd")
    x2 = _matmul(att, wf["wo"], "nn", "attn_o", residual=x1)
    h3 = _rmsnorm_fwd(x2, norm3_w, "norm3")
    gather_finish(3, h3)
    fconv8 = wf["fconv8"]
    gate = _matmul(h3, wf["w_gate"], "nn", "ffn_gate", out_dtype=BF16)
    up = _matmul(h3, wf["w_up"], "nn", "ffn_up", out_dtype=BF16)
    z, act = _ffn_fwd(gate, up, fconv8, ffn_conv_b, "ffn_act")
    x3 = _matmul(z, wf["w_down"], "nn", "ffn_down", residual=x2)

    dx3, dx3b, g_final, loss8 = _final_loss_bwd(x3, tgt, fnw, "loss_bwd")
    gw = {}
    dz = _matmul(dx3b, wf["w_down"], "nt", "d_z", out_dtype=BF16)
    gw["w_down"] = _matmul(z, dx3b, "tn", "g_w_down", extra_bf16=True)
    dgate, du, g_fb, g_fw = _ffn_bwd(act, gate, up, dz, fconv8, "ffn_act_bwd")
    dh3 = _matmul(dgate, wf["w_gate"], "nt", "d_h3_gate")
    dh3 = _matmul(du, wf["w_up"], "nt", "d_h3_up", residual=dh3, out_dtype=BF16)
    gw["w_gate"] = _matmul(h3, dgate, "tn", "g_w_gate", extra_bf16=True)
    gw["w_up"] = _matmul(h3, du, "tn", "g_w_up", extra_bf16=True)
    pending = []

    def scatter_start(grp):
        sems, g_thru, lands, token = _scatter_start([gw[n][1] for n in grp], [big[n][1] for n in grp],
                                                    "scatter_start_" + grp[0])
        pending.append((grp, sems, g_thru, lands))
        return token[0:1, 0:1]

    tok1 = scatter_start(["w_down", "w_gate", "w_up"])
    dx2, dx2b, g_n3 = _rmsnorm_bwd(dh3, x2, norm3_w + tok1, dx3, "norm3_bwd")
    datt = _matmul(dx2b, wf["wo"], "nt", "d_att", out_dtype=BF16)
    gw["wo"] = _matmul(att, dx2b, "tn", "g_wo", extra_bf16=True)
    dqa, dka, dva = _attn_bwd(qa, ka, va, datt, "attn_bwd")
    dh2 = _matmul(dqa, wf["wq"], "nt", "d_h2", out_dtype=BF16)
    gw["wq"] = _matmul(h2, dqa, "tn", "g_wq", extra_bf16=True)
    gw["wk"] = _matmul(mem_n, dka, "tn", "g_wk", extra_bf16=True)
    gw["wv"] = _matmul(mem_n, dva, "tn", "g_wv", extra_bf16=True)
    tok2 = scatter_start(["wo", "wq", "wk", "wv"])
    dmem_n = _matmul(dka, wf["wk"], "nt", "d_memn_k")
    dmem_n = _matmul(dva, wf["wv"], "nt", "d_memn_v", residual=dmem_n)
    _, _, g_nm = _rmsnorm_bwd(dmem_n, mems, mem_norm_w, None, "norm_mem_bwd")
    dx1, dx1b, g_n2 = _rmsnorm_bwd(dh2, x1, norm2_w + tok2, dx2, "norm2_bwd")
    dmix = _matmul(dx1b, wf["w_out"], "nt", "d_mix", out_dtype=BF16)
    gw["w_out"] = _matmul(mix, dx1b, "tn", "g_w_out", extra_bf16=True)
    tok3 = scatter_start(["w_out"])
    dproj, g_lb, g_hn = _hgrn_bwd(proj, lb0, lb1, hgrn_norm_w + tok3, o_h, states, dmix, "hgrn_bwd")
    dproj, g_sw = _sconv_bwd(proj, sconv8, dmix, dproj, "sconv_bwd")
    gw["w_in"] = _matmul(h1, dproj, "tn", "g_w_in", extra_bf16=True, groups=7)
    tok4 = scatter_start(["w_in"])
    dh1 = _matmul(dproj, wf["w_in"], "nt", "d_h1", out_dtype=BF16, groups=7, tn=2048)
    dx, _, g_n1 = _rmsnorm_bwd(dh1, xs, norm1_w + tok4, dx1, "norm1_bwd")

    small = [g_n1, g_n2, g_n3, g_final, g_nm, g_lb, g_hn, g_fb,
             g_sw[0:8], g_sw[8:16], g_sw[16:24], g_fw[0:8], g_fw[8:16], g_fw[16:24], loss8]
    widths = [a.shape[1] for a in small]
    tot = _all_reduce_small(jnp.concatenate(small, axis=1), "all_reduce_small")
    offs = [0]
    for wd_ in widths:
        offs.append(offs[-1] + wd_)
    sm = [tot[:, offs[i]:offs[i + 1]] for i in range(len(small))]
    s_n1, s_n2, s_n3, s_final, s_nm, s_lb, s_hn, s_fb = sm[:8]
    s_sw = jnp.concatenate(sm[8:11], axis=0)
    s_fw = jnp.concatenate(sm[11:14], axis=0)
    loss = sm[14][0, 0]
    slot = 2 * lax.axis_index("x") + lax.axis_index("y")
    s_sw = lax.dynamic_slice_in_dim(s_sw, slot * (HGRN_W // N_CHIPS), HGRN_W // N_CHIPS, axis=1)
    fsh = ffn_conv_w.shape[2]
    s_fw = lax.dynamic_slice_in_dim(s_fw, slot * fsh, fsh, axis=1)
    s_lb2 = jnp.concatenate([s_lb, -s_lb], axis=0)

    swaps = []
    after = tot
    for grp, sems, g_thru, lands in pending:
        got = _scatter_wait(g_thru, lands, [big[n][1] for n in grp], sems, after, "scatter_wait_" + grp[0])
        sums = [_sum4(gw[n][0], big[n][1], slot_arr, r, "core_sum_" + n) for n, r in zip(grp, got)]
        ssems, s_thru, s_lands, after = _sibling_start(sums, "sibling_start_" + grp[0])
        swaps.append((grp, ssems, s_thru, s_lands))

    moments = {"hgrn_lb": (m_hgrn_lb, v_hgrn_lb), "norm1_w": (m_norm1_w, v_norm1_w), "w_in": (m_w_in, v_w_in),
               "hgrn_norm_w": (m_hgrn_norm_w, v_hgrn_norm_w), "sconv_w": (m_sconv_w, v_sconv_w),
               "w_out": (m_w_out, v_w_out), "norm2_w": (m_norm2_w, v_norm2_w),
               "mem_norm_w": (m_mem_norm_w, v_mem_norm_w), "wq": (m_wq, v_wq), "wk": (m_wk, v_wk), "wv": (m_wv, v_wv),
               "wo": (m_wo, v_wo), "norm3_w": (m_norm3_w, v_norm3_w), "w_gate": (m_w_gate, v_w_gate),
               "w_up": (m_w_up, v_w_up), "ffn_conv_w": (m_ffn_conv_w, v_ffn_conv_w),
               "ffn_conv_b": (m_ffn_conv_b, v_ffn_conv_b), "w_down": (m_w_down, v_w_down),
               "final_norm_w": (m_final_norm_w, v_final_norm_w)}
    weights = {"hgrn_lb": hgrn_lb, "norm1_w": norm1_w, "w_in": w_in, "hgrn_norm_w": hgrn_norm_w, "sconv_w": sconv_w,
               "w_out": w_out, "norm2_w": norm2_w, "mem_norm_w": mem_norm_w, "wq": wq, "wk": wk, "wv": wv, "wo": wo,
               "norm3_w": norm3_w, "w_gate": w_gate, "w_up": w_up, "ffn_conv_w": ffn_conv_w, "ffn_conv_b": ffn_conv_b,
               "w_down": w_down, "final_norm_w": final_norm_w}
    small_g = {"hgrn_lb": s_lb2, "norm1_w": s_n1, "hgrn_norm_w": s_hn, "sconv_w": s_sw, "norm2_w": s_n2,
               "mem_norm_w": s_nm, "norm3_w": s_n3, "ffn_conv_w": s_fw, "ffn_conv_b": s_fb, "final_norm_w": s_final}
    order = list(weights)
    res = {}

    def adamw(n, parts):
        shape = weights[n].shape
        w2 = weights[n].reshape((-1, shape[-1]))
        m2, v2 = (t.reshape(w2.shape) for t in moments[n])
        res[n] = [t.reshape(shape) for t in _adamw(w2, [p.reshape(w2.shape) for p in parts], m2, v2, "adamw_" + n)]

    for n in order:
        if n not in big:
            adamw(n, [small_g[n]])
    after = after + res["final_norm_w"][1][0]
    for grp, ssems, s_thru, s_lands in swaps:
        own, other = _sibling_wait(s_thru, s_lands, ssems, after, "sibling_wait_" + grp[0])
        for n, a, b in zip(grp, own, other):
            adamw(n, [a, b])
        after = res[grp[-1]][1]

    return (loss, dx[None], *[res[n][0] for n in order], *[res[n][1] for n in order],
            *[res[n][2] for n in order], *[res[n][3] for n in order])
```

```python
import functools

import jax
import jax.numpy as jnp
from jax import lax
from jax.experimental import pallas as pl
from jax.experimental.pallas import tpu as pltpu

F32 = jnp.float32
BF16 = jnp.bfloat16
MESH = pl.DeviceIdType.MESH

EPS = 1e-6
HGRN_W = 1024
HEAD = 128
N_HEADS = 8
CHUNK = 64
HGRN_UNROLL = 8
HGRN_HEADS_PER_STEP = 4
DPROJ_GROUPS = 8
MEM_HEADS = 4
MEM_HEAD_DIM = 512
N_CHIPS = 4
HALO = 8

ADAM_LR = 0.001
ADAM_B1 = 0.9
ADAM_B2 = 0.999
ADAM_EPS = 1e-08
ADAM_WD = 0.01
ADAM_STEP = 10


def _sigmoid(x):
    return 1.0 / (1.0 + jnp.exp(-x))


def _dot(a, b, dims):
    return lax.dot_general(a.astype(BF16), b.astype(BF16), (dims, ((), ())),
                           preferred_element_type=F32)


def _dot_nn(a, b):
    return _dot(a, b, ((1,), (0,)))


def _dot_nt(a, b):
    return _dot(a, b, ((1,), (1,)))


def _dot_tn(a, b):
    return _dot(a, b, ((0,), (0,)))


def _hdot(a, b, dims):
    return lax.dot_general(a, b, (dims, ((), ())), precision=lax.Precision.HIGH, preferred_element_type=F32)


def _hdot_nn(a, b):
    return _hdot(a, b, ((1,), (0,)))


def _hdot_nt(a, b):
    return _hdot(a, b, ((1,), (1,)))


def _hdot_tn(a, b):
    return _hdot(a, b, ((0,), (0,)))


def _exact_ones_dot(ones_bf16, x):
    hi = x.astype(BF16)
    r1 = x - hi.astype(F32)
    mid = r1.astype(BF16)
    lo = (r1 - mid.astype(F32)).astype(BF16)
    dims = (((1,), (0,)), ((), ()))
    return (lax.dot_general(ones_bf16, hi, dims, preferred_element_type=F32)
            + lax.dot_general(ones_bf16, mid, dims, preferred_element_type=F32)
            + lax.dot_general(ones_bf16, lo, dims, preferred_element_type=F32))


def _rows8(v):
    t, c = v.shape
    return v.reshape(t // 8, 8, c).sum(axis=0)


def _shift_down(x, halo, s):
    rolled = pltpu.roll(x, s, 0)
    hrolled = pltpu.roll(halo, s, 0)
    row = lax.broadcasted_iota(jnp.int32, hrolled.shape, 0)
    head = jnp.where(row < s, hrolled, rolled[:HALO])
    return jnp.concatenate([head, rolled[HALO:]], axis=0)


def _shift_up(x, halo, s):
    t = x.shape[0]
    rolled = pltpu.roll(x, t - s, 0)
    hrolled = pltpu.roll(halo, HALO - s, 0)
    row = lax.broadcasted_iota(jnp.int32, hrolled.shape, 0)
    tail = jnp.where(row >= HALO - s, hrolled, rolled[t - HALO:])
    return jnp.concatenate([rolled[:t - HALO], tail], axis=0)


def _params(*sem):
    return pltpu.CompilerParams(dimension_semantics=sem, vmem_limit_bytes=MM_VMEM_LIMIT)


def _row_tile(r, pref):
    while r % pref:
        pref //= 2
    return pref


def _rmsnorm_fwd(x, w, name, tm=512):
    s, d = x.shape
    tm = min(tm, s)

    def body(x_ref, w_ref, o_ref):
        xv = x_ref[...]
        r = lax.rsqrt(jnp.mean(xv * xv, axis=-1, keepdims=True) + EPS)
        o_ref[...] = ((xv * r) * w_ref[...]).astype(BF16)

    return pl.pallas_call(
        body, name=name, grid=(s // tm,),
        in_specs=[pl.BlockSpec((tm, d), lambda i: (i, 0)), pl.BlockSpec((1, d), lambda i: (0, 0))],
        out_specs=pl.BlockSpec((tm, d), lambda i: (i, 0)),
        out_shape=jax.ShapeDtypeStruct((s, d), BF16),
        compiler_params=_params("parallel"),
    )(x, w)


def _rmsnorm_bwd(dh, x, w, dres, name, tm=512):
    s, d = x.shape
    tm = min(tm, s)
    has_res = dres is not None

    def body(*refs):
        if has_res:
            dh_ref, x_ref, w_ref, dres_ref, dx_ref, dxb_ref, gw_ref = refs
        else:
            dh_ref, x_ref, w_ref, dx_ref, dxb_ref, gw_ref = refs

        @pl.when(pl.program_id(0) == 0)
        def _():
            gw_ref[...] = jnp.zeros_like(gw_ref)

        xv = x_ref[...]
        dhv = dh_ref[...].astype(F32)
        r = lax.rsqrt(jnp.mean(xv * xv, axis=-1, keepdims=True) + EPS)
        xhat = xv * r
        gw_ref[...] += _rows8(dhv * xhat)
        dxh = dhv * w_ref[...]
        dx = r * (dxh - xhat * jnp.mean(dxh * xhat, axis=-1, keepdims=True))
        if has_res:
            dx = dres_ref[...] + dx
        dx_ref[...] = dx
        dxb_ref[...] = dx.astype(BF16)

    row = pl.BlockSpec((tm, d), lambda i: (i, 0))
    in_specs = [row, row, pl.BlockSpec((1, d), lambda i: (0, 0))] + ([row] if has_res else [])
    args = (dh, x, w) + ((dres,) if has_res else ())
    return pl.pallas_call(
        body, name=name, grid=(s // tm,),
        in_specs=in_specs,
        out_specs=[row, row, pl.BlockSpec((8, d), lambda i: (0, 0))],
        out_shape=[jax.ShapeDtypeStruct((s, d), F32), jax.ShapeDtypeStruct((s, d), BF16),
                   jax.ShapeDtypeStruct((8, d), F32)],
        compiler_params=_params("arbitrary"),
    )(*args)


def _final_loss_bwd(x3, target, w, name, tm=512):
    s, d = x3.shape
    tm = min(tm, s)

    def body(x_ref, t_ref, w_ref, dx_ref, dxb_ref, gw_ref, loss_ref):
        @pl.when(pl.program_id(0) == 0)
        def _():
            gw_ref[...] = jnp.zeros_like(gw_ref)
            loss_ref[...] = jnp.zeros_like(loss_ref)

        xv = x_ref[...]
        r = lax.rsqrt(jnp.mean(xv * xv, axis=-1, keepdims=True) + EPS)
        xhat = xv * r
        y = xhat * w_ref[...]
        err = y - t_ref[...]
        part = 0.5 * jnp.mean(err * err, axis=-1, keepdims=True)
        tot = jnp.sum(part, axis=0, keepdims=True)
        rr = lax.broadcasted_iota(jnp.int32, loss_ref.shape, 0)
        cc = lax.broadcasted_iota(jnp.int32, loss_ref.shape, 1)
        loss_ref[...] += jnp.where((rr == 0) & (cc == 0), tot, 0.0)
        dy = err * (1.0 / d)
        gw_ref[...] += _rows8(dy * xhat)
        dxh = dy * w_ref[...]
        dx = r * (dxh - xhat * jnp.mean(dxh * xhat, axis=-1, keepdims=True))
        dx_ref[...] = dx
        dxb_ref[...] = dx.astype(BF16)

    row = pl.BlockSpec((tm, d), lambda i: (i, 0))
    return pl.pallas_call(
        body, name=name, grid=(s // tm,),
        in_specs=[row, row, pl.BlockSpec((1, d), lambda i: (0, 0))],
        out_specs=[row, row, pl.BlockSpec((8, d), lambda i: (0, 0)), pl.BlockSpec((8, 128), lambda i: (0, 0))],
        out_shape=[jax.ShapeDtypeStruct((s, d), F32), jax.ShapeDtypeStruct((s, d), BF16),
                   jax.ShapeDtypeStruct((8, d), F32), jax.ShapeDtypeStruct((8, 128), F32)],
        compiler_params=_params("arbitrary"),
    )(x3, target, w)


MM_TILES = (1024, 1408, 512, 256, 128)
MM_K_TILES = (2816, 2048, 1792, 1408, 1024, 512, 256, 128)
MM_VMEM_LIMIT = 56 * 1024 * 1024
MM_VMEM_BUDGET = 46 * 1024 * 1024


def _pick_tile(dim):
    for t in MM_TILES:
        if dim % t == 0:
            return t
    return dim


def _matmul(a, b, mode, name, *, out_dtype=F32, residual=None, extra_bf16=False, tm=None, tn=None, tk=None,
            groups=None):
    if groups is not None and mode == "nt":
        _, m, gw = a.shape
        n, k2 = b.shape
        k, tk = groups * gw, gw
    elif groups is not None and mode == "tn":
        k, m = a.shape
        _, k2, gw = b.shape
        n, tn = groups * gw, gw
    elif mode == "nn":
        (m, k), (k2, n) = a.shape, b.shape
    elif mode == "nt":
        (m, k), (n, k2) = a.shape, b.shape
    else:
        (k, m), (k2, n) = a.shape, b.shape
    assert k == k2, (a.shape, b.shape, mode)
    auto_tm = tm is None
    tm = _pick_tile(m) if tm is None else min(tm, m)
    tn = _pick_tile(n) if tn is None else min(tn, n)
    out_elt = jnp.dtype(out_dtype).itemsize + (2 if extra_bf16 else 0) + (4 if residual is not None else 0)

    def vmem_bytes(t, rows=None):
        rows = tm if rows is None else rows
        return (2 * (rows * t * a.dtype.itemsize + t * tn * b.dtype.itemsize) + 2 * rows * tn * out_elt
                + rows * tn * 4)

    if auto_tm and tk is None and m % (2 * tm) == 0 and vmem_bytes(k, 2 * tm) <= MM_VMEM_BUDGET:
        tm = 2 * tm

    if tk is None:
        tk = next(t for t in MM_K_TILES if k % t == 0 and t <= k and vmem_bytes(t) <= MM_VMEM_BUDGET)
    assert m % tm == 0 and n % tn == 0 and k % tk == 0, (m, n, k, tm, tn, tk)
    nk = k // tk
    dims = {"nn": ((1,), (0,)), "nt": ((1,), (1,)), "tn": ((0,), (0,))}[mode]
    has_res = residual is not None

    def body(*refs):
        refs = list(refs)
        a_ref, b_ref = refs[0], refs[1]
        r_ref = refs[2] if has_res else None
        outs = refs[2 + has_res:]
        o_ref = outs[0]
        o2_ref = outs[1] if extra_bf16 else None
        def finish(r):
            if has_res:
                r = r_ref[...] + r
            o_ref[...] = r.astype(out_dtype)
            if extra_bf16:
                o2_ref[...] = r.astype(BF16)

        if nk == 1:
            finish(_dot(a_ref[...], b_ref[...], dims))
            return
        acc = outs[-1]
        kk = pl.program_id(2)

        @pl.when(kk == 0)
        def _():
            acc[...] = _dot(a_ref[...], b_ref[...], dims)

        if nk > 2:
            @pl.when((kk > 0) & (kk < nk - 1))
            def _():
                acc[...] += _dot(a_ref[...], b_ref[...], dims)

        @pl.when(kk == nk - 1)
        def _():
            finish(acc[...] + _dot(a_ref[...], b_ref[...], dims))

    if mode == "tn":
        a_spec = pl.BlockSpec((tk, tm), lambda i, j, kk: (kk, i))
    elif groups is not None:
        a_spec = pl.BlockSpec((None, tm, tk), lambda i, j, kk: (kk, i, 0))
    else:
        a_spec = pl.BlockSpec((tm, tk), lambda i, j, kk: (i, kk))
    if mode == "nt":
        b_spec = pl.BlockSpec((tn, tk), lambda i, j, kk: (j, kk))
    elif groups is not None:
        b_spec = pl.BlockSpec((None, tk, tn), lambda i, j, kk: (j, kk, 0))
    else:
        b_spec = pl.BlockSpec((tk, tn), lambda i, j, kk: (kk, j))
    o_spec = pl.BlockSpec((tm, tn), lambda i, j, kk: (i, j))
    in_specs = [a_spec, b_spec] + ([o_spec] if has_res else [])
    out_specs = [o_spec] + ([o_spec] if extra_bf16 else [])
    out_shape = [jax.ShapeDtypeStruct((m, n), out_dtype)] + ([jax.ShapeDtypeStruct((m, n), BF16)] if extra_bf16 else [])
    args = (a, b) + ((residual,) if has_res else ())
    res = pl.pallas_call(
        body, name=name, grid=(m // tm, n // tn, nk),
        in_specs=in_specs, out_specs=out_specs, out_shape=out_shape,
        scratch_shapes=[pltpu.VMEM((tm, tn) if nk > 1 else (8, 128), F32)],
        compiler_params=pltpu.CompilerParams(dimension_semantics=("parallel", "parallel", "arbitrary"),
                                             vmem_limit_bytes=MM_VMEM_LIMIT),
    )(*args)
    return res if extra_bf16 else res[0]


def _hgrn_gates(qp, fp, lb):
    sig = _sigmoid(fp)
    f = lb + (1.0 - lb) * sig
    logf = jnp.log(f)
    k = 1.0 - f
    sq = _sigmoid(qp)
    q = qp * sq
    return sig, f, logf, k, sq, q


def _hgrn_fwd(proj, lb0, lb1, norm_w, name, tb=512):
    s = proj.shape[0]
    tb = min(tb, s)
    nb, ncb = s // tb, tb // CHUNK

    def body(q_ref, f_ref, i_ref, g_ref, a0_ref, a1_ref, nw_ref, o_ref, og_ref, st_ref, state):
        @pl.when(pl.program_id(1) == 0)
        def _():
            state[...] = jnp.zeros_like(state)

        lb2 = _sigmoid(a0_ref[...] - a1_ref[...])
        row = lax.broadcasted_iota(jnp.int32, (CHUNK, CHUNK), 0)
        col = lax.broadcasted_iota(jnp.int32, (CHUNK, CHUNK), 1)
        tril = row >= col
        ones_l = tril.astype(BF16)
        nw = nw_ref[...]

        def chunk(c, carry):
            rows = pl.ds(pl.multiple_of(c * CHUNK, CHUNK), CHUNK)
            for hh in range(HGRN_HEADS_PER_STEP):
                cols = slice(hh * HEAD, (hh + 1) * HEAD)
                v = i_ref[rows, cols].astype(F32)
                _, _, logf, k, _, q = _hgrn_gates(q_ref[rows, cols].astype(F32), f_ref[rows, cols].astype(F32),
                                                  lb2[:, cols])
                b = _exact_ones_dot(ones_l, logf)
                bl = jnp.sum(logf, axis=0, keepdims=True)
                bm = 0.5 * bl
                st = state[hh]
                st_ref[hh, c] = st
                qt = q * jnp.exp(b - bm)
                kt = k * jnp.exp(bm - b)
                a = jnp.where(tril, _dot_nt(qt, kt), 0.0)
                o = _dot_nt(q * jnp.exp(b), st) + _dot_nn(a, v)
                state[hh] = st * jnp.exp(bl) + _dot_tn(v, k * jnp.exp(bl - b))
                o_ref[rows, cols] = o
                on = (o * lax.rsqrt(jnp.mean(o * o, axis=-1, keepdims=True) + EPS)) * nw
                gv = g_ref[rows, cols].astype(F32)
                og_ref[rows, cols] = (on * (gv * _sigmoid(gv))).astype(BF16)
            return carry

        lax.fori_loop(0, ncb, chunk, 0, unroll=HGRN_UNROLL)

    hp, wd = HGRN_HEADS_PER_STEP, HGRN_HEADS_PER_STEP * HEAD
    ngrp = N_HEADS // hp

    def colblk(group):
        return pl.BlockSpec((tb, wd), lambda h, j: (j, group * ngrp + h))

    vec = pl.BlockSpec((1, wd), lambda h, j: (0, h))
    out_blk = pl.BlockSpec((tb, wd), lambda h, j: (j, h))
    return pl.pallas_call(
        body, name=name, grid=(ngrp, nb),
        in_specs=[colblk(0), colblk(1), colblk(2), colblk(3), vec, vec, pl.BlockSpec((1, HEAD), lambda h, j: (0, 0))],
        out_specs=[out_blk, out_blk, pl.BlockSpec((hp, ncb, HEAD, HEAD), lambda h, j: (h, j, 0, 0))],
        out_shape=[jax.ShapeDtypeStruct((s, HGRN_W), F32), jax.ShapeDtypeStruct((s, 2 * HGRN_W), BF16),
                   jax.ShapeDtypeStruct((N_HEADS, s // CHUNK, HEAD, HEAD), F32)],
        scratch_shapes=[pltpu.VMEM((hp, HEAD, HEAD), F32)],
        compiler_params=_params("parallel", "arbitrary"),
    )(proj, proj, proj, proj, lb0, lb1, norm_w)


def _hgrn_bwd(proj, lb0, lb1, norm_w, o, states, dmix, name, tb=512):
    s = proj.shape[0]
    tb = min(tb, s)
    nb, ncb = s // tb, tb // CHUNK

    def body(q_ref, f_ref, i_ref, g_ref, a0_ref, a1_ref, nw_ref, o_ref, st_ref, dm_ref,
             dp_ref, glb_ref, gnw_ref, dstate):
        h = pl.program_id(0)

        @pl.when(pl.program_id(1) == 0)
        def _():
            dstate[...] = jnp.zeros_like(dstate)
            glb_ref[...] = jnp.zeros_like(glb_ref)

        @pl.when((pl.program_id(1) == 0) & (h == 0))
        def _():
            gnw_ref[...] = jnp.zeros_like(gnw_ref)

        lb2 = _sigmoid(a0_ref[...] - a1_ref[...])
        row = lax.broadcasted_iota(jnp.int32, (CHUNK, CHUNK), 0)
        col = lax.broadcasted_iota(jnp.int32, (CHUNK, CHUNK), 1)
        tril = row >= col
        ones_l = tril.astype(BF16)
        ones_u = (row <= col).astype(BF16)
        nw = nw_ref[...]

        def chunk(cc, carry):
            c = ncb - 1 - cc
            rows = pl.ds(pl.multiple_of(c * CHUNK, CHUNK), CHUNK)
            for hh in range(HGRN_HEADS_PER_STEP):
                cols = slice(hh * HEAD, (hh + 1) * HEAD)
                lb = lb2[:, cols]
                qp = q_ref[rows, cols].astype(F32)
                v = i_ref[rows, cols].astype(F32)
                sig, f, logf, k, sq, q = _hgrn_gates(qp, f_ref[rows, cols].astype(F32), lb)
                gv = g_ref[rows, cols].astype(F32)
                sg = _sigmoid(gv)
                silu_g = gv * sg
                dog = dm_ref[rows, cols].astype(F32)
                ov = o_ref[rows, cols]
                r = lax.rsqrt(jnp.mean(ov * ov, axis=-1, keepdims=True) + EPS)
                ohat = ov * r
                on = ohat * nw
                dp_ref[3, rows, cols] = (dog * on * (sg * (1.0 + gv * (1.0 - sg)))).astype(BF16)
                don = dog * silu_g
                gnw_ref[...] += _rows8(don * ohat)
                doh = don * nw
                do = r * (doh - ohat * jnp.mean(doh * ohat, axis=-1, keepdims=True))
                b = _exact_ones_dot(ones_l, logf)
                bl = jnp.sum(logf, axis=0, keepdims=True)
                bm = 0.5 * bl
                e_q = jnp.exp(b - bm)
                e_k = jnp.exp(bm - b)
                e_b = jnp.exp(b)
                e_l = jnp.exp(bl - b)
                qt, kt, qb, kb = q * e_q, k * e_k, q * e_b, k * e_l
                st0 = st_ref[hh, c]
                dst = dstate[hh]
                a = jnp.where(tril, _dot_nt(qt, kt), 0.0)
                da = jnp.where(tril, _dot_nt(do, v), 0.0)
                dq = _hdot_nn(da, kt) * e_q + _hdot_nn(do, st0) * e_b
                dkb = _hdot_nn(v, dst) * e_l
                dk = _hdot_tn(da, qt) * e_k + dkb
                dv = _dot_tn(a, do) + _dot_nt(kb, dst)
                e_bl = jnp.exp(bl)
                dstate[hh] = dst * e_bl + _dot_tn(do, qb)
                db = q * dq - k * dk
                db_last = jnp.sum(k * dkb, axis=0, keepdims=True) + e_bl * jnp.sum(st0 * dst, axis=0, keepdims=True)
                dlogf = _exact_ones_dot(ones_u, db) + db_last
                dfg = dlogf / f - dk
                dp_ref[1, rows, cols] = (dfg * (1.0 - lb) * (sig * (1.0 - sig))).astype(BF16)
                glb_ref[:, cols] += _rows8(dfg * (1.0 - sig)) * (lb * (1.0 - lb))
                dp_ref[0, rows, cols] = (dq * (sq * (1.0 + qp * (1.0 - sq)))).astype(BF16)
                dp_ref[2, rows, cols] = dv.astype(BF16)
            return carry

        lax.fori_loop(0, ncb, chunk, 0, unroll=HGRN_UNROLL)

    hp, wd = HGRN_HEADS_PER_STEP, HGRN_HEADS_PER_STEP * HEAD
    ngrp = N_HEADS // hp

    def colblk(group):
        return pl.BlockSpec((tb, wd), lambda h, j: (nb - 1 - j, group * ngrp + h))

    vec = pl.BlockSpec((1, wd), lambda h, j: (0, h))
    blk = pl.BlockSpec((tb, wd), lambda h, j: (nb - 1 - j, h))
    return pl.pallas_call(
        body, name=name, grid=(ngrp, nb),
        in_specs=[colblk(0), colblk(1), colblk(2), colblk(3), vec, vec, pl.BlockSpec((1, HEAD), lambda h, j: (0, 0)),
                  blk, pl.BlockSpec((hp, ncb, HEAD, HEAD), lambda h, j: (h, nb - 1 - j, 0, 0)), blk],
        out_specs=[pl.BlockSpec((4, tb, wd), lambda h, j: (0, nb - 1 - j, h)),
                   pl.BlockSpec((8, wd), lambda h, j: (0, h)), pl.BlockSpec((8, HEAD), lambda h, j: (0, 0))],
        out_shape=[jax.ShapeDtypeStruct((DPROJ_GROUPS, s, HGRN_W), BF16),
                   jax.ShapeDtypeStruct((8, HGRN_W), F32), jax.ShapeDtypeStruct((8, HEAD), F32)],
        scratch_shapes=[pltpu.VMEM((hp, HEAD, HEAD), F32)],
        compiler_params=_params("arbitrary", "arbitrary"),
    )(proj, proj, proj, proj, lb0, lb1, norm_w, o, states, dmix)


HALO_BLK = 16


def _f32(ref):
    return ref[...].astype(F32)


def _halo_prev(ref):
    return ref[...].astype(F32)[HALO_BLK - HALO:]


def _halo_next(ref):
    return ref[...].astype(F32)[:HALO]


def _conv3(x0, x1, x2, w_ref):
    y = x0 * w_ref[0:1, :]
    y = y + x1 * w_ref[1:2, :]
    return y + x2 * w_ref[2:3, :]


def _sconv_fwd(proj, w8, mix, name, tb=512):
    s = proj.shape[0]
    tb = min(tb, s)
    hb = tb // HALO_BLK

    def body(cb_ref, cc_ref, ch_ref, cch_ref, chh_ref, w_ref, mix_ref, y_ref):
        first = pl.program_id(0) == 0
        u = _f32(cc_ref) * _f32(ch_ref)
        uh = jnp.where(first, 0.0, _halo_prev(cch_ref) * _halo_prev(chh_ref))
        conv = _conv3(_shift_down(u, uh, 2), _shift_down(u, uh, 1), u, w_ref)
        y_ref[...] = (_f32(cb_ref) * conv).astype(BF16)

    def blk(g):
        return pl.BlockSpec((tb, HGRN_W), lambda j: (j, g))

    def halo(g):
        return pl.BlockSpec((HALO_BLK, HGRN_W), lambda j: (jnp.maximum(j * hb - 1, 0), g))

    return pl.pallas_call(
        body, name=name, grid=(s // tb,),
        in_specs=[blk(4), blk(5), blk(6), halo(5), halo(6), pl.BlockSpec((HALO, HGRN_W), lambda j: (0, 0)),
                  pl.BlockSpec(memory_space=pl.ANY)],
        out_specs=pl.BlockSpec((tb, HGRN_W), lambda j: (j, 1)),
        out_shape=jax.ShapeDtypeStruct(mix.shape, BF16),
        input_output_aliases={6: 0},
        compiler_params=_params("parallel"),
    )(proj, proj, proj, proj, proj, w8, mix)


def _sconv_bwd(proj, w8, dmix, dproj, name, tb=512):
    s = proj.shape[0]
    tb = min(tb, s)
    hb = tb // HALO_BLK
    nb = s // tb
    last_h = s // HALO_BLK - 1

    def body(cb_ref, cc_ref, ch_ref, cch_ref, chh_ref, cbn_ref, dy_ref, dyn_ref, w_ref, dproj_ref,
             dp_ref, gw_ref):
        j = pl.program_id(0)

        @pl.when(j == 0)
        def _():
            gw_ref[...] = jnp.zeros_like(gw_ref)

        cc, ch, cb = _f32(cc_ref), _f32(ch_ref), _f32(cb_ref)
        u = cc * ch
        uh = jnp.where(j == 0, 0.0, _halo_prev(cch_ref) * _halo_prev(chh_ref))
        u2, u1 = _shift_down(u, uh, 2), _shift_down(u, uh, 1)
        conv = _conv3(u2, u1, u, w_ref)
        dy = _f32(dy_ref)
        dp_ref[0] = (dy * conv).astype(BF16)
        dc = dy * cb
        dcn = jnp.where(j == nb - 1, 0.0, _halo_next(dyn_ref) * _halo_next(cbn_ref))
        gw_ref[0:8, :] += _rows8(dc * u2)
        gw_ref[8:16, :] += _rows8(dc * u1)
        gw_ref[16:24, :] += _rows8(dc * u)
        du = dc * w_ref[2:3, :] + _shift_up(dc, dcn, 1) * w_ref[1:2, :] + _shift_up(dc, dcn, 2) * w_ref[0:1, :]
        dp_ref[1] = (du * ch).astype(BF16)
        dp_ref[2] = (du * cc).astype(BF16)
        dp_ref[3] = jnp.zeros(dp_ref.shape[1:], BF16)

    def blk(g):
        return pl.BlockSpec((tb, HGRN_W), lambda j: (j, g))

    def halo_prev(g):
        return pl.BlockSpec((HALO_BLK, HGRN_W), lambda j: (jnp.maximum(j * hb - 1, 0), g))

    def halo_next(g):
        return pl.BlockSpec((HALO_BLK, HGRN_W), lambda j: (jnp.minimum((j + 1) * hb, last_h), g))

    return pl.pallas_call(
        body, name=name, grid=(nb,),
        in_specs=[blk(4), blk(5), blk(6), halo_prev(5), halo_prev(6), halo_next(4), blk(1), halo_next(1),
                  pl.BlockSpec((HALO, HGRN_W), lambda j: (0, 0)), pl.BlockSpec(memory_space=pl.ANY)],
        out_specs=[pl.BlockSpec((4, tb, HGRN_W), lambda j: (1, j, 0)), pl.BlockSpec((24, HGRN_W), lambda j: (0, 0))],
        out_shape=[jax.ShapeDtypeStruct(dproj.shape, BF16), jax.ShapeDtypeStruct((24, HGRN_W), F32)],
        input_output_aliases={9: 0},
        compiler_params=_params("arbitrary"),
    )(proj, proj, proj, proj, proj, proj, dmix, dmix, w8, dproj)


def _attn_fwd(q, kk, vv, name, tb=1024):
    s, d = q.shape
    m = kk.shape[0]
    tb = min(tb, s)
    scale = MEM_HEAD_DIM ** -0.5

    def body(q_ref, k_ref, v_ref, o_ref):
        for hh in range(MEM_HEADS):
            cols = slice(hh * MEM_HEAD_DIM, (hh + 1) * MEM_HEAD_DIM)
            sc = _dot_nt(q_ref[:, cols], k_ref[:, cols]) * scale
            sc = sc - jnp.max(sc, axis=-1, keepdims=True)
            e = jnp.exp(sc)
            p = e / jnp.sum(e, axis=-1, keepdims=True)
            o_ref[:, cols] = _dot_nn(p, v_ref[:, cols]).astype(BF16)

    full = pl.BlockSpec((m, d), lambda i: (0, 0))
    return pl.pallas_call(
        body, name=name, grid=(s // tb,),
        in_specs=[pl.BlockSpec((tb, d), lambda i: (i, 0)), full, full],
        out_specs=pl.BlockSpec((tb, d), lambda i: (i, 0)),
        out_shape=jax.ShapeDtypeStruct((s, d), BF16),
        compiler_params=pltpu.CompilerParams(dimension_semantics=("parallel",), vmem_limit_bytes=MM_VMEM_LIMIT),
    )(q, kk, vv)


def _attn_bwd(q, kk, vv, datt, name, tb=1024):
    s, d = q.shape
    m = kk.shape[0]
    tb = min(tb, s)
    scale = MEM_HEAD_DIM ** -0.5

    def body(q_ref, k_ref, v_ref, do_ref, dq_ref, dk_ref, dv_ref):
        @pl.when(pl.program_id(0) == 0)
        def _():
            dk_ref[...] = jnp.zeros_like(dk_ref)
            dv_ref[...] = jnp.zeros_like(dv_ref)

        for hh in range(MEM_HEADS):
            cols = slice(hh * MEM_HEAD_DIM, (hh + 1) * MEM_HEAD_DIM)
            qh, kh, vh, doh = q_ref[:, cols], k_ref[:, cols], v_ref[:, cols], do_ref[:, cols]
            sc = _dot_nt(qh, kh) * scale
            sc = sc - jnp.max(sc, axis=-1, keepdims=True)
            e = jnp.exp(sc)
            p = e / jnp.sum(e, axis=-1, keepdims=True)
            dp = _dot_nt(doh, vh)
            ds = p * (dp - jnp.sum(dp * p, axis=-1, keepdims=True)) * scale
            dq_ref[:, cols] = _dot_nn(ds, kh).astype(BF16)
            dk_ref[:, cols] += _dot_tn(ds, qh)
            dv_ref[:, cols] += _dot_tn(p, doh)

    full = pl.BlockSpec((m, d), lambda i: (0, 0))
    row = pl.BlockSpec((tb, d), lambda i: (i, 0))
    return pl.pallas_call(
        body, name=name, grid=(s // tb,),
        in_specs=[row, full, full, row],
        out_specs=[row, full, full],
        out_shape=[jax.ShapeDtypeStruct((s, d), BF16), jax.ShapeDtypeStruct((m, d), F32),
                   jax.ShapeDtypeStruct((m, d), F32)],
        compiler_params=pltpu.CompilerParams(dimension_semantics=("arbitrary",), vmem_limit_bytes=MM_VMEM_LIMIT),
    )(q, kk, vv, datt)


def _ffn_fwd(g, u, w8, bias, name, tb=512, tc=1408):
    s, f = g.shape
    tb = min(tb, s)
    tc = tc if f % tc == 0 else 512
    hb = tb // HALO_BLK

    def body(g_ref, gh_ref, u_ref, w_ref, b_ref, z_ref, a_ref):
        gv = _f32(g_ref)
        gh = jnp.where(pl.program_id(1) == 0, 0.0, _halo_prev(gh_ref))
        a = _conv3(_shift_down(gv, gh, 2), _shift_down(gv, gh, 1), gv, w_ref) + b_ref[...]
        a_ref[...] = a.astype(BF16)
        z_ref[...] = ((a * _sigmoid(a)) * _f32(u_ref)).astype(BF16)

    blk = pl.BlockSpec((tb, tc), lambda c, j: (j, c))
    return pl.pallas_call(
        body, name=name, grid=(f // tc, s // tb),
        in_specs=[blk, pl.BlockSpec((HALO_BLK, tc), lambda c, j: (jnp.maximum(j * hb - 1, 0), c)), blk,
                  pl.BlockSpec((HALO, tc), lambda c, j: (0, c)), pl.BlockSpec((1, tc), lambda c, j: (0, c))],
        out_specs=[blk, blk],
        out_shape=[jax.ShapeDtypeStruct((s, f), BF16), jax.ShapeDtypeStruct((s, f), BF16)],
        compiler_params=pltpu.CompilerParams(dimension_semantics=("parallel", "parallel"),
                                             vmem_limit_bytes=MM_VMEM_LIMIT),
    )(g, g, u, w8, bias)


def _ffn_bwd(a, g, u, dz, w8, name, tb=512, tc=1408):
    s, f = g.shape
    tb = min(tb, s)
    tc = tc if f % tc == 0 else 512
    nb = s // tb

    def body(a_ref, g_ref, u_ref, dz_ref, w_ref, dg_ref, du_ref, gb_ref, gw_ref, da_next):
        jj = pl.program_id(1)

        @pl.when(jj == 0)
        def _():
            gb_ref[...] = jnp.zeros_like(gb_ref)
            gw_ref[...] = jnp.zeros_like(gw_ref)
            da_next[...] = jnp.zeros_like(da_next)

        a = _f32(a_ref)
        sa = _sigmoid(a)
        dz = _f32(dz_ref)
        du_ref[...] = (dz * (a * sa)).astype(BF16)
        da = dz * _f32(u_ref) * (sa * (1.0 + a * (1.0 - sa)))
        gb_ref[...] += _rows8(da)
        dan = da_next[...]
        da1, da2 = _shift_up(da, dan, 1), _shift_up(da, dan, 2)
        gv = _f32(g_ref)
        gw_ref[0:8, :] += _rows8(da2 * gv)
        gw_ref[8:16, :] += _rows8(da1 * gv)
        gw_ref[16:24, :] += _rows8(da * gv)
        dg_ref[...] = (da * w_ref[2:3, :] + da1 * w_ref[1:2, :] + da2 * w_ref[0:1, :]).astype(BF16)
        da_next[...] = da[:HALO]

    blk = pl.BlockSpec((tb, tc), lambda c, jj: (nb - 1 - jj, c))
    return pl.pallas_call(
        body, name=name, grid=(f // tc, nb),
        in_specs=[blk, blk, blk, blk, pl.BlockSpec((HALO, tc), lambda c, jj: (0, c))],
        out_specs=[blk, blk, pl.BlockSpec((8, tc), lambda c, jj: (0, c)), pl.BlockSpec((24, tc), lambda c, jj: (0, c))],
        out_shape=[jax.ShapeDtypeStruct((s, f), BF16), jax.ShapeDtypeStruct((s, f), BF16),
                   jax.ShapeDtypeStruct((8, f), F32), jax.ShapeDtypeStruct((24, f), F32)],
        scratch_shapes=[pltpu.VMEM((HALO, tc), F32)],
        compiler_params=pltpu.CompilerParams(dimension_semantics=("parallel", "arbitrary"),
                                             vmem_limit_bytes=MM_VMEM_LIMIT),
    )(a, g, u, dz, w8)


def _window(ref, axis, slot, size):
    start = pl.multiple_of(slot * size, size)
    if axis == 0:
        return ref.at[pl.ds(start, size), :]
    return ref.at[:, pl.ds(start, size)]


def _chip_peers():
    x, y, c = lax.axis_index("x"), lax.axis_index("y"), lax.axis_index("c")
    peers = [(1 - x, y, c), (x, 1 - y, c), (1 - x, 1 - y, c)]
    slots = [2 * (1 - x) + y, 2 * x + (1 - y), 2 * (1 - x) + (1 - y)]
    return 2 * x + y, peers, slots


HBM_SPEC = pl.BlockSpec(memory_space=pltpu.HBM)
SEM_SPEC = pl.BlockSpec(memory_space=pltpu.SEMAPHORE)
EFFECT = pltpu.SideEffectType.DATAFLOW_SIDE_EFFECTING


def _hbm(a):
    return pltpu.with_memory_space_constraint(a, pltpu.HBM)


def _cast_into_full(x, axis, slot_arr, dtype, name, after=None):
    r, c = x.shape
    tr = _row_tile(r, 256)
    nb = r // tr
    full = (r * N_CHIPS, c) if axis == 0 else (r, c * N_CHIPS)

    def body(slot_ref, x_ref, *rest):
        rest[-1][...] = x_ref[...].astype(dtype)

    if axis == 0:
        out_map = lambda i, s: (s[0] * nb + i, 0)
    else:
        out_map = lambda i, s: (i, s[0])
    extra = [] if after is None else [after]
    return pl.pallas_call(
        body, name=name,
        grid_spec=pltpu.PrefetchScalarGridSpec(
            num_scalar_prefetch=1, grid=(nb,),
            in_specs=[pl.BlockSpec((tr, c), lambda i, s: (i, 0))] + [pl.BlockSpec(memory_space=pl.ANY)] * len(extra),
            out_specs=pl.BlockSpec((tr, c), out_map)),
        out_shape=jax.ShapeDtypeStruct(full, dtype),
        compiler_params=_params("parallel"),
    )(slot_arr, x, *extra)


def _piece(ref, axis, slot, half):
    size = ref.shape[axis] // N_CHIPS
    if half is None:
        return _window(ref, axis, slot, size)
    if axis == 0:
        h = size // 2
        return ref.at[pl.ds(pl.multiple_of(slot * size + half * h, h), h), :]
    h = ref.shape[0] // 2
    return ref.at[pl.ds(pl.multiple_of(half * h, h), h), pl.ds(pl.multiple_of(slot * size, size), size)]


def _gather_start(fulls, axes, split, groups, name):
    n, ng = len(fulls), len(groups)

    def body(*refs):
        outs = refs[n:]
        sems = outs[:2 * ng]
        thru = outs[2 * ng:2 * ng + n]
        token = outs[-1]
        slot, peers, _ = _chip_peers()
        c = lax.axis_index("c")
        for g, members in enumerate(groups):
            for i, t in enumerate(members):
                mine = _piece(thru[t], axes[t], slot, c if split[t] else None)
                for k in range(3):
                    pltpu.make_async_remote_copy(
                        src_ref=mine, dst_ref=mine, send_sem=sems[2 * g].at[3 * i + k],
                        recv_sem=sems[2 * g + 1].at[3 * i + k], device_id=peers[k], device_id_type=MESH).start()
        token[...] = jnp.zeros_like(token)

    sem_shapes = []
    for members in groups:
        sem_shapes += [pltpu.SemaphoreType.DMA((3 * len(members),))] * 2
    res = pl.pallas_call(
        body, name=name,
        in_specs=[HBM_SPEC] * n,
        out_specs=[SEM_SPEC] * (2 * ng) + [HBM_SPEC] * n + [pl.BlockSpec(memory_space=pltpu.VMEM)],
        out_shape=sem_shapes + [pltpu.HBM(f.shape, f.dtype) for f in fulls] + [jax.ShapeDtypeStruct((8, 128), F32)],
        input_output_aliases={t: 2 * ng + t for t in range(n)},
        compiler_params=pltpu.CompilerParams(has_side_effects=EFFECT),
    )(*[_hbm(f) for f in fulls])
    sems = [(res[2 * g], res[2 * g + 1]) for g in range(ng)]
    return sems, list(res[2 * ng:2 * ng + n]), res[-1]


def _gather_relay(fulls, axes, split, sems, after, name):
    n = len(fulls)
    nsplit = sum(split)

    def body(*refs):
        send_sems, recv_sems = refs[n], refs[n + 1]
        outs = refs[n + 3:]
        d_send, d_recv = outs[0], outs[1]
        thru = outs[2:2 + n]
        token = outs[-1]
        slot, peers, slots = _chip_peers()
        c = lax.axis_index("c")
        sibling = (lax.axis_index("x"), lax.axis_index("y"), 1 - c)
        for t in range(n):
            half = c if split[t] else None
            for k in range(3):
                cp = pltpu.make_async_remote_copy(
                    src_ref=_piece(thru[t], axes[t], slot, half), dst_ref=_piece(thru[t], axes[t], slots[k], half),
                    send_sem=send_sems.at[3 * t + k], recv_sem=recv_sems.at[3 * t + k],
                    device_id=peers[k], device_id_type=MESH)
                cp.wait_send()
                cp.wait_recv()
        i = 0
        for t in range(n):
            if not split[t]:
                continue
            for k in range(3):
                got = _piece(thru[t], axes[t], slots[k], c)
                pltpu.make_async_remote_copy(
                    src_ref=got, dst_ref=got, send_sem=d_send.at[3 * i + k], recv_sem=d_recv.at[3 * i + k],
                    device_id=sibling, device_id_type=MESH).start()
            i += 1
        token[...] = jnp.zeros_like(token)

    res = pl.pallas_call(
        body, name=name,
        in_specs=[HBM_SPEC] * n + [SEM_SPEC, SEM_SPEC, pl.BlockSpec(memory_space=pl.ANY)],
        out_specs=[SEM_SPEC, SEM_SPEC] + [HBM_SPEC] * n + [pl.BlockSpec(memory_space=pltpu.VMEM)],
        out_shape=[pltpu.SemaphoreType.DMA((3 * nsplit,)), pltpu.SemaphoreType.DMA((3 * nsplit,))]
        + [pltpu.HBM(f.shape, f.dtype) for f in fulls] + [jax.ShapeDtypeStruct((8, 128), F32)],
        input_output_aliases={t: 2 + t for t in range(n)},
        compiler_params=pltpu.CompilerParams(has_side_effects=EFFECT),
    )(*fulls, sems[0], sems[1], after)
    return (res[0], res[1]), list(res[2:2 + n]), res[-1]


def _gather_finish(fulls, axes, split, sems, after, name):
    n = len(fulls)

    def body(*refs):
        d_send, d_recv = refs[n], refs[n + 1]
        thru = refs[n + 3:]
        _, _, slots = _chip_peers()
        c = lax.axis_index("c")
        sibling = (lax.axis_index("x"), lax.axis_index("y"), 1 - c)
        i = 0
        for t in range(n):
            if not split[t]:
                continue
            for k in range(3):
                cp = pltpu.make_async_remote_copy(
                    src_ref=_piece(thru[t], axes[t], slots[k], c), dst_ref=_piece(thru[t], axes[t], slots[k], 1 - c),
                    send_sem=d_send.at[3 * i + k], recv_sem=d_recv.at[3 * i + k],
                    device_id=sibling, device_id_type=MESH)
                cp.wait_send()
                cp.wait_recv()
            i += 1

    return pl.pallas_call(
        body, name=name,
        in_specs=[HBM_SPEC] * n + [SEM_SPEC, SEM_SPEC, pl.BlockSpec(memory_space=pl.ANY)],
        out_specs=[HBM_SPEC] * n,
        out_shape=[pltpu.HBM(f.shape, f.dtype) for f in fulls],
        input_output_aliases={t: t for t in range(n)},
        compiler_params=pltpu.CompilerParams(has_side_effects=EFFECT),
    )(*fulls, sems[0], sems[1], after)


def _scatter_start(grads_bf16, axes, name):
    n = len(grads_bf16)

    def shard_shape(g, ax):
        return (g.shape[0] // N_CHIPS, g.shape[1]) if ax == 0 else (g.shape[0], g.shape[1] // N_CHIPS)

    shapes = [shard_shape(g, ax) for g, ax in zip(grads_bf16, axes)]

    def body(*refs):
        outs = refs[2 * n:]
        send_sems, recv_sems = outs[0], outs[1]
        gb, land = outs[2:2 + n], outs[2 + n:2 + 2 * n]
        token = outs[-1]
        _, peers, slots = _chip_peers()
        for t in range(n):
            size = shapes[t][axes[t]]
            for k in range(3):
                pltpu.make_async_remote_copy(
                    src_ref=_window(gb[t], axes[t], slots[k], size), dst_ref=land[t].at[k],
                    send_sem=send_sems.at[3 * t + k], recv_sem=recv_sems.at[3 * t + k],
                    device_id=peers[k], device_id_type=MESH).start()
        token[...] = jnp.zeros_like(token)

    lands = [_hbm(lax.empty((3,) + sh, BF16)) for sh in shapes]
    res = pl.pallas_call(
        body, name=name,
        in_specs=[HBM_SPEC] * (2 * n),
        out_specs=[SEM_SPEC, SEM_SPEC] + [HBM_SPEC] * (2 * n) + [pl.BlockSpec(memory_space=pltpu.VMEM)],
        out_shape=[pltpu.SemaphoreType.DMA((3 * n,)), pltpu.SemaphoreType.DMA((3 * n,))]
        + [pltpu.HBM(g.shape, g.dtype) for g in grads_bf16] + [pltpu.HBM((3,) + sh, BF16) for sh in shapes]
        + [jax.ShapeDtypeStruct((8, 128), F32)],
        input_output_aliases={t: 2 + t for t in range(2 * n)},
        compiler_params=pltpu.CompilerParams(has_side_effects=EFFECT),
    )(*[_hbm(g) for g in grads_bf16], *lands)
    return (res[0], res[1]), list(res[2:2 + n]), list(res[2 + n:2 + 2 * n]), res[-1]


def _scatter_wait(grads_thru, lands_thru, axes, sems, after, name):
    n = len(grads_thru)

    def body(*refs):
        send_sems, recv_sems = refs[2 * n], refs[2 * n + 1]
        outs = refs[2 * n + 3:]
        gb, land = outs[:n], outs[n:]
        _, peers, slots = _chip_peers()
        for t in range(n):
            size = land[t].shape[1 + axes[t]]
            for k in range(3):
                cp = pltpu.make_async_remote_copy(
                    src_ref=_window(gb[t], axes[t], slots[k], size), dst_ref=land[t].at[k],
                    send_sem=send_sems.at[3 * t + k], recv_sem=recv_sems.at[3 * t + k],
                    device_id=peers[k], device_id_type=MESH)
                cp.wait_send()
                cp.wait_recv()

    res = pl.pallas_call(
        body, name=name,
        in_specs=[HBM_SPEC] * (2 * n) + [SEM_SPEC, SEM_SPEC, pl.BlockSpec(memory_space=pl.ANY)],
        out_specs=[HBM_SPEC] * (2 * n),
        out_shape=[pltpu.HBM(g.shape, g.dtype) for g in grads_thru] + [pltpu.HBM(l.shape, l.dtype) for l in lands_thru],
        input_output_aliases={t: t for t in range(2 * n)},
        compiler_params=pltpu.CompilerParams(has_side_effects=EFFECT),
    )(*grads_thru, *lands_thru, sems[0], sems[1], after)
    return list(res[n:])


def _sibling_start(arrs, name):
    n = len(arrs)

    def body(*refs):
        outs = refs[2 * n:]
        send_sems, recv_sems = outs[0], outs[1]
        src, land = outs[2:2 + n], outs[2 + n:2 + 2 * n]
        token = outs[-1]
        sibling = (lax.axis_index("x"), lax.axis_index("y"), 1 - lax.axis_index("c"))
        for t in range(n):
            pltpu.make_async_remote_copy(
                src_ref=src[t], dst_ref=land[t], send_sem=send_sems.at[t], recv_sem=recv_sems.at[t],
                device_id=sibling, device_id_type=MESH).start()
        token[...] = jnp.zeros_like(token)

    lands = [_hbm(lax.empty(a.shape, a.dtype)) for a in arrs]
    res = pl.pallas_call(
        body, name=name,
        in_specs=[HBM_SPEC] * (2 * n),
        out_specs=[SEM_SPEC, SEM_SPEC] + [HBM_SPEC] * (2 * n) + [pl.BlockSpec(memory_space=pltpu.VMEM)],
        out_shape=[pltpu.SemaphoreType.DMA((n,)), pltpu.SemaphoreType.DMA((n,))]
        + [pltpu.HBM(a.shape, a.dtype) for a in arrs] * 2 + [jax.ShapeDtypeStruct((8, 128), F32)],
        input_output_aliases={t: 2 + t for t in range(2 * n)},
        compiler_params=pltpu.CompilerParams(has_side_effects=EFFECT),
    )(*[_hbm(a) for a in arrs], *lands)
    return (res[0], res[1]), list(res[2:2 + n]), list(res[2 + n:2 + 2 * n]), res[-1]


def _sibling_wait(src_thru, lands_thru, sems, after, name):
    n = len(src_thru)

    def body(*refs):
        send_sems, recv_sems = refs[2 * n], refs[2 * n + 1]
        outs = refs[2 * n + 3:]
        src, land = outs[:n], outs[n:]
        sibling = (lax.axis_index("x"), lax.axis_index("y"), 1 - lax.axis_index("c"))
        for t in range(n):
            cp = pltpu.make_async_remote_copy(
                src_ref=src[t], dst_ref=land[t], send_sem=send_sems.at[t], recv_sem=recv_sems.at[t],
                device_id=sibling, device_id_type=MESH)
            cp.wait_send()
            cp.wait_recv()

    res = pl.pallas_call(
        body, name=name,
        in_specs=[HBM_SPEC] * (2 * n) + [SEM_SPEC, SEM_SPEC, pl.BlockSpec(memory_space=pl.ANY)],
        out_specs=[HBM_SPEC] * (2 * n),
        out_shape=[pltpu.HBM(a.shape, a.dtype) for a in src_thru] * 2,
        input_output_aliases={t: t for t in range(2 * n)},
        compiler_params=pltpu.CompilerParams(has_side_effects=EFFECT),
    )(*src_thru, *lands_thru, sems[0], sems[1], after)
    return list(res[:n]), list(res[n:])


def _all_reduce_small(packed, name):
    nc = packed.shape[1]
    vmem = pl.BlockSpec(memory_space=pltpu.VMEM)

    def body(in_ref, out_ref, gbuf, send_sems, recv_sems):
        x, y, c = lax.axis_index("x"), lax.axis_index("y"), lax.axis_index("c")
        me = 4 * x + 2 * y + c
        gbuf[me] = jnp.sum(in_ref[...], axis=0, keepdims=True)
        copies = []
        for k in range(1, 8):
            peer = (x ^ ((k >> 2) & 1), y ^ ((k >> 1) & 1), c ^ (k & 1))
            rc = pltpu.make_async_remote_copy(
                src_ref=gbuf.at[me], dst_ref=gbuf.at[me], send_sem=send_sems.at[k - 1], recv_sem=recv_sems.at[k - 1],
                device_id=peer, device_id_type=MESH)
            rc.start()
            copies.append(rc)
        for k in range(1, 8):
            peer = (x ^ ((k >> 2) & 1), y ^ ((k >> 1) & 1), c ^ (k & 1))
            pltpu.make_async_remote_copy(
                src_ref=gbuf.at[me], dst_ref=gbuf.at[me ^ k], send_sem=send_sems.at[k - 1],
                recv_sem=recv_sems.at[k - 1], device_id=peer, device_id_type=MESH).wait_recv()
        for rc in copies:
            rc.wait_send()
        tot = gbuf[0]
        for d in range(1, 8):
            tot = tot + gbuf[d]
        out_ref[...] = tot

    return pl.pallas_call(
        body, name=name,
        in_specs=[vmem], out_specs=vmem,
        out_shape=jax.ShapeDtypeStruct((1, nc), F32),
        scratch_shapes=[pltpu.VMEM((8, 1, nc), F32), pltpu.SemaphoreType.DMA((7,)), pltpu.SemaphoreType.DMA((7,))],
    )(packed)


def _sum4(g_full, axis, slot_arr, recv, name):
    _, r, c = recv.shape
    tr = _row_tile(r, 512)
    nb = r // tr

    def body(slot_ref, own_ref, recv_ref, o_ref):
        acc = own_ref[...]
        for k in range(3):
            acc = acc + recv_ref[k].astype(F32)
        o_ref[...] = acc

    if axis == 0:
        own_map = lambda i, s: (s[0] * nb + i, 0)
    else:
        own_map = lambda i, s: (i, s[0])
    return pl.pallas_call(
        body, name=name,
        grid_spec=pltpu.PrefetchScalarGridSpec(
            num_scalar_prefetch=1, grid=(nb,),
            in_specs=[pl.BlockSpec((tr, c), own_map), pl.BlockSpec((3, tr, c), lambda i, s: (0, i, 0))],
            out_specs=pl.BlockSpec((tr, c), lambda i, s: (i, 0))),
        out_shape=jax.ShapeDtypeStruct((r, c), F32),
        compiler_params=pltpu.CompilerParams(dimension_semantics=("parallel",), vmem_limit_bytes=MM_VMEM_LIMIT),
    )(slot_arr, g_full, recv)


def _adamw(w, g_parts, m, v, name):
    r, c = w.shape
    tr = r if r % 128 else _row_tile(r, 256)
    npart = len(g_parts)

    def body(*refs):
        w_ref = refs[0]
        g_refs = refs[1:1 + npart]
        m_ref, v_ref, g_out, d_out, m_out, v_out = refs[1 + npart:]
        g = g_refs[0][...]
        for gr in g_refs[1:]:
            g = g + gr[...]
        mm = ADAM_B1 * m_ref[...] + (1.0 - ADAM_B1) * g
        vv = ADAM_B2 * v_ref[...] + (1.0 - ADAM_B2) * (g * g)
        m_hat = mm / (1.0 - ADAM_B1 ** ADAM_STEP)
        v_hat = vv / (1.0 - ADAM_B2 ** ADAM_STEP)
        g_out[...] = g
        d_out[...] = -ADAM_LR * (m_hat / (jnp.sqrt(v_hat) + ADAM_EPS) + ADAM_WD * w_ref[...])
        m_out[...] = mm
        v_out[...] = vv

    blk = pl.BlockSpec((tr, c), lambda i: (i, 0))
    shp = jax.ShapeDtypeStruct((r, c), F32)
    return pl.pallas_call(
        body, name=name, grid=(r // tr,),
        in_specs=[blk] * (3 + npart), out_specs=[blk] * 4, out_shape=[shp] * 4,
        compiler_params=pltpu.CompilerParams(dimension_semantics=("parallel",), vmem_limit_bytes=MM_VMEM_LIMIT),
    )(w, *g_parts, m, v)


def _pad_rows8(w):
    return jnp.pad(w, ((0, HALO - w.shape[0]), (0, 0)))


def kernel(x, mem, hgrn_lb, norm1_w, w_in, hgrn_norm_w, sconv_w, w_out, norm2_w, mem_norm_w, wq, wk, wv, wo, norm3_w, w_gate, w_up, ffn_conv_w, ffn_conv_b, w_down, final_norm_w, loss_target, m_hgrn_lb, m_norm1_w, m_w_in, m_hgrn_norm_w, m_sconv_w, m_w_out, m_norm2_w, m_mem_norm_w, m_wq, m_wk, m_wv, m_wo, m_norm3_w, m_w_gate, m_w_up, m_ffn_conv_w, m_ffn_conv_b, m_w_down, m_final_norm_w, v_hgrn_lb, v_norm1_w, v_w_in, v_hgrn_norm_w, v_sconv_w, v_w_out, v_norm2_w, v_mem_norm_w, v_wq, v_wk, v_wv, v_wo, v_norm3_w, v_w_gate, v_w_up, v_ffn_conv_w, v_ffn_conv_b, v_w_down, v_final_norm_w):
    xs, mems, tgt = x[0], mem[0], loss_target[0]
    d = xs.shape[1]
    fnw = final_norm_w.reshape(1, d)

    big = {"w_in": (w_in[0], 1), "w_out": (w_out[0], 0), "wq": (wq[0], 0), "wk": (wk[0], 0), "wv": (wv[0], 0),
           "wo": (wo[0], 0), "w_gate": (w_gate[0], 1), "w_up": (w_up[0], 1), "w_down": (w_down[0], 0)}
    names = list(big)
    slot_arr = (2 * lax.axis_index("x") + lax.axis_index("y")).astype(jnp.int32).reshape(1)
    gnames = names + ["sconv8", "fconv8"]
    axes = [big[n][1] for n in names] + [1, 1]
    groups = [["w_in"], ["w_out", "sconv8"], ["wq", "wk", "wv", "wo"], ["w_gate", "w_up", "fconv8", "w_down"]]
    gidx = [[gnames.index(n) for n in grp] for grp in groups]
    split = [True] * len(names) + [False, False]
    first = _cast_into_full(big["w_in"][0], 1, slot_arr, BF16, "cast_w_in")
    sems0, first, tok0 = _gather_start([first], [1], [True], [[0]], "gather_start_w_in")
    rest = [_cast_into_full(big[n][0], big[n][1], slot_arr, BF16, "cast_" + n, after=tok0) for n in names[1:]]
    rest += [_cast_into_full(_pad_rows8(sconv_w[0]), 1, slot_arr, F32, "cast_sconv_w", after=tok0),
             _cast_into_full(_pad_rows8(ffn_conv_w[0]), 1, slot_arr, F32, "cast_ffn_conv_w", after=tok0)]
    sems1, rest, tok = _gather_start(rest, axes[1:], split[1:], [[t - 1 for t in idx] for idx in gidx[1:]],
                                     "gather_start")
    gsems, fulls = sems0 + sems1, first + rest
    wf, relayed = {}, {}

    def gather_relay(g, after):
        idx = gidx[g]
        dsems, arrs, token = _gather_relay([fulls[t] for t in idx], [axes[t] for t in idx], [split[t] for t in idx],
                                           gsems[g], after, "gather_relay_%d" % g)
        relayed[g] = (dsems, arrs)
        return token[0:1, 0:1]

    def gather_finish(g, after):
        idx = gidx[g]
        dsems, arrs = relayed[g]
        got = _gather_finish(arrs, [axes[t] for t in idx], [split[t] for t in idx], dsems, after,
                             "gather_finish_%d" % g)
        wf.update(zip(groups[g], got))

    lb0, lb1 = hgrn_lb[0:1], hgrn_lb[1:2]

    h1 = _rmsnorm_fwd(xs, norm1_w + tok[0:1, 0:1], "norm1")
    gather_relay(0, h1)
    gather_finish(0, h1)
    proj = _matmul(h1, wf["w_in"], "nn", "proj_in", out_dtype=BF16, tn=1792)
    t1 = gather_relay(1, proj)
    o_h, og, states = _hgrn_fwd(proj, lb0, lb1, hgrn_norm_w + t1, "hgrn_fwd")
    gather_finish(1, o_h)
    t2 = gather_relay(2, o_h)
    sconv8 = wf["sconv8"]
    mix = _sconv_fwd(proj, sconv8 + t2, og, "sconv_fwd")
    x1 = _matmul(mix, wf["w_out"], "nn", "proj_out", residual=xs)
    t3 = gather_relay(3, x1)
    h2 = _rmsnorm_fwd(x1, norm2_w + t3, "norm2")
    mem_n = _rmsnorm_fwd(mems, mem_norm_w, "norm_mem")
    gather_finish(2, h2)
    qa = _matmul(h2, wf["wq"], "nn", "attn_q", out_dtype=BF16)
    ka = _matmul(mem_n, wf["wk"], "nn", "attn_k", out_dtype=BF16)
    va = _matmul(mem_n, wf["wv"], "nn", "attn_v", out_dtype=BF16)
    att = _attn_fwd(qa, ka, va, "attn_fwd")
    x2 = _matmul(att, wf["wo"], "nn", "attn_o", residual=x1)
    h3 = _rmsnorm_fwd(x2, norm3_w, "norm3")
    gather_finish(3, h3)
    fconv8 = wf["fconv8"]
    gate = _matmul(h3, wf["w_gate"], "nn", "ffn_gate", out_dtype=BF16)
    up = _matmul(h3, wf["w_up"], "nn", "ffn_up", out_dtype=BF16)
    z, act = _ffn_fwd(gate, up, fconv8, ffn_conv_b, "ffn_act")
    x3 = _matmul(z, wf["w_down"], "nn", "ffn_down", residual=x2)

    dx3, dx3b, g_final, loss8 = _final_loss_bwd(x3, tgt, fnw, "loss_bwd")
    gw = {}
    dz = _matmul(dx3b, wf["w_down"], "nt", "d_z", out_dtype=BF16)
    gw["w_down"] = _matmul(z, dx3b, "tn", "g_w_down", extra_bf16=True)
    dgate, du, g_fb, g_fw = _ffn_bwd(act, gate, up, dz, fconv8, "ffn_act_bwd")
    dh3 = _matmul(dgate, wf["w_gate"], "nt", "d_h3_gate")
    dh3 = _matmul(du, wf["w_up"], "nt", "d_h3_up", residual=dh3, out_dtype=BF16)
    gw["w_gate"] = _matmul(h3, dgate, "tn", "g_w_gate", extra_bf16=True)
    gw["w_up"] = _matmul(h3, du, "tn", "g_w_up", extra_bf16=True)
    pending = []

    def scatter_start(grp):
        sems, g_thru, lands, token = _scatter_start([gw[n][1] for n in grp], [big[n][1] for n in grp],
                                                    "scatter_start_" + grp[0])
        pending.append((grp, sems, g_thru, lands))
        return token[0:1, 0:1]

    tok1 = scatter_start(["w_down", "w_gate", "w_up"])
    dx2, dx2b, g_n3 = _rmsnorm_bwd(dh3, x2, norm3_w + tok1, dx3, "norm3_bwd")
    datt = _matmul(dx2b, wf["wo"], "nt", "d_att", out_dtype=BF16)
    gw["wo"] = _matmul(att, dx2b, "tn", "g_wo", extra_bf16=True)
    dqa, dka, dva = _attn_bwd(qa, ka, va, datt, "attn_bwd")
    dh2 = _matmul(dqa, wf["wq"], "nt", "d_h2", out_dtype=BF16)
    gw["wq"] = _matmul(h2, dqa, "tn", "g_wq", extra_bf16=True)
    gw["wk"] = _matmul(mem_n, dka, "tn", "g_wk", extra_bf16=True)
    gw["wv"] = _matmul(mem_n, dva, "tn", "g_wv", extra_bf16=True)
    tok2 = scatter_start(["wo", "wq", "wk", "wv"])
    dmem_n = _matmul(dka, wf["wk"], "nt", "d_memn_k")
    dmem_n = _matmul(dva, wf["wv"], "nt", "d_memn_v", residual=dmem_n)
    _, _, g_nm = _rmsnorm_bwd(dmem_n, mems, mem_norm_w, None, "norm_mem_bwd")
    dx1, dx1b, g_n2 = _rmsnorm_bwd(dh2, x1, norm2_w + tok2, dx2, "norm2_bwd")
    dmix = _matmul(dx1b, wf["w_out"], "nt", "d_mix", out_dtype=BF16)
    gw["w_out"] = _matmul(mix, dx1b, "tn", "g_w_out", extra_bf16=True)
    tok3 = scatter_start(["w_out"])
    dproj, g_lb, g_hn = _hgrn_bwd(proj, lb0, lb1, hgrn_norm_w + tok3, o_h, states, dmix, "hgrn_bwd")
    dproj, g_sw = _sconv_bwd(proj, sconv8, dmix, dproj, "sconv_bwd")
    gw["w_in"] = _matmul(h1, dproj, "tn", "g_w_in", extra_bf16=True, groups=7)
    tok4 = scatter_start(["w_in"])
    dh1 = _matmul(dproj, wf["w_in"], "nt", "d_h1", out_dtype=BF16, groups=7, tn=2048)
    dx, _, g_n1 = _rmsnorm_bwd(dh1, xs, norm1_w + tok4, dx1, "norm1_bwd")

    small = [g_n1, g_n2, g_n3, g_final, g_nm, g_lb, g_hn, g_fb,
             g_sw[0:8], g_sw[8:16], g_sw[16:24], g_fw[0:8], g_fw[8:16], g_fw[16:24], loss8]
    widths = [a.shape[1] for a in small]
    tot = _all_reduce_small(jnp.concatenate(small, axis=1), "all_reduce_small")
    offs = [0]
    for wd_ in widths:
        offs.append(offs[-1] + wd_)
    sm = [tot[:, offs[i]:offs[i + 1]] for i in range(len(small))]
    s_n1, s_n2, s_n3, s_final, s_nm, s_lb, s_hn, s_fb = sm[:8]
    s_sw = jnp.concatenate(sm[8:11], axis=0)
    s_fw = jnp.concatenate(sm[11:14], axis=0)
    loss = sm[14][0, 0]
    slot = 2 * lax.axis_index("x") + lax.axis_index("y")
    s_sw = lax.dynamic_slice_in_dim(s_sw, slot * (HGRN_W // N_CHIPS), HGRN_W // N_CHIPS, axis=1)
    fsh = ffn_conv_w.shape[2]
    s_fw = lax.dynamic_slice_in_dim(s_fw, slot * fsh, fsh, axis=1)
    s_lb2 = jnp.concatenate([s_lb, -s_lb], axis=0)

    swaps = []
    after = tot
    for grp, sems, g_thru, lands in pending:
        got = _scatter_wait(g_thru, lands, [big[n][1] for n in grp], sems, after, "scatter_wait_" + grp[0])
        sums = [_sum4(gw[n][0], big[n][1], slot_arr, r, "core_sum_" + n) for n, r in zip(grp, got)]
        ssems, s_thru, s_lands, after = _sibling_start(sums, "sibling_start_" + grp[0])
        swaps.append((grp, ssems, s_thru, s_lands))

    moments = {"hgrn_lb": (m_hgrn_lb, v_hgrn_lb), "norm1_w": (m_norm1_w, v_norm1_w), "w_in": (m_w_in, v_w_in),
               "hgrn_norm_w": (m_hgrn_norm_w, v_hgrn_norm_w), "sconv_w": (m_sconv_w, v_sconv_w),
               "w_out": (m_w_out, v_w_out), "norm2_w": (m_norm2_w, v_norm2_w),
               "mem_norm_w": (m_mem_norm_w, v_mem_norm_w), "wq": (m_wq, v_wq), "wk": (m_wk, v_wk), "wv": (m_wv, v_wv),
               "wo": (m_wo, v_wo), "norm3_w": (m_norm3_w, v_norm3_w), "w_gate": (m_w_gate, v_w_gate),
               "w_up": (m_w_up, v_w_up), "ffn_conv_w": (m_ffn_conv_w, v_ffn_conv_w),
               "ffn_conv_b": (m_ffn_conv_b, v_ffn_conv_b), "w_down": (m_w_down, v_w_down),
               "final_norm_w": (m_final_norm_w, v_final_norm_w)}
    weights = {"hgrn_lb": hgrn_lb, "norm1_w": norm1_w, "w_in": w_in, "hgrn_norm_w": hgrn_norm_w, "sconv_w": sconv_w,
               "w_out": w_out, "norm2_w": norm2_w, "mem_norm_w": mem_norm_w, "wq": wq, "wk": wk, "wv": wv, "wo": wo,
               "norm3_w": norm3_w, "w_gate": w_gate, "w_up": w_up, "ffn_conv_w": ffn_conv_w, "ffn_conv_b": ffn_conv_b,
               "w_down": w_down, "final_norm_w": final_norm_w}
    small_g = {"hgrn_lb": s_lb2, "norm1_w": s_n1, "hgrn_norm_w": s_hn, "sconv_w": s_sw, "norm2_w": s_n2,
               "mem_norm_w": s_nm, "norm3_w": s_n3, "ffn_conv_w": s_fw, "ffn_conv_b": s_fb, "final_norm_w": s_final}
    order = list(weights)
    res = {}

    def adamw(n, parts):
        shape = weights[n].shape
        w2 = weights[n].reshape((-1, shape[-1]))
        m2, v2 = (t.reshape(w2.shape) for t in moments[n])
        res[n] = [t.reshape(shape) for t in _adamw(w2, [p.reshape(w2.shape) for p in parts], m2, v2, "adamw_" + n)]

    for n in order:
        if n not in big:
            adamw(n, [small_g[n]])
    after = after + res["final_norm_w"][1][0]
    for grp, ssems, s_thru, s_lands in swaps:
        own, other = _sibling_wait(s_thru, s_lands, ssems, after, "sibling_wait_" + grp[0])
        for n, a, b in zip(grp, own, other):
            adamw(n, [a, b])
        after = res[grp[-1]][1]

    return (loss, dx[None], *[res[n][0] for n in order], *[res[n][1] for n in order],
            *[res[n][2] for n in order], *[res[n][3] for n in order])
```

```python
import jax
import jax.numpy as jnp
from jax import lax
from jax.experimental import pallas as pl
from jax.experimental.pallas import tpu as pltpu

F32 = jnp.float32
BF16 = jnp.bfloat16
MESH = pl.DeviceIdType.MESH

EPS = 1e-6
HGRN_W = 1024
HEAD = 128
N_HEADS = 8
CHUNK = 128
HGRN_UNROLL = 8
HGRN_HEADS_PER_STEP = 4
DPROJ_GROUPS = 8
MEM_HEADS = 4
MEM_HEAD_DIM = 512
N_CHIPS = 4
HALO = 8

ADAM_LR = 0.001
ADAM_B1 = 0.9
ADAM_B2 = 0.999
ADAM_EPS = 1e-08
ADAM_WD = 0.01
ADAM_STEP = 10


def _sigmoid(x):
    return 1.0 / (1.0 + jnp.exp(-x))


def _dot(a, b, dims):
    return lax.dot_general(a.astype(BF16), b.astype(BF16), (dims, ((), ())),
                           preferred_element_type=F32)


def _dot_nn(a, b):
    return _dot(a, b, ((1,), (0,)))


def _dot_nt(a, b):
    return _dot(a, b, ((1,), (1,)))


def _dot_tn(a, b):
    return _dot(a, b, ((0,), (0,)))


def _hdot(a, b, dims):
    return lax.dot_general(a, b, (dims, ((), ())), precision=lax.Precision.HIGH, preferred_element_type=F32)


def _hdot_nn(a, b):
    return _hdot(a, b, ((1,), (0,)))


def _hdot_tn(a, b):
    return _hdot(a, b, ((0,), (0,)))


def _exact_ones_dot(ones_bf16, x):
    hi = x.astype(BF16)
    r1 = x - hi.astype(F32)
    mid = r1.astype(BF16)
    lo = (r1 - mid.astype(F32)).astype(BF16)
    dims = (((1,), (0,)), ((), ()))
    return (lax.dot_general(ones_bf16, hi, dims, preferred_element_type=F32)
            + lax.dot_general(ones_bf16, mid, dims, preferred_element_type=F32)
            + lax.dot_general(ones_bf16, lo, dims, preferred_element_type=F32))


def _rows8(v):
    t, c = v.shape
    return v.reshape(t // 8, 8, c).sum(axis=0)


def _shift_down(x, halo, s):
    rolled = pltpu.roll(x, s, 0)
    hrolled = pltpu.roll(halo, s, 0)
    row = lax.broadcasted_iota(jnp.int32, hrolled.shape, 0)
    head = jnp.where(row < s, hrolled, rolled[:HALO])
    return jnp.concatenate([head, rolled[HALO:]], axis=0)


def _shift_up(x, halo, s):
    t = x.shape[0]
    rolled = pltpu.roll(x, t - s, 0)
    hrolled = pltpu.roll(halo, HALO - s, 0)
    row = lax.broadcasted_iota(jnp.int32, hrolled.shape, 0)
    tail = jnp.where(row >= HALO - s, hrolled, rolled[t - HALO:])
    return jnp.concatenate([rolled[:t - HALO], tail], axis=0)


def _params(*sem):
    return pltpu.CompilerParams(dimension_semantics=sem, vmem_limit_bytes=MM_VMEM_LIMIT)


def _row_tile(r, pref):
    while r % pref:
        pref //= 2
    return pref


def _rmsnorm_fwd(x, w, name, tm=512):
    s, d = x.shape
    tm = min(tm, s)

    def body(x_ref, w_ref, o_ref):
        xv = x_ref[...]
        r = lax.rsqrt(jnp.mean(xv * xv, axis=-1, keepdims=True) + EPS)
        o_ref[...] = ((xv * r) * w_ref[...]).astype(BF16)

    return pl.pallas_call(
        body, name=name, grid=(s // tm,),
        in_specs=[pl.BlockSpec((tm, d), lambda i: (i, 0)), pl.BlockSpec((1, d), lambda i: (0, 0))],
        out_specs=pl.BlockSpec((tm, d), lambda i: (i, 0)),
        out_shape=jax.ShapeDtypeStruct((s, d), BF16),
        compiler_params=_params("parallel"),
    )(x, w)


def _rmsnorm_bwd(dh, x, w, dres, name, tm=512):
    s, d = x.shape
    tm = min(tm, s)
    has_res = dres is not None

    def body(*refs):
        if has_res:
            dh_ref, x_ref, w_ref, dres_ref, dx_ref, dxb_ref, gw_ref = refs
        else:
            dh_ref, x_ref, w_ref, dx_ref, dxb_ref, gw_ref = refs

        @pl.when(pl.program_id(0) == 0)
        def _():
            gw_ref[...] = jnp.zeros_like(gw_ref)

        xv = x_ref[...]
        dhv = dh_ref[...].astype(F32)
        r = lax.rsqrt(jnp.mean(xv * xv, axis=-1, keepdims=True) + EPS)
        xhat = xv * r
        gw_ref[...] += _rows8(dhv * xhat)
        dxh = dhv * w_ref[...]
        dx = r * (dxh - xhat * jnp.mean(dxh * xhat, axis=-1, keepdims=True))
        if has_res:
            dx = dres_ref[...] + dx
        dx_ref[...] = dx
        dxb_ref[...] = dx.astype(BF16)

    row = pl.BlockSpec((tm, d), lambda i: (i, 0))
    in_specs = [row, row, pl.BlockSpec((1, d), lambda i: (0, 0))] + ([row] if has_res else [])
    args = (dh, x, w) + ((dres,) if has_res else ())
    return pl.pallas_call(
        body, name=name, grid=(s // tm,),
        in_specs=in_specs,
        out_specs=[row, row, pl.BlockSpec((8, d), lambda i: (0, 0))],
        out_shape=[jax.ShapeDtypeStruct((s, d), F32), jax.ShapeDtypeStruct((s, d), BF16),
                   jax.ShapeDtypeStruct((8, d), F32)],
        compiler_params=_params("arbitrary"),
    )(*args)


def _final_loss_bwd(x3, target, w, name, tm=512):
    s, d = x3.shape
    tm = min(tm, s)

    def body(x_ref, t_ref, w_ref, dx_ref, dxb_ref, gw_ref, loss_ref):
        @pl.when(pl.program_id(0) == 0)
        def _():
            gw_ref[...] = jnp.zeros_like(gw_ref)
            loss_ref[...] = jnp.zeros_like(loss_ref)

        xv = x_ref[...]
        r = lax.rsqrt(jnp.mean(xv * xv, axis=-1, keepdims=True) + EPS)
        xhat = xv * r
        y = xhat * w_ref[...]
        err = y - t_ref[...]
        part = 0.5 * jnp.mean(err * err, axis=-1, keepdims=True)
        tot = jnp.sum(part, axis=0, keepdims=True)
        rr = lax.broadcasted_iota(jnp.int32, loss_ref.shape, 0)
        cc = lax.broadcasted_iota(jnp.int32, loss_ref.shape, 1)
        loss_ref[...] += jnp.where((rr == 0) & (cc == 0), tot, 0.0)
        dy = err * (1.0 / d)
        gw_ref[...] += _rows8(dy * xhat)
        dxh = dy * w_ref[...]
        dx = r * (dxh - xhat * jnp.mean(dxh * xhat, axis=-1, keepdims=True))
        dx_ref[...] = dx
        dxb_ref[...] = dx.astype(BF16)

    row = pl.BlockSpec((tm, d), lambda i: (i, 0))
    return pl.pallas_call(
        body, name=name, grid=(s // tm,),
        in_specs=[row, row, pl.BlockSpec((1, d), lambda i: (0, 0))],
        out_specs=[row, row, pl.BlockSpec((8, d), lambda i: (0, 0)), pl.BlockSpec((8, 128), lambda i: (0, 0))],
        out_shape=[jax.ShapeDtypeStruct((s, d), F32), jax.ShapeDtypeStruct((s, d), BF16),
                   jax.ShapeDtypeStruct((8, d), F32), jax.ShapeDtypeStruct((8, 128), F32)],
        compiler_params=_params("arbitrary"),
    )(x3, target, w)


MM_TILES = (1024, 1408, 512, 256, 128)
MM_K_TILES = (2816, 2048, 1792, 1408, 1024, 512, 256, 128)
MM_VMEM_LIMIT = 56 * 1024 * 1024
MM_VMEM_BUDGET = 46 * 1024 * 1024


def _pick_tile(dim):
    for t in MM_TILES:
        if dim % t == 0:
            return t
    return dim


def _matmul(a, b, mode, name, *, out_dtype=F32, residual=None, extra_bf16=False, tm=None, tn=None, tk=None,
            groups=None):
    if groups is not None and mode == "nt":
        _, m, gw = a.shape
        n, k2 = b.shape
        k, tk = groups * gw, gw
    elif groups is not None and mode == "tn":
        k, m = a.shape
        _, k2, gw = b.shape
        n, tn = groups * gw, gw
    elif mode == "nn":
        (m, k), (k2, n) = a.shape, b.shape
    elif mode == "nt":
        (m, k), (n, k2) = a.shape, b.shape
    else:
        (k, m), (k2, n) = a.shape, b.shape
    assert k == k2, (a.shape, b.shape, mode)
    auto_tm = tm is None
    tm = _pick_tile(m) if tm is None else min(tm, m)
    tn = _pick_tile(n) if tn is None else min(tn, n)
    out_elt = jnp.dtype(out_dtype).itemsize + (2 if extra_bf16 else 0) + (4 if residual is not None else 0)

    def vmem_bytes(t, rows=None):
        rows = tm if rows is None else rows
        return (2 * (rows * t * a.dtype.itemsize + t * tn * b.dtype.itemsize) + 2 * rows * tn * out_elt
                + rows * tn * 4)

    if auto_tm and tk is None and m % (2 * tm) == 0 and vmem_bytes(k, 2 * tm) <= MM_VMEM_BUDGET:
        tm = 2 * tm

    if tk is None:
        tk = next(t for t in MM_K_TILES if k % t == 0 and t <= k and vmem_bytes(t) <= MM_VMEM_BUDGET)
    assert m % tm == 0 and n % tn == 0 and k % tk == 0, (m, n, k, tm, tn, tk)
    nk = k // tk
    dims = {"nn": ((1,), (0,)), "nt": ((1,), (1,)), "tn": ((0,), (0,))}[mode]
    has_res = residual is not None

    def body(*refs):
        refs = list(refs)
        a_ref, b_ref = refs[0], refs[1]
        r_ref = refs[2] if has_res else None
        outs = refs[2 + has_res:]
        o_ref = outs[0]
        o2_ref = outs[1] if extra_bf16 else None
        def finish(r):
            if has_res:
                r = r_ref[...] + r
            o_ref[...] = r.astype(out_dtype)
            if extra_bf16:
                o2_ref[...] = r.astype(BF16)

        if nk == 1:
            finish(_dot(a_ref[...], b_ref[...], dims))
            return
        acc = outs[-1]
        kk = pl.program_id(2)

        @pl.when(kk == 0)
        def _():
            acc[...] = _dot(a_ref[...], b_ref[...], dims)

        if nk > 2:
            @pl.when((kk > 0) & (kk < nk - 1))
            def _():
                acc[...] += _dot(a_ref[...], b_ref[...], dims)

        @pl.when(kk == nk - 1)
        def _():
            finish(acc[...] + _dot(a_ref[...], b_ref[...], dims))

    if mode == "tn":
        a_spec = pl.BlockSpec((tk, tm), lambda i, j, kk: (kk, i))
    elif groups is not None:
        a_spec = pl.BlockSpec((None, tm, tk), lambda i, j, kk: (kk, i, 0))
    else:
        a_spec = pl.BlockSpec((tm, tk), lambda i, j, kk: (i, kk))
    if mode == "nt":
        b_spec = pl.BlockSpec((tn, tk), lambda i, j, kk: (j, kk))
    elif groups is not None:
        b_spec = pl.BlockSpec((None, tk, tn), lambda i, j, kk: (j, kk, 0))
    else:
        b_spec = pl.BlockSpec((tk, tn), lambda i, j, kk: (kk, j))
    o_spec = pl.BlockSpec((tm, tn), lambda i, j, kk: (i, j))
    in_specs = [a_spec, b_spec] + ([o_spec] if has_res else [])
    out_specs = [o_spec] + ([o_spec] if extra_bf16 else [])
    out_shape = [jax.ShapeDtypeStruct((m, n), out_dtype)] + ([jax.ShapeDtypeStruct((m, n), BF16)] if extra_bf16 else [])
    args = (a, b) + ((residual,) if has_res else ())
    res = pl.pallas_call(
        body, name=name, grid=(m // tm, n // tn, nk),
        in_specs=in_specs, out_specs=out_specs, out_shape=out_shape,
        scratch_shapes=[pltpu.VMEM((tm, tn) if nk > 1 else (8, 128), F32)],
        compiler_params=pltpu.CompilerParams(dimension_semantics=("parallel", "parallel", "arbitrary"),
                                             vmem_limit_bytes=MM_VMEM_LIMIT),
    )(*args)
    return res if extra_bf16 else res[0]


def _hgrn_gates(qp, fp, lb):
    sig = _sigmoid(fp)
    f = lb + (1.0 - lb) * sig
    logf = jnp.log(f)
    k = 1.0 - f
    sq = _sigmoid(qp)
    q = qp * sq
    return sig, f, logf, k, sq, q


def _hgrn_fwd(proj, lb0, lb1, norm_w, name, tb=1024):
    s = proj.shape[0]
    tb = min(tb, s)
    nb, ncb = s // tb, tb // CHUNK

    def body(q_ref, f_ref, i_ref, g_ref, a0_ref, a1_ref, nw_ref, o_ref, og_ref, st_ref, state):
        @pl.when(pl.program_id(1) == 0)
        def _():
            state[...] = jnp.zeros_like(state)

        lb2 = _sigmoid(a0_ref[...] - a1_ref[...])
        row = lax.broadcasted_iota(jnp.int32, (CHUNK, CHUNK), 0)
        col = lax.broadcasted_iota(jnp.int32, (CHUNK, CHUNK), 1)
        tril = row >= col
        ones_l = tril.astype(BF16)
        nw = nw_ref[...]

        def chunk(c, carry):
            rows = pl.ds(pl.multiple_of(c * CHUNK, CHUNK), CHUNK)
            for hh in range(HGRN_HEADS_PER_STEP):
                cols = slice(hh * HEAD, (hh + 1) * HEAD)
                v = i_ref[rows, cols].astype(F32)
                _, _, logf, k, _, q = _hgrn_gates(q_ref[rows, cols].astype(F32), f_ref[rows, cols].astype(F32),
                                                  lb2[:, cols])
                b = _exact_ones_dot(ones_l, logf)
                bl = jnp.sum(logf, axis=0, keepdims=True)
                bm = 0.5 * bl
                st = state[hh]
                st_ref[hh, c] = st
                qt = q * jnp.exp(b - bm)
                kt = k * jnp.exp(bm - b)
                a = jnp.where(tril, _dot_nt(qt, kt), 0.0)
                o = _dot_nt(q * jnp.exp(b), st) + _dot_nn(a, v)
                state[hh] = st * jnp.exp(bl) + _dot_tn(v, k * jnp.exp(bl - b))
                o_ref[rows, cols] = o
                on = (o * lax.rsqrt(jnp.mean(o * o, axis=-1, keepdims=True) + EPS)) * nw
                gv = g_ref[rows, cols].astype(F32)
                og_ref[rows, cols] = (on * (gv * _sigmoid(gv))).astype(BF16)
            return carry

        lax.fori_loop(0, ncb, chunk, 0, unroll=HGRN_UNROLL)

    hp, wd = HGRN_HEADS_PER_STEP, HGRN_HEADS_PER_STEP * HEAD
    ngrp = N_HEADS // hp

    def colblk(group):
        return pl.BlockSpec((tb, wd), lambda h, j: (j, group * ngrp + h))

    vec = pl.BlockSpec((1, wd), lambda h, j: (0, h))
    out_blk = pl.BlockSpec((tb, wd), lambda h, j: (j, h))
    return pl.pallas_call(
        body, name=name, grid=(ngrp, nb),
        in_specs=[colblk(0), colblk(1), colblk(2), colblk(3), vec, vec, pl.BlockSpec((1, HEAD), lambda h, j: (0, 0))],
        out_specs=[out_blk, out_blk, pl.BlockSpec((hp, ncb, HEAD, HEAD), lambda h, j: (h, j, 0, 0))],
        out_shape=[jax.ShapeDtypeStruct((s, HGRN_W), F32), jax.ShapeDtypeStruct((s, 2 * HGRN_W), BF16),
                   jax.ShapeDtypeStruct((N_HEADS, s // CHUNK, HEAD, HEAD), F32)],
        scratch_shapes=[pltpu.VMEM((hp, HEAD, HEAD), F32)],
        compiler_params=_params("parallel", "arbitrary"),
    )(proj, proj, proj, proj, lb0, lb1, norm_w)


def _hgrn_bwd(proj, lb0, lb1, norm_w, o, states, dmix, name, tb=1024):
    s = proj.shape[0]
    tb = min(tb, s)
    nb, ncb = s // tb, tb // CHUNK

    def body(q_ref, f_ref, i_ref, g_ref, a0_ref, a1_ref, nw_ref, o_ref, st_ref, dm_ref,
             dp_ref, glb_ref, gnw_ref, dstate):
        h = pl.program_id(0)

        @pl.when(pl.program_id(1) == 0)
        def _():
            dstate[...] = jnp.zeros_like(dstate)
            glb_ref[...] = jnp.zeros_like(glb_ref)

        @pl.when((pl.program_id(1) == 0) & (h == 0))
        def _():
            gnw_ref[...] = jnp.zeros_like(gnw_ref)

        lb2 = _sigmoid(a0_ref[...] - a1_ref[...])
        row = lax.broadcasted_iota(jnp.int32, (CHUNK, CHUNK), 0)
        col = lax.broadcasted_iota(jnp.int32, (CHUNK, CHUNK), 1)
        tril = row >= col
        ones_l = tril.astype(BF16)
        ones_u = (row <= col).astype(BF16)
        nw = nw_ref[...]

        def chunk(cc, carry):
            c = ncb - 1 - cc
            rows = pl.ds(pl.multiple_of(c * CHUNK, CHUNK), CHUNK)
            for hh in range(HGRN_HEADS_PER_STEP):
                cols = slice(hh * HEAD, (hh + 1) * HEAD)
                lb = lb2[:, cols]
                qp = q_ref[rows, cols].astype(F32)
                v = i_ref[rows, cols].astype(F32)
                sig, f, logf, k, sq, q = _hgrn_gates(qp, f_ref[rows, cols].astype(F32), lb)
                gv = g_ref[rows, cols].astype(F32)
                sg = _sigmoid(gv)
                silu_g = gv * sg
                dog = dm_ref[rows, cols].astype(F32)
                ov = o_ref[rows, cols]
                r = lax.rsqrt(jnp.mean(ov * ov, axis=-1, keepdims=True) + EPS)
                ohat = ov * r
                on = ohat * nw
                dp_ref[3, rows, cols] = (dog * on * (sg * (1.0 + gv * (1.0 - sg)))).astype(BF16)
                don = dog * silu_g
                gnw_ref[...] += _rows8(don * ohat)
                doh = don * nw
                do = r * (doh - ohat * jnp.mean(doh * ohat, axis=-1, keepdims=True))
                b = _exact_ones_dot(ones_l, logf)
                bl = jnp.sum(logf, axis=0, keepdims=True)
                bm = 0.5 * bl
                e_q = jnp.exp(b - bm)
                e_k = jnp.exp(bm - b)
                e_b = jnp.exp(b)
                e_l = jnp.exp(bl - b)
                qt, kt, qb, kb = q * e_q, k * e_k, q * e_b, k * e_l
                st0 = st_ref[hh, c]
                dst = dstate[hh]
                a = jnp.where(tril, _dot_nt(qt, kt), 0.0)
                da = jnp.where(tril, _dot_nt(do, v), 0.0)
                dq = _hdot_nn(da, kt) * e_q + _hdot_nn(do, st0) * e_b
                dkb = _hdot_nn(v, dst) * e_l
                dk = _hdot_tn(da, qt) * e_k + dkb
                dv = _dot_tn(a, do) + _dot_nt(kb, dst)
                e_bl = jnp.exp(bl)
                dstate[hh] = dst * e_bl + _dot_tn(do, qb)
                db = q * dq - k * dk
                db_last = jnp.sum(k * dkb, axis=0, keepdims=True) + e_bl * jnp.sum(st0 * dst, axis=0, keepdims=True)
                dlogf = _exact_ones_dot(ones_u, db) + db_last
                dfg = dlogf / f - dk
                dp_ref[1, rows, cols] = (dfg * (1.0 - lb) * (sig * (1.0 - sig))).astype(BF16)
                glb_ref[:, cols] += _rows8(dfg * (1.0 - sig)) * (lb * (1.0 - lb))
                dp_ref[0, rows, cols] = (dq * (sq * (1.0 + qp * (1.0 - sq)))).astype(BF16)
                dp_ref[2, rows, cols] = dv.astype(BF16)
            return carry

        lax.fori_loop(0, ncb, chunk, 0, unroll=HGRN_UNROLL)

    hp, wd = HGRN_HEADS_PER_STEP, HGRN_HEADS_PER_STEP * HEAD
    ngrp = N_HEADS // hp

    def colblk(group):
        return pl.BlockSpec((tb, wd), lambda h, j: (nb - 1 - j, group * ngrp + h))

    vec = pl.BlockSpec((1, wd), lambda h, j: (0, h))
    blk = pl.BlockSpec((tb, wd), lambda h, j: (nb - 1 - j, h))
    return pl.pallas_call(
        body, name=name, grid=(ngrp, nb),
        in_specs=[colblk(0), colblk(1), colblk(2), colblk(3), vec, vec, pl.BlockSpec((1, HEAD), lambda h, j: (0, 0)),
                  blk, pl.BlockSpec((hp, ncb, HEAD, HEAD), lambda h, j: (h, nb - 1 - j, 0, 0)), blk],
        out_specs=[pl.BlockSpec((4, tb, wd), lambda h, j: (0, nb - 1 - j, h)),
                   pl.BlockSpec((8, wd), lambda h, j: (0, h)), pl.BlockSpec((8, HEAD), lambda h, j: (0, 0))],
        out_shape=[jax.ShapeDtypeStruct((DPROJ_GROUPS, s, HGRN_W), BF16),
                   jax.ShapeDtypeStruct((8, HGRN_W), F32), jax.ShapeDtypeStruct((8, HEAD), F32)],
        scratch_shapes=[pltpu.VMEM((hp, HEAD, HEAD), F32)],
        compiler_params=_params("arbitrary", "arbitrary"),
    )(proj, proj, proj, proj, lb0, lb1, norm_w, o, states, dmix)


HALO_BLK = 16


def _f32(ref):
    return ref[...].astype(F32)


def _halo_prev(ref):
    return ref[...].astype(F32)[HALO_BLK - HALO:]


def _halo_next(ref):
    return ref[...].astype(F32)[:HALO]


def _conv3(x0, x1, x2, w_ref):
    y = x0 * w_ref[0:1, :]
    y = y + x1 * w_ref[1:2, :]
    return y + x2 * w_ref[2:3, :]


def _sconv_fwd(proj, w8, mix, name, tb=512):
    s = proj.shape[0]
    tb = min(tb, s)
    hb = tb // HALO_BLK

    def body(cb_ref, cc_ref, ch_ref, cch_ref, chh_ref, w_ref, mix_ref, y_ref):
        first = pl.program_id(0) == 0
        u = _f32(cc_ref) * _f32(ch_ref)
        uh = jnp.where(first, 0.0, _halo_prev(cch_ref) * _halo_prev(chh_ref))
        conv = _conv3(_shift_down(u, uh, 2), _shift_down(u, uh, 1), u, w_ref)
        y_ref[...] = (_f32(cb_ref) * conv).astype(BF16)

    def blk(g):
        return pl.BlockSpec((tb, HGRN_W), lambda j: (j, g))

    def halo(g):
        return pl.BlockSpec((HALO_BLK, HGRN_W), lambda j: (jnp.maximum(j * hb - 1, 0), g))

    return pl.pallas_call(
        body, name=name, grid=(s // tb,),
        in_specs=[blk(4), blk(5), blk(6), halo(5), halo(6), pl.BlockSpec((HALO, HGRN_W), lambda j: (0, 0)),
                  pl.BlockSpec(memory_space=pl.ANY)],
        out_specs=pl.BlockSpec((tb, HGRN_W), lambda j: (j, 1)),
        out_shape=jax.ShapeDtypeStruct(mix.shape, BF16),
        input_output_aliases={6: 0},
        compiler_params=_params("parallel"),
    )(proj, proj, proj, proj, proj, w8, mix)


def _sconv_bwd(proj, w8, dmix, dproj, name, tb=512):
    s = proj.shape[0]
    tb = min(tb, s)
    hb = tb // HALO_BLK
    nb = s // tb
    last_h = s // HALO_BLK - 1

    def body(cb_ref, cc_ref, ch_ref, cch_ref, chh_ref, cbn_ref, dy_ref, dyn_ref, w_ref, dproj_ref,
             dp_ref, gw_ref):
        j = pl.program_id(0)

        @pl.when(j == 0)
        def _():
            gw_ref[...] = jnp.zeros_like(gw_ref)

        cc, ch, cb = _f32(cc_ref), _f32(ch_ref), _f32(cb_ref)
        u = cc * ch
        uh = jnp.where(j == 0, 0.0, _halo_prev(cch_ref) * _halo_prev(chh_ref))
        u2, u1 = _shift_down(u, uh, 2), _shift_down(u, uh, 1)
        conv = _conv3(u2, u1, u, w_ref)
        dy = _f32(dy_ref)
        dp_ref[0] = (dy * conv).astype(BF16)
        dc = dy * cb
        dcn = jnp.where(j == nb - 1, 0.0, _halo_next(dyn_ref) * _halo_next(cbn_ref))
        gw_ref[0:8, :] += _rows8(dc * u2)
        gw_ref[8:16, :] += _rows8(dc * u1)
        gw_ref[16:24, :] += _rows8(dc * u)
        du = dc * w_ref[2:3, :] + _shift_up(dc, dcn, 1) * w_ref[1:2, :] + _shift_up(dc, dcn, 2) * w_ref[0:1, :]
        dp_ref[1] = (du * ch).astype(BF16)
        dp_ref[2] = (du * cc).astype(BF16)
        dp_ref[3] = jnp.zeros(dp_ref.shape[1:], BF16)

    def blk(g):
        return pl.BlockSpec((tb, HGRN_W), lambda j: (j, g))

    def halo_prev(g):
        return pl.BlockSpec((HALO_BLK, HGRN_W), lambda j: (jnp.maximum(j * hb - 1, 0), g))

    def halo_next(g):
        return pl.BlockSpec((HALO_BLK, HGRN_W), lambda j: (jnp.minimum((j + 1) * hb, last_h), g))

    return pl.pallas_call(
        body, name=name, grid=(nb,),
        in_specs=[blk(4), blk(5), blk(6), halo_prev(5), halo_prev(6), halo_next(4), blk(1), halo_next(1),
                  pl.BlockSpec((HALO, HGRN_W), lambda j: (0, 0)), pl.BlockSpec(memory_space=pl.ANY)],
        out_specs=[pl.BlockSpec((4, tb, HGRN_W), lambda j: (1, j, 0)), pl.BlockSpec((24, HGRN_W), lambda j: (0, 0))],
        out_shape=[jax.ShapeDtypeStruct(dproj.shape, BF16), jax.ShapeDtypeStruct((24, HGRN_W), F32)],
        input_output_aliases={9: 0},
        compiler_params=_params("arbitrary"),
    )(proj, proj, proj, proj, proj, proj, dmix, dmix, w8, dproj)


def _attn_fwd(q, kk, vv, name, tb=1024):
    s, d = q.shape
    m = kk.shape[0]
    tb = min(tb, s)
    scale = MEM_HEAD_DIM ** -0.5

    def body(q_ref, k_ref, v_ref, o_ref):
        for hh in range(MEM_HEADS):
            cols = slice(hh * MEM_HEAD_DIM, (hh + 1) * MEM_HEAD_DIM)
            sc = _dot_nt(q_ref[:, cols], k_ref[:, cols]) * scale
            sc = sc - jnp.max(sc, axis=-1, keepdims=True)
            e = jnp.exp(sc)
            p = e / jnp.sum(e, axis=-1, keepdims=True)
            o_ref[:, cols] = _dot_nn(p, v_ref[:, cols]).astype(BF16)

    full = pl.BlockSpec((m, d), lambda i: (0, 0))
    return pl.pallas_call(
        body, name=name, grid=(s // tb,),
        in_specs=[pl.BlockSpec((tb, d), lambda i: (i, 0)), full, full],
        out_specs=pl.BlockSpec((tb, d), lambda i: (i, 0)),
        out_shape=jax.ShapeDtypeStruct((s, d), BF16),
        compiler_params=pltpu.CompilerParams(dimension_semantics=("parallel",), vmem_limit_bytes=MM_VMEM_LIMIT),
    )(q, kk, vv)


def _attn_bwd(q, kk, vv, datt, name, tb=1024):
    s, d = q.shape
    m = kk.shape[0]
    tb = min(tb, s)
    scale = MEM_HEAD_DIM ** -0.5

    def body(q_ref, k_ref, v_ref, do_ref, dq_ref, dk_ref, dv_ref):
        @pl.when(pl.program_id(0) == 0)
        def _():
            dk_ref[...] = jnp.zeros_like(dk_ref)
            dv_ref[...] = jnp.zeros_like(dv_ref)

        for hh in range(MEM_HEADS):
            cols = slice(hh * MEM_HEAD_DIM, (hh + 1) * MEM_HEAD_DIM)
            qh, kh, vh, doh = q_ref[:, cols], k_ref[:, cols], v_ref[:, cols], do_ref[:, cols]
            sc = _dot_nt(qh, kh) * scale
            sc = sc - jnp.max(sc, axis=-1, keepdims=True)
            e = jnp.exp(sc)
            p = e / jnp.sum(e, axis=-1, keepdims=True)
            dp = _dot_nt(doh, vh)
            ds = p * (dp - jnp.sum(dp * p, axis=-1, keepdims=True)) * scale
            dq_ref[:, cols] = _dot_nn(ds, kh).astype(BF16)
            dk_ref[:, cols] += _dot_tn(ds, qh)
            dv_ref[:, cols] += _dot_tn(p, doh)

    full = pl.BlockSpec((m, d), lambda i: (0, 0))
    row = pl.BlockSpec((tb, d), lambda i: (i, 0))
    return pl.pallas_call(
        body, name=name, grid=(s // tb,),
        in_specs=[row, full, full, row],
        out_specs=[row, full, full],
        out_shape=[jax.ShapeDtypeStruct((s, d), BF16), jax.ShapeDtypeStruct((m, d), F32),
                   jax.ShapeDtypeStruct((m, d), F32)],
        compiler_params=pltpu.CompilerParams(dimension_semantics=("arbitrary",), vmem_limit_bytes=MM_VMEM_LIMIT),
    )(q, kk, vv, datt)


def _ffn_fwd(g, u, w8, bias, name, tb=512, tc=1408):
    s, f = g.shape
    tb = min(tb, s)
    tc = tc if f % tc == 0 else 512
    hb = tb // HALO_BLK

    def body(g_ref, gh_ref, u_ref, w_ref, b_ref, z_ref, a_ref):
        gv = _f32(g_ref)
        gh = jnp.where(pl.program_id(1) == 0, 0.0, _halo_prev(gh_ref))
        a = _conv3(_shift_down(gv, gh, 2), _shift_down(gv, gh, 1), gv, w_ref) + b_ref[...]
        a_ref[...] = a.astype(BF16)
        z_ref[...] = ((a * _sigmoid(a)) * _f32(u_ref)).astype(BF16)

    blk = pl.BlockSpec((tb, tc), lambda c, j: (j, c))
    return pl.pallas_call(
        body, name=name, grid=(f // tc, s // tb),
        in_specs=[blk, pl.BlockSpec((HALO_BLK, tc), lambda c, j: (jnp.maximum(j * hb - 1, 0), c)), blk,
                  pl.BlockSpec((HALO, tc), lambda c, j: (0, c)), pl.BlockSpec((1, tc), lambda c, j: (0, c))],
        out_specs=[blk, blk],
        out_shape=[jax.ShapeDtypeStruct((s, f), BF16), jax.ShapeDtypeStruct((s, f), BF16)],
        compiler_params=pltpu.CompilerParams(dimension_semantics=("parallel", "parallel"),
                                             vmem_limit_bytes=MM_VMEM_LIMIT),
    )(g, g, u, w8, bias)


def _ffn_bwd(a, g, u, dz, w8, name, tb=512, tc=1408):
    s, f = g.shape
    tb = min(tb, s)
    tc = tc if f % tc == 0 else 512
    nb = s // tb

    def body(a_ref, g_ref, u_ref, dz_ref, w_ref, dg_ref, du_ref, gb_ref, gw_ref, da_next):
        jj = pl.program_id(1)

        @pl.when(jj == 0)
        def _():
            gb_ref[...] = jnp.zeros_like(gb_ref)
            gw_ref[...] = jnp.zeros_like(gw_ref)
            da_next[...] = jnp.zeros_like(da_next)

        a = _f32(a_ref)
        sa = _sigmoid(a)
        dz = _f32(dz_ref)
        du_ref[...] = (dz * (a * sa)).astype(BF16)
        da = dz * _f32(u_ref) * (sa * (1.0 + a * (1.0 - sa)))
        gb_ref[...] += _rows8(da)
        dan = da_next[...]
        da1, da2 = _shift_up(da, dan, 1), _shift_up(da, dan, 2)
        gv = _f32(g_ref)
        gw_ref[0:8, :] += _rows8(da2 * gv)
        gw_ref[8:16, :] += _rows8(da1 * gv)
        gw_ref[16:24, :] += _rows8(da * gv)
        dg_ref[...] = (da * w_ref[2:3, :] + da1 * w_ref[1:2, :] + da2 * w_ref[0:1, :]).astype(BF16)
        da_next[...] = da[:HALO]

    blk = pl.BlockSpec((tb, tc), lambda c, jj: (nb - 1 - jj, c))
    return pl.pallas_call(
        body, name=name, grid=(f // tc, nb),
        in_specs=[blk, blk, blk, blk, pl.BlockSpec((HALO, tc), lambda c, jj: (0, c))],
        out_specs=[blk, blk, pl.BlockSpec((8, tc), lambda c, jj: (0, c)), pl.BlockSpec((24, tc), lambda c, jj: (0, c))],
        out_shape=[jax.ShapeDtypeStruct((s, f), BF16), jax.ShapeDtypeStruct((s, f), BF16),
                   jax.ShapeDtypeStruct((8, f), F32), jax.ShapeDtypeStruct((24, f), F32)],
        scratch_shapes=[pltpu.VMEM((HALO, tc), F32)],
        compiler_params=pltpu.CompilerParams(dimension_semantics=("parallel", "arbitrary"),
                                             vmem_limit_bytes=MM_VMEM_LIMIT),
    )(a, g, u, dz, w8)


def _window(ref, axis, slot, size):
    start = pl.multiple_of(slot * size, size)
    if axis == 0:
        return ref.at[pl.ds(start, size), :]
    return ref.at[:, pl.ds(start, size)]


def _chip_peers():
    x, y, c = lax.axis_index("x"), lax.axis_index("y"), lax.axis_index("c")
    peers = [(1 - x, y, c), (x, 1 - y, c), (1 - x, 1 - y, c)]
    slots = [2 * (1 - x) + y, 2 * x + (1 - y), 2 * (1 - x) + (1 - y)]
    return 2 * x + y, peers, slots


HBM_SPEC = pl.BlockSpec(memory_space=pltpu.HBM)
SEM_SPEC = pl.BlockSpec(memory_space=pltpu.SEMAPHORE)
EFFECT = pltpu.SideEffectType.DATAFLOW_SIDE_EFFECTING


def _hbm(a):
    return pltpu.with_memory_space_constraint(a, pltpu.HBM)


def _cast_into_full(x, axis, slot_arr, dtype, name, after=None):
    r, c = x.shape
    tr = _row_tile(r, 512)
    nb = r // tr
    full = (r * N_CHIPS, c) if axis == 0 else (r, c * N_CHIPS)

    def body(slot_ref, x_ref, *rest):
        rest[-1][...] = x_ref[...].astype(dtype)

    if axis == 0:
        out_map = lambda i, s: (s[0] * nb + i, 0)
    else:
        out_map = lambda i, s: (i, s[0])
    extra = [] if after is None else [after]
    return pl.pallas_call(
        body, name=name,
        grid_spec=pltpu.PrefetchScalarGridSpec(
            num_scalar_prefetch=1, grid=(nb,),
            in_specs=[pl.BlockSpec((tr, c), lambda i, s: (i, 0))] + [pl.BlockSpec(memory_space=pl.ANY)] * len(extra),
            out_specs=pl.BlockSpec((tr, c), out_map)),
        out_shape=jax.ShapeDtypeStruct(full, dtype),
        compiler_params=_params("parallel"),
    )(slot_arr, x, *extra)


def _piece(ref, axis, slot, half):
    size = ref.shape[axis] // N_CHIPS
    if half is None:
        return _window(ref, axis, slot, size)
    if axis == 0:
        h = size // 2
        return ref.at[pl.ds(pl.multiple_of(slot * size + half * h, h), h), :]
    h = ref.shape[0] // 2
    return ref.at[pl.ds(pl.multiple_of(half * h, h), h), pl.ds(pl.multiple_of(slot * size, size), size)]


def _gather_start(fulls, axes, split, groups, name):
    n, ng = len(fulls), len(groups)

    def body(*refs):
        outs = refs[n:]
        sems = outs[:2 * ng]
        thru = outs[2 * ng:2 * ng + n]
        token = outs[-1]
        slot, peers, _ = _chip_peers()
        c = lax.axis_index("c")
        for g, members in enumerate(groups):
            for i, t in enumerate(members):
                mine = _piece(thru[t], axes[t], slot, c if split[t] else None)
                for k in range(3):
                    pltpu.make_async_remote_copy(
                        src_ref=mine, dst_ref=mine, send_sem=sems[2 * g].at[3 * i + k],
                        recv_sem=sems[2 * g + 1].at[3 * i + k], device_id=peers[k], device_id_type=MESH).start()
        token[...] = jnp.zeros_like(token)

    sem_shapes = []
    for members in groups:
        sem_shapes += [pltpu.SemaphoreType.DMA((3 * len(members),))] * 2
    res = pl.pallas_call(
        body, name=name,
        in_specs=[HBM_SPEC] * n,
        out_specs=[SEM_SPEC] * (2 * ng) + [HBM_SPEC] * n + [pl.BlockSpec(memory_space=pltpu.VMEM)],
        out_shape=sem_shapes + [pltpu.HBM(f.shape, f.dtype) for f in fulls] + [jax.ShapeDtypeStruct((8, 128), F32)],
        input_output_aliases={t: 2 * ng + t for t in range(n)},
        compiler_params=pltpu.CompilerParams(has_side_effects=EFFECT),
    )(*[_hbm(f) for f in fulls])
    sems = [(res[2 * g], res[2 * g + 1]) for g in range(ng)]
    return sems, list(res[2 * ng:2 * ng + n]), res[-1]


def _gather_relay(fulls, axes, split, sems, after, name):
    n = len(fulls)
    nsplit = sum(split)

    def body(*refs):
        send_sems, recv_sems = refs[n], refs[n + 1]
        outs = refs[n + 3:]
        d_send, d_recv = outs[0], outs[1]
        thru = outs[2:2 + n]
        token = outs[-1]
        slot, peers, slots = _chip_peers()
        c = lax.axis_index("c")
        sibling = (lax.axis_index("x"), lax.axis_index("y"), 1 - c)
        for t in range(n):
            half = c if split[t] else None
            for k in range(3):
                cp = pltpu.make_async_remote_copy(
                    src_ref=_piece(thru[t], axes[t], slot, half), dst_ref=_piece(thru[t], axes[t], slots[k], half),
                    send_sem=send_sems.at[3 * t + k], recv_sem=recv_sems.at[3 * t + k],
                    device_id=peers[k], device_id_type=MESH)
                cp.wait_send()
                cp.wait_recv()
        i = 0
        for t in range(n):
            if not split[t]:
                continue
            for k in range(3):
                got = _piece(thru[t], axes[t], slots[k], c)
                pltpu.make_async_remote_copy(
                    src_ref=got, dst_ref=got, send_sem=d_send.at[3 * i + k], recv_sem=d_recv.at[3 * i + k],
                    device_id=sibling, device_id_type=MESH).start()
            i += 1
        token[...] = jnp.zeros_like(token)

    res = pl.pallas_call(
        body, name=name,
        in_specs=[HBM_SPEC] * n + [SEM_SPEC, SEM_SPEC, pl.BlockSpec(memory_space=pl.ANY)],
        out_specs=[SEM_SPEC, SEM_SPEC] + [HBM_SPEC] * n + [pl.BlockSpec(memory_space=pltpu.VMEM)],
        out_shape=[pltpu.SemaphoreType.DMA((3 * nsplit,)), pltpu.SemaphoreType.DMA((3 * nsplit,))]
        + [pltpu.HBM(f.shape, f.dtype) for f in fulls] + [jax.ShapeDtypeStruct((8, 128), F32)],
        input_output_aliases={t: 2 + t for t in range(n)},
        compiler_params=pltpu.CompilerParams(has_side_effects=EFFECT),
    )(*fulls, sems[0], sems[1], after)
    return (res[0], res[1]), list(res[2:2 + n]), res[-1]


def _gather_finish(fulls, axes, split, sems, after, name):
    n = len(fulls)

    def body(*refs):
        d_send, d_recv = refs[n], refs[n + 1]
        thru = refs[n + 3:]
        _, _, slots = _chip_peers()
        c = lax.axis_index("c")
        sibling = (lax.axis_index("x"), lax.axis_index("y"), 1 - c)
        i = 0
        for t in range(n):
            if not split[t]:
                continue
            for k in range(3):
                cp = pltpu.make_async_remote_copy(
                    src_ref=_piece(thru[t], axes[t], slots[k], c), dst_ref=_piece(thru[t], axes[t], slots[k], 1 - c),
                    send_sem=d_send.at[3 * i + k], recv_sem=d_recv.at[3 * i + k],
                    device_id=sibling, device_id_type=MESH)
                cp.wait_send()
                cp.wait_recv()
            i += 1

    return pl.pallas_call(
        body, name=name,
        in_specs=[HBM_SPEC] * n + [SEM_SPEC, SEM_SPEC, pl.BlockSpec(memory_space=pl.ANY)],
        out_specs=[HBM_SPEC] * n,
        out_shape=[pltpu.HBM(f.shape, f.dtype) for f in fulls],
        input_output_aliases={t: t for t in range(n)},
        compiler_params=pltpu.CompilerParams(has_side_effects=EFFECT),
    )(*fulls, sems[0], sems[1], after)


def _scatter_start(grads_bf16, axes, name):
    n = len(grads_bf16)

    def shard_shape(g, ax):
        return (g.shape[0] // N_CHIPS, g.shape[1]) if ax == 0 else (g.shape[0], g.shape[1] // N_CHIPS)

    shapes = [shard_shape(g, ax) for g, ax in zip(grads_bf16, axes)]

    def body(*refs):
        outs = refs[2 * n:]
        send_sems, recv_sems = outs[0], outs[1]
        gb, land = outs[2:2 + n], outs[2 + n:2 + 2 * n]
        token = outs[-1]
        _, peers, slots = _chip_peers()
        for t in range(n):
            size = shapes[t][axes[t]]
            for k in range(3):
                pltpu.make_async_remote_copy(
                    src_ref=_window(gb[t], axes[t], slots[k], size), dst_ref=land[t].at[k],
                    send_sem=send_sems.at[3 * t + k], recv_sem=recv_sems.at[3 * t + k],
                    device_id=peers[k], device_id_type=MESH).start()
        token[...] = jnp.zeros_like(token)

    lands = [_hbm(lax.empty((3,) + sh, BF16)) for sh in shapes]
    res = pl.pallas_call(
        body, name=name,
        in_specs=[HBM_SPEC] * (2 * n),
        out_specs=[SEM_SPEC, SEM_SPEC] + [HBM_SPEC] * (2 * n) + [pl.BlockSpec(memory_space=pltpu.VMEM)],
        out_shape=[pltpu.SemaphoreType.DMA((3 * n,)), pltpu.SemaphoreType.DMA((3 * n,))]
        + [pltpu.HBM(g.shape, g.dtype) for g in grads_bf16] + [pltpu.HBM((3,) + sh, BF16) for sh in shapes]
        + [jax.ShapeDtypeStruct((8, 128), F32)],
        input_output_aliases={t: 2 + t for t in range(2 * n)},
        compiler_params=pltpu.CompilerParams(has_side_effects=EFFECT),
    )(*[_hbm(g) for g in grads_bf16], *lands)
    return (res[0], res[1]), list(res[2:2 + n]), list(res[2 + n:2 + 2 * n]), res[-1]


def _scatter_wait(grads_thru, lands_thru, axes, sems, after, name):
    n = len(grads_thru)

    def body(*refs):
        send_sems, recv_sems = refs[2 * n], refs[2 * n + 1]
        outs = refs[2 * n + 3:]
        gb, land = outs[:n], outs[n:]
        _, peers, slots = _chip_peers()
        for t in range(n):
            size = land[t].shape[1 + axes[t]]
            for k in range(3):
                cp = pltpu.make_async_remote_copy(
                    src_ref=_window(gb[t], axes[t], slots[k], size), dst_ref=land[t].at[k],
                    send_sem=send_sems.at[3 * t + k], recv_sem=recv_sems.at[3 * t + k],
                    device_id=peers[k], device_id_type=MESH)
                cp.wait_send()
                cp.wait_recv()

    res = pl.pallas_call(
        body, name=name,
        in_specs=[HBM_SPEC] * (2 * n) + [SEM_SPEC, SEM_SPEC, pl.BlockSpec(memory_space=pl.ANY)],
        out_specs=[HBM_SPEC] * (2 * n),
        out_shape=[pltpu.HBM(g.shape, g.dtype) for g in grads_thru] + [pltpu.HBM(l.shape, l.dtype) for l in lands_thru],
        input_output_aliases={t: t for t in range(2 * n)},
        compiler_params=pltpu.CompilerParams(has_side_effects=EFFECT),
    )(*grads_thru, *lands_thru, sems[0], sems[1], after)
    return list(res[n:])


def _sibling_start(arrs, name):
    n = len(arrs)

    def body(*refs):
        outs = refs[2 * n:]
        send_sems, recv_sems = outs[0], outs[1]
        src, land = outs[2:2 + n], outs[2 + n:2 + 2 * n]
        token = outs[-1]
        sibling = (lax.axis_index("x"), lax.axis_index("y"), 1 - lax.axis_index("c"))
        for t in range(n):
            pltpu.make_async_remote_copy(
                src_ref=src[t], dst_ref=land[t], send_sem=send_sems.at[t], recv_sem=recv_sems.at[t],
                device_id=sibling, device_id_type=MESH).start()
        token[...] = jnp.zeros_like(token)

    lands = [_hbm(lax.empty(a.shape, a.dtype)) for a in arrs]
    res = pl.pallas_call(
        body, name=name,
        in_specs=[HBM_SPEC] * (2 * n),
        out_specs=[SEM_SPEC, SEM_SPEC] + [HBM_SPEC] * (2 * n) + [pl.BlockSpec(memory_space=pltpu.VMEM)],
        out_shape=[pltpu.SemaphoreType.DMA((n,)), pltpu.SemaphoreType.DMA((n,))]
        + [pltpu.HBM(a.shape, a.dtype) for a in arrs] * 2 + [jax.ShapeDtypeStruct((8, 128), F32)],
        input_output_aliases={t: 2 + t for t in range(2 * n)},
        compiler_params=pltpu.CompilerParams(has_side_effects=EFFECT),
    )(*[_hbm(a) for a in arrs], *lands)
    return (res[0], res[1]), list(res[2:2 + n]), list(res[2 + n:2 + 2 * n]), res[-1]


def _sibling_wait(src_thru, lands_thru, sems, after, name):
    n = len(src_thru)

    def body(*refs):
        send_sems, recv_sems = refs[2 * n], refs[2 * n + 1]
        outs = refs[2 * n + 3:]
        src, land = outs[:n], outs[n:]
        sibling = (lax.axis_index("x"), lax.axis_index("y"), 1 - lax.axis_index("c"))
        for t in range(n):
            cp = pltpu.make_async_remote_copy(
                src_ref=src[t], dst_ref=land[t], send_sem=send_sems.at[t], recv_sem=recv_sems.at[t],
                device_id=sibling, device_id_type=MESH)
            cp.wait_send()
            cp.wait_recv()

    res = pl.pallas_call(
        body, name=name,
        in_specs=[HBM_SPEC] * (2 * n) + [SEM_SPEC, SEM_SPEC, pl.BlockSpec(memory_space=pl.ANY)],
        out_specs=[HBM_SPEC] * (2 * n),
        out_shape=[pltpu.HBM(a.shape, a.dtype) for a in src_thru] * 2,
        input_output_aliases={t: t for t in range(2 * n)},
        compiler_params=pltpu.CompilerParams(has_side_effects=EFFECT),
    )(*src_thru, *lands_thru, sems[0], sems[1], after)
    return list(res[:n]), list(res[n:])


def _all_reduce_small(packed, name):
    nc = packed.shape[1]
    vmem = pl.BlockSpec(memory_space=pltpu.VMEM)

    def body(in_ref, out_ref, gbuf, send_sems, recv_sems):
        x, y, c = lax.axis_index("x"), lax.axis_index("y"), lax.axis_index("c")
        me = 4 * x + 2 * y + c
        gbuf[me] = jnp.sum(in_ref[...], axis=0, keepdims=True)
        copies = []
        for k in range(1, 8):
            peer = (x ^ ((k >> 2) & 1), y ^ ((k >> 1) & 1), c ^ (k & 1))
            rc = pltpu.make_async_remote_copy(
                src_ref=gbuf.at[me], dst_ref=gbuf.at[me], send_sem=send_sems.at[k - 1], recv_sem=recv_sems.at[k - 1],
                device_id=peer, device_id_type=MESH)
            rc.start()
            copies.append(rc)
        for k in range(1, 8):
            peer = (x ^ ((k >> 2) & 1), y ^ ((k >> 1) & 1), c ^ (k & 1))
            pltpu.make_async_remote_copy(
                src_ref=gbuf.at[me], dst_ref=gbuf.at[me ^ k], send_sem=send_sems.at[k - 1],
                recv_sem=recv_sems.at[k - 1], device_id=peer, device_id_type=MESH).wait_recv()
        for rc in copies:
            rc.wait_send()
        tot = gbuf[0]
        for d in range(1, 8):
            tot = tot + gbuf[d]
        out_ref[...] = tot

    return pl.pallas_call(
        body, name=name,
        in_specs=[vmem], out_specs=vmem,
        out_shape=jax.ShapeDtypeStruct((1, nc), F32),
        scratch_shapes=[pltpu.VMEM((8, 1, nc), F32), pltpu.SemaphoreType.DMA((7,)), pltpu.SemaphoreType.DMA((7,))],
    )(packed)


def _sum4(g_full, axis, slot_arr, recv, name):
    _, r, c = recv.shape
    tr = _row_tile(r, 512)
    nb = r // tr

    def body(slot_ref, own_ref, recv_ref, o_ref):
        acc = own_ref[...]
        for k in range(3):
            acc = acc + recv_ref[k].astype(F32)
        o_ref[...] = acc

    if axis == 0:
        own_map = lambda i, s: (s[0] * nb + i, 0)
    else:
        own_map = lambda i, s: (i, s[0])
    return pl.pallas_call(
        body, name=name,
        grid_spec=pltpu.PrefetchScalarGridSpec(
            num_scalar_prefetch=1, grid=(nb,),
            in_specs=[pl.BlockSpec((tr, c), own_map), pl.BlockSpec((3, tr, c), lambda i, s: (0, i, 0))],
            out_specs=pl.BlockSpec((tr, c), lambda i, s: (i, 0))),
        out_shape=jax.ShapeDtypeStruct((r, c), F32),
        compiler_params=pltpu.CompilerParams(dimension_semantics=("parallel",), vmem_limit_bytes=MM_VMEM_LIMIT),
    )(slot_arr, g_full, recv)


def _adamw(w, g_parts, m, v, name):
    r, c = w.shape
    tr = r if r % 128 else _row_tile(r, 256)
    npart = len(g_parts)

    def body(*refs):
        w_ref = refs[0]
        g_refs = refs[1:1 + npart]
        m_ref, v_ref, g_out, d_out, m_out, v_out = refs[1 + npart:]
        g = g_refs[0][...]
        for gr in g_refs[1:]:
            g = g + gr[...]
        mm = ADAM_B1 * m_ref[...] + (1.0 - ADAM_B1) * g
        vv = ADAM_B2 * v_ref[...] + (1.0 - ADAM_B2) * (g * g)
        m_hat = mm / (1.0 - ADAM_B1 ** ADAM_STEP)
        v_hat = vv / (1.0 - ADAM_B2 ** ADAM_STEP)
        g_out[...] = g
        d_out[...] = -ADAM_LR * (m_hat / (jnp.sqrt(v_hat) + ADAM_EPS) + ADAM_WD * w_ref[...])
        m_out[...] = mm
        v_out[...] = vv

    blk = pl.BlockSpec((tr, c), lambda i: (i, 0))
    shp = jax.ShapeDtypeStruct((r, c), F32)
    return pl.pallas_call(
        body, name=name, grid=(r // tr,),
        in_specs=[blk] * (3 + npart), out_specs=[blk] * 4, out_shape=[shp] * 4,
        compiler_params=pltpu.CompilerParams(dimension_semantics=("parallel",), vmem_limit_bytes=MM_VMEM_LIMIT),
    )(w, *g_parts, m, v)


def _pad_rows8(w):
    return jnp.pad(w, ((0, HALO - w.shape[0]), (0, 0)))


def kernel(x, mem, hgrn_lb, norm1_w, w_in, hgrn_norm_w, sconv_w, w_out, norm2_w, mem_norm_w, wq, wk, wv, wo, norm3_w, w_gate, w_up, ffn_conv_w, ffn_conv_b, w_down, final_norm_w, loss_target, m_hgrn_lb, m_norm1_w, m_w_in, m_hgrn_norm_w, m_sconv_w, m_w_out, m_norm2_w, m_mem_norm_w, m_wq, m_wk, m_wv, m_wo, m_norm3_w, m_w_gate, m_w_up, m_ffn_conv_w, m_ffn_conv_b, m_w_down, m_final_norm_w, v_hgrn_lb, v_norm1_w, v_w_in, v_hgrn_norm_w, v_sconv_w, v_w_out, v_norm2_w, v_mem_norm_w, v_wq, v_wk, v_wv, v_wo, v_norm3_w, v_w_gate, v_w_up, v_ffn_conv_w, v_ffn_conv_b, v_w_down, v_final_norm_w):
    xs, mems, tgt = x[0], mem[0], loss_target[0]
    d = xs.shape[1]
    fnw = final_norm_w.reshape(1, d)

    big = {"w_in": (w_in[0], 1), "w_out": (w_out[0], 0), "wq": (wq[0], 0), "wk": (wk[0], 0), "wv": (wv[0], 0),
           "wo": (wo[0], 0), "w_gate": (w_gate[0], 1), "w_up": (w_up[0], 1), "w_down": (w_down[0], 0)}
    names = list(big)
    slot_arr = (2 * lax.axis_index("x") + lax.axis_index("y")).astype(jnp.int32).reshape(1)
    gnames = names + ["sconv8", "fconv8"]
    axes = [big[n][1] for n in names] + [1, 1]
    groups = [["w_in"], ["w_out", "sconv8"], ["wq", "wk", "wv", "wo"], ["w_gate", "w_up", "fconv8", "w_down"]]
    gidx = [[gnames.index(n) for n in grp] for grp in groups]
    split = [True] * len(names) + [False, False]
    first = _cast_into_full(big["w_in"][0], 1, slot_arr, BF16, "cast_w_in")
    sems0, first, tok0 = _gather_start([first], [1], [True], [[0]], "gather_start_w_in")
    rest = [_cast_into_full(big[n][0], big[n][1], slot_arr, BF16, "cast_" + n, after=tok0) for n in names[1:]]
    rest += [_cast_into_full(_pad_rows8(sconv_w[0]), 1, slot_arr, F32, "cast_sconv_w", after=tok0),
             _cast_into_full(_pad_rows8(ffn_conv_w[0]), 1, slot_arr, F32, "cast_ffn_conv_w", after=tok0)]
    sems1, rest, tok = _gather_start(rest, axes[1:], split[1:], [[t - 1 for t in idx] for idx in gidx[1:]],
                                     "gather_start")
    gsems, fulls = sems0 + sems1, first + rest
    wf, relayed = {}, {}

    def gather_relay(g, after):
        idx = gidx[g]
        dsems, arrs, token = _gather_relay([fulls[t] for t in idx], [axes[t] for t in idx], [split[t] for t in idx],
                                           gsems[g], after, "gather_relay_%d" % g)
        relayed[g] = (dsems, arrs)
        return token[0:1, 0:1]

    def gather_finish(g, after):
        idx = gidx[g]
        dsems, arrs = relayed[g]
        got = _gather_finish(arrs, [axes[t] for t in idx], [split[t] for t in idx], dsems, after,
                             "gather_finish_%d" % g)
        wf.update(zip(groups[g], got))

    lb0, lb1 = hgrn_lb[0:1], hgrn_lb[1:2]

    h1 = _rmsnorm_fwd(xs, norm1_w + tok[0:1, 0:1], "norm1")
    gather_relay(0, h1)
    gather_finish(0, h1)
    proj = _matmul(h1, wf["w_in"], "nn", "proj_in", out_dtype=BF16, tn=1792)
    t1 = gather_relay(1, proj)
    o_h, og, states = _hgrn_fwd(proj, lb0, lb1, hgrn_norm_w + t1, "hgrn_fwd")
    gather_finish(1, o_h)
    t2 = gather_relay(2, o_h)
    sconv8 = wf["sconv8"]
    mix = _sconv_fwd(proj, sconv8 + t2, og, "sconv_fwd")
    x1 = _matmul(mix, wf["w_out"], "nn", "proj_out", residual=xs)
    t3 = gather_relay(3, x1)
    h2 = _rmsnorm_fwd(x1, norm2_w + t3, "norm2")
    mem_n = _rmsnorm_fwd(mems, mem_norm_w, "norm_mem")
    gather_finish(2, h2)
    qa = _matmul(h2, wf["wq"], "nn", "attn_q", out_dtype=BF16)
    ka = _matmul(mem_n, wf["wk"], "nn", "attn_k", out_dtype=BF16)
    va = _matmul(mem_n, wf["wv"], "nn", "attn_v", out_dtype=BF16)
    att = _attn_fwd(qa, ka, va, "attn_fwd")
    x2 = _matmul(att, wf["wo"], "nn", "attn_o", residual=x1)
    h3 = _rmsnorm_fwd(x2, norm3_w, "norm3")
    gather_finish(3, h3)
    fconv8 = wf["fconv8"]
    gate = _matmul(h3, wf["w_gate"], "nn", "ffn_gate", out_dtype=BF16)
    up = _matmul(h3, wf["w_up"], "nn", "ffn_up", out_dtype=BF16)
    z, act = _ffn_fwd(gate, up, fconv8, ffn_conv_b, "ffn_act")
    x3 = _matmul(z, wf["w_down"], "nn", "ffn_down", residual=x2)

    dx3, dx3b, g_final, loss8 = _final_loss_bwd(x3, tgt, fnw, "loss_bwd")
    gw = {}
    dz = _matmul(dx3b, wf["w_down"], "nt", "d_z", out_dtype=BF16)
    gw["w_down"] = _matmul(z, dx3b, "tn", "g_w_down", extra_bf16=True)
    dgate, du, g_fb, g_fw = _ffn_bwd(act, gate, up, dz, fconv8, "ffn_act_bwd")
    dh3 = _matmul(dgate, wf["w_gate"], "nt", "d_h3_gate")
    dh3 = _matmul(du, wf["w_up"], "nt", "d_h3_up", residual=dh3, out_dtype=BF16)
    gw["w_gate"] = _matmul(h3, dgate, "tn", "g_w_gate", extra_bf16=True)
    gw["w_up"] = _matmul(h3, du, "tn", "g_w_up", extra_bf16=True)
    pending = []

    def scatter_start(grp):
        sems, g_thru, lands, token = _scatter_start([gw[n][1] for n in grp], [big[n][1] for n in grp],
                                                    "scatter_start_" + grp[0])
        pending.append((grp, sems, g_thru, lands))
        return token[0:1, 0:1]

    tok1 = scatter_start(["w_down", "w_gate", "w_up"])
    dx2, dx2b, g_n3 = _rmsnorm_bwd(dh3, x2, norm3_w + tok1, dx3, "norm3_bwd")
    datt = _matmul(dx2b, wf["wo"], "nt", "d_att", out_dtype=BF16)
    gw["wo"] = _matmul(att, dx2b, "tn", "g_wo", extra_bf16=True)
    dqa, dka, dva = _attn_bwd(qa, ka, va, datt, "attn_bwd")
    dh2 = _matmul(dqa, wf["wq"], "nt", "d_h2", out_dtype=BF16)
    gw["wq"] = _matmul(h2, dqa, "tn", "g_wq", extra_bf16=True)
    gw["wk"] = _matmul(mem_n, dka, "tn", "g_wk", extra_bf16=True)
    gw["wv"] = _matmul(mem_n, dva, "tn", "g_wv", extra_bf16=True)
    tok2 = scatter_start(["wo", "wq", "wk", "wv"])
    dmem_n = _matmul(dka, wf["wk"], "nt", "d_memn_k")
    dmem_n = _matmul(dva, wf["wv"], "nt", "d_memn_v", residual=dmem_n)
    _, _, g_nm = _rmsnorm_bwd(dmem_n, mems, mem_norm_w, None, "norm_mem_bwd")
    dx1, dx1b, g_n2 = _rmsnorm_bwd(dh2, x1, norm2_w + tok2, dx2, "norm2_bwd")
    dmix = _matmul(dx1b, wf["w_out"], "nt", "d_mix", out_dtype=BF16)
    gw["w_out"] = _matmul(mix, dx1b, "tn", "g_w_out", extra_bf16=True)
    tok3 = scatter_start(["w_out"])
    dproj, g_lb, g_hn = _hgrn_bwd(proj, lb0, lb1, hgrn_norm_w + tok3, o_h, states, dmix, "hgrn_bwd")
    dproj, g_sw = _sconv_bwd(proj, sconv8, dmix, dproj, "sconv_bwd")
    gw["w_in"] = _matmul(h1, dproj, "tn", "g_w_in", extra_bf16=True, groups=7)
    tok4 = scatter_start(["w_in"])
    dh1 = _matmul(dproj, wf["w_in"], "nt", "d_h1", out_dtype=BF16, groups=7, tn=2048)
    dx, _, g_n1 = _rmsnorm_bwd(dh1, xs, norm1_w + tok4, dx1, "norm1_bwd")

    small = [g_n1, g_n2, g_n3, g_final, g_nm, g_lb, g_hn, g_fb,
             g_sw[0:8], g_sw[8:16], g_sw[16:24], g_fw[0:8], g_fw[8:16], g_fw[16:24], loss8]
    widths = [a.shape[1] for a in small]
    tot = _all_reduce_small(jnp.concatenate(small, axis=1), "all_reduce_small")
    offs = [0]
    for wd_ in widths:
        offs.append(offs[-1] + wd_)
    sm = [tot[:, offs[i]:offs[i + 1]] for i in range(len(small))]
    s_n1, s_n2, s_n3, s_final, s_nm, s_lb, s_hn, s_fb = sm[:8]
    s_sw = jnp.concatenate(sm[8:11], axis=0)
    s_fw = jnp.concatenate(sm[11:14], axis=0)
    loss = sm[14][0, 0]
    slot = 2 * lax.axis_index("x") + lax.axis_index("y")
    s_sw = lax.dynamic_slice_in_dim(s_sw, slot * (HGRN_W // N_CHIPS), HGRN_W // N_CHIPS, axis=1)
    fsh = ffn_conv_w.shape[2]
    s_fw = lax.dynamic_slice_in_dim(s_fw, slot * fsh, fsh, axis=1)
    s_lb2 = jnp.concatenate([s_lb, -s_lb], axis=0)

    swaps = []
    after = tot
    for grp, sems, g_thru, lands in pending:
        got = _scatter_wait(g_thru, lands, [big[n][1] for n in grp], sems, after, "scatter_wait_" + grp[0])
        sums = [_sum4(gw[n][0], big[n][1], slot_arr, r, "core_sum_" + n) for n, r in zip(grp, got)]
        ssems, s_thru, s_lands, after = _sibling_start(sums, "sibling_start_" + grp[0])
        swaps.append((grp, ssems, s_thru, s_lands))

    moments = {"hgrn_lb": (m_hgrn_lb, v_hgrn_lb), "norm1_w": (m_norm1_w, v_norm1_w), "w_in": (m_w_in, v_w_in),
               "hgrn_norm_w": (m_hgrn_norm_w, v_hgrn_norm_w), "sconv_w": (m_sconv_w, v_sconv_w),
               "w_out": (m_w_out, v_w_out), "norm2_w": (m_norm2_w, v_norm2_w),
               "mem_norm_w": (m_mem_norm_w, v_mem_norm_w), "wq": (m_wq, v_wq), "wk": (m_wk, v_wk), "wv": (m_wv, v_wv),
               "wo": (m_wo, v_wo), "norm3_w": (m_norm3_w, v_norm3_w), "w_gate": (m_w_gate, v_w_gate),
               "w_up": (m_w_up, v_w_up), "ffn_conv_w": (m_ffn_conv_w, v_ffn_conv_w),
               "ffn_conv_b": (m_ffn_conv_b, v_ffn_conv_b), "w_down": (m_w_down, v_w_down),
               "final_norm_w": (m_final_norm_w, v_final_norm_w)}
    weights = {"hgrn_lb": hgrn_lb, "norm1_w": norm1_w, "w_in": w_in, "hgrn_norm_w": hgrn_norm_w, "sconv_w": sconv_w,
               "w_out": w_out, "norm2_w": norm2_w, "mem_norm_w": mem_norm_w, "wq": wq, "wk": wk, "wv": wv, "wo": wo,
               "norm3_w": norm3_w, "w_gate": w_gate, "w_up": w_up, "ffn_conv_w": ffn_conv_w, "ffn_conv_b": ffn_conv_b,
               "w_down": w_down, "final_norm_w": final_norm_w}
    small_g = {"hgrn_lb": s_lb2, "norm1_w": s_n1, "hgrn_norm_w": s_hn, "sconv_w": s_sw, "norm2_w": s_n2,
               "mem_norm_w": s_nm, "norm3_w": s_n3, "ffn_conv_w": s_fw, "ffn_conv_b": s_fb, "final_norm_w": s_final}
    order = list(weights)
    res = {}

    def adamw(n, parts):
        shape = weights[n].shape
        w2 = weights[n].reshape((-1, shape[-1]))
        m2, v2 = (t.reshape(w2.shape) for t in moments[n])
        res[n] = [t.reshape(shape) for t in _adamw(w2, [p.reshape(w2.shape) for p in parts], m2, v2, "adamw_" + n)]

    for n in order:
        if n not in big:
            adamw(n, [small_g[n]])
    after = after + res["final_norm_w"][1][0]
    for grp, ssems, s_thru, s_lands in swaps:
        own, other = _sibling_wait(s_thru, s_lands, ssems, after, "sibling_wait_" + grp[0])
        for n, a, b in zip(grp, own, other):
            adamw(n, [a, b])
        after = res[grp[-1]][1]

    return (loss, dx[None], *[res[n][0] for n in order], *[res[n][1] for n in order],
            *[res[n][2] for n in order], *[res[n][3] for n in order])
```

```python
import jax
import jax.numpy as jnp
from jax import lax
from jax.experimental import pallas as pl
from jax.experimental.pallas import tpu as pltpu

F32 = jnp.float32
BF16 = jnp.bfloat16
MESH = pl.DeviceIdType.MESH

EPS = 1e-6
HGRN_W = 1024
HEAD = 128
N_HEADS = 8
CHUNK = 128
HGRN_UNROLL = 8
HGRN_HEADS_PER_STEP = 4
DPROJ_GROUPS = 8
MEM_HEADS = 4
MEM_HEAD_DIM = 512
N_CHIPS = 4
HALO = 8

ADAM_LR = 0.001
ADAM_B1 = 0.9
ADAM_B2 = 0.999
ADAM_EPS = 1e-08
ADAM_WD = 0.01
ADAM_STEP = 10


def _sigmoid(x):
    return 1.0 / (1.0 + jnp.exp(-x))


def _dot(a, b, dims):
    return lax.dot_general(a.astype(BF16), b.astype(BF16), (dims, ((), ())),
                           preferred_element_type=F32)


def _dot_nn(a, b):
    return _dot(a, b, ((1,), (0,)))


def _dot_nt(a, b):
    return _dot(a, b, ((1,), (1,)))


def _dot_tn(a, b):
    return _dot(a, b, ((0,), (0,)))


def _hdot(a, b, dims):
    return lax.dot_general(a, b, (dims, ((), ())), precision=lax.Precision.HIGH, preferred_element_type=F32)


def _hdot_nn(a, b):
    return _hdot(a, b, ((1,), (0,)))


def _hdot_tn(a, b):
    return _hdot(a, b, ((0,), (0,)))


def _exact_ones_dot(ones_bf16, x):
    hi = x.astype(BF16)
    r1 = x - hi.astype(F32)
    mid = r1.astype(BF16)
    lo = (r1 - mid.astype(F32)).astype(BF16)
    dims = (((1,), (0,)), ((), ()))
    return (lax.dot_general(ones_bf16, hi, dims, preferred_element_type=F32)
            + lax.dot_general(ones_bf16, mid, dims, preferred_element_type=F32)
            + lax.dot_general(ones_bf16, lo, dims, preferred_element_type=F32))


def _rows8(v):
    t, c = v.shape
    return v.reshape(t // 8, 8, c).sum(axis=0)


def _shift_down(x, halo, s):
    rolled = pltpu.roll(x, s, 0)
    hrolled = pltpu.roll(halo, s, 0)
    row = lax.broadcasted_iota(jnp.int32, hrolled.shape, 0)
    head = jnp.where(row < s, hrolled, rolled[:HALO])
    return jnp.concatenate([head, rolled[HALO:]], axis=0)


def _shift_up(x, halo, s):
    t = x.shape[0]
    rolled = pltpu.roll(x, t - s, 0)
    hrolled = pltpu.roll(halo, HALO - s, 0)
    row = lax.broadcasted_iota(jnp.int32, hrolled.shape, 0)
    tail = jnp.where(row >= HALO - s, hrolled, rolled[t - HALO:])
    return jnp.concatenate([rolled[:t - HALO], tail], axis=0)


def _params(*sem):
    return pltpu.CompilerParams(dimension_semantics=sem, vmem_limit_bytes=MM_VMEM_LIMIT)


def _row_tile(r, pref):
    while r % pref:
        pref //= 2
    return pref


def _rmsnorm_fwd(x, w, name, tm=512):
    s, d = x.shape
    tm = min(tm, s)

    def body(x_ref, w_ref, o_ref):
        xv = x_ref[...]
        r = lax.rsqrt(jnp.mean(xv * xv, axis=-1, keepdims=True) + EPS)
        o_ref[...] = ((xv * r) * w_ref[...]).astype(BF16)

    return pl.pallas_call(
        body, name=name, grid=(s // tm,),
        in_specs=[pl.BlockSpec((tm, d), lambda i: (i, 0)), pl.BlockSpec((1, d), lambda i: (0, 0))],
        out_specs=pl.BlockSpec((tm, d), lambda i: (i, 0)),
        out_shape=jax.ShapeDtypeStruct((s, d), BF16),
        compiler_params=_params("parallel"),
    )(x, w)


def _rmsnorm_bwd(dh, x, w, dres, name, tm=512):
    s, d = x.shape
    tm = min(tm, s)
    has_res = dres is not None

    def body(*refs):
        if has_res:
            dh_ref, x_ref, w_ref, dres_ref, dx_ref, dxb_ref, gw_ref = refs
        else:
            dh_ref, x_ref, w_ref, dx_ref, dxb_ref, gw_ref = refs

        @pl.when(pl.program_id(0) == 0)
        def _():
            gw_ref[...] = jnp.zeros_like(gw_ref)

        xv = x_ref[...]
        dhv = dh_ref[...].astype(F32)
        r = lax.rsqrt(jnp.mean(xv * xv, axis=-1, keepdims=True) + EPS)
        xhat = xv * r
        gw_ref[...] += _rows8(dhv * xhat)
        dxh = dhv * w_ref[...]
        dx = r * (dxh - xhat * jnp.mean(dxh * xhat, axis=-1, keepdims=True))
        if has_res:
            dx = dres_ref[...] + dx
        dx_ref[...] = dx
        dxb_ref[...] = dx.astype(BF16)

    row = pl.BlockSpec((tm, d), lambda i: (i, 0))
    in_specs = [row, row, pl.BlockSpec((1, d), lambda i: (0, 0))] + ([row] if has_res else [])
    args = (dh, x, w) + ((dres,) if has_res else ())
    return pl.pallas_call(
        body, name=name, grid=(s // tm,),
        in_specs=in_specs,
        out_specs=[row, row, pl.BlockSpec((8, d), lambda i: (0, 0))],
        out_shape=[jax.ShapeDtypeStruct((s, d), F32), jax.ShapeDtypeStruct((s, d), BF16),
                   jax.ShapeDtypeStruct((8, d), F32)],
        compiler_params=_params("arbitrary"),
    )(*args)


def _final_loss_bwd(x3, target, w, name, tm=512):
    s, d = x3.shape
    tm = min(tm, s)

    def body(x_ref, t_ref, w_ref, dx_ref, dxb_ref, gw_ref, loss_ref):
        @pl.when(pl.program_id(0) == 0)
        def _():
            gw_ref[...] = jnp.zeros_like(gw_ref)
            loss_ref[...] = jnp.zeros_like(loss_ref)

        xv = x_ref[...]
        r = lax.rsqrt(jnp.mean(xv * xv, axis=-1, keepdims=True) + EPS)
        xhat = xv * r
        y = xhat * w_ref[...]
        err = y - t_ref[...]
        part = 0.5 * jnp.mean(err * err, axis=-1, keepdims=True)
        tot = jnp.sum(part, axis=0, keepdims=True)
        rr = lax.broadcasted_iota(jnp.int32, loss_ref.shape, 0)
        cc = lax.broadcasted_iota(jnp.int32, loss_ref.shape, 1)
        loss_ref[...] += jnp.where((rr == 0) & (cc == 0), tot, 0.0)
        dy = err * (1.0 / d)
        gw_ref[...] += _rows8(dy * xhat)
        dxh = dy * w_ref[...]
        dx = r * (dxh - xhat * jnp.mean(dxh * xhat, axis=-1, keepdims=True))
        dx_ref[...] = dx
        dxb_ref[...] = dx.astype(BF16)

    row = pl.BlockSpec((tm, d), lambda i: (i, 0))
    return pl.pallas_call(
        body, name=name, grid=(s // tm,),
        in_specs=[row, row, pl.BlockSpec((1, d), lambda i: (0, 0))],
        out_specs=[row, row, pl.BlockSpec((8, d), lambda i: (0, 0)), pl.BlockSpec((8, 128), lambda i: (0, 0))],
        out_shape=[jax.ShapeDtypeStruct((s, d), F32), jax.ShapeDtypeStruct((s, d), BF16),
                   jax.ShapeDtypeStruct((8, d), F32), jax.ShapeDtypeStruct((8, 128), F32)],
        compiler_params=_params("arbitrary"),
    )(x3, target, w)


MM_TILES = (1024, 1408, 512, 256, 128)
MM_K_TILES = (2816, 2048, 1792, 1408, 1024, 512, 256, 128)
MM_VMEM_LIMIT = 56 * 1024 * 1024
MM_VMEM_BUDGET = 46 * 1024 * 1024


def _pick_tile(dim):
    for t in MM_TILES:
        if dim % t == 0:
            return t
    return dim


def _matmul(a, b, mode, name, *, out_dtype=F32, residual=None, extra_bf16=False, tm=None, tn=None, tk=None,
            groups=None):
    if groups is not None and mode == "nt":
        _, m, gw = a.shape
        n, k2 = b.shape
        k, tk = groups * gw, gw
    elif groups is not None and mode == "tn":
        k, m = a.shape
        _, k2, gw = b.shape
        n, tn = groups * gw, gw
    elif mode == "nn":
        (m, k), (k2, n) = a.shape, b.shape
    elif mode == "nt":
        (m, k), (n, k2) = a.shape, b.shape
    else:
        (k, m), (k2, n) = a.shape, b.shape
    assert k == k2, (a.shape, b.shape, mode)
    auto_tm = tm is None
    tm = _pick_tile(m) if tm is None else min(tm, m)
    tn = _pick_tile(n) if tn is None else min(tn, n)
    out_elt = jnp.dtype(out_dtype).itemsize + (2 if extra_bf16 else 0) + (4 if residual is not None else 0)

    def vmem_bytes(t, rows=None):
        rows = tm if rows is None else rows
        return (2 * (rows * t * a.dtype.itemsize + t * tn * b.dtype.itemsize) + 2 * rows * tn * out_elt
                + rows * tn * 4)

    if auto_tm and tk is None and m % (2 * tm) == 0 and vmem_bytes(k, 2 * tm) <= MM_VMEM_BUDGET:
        tm = 2 * tm

    if tk is None:
        tk = next(t for t in MM_K_TILES if k % t == 0 and t <= k and vmem_bytes(t) <= MM_VMEM_BUDGET)
    assert m % tm == 0 and n % tn == 0 and k % tk == 0, (m, n, k, tm, tn, tk)
    nk = k // tk
    dims = {"nn": ((1,), (0,)), "nt": ((1,), (1,)), "tn": ((0,), (0,))}[mode]
    has_res = residual is not None

    def body(*refs):
        refs = list(refs)
        a_ref, b_ref = refs[0], refs[1]
        r_ref = refs[2] if has_res else None
        outs = refs[2 + has_res:]
        o_ref = outs[0]
        o2_ref = outs[1] if extra_bf16 else None
        def finish(r):
            if has_res:
                r = r_ref[...] + r
            o_ref[...] = r.astype(out_dtype)
            if extra_bf16:
                o2_ref[...] = r.astype(BF16)

        if nk == 1:
            finish(_dot(a_ref[...], b_ref[...], dims))
            return
        acc = outs[-1]
        kk = pl.program_id(2)

        @pl.when(kk == 0)
        def _():
            acc[...] = _dot(a_ref[...], b_ref[...], dims)

        if nk > 2:
            @pl.when((kk > 0) & (kk < nk - 1))
            def _():
                acc[...] += _dot(a_ref[...], b_ref[...], dims)

        @pl.when(kk == nk - 1)
        def _():
            finish(acc[...] + _dot(a_ref[...], b_ref[...], dims))

    if mode == "tn":
        a_spec = pl.BlockSpec((tk, tm), lambda i, j, kk: (kk, i))
    elif groups is not None:
        a_spec = pl.BlockSpec((None, tm, tk), lambda i, j, kk: (kk, i, 0))
    else:
        a_spec = pl.BlockSpec((tm, tk), lambda i, j, kk: (i, kk))
    if mode == "nt":
        b_spec = pl.BlockSpec((tn, tk), lambda i, j, kk: (j, kk))
    elif groups is not None:
        b_spec = pl.BlockSpec((None, tk, tn), lambda i, j, kk: (j, kk, 0))
    else:
        b_spec = pl.BlockSpec((tk, tn), lambda i, j, kk: (kk, j))
    o_spec = pl.BlockSpec((tm, tn), lambda i, j, kk: (i, j))
    in_specs = [a_spec, b_spec] + ([o_spec] if has_res else [])
    out_specs = [o_spec] + ([o_spec] if extra_bf16 else [])
    out_shape = [jax.ShapeDtypeStruct((m, n), out_dtype)] + ([jax.ShapeDtypeStruct((m, n), BF16)] if extra_bf16 else [])
    args = (a, b) + ((residual,) if has_res else ())
    res = pl.pallas_call(
        body, name=name, grid=(m // tm, n // tn, nk),
        in_specs=in_specs, out_specs=out_specs, out_shape=out_shape,
        scratch_shapes=[pltpu.VMEM((tm, tn) if nk > 1 else (8, 128), F32)],
        compiler_params=pltpu.CompilerParams(dimension_semantics=("parallel", "parallel", "arbitrary"),
                                             vmem_limit_bytes=MM_VMEM_LIMIT),
    )(*args)
    return res if extra_bf16 else res[0]


def _matmul_windows(h, w, windows, name, prev=None):
    s, k = h.shape
    n = w.shape[1]
    tn = n // N_CHIPS
    tm = min(s, MM_TILES[0])

    def body(win_ref, h_ref, w_ref, *rest):
        rest[-1][...] = _dot(h_ref[...], w_ref[...], ((1,), (0,))).astype(BF16)

    extra = [] if prev is None else [prev]
    return pl.pallas_call(
        body, name=name,
        grid_spec=pltpu.PrefetchScalarGridSpec(
            num_scalar_prefetch=1, grid=(s // tm, windows.shape[0]),
            in_specs=[pl.BlockSpec((tm, k), lambda i, j, win: (i, 0)),
                      pl.BlockSpec((k, tn), lambda i, j, win: (0, win[j]))]
            + [pl.BlockSpec(memory_space=pl.ANY)] * len(extra),
            out_specs=pl.BlockSpec((tm, tn), lambda i, j, win: (i, win[j]))),
        out_shape=jax.ShapeDtypeStruct((s, n), BF16),
        input_output_aliases={3: 0} if extra else {},
        compiler_params=_params("parallel", "arbitrary"),
    )(windows, h, w, *extra)


def _hgrn_gates(qp, fp, lb):
    sig = _sigmoid(fp)
    f = lb + (1.0 - lb) * sig
    logf = jnp.log(f)
    k = 1.0 - f
    sq = _sigmoid(qp)
    q = qp * sq
    return sig, f, logf, k, sq, q


def _hgrn_fwd(proj, lb0, lb1, norm_w, name, tb=1024):
    s = proj.shape[0]
    tb = min(tb, s)
    nb, ncb = s // tb, tb // CHUNK

    def body(q_ref, f_ref, i_ref, g_ref, a0_ref, a1_ref, nw_ref, o_ref, og_ref, st_ref, state):
        @pl.when(pl.program_id(1) == 0)
        def _():
            state[...] = jnp.zeros_like(state)

        lb2 = _sigmoid(a0_ref[...] - a1_ref[...])
        row = lax.broadcasted_iota(jnp.int32, (CHUNK, CHUNK), 0)
        col = lax.broadcasted_iota(jnp.int32, (CHUNK, CHUNK), 1)
        tril = row >= col
        ones_l = tril.astype(BF16)
        nw = nw_ref[...]

        def chunk(c, carry):
            rows = pl.ds(pl.multiple_of(c * CHUNK, CHUNK), CHUNK)
            for hh in range(HGRN_HEADS_PER_STEP):
                cols = slice(hh * HEAD, (hh + 1) * HEAD)
                v = i_ref[rows, cols].astype(F32)
                _, _, logf, k, _, q = _hgrn_gates(q_ref[rows, cols].astype(F32), f_ref[rows, cols].astype(F32),
                                                  lb2[:, cols])
                b = _exact_ones_dot(ones_l, logf)
                bl = jnp.sum(logf, axis=0, keepdims=True)
                bm = 0.5 * bl
                st = state[hh]
                st_ref[hh, c] = st
                qt = q * jnp.exp(b - bm)
                kt = k * jnp.exp(bm - b)
                a = jnp.where(tril, _dot_nt(qt, kt), 0.0)
                o = _dot_nt(q * jnp.exp(b), st) + _dot_nn(a, v)
                state[hh] = st * jnp.exp(bl) + _dot_tn(v, k * jnp.exp(bl - b))
                o_ref[rows, cols] = o
                on = (o * lax.rsqrt(jnp.mean(o * o, axis=-1, keepdims=True) + EPS)) * nw
                gv = g_ref[rows, cols].astype(F32)
                og_ref[rows, cols] = (on * (gv * _sigmoid(gv))).astype(BF16)
            return carry

        lax.fori_loop(0, ncb, chunk, 0, unroll=HGRN_UNROLL)

    hp, wd = HGRN_HEADS_PER_STEP, HGRN_HEADS_PER_STEP * HEAD
    ngrp = N_HEADS // hp

    def colblk(group):
        return pl.BlockSpec((tb, wd), lambda h, j: (j, group * ngrp + h))

    vec = pl.BlockSpec((1, wd), lambda h, j: (0, h))
    out_blk = pl.BlockSpec((tb, wd), lambda h, j: (j, h))
    return pl.pallas_call(
        body, name=name, grid=(ngrp, nb),
        in_specs=[colblk(0), colblk(1), colblk(2), colblk(3), vec, vec, pl.BlockSpec((1, HEAD), lambda h, j: (0, 0))],
        out_specs=[out_blk, out_blk, pl.BlockSpec((hp, ncb, HEAD, HEAD), lambda h, j: (h, j, 0, 0))],
        out_shape=[jax.ShapeDtypeStruct((s, HGRN_W), F32), jax.ShapeDtypeStruct((s, 2 * HGRN_W), BF16),
                   jax.ShapeDtypeStruct((N_HEADS, s // CHUNK, HEAD, HEAD), F32)],
        scratch_shapes=[pltpu.VMEM((hp, HEAD, HEAD), F32)],
        compiler_params=_params("parallel", "arbitrary"),
    )(proj, proj, proj, proj, lb0, lb1, norm_w)


def _hgrn_bwd(proj, lb0, lb1, norm_w, o, states, dmix, name, tb=1024):
    s = proj.shape[0]
    tb = min(tb, s)
    nb, ncb = s // tb, tb // CHUNK

    def body(q_ref, f_ref, i_ref, g_ref, a0_ref, a1_ref, nw_ref, o_ref, st_ref, dm_ref,
             dp_ref, glb_ref, gnw_ref, dstate):
        h = pl.program_id(0)

        @pl.when(pl.program_id(1) == 0)
        def _():
            dstate[...] = jnp.zeros_like(dstate)
            glb_ref[...] = jnp.zeros_like(glb_ref)

        @pl.when((pl.program_id(1) == 0) & (h == 0))
        def _():
            gnw_ref[...] = jnp.zeros_like(gnw_ref)

        lb2 = _sigmoid(a0_ref[...] - a1_ref[...])
        row = lax.broadcasted_iota(jnp.int32, (CHUNK, CHUNK), 0)
        col = lax.broadcasted_iota(jnp.int32, (CHUNK, CHUNK), 1)
        tril = row >= col
        ones_l = tril.astype(BF16)
        ones_u = (row <= col).astype(BF16)
        nw = nw_ref[...]

        def chunk(cc, carry):
            c = ncb - 1 - cc
            rows = pl.ds(pl.multiple_of(c * CHUNK, CHUNK), CHUNK)
            for hh in range(HGRN_HEADS_PER_STEP):
                cols = slice(hh * HEAD, (hh + 1) * HEAD)
                lb = lb2[:, cols]
                qp = q_ref[rows, cols].astype(F32)
                v = i_ref[rows, cols].astype(F32)
                sig, f, logf, k, sq, q = _hgrn_gates(qp, f_ref[rows, cols].astype(F32), lb)
                gv = g_ref[rows, cols].astype(F32)
                sg = _sigmoid(gv)
                silu_g = gv * sg
                dog = dm_ref[rows, cols].astype(F32)
                ov = o_ref[rows, cols]
                r = lax.rsqrt(jnp.mean(ov * ov, axis=-1, keepdims=True) + EPS)
                ohat = ov * r
                on = ohat * nw
                dp_ref[3, rows, cols] = (dog * on * (sg * (1.0 + gv * (1.0 - sg)))).astype(BF16)
                don = dog * silu_g
                gnw_ref[...] += _rows8(don * ohat)
                doh = don * nw
                do = r * (doh - ohat * jnp.mean(doh * ohat, axis=-1, keepdims=True))
                b = _exact_ones_dot(ones_l, logf)
                bl = jnp.sum(logf, axis=0, keepdims=True)
                bm = 0.5 * bl
                e_q = jnp.exp(b - bm)
                e_k = jnp.exp(bm - b)
                e_b = jnp.exp(b)
                e_l = jnp.exp(bl - b)
                qt, kt, qb, kb = q * e_q, k * e_k, q * e_b, k * e_l
                st0 = st_ref[hh, c]
                dst = dstate[hh]
                a = jnp.where(tril, _dot_nt(qt, kt), 0.0)
                da = jnp.where(tril, _dot_nt(do, v), 0.0)
                dq = _hdot_nn(da, kt) * e_q + _hdot_nn(do, st0) * e_b
                dkb = _hdot_nn(v, dst) * e_l
                dk = _hdot_tn(da, qt) * e_k + dkb
                dv = _dot_tn(a, do) + _dot_nt(kb, dst)
                e_bl = jnp.exp(bl)
                dstate[hh] = dst * e_bl + _dot_tn(do, qb)
                db = q * dq - k * dk
                db_last = jnp.sum(k * dkb, axis=0, keepdims=True) + e_bl * jnp.sum(st0 * dst, axis=0, keepdims=True)
                dlogf = _exact_ones_dot(ones_u, db) + db_last
                dfg = dlogf / f - dk
                dp_ref[1, rows, cols] = (dfg * (1.0 - lb) * (sig * (1.0 - sig))).astype(BF16)
                glb_ref[:, cols] += _rows8(dfg * (1.0 - sig)) * (lb * (1.0 - lb))
                dp_ref[0, rows, cols] = (dq * (sq * (1.0 + qp * (1.0 - sq)))).astype(BF16)
                dp_ref[2, rows, cols] = dv.astype(BF16)
            return carry

        lax.fori_loop(0, ncb, chunk, 0, unroll=HGRN_UNROLL)

    hp, wd = HGRN_HEADS_PER_STEP, HGRN_HEADS_PER_STEP * HEAD
    ngrp = N_HEADS // hp

    def colblk(group):
        return pl.BlockSpec((tb, wd), lambda h, j: (nb - 1 - j, group * ngrp + h))

    vec = pl.BlockSpec((1, wd), lambda h, j: (0, h))
    blk = pl.BlockSpec((tb, wd), lambda h, j: (nb - 1 - j, h))
    return pl.pallas_call(
        body, name=name, grid=(ngrp, nb),
        in_specs=[colblk(0), colblk(1), colblk(2), colblk(3), vec, vec, pl.BlockSpec((1, HEAD), lambda h, j: (0, 0)),
                  blk, pl.BlockSpec((hp, ncb, HEAD, HEAD), lambda h, j: (h, nb - 1 - j, 0, 0)), blk],
        out_specs=[pl.BlockSpec((4, tb, wd), lambda h, j: (0, nb - 1 - j, h)),
                   pl.BlockSpec((8, wd), lambda h, j: (0, h)), pl.BlockSpec((8, HEAD), lambda h, j: (0, 0))],
        out_shape=[jax.ShapeDtypeStruct((DPROJ_GROUPS, s, HGRN_W), BF16),
                   jax.ShapeDtypeStruct((8, HGRN_W), F32), jax.ShapeDtypeStruct((8, HEAD), F32)],
        scratch_shapes=[pltpu.VMEM((hp, HEAD, HEAD), F32)],
        compiler_params=_params("arbitrary", "arbitrary"),
    )(proj, proj, proj, proj, lb0, lb1, norm_w, o, states, dmix)


HALO_BLK = 16


def _f32(ref):
    return ref[...].astype(F32)


def _halo_prev(ref):
    return ref[...].astype(F32)[HALO_BLK - HALO:]


def _halo_next(ref):
    return ref[...].astype(F32)[:HALO]


def _conv3(x0, x1, x2, w_ref):
    y = x0 * w_ref[0:1, :]
    y = y + x1 * w_ref[1:2, :]
    return y + x2 * w_ref[2:3, :]


def _sconv_fwd(proj, w8, mix, name, tb=512):
    s = proj.shape[0]
    tb = min(tb, s)
    hb = tb // HALO_BLK

    def body(cb_ref, cc_ref, ch_ref, cch_ref, chh_ref, w_ref, mix_ref, y_ref):
        first = pl.program_id(0) == 0
        u = _f32(cc_ref) * _f32(ch_ref)
        uh = jnp.where(first, 0.0, _halo_prev(cch_ref) * _halo_prev(chh_ref))
        conv = _conv3(_shift_down(u, uh, 2), _shift_down(u, uh, 1), u, w_ref)
        y_ref[...] = (_f32(cb_ref) * conv).astype(BF16)

    def blk(g):
        return pl.BlockSpec((tb, HGRN_W), lambda j: (j, g))

    def halo(g):
        return pl.BlockSpec((HALO_BLK, HGRN_W), lambda j: (jnp.maximum(j * hb - 1, 0), g))

    return pl.pallas_call(
        body, name=name, grid=(s // tb,),
        in_specs=[blk(4), blk(5), blk(6), halo(5), halo(6), pl.BlockSpec((HALO, HGRN_W), lambda j: (0, 0)),
                  pl.BlockSpec(memory_space=pl.ANY)],
        out_specs=pl.BlockSpec((tb, HGRN_W), lambda j: (j, 1)),
        out_shape=jax.ShapeDtypeStruct(mix.shape, BF16),
        input_output_aliases={6: 0},
        compiler_params=_params("parallel"),
    )(proj, proj, proj, proj, proj, w8, mix)


def _sconv_bwd(proj, w8, dmix, dproj, name, tb=512):
    s = proj.shape[0]
    tb = min(tb, s)
    hb = tb // HALO_BLK
    nb = s // tb
    last_h = s // HALO_BLK - 1

    def body(cb_ref, cc_ref, ch_ref, cch_ref, chh_ref, cbn_ref, dy_ref, dyn_ref, w_ref, dproj_ref,
             dp_ref, gw_ref):
        j = pl.program_id(0)

        @pl.when(j == 0)
        def _():
            gw_ref[...] = jnp.zeros_like(gw_ref)

        cc, ch, cb = _f32(cc_ref), _f32(ch_ref), _f32(cb_ref)
        u = cc * ch
        uh = jnp.where(j == 0, 0.0, _halo_prev(cch_ref) * _halo_prev(chh_ref))
        u2, u1 = _shift_down(u, uh, 2), _shift_down(u, uh, 1)
        conv = _conv3(u2, u1, u, w_ref)
        dy = _f32(dy_ref)
        dp_ref[0] = (dy * conv).astype(BF16)
        dc = dy * cb
        dcn = jnp.where(j == nb - 1, 0.0, _halo_next(dyn_ref) * _halo_next(cbn_ref))
        gw_ref[0:8, :] += _rows8(dc * u2)
        gw_ref[8:16, :] += _rows8(dc * u1)
        gw_ref[16:24, :] += _rows8(dc * u)
        du = dc * w_ref[2:3, :] + _shift_up(dc, dcn, 1) * w_ref[1:2, :] + _shift_up(dc, dcn, 2) * w_ref[0:1, :]
        dp_ref[1] = (du * ch).astype(BF16)
        dp_ref[2] = (du * cc).astype(BF16)
        dp_ref[3] = jnp.zeros(dp_ref.shape[1:], BF16)

    def blk(g):
        return pl.BlockSpec((tb, HGRN_W), lambda j: (j, g))

    def halo_prev(g):
        return pl.BlockSpec((HALO_BLK, HGRN_W), lambda j: (jnp.maximum(j * hb - 1, 0), g))

    def halo_next(g):
        return pl.BlockSpec((HALO_BLK, HGRN_W), lambda j: (jnp.minimum((j + 1) * hb, last_h), g))

    return pl.pallas_call(
        body, name=name, grid=(nb,),
        in_specs=[blk(4), blk(5), blk(6), halo_prev(5), halo_prev(6), halo_next(4), blk(1), halo_next(1),
                  pl.BlockSpec((HALO, HGRN_W), lambda j: (0, 0)), pl.BlockSpec(memory_space=pl.ANY)],
        out_specs=[pl.BlockSpec((4, tb, HGRN_W), lambda j: (1, j, 0)), pl.BlockSpec((24, HGRN_W), lambda j: (0, 0))],
        out_shape=[jax.ShapeDtypeStruct(dproj.shape, BF16), jax.ShapeDtypeStruct((24, HGRN_W), F32)],
        input_output_aliases={9: 0},
        compiler_params=_params("arbitrary"),
    )(proj, proj, proj, proj, proj, proj, dmix, dmix, w8, dproj)


def _attn_fwd(q, kk, vv, name, tb=1024):
    s, d = q.shape
    m = kk.shape[0]
    tb = min(tb, s)
    scale = MEM_HEAD_DIM ** -0.5

    def body(q_ref, k_ref, v_ref, o_ref):
        for hh in range(MEM_HEADS):
            cols = slice(hh * MEM_HEAD_DIM, (hh + 1) * MEM_HEAD_DIM)
            sc = _dot_nt(q_ref[:, cols], k_ref[:, cols]) * scale
            sc = sc - jnp.max(sc, axis=-1, keepdims=True)
            e = jnp.exp(sc)
            p = e / jnp.sum(e, axis=-1, keepdims=True)
            o_ref[:, cols] = _dot_nn(p, v_ref[:, cols]).astype(BF16)

    full = pl.BlockSpec((m, d), lambda i: (0, 0))
    return pl.pallas_call(
        body, name=name, grid=(s // tb,),
        in_specs=[pl.BlockSpec((tb, d), lambda i: (i, 0)), full, full],
        out_specs=pl.BlockSpec((tb, d), lambda i: (i, 0)),
        out_shape=jax.ShapeDtypeStruct((s, d), BF16),
        compiler_params=pltpu.CompilerParams(dimension_semantics=("parallel",), vmem_limit_bytes=MM_VMEM_LIMIT),
    )(q, kk, vv)


def _attn_bwd(q, kk, vv, datt, name, tb=1024):
    s, d = q.shape
    m = kk.shape[0]
    tb = min(tb, s)
    scale = MEM_HEAD_DIM ** -0.5

    def body(q_ref, k_ref, v_ref, do_ref, dq_ref, dk_ref, dv_ref):
        @pl.when(pl.program_id(0) == 0)
        def _():
            dk_ref[...] = jnp.zeros_like(dk_ref)
            dv_ref[...] = jnp.zeros_like(dv_ref)

        for hh in range(MEM_HEADS):
            cols = slice(hh * MEM_HEAD_DIM, (hh + 1) * MEM_HEAD_DIM)
            qh, kh, vh, doh = q_ref[:, cols], k_ref[:, cols], v_ref[:, cols], do_ref[:, cols]
            sc = _dot_nt(qh, kh) * scale
            sc = sc - jnp.max(sc, axis=-1, keepdims=True)
            e = jnp.exp(sc)
            p = e / jnp.sum(e, axis=-1, keepdims=True)
            dp = _dot_nt(doh, vh)
            ds = p * (dp - jnp.sum(dp * p, axis=-1, keepdims=True)) * scale
            dq_ref[:, cols] = _dot_nn(ds, kh).astype(BF16)
            dk_ref[:, cols] += _dot_tn(ds, qh)
            dv_ref[:, cols] += _dot_tn(p, doh)

    full = pl.BlockSpec((m, d), lambda i: (0, 0))
    row = pl.BlockSpec((tb, d), lambda i: (i, 0))
    return pl.pallas_call(
        body, name=name, grid=(s // tb,),
        in_specs=[row, full, full, row],
        out_specs=[row, full, full],
        out_shape=[jax.ShapeDtypeStruct((s, d), BF16), jax.ShapeDtypeStruct((m, d), F32),
                   jax.ShapeDtypeStruct((m, d), F32)],
        compiler_params=pltpu.CompilerParams(dimension_semantics=("arbitrary",), vmem_limit_bytes=MM_VMEM_LIMIT),
    )(q, kk, vv, datt)


def _ffn_fwd(g, u, w8, bias, name, tb=512, tc=1408):
    s, f = g.shape
    tb = min(tb, s)
    tc = tc if f % tc == 0 else 512
    hb = tb // HALO_BLK

    def body(g_ref, gh_ref, u_ref, w_ref, b_ref, z_ref, a_ref):
        gv = _f32(g_ref)
        gh = jnp.where(pl.program_id(1) == 0, 0.0, _halo_prev(gh_ref))
        a = _conv3(_shift_down(gv, gh, 2), _shift_down(gv, gh, 1), gv, w_ref) + b_ref[...]
        a_ref[...] = a.astype(BF16)
        z_ref[...] = ((a * _sigmoid(a)) * _f32(u_ref)).astype(BF16)

    blk = pl.BlockSpec((tb, tc), lambda c, j: (j, c))
    return pl.pallas_call(
        body, name=name, grid=(f // tc, s // tb),
        in_specs=[blk, pl.BlockSpec((HALO_BLK, tc), lambda c, j: (jnp.maximum(j * hb - 1, 0), c)), blk,
                  pl.BlockSpec((HALO, tc), lambda c, j: (0, c)), pl.BlockSpec((1, tc), lambda c, j: (0, c))],
        out_specs=[blk, blk],
        out_shape=[jax.ShapeDtypeStruct((s, f), BF16), jax.ShapeDtypeStruct((s, f), BF16)],
        compiler_params=pltpu.CompilerParams(dimension_semantics=("parallel", "parallel"),
                                             vmem_limit_bytes=MM_VMEM_LIMIT),
    )(g, g, u, w8, bias)


def _ffn_bwd(a, g, u, dz, w8, name, tb=512, tc=1408):
    s, f = g.shape
    tb = min(tb, s)
    tc = tc if f % tc == 0 else 512
    nb = s // tb

    def body(a_ref, g_ref, u_ref, dz_ref, w_ref, dg_ref, du_ref, gb_ref, gw_ref, da_next):
        jj = pl.program_id(1)

        @pl.when(jj == 0)
        def _():
            gb_ref[...] = jnp.zeros_like(gb_ref)
            gw_ref[...] = jnp.zeros_like(gw_ref)
            da_next[...] = jnp.zeros_like(da_next)

        a = _f32(a_ref)
        sa = _sigmoid(a)
        dz = _f32(dz_ref)
        du_ref[...] = (dz * (a * sa)).astype(BF16)
        da = dz * _f32(u_ref) * (sa * (1.0 + a * (1.0 - sa)))
        gb_ref[...] += _rows8(da)
        dan = da_next[...]
        da1, da2 = _shift_up(da, dan, 1), _shift_up(da, dan, 2)
        gv = _f32(g_ref)
        gw_ref[0:8, :] += _rows8(da2 * gv)
        gw_ref[8:16, :] += _rows8(da1 * gv)
        gw_ref[16:24, :] += _rows8(da * gv)
        dg_ref[...] = (da * w_ref[2:3, :] + da1 * w_ref[1:2, :] + da2 * w_ref[0:1, :]).astype(BF16)
        da_next[...] = da[:HALO]

    blk = pl.BlockSpec((tb, tc), lambda c, jj: (nb - 1 - jj, c))
    return pl.pallas_call(
        body, name=name, grid=(f // tc, nb),
        in_specs=[blk, blk, blk, blk, pl.BlockSpec((HALO, tc), lambda c, jj: (0, c))],
        out_specs=[blk, blk, pl.BlockSpec((8, tc), lambda c, jj: (0, c)), pl.BlockSpec((24, tc), lambda c, jj: (0, c))],
        out_shape=[jax.ShapeDtypeStruct((s, f), BF16), jax.ShapeDtypeStruct((s, f), BF16),
                   jax.ShapeDtypeStruct((8, f), F32), jax.ShapeDtypeStruct((24, f), F32)],
        scratch_shapes=[pltpu.VMEM((HALO, tc), F32)],
        compiler_params=pltpu.CompilerParams(dimension_semantics=("parallel", "arbitrary"),
                                             vmem_limit_bytes=MM_VMEM_LIMIT),
    )(a, g, u, dz, w8)


def _window(ref, axis, slot, size):
    start = pl.multiple_of(slot * size, size)
    if axis == 0:
        return ref.at[pl.ds(start, size), :]
    return ref.at[:, pl.ds(start, size)]


def _chip_peers():
    x, y, c = lax.axis_index("x"), lax.axis_index("y"), lax.axis_index("c")
    peers = [(1 - x, y, c), (x, 1 - y, c), (1 - x, 1 - y, c)]
    slots = [2 * (1 - x) + y, 2 * x + (1 - y), 2 * (1 - x) + (1 - y)]
    return 2 * x + y, peers, slots


HBM_SPEC = pl.BlockSpec(memory_space=pltpu.HBM)
SEM_SPEC = pl.BlockSpec(memory_space=pltpu.SEMAPHORE)
EFFECT = pltpu.SideEffectType.DATAFLOW_SIDE_EFFECTING


def _hbm(a):
    return pltpu.with_memory_space_constraint(a, pltpu.HBM)


def _cast_into_full(x, axis, slot_arr, dtype, name, after=None):
    r, c = x.shape
    tr = _row_tile(r, 512)
    nb = r // tr
    full = (r * N_CHIPS, c) if axis == 0 else (r, c * N_CHIPS)

    def body(slot_ref, x_ref, *rest):
        rest[-1][...] = x_ref[...].astype(dtype)

    if axis == 0:
        out_map = lambda i, s: (s[0] * nb + i, 0)
    else:
        out_map = lambda i, s: (i, s[0])
    extra = [] if after is None else [after]
    return pl.pallas_call(
        body, name=name,
        grid_spec=pltpu.PrefetchScalarGridSpec(
            num_scalar_prefetch=1, grid=(nb,),
            in_specs=[pl.BlockSpec((tr, c), lambda i, s: (i, 0))] + [pl.BlockSpec(memory_space=pl.ANY)] * len(extra),
            out_specs=pl.BlockSpec((tr, c), out_map)),
        out_shape=jax.ShapeDtypeStruct(full, dtype),
        compiler_params=_params("parallel"),
    )(slot_arr, x, *extra)


def _piece(ref, axis, slot, half):
    size = ref.shape[axis] // N_CHIPS
    if half is None:
        return _window(ref, axis, slot, size)
    if axis == 0:
        h = size // 2
        return ref.at[pl.ds(pl.multiple_of(slot * size + half * h, h), h), :]
    h = ref.shape[0] // 2
    return ref.at[pl.ds(pl.multiple_of(half * h, h), h), pl.ds(pl.multiple_of(slot * size, size), size)]


def _gather_start(fulls, axes, split, groups, name):
    n, ng = len(fulls), len(groups)

    def body(*refs):
        outs = refs[n:]
        sems = outs[:2 * ng]
        thru = outs[2 * ng:2 * ng + n]
        token = outs[-1]
        slot, peers, _ = _chip_peers()
        c = lax.axis_index("c")
        for g, members in enumerate(groups):
            for i, t in enumerate(members):
                mine = _piece(thru[t], axes[t], slot, c if split[t] else None)
                for k in range(3):
                    pltpu.make_async_remote_copy(
                        src_ref=mine, dst_ref=mine, send_sem=sems[2 * g].at[3 * i + k],
                        recv_sem=sems[2 * g + 1].at[3 * i + k], device_id=peers[k], device_id_type=MESH).start()
        token[...] = jnp.zeros_like(token)

    sem_shapes = []
    for members in groups:
        sem_shapes += [pltpu.SemaphoreType.DMA((3 * len(members),))] * 2
    res = pl.pallas_call(
        body, name=name,
        in_specs=[HBM_SPEC] * n,
        out_specs=[SEM_SPEC] * (2 * ng) + [HBM_SPEC] * n + [pl.BlockSpec(memory_space=pltpu.VMEM)],
        out_shape=sem_shapes + [pltpu.HBM(f.shape, f.dtype) for f in fulls] + [jax.ShapeDtypeStruct((8, 128), F32)],
        input_output_aliases={t: 2 * ng + t for t in range(n)},
        compiler_params=pltpu.CompilerParams(has_side_effects=EFFECT),
    )(*[_hbm(f) for f in fulls])
    sems = [(res[2 * g], res[2 * g + 1]) for g in range(ng)]
    return sems, list(res[2 * ng:2 * ng + n]), res[-1]


def _gather_relay(fulls, axes, split, sems, after, name):
    n = len(fulls)
    nsplit = sum(split)

    def body(*refs):
        send_sems, recv_sems = refs[n], refs[n + 1]
        outs = refs[n + 3:]
        d_send, d_recv = outs[0], outs[1]
        thru = outs[2:2 + n]
        token = outs[-1]
        slot, peers, slots = _chip_peers()
        c = lax.axis_index("c")
        sibling = (lax.axis_index("x"), lax.axis_index("y"), 1 - c)
        for t in range(n):
            half = c if split[t] else None
            for k in range(3):
                cp = pltpu.make_async_remote_copy(
                    src_ref=_piece(thru[t], axes[t], slot, half), dst_ref=_piece(thru[t], axes[t], slots[k], half),
                    send_sem=send_sems.at[3 * t + k], recv_sem=recv_sems.at[3 * t + k],
                    device_id=peers[k], device_id_type=MESH)
                cp.wait_send()
                cp.wait_recv()
        i = 0
        for t in range(n):
            if not split[t]:
                continue
            for k in range(3):
                got = _piece(thru[t], axes[t], slots[k], c)
                pltpu.make_async_remote_copy(
                    src_ref=got, dst_ref=got, send_sem=d_send.at[3 * i + k], recv_sem=d_recv.at[3 * i + k],
                    device_id=sibling, device_id_type=MESH).start()
            i += 1
        token[...] = jnp.zeros_like(token)

    res = pl.pallas_call(
        body, name=name,
        in_specs=[HBM_SPEC] * n + [SEM_SPEC, SEM_SPEC, pl.BlockSpec(memory_space=pl.ANY)],
        out_specs=[SEM_SPEC, SEM_SPEC] + [HBM_SPEC] * n + [pl.BlockSpec(memory_space=pltpu.VMEM)],
        out_shape=[pltpu.SemaphoreType.DMA((3 * nsplit,)), pltpu.SemaphoreType.DMA((3 * nsplit,))]
        + [pltpu.HBM(f.shape, f.dtype) for f in fulls] + [jax.ShapeDtypeStruct((8, 128), F32)],
        input_output_aliases={t: 2 + t for t in range(n)},
        compiler_params=pltpu.CompilerParams(has_side_effects=EFFECT),
    )(*fulls, sems[0], sems[1], after)
    return (res[0], res[1]), list(res[2:2 + n]), res[-1]


def _gather_finish(fulls, axes, split, sems, after, name):
    n = len(fulls)

    def body(*refs):
        d_send, d_recv = refs[n], refs[n + 1]
        thru = refs[n + 3:]
        _, _, slots = _chip_peers()
        c = lax.axis_index("c")
        sibling = (lax.axis_index("x"), lax.axis_index("y"), 1 - c)
        i = 0
        for t in range(n):
            if not split[t]:
                continue
            for k in range(3):
                cp = pltpu.make_async_remote_copy(
                    src_ref=_piece(thru[t], axes[t], slots[k], c), dst_ref=_piece(thru[t], axes[t], slots[k], 1 - c),
                    send_sem=d_send.at[3 * i + k], recv_sem=d_recv.at[3 * i + k],
                    device_id=sibling, device_id_type=MESH)
                cp.wait_send()
                cp.wait_recv()
            i += 1

    return pl.pallas_call(
        body, name=name,
        in_specs=[HBM_SPEC] * n + [SEM_SPEC, SEM_SPEC, pl.BlockSpec(memory_space=pl.ANY)],
        out_specs=[HBM_SPEC] * n,
        out_shape=[pltpu.HBM(f.shape, f.dtype) for f in fulls],
        input_output_aliases={t: t for t in range(n)},
        compiler_params=pltpu.CompilerParams(has_side_effects=EFFECT),
    )(*fulls, sems[0], sems[1], after)


def _scatter_start(grads_bf16, axes, name):
    n = len(grads_bf16)

    def shard_shape(g, ax):
        return (g.shape[0] // N_CHIPS, g.shape[1]) if ax == 0 else (g.shape[0], g.shape[1] // N_CHIPS)

    shapes = [shard_shape(g, ax) for g, ax in zip(grads_bf16, axes)]

    def body(*refs):
        outs = refs[2 * n:]
        send_sems, recv_sems = outs[0], outs[1]
        gb, land = outs[2:2 + n], outs[2 + n:2 + 2 * n]
        token = outs[-1]
        _, peers, slots = _chip_peers()
        for t in range(n):
            size = shapes[t][axes[t]]
            for k in range(3):
                pltpu.make_async_remote_copy(
                    src_ref=_window(gb[t], axes[t], slots[k], size), dst_ref=land[t].at[k],
                    send_sem=send_sems.at[3 * t + k], recv_sem=recv_sems.at[3 * t + k],
                    device_id=peers[k], device_id_type=MESH).start()
        token[...] = jnp.zeros_like(token)

    lands = [_hbm(lax.empty((3,) + sh, BF16)) for sh in shapes]
    res = pl.pallas_call(
        body, name=name,
        in_specs=[HBM_SPEC] * (2 * n),
        out_specs=[SEM_SPEC, SEM_SPEC] + [HBM_SPEC] * (2 * n) + [pl.BlockSpec(memory_space=pltpu.VMEM)],
        out_shape=[pltpu.SemaphoreType.DMA((3 * n,)), pltpu.SemaphoreType.DMA((3 * n,))]
        + [pltpu.HBM(g.shape, g.dtype) for g in grads_bf16] + [pltpu.HBM((3,) + sh, BF16) for sh in shapes]
        + [jax.ShapeDtypeStruct((8, 128), F32)],
        input_output_aliases={t: 2 + t for t in range(2 * n)},
        compiler_params=pltpu.CompilerParams(has_side_effects=EFFECT),
    )(*[_hbm(g) for g in grads_bf16], *lands)
    return (res[0], res[1]), list(res[2:2 + n]), list(res[2 + n:2 + 2 * n]), res[-1]


def _scatter_wait(grads_thru, lands_thru, axes, sems, after, name):
    n = len(grads_thru)

    def body(*refs):
        send_sems, recv_sems = refs[2 * n], refs[2 * n + 1]
        outs = refs[2 * n + 3:]
        gb, land = outs[:n], outs[n:]
        _, peers, slots = _chip_peers()
        for t in range(n):
            size = land[t].shape[1 + axes[t]]
            for k in range(3):
                cp = pltpu.make_async_remote_copy(
                    src_ref=_window(gb[t], axes[t], slots[k], size), dst_ref=land[t].at[k],
                    send_sem=send_sems.at[3 * t + k], recv_sem=recv_sems.at[3 * t + k],
                    device_id=peers[k], device_id_type=MESH)
                cp.wait_send()
                cp.wait_recv()

    res = pl.pallas_call(
        body, name=name,
        in_specs=[HBM_SPEC] * (2 * n) + [SEM_SPEC, SEM_SPEC, pl.BlockSpec(memory_space=pl.ANY)],
        out_specs=[HBM_SPEC] * (2 * n),
        out_shape=[pltpu.HBM(g.shape, g.dtype) for g in grads_thru] + [pltpu.HBM(l.shape, l.dtype) for l in lands_thru],
        input_output_aliases={t: t for t in range(2 * n)},
        compiler_params=pltpu.CompilerParams(has_side_effects=EFFECT),
    )(*grads_thru, *lands_thru, sems[0], sems[1], after)
    return list(res[n:])


def _sibling_start(arrs, name):
    n = len(arrs)

    def body(*refs):
        outs = refs[2 * n:]
        send_sems, recv_sems = outs[0], outs[1]
        src, land = outs[2:2 + n], outs[2 + n:2 + 2 * n]
        token = outs[-1]
        sibling = (lax.axis_index("x"), lax.axis_index("y"), 1 - lax.axis_index("c"))
        for t in range(n):
            pltpu.make_async_remote_copy(
                src_ref=src[t], dst_ref=land[t], send_sem=send_sems.at[t], recv_sem=recv_sems.at[t],
                device_id=sibling, device_id_type=MESH).start()
        token[...] = jnp.zeros_like(token)

    lands = [_hbm(lax.empty(a.shape, a.dtype)) for a in arrs]
    res = pl.pallas_call(
        body, name=name,
        in_specs=[HBM_SPEC] * (2 * n),
        out_specs=[SEM_SPEC, SEM_SPEC] + [HBM_SPEC] * (2 * n) + [pl.BlockSpec(memory_space=pltpu.VMEM)],
        out_shape=[pltpu.SemaphoreType.DMA((n,)), pltpu.SemaphoreType.DMA((n,))]
        + [pltpu.HBM(a.shape, a.dtype) for a in arrs] * 2 + [jax.ShapeDtypeStruct((8, 128), F32)],
        input_output_aliases={t: 2 + t for t in range(2 * n)},
        compiler_params=pltpu.CompilerParams(has_side_effects=EFFECT),
    )(*[_hbm(a) for a in arrs], *lands)
    return (res[0], res[1]), list(res[2:2 + n]), list(res[2 + n:2 + 2 * n]), res[-1]


def _sibling_wait(src_thru, lands_thru, sems, after, name):
    n = len(src_thru)

    def body(*refs):
        send_sems, recv_sems = refs[2 * n], refs[2 * n + 1]
        outs = refs[2 * n + 3:]
        src, land = outs[:n], outs[n:]
        sibling = (lax.axis_index("x"), lax.axis_index("y"), 1 - lax.axis_index("c"))
        for t in range(n):
            cp = pltpu.make_async_remote_copy(
                src_ref=src[t], dst_ref=land[t], send_sem=send_sems.at[t], recv_sem=recv_sems.at[t],
                device_id=sibling, device_id_type=MESH)
            cp.wait_send()
            cp.wait_recv()

    res = pl.pallas_call(
        body, name=name,
        in_specs=[HBM_SPEC] * (2 * n) + [SEM_SPEC, SEM_SPEC, pl.BlockSpec(memory_space=pl.ANY)],
        out_specs=[HBM_SPEC] * (2 * n),
        out_shape=[pltpu.HBM(a.shape, a.dtype) for a in src_thru] * 2,
        input_output_aliases={t: t for t in range(2 * n)},
        compiler_params=pltpu.CompilerParams(has_side_effects=EFFECT),
    )(*src_thru, *lands_thru, sems[0], sems[1], after)
    return list(res[:n]), list(res[n:])


def _all_reduce_small(packed, name):
    nc = packed.shape[1]
    vmem = pl.BlockSpec(memory_space=pltpu.VMEM)

    def body(in_ref, out_ref, gbuf, send_sems, recv_sems):
        x, y, c = lax.axis_index("x"), lax.axis_index("y"), lax.axis_index("c")
        me = 4 * x + 2 * y + c
        gbuf[me] = jnp.sum(in_ref[...], axis=0, keepdims=True)
        copies = []
        for k in range(1, 8):
            peer = (x ^ ((k >> 2) & 1), y ^ ((k >> 1) & 1), c ^ (k & 1))
            rc = pltpu.make_async_remote_copy(
                src_ref=gbuf.at[me], dst_ref=gbuf.at[me], send_sem=send_sems.at[k - 1], recv_sem=recv_sems.at[k - 1],
                device_id=peer, device_id_type=MESH)
            rc.start()
            copies.append(rc)
        for k in range(1, 8):
            peer = (x ^ ((k >> 2) & 1), y ^ ((k >> 1) & 1), c ^ (k & 1))
            pltpu.make_async_remote_copy(
                src_ref=gbuf.at[me], dst_ref=gbuf.at[me ^ k], send_sem=send_sems.at[k - 1],
                recv_sem=recv_sems.at[k - 1], device_id=peer, device_id_type=MESH).wait_recv()
        for rc in copies:
            rc.wait_send()
        tot = gbuf[0]
        for d in range(1, 8):
            tot = tot + gbuf[d]
        out_ref[...] = tot

    return pl.pallas_call(
        body, name=name,
        in_specs=[vmem], out_specs=vmem,
        out_shape=jax.ShapeDtypeStruct((1, nc), F32),
        scratch_shapes=[pltpu.VMEM((8, 1, nc), F32), pltpu.SemaphoreType.DMA((7,)), pltpu.SemaphoreType.DMA((7,))],
    )(packed)


def _sum4(g_full, axis, slot_arr, recv, name):
    _, r, c = recv.shape
    tr = _row_tile(r, 512)
    nb = r // tr

    def body(slot_ref, own_ref, recv_ref, o_ref):
        acc = own_ref[...]
        for k in range(3):
            acc = acc + recv_ref[k].astype(F32)
        o_ref[...] = acc

    if axis == 0:
        own_map = lambda i, s: (s[0] * nb + i, 0)
    else:
        own_map = lambda i, s: (i, s[0])
    return pl.pallas_call(
        body, name=name,
        grid_spec=pltpu.PrefetchScalarGridSpec(
            num_scalar_prefetch=1, grid=(nb,),
            in_specs=[pl.BlockSpec((tr, c), own_map), pl.BlockSpec((3, tr, c), lambda i, s: (0, i, 0))],
            out_specs=pl.BlockSpec((tr, c), lambda i, s: (i, 0))),
        out_shape=jax.ShapeDtypeStruct((r, c), F32),
        compiler_params=pltpu.CompilerParams(dimension_semantics=("parallel",), vmem_limit_bytes=MM_VMEM_LIMIT),
    )(slot_arr, g_full, recv)


def _adamw(w, g_parts, m, v, name):
    r, c = w.shape
    tr = r if r % 128 else _row_tile(r, 256)
    npart = len(g_parts)

    def body(*refs):
        w_ref = refs[0]
        g_refs = refs[1:1 + npart]
        m_ref, v_ref, g_out, d_out, m_out, v_out = refs[1 + npart:]
        g = g_refs[0][...]
        for gr in g_refs[1:]:
            g = g + gr[...]
        mm = ADAM_B1 * m_ref[...] + (1.0 - ADAM_B1) * g
        vv = ADAM_B2 * v_ref[...] + (1.0 - ADAM_B2) * (g * g)
        m_hat = mm / (1.0 - ADAM_B1 ** ADAM_STEP)
        v_hat = vv / (1.0 - ADAM_B2 ** ADAM_STEP)
        g_out[...] = g
        d_out[...] = -ADAM_LR * (m_hat / (jnp.sqrt(v_hat) + ADAM_EPS) + ADAM_WD * w_ref[...])
        m_out[...] = mm
        v_out[...] = vv

    blk = pl.BlockSpec((tr, c), lambda i: (i, 0))
    shp = jax.ShapeDtypeStruct((r, c), F32)
    return pl.pallas_call(
        body, name=name, grid=(r // tr,),
        in_specs=[blk] * (3 + npart), out_specs=[blk] * 4, out_shape=[shp] * 4,
        compiler_params=pltpu.CompilerParams(dimension_semantics=("parallel",), vmem_limit_bytes=MM_VMEM_LIMIT),
    )(w, *g_parts, m, v)


def _pad_rows8(w):
    return jnp.pad(w, ((0, HALO - w.shape[0]), (0, 0)))


def kernel(x, mem, hgrn_lb, norm1_w, w_in, hgrn_norm_w, sconv_w, w_out, norm2_w, mem_norm_w, wq, wk, wv, wo, norm3_w, w_gate, w_up, ffn_conv_w, ffn_conv_b, w_down, final_norm_w, loss_target, m_hgrn_lb, m_norm1_w, m_w_in, m_hgrn_norm_w, m_sconv_w, m_w_out, m_norm2_w, m_mem_norm_w, m_wq, m_wk, m_wv, m_wo, m_norm3_w, m_w_gate, m_w_up, m_ffn_conv_w, m_ffn_conv_b, m_w_down, m_final_norm_w, v_hgrn_lb, v_norm1_w, v_w_in, v_hgrn_norm_w, v_sconv_w, v_w_out, v_norm2_w, v_mem_norm_w, v_wq, v_wk, v_wv, v_wo, v_norm3_w, v_w_gate, v_w_up, v_ffn_conv_w, v_ffn_conv_b, v_w_down, v_final_norm_w):
    xs, mems, tgt = x[0], mem[0], loss_target[0]
    d = xs.shape[1]
    fnw = final_norm_w.reshape(1, d)

    big = {"w_in": (w_in[0], 1), "w_out": (w_out[0], 0), "wq": (wq[0], 0), "wk": (wk[0], 0), "wv": (wv[0], 0),
           "wo": (wo[0], 0), "w_gate": (w_gate[0], 1), "w_up": (w_up[0], 1), "w_down": (w_down[0], 0)}
    names = list(big)
    slot_arr = (2 * lax.axis_index("x") + lax.axis_index("y")).astype(jnp.int32).reshape(1)
    gnames = names + ["sconv8", "fconv8"]
    axes = [big[n][1] for n in names] + [1, 1]
    groups = [["w_in"], ["w_out", "sconv8"], ["wq", "wk", "wv", "wo"], ["w_gate", "w_up", "fconv8", "w_down"]]
    gidx = [[gnames.index(n) for n in grp] for grp in groups]
    split = [True] * len(names) + [False, False]
    first = _cast_into_full(big["w_in"][0], 1, slot_arr, BF16, "cast_w_in")
    sems0, first, tok0 = _gather_start([first], [1], [True], [[0]], "gather_start_w_in")
    rest = [_cast_into_full(big[n][0], big[n][1], slot_arr, BF16, "cast_" + n, after=tok0) for n in names[1:]]
    rest += [_cast_into_full(_pad_rows8(sconv_w[0]), 1, slot_arr, F32, "cast_sconv_w", after=tok0),
             _cast_into_full(_pad_rows8(ffn_conv_w[0]), 1, slot_arr, F32, "cast_ffn_conv_w", after=tok0)]
    sems1, rest, tok = _gather_start(rest, axes[1:], split[1:], [[t - 1 for t in idx] for idx in gidx[1:]],
                                     "gather_start")
    gsems, fulls = sems0 + sems1, first + rest
    wf, relayed = {}, {}

    def gather_relay(g, after):
        idx = gidx[g]
        dsems, arrs, token = _gather_relay([fulls[t] for t in idx], [axes[t] for t in idx], [split[t] for t in idx],
                                           gsems[g], after, "gather_relay_%d" % g)
        relayed[g] = (dsems, arrs)
        return token[0:1, 0:1]

    def gather_finish(g, after):
        idx = gidx[g]
        dsems, arrs = relayed[g]
        got = _gather_finish(arrs, [axes[t] for t in idx], [split[t] for t in idx], dsems, after,
                             "gather_finish_%d" % g)
        wf.update(zip(groups[g], got))

    lb0, lb1 = hgrn_lb[0:1], hgrn_lb[1:2]

    h1 = _rmsnorm_fwd(xs, norm1_w + tok[0:1, 0:1], "norm1")
    slot = slot_arr[0]
    proj = _matmul_windows(h1, fulls[0], slot_arr, "proj_in_own")
    gather_relay(0, proj)
    gather_finish(0, proj)
    others = jnp.stack([(slot + 1) % N_CHIPS, (slot + 2) % N_CHIPS, (slot + 3) % N_CHIPS]).astype(jnp.int32)
    proj = _matmul_windows(h1, wf["w_in"], others, "proj_in", prev=proj)
    t1 = gather_relay(1, proj)
    o_h, og, states = _hgrn_fwd(proj, lb0, lb1, hgrn_norm_w + t1, "hgrn_fwd")
    gather_finish(1, o_h)
    t2 = gather_relay(2, o_h)
    sconv8 = wf["sconv8"]
    mix = _sconv_fwd(proj, sconv8 + t2, og, "sconv_fwd")
    x1 = _matmul(mix, wf["w_out"], "nn", "proj_out", residual=xs)
    h2 = _rmsnorm_fwd(x1, norm2_w, "norm2")
    gather_finish(2, h2)
    t3 = gather_relay(3, h2)
    mem_n = _rmsnorm_fwd(mems, mem_norm_w + t3, "norm_mem")
    qa = _matmul(h2, wf["wq"], "nn", "attn_q", out_dtype=BF16)
    ka = _matmul(mem_n, wf["wk"], "nn", "attn_k", out_dtype=BF16)
    va = _matmul(mem_n, wf["wv"], "nn", "attn_v", out_dtype=BF16)
    att = _attn_fwd(qa, ka, va, "attn_fwd")
    x2 = _matmul(att, wf["wo"], "nn", "attn_o", residual=x1)
    h3 = _rmsnorm_fwd(x2, norm3_w, "norm3")
    gather_finish(3, h3)
    fconv8 = wf["fconv8"]
    gate = _matmul(h3, wf["w_gate"], "nn", "ffn_gate", out_dtype=BF16)
    up = _matmul(h3, wf["w_up"], "nn", "ffn_up", out_dtype=BF16)
    z, act = _ffn_fwd(gate, up, fconv8, ffn_conv_b, "ffn_act")
    x3 = _matmul(z, wf["w_down"], "nn", "ffn_down", residual=x2)

    dx3, dx3b, g_final, loss8 = _final_loss_bwd(x3, tgt, fnw, "loss_bwd")
    gw = {}
    dz = _matmul(dx3b, wf["w_down"], "nt", "d_z", out_dtype=BF16)
    gw["w_down"] = _matmul(z, dx3b, "tn", "g_w_down", extra_bf16=True)
    dgate, du, g_fb, g_fw = _ffn_bwd(act, gate, up, dz, fconv8, "ffn_act_bwd")
    dh3 = _matmul(dgate, wf["w_gate"], "nt", "d_h3_gate")
    dh3 = _matmul(du, wf["w_up"], "nt", "d_h3_up", residual=dh3, out_dtype=BF16)
    gw["w_gate"] = _matmul(h3, dgate, "tn", "g_w_gate", extra_bf16=True)
    gw["w_up"] = _matmul(h3, du, "tn", "g_w_up", extra_bf16=True)
    pending = []

    def scatter_start(grp):
        sems, g_thru, lands, token = _scatter_start([gw[n][1] for n in grp], [big[n][1] for n in grp],
                                                    "scatter_start_" + grp[0])
        pending.append((grp, sems, g_thru, lands))
        return token[0:1, 0:1]

    tok1 = scatter_start(["w_down", "w_gate", "w_up"])
    dx2, dx2b, g_n3 = _rmsnorm_bwd(dh3, x2, norm3_w + tok1, dx3, "norm3_bwd")
    datt = _matmul(dx2b, wf["wo"], "nt", "d_att", out_dtype=BF16)
    gw["wo"] = _matmul(att, dx2b, "tn", "g_wo", extra_bf16=True)
    dqa, dka, dva = _attn_bwd(qa, ka, va, datt, "attn_bwd")
    dh2 = _matmul(dqa, wf["wq"], "nt", "d_h2", out_dtype=BF16)
    gw["wq"] = _matmul(h2, dqa, "tn", "g_wq", extra_bf16=True)
    gw["wk"] = _matmul(mem_n, dka, "tn", "g_wk", extra_bf16=True)
    gw["wv"] = _matmul(mem_n, dva, "tn", "g_wv", extra_bf16=True)
    tok2 = scatter_start(["wo", "wq", "wk", "wv"])
    dmem_n = _matmul(dka, wf["wk"], "nt", "d_memn_k")
    dmem_n = _matmul(dva, wf["wv"], "nt", "d_memn_v", residual=dmem_n)
    _, _, g_nm = _rmsnorm_bwd(dmem_n, mems, mem_norm_w, None, "norm_mem_bwd")
    dx1, dx1b, g_n2 = _rmsnorm_bwd(dh2, x1, norm2_w + tok2, dx2, "norm2_bwd")
    dmix = _matmul(dx1b, wf["w_out"], "nt", "d_mix", out_dtype=BF16)
    gw["w_out"] = _matmul(mix, dx1b, "tn", "g_w_out", extra_bf16=True)
    tok3 = scatter_start(["w_out"])
    dproj, g_lb, g_hn = _hgrn_bwd(proj, lb0, lb1, hgrn_norm_w + tok3, o_h, states, dmix, "hgrn_bwd")
    dproj, g_sw = _sconv_bwd(proj, sconv8, dmix, dproj, "sconv_bwd")
    gw["w_in"] = _matmul(h1, dproj, "tn", "g_w_in", extra_bf16=True, groups=7)
    tok4 = scatter_start(["w_in"])
    dh1 = _matmul(dproj, wf["w_in"], "nt", "d_h1", out_dtype=BF16, groups=7, tn=2048)
    dx, _, g_n1 = _rmsnorm_bwd(dh1, xs, norm1_w + tok4, dx1, "norm1_bwd")

    small = [g_n1, g_n2, g_n3, g_final, g_nm, g_lb, g_hn, g_fb,
             g_sw[0:8], g_sw[8:16], g_sw[16:24], g_fw[0:8], g_fw[8:16], g_fw[16:24], loss8]
    widths = [a.shape[1] for a in small]
    tot = _all_reduce_small(jnp.concatenate(small, axis=1), "all_reduce_small")
    offs = [0]
    for wd_ in widths:
        offs.append(offs[-1] + wd_)
    sm = [tot[:, offs[i]:offs[i + 1]] for i in range(len(small))]
    s_n1, s_n2, s_n3, s_final, s_nm, s_lb, s_hn, s_fb = sm[:8]
    s_sw = jnp.concatenate(sm[8:11], axis=0)
    s_fw = jnp.concatenate(sm[11:14], axis=0)
    loss = sm[14][0, 0]
    slot = 2 * lax.axis_index("x") + lax.axis_index("y")
    s_sw = lax.dynamic_slice_in_dim(s_sw, slot * (HGRN_W // N_CHIPS), HGRN_W // N_CHIPS, axis=1)
    fsh = ffn_conv_w.shape[2]
    s_fw = lax.dynamic_slice_in_dim(s_fw, slot * fsh, fsh, axis=1)
    s_lb2 = jnp.concatenate([s_lb, -s_lb], axis=0)

    swaps = []
    after = tot
    for grp, sems, g_thru, lands in pending:
        got = _scatter_wait(g_thru, lands, [big[n][1] for n in grp], sems, after, "scatter_wait_" + grp[0])
        sums = [_sum4(gw[n][0], big[n][1], slot_arr, r, "core_sum_" + n) for n, r in zip(grp, got)]
        ssems, s_thru, s_lands, after = _sibling_start(sums, "sibling_start_" + grp[0])
        swaps.append((grp, ssems, s_thru, s_lands))

    moments = {"hgrn_lb": (m_hgrn_lb, v_hgrn_lb), "norm1_w": (m_norm1_w, v_norm1_w), "w_in": (m_w_in, v_w_in),
               "hgrn_norm_w": (m_hgrn_norm_w, v_hgrn_norm_w), "sconv_w": (m_sconv_w, v_sconv_w),
               "w_out": (m_w_out, v_w_out), "norm2_w": (m_norm2_w, v_norm2_w),
               "mem_norm_w": (m_mem_norm_w, v_mem_norm_w), "wq": (m_wq, v_wq), "wk": (m_wk, v_wk), "wv": (m_wv, v_wv),
               "wo": (m_wo, v_wo), "norm3_w": (m_norm3_w, v_norm3_w), "w_gate": (m_w_gate, v_w_gate),
               "w_up": (m_w_up, v_w_up), "ffn_conv_w": (m_ffn_conv_w, v_ffn_conv_w),
               "ffn_conv_b": (m_ffn_conv_b, v_ffn_conv_b), "w_down": (m_w_down, v_w_down),
               "final_norm_w": (m_final_norm_w, v_final_norm_w)}
    weights = {"hgrn_lb": hgrn_lb, "norm1_w": norm1_w, "w_in": w_in, "hgrn_norm_w": hgrn_norm_w, "sconv_w": sconv_w,
               "w_out": w_out, "norm2_w": norm2_w, "mem_norm_w": mem_norm_w, "wq": wq, "wk": wk, "wv": wv, "wo": wo,
               "norm3_w": norm3_w, "w_gate": w_gate, "w_up": w_up, "ffn_conv_w": ffn_conv_w, "ffn_conv_b": ffn_conv_b,
               "w_down": w_down, "final_norm_w": final_norm_w}
    small_g = {"hgrn_lb": s_lb2, "norm1_w": s_n1, "hgrn_norm_w": s_hn, "sconv_w": s_sw, "norm2_w": s_n2,
               "mem_norm_w": s_nm, "norm3_w": s_n3, "ffn_conv_w": s_fw, "ffn_conv_b": s_fb, "final_norm_w": s_final}
    order = list(weights)
    res = {}

    def adamw(n, parts):
        shape = weights[n].shape
        w2 = weights[n].reshape((-1, shape[-1]))
        m2, v2 = (t.reshape(w2.shape) for t in moments[n])
        res[n] = [t.reshape(shape) for t in _adamw(w2, [p.reshape(w2.shape) for p in parts], m2, v2, "adamw_" + n)]

    for n in order:
        if n not in big:
            adamw(n, [small_g[n]])
    after = after + res["final_norm_w"][1][0]
    for grp, ssems, s_thru, s_lands in swaps:
        own, other = _sibling_wait(s_thru, s_lands, ssems, after, "sibling_wait_" + grp[0])
        for n, a, b in zip(grp, own, other):
            adamw(n, [a, b])
        after = res[grp[-1]][1]

    return (loss, dx[None], *[res[n][0] for n in order], *[res[n][1] for n in order],
            *[res[n][2] for n in order], *[res[n][3] for n in order])
```

```python
import jax
import jax.numpy as jnp
from jax import lax
from jax.experimental import pallas as pl
from jax.experimental.pallas import tpu as pltpu

F32 = jnp.float32
BF16 = jnp.bfloat16
MESH = pl.DeviceIdType.MESH

EPS = 1e-6
HGRN_W = 1024
HEAD = 128
N_HEADS = 8
CHUNK = 128
HGRN_UNROLL = 8
HGRN_HEADS_PER_STEP = 4
DPROJ_GROUPS = 8
MEM_HEADS = 4
MEM_HEAD_DIM = 512
N_CHIPS = 4
HALO = 8

ADAM_LR = 0.001
ADAM_B1 = 0.9
ADAM_B2 = 0.999
ADAM_EPS = 1e-08
ADAM_WD = 0.01
ADAM_STEP = 10


def _sigmoid(x):
    return 1.0 / (1.0 + jnp.exp(-x))


def _dot(a, b, dims):
    return lax.dot_general(a.astype(BF16), b.astype(BF16), (dims, ((), ())),
                           preferred_element_type=F32)


def _dot_nn(a, b):
    return _dot(a, b, ((1,), (0,)))


def _dot_nt(a, b):
    return _dot(a, b, ((1,), (1,)))


def _dot_tn(a, b):
    return _dot(a, b, ((0,), (0,)))


def _hdot(a, b, dims):
    return lax.dot_general(a, b, (dims, ((), ())), precision=lax.Precision.HIGH, preferred_element_type=F32)


def _hdot_nn(a, b):
    return _hdot(a, b, ((1,), (0,)))


def _hdot_tn(a, b):
    return _hdot(a, b, ((0,), (0,)))


def _exact_ones_dot(ones_bf16, x):
    hi = x.astype(BF16)
    r1 = x - hi.astype(F32)
    mid = r1.astype(BF16)
    lo = (r1 - mid.astype(F32)).astype(BF16)
    dims = (((1,), (0,)), ((), ()))
    return (lax.dot_general(ones_bf16, hi, dims, preferred_element_type=F32)
            + lax.dot_general(ones_bf16, mid, dims, preferred_element_type=F32)
            + lax.dot_general(ones_bf16, lo, dims, preferred_element_type=F32))


def _rows8(v):
    t, c = v.shape
    return v.reshape(t // 8, 8, c).sum(axis=0)


def _shift_down(x, halo, s):
    rolled = pltpu.roll(x, s, 0)
    hrolled = pltpu.roll(halo, s, 0)
    row = lax.broadcasted_iota(jnp.int32, hrolled.shape, 0)
    head = jnp.where(row < s, hrolled, rolled[:HALO])
    return jnp.concatenate([head, rolled[HALO:]], axis=0)


def _shift_up(x, halo, s):
    t = x.shape[0]
    rolled = pltpu.roll(x, t - s, 0)
    hrolled = pltpu.roll(halo, HALO - s, 0)
    row = lax.broadcasted_iota(jnp.int32, hrolled.shape, 0)
    tail = jnp.where(row >= HALO - s, hrolled, rolled[t - HALO:])
    return jnp.concatenate([rolled[:t - HALO], tail], axis=0)


def _params(*sem):
    return pltpu.CompilerParams(dimension_semantics=sem, vmem_limit_bytes=MM_VMEM_LIMIT)


def _row_tile(r, pref):
    while r % pref:
        pref //= 2
    return pref


def _rmsnorm_fwd(x, w, name, tm=512):
    s, d = x.shape
    tm = min(tm, s)

    def body(x_ref, w_ref, o_ref):
        xv = x_ref[...]
        r = lax.rsqrt(jnp.mean(xv * xv, axis=-1, keepdims=True) + EPS)
        o_ref[...] = ((xv * r) * w_ref[...]).astype(BF16)

    return pl.pallas_call(
        body, name=name, grid=(s // tm,),
        in_specs=[pl.BlockSpec((tm, d), lambda i: (i, 0)), pl.BlockSpec((1, d), lambda i: (0, 0))],
        out_specs=pl.BlockSpec((tm, d), lambda i: (i, 0)),
        out_shape=jax.ShapeDtypeStruct((s, d), BF16),
        compiler_params=_params("parallel"),
    )(x, w)


def _rmsnorm_bwd(dh, x, w, dres, name, tm=512):
    s, d = x.shape
    tm = min(tm, s)
    has_res = dres is not None

    def body(*refs):
        if has_res:
            dh_ref, x_ref, w_ref, dres_ref, dx_ref, dxb_ref, gw_ref = refs
        else:
            dh_ref, x_ref, w_ref, dx_ref, dxb_ref, gw_ref = refs

        @pl.when(pl.program_id(0) == 0)
        def _():
            gw_ref[...] = jnp.zeros_like(gw_ref)

        xv = x_ref[...]
        dhv = dh_ref[...].astype(F32)
        r = lax.rsqrt(jnp.mean(xv * xv, axis=-1, keepdims=True) + EPS)
        xhat = xv * r
        gw_ref[...] += _rows8(dhv * xhat)
        dxh = dhv * w_ref[...]
        dx = r * (dxh - xhat * jnp.mean(dxh * xhat, axis=-1, keepdims=True))
        if has_res:
            dx = dres_ref[...] + dx
        dx_ref[...] = dx
        dxb_ref[...] = dx.astype(BF16)

    row = pl.BlockSpec((tm, d), lambda i: (i, 0))
    in_specs = [row, row, pl.BlockSpec((1, d), lambda i: (0, 0))] + ([row] if has_res else [])
    args = (dh, x, w) + ((dres,) if has_res else ())
    return pl.pallas_call(
        body, name=name, grid=(s // tm,),
        in_specs=in_specs,
        out_specs=[row, row, pl.BlockSpec((8, d), lambda i: (0, 0))],
        out_shape=[jax.ShapeDtypeStruct((s, d), F32), jax.ShapeDtypeStruct((s, d), BF16),
                   jax.ShapeDtypeStruct((8, d), F32)],
        compiler_params=_params("arbitrary"),
    )(*args)


def _final_loss_bwd(x3, target, w, name, tm=512):
    s, d = x3.shape
    tm = min(tm, s)

    def body(x_ref, t_ref, w_ref, dx_ref, dxb_ref, gw_ref, loss_ref):
        @pl.when(pl.program_id(0) == 0)
        def _():
            gw_ref[...] = jnp.zeros_like(gw_ref)
            loss_ref[...] = jnp.zeros_like(loss_ref)

        xv = x_ref[...]
        r = lax.rsqrt(jnp.mean(xv * xv, axis=-1, keepdims=True) + EPS)
        xhat = xv * r
        y = xhat * w_ref[...]
        err = y - t_ref[...]
        part = 0.5 * jnp.mean(err * err, axis=-1, keepdims=True)
        tot = jnp.sum(part, axis=0, keepdims=True)
        rr = lax.broadcasted_iota(jnp.int32, loss_ref.shape, 0)
        cc = lax.broadcasted_iota(jnp.int32, loss_ref.shape, 1)
        loss_ref[...] += jnp.where((rr == 0) & (cc == 0), tot, 0.0)
        dy = err * (1.0 / d)
        gw_ref[...] += _rows8(dy * xhat)
        dxh = dy * w_ref[...]
        dx = r * (dxh - xhat * jnp.mean(dxh * xhat, axis=-1, keepdims=True))
        dx_ref[...] = dx
        dxb_ref[...] = dx.astype(BF16)

    row = pl.BlockSpec((tm, d), lambda i: (i, 0))
    return pl.pallas_call(
        body, name=name, grid=(s // tm,),
        in_specs=[row, row, pl.BlockSpec((1, d), lambda i: (0, 0))],
        out_specs=[row, row, pl.BlockSpec((8, d), lambda i: (0, 0)), pl.BlockSpec((8, 128), lambda i: (0, 0))],
        out_shape=[jax.ShapeDtypeStruct((s, d), F32), jax.ShapeDtypeStruct((s, d), BF16),
                   jax.ShapeDtypeStruct((8, d), F32), jax.ShapeDtypeStruct((8, 128), F32)],
        compiler_params=_params("arbitrary"),
    )(x3, target, w)


MM_TILES = (1024, 1408, 512, 256, 128)
MM_K_TILES = (2816, 2048, 1792, 1408, 1024, 512, 256, 128)
MM_VMEM_LIMIT = 56 * 1024 * 1024
MM_VMEM_BUDGET = 46 * 1024 * 1024


def _pick_tile(dim):
    for t in MM_TILES:
        if dim % t == 0:
            return t
    return dim


def _matmul(a, b, mode, name, *, out_dtype=F32, residual=None, extra_bf16=False, tm=None, tn=None, tk=None,
            groups=None):
    if groups is not None and mode == "nt":
        _, m, gw = a.shape
        n, k2 = b.shape
        k, tk = groups * gw, gw
    elif groups is not None and mode == "tn":
        k, m = a.shape
        _, k2, gw = b.shape
        n, tn = groups * gw, gw
    elif mode == "nn":
        (m, k), (k2, n) = a.shape, b.shape
    elif mode == "nt":
        (m, k), (n, k2) = a.shape, b.shape
    else:
        (k, m), (k2, n) = a.shape, b.shape
    assert k == k2, (a.shape, b.shape, mode)
    auto_tm = tm is None
    tm = _pick_tile(m) if tm is None else min(tm, m)
    tn = _pick_tile(n) if tn is None else min(tn, n)
    out_elt = jnp.dtype(out_dtype).itemsize + (2 if extra_bf16 else 0) + (4 if residual is not None else 0)

    def vmem_bytes(t, rows=None):
        rows = tm if rows is None else rows
        return (2 * (rows * t * a.dtype.itemsize + t * tn * b.dtype.itemsize) + 2 * rows * tn * out_elt
                + rows * tn * 4)

    if auto_tm and tk is None and m % (2 * tm) == 0 and vmem_bytes(k, 2 * tm) <= MM_VMEM_BUDGET:
        tm = 2 * tm

    if tk is None:
        tk = next(t for t in MM_K_TILES if k % t == 0 and t <= k and vmem_bytes(t) <= MM_VMEM_BUDGET)
    assert m % tm == 0 and n % tn == 0 and k % tk == 0, (m, n, k, tm, tn, tk)
    nk = k // tk
    dims = {"nn": ((1,), (0,)), "nt": ((1,), (1,)), "tn": ((0,), (0,))}[mode]
    has_res = residual is not None

    def body(*refs):
        refs = list(refs)
        a_ref, b_ref = refs[0], refs[1]
        r_ref = refs[2] if has_res else None
        outs = refs[2 + has_res:]
        o_ref = outs[0]
        o2_ref = outs[1] if extra_bf16 else None
        def finish(r):
            if has_res:
                r = r_ref[...] + r
            o_ref[...] = r.astype(out_dtype)
            if extra_bf16:
                o2_ref[...] = r.astype(BF16)

        if nk == 1:
            finish(_dot(a_ref[...], b_ref[...], dims))
            return
        acc = outs[-1]
        kk = pl.program_id(2)

        @pl.when(kk == 0)
        def _():
            acc[...] = _dot(a_ref[...], b_ref[...], dims)

        if nk > 2:
            @pl.when((kk > 0) & (kk < nk - 1))
            def _():
                acc[...] += _dot(a_ref[...], b_ref[...], dims)

        @pl.when(kk == nk - 1)
        def _():
            finish(acc[...] + _dot(a_ref[...], b_ref[...], dims))

    if mode == "tn":
        a_spec = pl.BlockSpec((tk, tm), lambda i, j, kk: (kk, i))
    elif groups is not None:
        a_spec = pl.BlockSpec((None, tm, tk), lambda i, j, kk: (kk, i, 0))
    else:
        a_spec = pl.BlockSpec((tm, tk), lambda i, j, kk: (i, kk))
    if mode == "nt":
        b_spec = pl.BlockSpec((tn, tk), lambda i, j, kk: (j, kk))
    elif groups is not None:
        b_spec = pl.BlockSpec((None, tk, tn), lambda i, j, kk: (j, kk, 0))
    else:
        b_spec = pl.BlockSpec((tk, tn), lambda i, j, kk: (kk, j))
    o_spec = pl.BlockSpec((tm, tn), lambda i, j, kk: (i, j))
    in_specs = [a_spec, b_spec] + ([o_spec] if has_res else [])
    out_specs = [o_spec] + ([o_spec] if extra_bf16 else [])
    out_shape = [jax.ShapeDtypeStruct((m, n), out_dtype)] + ([jax.ShapeDtypeStruct((m, n), BF16)] if extra_bf16 else [])
    args = (a, b) + ((residual,) if has_res else ())
    res = pl.pallas_call(
        body, name=name, grid=(m // tm, n // tn, nk),
        in_specs=in_specs, out_specs=out_specs, out_shape=out_shape,
        scratch_shapes=[pltpu.VMEM((tm, tn) if nk > 1 else (8, 128), F32)],
        compiler_params=pltpu.CompilerParams(dimension_semantics=("parallel", "parallel", "arbitrary"),
                                             vmem_limit_bytes=MM_VMEM_LIMIT),
    )(*args)
    return res if extra_bf16 else res[0]


def _matmul_windows(h, w, windows, name, prev=None):
    s, k = h.shape
    n = w.shape[1]
    tn = n // N_CHIPS
    tm = min(s, MM_TILES[0])

    def body(win_ref, h_ref, w_ref, *rest):
        rest[-1][...] = _dot(h_ref[...], w_ref[...], ((1,), (0,))).astype(BF16)

    extra = [] if prev is None else [prev]
    return pl.pallas_call(
        body, name=name,
        grid_spec=pltpu.PrefetchScalarGridSpec(
            num_scalar_prefetch=1, grid=(s // tm, windows.shape[0]),
            in_specs=[pl.BlockSpec((tm, k), lambda i, j, win: (i, 0)),
                      pl.BlockSpec((k, tn), lambda i, j, win: (0, win[j]))]
            + [pl.BlockSpec(memory_space=pl.ANY)] * len(extra),
            out_specs=pl.BlockSpec((tm, tn), lambda i, j, win: (i, win[j]))),
        out_shape=jax.ShapeDtypeStruct((s, n), BF16),
        input_output_aliases={3: 0} if extra else {},
        compiler_params=_params("parallel", "arbitrary"),
    )(windows, h, w, *extra)


def _hgrn_gates(qp, fp, lb):
    sig = _sigmoid(fp)
    f = lb + (1.0 - lb) * sig
    logf = jnp.log(f)
    k = 1.0 - f
    sq = _sigmoid(qp)
    q = qp * sq
    return sig, f, logf, k, sq, q


def _hgrn_fwd(proj, lb0, lb1, norm_w, name, tb=1024):
    s = proj.shape[0]
    tb = min(tb, s)
    nb, ncb = s // tb, tb // CHUNK

    def body(q_ref, f_ref, i_ref, g_ref, a0_ref, a1_ref, nw_ref, o_ref, og_ref, st_ref, state):
        @pl.when(pl.program_id(1) == 0)
        def _():
            state[...] = jnp.zeros_like(state)

        lb2 = _sigmoid(a0_ref[...] - a1_ref[...])
        row = lax.broadcasted_iota(jnp.int32, (CHUNK, CHUNK), 0)
        col = lax.broadcasted_iota(jnp.int32, (CHUNK, CHUNK), 1)
        tril = row >= col
        ones_l = tril.astype(BF16)
        nw = nw_ref[...]

        def chunk(c, carry):
            rows = pl.ds(pl.multiple_of(c * CHUNK, CHUNK), CHUNK)
            for hh in range(HGRN_HEADS_PER_STEP):
                cols = slice(hh * HEAD, (hh + 1) * HEAD)
                v = i_ref[rows, cols].astype(F32)
                _, _, logf, k, _, q = _hgrn_gates(q_ref[rows, cols].astype(F32), f_ref[rows, cols].astype(F32),
                                                  lb2[:, cols])
                b = _exact_ones_dot(ones_l, logf)
                bl = jnp.sum(logf, axis=0, keepdims=True)
                bm = 0.5 * bl
                st = state[hh]
                st_ref[hh, c] = st
                qt = q * jnp.exp(b - bm)
                kt = k * jnp.exp(bm - b)
                a = jnp.where(tril, _dot_nt(qt, kt), 0.0)
                o = _dot_nt(q * jnp.exp(b), st) + _dot_nn(a, v)
                state[hh] = st * jnp.exp(bl) + _dot_tn(v, k * jnp.exp(bl - b))
                o_ref[rows, cols] = o
                on = (o * lax.rsqrt(jnp.mean(o * o, axis=-1, keepdims=True) + EPS)) * nw
                gv = g_ref[rows, cols].astype(F32)
                og_ref[rows, cols] = (on * (gv * _sigmoid(gv))).astype(BF16)
            return carry

        lax.fori_loop(0, ncb, chunk, 0, unroll=HGRN_UNROLL)

    hp, wd = HGRN_HEADS_PER_STEP, HGRN_HEADS_PER_STEP * HEAD
    ngrp = N_HEADS // hp

    def colblk(group):
        return pl.BlockSpec((tb, wd), lambda h, j: (j, group * ngrp + h))

    vec = pl.BlockSpec((1, wd), lambda h, j: (0, h))
    out_blk = pl.BlockSpec((tb, wd), lambda h, j: (j, h))
    return pl.pallas_call(
        body, name=name, grid=(ngrp, nb),
        in_specs=[colblk(0), colblk(1), colblk(2), colblk(3), vec, vec, pl.BlockSpec((1, HEAD), lambda h, j: (0, 0))],
        out_specs=[out_blk, out_blk, pl.BlockSpec((hp, ncb, HEAD, HEAD), lambda h, j: (h, j, 0, 0))],
        out_shape=[jax.ShapeDtypeStruct((s, HGRN_W), F32), jax.ShapeDtypeStruct((s, 2 * HGRN_W), BF16),
                   jax.ShapeDtypeStruct((N_HEADS, s // CHUNK, HEAD, HEAD), F32)],
        scratch_shapes=[pltpu.VMEM((hp, HEAD, HEAD), F32)],
        compiler_params=_params("parallel", "arbitrary"),
    )(proj, proj, proj, proj, lb0, lb1, norm_w)


def _hgrn_bwd(proj, lb0, lb1, norm_w, o, states, dmix, name, tb=1024):
    s = proj.shape[0]
    tb = min(tb, s)
    nb, ncb = s // tb, tb // CHUNK

    def body(q_ref, f_ref, i_ref, g_ref, a0_ref, a1_ref, nw_ref, o_ref, st_ref, dm_ref,
             dp_ref, glb_ref, gnw_ref, dstate):
        h = pl.program_id(0)

        @pl.when(pl.program_id(1) == 0)
        def _():
            dstate[...] = jnp.zeros_like(dstate)
            glb_ref[...] = jnp.zeros_like(glb_ref)

        @pl.when((pl.program_id(1) == 0) & (h == 0))
        def _():
            gnw_ref[...] = jnp.zeros_like(gnw_ref)

        lb2 = _sigmoid(a0_ref[...] - a1_ref[...])
        row = lax.broadcasted_iota(jnp.int32, (CHUNK, CHUNK), 0)
        col = lax.broadcasted_iota(jnp.int32, (CHUNK, CHUNK), 1)
        tril = row >= col
        ones_l = tril.astype(BF16)
        ones_u = (row <= col).astype(BF16)
        nw = nw_ref[...]

        def chunk(cc, carry):
            c = ncb - 1 - cc
            rows = pl.ds(pl.multiple_of(c * CHUNK, CHUNK), CHUNK)
            for hh in range(HGRN_HEADS_PER_STEP):
                cols = slice(hh * HEAD, (hh + 1) * HEAD)
                lb = lb2[:, cols]
                qp = q_ref[rows, cols].astype(F32)
                v = i_ref[rows, cols].astype(F32)
                sig, f, logf, k, sq, q = _hgrn_gates(qp, f_ref[rows, cols].astype(F32), lb)
                gv = g_ref[rows, cols].astype(F32)
                sg = _sigmoid(gv)
                silu_g = gv * sg
                dog = dm_ref[rows, cols].astype(F32)
                ov = o_ref[rows, cols]
                r = lax.rsqrt(jnp.mean(ov * ov, axis=-1, keepdims=True) + EPS)
                ohat = ov * r
                on = ohat * nw
                dp_ref[3, rows, cols] = (dog * on * (sg * (1.0 + gv * (1.0 - sg)))).astype(BF16)
                don = dog * silu_g
                gnw_ref[...] += _rows8(don * ohat)
                doh = don * nw
                do = r * (doh - ohat * jnp.mean(doh * ohat, axis=-1, keepdims=True))
                b = _exact_ones_dot(ones_l, logf)
                bl = jnp.sum(logf, axis=0, keepdims=True)
                bm = 0.5 * bl
                e_q = jnp.exp(b - bm)
                e_k = jnp.exp(bm - b)
                e_b = jnp.exp(b)
                e_l = jnp.exp(bl - b)
                qt, kt, qb, kb = q * e_q, k * e_k, q * e_b, k * e_l
                st0 = st_ref[hh, c]
                dst = dstate[hh]
                a = jnp.where(tril, _dot_nt(qt, kt), 0.0)
                da = jnp.where(tril, _dot_nt(do, v), 0.0)
                dq = _hdot_nn(da, kt) * e_q + _dot_nn(do, st0) * e_b
                dkb = _dot_nn(v, dst) * e_l
                dk = _hdot_tn(da, qt) * e_k + dkb
                dv = _dot_tn(a, do) + _dot_nt(kb, dst)
                e_bl = jnp.exp(bl)
                dstate[hh] = dst * e_bl + _dot_tn(do, qb)
                db = q * dq - k * dk
                db_last = jnp.sum(k * dkb, axis=0, keepdims=True) + e_bl * jnp.sum(st0 * dst, axis=0, keepdims=True)
                dlogf = _exact_ones_dot(ones_u, db) + db_last
                dfg = dlogf / f - dk
                dp_ref[1, rows, cols] = (dfg * (1.0 - lb) * (sig * (1.0 - sig))).astype(BF16)
                glb_ref[:, cols] += _rows8(dfg * (1.0 - sig)) * (lb * (1.0 - lb))
                dp_ref[0, rows, cols] = (dq * (sq * (1.0 + qp * (1.0 - sq)))).astype(BF16)
                dp_ref[2, rows, cols] = dv.astype(BF16)
            return carry

        lax.fori_loop(0, ncb, chunk, 0, unroll=HGRN_UNROLL)

    hp, wd = HGRN_HEADS_PER_STEP, HGRN_HEADS_PER_STEP * HEAD
    ngrp = N_HEADS // hp

    def colblk(group):
        return pl.BlockSpec((tb, wd), lambda h, j: (nb - 1 - j, group * ngrp + h))

    vec = pl.BlockSpec((1, wd), lambda h, j: (0, h))
    blk = pl.BlockSpec((tb, wd), lambda h, j: (nb - 1 - j, h))
    return pl.pallas_call(
        body, name=name, grid=(ngrp, nb),
        in_specs=[colblk(0), colblk(1), colblk(2), colblk(3), vec, vec, pl.BlockSpec((1, HEAD), lambda h, j: (0, 0)),
                  blk, pl.BlockSpec((hp, ncb, HEAD, HEAD), lambda h, j: (h, nb - 1 - j, 0, 0)), blk],
        out_specs=[pl.BlockSpec((4, tb, wd), lambda h, j: (0, nb - 1 - j, h)),
                   pl.BlockSpec((8, wd), lambda h, j: (0, h)), pl.BlockSpec((8, HEAD), lambda h, j: (0, 0))],
        out_shape=[jax.ShapeDtypeStruct((DPROJ_GROUPS, s, HGRN_W), BF16),
                   jax.ShapeDtypeStruct((8, HGRN_W), F32), jax.ShapeDtypeStruct((8, HEAD), F32)],
        scratch_shapes=[pltpu.VMEM((hp, HEAD, HEAD), F32)],
        compiler_params=_params("arbitrary", "arbitrary"),
    )(proj, proj, proj, proj, lb0, lb1, norm_w, o, states, dmix)


HALO_BLK = 16


def _f32(ref):
    return ref[...].astype(F32)


def _halo_prev(ref):
    return ref[...].astype(F32)[HALO_BLK - HALO:]


def _halo_next(ref):
    return ref[...].astype(F32)[:HALO]


def _conv3(x0, x1, x2, w_ref):
    y = x0 * w_ref[0:1, :]
    y = y + x1 * w_ref[1:2, :]
    return y + x2 * w_ref[2:3, :]


def _sconv_fwd(proj, w8, mix, name, tb=512):
    s = proj.shape[0]
    tb = min(tb, s)
    hb = tb // HALO_BLK

    def body(cb_ref, cc_ref, ch_ref, cch_ref, chh_ref, w_ref, mix_ref, y_ref):
        first = pl.program_id(0) == 0
        u = _f32(cc_ref) * _f32(ch_ref)
        uh = jnp.where(first, 0.0, _halo_prev(cch_ref) * _halo_prev(chh_ref))
        conv = _conv3(_shift_down(u, uh, 2), _shift_down(u, uh, 1), u, w_ref)
        y_ref[...] = (_f32(cb_ref) * conv).astype(BF16)

    def blk(g):
        return pl.BlockSpec((tb, HGRN_W), lambda j: (j, g))

    def halo(g):
        return pl.BlockSpec((HALO_BLK, HGRN_W), lambda j: (jnp.maximum(j * hb - 1, 0), g))

    return pl.pallas_call(
        body, name=name, grid=(s // tb,),
        in_specs=[blk(4), blk(5), blk(6), halo(5), halo(6), pl.BlockSpec((HALO, HGRN_W), lambda j: (0, 0)),
                  pl.BlockSpec(memory_space=pl.ANY)],
        out_specs=pl.BlockSpec((tb, HGRN_W), lambda j: (j, 1)),
        out_shape=jax.ShapeDtypeStruct(mix.shape, BF16),
        input_output_aliases={6: 0},
        compiler_params=_params("parallel"),
    )(proj, proj, proj, proj, proj, w8, mix)


def _sconv_bwd(proj, w8, dmix, dproj, name, tb=512):
    s = proj.shape[0]
    tb = min(tb, s)
    hb = tb // HALO_BLK
    nb = s // tb
    last_h = s // HALO_BLK - 1

    def body(cb_ref, cc_ref, ch_ref, cch_ref, chh_ref, cbn_ref, dy_ref, dyn_ref, w_ref, dproj_ref,
             dp_ref, gw_ref):
        j = pl.program_id(0)

        @pl.when(j == 0)
        def _():
            gw_ref[...] = jnp.zeros_like(gw_ref)

        cc, ch, cb = _f32(cc_ref), _f32(ch_ref), _f32(cb_ref)
        u = cc * ch
        uh = jnp.where(j == 0, 0.0, _halo_prev(cch_ref) * _halo_prev(chh_ref))
        u2, u1 = _shift_down(u, uh, 2), _shift_down(u, uh, 1)
        conv = _conv3(u2, u1, u, w_ref)
        dy = _f32(dy_ref)
        dp_ref[0] = (dy * conv).astype(BF16)
        dc = dy * cb
        dcn = jnp.where(j == nb - 1, 0.0, _halo_next(dyn_ref) * _halo_next(cbn_ref))
        gw_ref[0:8, :] += _rows8(dc * u2)
        gw_ref[8:16, :] += _rows8(dc * u1)
        gw_ref[16:24, :] += _rows8(dc * u)
        du = dc * w_ref[2:3, :] + _shift_up(dc, dcn, 1) * w_ref[1:2, :] + _shift_up(dc, dcn, 2) * w_ref[0:1, :]
        dp_ref[1] = (du * ch).astype(BF16)
        dp_ref[2] = (du * cc).astype(BF16)
        dp_ref[3] = jnp.zeros(dp_ref.shape[1:], BF16)

    def blk(g):
        return pl.BlockSpec((tb, HGRN_W), lambda j: (j, g))

    def halo_prev(g):
        return pl.BlockSpec((HALO_BLK, HGRN_W), lambda j: (jnp.maximum(j * hb - 1, 0), g))

    def halo_next(g):
        return pl.BlockSpec((HALO_BLK, HGRN_W), lambda j: (jnp.minimum((j + 1) * hb, last_h), g))

    return pl.pallas_call(
        body, name=name, grid=(nb,),
        in_specs=[blk(4), blk(5), blk(6), halo_prev(5), halo_prev(6), halo_next(4), blk(1), halo_next(1),
                  pl.BlockSpec((HALO, HGRN_W), lambda j: (0, 0)), pl.BlockSpec(memory_space=pl.ANY)],
        out_specs=[pl.BlockSpec((4, tb, HGRN_W), lambda j: (1, j, 0)), pl.BlockSpec((24, HGRN_W), lambda j: (0, 0))],
        out_shape=[jax.ShapeDtypeStruct(dproj.shape, BF16), jax.ShapeDtypeStruct((24, HGRN_W), F32)],
        input_output_aliases={9: 0},
        compiler_params=_params("arbitrary"),
    )(proj, proj, proj, proj, proj, proj, dmix, dmix, w8, dproj)


def _attn_fwd(q, kk, vv, name, tb=1024):
    s, d = q.shape
    m = kk.shape[0]
    tb = min(tb, s)
    scale = MEM_HEAD_DIM ** -0.5

    def body(q_ref, k_ref, v_ref, o_ref):
        for hh in range(MEM_HEADS):
            cols = slice(hh * MEM_HEAD_DIM, (hh + 1) * MEM_HEAD_DIM)
            sc = _dot_nt(q_ref[:, cols], k_ref[:, cols]) * scale
            sc = sc - jnp.max(sc, axis=-1, keepdims=True)
            e = jnp.exp(sc)
            p = e / jnp.sum(e, axis=-1, keepdims=True)
            o_ref[:, cols] = _dot_nn(p, v_ref[:, cols]).astype(BF16)

    full = pl.BlockSpec((m, d), lambda i: (0, 0))
    return pl.pallas_call(
        body, name=name, grid=(s // tb,),
        in_specs=[pl.BlockSpec((tb, d), lambda i: (i, 0)), full, full],
        out_specs=pl.BlockSpec((tb, d), lambda i: (i, 0)),
        out_shape=jax.ShapeDtypeStruct((s, d), BF16),
        compiler_params=pltpu.CompilerParams(dimension_semantics=("parallel",), vmem_limit_bytes=MM_VMEM_LIMIT),
    )(q, kk, vv)


def _attn_bwd(q, kk, vv, datt, name, tb=1024):
    s, d = q.shape
    m = kk.shape[0]
    tb = min(tb, s)
    scale = MEM_HEAD_DIM ** -0.5

    def body(q_ref, k_ref, v_ref, do_ref, dq_ref, dk_ref, dv_ref):
        @pl.when(pl.program_id(0) == 0)
        def _():
            dk_ref[...] = jnp.zeros_like(dk_ref)
            dv_ref[...] = jnp.zeros_like(dv_ref)

        for hh in range(MEM_HEADS):
            cols = slice(hh * MEM_HEAD_DIM, (hh + 1) * MEM_HEAD_DIM)
            qh, kh, vh, doh = q_ref[:, cols], k_ref[:, cols], v_ref[:, cols], do_ref[:, cols]
            sc = _dot_nt(qh, kh) * scale
            sc = sc - jnp.max(sc, axis=-1, keepdims=True)
            e = jnp.exp(sc)
            p = e / jnp.sum(e, axis=-1, keepdims=True)
            dp = _dot_nt(doh, vh)
            ds = p * (dp - jnp.sum(dp * p, axis=-1, keepdims=True)) * scale
            dq_ref[:, cols] = _dot_nn(ds, kh).astype(BF16)
            dk_ref[:, cols] += _dot_tn(ds, qh)
            dv_ref[:, cols] += _dot_tn(p, doh)

    full = pl.BlockSpec((m, d), lambda i: (0, 0))
    row = pl.BlockSpec((tb, d), lambda i: (i, 0))
    return pl.pallas_call(
        body, name=name, grid=(s // tb,),
        in_specs=[row, full, full, row],
        out_specs=[row, full, full],
        out_shape=[jax.ShapeDtypeStruct((s, d), BF16), jax.ShapeDtypeStruct((m, d), F32),
                   jax.ShapeDtypeStruct((m, d), F32)],
        compiler_params=pltpu.CompilerParams(dimension_semantics=("arbitrary",), vmem_limit_bytes=MM_VMEM_LIMIT),
    )(q, kk, vv, datt)


def _ffn_fwd(g, u, w8, bias, name, tb=512, tc=1408):
    s, f = g.shape
    tb = min(tb, s)
    tc = tc if f % tc == 0 else 512
    hb = tb // HALO_BLK

    def body(g_ref, gh_ref, u_ref, w_ref, b_ref, z_ref, a_ref):
        gv = _f32(g_ref)
        gh = jnp.where(pl.program_id(1) == 0, 0.0, _halo_prev(gh_ref))
        a = _conv3(_shift_down(gv, gh, 2), _shift_down(gv, gh, 1), gv, w_ref) + b_ref[...]
        a_ref[...] = a.astype(BF16)
        z_ref[...] = ((a * _sigmoid(a)) * _f32(u_ref)).astype(BF16)

    blk = pl.BlockSpec((tb, tc), lambda c, j: (j, c))
    return pl.pallas_call(
        body, name=name, grid=(f // tc, s // tb),
        in_specs=[blk, pl.BlockSpec((HALO_BLK, tc), lambda c, j: (jnp.maximum(j * hb - 1, 0), c)), blk,
                  pl.BlockSpec((HALO, tc), lambda c, j: (0, c)), pl.BlockSpec((1, tc), lambda c, j: (0, c))],
        out_specs=[blk, blk],
        out_shape=[jax.ShapeDtypeStruct((s, f), BF16), jax.ShapeDtypeStruct((s, f), BF16)],
        compiler_params=pltpu.CompilerParams(dimension_semantics=("parallel", "parallel"),
                                             vmem_limit_bytes=MM_VMEM_LIMIT),
    )(g, g, u, w8, bias)


def _ffn_bwd(a, g, u, dz, w8, name, tb=512, tc=1408):
    s, f = g.shape
    tb = min(tb, s)
    tc = tc if f % tc == 0 else 512
    nb = s // tb

    def body(a_ref, g_ref, u_ref, dz_ref, w_ref, dg_ref, du_ref, gb_ref, gw_ref, da_next):
        jj = pl.program_id(1)

        @pl.when(jj == 0)
        def _():
            gb_ref[...] = jnp.zeros_like(gb_ref)
            gw_ref[...] = jnp.zeros_like(gw_ref)
            da_next[...] = jnp.zeros_like(da_next)

        a = _f32(a_ref)
        sa = _sigmoid(a)
        dz = _f32(dz_ref)
        du_ref[...] = (dz * (a * sa)).astype(BF16)
        da = dz * _f32(u_ref) * (sa * (1.0 + a * (1.0 - sa)))
        gb_ref[...] += _rows8(da)
        dan = da_next[...]
        da1, da2 = _shift_up(da, dan, 1), _shift_up(da, dan, 2)
        gv = _f32(g_ref)
        gw_ref[0:8, :] += _rows8(da2 * gv)
        gw_ref[8:16, :] += _rows8(da1 * gv)
        gw_ref[16:24, :] += _rows8(da * gv)
        dg_ref[...] = (da * w_ref[2:3, :] + da1 * w_ref[1:2, :] + da2 * w_ref[0:1, :]).astype(BF16)
        da_next[...] = da[:HALO]

    blk = pl.BlockSpec((tb, tc), lambda c, jj: (nb - 1 - jj, c))
    return pl.pallas_call(
        body, name=name, grid=(f // tc, nb),
        in_specs=[blk, blk, blk, blk, pl.BlockSpec((HALO, tc), lambda c, jj: (0, c))],
        out_specs=[blk, blk, pl.BlockSpec((8, tc), lambda c, jj: (0, c)), pl.BlockSpec((24, tc), lambda c, jj: (0, c))],
        out_shape=[jax.ShapeDtypeStruct((s, f), BF16), jax.ShapeDtypeStruct((s, f), BF16),
                   jax.ShapeDtypeStruct((8, f), F32), jax.ShapeDtypeStruct((24, f), F32)],
        scratch_shapes=[pltpu.VMEM((HALO, tc), F32)],
        compiler_params=pltpu.CompilerParams(dimension_semantics=("parallel", "arbitrary"),
                                             vmem_limit_bytes=MM_VMEM_LIMIT),
    )(a, g, u, dz, w8)


def _window(ref, axis, slot, size):
    start = pl.multiple_of(slot * size, size)
    if axis == 0:
        return ref.at[pl.ds(start, size), :]
    return ref.at[:, pl.ds(start, size)]


def _chip_peers():
    x, y, c = lax.axis_index("x"), lax.axis_index("y"), lax.axis_index("c")
    peers = [(1 - x, y, c), (x, 1 - y, c), (1 - x, 1 - y, c)]
    slots = [2 * (1 - x) + y, 2 * x + (1 - y), 2 * (1 - x) + (1 - y)]
    return 2 * x + y, peers, slots


HBM_SPEC = pl.BlockSpec(memory_space=pltpu.HBM)
SEM_SPEC = pl.BlockSpec(memory_space=pltpu.SEMAPHORE)
EFFECT = pltpu.SideEffectType.DATAFLOW_SIDE_EFFECTING


def _hbm(a):
    return pltpu.with_memory_space_constraint(a, pltpu.HBM)


def _cast_into_full(x, axis, slot_arr, dtype, name, after=None):
    r, c = x.shape
    tr = _row_tile(r, 512)
    nb = r // tr
    full = (r * N_CHIPS, c) if axis == 0 else (r, c * N_CHIPS)

    def body(slot_ref, x_ref, *rest):
        rest[-1][...] = x_ref[...].astype(dtype)

    if axis == 0:
        out_map = lambda i, s: (s[0] * nb + i, 0)
    else:
        out_map = lambda i, s: (i, s[0])
    extra = [] if after is None else [after]
    return pl.pallas_call(
        body, name=name,
        grid_spec=pltpu.PrefetchScalarGridSpec(
            num_scalar_prefetch=1, grid=(nb,),
            in_specs=[pl.BlockSpec((tr, c), lambda i, s: (i, 0))] + [pl.BlockSpec(memory_space=pl.ANY)] * len(extra),
            out_specs=pl.BlockSpec((tr, c), out_map)),
        out_shape=jax.ShapeDtypeStruct(full, dtype),
        compiler_params=_params("parallel"),
    )(slot_arr, x, *extra)


def _piece(ref, axis, slot, half):
    size = ref.shape[axis] // N_CHIPS
    if half is None:
        return _window(ref, axis, slot, size)
    if axis == 0:
        h = size // 2
        return ref.at[pl.ds(pl.multiple_of(slot * size + half * h, h), h), :]
    h = ref.shape[0] // 2
    return ref.at[pl.ds(pl.multiple_of(half * h, h), h), pl.ds(pl.multiple_of(slot * size, size), size)]


def _gather_start(fulls, axes, split, groups, name):
    n, ng = len(fulls), len(groups)

    def body(*refs):
        outs = refs[n:]
        sems = outs[:2 * ng]
        thru = outs[2 * ng:2 * ng + n]
        token = outs[-1]
        slot, peers, _ = _chip_peers()
        c = lax.axis_index("c")
        for g, members in enumerate(groups):
            for i, t in enumerate(members):
                mine = _piece(thru[t], axes[t], slot, c if split[t] else None)
                for k in range(3):
                    pltpu.make_async_remote_copy(
                        src_ref=mine, dst_ref=mine, send_sem=sems[2 * g].at[3 * i + k],
                        recv_sem=sems[2 * g + 1].at[3 * i + k], device_id=peers[k], device_id_type=MESH).start()
        token[...] = jnp.zeros_like(token)

    sem_shapes = []
    for members in groups:
        sem_shapes += [pltpu.SemaphoreType.DMA((3 * len(members),))] * 2
    res = pl.pallas_call(
        body, name=name,
        in_specs=[HBM_SPEC] * n,
        out_specs=[SEM_SPEC] * (2 * ng) + [HBM_SPEC] * n + [pl.BlockSpec(memory_space=pltpu.VMEM)],
        out_shape=sem_shapes + [pltpu.HBM(f.shape, f.dtype) for f in fulls] + [jax.ShapeDtypeStruct((8, 128), F32)],
        input_output_aliases={t: 2 * ng + t for t in range(n)},
        compiler_params=pltpu.CompilerParams(has_side_effects=EFFECT),
    )(*[_hbm(f) for f in fulls])
    sems = [(res[2 * g], res[2 * g + 1]) for g in range(ng)]
    return sems, list(res[2 * ng:2 * ng + n]), res[-1]


def _gather_relay(fulls, axes, split, sems, after, name):
    n = len(fulls)
    nsplit = sum(split)

    def body(*refs):
        send_sems, recv_sems = refs[n], refs[n + 1]
        outs = refs[n + 3:]
        d_send, d_recv = outs[0], outs[1]
        thru = outs[2:2 + n]
        token = outs[-1]
        slot, peers, slots = _chip_peers()
        c = lax.axis_index("c")
        sibling = (lax.axis_index("x"), lax.axis_index("y"), 1 - c)
        for t in range(n):
            half = c if split[t] else None
            for k in range(3):
                cp = pltpu.make_async_remote_copy(
                    src_ref=_piece(thru[t], axes[t], slot, half), dst_ref=_piece(thru[t], axes[t], slots[k], half),
                    send_sem=send_sems.at[3 * t + k], recv_sem=recv_sems.at[3 * t + k],
                    device_id=peers[k], device_id_type=MESH)
                cp.wait_send()
                cp.wait_recv()
        i = 0
        for t in range(n):
            if not split[t]:
                continue
            for k in range(3):
                got = _piece(thru[t], axes[t], slots[k], c)
                pltpu.make_async_remote_copy(
                    src_ref=got, dst_ref=got, send_sem=d_send.at[3 * i + k], recv_sem=d_recv.at[3 * i + k],
                    device_id=sibling, device_id_type=MESH).start()
            i += 1
        token[...] = jnp.zeros_like(token)

    res = pl.pallas_call(
        body, name=name,
        in_specs=[HBM_SPEC] * n + [SEM_SPEC, SEM_SPEC, pl.BlockSpec(memory_space=pl.ANY)],
        out_specs=[SEM_SPEC, SEM_SPEC] + [HBM_SPEC] * n + [pl.BlockSpec(memory_space=pltpu.VMEM)],
        out_shape=[pltpu.SemaphoreType.DMA((3 * nsplit,)), pltpu.SemaphoreType.DMA((3 * nsplit,))]
        + [pltpu.HBM(f.shape, f.dtype) for f in fulls] + [jax.ShapeDtypeStruct((8, 128), F32)],
        input_output_aliases={t: 2 + t for t in range(n)},
        compiler_params=pltpu.CompilerParams(has_side_effects=EFFECT),
    )(*fulls, sems[0], sems[1], after)
    return (res[0], res[1]), list(res[2:2 + n]), res[-1]


def _gather_finish(fulls, axes, split, sems, after, name):
    n = len(fulls)

    def body(*refs):
        d_send, d_recv = refs[n], refs[n + 1]
        thru = refs[n + 3:]
        _, _, slots = _chip_peers()
        c = lax.axis_index("c")
        sibling = (lax.axis_index("x"), lax.axis_index("y"), 1 - c)
        i = 0
        for t in range(n):
            if not split[t]:
                continue
            for k in range(3):
                cp = pltpu.make_async_remote_copy(
                    src_ref=_piece(thru[t], axes[t], slots[k], c), dst_ref=_piece(thru[t], axes[t], slots[k], 1 - c),
                    send_sem=d_send.at[3 * i + k], recv_sem=d_recv.at[3 * i + k],
                    device_id=sibling, device_id_type=MESH)
                cp.wait_send()
                cp.wait_recv()
            i += 1

    return pl.pallas_call(
        body, name=name,
        in_specs=[HBM_SPEC] * n + [SEM_SPEC, SEM_SPEC, pl.BlockSpec(memory_space=pl.ANY)],
        out_specs=[HBM_SPEC] * n,
        out_shape=[pltpu.HBM(f.shape, f.dtype) for f in fulls],
        input_output_aliases={t: t for t in range(n)},
        compiler_params=pltpu.CompilerParams(has_side_effects=EFFECT),
    )(*fulls, sems[0], sems[1], after)


def _scatter_start(grads_bf16, axes, name):
    n = len(grads_bf16)

    def shard_shape(g, ax):
        return (g.shape[0] // N_CHIPS, g.shape[1]) if ax == 0 else (g.shape[0], g.shape[1] // N_CHIPS)

    shapes = [shard_shape(g, ax) for g, ax in zip(grads_bf16, axes)]

    def body(*refs):
        outs = refs[2 * n:]
        send_sems, recv_sems = outs[0], outs[1]
        gb, land = outs[2:2 + n], outs[2 + n:2 + 2 * n]
        token = outs[-1]
        _, peers, slots = _chip_peers()
        for t in range(n):
            size = shapes[t][axes[t]]
            for k in range(3):
                pltpu.make_async_remote_copy(
                    src_ref=_window(gb[t], axes[t], slots[k], size), dst_ref=land[t].at[k],
                    send_sem=send_sems.at[3 * t + k], recv_sem=recv_sems.at[3 * t + k],
                    device_id=peers[k], device_id_type=MESH).start()
        token[...] = jnp.zeros_like(token)

    lands = [_hbm(lax.empty((3,) + sh, BF16)) for sh in shapes]
    res = pl.pallas_call(
        body, name=name,
        in_specs=[HBM_SPEC] * (2 * n),
        out_specs=[SEM_SPEC, SEM_SPEC] + [HBM_SPEC] * (2 * n) + [pl.BlockSpec(memory_space=pltpu.VMEM)],
        out_shape=[pltpu.SemaphoreType.DMA((3 * n,)), pltpu.SemaphoreType.DMA((3 * n,))]
        + [pltpu.HBM(g.shape, g.dtype) for g in grads_bf16] + [pltpu.HBM((3,) + sh, BF16) for sh in shapes]
        + [jax.ShapeDtypeStruct((8, 128), F32)],
        input_output_aliases={t: 2 + t for t in range(2 * n)},
        compiler_params=pltpu.CompilerParams(has_side_effects=EFFECT),
    )(*[_hbm(g) for g in grads_bf16], *lands)
    return (res[0], res[1]), list(res[2:2 + n]), list(res[2 + n:2 + 2 * n]), res[-1]


def _scatter_wait(grads_thru, lands_thru, axes, sems, after, name):
    n = len(grads_thru)

    def body(*refs):
        send_sems, recv_sems = refs[2 * n], refs[2 * n + 1]
        outs = refs[2 * n + 3:]
        gb, land = outs[:n], outs[n:]
        _, peers, slots = _chip_peers()
        for t in range(n):
            size = land[t].shape[1 + axes[t]]
            for k in range(3):
                cp = pltpu.make_async_remote_copy(
                    src_ref=_window(gb[t], axes[t], slots[k], size), dst_ref=land[t].at[k],
                    send_sem=send_sems.at[3 * t + k], recv_sem=recv_sems.at[3 * t + k],
                    device_id=peers[k], device_id_type=MESH)
                cp.wait_send()
                cp.wait_recv()

    res = pl.pallas_call(
        body, name=name,
        in_specs=[HBM_SPEC] * (2 * n) + [SEM_SPEC, SEM_SPEC, pl.BlockSpec(memory_space=pl.ANY)],
        out_specs=[HBM_SPEC] * (2 * n),
        out_shape=[pltpu.HBM(g.shape, g.dtype) for g in grads_thru] + [pltpu.HBM(l.shape, l.dtype) for l in lands_thru],
        input_output_aliases={t: t for t in range(2 * n)},
        compiler_params=pltpu.CompilerParams(has_side_effects=EFFECT),
    )(*grads_thru, *lands_thru, sems[0], sems[1], after)
    return list(res[n:])


def _sibling_start(arrs, name):
    n = len(arrs)

    def body(*refs):
        outs = refs[2 * n:]
        send_sems, recv_sems = outs[0], outs[1]
        src, land = outs[2:2 + n], outs[2 + n:2 + 2 * n]
        token = outs[-1]
        sibling = (lax.axis_index("x"), lax.axis_index("y"), 1 - lax.axis_index("c"))
        for t in range(n):
            pltpu.make_async_remote_copy(
                src_ref=src[t], dst_ref=land[t], send_sem=send_sems.at[t], recv_sem=recv_sems.at[t],
                device_id=sibling, device_id_type=MESH).start()
        token[...] = jnp.zeros_like(token)

    lands = [_hbm(lax.empty(a.shape, a.dtype)) for a in arrs]
    res = pl.pallas_call(
        body, name=name,
        in_specs=[HBM_SPEC] * (2 * n),
        out_specs=[SEM_SPEC, SEM_SPEC] + [HBM_SPEC] * (2 * n) + [pl.BlockSpec(memory_space=pltpu.VMEM)],
        out_shape=[pltpu.SemaphoreType.DMA((n,)), pltpu.SemaphoreType.DMA((n,))]
        + [pltpu.HBM(a.shape, a.dtype) for a in arrs] * 2 + [jax.ShapeDtypeStruct((8, 128), F32)],
        input_output_aliases={t: 2 + t for t in range(2 * n)},
        compiler_params=pltpu.CompilerParams(has_side_effects=EFFECT),
    )(*[_hbm(a) for a in arrs], *lands)
    return (res[0], res[1]), list(res[2:2 + n]), list(res[2 + n:2 + 2 * n]), res[-1]


def _sibling_wait(src_thru, lands_thru, sems, after, name):
    n = len(src_thru)

    def body(*refs):
        send_sems, recv_sems = refs[2 * n], refs[2 * n + 1]
        outs = refs[2 * n + 3:]
        src, land = outs[:n], outs[n:]
        sibling = (lax.axis_index("x"), lax.axis_index("y"), 1 - lax.axis_index("c"))
        for t in range(n):
            cp = pltpu.make_async_remote_copy(
                src_ref=src[t], dst_ref=land[t], send_sem=send_sems.at[t], recv_sem=recv_sems.at[t],
                device_id=sibling, device_id_type=MESH)
            cp.wait_send()
            cp.wait_recv()

    res = pl.pallas_call(
        body, name=name,
        in_specs=[HBM_SPEC] * (2 * n) + [SEM_SPEC, SEM_SPEC, pl.BlockSpec(memory_space=pl.ANY)],
        out_specs=[HBM_SPEC] * (2 * n),
        out_shape=[pltpu.HBM(a.shape, a.dtype) for a in src_thru] * 2,
        input_output_aliases={t: t for t in range(2 * n)},
        compiler_params=pltpu.CompilerParams(has_side_effects=EFFECT),
    )(*src_thru, *lands_thru, sems[0], sems[1], after)
    return list(res[:n]), list(res[n:])


def _all_reduce_small(packed, name):
    nc = packed.shape[1]
    vmem = pl.BlockSpec(memory_space=pltpu.VMEM)

    def body(in_ref, out_ref, gbuf, send_sems, recv_sems):
        x, y, c = lax.axis_index("x"), lax.axis_index("y"), lax.axis_index("c")
        me = 4 * x + 2 * y + c
        gbuf[me] = jnp.sum(in_ref[...], axis=0, keepdims=True)
        copies = []
        for k in range(1, 8):
            peer = (x ^ ((k >> 2) & 1), y ^ ((k >> 1) & 1), c ^ (k & 1))
            rc = pltpu.make_async_remote_copy(
                src_ref=gbuf.at[me], dst_ref=gbuf.at[me], send_sem=send_sems.at[k - 1], recv_sem=recv_sems.at[k - 1],
                device_id=peer, device_id_type=MESH)
            rc.start()
            copies.append(rc)
        for k in range(1, 8):
            peer = (x ^ ((k >> 2) & 1), y ^ ((k >> 1) & 1), c ^ (k & 1))
            pltpu.make_async_remote_copy(
                src_ref=gbuf.at[me], dst_ref=gbuf.at[me ^ k], send_sem=send_sems.at[k - 1],
                recv_sem=recv_sems.at[k - 1], device_id=peer, device_id_type=MESH).wait_recv()
        for rc in copies:
            rc.wait_send()
        tot = gbuf[0]
        for d in range(1, 8):
            tot = tot + gbuf[d]
        out_ref[...] = tot

    return pl.pallas_call(
        body, name=name,
        in_specs=[vmem], out_specs=vmem,
        out_shape=jax.ShapeDtypeStruct((1, nc), F32),
        scratch_shapes=[pltpu.VMEM((8, 1, nc), F32), pltpu.SemaphoreType.DMA((7,)), pltpu.SemaphoreType.DMA((7,))],
    )(packed)


def _sum4(g_full, axis, slot_arr, recv, name):
    _, r, c = recv.shape
    tr = _row_tile(r, 512)
    nb = r // tr

    def body(slot_ref, own_ref, recv_ref, o_ref):
        acc = own_ref[...]
        for k in range(3):
            acc = acc + recv_ref[k].astype(F32)
        o_ref[...] = acc

    if axis == 0:
        own_map = lambda i, s: (s[0] * nb + i, 0)
    else:
        own_map = lambda i, s: (i, s[0])
    return pl.pallas_call(
        body, name=name,
        grid_spec=pltpu.PrefetchScalarGridSpec(
            num_scalar_prefetch=1, grid=(nb,),
            in_specs=[pl.BlockSpec((tr, c), own_map), pl.BlockSpec((3, tr, c), lambda i, s: (0, i, 0))],
            out_specs=pl.BlockSpec((tr, c), lambda i, s: (i, 0))),
        out_shape=jax.ShapeDtypeStruct((r, c), F32),
        compiler_params=pltpu.CompilerParams(dimension_semantics=("parallel",), vmem_limit_bytes=MM_VMEM_LIMIT),
    )(slot_arr, g_full, recv)


def _adamw(w, g_parts, m, v, name):
    r, c = w.shape
    tr = r if r % 128 else _row_tile(r, 256)
    npart = len(g_parts)

    def body(*refs):
        w_ref = refs[0]
        g_refs = refs[1:1 + npart]
        m_ref, v_ref, g_out, d_out, m_out, v_out = refs[1 + npart:]
        g = g_refs[0][...]
        for gr in g_refs[1:]:
            g = g + gr[...]
        mm = ADAM_B1 * m_ref[...] + (1.0 - ADAM_B1) * g
        vv = ADAM_B2 * v_ref[...] + (1.0 - ADAM_B2) * (g * g)
        m_hat = mm / (1.0 - ADAM_B1 ** ADAM_STEP)
        v_hat = vv / (1.0 - ADAM_B2 ** ADAM_STEP)
        g_out[...] = g
        d_out[...] = -ADAM_LR * (m_hat / (jnp.sqrt(v_hat) + ADAM_EPS) + ADAM_WD * w_ref[...])
        m_out[...] = mm
        v_out[...] = vv

    blk = pl.BlockSpec((tr, c), lambda i: (i, 0))
    shp = jax.ShapeDtypeStruct((r, c), F32)
    return pl.pallas_call(
        body, name=name, grid=(r // tr,),
        in_specs=[blk] * (3 + npart), out_specs=[blk] * 4, out_shape=[shp] * 4,
        compiler_params=pltpu.CompilerParams(dimension_semantics=("parallel",), vmem_limit_bytes=MM_VMEM_LIMIT),
    )(w, *g_parts, m, v)


def _pad_rows8(w):
    return jnp.pad(w, ((0, HALO - w.shape[0]), (0, 0)))


def kernel(x, mem, hgrn_lb, norm1_w, w_in, hgrn_norm_w, sconv_w, w_out, norm2_w, mem_norm_w, wq, wk, wv, wo, norm3_w, w_gate, w_up, ffn_conv_w, ffn_conv_b, w_down, final_norm_w, loss_target, m_hgrn_lb, m_norm1_w, m_w_in, m_hgrn_norm_w, m_sconv_w, m_w_out, m_norm2_w, m_mem_norm_w, m_wq, m_wk, m_wv, m_wo, m_norm3_w, m_w_gate, m_w_up, m_ffn_conv_w, m_ffn_conv_b, m_w_down, m_final_norm_w, v_hgrn_lb, v_norm1_w, v_w_in, v_hgrn_norm_w, v_sconv_w, v_w_out, v_norm2_w, v_mem_norm_w, v_wq, v_wk, v_wv, v_wo, v_norm3_w, v_w_gate, v_w_up, v_ffn_conv_w, v_ffn_conv_b, v_w_down, v_final_norm_w):
    xs, mems, tgt = x[0], mem[0], loss_target[0]
    d = xs.shape[1]
    fnw = final_norm_w.reshape(1, d)

    big = {"w_in": (w_in[0], 1), "w_out": (w_out[0], 0), "wq": (wq[0], 0), "wk": (wk[0], 0), "wv": (wv[0], 0),
           "wo": (wo[0], 0), "w_gate": (w_gate[0], 1), "w_up": (w_up[0], 1), "w_down": (w_down[0], 0)}
    names = list(big)
    slot_arr = (2 * lax.axis_index("x") + lax.axis_index("y")).astype(jnp.int32).reshape(1)
    gnames = names + ["sconv8", "fconv8"]
    axes = [big[n][1] for n in names] + [1, 1]
    groups = [["w_in"], ["w_out", "sconv8"], ["wq", "wk", "wv", "wo"], ["w_gate", "w_up", "fconv8", "w_down"]]
    gidx = [[gnames.index(n) for n in grp] for grp in groups]
    split = [True] * len(names) + [False, False]
    first = _cast_into_full(big["w_in"][0], 1, slot_arr, BF16, "cast_w_in")
    sems0, first, tok0 = _gather_start([first], [1], [True], [[0]], "gather_start_w_in")
    rest = [_cast_into_full(big[n][0], big[n][1], slot_arr, BF16, "cast_" + n, after=tok0) for n in names[1:]]
    rest += [_cast_into_full(_pad_rows8(sconv_w[0]), 1, slot_arr, F32, "cast_sconv_w", after=tok0),
             _cast_into_full(_pad_rows8(ffn_conv_w[0]), 1, slot_arr, F32, "cast_ffn_conv_w", after=tok0)]
    sems1, rest, tok = _gather_start(rest, axes[1:], split[1:], [[t - 1 for t in idx] for idx in gidx[1:]],
                                     "gather_start")
    gsems, fulls = sems0 + sems1, first + rest
    wf, relayed = {}, {}

    def gather_relay(g, after):
        idx = gidx[g]
        dsems, arrs, token = _gather_relay([fulls[t] for t in idx], [axes[t] for t in idx], [split[t] for t in idx],
                                           gsems[g], after, "gather_relay_%d" % g)
        relayed[g] = (dsems, arrs)
        return token[0:1, 0:1]

    def gather_finish(g, after):
        idx = gidx[g]
        dsems, arrs = relayed[g]
        got = _gather_finish(arrs, [axes[t] for t in idx], [split[t] for t in idx], dsems, after,
                             "gather_finish_%d" % g)
        wf.update(zip(groups[g], got))

    lb0, lb1 = hgrn_lb[0:1], hgrn_lb[1:2]

    h1 = _rmsnorm_fwd(xs, norm1_w + tok[0:1, 0:1], "norm1")
    slot = slot_arr[0]
    proj = _matmul_windows(h1, fulls[0], slot_arr, "proj_in_own")
    gather_relay(0, proj)
    gather_finish(0, proj)
    others = jnp.stack([(slot + 1) % N_CHIPS, (slot + 2) % N_CHIPS, (slot + 3) % N_CHIPS]).astype(jnp.int32)
    proj = _matmul_windows(h1, wf["w_in"], others, "proj_in", prev=proj)
    t1 = gather_relay(1, proj)
    o_h, og, states = _hgrn_fwd(proj, lb0, lb1, hgrn_norm_w + t1, "hgrn_fwd")
    gather_finish(1, o_h)
    t2 = gather_relay(2, o_h)
    sconv8 = wf["sconv8"]
    mix = _sconv_fwd(proj, sconv8 + t2, og, "sconv_fwd")
    x1 = _matmul(mix, wf["w_out"], "nn", "proj_out", residual=xs)
    h2 = _rmsnorm_fwd(x1, norm2_w, "norm2")
    gather_finish(2, h2)
    t3 = gather_relay(3, h2)
    mem_n = _rmsnorm_fwd(mems, mem_norm_w + t3, "norm_mem")
    qa = _matmul(h2, wf["wq"], "nn", "attn_q", out_dtype=BF16)
    ka = _matmul(mem_n, wf["wk"], "nn", "attn_k", out_dtype=BF16)
    va = _matmul(mem_n, wf["wv"], "nn", "attn_v", out_dtype=BF16)
    att = _attn_fwd(qa, ka, va, "attn_fwd")
    x2 = _matmul(att, wf["wo"], "nn", "attn_o", residual=x1)
    h3 = _rmsnorm_fwd(x2, norm3_w, "norm3")
    gather_finish(3, h3)
    fconv8 = wf["fconv8"]
    gate = _matmul(h3, wf["w_gate"], "nn", "ffn_gate", out_dtype=BF16)
    up = _matmul(h3, wf["w_up"], "nn", "ffn_up", out_dtype=BF16)
    z, act = _ffn_fwd(gate, up, fconv8, ffn_conv_b, "ffn_act")
    x3 = _matmul(z, wf["w_down"], "nn", "ffn_down", residual=x2)

    dx3, dx3b, g_final, loss8 = _final_loss_bwd(x3, tgt, fnw, "loss_bwd")
    gw = {}
    dz = _matmul(dx3b, wf["w_down"], "nt", "d_z", out_dtype=BF16)
    gw["w_down"] = _matmul(z, dx3b, "tn", "g_w_down", extra_bf16=True)
    dgate, du, g_fb, g_fw = _ffn_bwd(act, gate, up, dz, fconv8, "ffn_act_bwd")
    dh3 = _matmul(dgate, wf["w_gate"], "nt", "d_h3_gate")
    dh3 = _matmul(du, wf["w_up"], "nt", "d_h3_up", residual=dh3, out_dtype=BF16)
    gw["w_gate"] = _matmul(h3, dgate, "tn", "g_w_gate", extra_bf16=True)
    gw["w_up"] = _matmul(h3, du, "tn", "g_w_up", extra_bf16=True)
    pending = []

    def scatter_start(grp):
        sems, g_thru, lands, token = _scatter_start([gw[n][1] for n in grp], [big[n][1] for n in grp],
                                                    "scatter_start_" + grp[0])
        pending.append((grp, sems, g_thru, lands))
        return token[0:1, 0:1]

    tok1 = scatter_start(["w_down", "w_gate", "w_up"])
    dx2, dx2b, g_n3 = _rmsnorm_bwd(dh3, x2, norm3_w + tok1, dx3, "norm3_bwd")
    datt = _matmul(dx2b, wf["wo"], "nt", "d_att", out_dtype=BF16)
    gw["wo"] = _matmul(att, dx2b, "tn", "g_wo", extra_bf16=True)
    dqa, dka, dva = _attn_bwd(qa, ka, va, datt, "attn_bwd")
    dh2 = _matmul(dqa, wf["wq"], "nt", "d_h2", out_dtype=BF16)
    gw["wq"] = _matmul(h2, dqa, "tn", "g_wq", extra_bf16=True)
    gw["wk"] = _matmul(mem_n, dka, "tn", "g_wk", extra_bf16=True)
    gw["wv"] = _matmul(mem_n, dva, "tn", "g_wv", extra_bf16=True)
    tok2 = scatter_start(["wo", "wq", "wk", "wv"])
    dmem_n = _matmul(dka, wf["wk"], "nt", "d_memn_k")
    dmem_n = _matmul(dva, wf["wv"], "nt", "d_memn_v", residual=dmem_n)
    _, _, g_nm = _rmsnorm_bwd(dmem_n, mems, mem_norm_w, None, "norm_mem_bwd")
    dx1, dx1b, g_n2 = _rmsnorm_bwd(dh2, x1, norm2_w + tok2, dx2, "norm2_bwd")
    dmix = _matmul(dx1b, wf["w_out"], "nt", "d_mix", out_dtype=BF16)
    gw["w_out"] = _matmul(mix, dx1b, "tn", "g_w_out", extra_bf16=True)
    tok3 = scatter_start(["w_out"])
    dproj, g_lb, g_hn = _hgrn_bwd(proj, lb0, lb1, hgrn_norm_w + tok3, o_h, states, dmix, "hgrn_bwd")
    dproj, g_sw = _sconv_bwd(proj, sconv8, dmix, dproj, "sconv_bwd")
    gw["w_in"] = _matmul(h1, dproj, "tn", "g_w_in", extra_bf16=True, groups=7)
    tok4 = scatter_start(["w_in"])
    dh1 = _matmul(dproj, wf["w_in"], "nt", "d_h1", out_dtype=BF16, groups=7, tn=2048)
    dx, _, g_n1 = _rmsnorm_bwd(dh1, xs, norm1_w + tok4, dx1, "norm1_bwd")

    small = [g_n1, g_n2, g_n3, g_final, g_nm, g_lb, g_hn, g_fb,
             g_sw[0:8], g_sw[8:16], g_sw[16:24], g_fw[0:8], g_fw[8:16], g_fw[16:24], loss8]
    widths = [a.shape[1] for a in small]
    tot = _all_reduce_small(jnp.concatenate(small, axis=1), "all_reduce_small")
    offs = [0]
    for wd_ in widths:
        offs.append(offs[-1] + wd_)
    sm = [tot[:, offs[i]:offs[i + 1]] for i in range(len(small))]
    s_n1, s_n2, s_n3, s_final, s_nm, s_lb, s_hn, s_fb = sm[:8]
    s_sw = jnp.concatenate(sm[8:11], axis=0)
    s_fw = jnp.concatenate(sm[11:14], axis=0)
    loss = sm[14][0, 0]
    slot = 2 * lax.axis_index("x") + lax.axis_index("y")
    s_sw = lax.dynamic_slice_in_dim(s_sw, slot * (HGRN_W // N_CHIPS), HGRN_W // N_CHIPS, axis=1)
    fsh = ffn_conv_w.shape[2]
    s_fw = lax.dynamic_slice_in_dim(s_fw, slot * fsh, fsh, axis=1)
    s_lb2 = jnp.concatenate([s_lb, -s_lb], axis=0)

    swaps = []
    after = tot
    for grp, sems, g_thru, lands in pending:
        got = _scatter_wait(g_thru, lands, [big[n][1] for n in grp], sems, after, "scatter_wait_" + grp[0])
        sums = [_sum4(gw[n][0], big[n][1], slot_arr, r, "core_sum_" + n) for n, r in zip(grp, got)]
        ssems, s_thru, s_lands, after = _sibling_start(sums, "sibling_start_" + grp[0])
        swaps.append((grp, ssems, s_thru, s_lands))

    moments = {"hgrn_lb": (m_hgrn_lb, v_hgrn_lb), "norm1_w": (m_norm1_w, v_norm1_w), "w_in": (m_w_in, v_w_in),
               "hgrn_norm_w": (m_hgrn_norm_w, v_hgrn_norm_w), "sconv_w": (m_sconv_w, v_sconv_w),
               "w_out": (m_w_out, v_w_out), "norm2_w": (m_norm2_w, v_norm2_w),
               "mem_norm_w": (m_mem_norm_w, v_mem_norm_w), "wq": (m_wq, v_wq), "wk": (m_wk, v_wk), "wv": (m_wv, v_wv),
               "wo": (m_wo, v_wo), "norm3_w": (m_norm3_w, v_norm3_w), "w_gate": (m_w_gate, v_w_gate),
               "w_up": (m_w_up, v_w_up), "ffn_conv_w": (m_ffn_conv_w, v_ffn_conv_w),
               "ffn_conv_b": (m_ffn_conv_b, v_ffn_conv_b), "w_down": (m_w_down, v_w_down),
               "final_norm_w": (m_final_norm_w, v_final_norm_w)}
    weights = {"hgrn_lb": hgrn_lb, "norm1_w": norm1_w, "w_in": w_in, "hgrn_norm_w": hgrn_norm_w, "sconv_w": sconv_w,
               "w_out": w_out, "norm2_w": norm2_w, "mem_norm_w": mem_norm_w, "wq": wq, "wk": wk, "wv": wv, "wo": wo,
               "norm3_w": norm3_w, "w_gate": w_gate, "w_up": w_up, "ffn_conv_w": ffn_conv_w, "ffn_conv_b": ffn_conv_b,
               "w_down": w_down, "final_norm_w": final_norm_w}
    small_g = {"hgrn_lb": s_lb2, "norm1_w": s_n1, "hgrn_norm_w": s_hn, "sconv_w": s_sw, "norm2_w": s_n2,
               "mem_norm_w": s_nm, "norm3_w": s_n3, "ffn_conv_w": s_fw, "ffn_conv_b": s_fb, "final_norm_w": s_final}
    order = list(weights)
    res = {}

    def adamw(n, parts):
        shape = weights[n].shape
        w2 = weights[n].reshape((-1, shape[-1]))
        m2, v2 = (t.reshape(w2.shape) for t in moments[n])
        res[n] = [t.reshape(shape) for t in _adamw(w2, [p.reshape(w2.shape) for p in parts], m2, v2, "adamw_" + n)]

    for n in order:
        if n not in big:
            adamw(n, [small_g[n]])
    after = after + res["final_norm_w"][1][0]
    for grp, ssems, s_thru, s_lands in swaps:
        own, other = _sibling_wait(s_thru, s_lands, ssems, after, "sibling_wait_" + grp[0])
        for n, a, b in zip(grp, own, other):
            adamw(n, [a, b])
        after = res[grp[-1]][1]

    return (loss, dx[None], *[res[n][0] for n in order], *[res[n][1] for n in order],
            *[res[n][2] for n in order], *[res[n][3] for n in order])
```

```python
import jax
import jax.numpy as jnp
from jax import lax
from jax.experimental import pallas as pl
from jax.experimental.pallas import tpu as pltpu

F32 = jnp.float32
BF16 = jnp.bfloat16
MESH = pl.DeviceIdType.MESH

EPS = 1e-6
HGRN_W = 1024
HEAD = 128
N_HEADS = 8
CHUNK = 128
HGRN_UNROLL = 8
HGRN_HEADS_PER_STEP = 4
DPROJ_GROUPS = 8
MEM_HEADS = 4
MEM_HEAD_DIM = 512
N_CHIPS = 4
HALO = 8

ADAM_LR = 0.001
ADAM_B1 = 0.9
ADAM_B2 = 0.999
ADAM_EPS = 1e-08
ADAM_WD = 0.01
ADAM_STEP = 10


def _sigmoid(x):
    return 1.0 / (1.0 + jnp.exp(-x))


def _dot(a, b, dims):
    return lax.dot_general(a.astype(BF16), b.astype(BF16), (dims, ((), ())),
                           preferred_element_type=F32)


def _dot_nn(a, b):
    return _dot(a, b, ((1,), (0,)))


def _dot_nt(a, b):
    return _dot(a, b, ((1,), (1,)))


def _dot_tn(a, b):
    return _dot(a, b, ((0,), (0,)))


def _hdot(a, b, dims):
    return lax.dot_general(a, b, (dims, ((), ())), precision=lax.Precision.HIGH, preferred_element_type=F32)


def _hdot_nn(a, b):
    return _hdot(a, b, ((1,), (0,)))


def _hdot_tn(a, b):
    return _hdot(a, b, ((0,), (0,)))


def _exact_ones_dot(ones_bf16, x):
    hi = x.astype(BF16)
    lo = (x - hi.astype(F32)).astype(BF16)
    dims = (((1,), (0,)), ((), ()))
    return (lax.dot_general(ones_bf16, hi, dims, preferred_element_type=F32)
            + lax.dot_general(ones_bf16, lo, dims, preferred_element_type=F32))


def _rows8(v):
    t, c = v.shape
    return v.reshape(t // 8, 8, c).sum(axis=0)


def _shift_down(x, halo, s):
    rolled = pltpu.roll(x, s, 0)
    hrolled = pltpu.roll(halo, s, 0)
    row = lax.broadcasted_iota(jnp.int32, hrolled.shape, 0)
    head = jnp.where(row < s, hrolled, rolled[:HALO])
    return jnp.concatenate([head, rolled[HALO:]], axis=0)


def _shift_up(x, halo, s):
    t = x.shape[0]
    rolled = pltpu.roll(x, t - s, 0)
    hrolled = pltpu.roll(halo, HALO - s, 0)
    row = lax.broadcasted_iota(jnp.int32, hrolled.shape, 0)
    tail = jnp.where(row >= HALO - s, hrolled, rolled[t - HALO:])
    return jnp.concatenate([rolled[:t - HALO], tail], axis=0)


def _params(*sem):
    return pltpu.CompilerParams(dimension_semantics=sem, vmem_limit_bytes=MM_VMEM_LIMIT)


def _row_tile(r, pref):
    while r % pref:
        pref //= 2
    return pref


def _rmsnorm_fwd(x, w, name, tm=512):
    s, d = x.shape
    tm = min(tm, s)

    def body(x_ref, w_ref, o_ref):
        xv = x_ref[...]
        r = lax.rsqrt(jnp.mean(xv * xv, axis=-1, keepdims=True) + EPS)
        o_ref[...] = ((xv * r) * w_ref[...]).astype(BF16)

    return pl.pallas_call(
        body, name=name, grid=(s // tm,),
        in_specs=[pl.BlockSpec((tm, d), lambda i: (i, 0)), pl.BlockSpec((1, d), lambda i: (0, 0))],
        out_specs=pl.BlockSpec((tm, d), lambda i: (i, 0)),
        out_shape=jax.ShapeDtypeStruct((s, d), BF16),
        compiler_params=_params("parallel"),
    )(x, w)


def _rmsnorm_bwd(dh, x, w, dres, name, tm=512):
    s, d = x.shape
    tm = min(tm, s)
    has_res = dres is not None

    def body(*refs):
        if has_res:
            dh_ref, x_ref, w_ref, dres_ref, dx_ref, dxb_ref, gw_ref = refs
        else:
            dh_ref, x_ref, w_ref, dx_ref, dxb_ref, gw_ref = refs

        @pl.when(pl.program_id(0) == 0)
        def _():
            gw_ref[...] = jnp.zeros_like(gw_ref)

        xv = x_ref[...]
        dhv = dh_ref[...].astype(F32)
        r = lax.rsqrt(jnp.mean(xv * xv, axis=-1, keepdims=True) + EPS)
        xhat = xv * r
        gw_ref[...] += _rows8(dhv * xhat)
        dxh = dhv * w_ref[...]
        dx = r * (dxh - xhat * jnp.mean(dxh * xhat, axis=-1, keepdims=True))
        if has_res:
            dx = dres_ref[...] + dx
        dx_ref[...] = dx
        dxb_ref[...] = dx.astype(BF16)

    row = pl.BlockSpec((tm, d), lambda i: (i, 0))
    in_specs = [row, row, pl.BlockSpec((1, d), lambda i: (0, 0))] + ([row] if has_res else [])
    args = (dh, x, w) + ((dres,) if has_res else ())
    return pl.pallas_call(
        body, name=name, grid=(s // tm,),
        in_specs=in_specs,
        out_specs=[row, row, pl.BlockSpec((8, d), lambda i: (0, 0))],
        out_shape=[jax.ShapeDtypeStruct((s, d), F32), jax.ShapeDtypeStruct((s, d), BF16),
                   jax.ShapeDtypeStruct((8, d), F32)],
        compiler_params=_params("arbitrary"),
    )(*args)


def _final_loss_bwd(x3, target, w, name, tm=512):
    s, d = x3.shape
    tm = min(tm, s)

    def body(x_ref, t_ref, w_ref, dx_ref, dxb_ref, gw_ref, loss_ref):
        @pl.when(pl.program_id(0) == 0)
        def _():
            gw_ref[...] = jnp.zeros_like(gw_ref)
            loss_ref[...] = jnp.zeros_like(loss_ref)

        xv = x_ref[...]
        r = lax.rsqrt(jnp.mean(xv * xv, axis=-1, keepdims=True) + EPS)
        xhat = xv * r
        y = xhat * w_ref[...]
        err = y - t_ref[...]
        part = 0.5 * jnp.mean(err * err, axis=-1, keepdims=True)
        tot = jnp.sum(part, axis=0, keepdims=True)
        rr = lax.broadcasted_iota(jnp.int32, loss_ref.shape, 0)
        cc = lax.broadcasted_iota(jnp.int32, loss_ref.shape, 1)
        loss_ref[...] += jnp.where((rr == 0) & (cc == 0), tot, 0.0)
        dy = err * (1.0 / d)
        gw_ref[...] += _rows8(dy * xhat)
        dxh = dy * w_ref[...]
        dx = r * (dxh - xhat * jnp.mean(dxh * xhat, axis=-1, keepdims=True))
        dx_ref[...] = dx
        dxb_ref[...] = dx.astype(BF16)

    row = pl.BlockSpec((tm, d), lambda i: (i, 0))
    return pl.pallas_call(
        body, name=name, grid=(s // tm,),
        in_specs=[row, row, pl.BlockSpec((1, d), lambda i: (0, 0))],
        out_specs=[row, row, pl.BlockSpec((8, d), lambda i: (0, 0)), pl.BlockSpec((8, 128), lambda i: (0, 0))],
        out_shape=[jax.ShapeDtypeStruct((s, d), F32), jax.ShapeDtypeStruct((s, d), BF16),
                   jax.ShapeDtypeStruct((8, d), F32), jax.ShapeDtypeStruct((8, 128), F32)],
        compiler_params=_params("arbitrary"),
    )(x3, target, w)


MM_TILES = (1024, 1408, 512, 256, 128)
MM_K_TILES = (2816, 2048, 1792, 1408, 1024, 512, 256, 128)
MM_VMEM_LIMIT = 56 * 1024 * 1024
MM_VMEM_BUDGET = 46 * 1024 * 1024


def _pick_tile(dim):
    for t in MM_TILES:
        if dim % t == 0:
            return t
    return dim


def _matmul(a, b, mode, name, *, out_dtype=F32, residual=None, extra_bf16=False, tm=None, tn=None, tk=None,
            groups=None):
    if groups is not None and mode == "nt":
        _, m, gw = a.shape
        n, k2 = b.shape
        k, tk = groups * gw, gw
    elif groups is not None and mode == "tn":
        k, m = a.shape
        _, k2, gw = b.shape
        n, tn = groups * gw, gw
    elif mode == "nn":
        (m, k), (k2, n) = a.shape, b.shape
    elif mode == "nt":
        (m, k), (n, k2) = a.shape, b.shape
    else:
        (k, m), (k2, n) = a.shape, b.shape
    assert k == k2, (a.shape, b.shape, mode)
    auto_tm = tm is None
    tm = _pick_tile(m) if tm is None else min(tm, m)
    tn = _pick_tile(n) if tn is None else min(tn, n)
    out_elt = jnp.dtype(out_dtype).itemsize + (2 if extra_bf16 else 0) + (4 if residual is not None else 0)

    def vmem_bytes(t, rows=None):
        rows = tm if rows is None else rows
        return (2 * (rows * t * a.dtype.itemsize + t * tn * b.dtype.itemsize) + 2 * rows * tn * out_elt
                + rows * tn * 4)

    if auto_tm and tk is None and m % (2 * tm) == 0 and vmem_bytes(k, 2 * tm) <= MM_VMEM_BUDGET:
        tm = 2 * tm

    if tk is None:
        tk = next(t for t in MM_K_TILES if k % t == 0 and t <= k and vmem_bytes(t) <= MM_VMEM_BUDGET)
    assert m % tm == 0 and n % tn == 0 and k % tk == 0, (m, n, k, tm, tn, tk)
    nk = k // tk
    dims = {"nn": ((1,), (0,)), "nt": ((1,), (1,)), "tn": ((0,), (0,))}[mode]
    has_res = residual is not None

    def body(*refs):
        refs = list(refs)
        a_ref, b_ref = refs[0], refs[1]
        r_ref = refs[2] if has_res else None
        outs = refs[2 + has_res:]
        o_ref = outs[0]
        o2_ref = outs[1] if extra_bf16 else None
        def finish(r):
            if has_res:
                r = r_ref[...] + r
            o_ref[...] = r.astype(out_dtype)
            if extra_bf16:
                o2_ref[...] = r.astype(BF16)

        if nk == 1:
            finish(_dot(a_ref[...], b_ref[...], dims))
            return
        acc = outs[-1]
        kk = pl.program_id(2)

        @pl.when(kk == 0)
        def _():
            acc[...] = _dot(a_ref[...], b_ref[...], dims)

        if nk > 2:
            @pl.when((kk > 0) & (kk < nk - 1))
            def _():
                acc[...] += _dot(a_ref[...], b_ref[...], dims)

        @pl.when(kk == nk - 1)
        def _():
            finish(acc[...] + _dot(a_ref[...], b_ref[...], dims))

    if mode == "tn":
        a_spec = pl.BlockSpec((tk, tm), lambda i, j, kk: (kk, i))
    elif groups is not None:
        a_spec = pl.BlockSpec((None, tm, tk), lambda i, j, kk: (kk, i, 0))
    else:
        a_spec = pl.BlockSpec((tm, tk), lambda i, j, kk: (i, kk))
    if mode == "nt":
        b_spec = pl.BlockSpec((tn, tk), lambda i, j, kk: (j, kk))
    elif groups is not None:
        b_spec = pl.BlockSpec((None, tk, tn), lambda i, j, kk: (j, kk, 0))
    else:
        b_spec = pl.BlockSpec((tk, tn), lambda i, j, kk: (kk, j))
    o_spec = pl.BlockSpec((tm, tn), lambda i, j, kk: (i, j))
    in_specs = [a_spec, b_spec] + ([o_spec] if has_res else [])
    out_specs = [o_spec] + ([o_spec] if extra_bf16 else [])
    out_shape = [jax.ShapeDtypeStruct((m, n), out_dtype)] + ([jax.ShapeDtypeStruct((m, n), BF16)] if extra_bf16 else [])
    args = (a, b) + ((residual,) if has_res else ())
    res = pl.pallas_call(
        body, name=name, grid=(m // tm, n // tn, nk),
        in_specs=in_specs, out_specs=out_specs, out_shape=out_shape,
        scratch_shapes=[pltpu.VMEM((tm, tn) if nk > 1 else (8, 128), F32)],
        compiler_params=pltpu.CompilerParams(dimension_semantics=("parallel", "parallel", "arbitrary"),
                                             vmem_limit_bytes=MM_VMEM_LIMIT),
    )(*args)
    return res if extra_bf16 else res[0]


def _matmul_windows(h, w, windows, name, prev=None):
    s, k = h.shape
    n = w.shape[1]
    tn = n // N_CHIPS
    tm = min(s, MM_TILES[0])

    def body(win_ref, h_ref, w_ref, *rest):
        rest[-1][...] = _dot(h_ref[...], w_ref[...], ((1,), (0,))).astype(BF16)

    extra = [] if prev is None else [prev]
    return pl.pallas_call(
        body, name=name,
        grid_spec=pltpu.PrefetchScalarGridSpec(
            num_scalar_prefetch=1, grid=(s // tm, windows.shape[0]),
            in_specs=[pl.BlockSpec((tm, k), lambda i, j, win: (i, 0)),
                      pl.BlockSpec((k, tn), lambda i, j, win: (0, win[j]))]
            + [pl.BlockSpec(memory_space=pl.ANY)] * len(extra),
            out_specs=pl.BlockSpec((tm, tn), lambda i, j, win: (i, win[j]))),
        out_shape=jax.ShapeDtypeStruct((s, n), BF16),
        input_output_aliases={3: 0} if extra else {},
        compiler_params=_params("parallel", "arbitrary"),
    )(windows, h, w, *extra)


def _hgrn_gates(qp, fp, lb):
    sig = _sigmoid(fp)
    f = lb + (1.0 - lb) * sig
    logf = jnp.log(f)
    k = 1.0 - f
    sq = _sigmoid(qp)
    q = qp * sq
    return sig, f, logf, k, sq, q


def _hgrn_fwd(proj, lb0, lb1, norm_w, name, tb=1024):
    s = proj.shape[0]
    tb = min(tb, s)
    nb, ncb = s // tb, tb // CHUNK

    def body(q_ref, f_ref, i_ref, g_ref, a0_ref, a1_ref, nw_ref, o_ref, og_ref, st_ref, state):
        @pl.when(pl.program_id(1) == 0)
        def _():
            state[...] = jnp.zeros_like(state)

        lb2 = _sigmoid(a0_ref[...] - a1_ref[...])
        row = lax.broadcasted_iota(jnp.int32, (CHUNK, CHUNK), 0)
        col = lax.broadcasted_iota(jnp.int32, (CHUNK, CHUNK), 1)
        tril = row >= col
        ones_l = tril.astype(BF16)
        nw = nw_ref[...]

        def chunk(c, carry):
            rows = pl.ds(pl.multiple_of(c * CHUNK, CHUNK), CHUNK)
            for hh in range(HGRN_HEADS_PER_STEP):
                cols = slice(hh * HEAD, (hh + 1) * HEAD)
                v = i_ref[rows, cols].astype(F32)
                _, _, logf, k, _, q = _hgrn_gates(q_ref[rows, cols].astype(F32), f_ref[rows, cols].astype(F32),
                                                  lb2[:, cols])
                b = _exact_ones_dot(ones_l, logf)
                bl = jnp.sum(logf, axis=0, keepdims=True)
                bm = 0.5 * bl
                st = state[hh]
                st_ref[hh, c] = st
                qt = q * jnp.exp(b - bm)
                kt = k * jnp.exp(bm - b)
                a = jnp.where(tril, _dot_nt(qt, kt), 0.0)
                o = _dot_nt(q * jnp.exp(b), st) + _dot_nn(a, v)
                state[hh] = st * jnp.exp(bl) + _dot_tn(v, k * jnp.exp(bl - b))
                o_ref[rows, cols] = o
                on = (o * lax.rsqrt(jnp.mean(o * o, axis=-1, keepdims=True) + EPS)) * nw
                gv = g_ref[rows, cols].astype(F32)
                og_ref[rows, cols] = (on * (gv * _sigmoid(gv))).astype(BF16)
            return carry

        lax.fori_loop(0, ncb, chunk, 0, unroll=HGRN_UNROLL)

    hp, wd = HGRN_HEADS_PER_STEP, HGRN_HEADS_PER_STEP * HEAD
    ngrp = N_HEADS // hp

    def colblk(group):
        return pl.BlockSpec((tb, wd), lambda h, j: (j, group * ngrp + h))

    vec = pl.BlockSpec((1, wd), lambda h, j: (0, h))
    out_blk = pl.BlockSpec((tb, wd), lambda h, j: (j, h))
    return pl.pallas_call(
        body, name=name, grid=(ngrp, nb),
        in_specs=[colblk(0), colblk(1), colblk(2), colblk(3), vec, vec, pl.BlockSpec((1, HEAD), lambda h, j: (0, 0))],
        out_specs=[out_blk, out_blk, pl.BlockSpec((hp, ncb, HEAD, HEAD), lambda h, j: (h, j, 0, 0))],
        out_shape=[jax.ShapeDtypeStruct((s, HGRN_W), F32), jax.ShapeDtypeStruct((s, 2 * HGRN_W), BF16),
                   jax.ShapeDtypeStruct((N_HEADS, s // CHUNK, HEAD, HEAD), F32)],
        scratch_shapes=[pltpu.VMEM((hp, HEAD, HEAD), F32)],
        compiler_params=_params("parallel", "arbitrary"),
    )(proj, proj, proj, proj, lb0, lb1, norm_w)


def _hgrn_bwd(proj, lb0, lb1, norm_w, o, states, dmix, name, tb=1024):
    s = proj.shape[0]
    tb = min(tb, s)
    nb, ncb = s // tb, tb // CHUNK

    def body(q_ref, f_ref, i_ref, g_ref, a0_ref, a1_ref, nw_ref, o_ref, st_ref, dm_ref,
             dp_ref, glb_ref, gnw_ref, dstate):
        h = pl.program_id(0)

        @pl.when(pl.program_id(1) == 0)
        def _():
            dstate[...] = jnp.zeros_like(dstate)
            glb_ref[...] = jnp.zeros_like(glb_ref)

        @pl.when((pl.program_id(1) == 0) & (h == 0))
        def _():
            gnw_ref[...] = jnp.zeros_like(gnw_ref)

        lb2 = _sigmoid(a0_ref[...] - a1_ref[...])
        row = lax.broadcasted_iota(jnp.int32, (CHUNK, CHUNK), 0)
        col = lax.broadcasted_iota(jnp.int32, (CHUNK, CHUNK), 1)
        tril = row >= col
        ones_l = tril.astype(BF16)
        ones_u = (row <= col).astype(BF16)
        nw = nw_ref[...]

        def chunk(cc, carry):
            c = ncb - 1 - cc
            rows = pl.ds(pl.multiple_of(c * CHUNK, CHUNK), CHUNK)
            for hh in range(HGRN_HEADS_PER_STEP):
                cols = slice(hh * HEAD, (hh + 1) * HEAD)
                lb = lb2[:, cols]
                qp = q_ref[rows, cols].astype(F32)
                v = i_ref[rows, cols].astype(F32)
                sig, f, logf, k, sq, q = _hgrn_gates(qp, f_ref[rows, cols].astype(F32), lb)
                gv = g_ref[rows, cols].astype(F32)
                sg = _sigmoid(gv)
                silu_g = gv * sg
                dog = dm_ref[rows, cols].astype(F32)
                ov = o_ref[rows, cols]
                r = lax.rsqrt(jnp.mean(ov * ov, axis=-1, keepdims=True) + EPS)
                ohat = ov * r
                on = ohat * nw
                dp_ref[3, rows, cols] = (dog * on * (sg * (1.0 + gv * (1.0 - sg)))).astype(BF16)
                don = dog * silu_g
                gnw_ref[...] += _rows8(don * ohat)
                doh = don * nw
                do = r * (doh - ohat * jnp.mean(doh * ohat, axis=-1, keepdims=True))
                b = _exact_ones_dot(ones_l, logf)
                bl = jnp.sum(logf, axis=0, keepdims=True)
                bm = 0.5 * bl
                e_q = jnp.exp(b - bm)
                e_k = jnp.exp(bm - b)
                e_b = jnp.exp(b)
                e_l = jnp.exp(bl - b)
                qt, kt, qb, kb = q * e_q, k * e_k, q * e_b, k * e_l
                st0 = st_ref[hh, c]
                dst = dstate[hh]
                a = jnp.where(tril, _dot_nt(qt, kt), 0.0)
                da = jnp.where(tril, _dot_nt(do, v), 0.0)
                dq = _hdot_nn(da, kt) * e_q + _dot_nn(do, st0) * e_b
                dkb = _dot_nn(v, dst) * e_l
                dk = _hdot_tn(da, qt) * e_k + dkb
                dv = _dot_tn(a, do) + _dot_nt(kb, dst)
                e_bl = jnp.exp(bl)
                dstate[hh] = dst * e_bl + _dot_tn(do, qb)
                db = q * dq - k * dk
                db_last = jnp.sum(k * dkb, axis=0, keepdims=True) + e_bl * jnp.sum(st0 * dst, axis=0, keepdims=True)
                dlogf = _exact_ones_dot(ones_u, db) + db_last
                dfg = dlogf / f - dk
                dp_ref[1, rows, cols] = (dfg * (1.0 - lb) * (sig * (1.0 - sig))).astype(BF16)
                glb_ref[:, cols] += _rows8(dfg * (1.0 - sig)) * (lb * (1.0 - lb))
                dp_ref[0, rows, cols] = (dq * (sq * (1.0 + qp * (1.0 - sq)))).astype(BF16)
                dp_ref[2, rows, cols] = dv.astype(BF16)
            return carry

        lax.fori_loop(0, ncb, chunk, 0, unroll=HGRN_UNROLL)

    hp, wd = HGRN_HEADS_PER_STEP, HGRN_HEADS_PER_STEP * HEAD
    ngrp = N_HEADS // hp

    def colblk(group):
        return pl.BlockSpec((tb, wd), lambda h, j: (nb - 1 - j, group * ngrp + h))

    vec = pl.BlockSpec((1, wd), lambda h, j: (0, h))
    blk = pl.BlockSpec((tb, wd), lambda h, j: (nb - 1 - j, h))
    return pl.pallas_call(
        body, name=name, grid=(ngrp, nb),
        in_specs=[colblk(0), colblk(1), colblk(2), colblk(3), vec, vec, pl.BlockSpec((1, HEAD), lambda h, j: (0, 0)),
                  blk, pl.BlockSpec((hp, ncb, HEAD, HEAD), lambda h, j: (h, nb - 1 - j, 0, 0)), blk],
        out_specs=[pl.BlockSpec((4, tb, wd), lambda h, j: (0, nb - 1 - j, h)),
                   pl.BlockSpec((8, wd), lambda h, j: (0, h)), pl.BlockSpec((8, HEAD), lambda h, j: (0, 0))],
        out_shape=[jax.ShapeDtypeStruct((DPROJ_GROUPS, s, HGRN_W), BF16),
                   jax.ShapeDtypeStruct((8, HGRN_W), F32), jax.ShapeDtypeStruct((8, HEAD), F32)],
        scratch_shapes=[pltpu.VMEM((hp, HEAD, HEAD), F32)],
        compiler_params=_params("arbitrary", "arbitrary"),
    )(proj, proj, proj, proj, lb0, lb1, norm_w, o, states, dmix)


HALO_BLK = 16


def _f32(ref):
    return ref[...].astype(F32)


def _halo_prev(ref):
    return ref[...].astype(F32)[HALO_BLK - HALO:]


def _halo_next(ref):
    return ref[...].astype(F32)[:HALO]


def _conv3(x0, x1, x2, w_ref):
    y = x0 * w_ref[0:1, :]
    y = y + x1 * w_ref[1:2, :]
    return y + x2 * w_ref[2:3, :]


def _sconv_fwd(proj, w8, mix, name, tb=512):
    s = proj.shape[0]
    tb = min(tb, s)
    hb = tb // HALO_BLK

    def body(cb_ref, cc_ref, ch_ref, cch_ref, chh_ref, w_ref, mix_ref, y_ref):
        first = pl.program_id(0) == 0
        u = _f32(cc_ref) * _f32(ch_ref)
        uh = jnp.where(first, 0.0, _halo_prev(cch_ref) * _halo_prev(chh_ref))
        conv = _conv3(_shift_down(u, uh, 2), _shift_down(u, uh, 1), u, w_ref)
        y_ref[...] = (_f32(cb_ref) * conv).astype(BF16)

    def blk(g):
        return pl.BlockSpec((tb, HGRN_W), lambda j: (j, g))

    def halo(g):
        return pl.BlockSpec((HALO_BLK, HGRN_W), lambda j: (jnp.maximum(j * hb - 1, 0), g))

    return pl.pallas_call(
        body, name=name, grid=(s // tb,),
        in_specs=[blk(4), blk(5), blk(6), halo(5), halo(6), pl.BlockSpec((HALO, HGRN_W), lambda j: (0, 0)),
                  pl.BlockSpec(memory_space=pl.ANY)],
        out_specs=pl.BlockSpec((tb, HGRN_W), lambda j: (j, 1)),
        out_shape=jax.ShapeDtypeStruct(mix.shape, BF16),
        input_output_aliases={6: 0},
        compiler_params=_params("parallel"),
    )(proj, proj, proj, proj, proj, w8, mix)


def _sconv_bwd(proj, w8, dmix, dproj, name, tb=512):
    s = proj.shape[0]
    tb = min(tb, s)
    hb = tb // HALO_BLK
    nb = s // tb
    last_h = s // HALO_BLK - 1

    def body(cb_ref, cc_ref, ch_ref, cch_ref, chh_ref, cbn_ref, dy_ref, dyn_ref, w_ref, dproj_ref,
             dp_ref, gw_ref):
        j = pl.program_id(0)

        @pl.when(j == 0)
        def _():
            gw_ref[...] = jnp.zeros_like(gw_ref)

        cc, ch, cb = _f32(cc_ref), _f32(ch_ref), _f32(cb_ref)
        u = cc * ch
        uh = jnp.where(j == 0, 0.0, _halo_prev(cch_ref) * _halo_prev(chh_ref))
        u2, u1 = _shift_down(u, uh, 2), _shift_down(u, uh, 1)
        conv = _conv3(u2, u1, u, w_ref)
        dy = _f32(dy_ref)
        dp_ref[0] = (dy * conv).astype(BF16)
        dc = dy * cb
        dcn = jnp.where(j == nb - 1, 0.0, _halo_next(dyn_ref) * _halo_next(cbn_ref))
        gw_ref[0:8, :] += _rows8(dc * u2)
        gw_ref[8:16, :] += _rows8(dc * u1)
        gw_ref[16:24, :] += _rows8(dc * u)
        du = dc * w_ref[2:3, :] + _shift_up(dc, dcn, 1) * w_ref[1:2, :] + _shift_up(dc, dcn, 2) * w_ref[0:1, :]
        dp_ref[1] = (du * ch).astype(BF16)
        dp_ref[2] = (du * cc).astype(BF16)
        dp_ref[3] = jnp.zeros(dp_ref.shape[1:], BF16)

    def blk(g):
        return pl.BlockSpec((tb, HGRN_W), lambda j: (j, g))

    def halo_prev(g):
        return pl.BlockSpec((HALO_BLK, HGRN_W), lambda j: (jnp.maximum(j * hb - 1, 0), g))

    def halo_next(g):
        return pl.BlockSpec((HALO_BLK, HGRN_W), lambda j: (jnp.minimum((j + 1) * hb, last_h), g))

    return pl.pallas_call(
        body, name=name, grid=(nb,),
        in_specs=[blk(4), blk(5), blk(6), halo_prev(5), halo_prev(6), halo_next(4), blk(1), halo_next(1),
                  pl.BlockSpec((HALO, HGRN_W), lambda j: (0, 0)), pl.BlockSpec(memory_space=pl.ANY)],
        out_specs=[pl.BlockSpec((4, tb, HGRN_W), lambda j: (1, j, 0)), pl.BlockSpec((24, HGRN_W), lambda j: (0, 0))],
        out_shape=[jax.ShapeDtypeStruct(dproj.shape, BF16), jax.ShapeDtypeStruct((24, HGRN_W), F32)],
        input_output_aliases={9: 0},
        compiler_params=_params("arbitrary"),
    )(proj, proj, proj, proj, proj, proj, dmix, dmix, w8, dproj)


def _attn_fwd(q, kk, vv, name, tb=1024):
    s, d = q.shape
    m = kk.shape[0]
    tb = min(tb, s)
    scale = MEM_HEAD_DIM ** -0.5

    def body(q_ref, k_ref, v_ref, o_ref):
        for hh in range(MEM_HEADS):
            cols = slice(hh * MEM_HEAD_DIM, (hh + 1) * MEM_HEAD_DIM)
            sc = _dot_nt(q_ref[:, cols], k_ref[:, cols]) * scale
            sc = sc - jnp.max(sc, axis=-1, keepdims=True)
            e = jnp.exp(sc)
            p = e / jnp.sum(e, axis=-1, keepdims=True)
            o_ref[:, cols] = _dot_nn(p, v_ref[:, cols]).astype(BF16)

    full = pl.BlockSpec((m, d), lambda i: (0, 0))
    return pl.pallas_call(
        body, name=name, grid=(s // tb,),
        in_specs=[pl.BlockSpec((tb, d), lambda i: (i, 0)), full, full],
        out_specs=pl.BlockSpec((tb, d), lambda i: (i, 0)),
        out_shape=jax.ShapeDtypeStruct((s, d), BF16),
        compiler_params=pltpu.CompilerParams(dimension_semantics=("parallel",), vmem_limit_bytes=MM_VMEM_LIMIT),
    )(q, kk, vv)


def _attn_bwd(q, kk, vv, datt, name, tb=1024):
    s, d = q.shape
    m = kk.shape[0]
    tb = min(tb, s)
    scale = MEM_HEAD_DIM ** -0.5

    def body(q_ref, k_ref, v_ref, do_ref, dq_ref, dk_ref, dv_ref):
        @pl.when(pl.program_id(0) == 0)
        def _():
            dk_ref[...] = jnp.zeros_like(dk_ref)
            dv_ref[...] = jnp.zeros_like(dv_ref)

        for hh in range(MEM_HEADS):
            cols = slice(hh * MEM_HEAD_DIM, (hh + 1) * MEM_HEAD_DIM)
            qh, kh, vh, doh = q_ref[:, cols], k_ref[:, cols], v_ref[:, cols], do_ref[:, cols]
            sc = _dot_nt(qh, kh) * scale
            sc = sc - jnp.max(sc, axis=-1, keepdims=True)
            e = jnp.exp(sc)
            p = e / jnp.sum(e, axis=-1, keepdims=True)
            dp = _dot_nt(doh, vh)
            ds = p * (dp - jnp.sum(dp * p, axis=-1, keepdims=True)) * scale
            dq_ref[:, cols] = _dot_nn(ds, kh).astype(BF16)
            dk_ref[:, cols] += _dot_tn(ds, qh)
            dv_ref[:, cols] += _dot_tn(p, doh)

    full = pl.BlockSpec((m, d), lambda i: (0, 0))
    row = pl.BlockSpec((tb, d), lambda i: (i, 0))
    return pl.pallas_call(
        body, name=name, grid=(s // tb,),
        in_specs=[row, full, full, row],
        out_specs=[row, full, full],
        out_shape=[jax.ShapeDtypeStruct((s, d), BF16), jax.ShapeDtypeStruct((m, d), F32),
                   jax.ShapeDtypeStruct((m, d), F32)],
        compiler_params=pltpu.CompilerParams(dimension_semantics=("arbitrary",), vmem_limit_bytes=MM_VMEM_LIMIT),
    )(q, kk, vv, datt)


def _ffn_fwd(g, u, w8, bias, name, tb=512, tc=1408):
    s, f = g.shape
    tb = min(tb, s)
    tc = tc if f % tc == 0 else 512
    hb = tb // HALO_BLK

    def body(g_ref, gh_ref, u_ref, w_ref, b_ref, z_ref, a_ref):
        gv = _f32(g_ref)
        gh = jnp.where(pl.program_id(1) == 0, 0.0, _halo_prev(gh_ref))
        a = _conv3(_shift_down(gv, gh, 2), _shift_down(gv, gh, 1), gv, w_ref) + b_ref[...]
        a_ref[...] = a.astype(BF16)
        z_ref[...] = ((a * _sigmoid(a)) * _f32(u_ref)).astype(BF16)

    blk = pl.BlockSpec((tb, tc), lambda c, j: (j, c))
    return pl.pallas_call(
        body, name=name, grid=(f // tc, s // tb),
        in_specs=[blk, pl.BlockSpec((HALO_BLK, tc), lambda c, j: (jnp.maximum(j * hb - 1, 0), c)), blk,
                  pl.BlockSpec((HALO, tc), lambda c, j: (0, c)), pl.BlockSpec((1, tc), lambda c, j: (0, c))],
        out_specs=[blk, blk],
        out_shape=[jax.ShapeDtypeStruct((s, f), BF16), jax.ShapeDtypeStruct((s, f), BF16)],
        compiler_params=pltpu.CompilerParams(dimension_semantics=("parallel", "parallel"),
                                             vmem_limit_bytes=MM_VMEM_LIMIT),
    )(g, g, u, w8, bias)


def _ffn_bwd(a, g, u, dz, w8, name, tb=512, tc=1408):
    s, f = g.shape
    tb = min(tb, s)
    tc = tc if f % tc == 0 else 512
    nb = s // tb

    def body(a_ref, g_ref, u_ref, dz_ref, w_ref, dg_ref, du_ref, gb_ref, gw_ref, da_next):
        jj = pl.program_id(1)

        @pl.when(jj == 0)
        def _():
            gb_ref[...] = jnp.zeros_like(gb_ref)
            gw_ref[...] = jnp.zeros_like(gw_ref)
            da_next[...] = jnp.zeros_like(da_next)

        a = _f32(a_ref)
        sa = _sigmoid(a)
        dz = _f32(dz_ref)
        silu = a * sa
        du_ref[...] = (dz * silu).astype(BF16)
        da = dz * _f32(u_ref) * (sa + silu * (1.0 - sa))
        gb_ref[...] += _rows8(da)
        dan = da_next[...]
        da1, da2 = _shift_up(da, dan, 1), _shift_up(da, dan, 2)
        gv = _f32(g_ref)
        gw_ref[0:8, :] += _rows8(da2 * gv)
        gw_ref[8:16, :] += _rows8(da1 * gv)
        gw_ref[16:24, :] += _rows8(da * gv)
        dg_ref[...] = (da * w_ref[2:3, :] + da1 * w_ref[1:2, :] + da2 * w_ref[0:1, :]).astype(BF16)
        da_next[...] = da[:HALO]

    blk = pl.BlockSpec((tb, tc), lambda c, jj: (nb - 1 - jj, c))
    return pl.pallas_call(
        body, name=name, grid=(f // tc, nb),
        in_specs=[blk, blk, blk, blk, pl.BlockSpec((HALO, tc), lambda c, jj: (0, c))],
        out_specs=[blk, blk, pl.BlockSpec((8, tc), lambda c, jj: (0, c)), pl.BlockSpec((24, tc), lambda c, jj: (0, c))],
        out_shape=[jax.ShapeDtypeStruct((s, f), BF16), jax.ShapeDtypeStruct((s, f), BF16),
                   jax.ShapeDtypeStruct((8, f), F32), jax.ShapeDtypeStruct((24, f), F32)],
        scratch_shapes=[pltpu.VMEM((HALO, tc), F32)],
        compiler_params=pltpu.CompilerParams(dimension_semantics=("parallel", "arbitrary"),
                                             vmem_limit_bytes=MM_VMEM_LIMIT),
    )(a, g, u, dz, w8)


def _window(ref, axis, slot, size):
    start = pl.multiple_of(slot * size, size)
    if axis == 0:
        return ref.at[pl.ds(start, size), :]
    return ref.at[:, pl.ds(start, size)]


def _chip_peers():
    x, y, c = lax.axis_index("x"), lax.axis_index("y"), lax.axis_index("c")
    peers = [(1 - x, y, c), (x, 1 - y, c), (1 - x, 1 - y, c)]
    slots = [2 * (1 - x) + y, 2 * x + (1 - y), 2 * (1 - x) + (1 - y)]
    return 2 * x + y, peers, slots


HBM_SPEC = pl.BlockSpec(memory_space=pltpu.HBM)
SEM_SPEC = pl.BlockSpec(memory_space=pltpu.SEMAPHORE)
EFFECT = pltpu.SideEffectType.DATAFLOW_SIDE_EFFECTING


def _hbm(a):
    return pltpu.with_memory_space_constraint(a, pltpu.HBM)


def _cast_into_full(x, axis, slot_arr, dtype, name, after=None):
    r, c = x.shape
    tr = _row_tile(r, 512)
    nb = r // tr
    full = (r * N_CHIPS, c) if axis == 0 else (r, c * N_CHIPS)

    def body(slot_ref, x_ref, *rest):
        rest[-1][...] = x_ref[...].astype(dtype)

    if axis == 0:
        out_map = lambda i, s: (s[0] * nb + i, 0)
    else:
        out_map = lambda i, s: (i, s[0])
    extra = [] if after is None else [after]
    return pl.pallas_call(
        body, name=name,
        grid_spec=pltpu.PrefetchScalarGridSpec(
            num_scalar_prefetch=1, grid=(nb,),
            in_specs=[pl.BlockSpec((tr, c), lambda i, s: (i, 0))] + [pl.BlockSpec(memory_space=pl.ANY)] * len(extra),
            out_specs=pl.BlockSpec((tr, c), out_map)),
        out_shape=jax.ShapeDtypeStruct(full, dtype),
        compiler_params=_params("parallel"),
    )(slot_arr, x, *extra)


def _piece(ref, axis, slot, half):
    size = ref.shape[axis] // N_CHIPS
    if half is None:
        return _window(ref, axis, slot, size)
    if axis == 0:
        h = size // 2
        return ref.at[pl.ds(pl.multiple_of(slot * size + half * h, h), h), :]
    h = ref.shape[0] // 2
    return ref.at[pl.ds(pl.multiple_of(half * h, h), h), pl.ds(pl.multiple_of(slot * size, size), size)]


def _gather_start(fulls, axes, split, groups, name):
    n, ng = len(fulls), len(groups)

    def body(*refs):
        outs = refs[n:]
        sems = outs[:2 * ng]
        thru = outs[2 * ng:2 * ng + n]
        token = outs[-1]
        slot, peers, _ = _chip_peers()
        c = lax.axis_index("c")
        for g, members in enumerate(groups):
            for i, t in enumerate(members):
                mine = _piece(thru[t], axes[t], slot, c if split[t] else None)
                for k in range(3):
                    pltpu.make_async_remote_copy(
                        src_ref=mine, dst_ref=mine, send_sem=sems[2 * g].at[3 * i + k],
                        recv_sem=sems[2 * g + 1].at[3 * i + k], device_id=peers[k], device_id_type=MESH).start()
        token[...] = jnp.zeros_like(token)

    sem_shapes = []
    for members in groups:
        sem_shapes += [pltpu.SemaphoreType.DMA((3 * len(members),))] * 2
    res = pl.pallas_call(
        body, name=name,
        in_specs=[HBM_SPEC] * n,
        out_specs=[SEM_SPEC] * (2 * ng) + [HBM_SPEC] * n + [pl.BlockSpec(memory_space=pltpu.VMEM)],
        out_shape=sem_shapes + [pltpu.HBM(f.shape, f.dtype) for f in fulls] + [jax.ShapeDtypeStruct((8, 128), F32)],
        input_output_aliases={t: 2 * ng + t for t in range(n)},
        compiler_params=pltpu.CompilerParams(has_side_effects=EFFECT),
    )(*[_hbm(f) for f in fulls])
    sems = [(res[2 * g], res[2 * g + 1]) for g in range(ng)]
    return sems, list(res[2 * ng:2 * ng + n]), res[-1]


def _gather_relay(fulls, axes, split, sems, after, name):
    n = len(fulls)
    nsplit = sum(split)

    def body(*refs):
        send_sems, recv_sems = refs[n], refs[n + 1]
        outs = refs[n + 3:]
        d_send, d_recv = outs[0], outs[1]
        thru = outs[2:2 + n]
        token = outs[-1]
        slot, peers, slots = _chip_peers()
        c = lax.axis_index("c")
        sibling = (lax.axis_index("x"), lax.axis_index("y"), 1 - c)
        for t in range(n):
            half = c if split[t] else None
            for k in range(3):
                cp = pltpu.make_async_remote_copy(
                    src_ref=_piece(thru[t], axes[t], slot, half), dst_ref=_piece(thru[t], axes[t], slots[k], half),
                    send_sem=send_sems.at[3 * t + k], recv_sem=recv_sems.at[3 * t + k],
                    device_id=peers[k], device_id_type=MESH)
                cp.wait_send()
                cp.wait_recv()
        i = 0
        for t in range(n):
            if not split[t]:
                continue
            for k in range(3):
                got = _piece(thru[t], axes[t], slots[k], c)
                pltpu.make_async_remote_copy(
                    src_ref=got, dst_ref=got, send_sem=d_send.at[3 * i + k], recv_sem=d_recv.at[3 * i + k],
                    device_id=sibling, device_id_type=MESH).start()
            i += 1
        token[...] = jnp.zeros_like(token)

    res = pl.pallas_call(
        body, name=name,
        in_specs=[HBM_SPEC] * n + [SEM_SPEC, SEM_SPEC, pl.BlockSpec(memory_space=pl.ANY)],
        out_specs=[SEM_SPEC, SEM_SPEC] + [HBM_SPEC] * n + [pl.BlockSpec(memory_space=pltpu.VMEM)],
        out_shape=[pltpu.SemaphoreType.DMA((3 * nsplit,)), pltpu.SemaphoreType.DMA((3 * nsplit,))]
        + [pltpu.HBM(f.shape, f.dtype) for f in fulls] + [jax.ShapeDtypeStruct((8, 128), F32)],
        input_output_aliases={t: 2 + t for t in range(n)},
        compiler_params=pltpu.CompilerParams(has_side_effects=EFFECT),
    )(*fulls, sems[0], sems[1], after)
    return (res[0], res[1]), list(res[2:2 + n]), res[-1]


def _gather_finish(fulls, axes, split, sems, after, name):
    n = len(fulls)

    def body(*refs):
        d_send, d_recv = refs[n], refs[n + 1]
        thru = refs[n + 3:]
        _, _, slots = _chip_peers()
        c = lax.axis_index("c")
        sibling = (lax.axis_index("x"), lax.axis_index("y"), 1 - c)
        i = 0
        for t in range(n):
            if not split[t]:
                continue
            for k in range(3):
                cp = pltpu.make_async_remote_copy(
                    src_ref=_piece(thru[t], axes[t], slots[k], c), dst_ref=_piece(thru[t], axes[t], slots[k], 1 - c),
                    send_sem=d_send.at[3 * i + k], recv_sem=d_recv.at[3 * i + k],
                    device_id=sibling, device_id_type=MESH)
                cp.wait_send()
                cp.wait_recv()
            i += 1

    return pl.pallas_call(
        body, name=name,
        in_specs=[HBM_SPEC] * n + [SEM_SPEC, SEM_SPEC, pl.BlockSpec(memory_space=pl.ANY)],
        out_specs=[HBM_SPEC] * n,
        out_shape=[pltpu.HBM(f.shape, f.dtype) for f in fulls],
        input_output_aliases={t: t for t in range(n)},
        compiler_params=pltpu.CompilerParams(has_side_effects=EFFECT),
    )(*fulls, sems[0], sems[1], after)


def _scatter_start(grads_bf16, axes, name):
    n = len(grads_bf16)

    def shard_shape(g, ax):
        return (g.shape[0] // N_CHIPS, g.shape[1]) if ax == 0 else (g.shape[0], g.shape[1] // N_CHIPS)

    shapes = [shard_shape(g, ax) for g, ax in zip(grads_bf16, axes)]

    def body(*refs):
        outs = refs[2 * n:]
        send_sems, recv_sems = outs[0], outs[1]
        gb, land = outs[2:2 + n], outs[2 + n:2 + 2 * n]
        token = outs[-1]
        _, peers, slots = _chip_peers()
        for t in range(n):
            size = shapes[t][axes[t]]
            for k in range(3):
                pltpu.make_async_remote_copy(
                    src_ref=_window(gb[t], axes[t], slots[k], size), dst_ref=land[t].at[k],
                    send_sem=send_sems.at[3 * t + k], recv_sem=recv_sems.at[3 * t + k],
                    device_id=peers[k], device_id_type=MESH).start()
        token[...] = jnp.zeros_like(token)

    lands = [_hbm(lax.empty((3,) + sh, BF16)) for sh in shapes]
    res = pl.pallas_call(
        body, name=name,
        in_specs=[HBM_SPEC] * (2 * n),
        out_specs=[SEM_SPEC, SEM_SPEC] + [HBM_SPEC] * (2 * n) + [pl.BlockSpec(memory_space=pltpu.VMEM)],
        out_shape=[pltpu.SemaphoreType.DMA((3 * n,)), pltpu.SemaphoreType.DMA((3 * n,))]
        + [pltpu.HBM(g.shape, g.dtype) for g in grads_bf16] + [pltpu.HBM((3,) + sh, BF16) for sh in shapes]
        + [jax.ShapeDtypeStruct((8, 128), F32)],
        input_output_aliases={t: 2 + t for t in range(2 * n)},
        compiler_params=pltpu.CompilerParams(has_side_effects=EFFECT),
    )(*[_hbm(g) for g in grads_bf16], *lands)
    return (res[0], res[1]), list(res[2:2 + n]), list(res[2 + n:2 + 2 * n]), res[-1]


def _scatter_wait(grads_thru, lands_thru, axes, sems, after, name):
    n = len(grads_thru)

    def body(*refs):
        send_sems, recv_sems = refs[2 * n], refs[2 * n + 1]
        outs = refs[2 * n + 3:]
        gb, land = outs[:n], outs[n:]
        _, peers, slots = _chip_peers()
        for t in range(n):
            size = land[t].shape[1 + axes[t]]
            for k in range(3):
                cp = pltpu.make_async_remote_copy(
                    src_ref=_window(gb[t], axes[t], slots[k], size), dst_ref=land[t].at[k],
                    send_sem=send_sems.at[3 * t + k], recv_sem=recv_sems.at[3 * t + k],
                    device_id=peers[k], device_id_type=MESH)
                cp.wait_send()
                cp.wait_recv()

    res = pl.pallas_call(
        body, name=name,
        in_specs=[HBM_SPEC] * (2 * n) + [SEM_SPEC, SEM_SPEC, pl.BlockSpec(memory_space=pl.ANY)],
        out_specs=[HBM_SPEC] * (2 * n),
        out_shape=[pltpu.HBM(g.shape, g.dtype) for g in grads_thru] + [pltpu.HBM(l.shape, l.dtype) for l in lands_thru],
        input_output_aliases={t: t for t in range(2 * n)},
        compiler_params=pltpu.CompilerParams(has_side_effects=EFFECT),
    )(*grads_thru, *lands_thru, sems[0], sems[1], after)
    return list(res[n:])


def _sibling_start(arrs, name):
    n = len(arrs)

    def body(*refs):
        outs = refs[2 * n:]
        send_sems, recv_sems = outs[0], outs[1]
        src, land = outs[2:2 + n], outs[2 + n:2 + 2 * n]
        token = outs[-1]
        sibling = (lax.axis_index("x"), lax.axis_index("y"), 1 - lax.axis_index("c"))
        for t in range(n):
            pltpu.make_async_remote_copy(
                src_ref=src[t], dst_ref=land[t], send_sem=send_sems.at[t], recv_sem=recv_sems.at[t],
                device_id=sibling, device_id_type=MESH).start()
        token[...] = jnp.zeros_like(token)

    lands = [_hbm(lax.empty(a.shape, a.dtype)) for a in arrs]
    res = pl.pallas_call(
        body, name=name,
        in_specs=[HBM_SPEC] * (2 * n),
        out_specs=[SEM_SPEC, SEM_SPEC] + [HBM_SPEC] * (2 * n) + [pl.BlockSpec(memory_space=pltpu.VMEM)],
        out_shape=[pltpu.SemaphoreType.DMA((n,)), pltpu.SemaphoreType.DMA((n,))]
        + [pltpu.HBM(a.shape, a.dtype) for a in arrs] * 2 + [jax.ShapeDtypeStruct((8, 128), F32)],
        input_output_aliases={t: 2 + t for t in range(2 * n)},
        compiler_params=pltpu.CompilerParams(has_side_effects=EFFECT),
    )(*[_hbm(a) for a in arrs], *lands)
    return (res[0], res[1]), list(res[2:2 + n]), list(res[2 + n:2 + 2 * n]), res[-1]


def _sibling_wait(src_thru, lands_thru, sems, after, name):
    n = len(src_thru)

    def body(*refs):
        send_sems, recv_sems = refs[2 * n], refs[2 * n + 1]
        outs = refs[2 * n + 3:]
        src, land = outs[:n], outs[n:]
        sibling = (lax.axis_index("x"), lax.axis_index("y"), 1 - lax.axis_index("c"))
        for t in range(n):
            cp = pltpu.make_async_remote_copy(
                src_ref=src[t], dst_ref=land[t], send_sem=send_sems.at[t], recv_sem=recv_sems.at[t],
                device_id=sibling, device_id_type=MESH)
            cp.wait_send()
            cp.wait_recv()

    res = pl.pallas_call(
        body, name=name,
        in_specs=[HBM_SPEC] * (2 * n) + [SEM_SPEC, SEM_SPEC, pl.BlockSpec(memory_space=pl.ANY)],
        out_specs=[HBM_SPEC] * (2 * n),
        out_shape=[pltpu.HBM(a.shape, a.dtype) for a in src_thru] * 2,
        input_output_aliases={t: t for t in range(2 * n)},
        compiler_params=pltpu.CompilerParams(has_side_effects=EFFECT),
    )(*src_thru, *lands_thru, sems[0], sems[1], after)
    return list(res[:n]), list(res[n:])


def _all_reduce_small(packed, name):
    nc = packed.shape[1]
    vmem = pl.BlockSpec(memory_space=pltpu.VMEM)

    def body(in_ref, out_ref, gbuf, send_sems, recv_sems):
        x, y, c = lax.axis_index("x"), lax.axis_index("y"), lax.axis_index("c")
        me = 4 * x + 2 * y + c
        gbuf[me] = jnp.sum(in_ref[...], axis=0, keepdims=True)
        copies = []
        for k in range(1, 8):
            peer = (x ^ ((k >> 2) & 1), y ^ ((k >> 1) & 1), c ^ (k & 1))
            rc = pltpu.make_async_remote_copy(
                src_ref=gbuf.at[me], dst_ref=gbuf.at[me], send_sem=send_sems.at[k - 1], recv_sem=recv_sems.at[k - 1],
                device_id=peer, device_id_type=MESH)
            rc.start()
            copies.append(rc)
        for k in range(1, 8):
            peer = (x ^ ((k >> 2) & 1), y ^ ((k >> 1) & 1), c ^ (k & 1))
            pltpu.make_async_remote_copy(
                src_ref=gbuf.at[me], dst_ref=gbuf.at[me ^ k], send_sem=send_sems.at[k - 1],
                recv_sem=recv_sems.at[k - 1], device_id=peer, device_id_type=MESH).wait_recv()
        for rc in copies:
            rc.wait_send()
        tot = gbuf[0]
        for d in range(1, 8):
            tot = tot + gbuf[d]
        out_ref[...] = tot

    return pl.pallas_call(
        body, name=name,
        in_specs=[vmem], out_specs=vmem,
        out_shape=jax.ShapeDtypeStruct((1, nc), F32),
        scratch_shapes=[pltpu.VMEM((8, 1, nc), F32), pltpu.SemaphoreType.DMA((7,)), pltpu.SemaphoreType.DMA((7,))],
    )(packed)


def _sum4(g_full, axis, slot_arr, recv, name):
    _, r, c = recv.shape
    tr = _row_tile(r, 512)
    nb = r // tr

    def body(slot_ref, own_ref, recv_ref, o_ref):
        acc = own_ref[...]
        for k in range(3):
            acc = acc + recv_ref[k].astype(F32)
        o_ref[...] = acc

    if axis == 0:
        own_map = lambda i, s: (s[0] * nb + i, 0)
    else:
        own_map = lambda i, s: (i, s[0])
    return pl.pallas_call(
        body, name=name,
        grid_spec=pltpu.PrefetchScalarGridSpec(
            num_scalar_prefetch=1, grid=(nb,),
            in_specs=[pl.BlockSpec((tr, c), own_map), pl.BlockSpec((3, tr, c), lambda i, s: (0, i, 0))],
            out_specs=pl.BlockSpec((tr, c), lambda i, s: (i, 0))),
        out_shape=jax.ShapeDtypeStruct((r, c), F32),
        compiler_params=pltpu.CompilerParams(dimension_semantics=("parallel",), vmem_limit_bytes=MM_VMEM_LIMIT),
    )(slot_arr, g_full, recv)


def _adamw(w, g_parts, m, v, name):
    r, c = w.shape
    tr = r if r % 128 else _row_tile(r, 256)
    npart = len(g_parts)

    def body(*refs):
        w_ref = refs[0]
        g_refs = refs[1:1 + npart]
        m_ref, v_ref, g_out, d_out, m_out, v_out = refs[1 + npart:]
        g = g_refs[0][...]
        for gr in g_refs[1:]:
            g = g + gr[...]
        mm = ADAM_B1 * m_ref[...] + (1.0 - ADAM_B1) * g
        vv = ADAM_B2 * v_ref[...] + (1.0 - ADAM_B2) * (g * g)
        m_hat = mm / (1.0 - ADAM_B1 ** ADAM_STEP)
        v_hat = vv / (1.0 - ADAM_B2 ** ADAM_STEP)
        g_out[...] = g
        d_out[...] = -ADAM_LR * (m_hat / (jnp.sqrt(v_hat) + ADAM_EPS) + ADAM_WD * w_ref[...])
        m_out[...] = mm
        v_out[...] = vv

    blk = pl.BlockSpec((tr, c), lambda i: (i, 0))
    shp = jax.ShapeDtypeStruct((r, c), F32)
    return pl.pallas_call(
        body, name=name, grid=(r // tr,),
        in_specs=[blk] * (3 + npart), out_specs=[blk] * 4, out_shape=[shp] * 4,
        compiler_params=pltpu.CompilerParams(dimension_semantics=("parallel",), vmem_limit_bytes=MM_VMEM_LIMIT),
    )(w, *g_parts, m, v)


def _pad_rows8(w):
    return jnp.pad(w, ((0, HALO - w.shape[0]), (0, 0)))


def kernel(x, mem, hgrn_lb, norm1_w, w_in, hgrn_norm_w, sconv_w, w_out, norm2_w, mem_norm_w, wq, wk, wv, wo, norm3_w, w_gate, w_up, ffn_conv_w, ffn_conv_b, w_down, final_norm_w, loss_target, m_hgrn_lb, m_norm1_w, m_w_in, m_hgrn_norm_w, m_sconv_w, m_w_out, m_norm2_w, m_mem_norm_w, m_wq, m_wk, m_wv, m_wo, m_norm3_w, m_w_gate, m_w_up, m_ffn_conv_w, m_ffn_conv_b, m_w_down, m_final_norm_w, v_hgrn_lb, v_norm1_w, v_w_in, v_hgrn_norm_w, v_sconv_w, v_w_out, v_norm2_w, v_mem_norm_w, v_wq, v_wk, v_wv, v_wo, v_norm3_w, v_w_gate, v_w_up, v_ffn_conv_w, v_ffn_conv_b, v_w_down, v_final_norm_w):
    xs, mems, tgt = x[0], mem[0], loss_target[0]
    d = xs.shape[1]
    fnw = final_norm_w.reshape(1, d)

    big = {"w_in": (w_in[0], 1), "w_out": (w_out[0], 0), "wq": (wq[0], 0), "wk": (wk[0], 0), "wv": (wv[0], 0),
           "wo": (wo[0], 0), "w_gate": (w_gate[0], 1), "w_up": (w_up[0], 1), "w_down": (w_down[0], 0)}
    names = list(big)
    slot_arr = (2 * lax.axis_index("x") + lax.axis_index("y")).astype(jnp.int32).reshape(1)
    gnames = names + ["sconv8", "fconv8"]
    axes = [big[n][1] for n in names] + [1, 1]
    groups = [["w_in"], ["w_out", "sconv8"], ["wq", "wk", "wv", "wo"], ["w_gate", "w_up", "fconv8", "w_down"]]
    gidx = [[gnames.index(n) for n in grp] for grp in groups]
    split = [True] * len(names) + [False, False]
    first = _cast_into_full(big["w_in"][0], 1, slot_arr, BF16, "cast_w_in")
    sems0, first, tok0 = _gather_start([first], [1], [True], [[0]], "gather_start_w_in")
    rest = [_cast_into_full(big[n][0], big[n][1], slot_arr, BF16, "cast_" + n, after=tok0) for n in names[1:]]
    rest += [_cast_into_full(_pad_rows8(sconv_w[0]), 1, slot_arr, F32, "cast_sconv_w", after=tok0),
             _cast_into_full(_pad_rows8(ffn_conv_w[0]), 1, slot_arr, F32, "cast_ffn_conv_w", after=tok0)]
    sems1, rest, tok = _gather_start(rest, axes[1:], split[1:], [[t - 1 for t in idx] for idx in gidx[1:]],
                                     "gather_start")
    gsems, fulls = sems0 + sems1, first + rest
    wf, relayed = {}, {}

    def gather_relay(g, after):
        idx = gidx[g]
        dsems, arrs, token = _gather_relay([fulls[t] for t in idx], [axes[t] for t in idx], [split[t] for t in idx],
                                           gsems[g], after, "gather_relay_%d" % g)
        relayed[g] = (dsems, arrs)
        return token[0:1, 0:1]

    def gather_finish(g, after):
        idx = gidx[g]
        dsems, arrs = relayed[g]
        got = _gather_finish(arrs, [axes[t] for t in idx], [split[t] for t in idx], dsems, after,
                             "gather_finish_%d" % g)
        wf.update(zip(groups[g], got))

    lb0, lb1 = hgrn_lb[0:1], hgrn_lb[1:2]

    h1 = _rmsnorm_fwd(xs, norm1_w + tok[0:1, 0:1], "norm1")
    slot = slot_arr[0]
    proj = _matmul_windows(h1, fulls[0], slot_arr, "proj_in_own")
    gather_relay(0, proj)
    gather_finish(0, proj)
    others = jnp.stack([(slot + 1) % N_CHIPS, (slot + 2) % N_CHIPS, (slot + 3) % N_CHIPS]).astype(jnp.int32)
    proj = _matmul_windows(h1, wf["w_in"], others, "proj_in", prev=proj)
    t1 = gather_relay(1, proj)
    o_h, og, states = _hgrn_fwd(proj, lb0, lb1, hgrn_norm_w + t1, "hgrn_fwd")
    gather_finish(1, o_h)
    t2 = gather_relay(2, o_h)
    sconv8 = wf["sconv8"]
    mix = _sconv_fwd(proj, sconv8 + t2, og, "sconv_fwd")
    x1 = _matmul(mix, wf["w_out"], "nn", "proj_out", residual=xs)
    h2 = _rmsnorm_fwd(x1, norm2_w, "norm2")
    gather_finish(2, h2)
    t3 = gather_relay(3, h2)
    mem_n = _rmsnorm_fwd(mems, mem_norm_w + t3, "norm_mem")
    qa = _matmul(h2, wf["wq"], "nn", "attn_q", out_dtype=BF16)
    ka = _matmul(mem_n, wf["wk"], "nn", "attn_k", out_dtype=BF16)
    va = _matmul(mem_n, wf["wv"], "nn", "attn_v", out_dtype=BF16)
    att = _attn_fwd(qa, ka, va, "attn_fwd")
    x2 = _matmul(att, wf["wo"], "nn", "attn_o", residual=x1)
    h3 = _rmsnorm_fwd(x2, norm3_w, "norm3")
    gather_finish(3, h3)
    fconv8 = wf["fconv8"]
    gate = _matmul(h3, wf["w_gate"], "nn", "ffn_gate", out_dtype=BF16)
    up = _matmul(h3, wf["w_up"], "nn", "ffn_up", out_dtype=BF16)
    z, act = _ffn_fwd(gate, up, fconv8, ffn_conv_b, "ffn_act")
    x3 = _matmul(z, wf["w_down"], "nn", "ffn_down", residual=x2)

    dx3, dx3b, g_final, loss8 = _final_loss_bwd(x3, tgt, fnw, "loss_bwd")
    gw = {}
    dz = _matmul(dx3b, wf["w_down"], "nt", "d_z", out_dtype=BF16)
    gw["w_down"] = _matmul(z, dx3b, "tn", "g_w_down", extra_bf16=True)
    dgate, du, g_fb, g_fw = _ffn_bwd(act, gate, up, dz, fconv8, "ffn_act_bwd")
    dh3 = _matmul(dgate, wf["w_gate"], "nt", "d_h3_gate")
    dh3 = _matmul(du, wf["w_up"], "nt", "d_h3_up", residual=dh3, out_dtype=BF16)
    gw["w_gate"] = _matmul(h3, dgate, "tn", "g_w_gate", extra_bf16=True)
    gw["w_up"] = _matmul(h3, du, "tn", "g_w_up", extra_bf16=True)
    pending = []

    def scatter_start(grp):
        sems, g_thru, lands, token = _scatter_start([gw[n][1] for n in grp], [big[n][1] for n in grp],
                                                    "scatter_start_" + grp[0])
        pending.append((grp, sems, g_thru, lands))
        return token[0:1, 0:1]

    tok1 = scatter_start(["w_down", "w_gate", "w_up"])
    dx2, dx2b, g_n3 = _rmsnorm_bwd(dh3, x2, norm3_w + tok1, dx3, "norm3_bwd")
    datt = _matmul(dx2b, wf["wo"], "nt", "d_att", out_dtype=BF16)
    gw["wo"] = _matmul(att, dx2b, "tn", "g_wo", extra_bf16=True)
    dqa, dka, dva = _attn_bwd(qa, ka, va, datt, "attn_bwd")
    dh2 = _matmul(dqa, wf["wq"], "nt", "d_h2", out_dtype=BF16)
    gw["wq"] = _matmul(h2, dqa, "tn", "g_wq", extra_bf16=True)
    gw["wk"] = _matmul(mem_n, dka, "tn", "g_wk", extra_bf16=True)
    gw["wv"] = _matmul(mem_n, dva, "tn", "g_wv", extra_bf16=True)
    tok2 = scatter_start(["wo", "wq", "wk", "wv"])
    dmem_n = _matmul(dka, wf["wk"], "nt", "d_memn_k")
    dmem_n = _matmul(dva, wf["wv"], "nt", "d_memn_v", residual=dmem_n)
    _, _, g_nm = _rmsnorm_bwd(dmem_n, mems, mem_norm_w, None, "norm_mem_bwd")
    dx1, dx1b, g_n2 = _rmsnorm_bwd(dh2, x1, norm2_w + tok2, dx2, "norm2_bwd")
    dmix = _matmul(dx1b, wf["w_out"], "nt", "d_mix", out_dtype=BF16)
    gw["w_out"] = _matmul(mix, dx1b, "tn", "g_w_out", extra_bf16=True)
    tok3 = scatter_start(["w_out"])
    dproj, g_lb, g_hn = _hgrn_bwd(proj, lb0, lb1, hgrn_norm_w + tok3, o_h, states, dmix, "hgrn_bwd")
    dproj, g_sw = _sconv_bwd(proj, sconv8, dmix, dproj, "sconv_bwd")
    gw["w_in"] = _matmul(h1, dproj, "tn", "g_w_in", extra_bf16=True, groups=7)
    tok4 = scatter_start(["w_in"])
    dh1 = _matmul(dproj, wf["w_in"], "nt", "d_h1", out_dtype=BF16, groups=7, tn=2048)
    dx, _, g_n1 = _rmsnorm_bwd(dh1, xs, norm1_w + tok4, dx1, "norm1_bwd")

    small = [g_n1, g_n2, g_n3, g_final, g_nm, g_lb, g_hn, g_fb,
             g_sw[0:8], g_sw[8:16], g_sw[16:24], g_fw[0:8], g_fw[8:16], g_fw[16:24], loss8]
    widths = [a.shape[1] for a in small]
    tot = _all_reduce_small(jnp.concatenate(small, axis=1), "all_reduce_small")
    offs = [0]
    for wd_ in widths:
        offs.append(offs[-1] + wd_)
    sm = [tot[:, offs[i]:offs[i + 1]] for i in range(len(small))]
    s_n1, s_n2, s_n3, s_final, s_nm, s_lb, s_hn, s_fb = sm[:8]
    s_sw = jnp.concatenate(sm[8:11], axis=0)
    s_fw = jnp.concatenate(sm[11:14], axis=0)
    loss = sm[14][0, 0]
    slot = 2 * lax.axis_index("x") + lax.axis_index("y")
    s_sw = lax.dynamic_slice_in_dim(s_sw, slot * (HGRN_W // N_CHIPS), HGRN_W // N_CHIPS, axis=1)
    fsh = ffn_conv_w.shape[2]
    s_fw = lax.dynamic_slice_in_dim(s_fw, slot * fsh, fsh, axis=1)
    s_lb2 = jnp.concatenate([s_lb, -s_lb], axis=0)

    swaps = []
    after = tot
    for grp, sems, g_thru, lands in pending:
        got = _scatter_wait(g_thru, lands, [big[n][1] for n in grp], sems, after, "scatter_wait_" + grp[0])
        sums = [_sum4(gw[n][0], big[n][1], slot_arr, r, "core_sum_" + n) for n, r in zip(grp, got)]
        ssems, s_thru, s_lands, after = _sibling_start(sums, "sibling_start_" + grp[0])
        swaps.append((grp, ssems, s_thru, s_lands))

    moments = {"hgrn_lb": (m_hgrn_lb, v_hgrn_lb), "norm1_w": (m_norm1_w, v_norm1_w), "w_in": (m_w_in, v_w_in),
               "hgrn_norm_w": (m_hgrn_norm_w, v_hgrn_norm_w), "sconv_w": (m_sconv_w, v_sconv_w),
               "w_out": (m_w_out, v_w_out), "norm2_w": (m_norm2_w, v_norm2_w),
               "mem_norm_w": (m_mem_norm_w, v_mem_norm_w), "wq": (m_wq, v_wq), "wk": (m_wk, v_wk), "wv": (m_wv, v_wv),
               "wo": (m_wo, v_wo), "norm3_w": (m_norm3_w, v_norm3_w), "w_gate": (m_w_gate, v_w_gate),
               "w_up": (m_w_up, v_w_up), "ffn_conv_w": (m_ffn_conv_w, v_ffn_conv_w),
               "ffn_conv_b": (m_ffn_conv_b, v_ffn_conv_b), "w_down": (m_w_down, v_w_down),
               "final_norm_w": (m_final_norm_w, v_final_norm_w)}
    weights = {"hgrn_lb": hgrn_lb, "norm1_w": norm1_w, "w_in": w_in, "hgrn_norm_w": hgrn_norm_w, "sconv_w": sconv_w,
               "w_out": w_out, "norm2_w": norm2_w, "mem_norm_w": mem_norm_w, "wq": wq, "wk": wk, "wv": wv, "wo": wo,
               "norm3_w": norm3_w, "w_gate": w_gate, "w_up": w_up, "ffn_conv_w": ffn_conv_w, "ffn_conv_b": ffn_conv_b,
               "w_down": w_down, "final_norm_w": final_norm_w}
    small_g = {"hgrn_lb": s_lb2, "norm1_w": s_n1, "hgrn_norm_w": s_hn, "sconv_w": s_sw, "norm2_w": s_n2,
               "mem_norm_w": s_nm, "norm3_w": s_n3, "ffn_conv_w": s_fw, "ffn_conv_b": s_fb, "final_norm_w": s_final}
    order = list(weights)
    res = {}

    def adamw(n, parts):
        shape = weights[n].shape
        w2 = weights[n].reshape((-1, shape[-1]))
        m2, v2 = (t.reshape(w2.shape) for t in moments[n])
        res[n] = [t.reshape(shape) for t in _adamw(w2, [p.reshape(w2.shape) for p in parts], m2, v2, "adamw_" + n)]

    for n in order:
        if n not in big:
            adamw(n, [small_g[n]])
    after = after + res["final_norm_w"][1][0]
    for grp, ssems, s_thru, s_lands in swaps:
        own, other = _sibling_wait(s_thru, s_lands, ssems, after, "sibling_wait_" + grp[0])
        for n, a, b in zip(grp, own, other):
            adamw(n, [a, b])
        after = res[grp[-1]][1]

    return (loss, dx[None], *[res[n][0] for n in order], *[res[n][1] for n in order],
            *[res[n][2] for n in order], *[res[n][3] for n in order])
```

```python
import jax
import jax.numpy as jnp
from jax import lax
from jax.experimental import pallas as pl
from jax.experimental.pallas import tpu as pltpu

F32 = jnp.float32
BF16 = jnp.bfloat16
MESH = pl.DeviceIdType.MESH

EPS = 1e-6
HGRN_W = 1024
HEAD = 128
N_HEADS = 8
CHUNK = 128
HGRN_UNROLL = 8
HGRN_HEADS_PER_STEP = 4
DPROJ_GROUPS = 8
MEM_HEADS = 4
MEM_HEAD_DIM = 512
N_CHIPS = 4
HALO = 8

ADAM_LR = 0.001
ADAM_B1 = 0.9
ADAM_B2 = 0.999
ADAM_EPS = 1e-08
ADAM_WD = 0.01
ADAM_STEP = 10


def _sigmoid(x):
    return 1.0 / (1.0 + jnp.exp(-x))


def _dot(a, b, dims):
    return lax.dot_general(a.astype(BF16), b.astype(BF16), (dims, ((), ())),
                           preferred_element_type=F32)


def _dot_nn(a, b):
    return _dot(a, b, ((1,), (0,)))


def _dot_nt(a, b):
    return _dot(a, b, ((1,), (1,)))


def _dot_tn(a, b):
    return _dot(a, b, ((0,), (0,)))


def _hdot(a, b, dims):
    return lax.dot_general(a, b, (dims, ((), ())), precision=lax.Precision.HIGH, preferred_element_type=F32)


def _hdot_nn(a, b):
    return _hdot(a, b, ((1,), (0,)))


def _hdot_tn(a, b):
    return _hdot(a, b, ((0,), (0,)))


def _exact_ones_dot(ones_bf16, x):
    hi = x.astype(BF16)
    r1 = x - hi.astype(F32)
    mid = r1.astype(BF16)
    lo = (r1 - mid.astype(F32)).astype(BF16)
    dims = (((1,), (0,)), ((), ()))
    return (lax.dot_general(ones_bf16, hi, dims, preferred_element_type=F32)
            + lax.dot_general(ones_bf16, mid, dims, preferred_element_type=F32)
            + lax.dot_general(ones_bf16, lo, dims, preferred_element_type=F32))


def _rows8(v):
    t, c = v.shape
    return v.reshape(t // 8, 8, c).sum(axis=0)


def _shift_down(x, halo, s):
    rolled = pltpu.roll(x, s, 0)
    hrolled = pltpu.roll(halo, s, 0)
    row = lax.broadcasted_iota(jnp.int32, hrolled.shape, 0)
    head = jnp.where(row < s, hrolled, rolled[:HALO])
    return jnp.concatenate([head, rolled[HALO:]], axis=0)


def _shift_up(x, halo, s):
    t = x.shape[0]
    rolled = pltpu.roll(x, t - s, 0)
    hrolled = pltpu.roll(halo, HALO - s, 0)
    row = lax.broadcasted_iota(jnp.int32, hrolled.shape, 0)
    tail = jnp.where(row >= HALO - s, hrolled, rolled[t - HALO:])
    return jnp.concatenate([rolled[:t - HALO], tail], axis=0)


def _params(*sem):
    return pltpu.CompilerParams(dimension_semantics=sem, vmem_limit_bytes=MM_VMEM_LIMIT)


def _row_tile(r, pref):
    while r % pref:
        pref //= 2
    return pref


def _rmsnorm_fwd(x, w, name, tm=512):
    s, d = x.shape
    tm = min(tm, s)

    def body(x_ref, w_ref, o_ref):
        xv = x_ref[...]
        r = lax.rsqrt(jnp.mean(xv * xv, axis=-1, keepdims=True) + EPS)
        o_ref[...] = ((xv * r) * w_ref[...]).astype(BF16)

    return pl.pallas_call(
        body, name=name, grid=(s // tm,),
        in_specs=[pl.BlockSpec((tm, d), lambda i: (i, 0)), pl.BlockSpec((1, d), lambda i: (0, 0))],
        out_specs=pl.BlockSpec((tm, d), lambda i: (i, 0)),
        out_shape=jax.ShapeDtypeStruct((s, d), BF16),
        compiler_params=_params("parallel"),
    )(x, w)


def _rmsnorm_bwd(dh, x, w, dres, name, tm=512):
    s, d = x.shape
    tm = min(tm, s)
    has_res = dres is not None

    def body(*refs):
        if has_res:
            dh_ref, x_ref, w_ref, dres_ref, dx_ref, dxb_ref, gw_ref = refs
        else:
            dh_ref, x_ref, w_ref, dx_ref, dxb_ref, gw_ref = refs

        @pl.when(pl.program_id(0) == 0)
        def _():
            gw_ref[...] = jnp.zeros_like(gw_ref)

        xv = x_ref[...]
        dhv = dh_ref[...].astype(F32)
        r = lax.rsqrt(jnp.mean(xv * xv, axis=-1, keepdims=True) + EPS)
        xhat = xv * r
        gw_ref[...] += _rows8(dhv * xhat)
        dxh = dhv * w_ref[...]
        dx = r * (dxh - xhat * jnp.mean(dxh * xhat, axis=-1, keepdims=True))
        if has_res:
            dx = dres_ref[...] + dx
        dx_ref[...] = dx
        dxb_ref[...] = dx.astype(BF16)

    row = pl.BlockSpec((tm, d), lambda i: (i, 0))
    in_specs = [row, row, pl.BlockSpec((1, d), lambda i: (0, 0))] + ([row] if has_res else [])
    args = (dh, x, w) + ((dres,) if has_res else ())
    return pl.pallas_call(
        body, name=name, grid=(s // tm,),
        in_specs=in_specs,
        out_specs=[row, row, pl.BlockSpec((8, d), lambda i: (0, 0))],
        out_shape=[jax.ShapeDtypeStruct((s, d), F32), jax.ShapeDtypeStruct((s, d), BF16),
                   jax.ShapeDtypeStruct((8, d), F32)],
        compiler_params=_params("arbitrary"),
    )(*args)


def _final_loss_bwd(x3, target, w, name, tm=512):
    s, d = x3.shape
    tm = min(tm, s)

    def body(x_ref, t_ref, w_ref, dx_ref, dxb_ref, gw_ref, loss_ref):
        @pl.when(pl.program_id(0) == 0)
        def _():
            gw_ref[...] = jnp.zeros_like(gw_ref)
            loss_ref[...] = jnp.zeros_like(loss_ref)

        xv = x_ref[...]
        r = lax.rsqrt(jnp.mean(xv * xv, axis=-1, keepdims=True) + EPS)
        xhat = xv * r
        y = xhat * w_ref[...]
        err = y - t_ref[...]
        part = 0.5 * jnp.mean(err * err, axis=-1, keepdims=True)
        tot = jnp.sum(part, axis=0, keepdims=True)
        rr = lax.broadcasted_iota(jnp.int32, loss_ref.shape, 0)
        cc = lax.broadcasted_iota(jnp.int32, loss_ref.shape, 1)
        loss_ref[...] += jnp.where((rr == 0) & (cc == 0), tot, 0.0)
        dy = err * (1.0 / d)
        gw_ref[...] += _rows8(dy * xhat)
        dxh = dy * w_ref[...]
        dx = r * (dxh - xhat * jnp.mean(dxh * xhat, axis=-1, keepdims=True))
        dx_ref[...] = dx
        dxb_ref[...] = dx.astype(BF16)

    row = pl.BlockSpec((tm, d), lambda i: (i, 0))
    return pl.pallas_call(
        body, name=name, grid=(s // tm,),
        in_specs=[row, row, pl.BlockSpec((1, d), lambda i: (0, 0))],
        out_specs=[row, row, pl.BlockSpec((8, d), lambda i: (0, 0)), pl.BlockSpec((8, 128), lambda i: (0, 0))],
        out_shape=[jax.ShapeDtypeStruct((s, d), F32), jax.ShapeDtypeStruct((s, d), BF16),
                   jax.ShapeDtypeStruct((8, d), F32), jax.ShapeDtypeStruct((8, 128), F32)],
        compiler_params=_params("arbitrary"),
    )(x3, target, w)


MM_TILES = (1024, 1408, 512, 256, 128)
MM_K_TILES = (2816, 2048, 1792, 1408, 1024, 512, 256, 128)
MM_VMEM_LIMIT = 56 * 1024 * 1024
MM_VMEM_BUDGET = 46 * 1024 * 1024


def _pick_tile(dim):
    for t in MM_TILES:
        if dim % t == 0:
            return t
    return dim


def _matmul(a, b, mode, name, *, out_dtype=F32, residual=None, extra_bf16=False, tm=None, tn=None, tk=None,
            groups=None):
    if groups is not None and mode == "nt":
        _, m, gw = a.shape
        n, k2 = b.shape
        k, tk = groups * gw, gw
    elif groups is not None and mode == "tn":
        k, m = a.shape
        _, k2, gw = b.shape
        n, tn = groups * gw, gw
    elif mode == "nn":
        (m, k), (k2, n) = a.shape, b.shape
    elif mode == "nt":
        (m, k), (n, k2) = a.shape, b.shape
    else:
        (k, m), (k2, n) = a.shape, b.shape
    assert k == k2, (a.shape, b.shape, mode)
    auto_tm = tm is None
    tm = _pick_tile(m) if tm is None else min(tm, m)
    tn = _pick_tile(n) if tn is None else min(tn, n)
    out_elt = jnp.dtype(out_dtype).itemsize + (2 if extra_bf16 else 0) + (4 if residual is not None else 0)

    def vmem_bytes(t, rows=None):
        rows = tm if rows is None else rows
        return (2 * (rows * t * a.dtype.itemsize + t * tn * b.dtype.itemsize) + 2 * rows * tn * out_elt
                + rows * tn * 4)

    if auto_tm and tk is None and m % (2 * tm) == 0 and vmem_bytes(k, 2 * tm) <= MM_VMEM_BUDGET:
        tm = 2 * tm

    if tk is None:
        tk = next(t for t in MM_K_TILES if k % t == 0 and t <= k and vmem_bytes(t) <= MM_VMEM_BUDGET)
    assert m % tm == 0 and n % tn == 0 and k % tk == 0, (m, n, k, tm, tn, tk)
    nk = k // tk
    dims = {"nn": ((1,), (0,)), "nt": ((1,), (1,)), "tn": ((0,), (0,))}[mode]
    has_res = residual is not None

    def body(*refs):
        refs = list(refs)
        a_ref, b_ref = refs[0], refs[1]
        r_ref = refs[2] if has_res else None
        outs = refs[2 + has_res:]
        o_ref = outs[0]
        o2_ref = outs[1] if extra_bf16 else None
        def finish(r):
            if has_res:
                r = r_ref[...] + r
            o_ref[...] = r.astype(out_dtype)
            if extra_bf16:
                o2_ref[...] = r.astype(BF16)

        if nk == 1:
            finish(_dot(a_ref[...], b_ref[...], dims))
            return
        acc = outs[-1]
        kk = pl.program_id(2)

        @pl.when(kk == 0)
        def _():
            acc[...] = _dot(a_ref[...], b_ref[...], dims)

        if nk > 2:
            @pl.when((kk > 0) & (kk < nk - 1))
            def _():
                acc[...] += _dot(a_ref[...], b_ref[...], dims)

        @pl.when(kk == nk - 1)
        def _():
            finish(acc[...] + _dot(a_ref[...], b_ref[...], dims))

    col_outer = nk == 1 and tn * b.dtype.itemsize > tm * a.dtype.itemsize

    def sp(shape, f):
        return pl.BlockSpec(shape, (lambda j, i, kk: f(i, j, kk)) if col_outer else f)

    if mode == "tn":
        a_spec = sp((tk, tm), lambda i, j, kk: (kk, i))
    elif groups is not None:
        a_spec = sp((None, tm, tk), lambda i, j, kk: (kk, i, 0))
    else:
        a_spec = sp((tm, tk), lambda i, j, kk: (i, kk))
    if mode == "nt":
        b_spec = sp((tn, tk), lambda i, j, kk: (j, kk))
    elif groups is not None:
        b_spec = sp((None, tk, tn), lambda i, j, kk: (j, kk, 0))
    else:
        b_spec = sp((tk, tn), lambda i, j, kk: (kk, j))
    o_spec = sp((tm, tn), lambda i, j, kk: (i, j))
    in_specs = [a_spec, b_spec] + ([o_spec] if has_res else [])
    out_specs = [o_spec] + ([o_spec] if extra_bf16 else [])
    out_shape = [jax.ShapeDtypeStruct((m, n), out_dtype)] + ([jax.ShapeDtypeStruct((m, n), BF16)] if extra_bf16 else [])
    args = (a, b) + ((residual,) if has_res else ())
    res = pl.pallas_call(
        body, name=name, grid=(n // tn, m // tm, nk) if col_outer else (m // tm, n // tn, nk),
        in_specs=in_specs, out_specs=out_specs, out_shape=out_shape,
        scratch_shapes=[pltpu.VMEM((tm, tn) if nk > 1 else (8, 128), F32)],
        compiler_params=pltpu.CompilerParams(dimension_semantics=("parallel", "parallel", "arbitrary"),
                                             vmem_limit_bytes=MM_VMEM_LIMIT),
    )(*args)
    return res if extra_bf16 else res[0]


def _matmul_windows(h, w, windows, name, prev=None):
    s, k = h.shape
    n = w.shape[1]
    tn = n // N_CHIPS
    tm = min(s, MM_TILES[0])

    def body(win_ref, h_ref, w_ref, *rest):
        rest[-1][...] = _dot(h_ref[...], w_ref[...], ((1,), (0,))).astype(BF16)

    extra = [] if prev is None else [prev]
    return pl.pallas_call(
        body, name=name,
        grid_spec=pltpu.PrefetchScalarGridSpec(
            num_scalar_prefetch=1, grid=(s // tm, windows.shape[0]),
            in_specs=[pl.BlockSpec((tm, k), lambda i, j, win: (i, 0)),
                      pl.BlockSpec((k, tn), lambda i, j, win: (0, win[j]))]
            + [pl.BlockSpec(memory_space=pl.ANY)] * len(extra),
            out_specs=pl.BlockSpec((tm, tn), lambda i, j, win: (i, win[j]))),
        out_shape=jax.ShapeDtypeStruct((s, n), BF16),
        input_output_aliases={3: 0} if extra else {},
        compiler_params=_params("parallel", "arbitrary"),
    )(windows, h, w, *extra)


def _hgrn_gates(qp, fp, lb):
    sig = _sigmoid(fp)
    f = lb + (1.0 - lb) * sig
    logf = jnp.log(f)
    k = 1.0 - f
    sq = _sigmoid(qp)
    q = qp * sq
    return sig, f, logf, k, sq, q


def _hgrn_fwd(proj, lb0, lb1, norm_w, name, tb=1024):
    s = proj.shape[0]
    tb = min(tb, s)
    nb, ncb = s // tb, tb // CHUNK

    def body(q_ref, f_ref, i_ref, g_ref, a0_ref, a1_ref, nw_ref, o_ref, og_ref, st_ref, state):
        @pl.when(pl.program_id(1) == 0)
        def _():
            state[...] = jnp.zeros_like(state)

        lb2 = _sigmoid(a0_ref[...] - a1_ref[...])
        row = lax.broadcasted_iota(jnp.int32, (CHUNK, CHUNK), 0)
        col = lax.broadcasted_iota(jnp.int32, (CHUNK, CHUNK), 1)
        tril = row >= col
        ones_l = tril.astype(BF16)
        nw = nw_ref[...]

        def chunk(c, carry):
            rows = pl.ds(pl.multiple_of(c * CHUNK, CHUNK), CHUNK)
            for hh in range(HGRN_HEADS_PER_STEP):
                cols = slice(hh * HEAD, (hh + 1) * HEAD)
                v = i_ref[rows, cols].astype(F32)
                _, _, logf, k, _, q = _hgrn_gates(q_ref[rows, cols].astype(F32), f_ref[rows, cols].astype(F32),
                                                  lb2[:, cols])
                b = _exact_ones_dot(ones_l, logf)
                bl = jnp.sum(logf, axis=0, keepdims=True)
                bm = 0.5 * bl
                st = state[hh]
                st_ref[hh, c] = st
                qt = q * jnp.exp(b - bm)
                kt = k * jnp.exp(bm - b)
                a = jnp.where(tril, _dot_nt(qt, kt), 0.0)
                o = _dot_nt(q * jnp.exp(b), st) + _dot_nn(a, v)
                state[hh] = st * jnp.exp(bl) + _dot_tn(v, k * jnp.exp(bl - b))
                o_ref[rows, cols] = o
                on = (o * lax.rsqrt(jnp.mean(o * o, axis=-1, keepdims=True) + EPS)) * nw
                gv = g_ref[rows, cols].astype(F32)
                og_ref[rows, cols] = (on * (gv * _sigmoid(gv))).astype(BF16)
            return carry

        lax.fori_loop(0, ncb, chunk, 0, unroll=HGRN_UNROLL)

    hp, wd = HGRN_HEADS_PER_STEP, HGRN_HEADS_PER_STEP * HEAD
    ngrp = N_HEADS // hp

    def colblk(group):
        return pl.BlockSpec((tb, wd), lambda h, j: (j, group * ngrp + h))

    vec = pl.BlockSpec((1, wd), lambda h, j: (0, h))
    out_blk = pl.BlockSpec((tb, wd), lambda h, j: (j, h))
    return pl.pallas_call(
        body, name=name, grid=(ngrp, nb),
        in_specs=[colblk(0), colblk(1), colblk(2), colblk(3), vec, vec, pl.BlockSpec((1, HEAD), lambda h, j: (0, 0))],
        out_specs=[out_blk, out_blk, pl.BlockSpec((hp, ncb, HEAD, HEAD), lambda h, j: (h, j, 0, 0))],
        out_shape=[jax.ShapeDtypeStruct((s, HGRN_W), F32), jax.ShapeDtypeStruct((s, 2 * HGRN_W), BF16),
                   jax.ShapeDtypeStruct((N_HEADS, s // CHUNK, HEAD, HEAD), F32)],
        scratch_shapes=[pltpu.VMEM((hp, HEAD, HEAD), F32)],
        compiler_params=_params("parallel", "arbitrary"),
    )(proj, proj, proj, proj, lb0, lb1, norm_w)


def _hgrn_bwd(proj, lb0, lb1, norm_w, o, states, dmix, name, tb=1024):
    s = proj.shape[0]
    tb = min(tb, s)
    nb, ncb = s // tb, tb // CHUNK

    def body(q_ref, f_ref, i_ref, g_ref, a0_ref, a1_ref, nw_ref, o_ref, st_ref, dm_ref,
             dp_ref, glb_ref, gnw_ref, dstate):
        h = pl.program_id(0)

        @pl.when(pl.program_id(1) == 0)
        def _():
            dstate[...] = jnp.zeros_like(dstate)
            glb_ref[...] = jnp.zeros_like(glb_ref)

        @pl.when((pl.program_id(1) == 0) & (h == 0))
        def _():
            gnw_ref[...] = jnp.zeros_like(gnw_ref)

        lb2 = _sigmoid(a0_ref[...] - a1_ref[...])
        row = lax.broadcasted_iota(jnp.int32, (CHUNK, CHUNK), 0)
        col = lax.broadcasted_iota(jnp.int32, (CHUNK, CHUNK), 1)
        tril = row >= col
        ones_l = tril.astype(BF16)
        ones_u = (row <= col).astype(BF16)
        nw = nw_ref[...]

        def chunk(cc, carry):
            c = ncb - 1 - cc
            rows = pl.ds(pl.multiple_of(c * CHUNK, CHUNK), CHUNK)
            for hh in range(HGRN_HEADS_PER_STEP):
                cols = slice(hh * HEAD, (hh + 1) * HEAD)
                lb = lb2[:, cols]
                qp = q_ref[rows, cols].astype(F32)
                v = i_ref[rows, cols].astype(F32)
                sig, f, logf, k, sq, q = _hgrn_gates(qp, f_ref[rows, cols].astype(F32), lb)
                gv = g_ref[rows, cols].astype(F32)
                sg = _sigmoid(gv)
                silu_g = gv * sg
                dog = dm_ref[rows, cols].astype(F32)
                ov = o_ref[rows, cols]
                r = lax.rsqrt(jnp.mean(ov * ov, axis=-1, keepdims=True) + EPS)
                ohat = ov * r
                on = ohat * nw
                dp_ref[3, rows, cols] = (dog * on * (sg * (1.0 + gv * (1.0 - sg)))).astype(BF16)
                don = dog * silu_g
                gnw_ref[...] += _rows8(don * ohat)
                doh = don * nw
                do = r * (doh - ohat * jnp.mean(doh * ohat, axis=-1, keepdims=True))
                b = _exact_ones_dot(ones_l, logf)
                bl = jnp.sum(logf, axis=0, keepdims=True)
                bm = 0.5 * bl
                e_q = jnp.exp(b - bm)
                e_k = jnp.exp(bm - b)
                e_b = jnp.exp(b)
                e_l = jnp.exp(bl - b)
                qt, kt, qb, kb = q * e_q, k * e_k, q * e_b, k * e_l
                st0 = st_ref[hh, c]
                dst = dstate[hh]
                a = jnp.where(tril, _dot_nt(qt, kt), 0.0)
                da = jnp.where(tril, _dot_nt(do, v), 0.0)
                dq = _hdot_nn(da, kt) * e_q + _dot_nn(do, st0) * e_b
                dkb = _dot_nn(v, dst) * e_l
                dk = _hdot_tn(da, qt) * e_k + dkb
                dv = _dot_tn(a, do) + _dot_nt(kb, dst)
                e_bl = jnp.exp(bl)
                dstate[hh] = dst * e_bl + _dot_tn(do, qb)
                db = q * dq - k * dk
                db_last = jnp.sum(k * dkb, axis=0, keepdims=True) + e_bl * jnp.sum(st0 * dst, axis=0, keepdims=True)
                dlogf = _exact_ones_dot(ones_u, db) + db_last
                dfg = dlogf / f - dk
                dp_ref[1, rows, cols] = (dfg * (1.0 - lb) * (sig * (1.0 - sig))).astype(BF16)
                glb_ref[:, cols] += _rows8(dfg * (1.0 - sig)) * (lb * (1.0 - lb))
                dp_ref[0, rows, cols] = (dq * (sq * (1.0 + qp * (1.0 - sq)))).astype(BF16)
                dp_ref[2, rows, cols] = dv.astype(BF16)
            return carry

        lax.fori_loop(0, ncb, chunk, 0, unroll=HGRN_UNROLL)

    hp, wd = HGRN_HEADS_PER_STEP, HGRN_HEADS_PER_STEP * HEAD
    ngrp = N_HEADS // hp

    def colblk(group):
        return pl.BlockSpec((tb, wd), lambda h, j: (nb - 1 - j, group * ngrp + h))

    vec = pl.BlockSpec((1, wd), lambda h, j: (0, h))
    blk = pl.BlockSpec((tb, wd), lambda h, j: (nb - 1 - j, h))
    return pl.pallas_call(
        body, name=name, grid=(ngrp, nb),
        in_specs=[colblk(0), colblk(1), colblk(2), colblk(3), vec, vec, pl.BlockSpec((1, HEAD), lambda h, j: (0, 0)),
                  blk, pl.BlockSpec((hp, ncb, HEAD, HEAD), lambda h, j: (h, nb - 1 - j, 0, 0)), blk],
        out_specs=[pl.BlockSpec((4, tb, wd), lambda h, j: (0, nb - 1 - j, h)),
                   pl.BlockSpec((8, wd), lambda h, j: (0, h)), pl.BlockSpec((8, HEAD), lambda h, j: (0, 0))],
        out_shape=[jax.ShapeDtypeStruct((DPROJ_GROUPS, s, HGRN_W), BF16),
                   jax.ShapeDtypeStruct((8, HGRN_W), F32), jax.ShapeDtypeStruct((8, HEAD), F32)],
        scratch_shapes=[pltpu.VMEM((hp, HEAD, HEAD), F32)],
        compiler_params=_params("arbitrary", "arbitrary"),
    )(proj, proj, proj, proj, lb0, lb1, norm_w, o, states, dmix)


HALO_BLK = 16


def _f32(ref):
    return ref[...].astype(F32)


def _halo_prev(ref):
    return ref[...].astype(F32)[HALO_BLK - HALO:]


def _halo_next(ref):
    return ref[...].astype(F32)[:HALO]


def _conv3(x0, x1, x2, w_ref):
    y = x0 * w_ref[0:1, :]
    y = y + x1 * w_ref[1:2, :]
    return y + x2 * w_ref[2:3, :]


def _sconv_fwd(proj, w8, mix, name, tb=512):
    s = proj.shape[0]
    tb = min(tb, s)
    hb = tb // HALO_BLK

    def body(cb_ref, cc_ref, ch_ref, cch_ref, chh_ref, w_ref, mix_ref, y_ref):
        first = pl.program_id(0) == 0
        u = _f32(cc_ref) * _f32(ch_ref)
        uh = jnp.where(first, 0.0, _halo_prev(cch_ref) * _halo_prev(chh_ref))
        conv = _conv3(_shift_down(u, uh, 2), _shift_down(u, uh, 1), u, w_ref)
        y_ref[...] = (_f32(cb_ref) * conv).astype(BF16)

    def blk(g):
        return pl.BlockSpec((tb, HGRN_W), lambda j: (j, g))

    def halo(g):
        return pl.BlockSpec((HALO_BLK, HGRN_W), lambda j: (jnp.maximum(j * hb - 1, 0), g))

    return pl.pallas_call(
        body, name=name, grid=(s // tb,),
        in_specs=[blk(4), blk(5), blk(6), halo(5), halo(6), pl.BlockSpec((HALO, HGRN_W), lambda j: (0, 0)),
                  pl.BlockSpec(memory_space=pl.ANY)],
        out_specs=pl.BlockSpec((tb, HGRN_W), lambda j: (j, 1)),
        out_shape=jax.ShapeDtypeStruct(mix.shape, BF16),
        input_output_aliases={6: 0},
        compiler_params=_params("parallel"),
    )(proj, proj, proj, proj, proj, w8, mix)


def _sconv_bwd(proj, w8, dmix, dproj, name, tb=512):
    s = proj.shape[0]
    tb = min(tb, s)
    hb = tb // HALO_BLK
    nb = s // tb
    last_h = s // HALO_BLK - 1

    def body(cb_ref, cc_ref, ch_ref, cch_ref, chh_ref, cbn_ref, dy_ref, dyn_ref, w_ref, dproj_ref,
             dp_ref, gw_ref):
        j = pl.program_id(0)

        @pl.when(j == 0)
        def _():
            gw_ref[...] = jnp.zeros_like(gw_ref)

        cc, ch, cb = _f32(cc_ref), _f32(ch_ref), _f32(cb_ref)
        u = cc * ch
        uh = jnp.where(j == 0, 0.0, _halo_prev(cch_ref) * _halo_prev(chh_ref))
        u2, u1 = _shift_down(u, uh, 2), _shift_down(u, uh, 1)
        conv = _conv3(u2, u1, u, w_ref)
        dy = _f32(dy_ref)
        dp_ref[0] = (dy * conv).astype(BF16)
        dc = dy * cb
        dcn = jnp.where(j == nb - 1, 0.0, _halo_next(dyn_ref) * _halo_next(cbn_ref))
        gw_ref[0:8, :] += _rows8(dc * u2)
        gw_ref[8:16, :] += _rows8(dc * u1)
        gw_ref[16:24, :] += _rows8(dc * u)
        du = dc * w_ref[2:3, :] + _shift_up(dc, dcn, 1) * w_ref[1:2, :] + _shift_up(dc, dcn, 2) * w_ref[0:1, :]
        dp_ref[1] = (du * ch).astype(BF16)
        dp_ref[2] = (du * cc).astype(BF16)
        dp_ref[3] = jnp.zeros(dp_ref.shape[1:], BF16)

    def blk(g):
        return pl.BlockSpec((tb, HGRN_W), lambda j: (j, g))

    def halo_prev(g):
        return pl.BlockSpec((HALO_BLK, HGRN_W), lambda j: (jnp.maximum(j * hb - 1, 0), g))

    def halo_next(g):
        return pl.BlockSpec((HALO_BLK, HGRN_W), lambda j: (jnp.minimum((j + 1) * hb, last_h), g))

    return pl.pallas_call(
        body, name=name, grid=(nb,),
        in_specs=[blk(4), blk(5), blk(6), halo_prev(5), halo_prev(6), halo_next(4), blk(1), halo_next(1),
                  pl.BlockSpec((HALO, HGRN_W), lambda j: (0, 0)), pl.BlockSpec(memory_space=pl.ANY)],
        out_specs=[pl.BlockSpec((4, tb, HGRN_W), lambda j: (1, j, 0)), pl.BlockSpec((24, HGRN_W), lambda j: (0, 0))],
        out_shape=[jax.ShapeDtypeStruct(dproj.shape, BF16), jax.ShapeDtypeStruct((24, HGRN_W), F32)],
        input_output_aliases={9: 0},
        compiler_params=_params("arbitrary"),
    )(proj, proj, proj, proj, proj, proj, dmix, dmix, w8, dproj)


def _attn_fwd(q, kk, vv, name, tb=1024):
    s, d = q.shape
    m = kk.shape[0]
    tb = min(tb, s)
    scale = MEM_HEAD_DIM ** -0.5

    def body(q_ref, k_ref, v_ref, o_ref):
        for hh in range(MEM_HEADS):
            cols = slice(hh * MEM_HEAD_DIM, (hh + 1) * MEM_HEAD_DIM)
            sc = _dot_nt(q_ref[:, cols], k_ref[:, cols]) * scale
            sc = sc - jnp.max(sc, axis=-1, keepdims=True)
            e = jnp.exp(sc)
            p = e / jnp.sum(e, axis=-1, keepdims=True)
            o_ref[:, cols] = _dot_nn(p, v_ref[:, cols]).astype(BF16)

    full = pl.BlockSpec((m, d), lambda i: (0, 0))
    return pl.pallas_call(
        body, name=name, grid=(s // tb,),
        in_specs=[pl.BlockSpec((tb, d), lambda i: (i, 0)), full, full],
        out_specs=pl.BlockSpec((tb, d), lambda i: (i, 0)),
        out_shape=jax.ShapeDtypeStruct((s, d), BF16),
        compiler_params=pltpu.CompilerParams(dimension_semantics=("parallel",), vmem_limit_bytes=MM_VMEM_LIMIT),
    )(q, kk, vv)


def _attn_bwd(q, kk, vv, datt, name, tb=1024):
    s, d = q.shape
    m = kk.shape[0]
    tb = min(tb, s)
    scale = MEM_HEAD_DIM ** -0.5

    def body(q_ref, k_ref, v_ref, do_ref, dq_ref, dk_ref, dv_ref):
        @pl.when(pl.program_id(0) == 0)
        def _():
            dk_ref[...] = jnp.zeros_like(dk_ref)
            dv_ref[...] = jnp.zeros_like(dv_ref)

        for hh in range(MEM_HEADS):
            cols = slice(hh * MEM_HEAD_DIM, (hh + 1) * MEM_HEAD_DIM)
            qh, kh, vh, doh = q_ref[:, cols], k_ref[:, cols], v_ref[:, cols], do_ref[:, cols]
            sc = _dot_nt(qh, kh) * scale
            sc = sc - jnp.max(sc, axis=-1, keepdims=True)
            e = jnp.exp(sc)
            p = e / jnp.sum(e, axis=-1, keepdims=True)
            dp = _dot_nt(doh, vh)
            ds = p * (dp - jnp.sum(dp * p, axis=-1, keepdims=True)) * scale
            dq_ref[:, cols] = _dot_nn(ds, kh).astype(BF16)
            dk_ref[:, cols] += _dot_tn(ds, qh)
            dv_ref[:, cols] += _dot_tn(p, doh)

    full = pl.BlockSpec((m, d), lambda i: (0, 0))
    row = pl.BlockSpec((tb, d), lambda i: (i, 0))
    return pl.pallas_call(
        body, name=name, grid=(s // tb,),
        in_specs=[row, full, full, row],
        out_specs=[row, full, full],
        out_shape=[jax.ShapeDtypeStruct((s, d), BF16), jax.ShapeDtypeStruct((m, d), F32),
                   jax.ShapeDtypeStruct((m, d), F32)],
        compiler_params=pltpu.CompilerParams(dimension_semantics=("arbitrary",), vmem_limit_bytes=MM_VMEM_LIMIT),
    )(q, kk, vv, datt)


def _ffn_fwd(g, u, w8, bias, name, tb=512, tc=1408):
    s, f = g.shape
    tb = min(tb, s)
    tc = tc if f % tc == 0 else 512
    hb = tb // HALO_BLK

    def body(g_ref, gh_ref, u_ref, w_ref, b_ref, z_ref, a_ref):
        gv = _f32(g_ref)
        gh = jnp.where(pl.program_id(1) == 0, 0.0, _halo_prev(gh_ref))
        a = _conv3(_shift_down(gv, gh, 2), _shift_down(gv, gh, 1), gv, w_ref) + b_ref[...]
        a_ref[...] = a.astype(BF16)
        z_ref[...] = ((a * _sigmoid(a)) * _f32(u_ref)).astype(BF16)

    blk = pl.BlockSpec((tb, tc), lambda c, j: (j, c))
    return pl.pallas_call(
        body, name=name, grid=(f // tc, s // tb),
        in_specs=[blk, pl.BlockSpec((HALO_BLK, tc), lambda c, j: (jnp.maximum(j * hb - 1, 0), c)), blk,
                  pl.BlockSpec((HALO, tc), lambda c, j: (0, c)), pl.BlockSpec((1, tc), lambda c, j: (0, c))],
        out_specs=[blk, blk],
        out_shape=[jax.ShapeDtypeStruct((s, f), BF16), jax.ShapeDtypeStruct((s, f), BF16)],
        compiler_params=pltpu.CompilerParams(dimension_semantics=("parallel", "parallel"),
                                             vmem_limit_bytes=MM_VMEM_LIMIT),
    )(g, g, u, w8, bias)


def _ffn_bwd(a, g, u, dz, w8, name, tb=512, tc=1408):
    s, f = g.shape
    tb = min(tb, s)
    tc = tc if f % tc == 0 else 512
    nb = s // tb

    def body(a_ref, g_ref, u_ref, dz_ref, w_ref, dg_ref, du_ref, gb_ref, gw_ref, da_next):
        jj = pl.program_id(1)

        @pl.when(jj == 0)
        def _():
            gb_ref[...] = jnp.zeros_like(gb_ref)
            gw_ref[...] = jnp.zeros_like(gw_ref)
            da_next[...] = jnp.zeros_like(da_next)

        a = _f32(a_ref)
        sa = _sigmoid(a)
        dz = _f32(dz_ref)
        du_ref[...] = (dz * (a * sa)).astype(BF16)
        da = dz * _f32(u_ref) * (sa * (1.0 + a * (1.0 - sa)))
        gb_ref[...] += _rows8(da)
        dan = da_next[...]
        da1, da2 = _shift_up(da, dan, 1), _shift_up(da, dan, 2)
        gv = _f32(g_ref)
        gw_ref[0:8, :] += _rows8(da2 * gv)
        gw_ref[8:16, :] += _rows8(da1 * gv)
        gw_ref[16:24, :] += _rows8(da * gv)
        dg_ref[...] = (da * w_ref[2:3, :] + da1 * w_ref[1:2, :] + da2 * w_ref[0:1, :]).astype(BF16)
        da_next[...] = da[:HALO]

    blk = pl.BlockSpec((tb, tc), lambda c, jj: (nb - 1 - jj, c))
    return pl.pallas_call(
        body, name=name, grid=(f // tc, nb),
        in_specs=[blk, blk, blk, blk, pl.BlockSpec((HALO, tc), lambda c, jj: (0, c))],
        out_specs=[blk, blk, pl.BlockSpec((8, tc), lambda c, jj: (0, c)), pl.BlockSpec((24, tc), lambda c, jj: (0, c))],
        out_shape=[jax.ShapeDtypeStruct((s, f), BF16), jax.ShapeDtypeStruct((s, f), BF16),
                   jax.ShapeDtypeStruct((8, f), F32), jax.ShapeDtypeStruct((24, f), F32)],
        scratch_shapes=[pltpu.VMEM((HALO, tc), F32)],
        compiler_params=pltpu.CompilerParams(dimension_semantics=("parallel", "arbitrary"),
                                             vmem_limit_bytes=MM_VMEM_LIMIT),
    )(a, g, u, dz, w8)


def _window(ref, axis, slot, size):
    start = pl.multiple_of(slot * size, size)
    if axis == 0:
        return ref.at[pl.ds(start, size), :]
    return ref.at[:, pl.ds(start, size)]


def _chip_peers():
    x, y, c = lax.axis_index("x"), lax.axis_index("y"), lax.axis_index("c")
    peers = [(1 - x, y, c), (x, 1 - y, c), (1 - x, 1 - y, c)]
    slots = [2 * (1 - x) + y, 2 * x + (1 - y), 2 * (1 - x) + (1 - y)]
    return 2 * x + y, peers, slots


HBM_SPEC = pl.BlockSpec(memory_space=pltpu.HBM)
SEM_SPEC = pl.BlockSpec(memory_space=pltpu.SEMAPHORE)
EFFECT = pltpu.SideEffectType.DATAFLOW_SIDE_EFFECTING


def _hbm(a):
    return pltpu.with_memory_space_constraint(a, pltpu.HBM)


def _cast_into_full(x, axis, slot_arr, dtype, name, after=None):
    r, c = x.shape
    tr = _row_tile(r, 512)
    nb = r // tr
    full = (r * N_CHIPS, c) if axis == 0 else (r, c * N_CHIPS)

    def body(slot_ref, x_ref, *rest):
        rest[-1][...] = x_ref[...].astype(dtype)

    if axis == 0:
        out_map = lambda i, s: (s[0] * nb + i, 0)
    else:
        out_map = lambda i, s: (i, s[0])
    extra = [] if after is None else [after]
    return pl.pallas_call(
        body, name=name,
        grid_spec=pltpu.PrefetchScalarGridSpec(
            num_scalar_prefetch=1, grid=(nb,),
            in_specs=[pl.BlockSpec((tr, c), lambda i, s: (i, 0))] + [pl.BlockSpec(memory_space=pl.ANY)] * len(extra),
            out_specs=pl.BlockSpec((tr, c), out_map)),
        out_shape=jax.ShapeDtypeStruct(full, dtype),
        compiler_params=_params("parallel"),
    )(slot_arr, x, *extra)


def _piece(ref, axis, slot, half):
    size = ref.shape[axis] // N_CHIPS
    if half is None:
        return _window(ref, axis, slot, size)
    if axis == 0:
        h = size // 2
        return ref.at[pl.ds(pl.multiple_of(slot * size + half * h, h), h), :]
    h = ref.shape[0] // 2
    return ref.at[pl.ds(pl.multiple_of(half * h, h), h), pl.ds(pl.multiple_of(slot * size, size), size)]


def _gather_start(fulls, axes, split, groups, name):
    n, ng = len(fulls), len(groups)

    def body(*refs):
        outs = refs[n:]
        sems = outs[:2 * ng]
        thru = outs[2 * ng:2 * ng + n]
        token = outs[-1]
        slot, peers, _ = _chip_peers()
        c = lax.axis_index("c")
        for g, members in enumerate(groups):
            for i, t in enumerate(members):
                mine = _piece(thru[t], axes[t], slot, c if split[t] else None)
                for k in range(3):
                    pltpu.make_async_remote_copy(
                        src_ref=mine, dst_ref=mine, send_sem=sems[2 * g].at[3 * i + k],
                        recv_sem=sems[2 * g + 1].at[3 * i + k], device_id=peers[k], device_id_type=MESH).start()
        token[...] = jnp.zeros_like(token)

    sem_shapes = []
    for members in groups:
        sem_shapes += [pltpu.SemaphoreType.DMA((3 * len(members),))] * 2
    res = pl.pallas_call(
        body, name=name,
        in_specs=[HBM_SPEC] * n,
        out_specs=[SEM_SPEC] * (2 * ng) + [HBM_SPEC] * n + [pl.BlockSpec(memory_space=pltpu.VMEM)],
        out_shape=sem_shapes + [pltpu.HBM(f.shape, f.dtype) for f in fulls] + [jax.ShapeDtypeStruct((8, 128), F32)],
        input_output_aliases={t: 2 * ng + t for t in range(n)},
        compiler_params=pltpu.CompilerParams(has_side_effects=EFFECT),
    )(*[_hbm(f) for f in fulls])
    sems = [(res[2 * g], res[2 * g + 1]) for g in range(ng)]
    return sems, list(res[2 * ng:2 * ng + n]), res[-1]


def _gather_relay(fulls, axes, split, sems, after, name):
    n = len(fulls)
    nsplit = sum(split)

    def body(*refs):
        send_sems, recv_sems = refs[n], refs[n + 1]
        outs = refs[n + 3:]
        d_send, d_recv = outs[0], outs[1]
        thru = outs[2:2 + n]
        token = outs[-1]
        slot, peers, slots = _chip_peers()
        c = lax.axis_index("c")
        sibling = (lax.axis_index("x"), lax.axis_index("y"), 1 - c)
        for t in range(n):
            half = c if split[t] else None
            for k in range(3):
                cp = pltpu.make_async_remote_copy(
                    src_ref=_piece(thru[t], axes[t], slot, half), dst_ref=_piece(thru[t], axes[t], slots[k], half),
                    send_sem=send_sems.at[3 * t + k], recv_sem=recv_sems.at[3 * t + k],
                    device_id=peers[k], device_id_type=MESH)
                cp.wait_send()
                cp.wait_recv()
        i = 0
        for t in range(n):
            if not split[t]:
                continue
            for k in range(3):
                got = _piece(thru[t], axes[t], slots[k], c)
                pltpu.make_async_remote_copy(
                    src_ref=got, dst_ref=got, send_sem=d_send.at[3 * i + k], recv_sem=d_recv.at[3 * i + k],
                    device_id=sibling, device_id_type=MESH).start()
            i += 1
        token[...] = jnp.zeros_like(token)

    res = pl.pallas_call(
        body, name=name,
        in_specs=[HBM_SPEC] * n + [SEM_SPEC, SEM_SPEC, pl.BlockSpec(memory_space=pl.ANY)],
        out_specs=[SEM_SPEC, SEM_SPEC] + [HBM_SPEC] * n + [pl.BlockSpec(memory_space=pltpu.VMEM)],
        out_shape=[pltpu.SemaphoreType.DMA((3 * nsplit,)), pltpu.SemaphoreType.DMA((3 * nsplit,))]
        + [pltpu.HBM(f.shape, f.dtype) for f in fulls] + [jax.ShapeDtypeStruct((8, 128), F32)],
        input_output_aliases={t: 2 + t for t in range(n)},
        compiler_params=pltpu.CompilerParams(has_side_effects=EFFECT),
    )(*fulls, sems[0], sems[1], after)
    return (res[0], res[1]), list(res[2:2 + n]), res[-1]


def _gather_finish(fulls, axes, split, sems, after, name):
    n = len(fulls)

    def body(*refs):
        d_send, d_recv = refs[n], refs[n + 1]
        thru = refs[n + 3:]
        _, _, slots = _chip_peers()
        c = lax.axis_index("c")
        sibling = (lax.axis_index("x"), lax.axis_index("y"), 1 - c)
        i = 0
        for t in range(n):
            if not split[t]:
                continue
            for k in range(3):
                cp = pltpu.make_async_remote_copy(
                    src_ref=_piece(thru[t], axes[t], slots[k], c), dst_ref=_piece(thru[t], axes[t], slots[k], 1 - c),
                    send_sem=d_send.at[3 * i + k], recv_sem=d_recv.at[3 * i + k],
                    device_id=sibling, device_id_type=MESH)
                cp.wait_send()
                cp.wait_recv()
            i += 1

    return pl.pallas_call(
        body, name=name,
        in_specs=[HBM_SPEC] * n + [SEM_SPEC, SEM_SPEC, pl.BlockSpec(memory_space=pl.ANY)],
        out_specs=[HBM_SPEC] * n,
        out_shape=[pltpu.HBM(f.shape, f.dtype) for f in fulls],
        input_output_aliases={t: t for t in range(n)},
        compiler_params=pltpu.CompilerParams(has_side_effects=EFFECT),
    )(*fulls, sems[0], sems[1], after)


def _scatter_start(grads_bf16, axes, name):
    n = len(grads_bf16)

    def shard_shape(g, ax):
        return (g.shape[0] // N_CHIPS, g.shape[1]) if ax == 0 else (g.shape[0], g.shape[1] // N_CHIPS)

    shapes = [shard_shape(g, ax) for g, ax in zip(grads_bf16, axes)]

    def body(*refs):
        outs = refs[2 * n:]
        send_sems, recv_sems = outs[0], outs[1]
        gb, land = outs[2:2 + n], outs[2 + n:2 + 2 * n]
        token = outs[-1]
        _, peers, slots = _chip_peers()
        for t in range(n):
            size = shapes[t][axes[t]]
            for k in range(3):
                pltpu.make_async_remote_copy(
                    src_ref=_window(gb[t], axes[t], slots[k], size), dst_ref=land[t].at[k],
                    send_sem=send_sems.at[3 * t + k], recv_sem=recv_sems.at[3 * t + k],
                    device_id=peers[k], device_id_type=MESH).start()
        token[...] = jnp.zeros_like(token)

    lands = [_hbm(lax.empty((3,) + sh, BF16)) for sh in shapes]
    res = pl.pallas_call(
        body, name=name,
        in_specs=[HBM_SPEC] * (2 * n),
        out_specs=[SEM_SPEC, SEM_SPEC] + [HBM_SPEC] * (2 * n) + [pl.BlockSpec(memory_space=pltpu.VMEM)],
        out_shape=[pltpu.SemaphoreType.DMA((3 * n,)), pltpu.SemaphoreType.DMA((3 * n,))]
        + [pltpu.HBM(g.shape, g.dtype) for g in grads_bf16] + [pltpu.HBM((3,) + sh, BF16) for sh in shapes]
        + [jax.ShapeDtypeStruct((8, 128), F32)],
        input_output_aliases={t: 2 + t for t in range(2 * n)},
        compiler_params=pltpu.CompilerParams(has_side_effects=EFFECT),
    )(*[_hbm(g) for g in grads_bf16], *lands)
    return (res[0], res[1]), list(res[2:2 + n]), list(res[2 + n:2 + 2 * n]), res[-1]


def _scatter_wait(grads_thru, lands_thru, axes, sems, after, name):
    n = len(grads_thru)

    def body(*refs):
        send_sems, recv_sems = refs[2 * n], refs[2 * n + 1]
        outs = refs[2 * n + 3:]
        gb, land = outs[:n], outs[n:]
        _, peers, slots = _chip_peers()
        for t in range(n):
            size = land[t].shape[1 + axes[t]]
            for k in range(3):
                cp = pltpu.make_async_remote_copy(
                    src_ref=_window(gb[t], axes[t], slots[k], size), dst_ref=land[t].at[k],
                    send_sem=send_sems.at[3 * t + k], recv_sem=recv_sems.at[3 * t + k],
                    device_id=peers[k], device_id_type=MESH)
                cp.wait_send()
                cp.wait_recv()

    res = pl.pallas_call(
        body, name=name,
        in_specs=[HBM_SPEC] * (2 * n) + [SEM_SPEC, SEM_SPEC, pl.BlockSpec(memory_space=pl.ANY)],
        out_specs=[HBM_SPEC] * (2 * n),
        out_shape=[pltpu.HBM(g.shape, g.dtype) for g in grads_thru] + [pltpu.HBM(l.shape, l.dtype) for l in lands_thru],
        input_output_aliases={t: t for t in range(2 * n)},
        compiler_params=pltpu.CompilerParams(has_side_effects=EFFECT),
    )(*grads_thru, *lands_thru, sems[0], sems[1], after)
    return list(res[n:])


def _sibling_start(arrs, name):
    n = len(arrs)

    def body(*refs):
        outs = refs[2 * n:]
        send_sems, recv_sems = outs[0], outs[1]
        src, land = outs[2:2 + n], outs[2 + n:2 + 2 * n]
        token = outs[-1]
        sibling = (lax.axis_index("x"), lax.axis_index("y"), 1 - lax.axis_index("c"))
        for t in range(n):
            pltpu.make_async_remote_copy(
                src_ref=src[t], dst_ref=land[t], send_sem=send_sems.at[t], recv_sem=recv_sems.at[t],
                device_id=sibling, device_id_type=MESH).start()
        token[...] = jnp.zeros_like(token)

    lands = [_hbm(lax.empty(a.shape, a.dtype)) for a in arrs]
    res = pl.pallas_call(
        body, name=name,
        in_specs=[HBM_SPEC] * (2 * n),
        out_specs=[SEM_SPEC, SEM_SPEC] + [HBM_SPEC] * (2 * n) + [pl.BlockSpec(memory_space=pltpu.VMEM)],
        out_shape=[pltpu.SemaphoreType.DMA((n,)), pltpu.SemaphoreType.DMA((n,))]
        + [pltpu.HBM(a.shape, a.dtype) for a in arrs] * 2 + [jax.ShapeDtypeStruct((8, 128), F32)],
        input_output_aliases={t: 2 + t for t in range(2 * n)},
        compiler_params=pltpu.CompilerParams(has_side_effects=EFFECT),
    )(*[_hbm(a) for a in arrs], *lands)
    return (res[0], res[1]), list(res[2:2 + n]), list(res[2 + n:2 + 2 * n]), res[-1]


def _sibling_wait(src_thru, lands_thru, sems, after, name):
    n = len(src_thru)

    def body(*refs):
        send_sems, recv_sems = refs[2 * n], refs[2 * n + 1]
        outs = refs[2 * n + 3:]
        src, land = outs[:n], outs[n:]
        sibling = (lax.axis_index("x"), lax.axis_index("y"), 1 - lax.axis_index("c"))
        for t in range(n):
            cp = pltpu.make_async_remote_copy(
                src_ref=src[t], dst_ref=land[t], send_sem=send_sems.at[t], recv_sem=recv_sems.at[t],
                device_id=sibling, device_id_type=MESH)
            cp.wait_send()
            cp.wait_recv()

    res = pl.pallas_call(
        body, name=name,
        in_specs=[HBM_SPEC] * (2 * n) + [SEM_SPEC, SEM_SPEC, pl.BlockSpec(memory_space=pl.ANY)],
        out_specs=[HBM_SPEC] * (2 * n),
        out_shape=[pltpu.HBM(a.shape, a.dtype) for a in src_thru] * 2,
        input_output_aliases={t: t for t in range(2 * n)},
        compiler_params=pltpu.CompilerParams(has_side_effects=EFFECT),
    )(*src_thru, *lands_thru, sems[0], sems[1], after)
    return list(res[:n]), list(res[n:])


def _all_reduce_small(packed, name):
    nc = packed.shape[1]
    vmem = pl.BlockSpec(memory_space=pltpu.VMEM)

    def body(in_ref, out_ref, gbuf, send_sems, recv_sems):
        x, y, c = lax.axis_index("x"), lax.axis_index("y"), lax.axis_index("c")
        me = 4 * x + 2 * y + c
        gbuf[me] = jnp.sum(in_ref[...], axis=0, keepdims=True)
        copies = []
        for k in range(1, 8):
            peer = (x ^ ((k >> 2) & 1), y ^ ((k >> 1) & 1), c ^ (k & 1))
            rc = pltpu.make_async_remote_copy(
                src_ref=gbuf.at[me], dst_ref=gbuf.at[me], send_sem=send_sems.at[k - 1], recv_sem=recv_sems.at[k - 1],
                device_id=peer, device_id_type=MESH)
            rc.start()
            copies.append(rc)
        for k in range(1, 8):
            peer = (x ^ ((k >> 2) & 1), y ^ ((k >> 1) & 1), c ^ (k & 1))
            pltpu.make_async_remote_copy(
                src_ref=gbuf.at[me], dst_ref=gbuf.at[me ^ k], send_sem=send_sems.at[k - 1],
                recv_sem=recv_sems.at[k - 1], device_id=peer, device_id_type=MESH).wait_recv()
        for rc in copies:
            rc.wait_send()
        tot = gbuf[0]
        for d in range(1, 8):
            tot = tot + gbuf[d]
        out_ref[...] = tot

    return pl.pallas_call(
        body, name=name,
        in_specs=[vmem], out_specs=vmem,
        out_shape=jax.ShapeDtypeStruct((1, nc), F32),
        scratch_shapes=[pltpu.VMEM((8, 1, nc), F32), pltpu.SemaphoreType.DMA((7,)), pltpu.SemaphoreType.DMA((7,))],
    )(packed)


def _sum4(g_full, axis, slot_arr, recv, name):
    _, r, c = recv.shape
    tr = _row_tile(r, 512)
    nb = r // tr

    def body(slot_ref, own_ref, recv_ref, o_ref):
        acc = own_ref[...]
        for k in range(3):
            acc = acc + recv_ref[k].astype(F32)
        o_ref[...] = acc

    if axis == 0:
        own_map = lambda i, s: (s[0] * nb + i, 0)
    else:
        own_map = lambda i, s: (i, s[0])
    return pl.pallas_call(
        body, name=name,
        grid_spec=pltpu.PrefetchScalarGridSpec(
            num_scalar_prefetch=1, grid=(nb,),
            in_specs=[pl.BlockSpec((tr, c), own_map), pl.BlockSpec((3, tr, c), lambda i, s: (0, i, 0))],
            out_specs=pl.BlockSpec((tr, c), lambda i, s: (i, 0))),
        out_shape=jax.ShapeDtypeStruct((r, c), F32),
        compiler_params=pltpu.CompilerParams(dimension_semantics=("parallel",), vmem_limit_bytes=MM_VMEM_LIMIT),
    )(slot_arr, g_full, recv)


def _adamw(w, g_parts, m, v, name):
    r, c = w.shape
    tr = r if r % 128 else _row_tile(r, 256)
    npart = len(g_parts)

    def body(*refs):
        w_ref = refs[0]
        g_refs = refs[1:1 + npart]
        m_ref, v_ref, g_out, d_out, m_out, v_out = refs[1 + npart:]
        g = g_refs[0][...]
        for gr in g_refs[1:]:
            g = g + gr[...]
        mm = ADAM_B1 * m_ref[...] + (1.0 - ADAM_B1) * g
        vv = ADAM_B2 * v_ref[...] + (1.0 - ADAM_B2) * (g * g)
        m_hat = mm / (1.0 - ADAM_B1 ** ADAM_STEP)
        v_hat = vv / (1.0 - ADAM_B2 ** ADAM_STEP)
        g_out[...] = g
        d_out[...] = -ADAM_LR * (m_hat / (jnp.sqrt(v_hat) + ADAM_EPS) + ADAM_WD * w_ref[...])
        m_out[...] = mm
        v_out[...] = vv

    blk = pl.BlockSpec((tr, c), lambda i: (i, 0))
    shp = jax.ShapeDtypeStruct((r, c), F32)
    return pl.pallas_call(
        body, name=name, grid=(r // tr,),
        in_specs=[blk] * (3 + npart), out_specs=[blk] * 4, out_shape=[shp] * 4,
        compiler_params=pltpu.CompilerParams(dimension_semantics=("parallel",), vmem_limit_bytes=MM_VMEM_LIMIT),
    )(w, *g_parts, m, v)


def _pad_rows8(w):
    return jnp.pad(w, ((0, HALO - w.shape[0]), (0, 0)))


def kernel(x, mem, hgrn_lb, norm1_w, w_in, hgrn_norm_w, sconv_w, w_out, norm2_w, mem_norm_w, wq, wk, wv, wo, norm3_w, w_gate, w_up, ffn_conv_w, ffn_conv_b, w_down, final_norm_w, loss_target, m_hgrn_lb, m_norm1_w, m_w_in, m_hgrn_norm_w, m_sconv_w, m_w_out, m_norm2_w, m_mem_norm_w, m_wq, m_wk, m_wv, m_wo, m_norm3_w, m_w_gate, m_w_up, m_ffn_conv_w, m_ffn_conv_b, m_w_down, m_final_norm_w, v_hgrn_lb, v_norm1_w, v_w_in, v_hgrn_norm_w, v_sconv_w, v_w_out, v_norm2_w, v_mem_norm_w, v_wq, v_wk, v_wv, v_wo, v_norm3_w, v_w_gate, v_w_up, v_ffn_conv_w, v_ffn_conv_b, v_w_down, v_final_norm_w):
    xs, mems, tgt = x[0], mem[0], loss_target[0]
    d = xs.shape[1]
    fnw = final_norm_w.reshape(1, d)

    big = {"w_in": (w_in[0], 1), "w_out": (w_out[0], 0), "wq": (wq[0], 0), "wk": (wk[0], 0), "wv": (wv[0], 0),
           "wo": (wo[0], 0), "w_gate": (w_gate[0], 1), "w_up": (w_up[0], 1), "w_down": (w_down[0], 0)}
    names = list(big)
    slot_arr = (2 * lax.axis_index("x") + lax.axis_index("y")).astype(jnp.int32).reshape(1)
    gnames = names + ["sconv8", "fconv8"]
    axes = [big[n][1] for n in names] + [1, 1]
    groups = [["w_in"], ["w_out", "sconv8"], ["wq", "wk", "wv", "wo"], ["w_gate", "w_up", "fconv8", "w_down"]]
    gidx = [[gnames.index(n) for n in grp] for grp in groups]
    split = [True] * len(names) + [False, False]
    first = _cast_into_full(big["w_in"][0], 1, slot_arr, BF16, "cast_w_in")
    sems0, first, tok0 = _gather_start([first], [1], [True], [[0]], "gather_start_w_in")
    rest = [_cast_into_full(big[n][0], big[n][1], slot_arr, BF16, "cast_" + n, after=tok0) for n in names[1:]]
    rest += [_cast_into_full(_pad_rows8(sconv_w[0]), 1, slot_arr, F32, "cast_sconv_w", after=tok0),
             _cast_into_full(_pad_rows8(ffn_conv_w[0]), 1, slot_arr, F32, "cast_ffn_conv_w", after=tok0)]
    sems1, rest, tok = _gather_start(rest, axes[1:], split[1:], [[t - 1 for t in idx] for idx in gidx[1:]],
                                     "gather_start")
    gsems, fulls = sems0 + sems1, first + rest
    wf, relayed = {}, {}

    def gather_relay(g, after):
        idx = gidx[g]
        dsems, arrs, token = _gather_relay([fulls[t] for t in idx], [axes[t] for t in idx], [split[t] for t in idx],
                                           gsems[g], after, "gather_relay_%d" % g)
        relayed[g] = (dsems, arrs)
        return token[0:1, 0:1]

    def gather_finish(g, after):
        idx = gidx[g]
        dsems, arrs = relayed[g]
        got = _gather_finish(arrs, [axes[t] for t in idx], [split[t] for t in idx], dsems, after,
                             "gather_finish_%d" % g)
        wf.update(zip(groups[g], got))

    lb0, lb1 = hgrn_lb[0:1], hgrn_lb[1:2]

    h1 = _rmsnorm_fwd(xs, norm1_w + tok[0:1, 0:1], "norm1")
    slot = slot_arr[0]
    proj = _matmul_windows(h1, fulls[0], slot_arr, "proj_in_own")
    gather_relay(0, proj)
    gather_finish(0, proj)
    others = jnp.stack([(slot + 1) % N_CHIPS, (slot + 2) % N_CHIPS, (slot + 3) % N_CHIPS]).astype(jnp.int32)
    proj = _matmul_windows(h1, wf["w_in"], others, "proj_in", prev=proj)
    t1 = gather_relay(1, proj)
    o_h, og, states = _hgrn_fwd(proj, lb0, lb1, hgrn_norm_w + t1, "hgrn_fwd")
    gather_finish(1, o_h)
    t2 = gather_relay(2, o_h)
    sconv8 = wf["sconv8"]
    mix = _sconv_fwd(proj, sconv8 + t2, og, "sconv_fwd")
    x1 = _matmul(mix, wf["w_out"], "nn", "proj_out", residual=xs)
    h2 = _rmsnorm_fwd(x1, norm2_w, "norm2")
    gather_finish(2, h2)
    t3 = gather_relay(3, h2)
    mem_n = _rmsnorm_fwd(mems, mem_norm_w + t3, "norm_mem")
    qa = _matmul(h2, wf["wq"], "nn", "attn_q", out_dtype=BF16)
    ka = _matmul(mem_n, wf["wk"], "nn", "attn_k", out_dtype=BF16)
    va = _matmul(mem_n, wf["wv"], "nn", "attn_v", out_dtype=BF16)
    att = _attn_fwd(qa, ka, va, "attn_fwd")
    x2 = _matmul(att, wf["wo"], "nn", "attn_o", residual=x1)
    h3 = _rmsnorm_fwd(x2, norm3_w, "norm3")
    gather_finish(3, h3)
    fconv8 = wf["fconv8"]
    gate = _matmul(h3, wf["w_gate"], "nn", "ffn_gate", out_dtype=BF16)
    up = _matmul(h3, wf["w_up"], "nn", "ffn_up", out_dtype=BF16)
    z, act = _ffn_fwd(gate, up, fconv8, ffn_conv_b, "ffn_act")
    x3 = _matmul(z, wf["w_down"], "nn", "ffn_down", residual=x2)

    dx3, dx3b, g_final, loss8 = _final_loss_bwd(x3, tgt, fnw, "loss_bwd")
    gw = {}
    dz = _matmul(dx3b, wf["w_down"], "nt", "d_z", out_dtype=BF16)
    gw["w_down"] = _matmul(z, dx3b, "tn", "g_w_down", extra_bf16=True)
    dgate, du, g_fb, g_fw = _ffn_bwd(act, gate, up, dz, fconv8, "ffn_act_bwd")
    dh3 = _matmul(dgate, wf["w_gate"], "nt", "d_h3_gate")
    dh3 = _matmul(du, wf["w_up"], "nt", "d_h3_up", residual=dh3, out_dtype=BF16)
    gw["w_gate"] = _matmul(h3, dgate, "tn", "g_w_gate", extra_bf16=True)
    gw["w_up"] = _matmul(h3, du, "tn", "g_w_up", extra_bf16=True)
    pending = []

    def scatter_start(grp):
        sems, g_thru, lands, token = _scatter_start([gw[n][1] for n in grp], [big[n][1] for n in grp],
                                                    "scatter_start_" + grp[0])
        pending.append((grp, sems, g_thru, lands))
        return token[0:1, 0:1]

    tok1 = scatter_start(["w_down", "w_gate", "w_up"])
    dx2, dx2b, g_n3 = _rmsnorm_bwd(dh3, x2, norm3_w + tok1, dx3, "norm3_bwd")
    datt = _matmul(dx2b, wf["wo"], "nt", "d_att", out_dtype=BF16)
    gw["wo"] = _matmul(att, dx2b, "tn", "g_wo", extra_bf16=True)
    dqa, dka, dva = _attn_bwd(qa, ka, va, datt, "attn_bwd")
    dh2 = _matmul(dqa, wf["wq"], "nt", "d_h2", out_dtype=BF16)
    gw["wq"] = _matmul(h2, dqa, "tn", "g_wq", extra_bf16=True)
    gw["wk"] = _matmul(mem_n, dka, "tn", "g_wk", extra_bf16=True)
    gw["wv"] = _matmul(mem_n, dva, "tn", "g_wv", extra_bf16=True)
    tok2 = scatter_start(["wo", "wq", "wk", "wv"])
    dmem_n = _matmul(dka, wf["wk"], "nt", "d_memn_k")
    dmem_n = _matmul(dva, wf["wv"], "nt", "d_memn_v", residual=dmem_n)
    _, _, g_nm = _rmsnorm_bwd(dmem_n, mems, mem_norm_w, None, "norm_mem_bwd")
    dx1, dx1b, g_n2 = _rmsnorm_bwd(dh2, x1, norm2_w + tok2, dx2, "norm2_bwd")
    dmix = _matmul(dx1b, wf["w_out"], "nt", "d_mix", out_dtype=BF16)
    gw["w_out"] = _matmul(mix, dx1b, "tn", "g_w_out", extra_bf16=True)
    tok3 = scatter_start(["w_out"])
    dproj, g_lb, g_hn = _hgrn_bwd(proj, lb0, lb1, hgrn_norm_w + tok3, o_h, states, dmix, "hgrn_bwd")
    dproj, g_sw = _sconv_bwd(proj, sconv8, dmix, dproj, "sconv_bwd")
    gw["w_in"] = _matmul(h1, dproj, "tn", "g_w_in", extra_bf16=True, groups=7)
    tok4 = scatter_start(["w_in"])
    dh1 = _matmul(dproj, wf["w_in"], "nt", "d_h1", out_dtype=BF16, groups=7, tn=2048)
    dx, _, g_n1 = _rmsnorm_bwd(dh1, xs, norm1_w + tok4, dx1, "norm1_bwd")

    small = [g_n1, g_n2, g_n3, g_final, g_nm, g_lb, g_hn, g_fb,
             g_sw[0:8], g_sw[8:16], g_sw[16:24], g_fw[0:8], g_fw[8:16], g_fw[16:24], loss8]
    widths = [a.shape[1] for a in small]
    tot = _all_reduce_small(jnp.concatenate(small, axis=1), "all_reduce_small")
    offs = [0]
    for wd_ in widths:
        offs.append(offs[-1] + wd_)
    sm = [tot[:, offs[i]:offs[i + 1]] for i in range(len(small))]
    s_n1, s_n2, s_n3, s_final, s_nm, s_lb, s_hn, s_fb = sm[:8]
    s_sw = jnp.concatenate(sm[8:11], axis=0)
    s_fw = jnp.concatenate(sm[11:14], axis=0)
    loss = sm[14][0, 0]
    slot = 2 * lax.axis_index("x") + lax.axis_index("y")
    s_sw = lax.dynamic_slice_in_dim(s_sw, slot * (HGRN_W // N_CHIPS), HGRN_W // N_CHIPS, axis=1)
    fsh = ffn_conv_w.shape[2]
    s_fw = lax.dynamic_slice_in_dim(s_fw, slot * fsh, fsh, axis=1)
    s_lb2 = jnp.concatenate([s_lb, -s_lb], axis=0)

    swaps = []
    after = tot
    for grp, sems, g_thru, lands in pending:
        got = _scatter_wait(g_thru, lands, [big[n][1] for n in grp], sems, after, "scatter_wait_" + grp[0])
        sums = [_sum4(gw[n][0], big[n][1], slot_arr, r, "core_sum_" + n) for n, r in zip(grp, got)]
        ssems, s_thru, s_lands, after = _sibling_start(sums, "sibling_start_" + grp[0])
        swaps.append((grp, ssems, s_thru, s_lands))

    moments = {"hgrn_lb": (m_hgrn_lb, v_hgrn_lb), "norm1_w": (m_norm1_w, v_norm1_w), "w_in": (m_w_in, v_w_in),
               "hgrn_norm_w": (m_hgrn_norm_w, v_hgrn_norm_w), "sconv_w": (m_sconv_w, v_sconv_w),
               "w_out": (m_w_out, v_w_out), "norm2_w": (m_norm2_w, v_norm2_w),
               "mem_norm_w": (m_mem_norm_w, v_mem_norm_w), "wq": (m_wq, v_wq), "wk": (m_wk, v_wk), "wv": (m_wv, v_wv),
               "wo": (m_wo, v_wo), "norm3_w": (m_norm3_w, v_norm3_w), "w_gate": (m_w_gate, v_w_gate),
               "w_up": (m_w_up, v_w_up), "ffn_conv_w": (m_ffn_conv_w, v_ffn_conv_w),
               "ffn_conv_b": (m_ffn_conv_b, v_ffn_conv_b), "w_down": (m_w_down, v_w_down),
               "final_norm_w": (m_final_norm_w, v_final_norm_w)}
    weights = {"hgrn_lb": hgrn_lb, "norm1_w": norm1_w, "w_in": w_in, "hgrn_norm_w": hgrn_norm_w, "sconv_w": sconv_w,
               "w_out": w_out, "norm2_w": norm2_w, "mem_norm_w": mem_norm_w, "wq": wq, "wk": wk, "wv": wv, "wo": wo,
               "norm3_w": norm3_w, "w_gate": w_gate, "w_up": w_up, "ffn_conv_w": ffn_conv_w, "ffn_conv_b": ffn_conv_b,
               "w_down": w_down, "final_norm_w": final_norm_w}
    small_g = {"hgrn_lb": s_lb2, "norm1_w": s_n1, "hgrn_norm_w": s_hn, "sconv_w": s_sw, "norm2_w": s_n2,
               "mem_norm_w": s_nm, "norm3_w": s_n3, "ffn_conv_w": s_fw, "ffn_conv_b": s_fb, "final_norm_w": s_final}
    order = list(weights)
    res = {}

    def adamw(n, parts):
        shape = weights[n].shape
        w2 = weights[n].reshape((-1, shape[-1]))
        m2, v2 = (t.reshape(w2.shape) for t in moments[n])
        res[n] = [t.reshape(shape) for t in _adamw(w2, [p.reshape(w2.shape) for p in parts], m2, v2, "adamw_" + n)]

    for n in order:
        if n not in big:
            adamw(n, [small_g[n]])
    after = after + res["final_norm_w"][1][0]
    for grp, ssems, s_thru, s_lands in swaps:
        own, other = _sibling_wait(s_thru, s_lands, ssems, after, "sibling_wait_" + grp[0])
        for n, a, b in zip(grp, own, other):
            adamw(n, [a, b])
        after = res[grp[-1]][1]

    return (loss, dx[None], *[res[n][0] for n in order], *[res[n][1] for n in order],
            *[res[n][2] for n in order], *[res[n][3] for n in order])
```
